```python
import math
import jax, jax.numpy as jnp
from jax import lax
import numpy as np

D_MODEL = 1024
BATCH = 8
SEQ = 8192
DEPTH = 1

LRU_WIDTH = D_MODEL
LRU_BLOCKS = 16
LRU_BLOCK_W = LRU_WIDTH // LRU_BLOCKS
LRU_C = 8.0
CONV_K = 4
SSD_HEAD_DIM = 64
SSD_INNER = D_MODEL
SSD_HEADS = SSD_INNER // SSD_HEAD_DIM
SSD_GROUPS = 2
SSD_HPG = SSD_HEADS // SSD_GROUPS
SSD_STATE = 128
SSD_CHUNK = 128
MIX_WIDTH = LRU_WIDTH + SSD_INNER
SPLITS = (LRU_WIDTH, LRU_WIDTH, SSD_INNER, SSD_INNER,
          SSD_GROUPS * SSD_STATE, SSD_GROUPS * SSD_STATE, SSD_HEADS)
IN_COLS = sum(SPLITS)
SSD_CONV_CH = SSD_INNER + 2 * SSD_GROUPS * SSD_STATE
D_FF = -(-8 * D_MODEL // (3 * 256)) * 256
EPS = 1e-6

kernel_name = "hymba_style_rglru_ssd_hybrid"


def rms_norm(x, w):
    xf = x.astype(jnp.float32)
    y = xf * lax.rsqrt(jnp.mean(xf * xf, axis=-1, keepdims=True) + EPS)
    return (y * w.astype(jnp.float32)).astype(x.dtype)


def causal_depthwise_conv(x, w, b):
    K = w.shape[0]
    T = x.shape[1]
    xp = jnp.pad(x, ((0, 0), (K - 1, 0), (0, 0)))
    y = b + xp[:, 0:T] * w[0]
    for k in range(1, K):
        y = y + xp[:, k:k + T] * w[k]
    return y


def rg_lru(x, w_a, b_a, w_x, b_x, lam):
    bsz, T, W = x.shape
    xf = x.astype(jnp.float32)
    xb = xf.reshape(bsz, T, LRU_BLOCKS, LRU_BLOCK_W)
    r = jax.nn.sigmoid(jnp.einsum("btki,kij->btkj", xb, w_a.astype(jnp.float32)).reshape(bsz, T, W) + b_a.astype(jnp.float32))
    i = jax.nn.sigmoid(jnp.einsum("btki,kij->btkj", xb, w_x.astype(jnp.float32)).reshape(bsz, T, W) + b_x.astype(jnp.float32))
    log_a = -LRU_C * r * jax.nn.softplus(-lam.astype(jnp.float32))
    a = jnp.exp(log_a)
    u = jnp.sqrt(-jnp.expm1(2.0 * log_a)) * (i * xf)

    def combine(left, right):
        a1, b1 = left
        a2, b2 = right
        return a1 * a2, a2 * b1 + b2

    _, h = lax.associative_scan(combine, (a, u), axis=1)
    return h.astype(x.dtype)


def segsum(a):
    L = a.shape[-1]
    cs = jnp.cumsum(a, axis=-1)
    diff = cs[..., :, None] - cs[..., None, :]
    mask = jnp.tril(jnp.ones((L, L), dtype=bool))
    return jnp.where(mask, diff, -jnp.inf)


def ssd_chunked(xs, a, Bm, Cm):
    b, t, g, e, p = xs.shape
    n = Bm.shape[-1]
    c = t // SSD_CHUNK
    xs = xs.reshape(b, c, SSD_CHUNK, g, e, p)
    Bm = Bm.reshape(b, c, SSD_CHUNK, g, n)
    Cm = Cm.reshape(b, c, SSD_CHUNK, g, n)
    a = a.reshape(b, c, SSD_CHUNK, g, e).transpose(0, 3, 4, 1, 2)
    a_cs = jnp.cumsum(a, axis=-1)
    Lmat = jnp.exp(segsum(a))
    scores = jnp.einsum("bclgn,bcsgn->bgcls", Cm, Bm)
    M = scores[:, :, None] * Lmat
    y_diag = jnp.einsum("bgecls,bcsgep->bclgep", M, xs)
    decay_states = jnp.exp(a_cs[..., -1:] - a_cs)
    states = jnp.einsum("bclgn,bgecl,bclgep->bcgepn", Bm, decay_states, xs)
    chunk_a = jnp.pad(a_cs[..., -1], ((0, 0), (0, 0), (0, 0), (1, 0)))
    decay_chunk = jnp.exp(segsum(chunk_a))
    states = jnp.concatenate([jnp.zeros_like(states[:, :1]), states], axis=1)
    prev_states = jnp.einsum("bgezc,bcgepn->bzgepn", decay_chunk, states)[:, :-1]
    y_off = jnp.einsum("bclgn,bcgepn,bgecl->bclgep", Cm, prev_states, jnp.exp(a_cs))
    return (y_diag + y_off).reshape(b, t, g, e, p)


def _fwd_setup_inputs(seed: int = 0) -> dict:
    key = jax.random.key(seed)
    ks = jax.random.split(key, 24)
    f32 = jnp.float32
    L = DEPTH

    def nrm(k, shape, scale):
        return jax.random.normal(k, shape, f32) * scale

    def gain(k, shape):
        return 1.0 + 0.05 * jax.random.normal(k, shape, f32)

    x = jax.random.normal(ks[0], (BATCH, SEQ, D_MODEL), f32)
    a_init = jax.random.uniform(ks[9], (L, LRU_WIDTH), f32, 0.9, 0.999)
    s = a_init ** (1.0 / LRU_C)
    lru_lambda = jnp.log(s) - jnp.log1p(-s)
    dt0 = jnp.exp(jax.random.uniform(ks[13], (L, SSD_HEADS), f32, math.log(1e-3), math.log(1e-1)))
    ssd_dt_bias = dt0 + jnp.log(-jnp.expm1(-dt0))
    ssd_a_log = jnp.log(jax.random.uniform(ks[14], (L, SSD_HEADS), f32, 1.0, 16.0))
    return {
        "x": x,
        "pre_mix_norm": gain(ks[1], (L, D_MODEL)),
        "w_in": nrm(ks[2], (L, D_MODEL, IN_COLS), D_MODEL ** -0.5),
        "lru_conv_w": nrm(ks[3], (L, CONV_K, LRU_WIDTH), CONV_K ** -0.5),
        "lru_conv_b": nrm(ks[4], (L, LRU_WIDTH), 0.01),
        "lru_wa": nrm(ks[5], (L, LRU_BLOCKS, LRU_BLOCK_W, LRU_BLOCK_W), LRU_BLOCK_W ** -0.5),
        "lru_ba": nrm(ks[6], (L, LRU_WIDTH), 0.01),
        "lru_wx": nrm(ks[7], (L, LRU_BLOCKS, LRU_BLOCK_W, LRU_BLOCK_W), LRU_BLOCK_W ** -0.5),
        "lru_bx": nrm(ks[8], (L, LRU_WIDTH), 0.01),
        "lru_lambda": lru_lambda,
        "lru_out_norm": gain(ks[10], (L, LRU_WIDTH)),
        "ssd_conv_w": nrm(ks[11], (L, CONV_K, SSD_CONV_CH), CONV_K ** -0.5),
        "ssd_conv_b": nrm(ks[12], (L, SSD_CONV_CH), 0.01),
        "ssd_dt_bias": ssd_dt_bias,
        "ssd_a_log": ssd_a_log,
        "ssd_d": gain(ks[15], (L, SSD_HEADS)),
        "ssd_out_norm": gain(ks[16], (L, SSD_INNER)),
        "w_out": nrm(ks[17], (L, MIX_WIDTH, D_MODEL), MIX_WIDTH ** -0.5),
        "post_mix_norm": gain(ks[18], (L, D_MODEL)),
        "pre_ffn_norm": gain(ks[19], (L, D_MODEL)),
        "w_gate": nrm(ks[20], (L, D_MODEL, D_FF), D_MODEL ** -0.5),
        "w_up": nrm(ks[21], (L, D_MODEL, D_FF), D_MODEL ** -0.5),
        "w_down": nrm(ks[22], (L, D_FF, D_MODEL), D_FF ** -0.5),
        "post_ffn_norm": gain(ks[23], (L, D_MODEL)),
    }


def _fwd_reference(x, pre_mix_norm, w_in, lru_conv_w, lru_conv_b, lru_wa, lru_ba, lru_wx, lru_bx,
              lru_lambda, lru_out_norm, ssd_conv_w, ssd_conv_b, ssd_dt_bias, ssd_a_log, ssd_d,
              ssd_out_norm, w_out, post_mix_norm, pre_ffn_norm, w_gate, w_up, w_down, post_ffn_norm):
    bsz, T, _ = x.shape
    offs = np.cumsum((0,) + SPLITS)
    for li in range(DEPTH):
        h = rms_norm(x, pre_mix_norm[li])
        proj = h @ w_in[li]
        lru_x = proj[..., offs[0]:offs[1]]
        lru_gate = proj[..., offs[1]:offs[2]]
        ssd_z = proj[..., offs[2]:offs[3]]
        ssd_xbc = proj[..., offs[3]:offs[6]]
        ssd_dt = proj[..., offs[6]:offs[7]]

        lx = causal_depthwise_conv(lru_x, lru_conv_w[li], lru_conv_b[li])
        lh = rg_lru(lx, lru_wa[li], lru_ba[li], lru_wx[li], lru_bx[li], lru_lambda[li])
        y_lru = rms_norm(lh * jax.nn.gelu(lru_gate), lru_out_norm[li])

        xbc = jax.nn.silu(causal_depthwise_conv(ssd_xbc, ssd_conv_w[li], ssd_conv_b[li]))
        sx = xbc[..., :SSD_INNER].astype(jnp.float32).reshape(bsz, T, SSD_GROUPS, SSD_HPG, SSD_HEAD_DIM)
        sB = xbc[..., SSD_INNER:SSD_INNER + SSD_GROUPS * SSD_STATE].astype(jnp.float32).reshape(bsz, T, SSD_GROUPS, SSD_STATE)
        sC = xbc[..., SSD_INNER + SSD_GROUPS * SSD_STATE:].astype(jnp.float32).reshape(bsz, T, SSD_GROUPS, SSD_STATE)
        dt = jax.nn.softplus(ssd_dt.astype(jnp.float32) + ssd_dt_bias[li].astype(jnp.float32))
        dt = dt.reshape(bsz, T, SSD_GROUPS, SSD_HPG)
        A = -jnp.exp(ssd_a_log[li].astype(jnp.float32)).reshape(SSD_GROUPS, SSD_HPG)
        y = ssd_chunked(sx * dt[..., None], dt * A, sB, sC)
        y = y + ssd_d[li].astype(jnp.float32).reshape(SSD_GROUPS, SSD_HPG)[..., None] * sx
        y = y.reshape(bsz, T, SSD_INNER).astype(x.dtype)
        y_ssd = rms_norm(y * jax.nn.silu(ssd_z), ssd_out_norm[li])

        mix = jnp.concatenate([y_lru, y_ssd], axis=-1) @ w_out[li]
        x = x + rms_norm(mix, post_mix_norm[li])

        h = rms_norm(x, pre_ffn_norm[li])
        f = (jax.nn.silu(h @ w_gate[li]) * (h @ w_up[li])) @ w_down[li]
        x = x + rms_norm(f, post_ffn_norm[li])
    return x


import jax as _jax
import jax.numpy as _jnp

TWIN_FORMAT = 'train_step'
FWD_PARAMS = ['x', 'pre_mix_norm', 'w_in', 'lru_conv_w', 'lru_conv_b', 'lru_wa', 'lru_ba', 'lru_wx', 'lru_bx', 'lru_lambda', 'lru_out_norm', 'ssd_conv_w', 'ssd_conv_b', 'ssd_dt_bias', 'ssd_a_log', 'ssd_d', 'ssd_out_norm', 'w_out', 'post_mix_norm', 'pre_ffn_norm', 'w_gate', 'w_up', 'w_down', 'post_ffn_norm']
TWIN_WEIGHTS = ['pre_mix_norm', 'w_in', 'lru_conv_w', 'lru_conv_b', 'lru_wa', 'lru_ba', 'lru_wx', 'lru_bx', 'lru_lambda', 'lru_out_norm', 'ssd_conv_w', 'ssd_conv_b', 'ssd_dt_bias', 'ssd_a_log', 'ssd_d', 'ssd_out_norm', 'w_out', 'post_mix_norm', 'pre_ffn_norm', 'w_gate', 'w_up', 'w_down', 'post_ffn_norm']
TWIN_DIFF_INPUT = 'x'
TWIN_INPUTS = ['x', 'pre_mix_norm', 'w_in', 'lru_conv_w', 'lru_conv_b', 'lru_wa', 'lru_ba', 'lru_wx', 'lru_bx', 'lru_lambda', 'lru_out_norm', 'ssd_conv_w', 'ssd_conv_b', 'ssd_dt_bias', 'ssd_a_log', 'ssd_d', 'ssd_out_norm', 'w_out', 'post_mix_norm', 'pre_ffn_norm', 'w_gate', 'w_up', 'w_down', 'post_ffn_norm', 'loss_target', 'm_pre_mix_norm', 'm_w_in', 'm_lru_conv_w', 'm_lru_conv_b', 'm_lru_wa', 'm_lru_ba', 'm_lru_wx', 'm_lru_bx', 'm_lru_lambda', 'm_lru_out_norm', 'm_ssd_conv_w', 'm_ssd_conv_b', 'm_ssd_dt_bias', 'm_ssd_a_log', 'm_ssd_d', 'm_ssd_out_norm', 'm_w_out', 'm_post_mix_norm', 'm_pre_ffn_norm', 'm_w_gate', 'm_w_up', 'm_w_down', 'm_post_ffn_norm', 'v_pre_mix_norm', 'v_w_in', 'v_lru_conv_w', 'v_lru_conv_b', 'v_lru_wa', 'v_lru_ba', 'v_lru_wx', 'v_lru_bx', 'v_lru_lambda', 'v_lru_out_norm', 'v_ssd_conv_w', 'v_ssd_conv_b', 'v_ssd_dt_bias', 'v_ssd_a_log', 'v_ssd_d', 'v_ssd_out_norm', 'v_w_out', 'v_post_mix_norm', 'v_pre_ffn_norm', 'v_w_gate', 'v_w_up', 'v_w_down', 'v_post_ffn_norm']
TWIN_OUTPUTS = ['loss', 'grad_x', 'grad_pre_mix_norm', 'grad_w_in', 'grad_lru_conv_w', 'grad_lru_conv_b', 'grad_lru_wa', 'grad_lru_ba', 'grad_lru_wx', 'grad_lru_bx', 'grad_lru_lambda', 'grad_lru_out_norm', 'grad_ssd_conv_w', 'grad_ssd_conv_b', 'grad_ssd_dt_bias', 'grad_ssd_a_log', 'grad_ssd_d', 'grad_ssd_out_norm', 'grad_w_out', 'grad_post_mix_norm', 'grad_pre_ffn_norm', 'grad_w_gate', 'grad_w_up', 'grad_w_down', 'grad_post_ffn_norm', 'delta_pre_mix_norm', 'delta_w_in', 'delta_lru_conv_w', 'delta_lru_conv_b', 'delta_lru_wa', 'delta_lru_ba', 'delta_lru_wx', 'delta_lru_bx', 'delta_lru_lambda', 'delta_lru_out_norm', 'delta_ssd_conv_w', 'delta_ssd_conv_b', 'delta_ssd_dt_bias', 'delta_ssd_a_log', 'delta_ssd_d', 'delta_ssd_out_norm', 'delta_w_out', 'delta_post_mix_norm', 'delta_pre_ffn_norm', 'delta_w_gate', 'delta_w_up', 'delta_w_down', 'delta_post_ffn_norm', 'new_m_pre_mix_norm', 'new_m_w_in', 'new_m_lru_conv_w', 'new_m_lru_conv_b', 'new_m_lru_wa', 'new_m_lru_ba', 'new_m_lru_wx', 'new_m_lru_bx', 'new_m_lru_lambda', 'new_m_lru_out_norm', 'new_m_ssd_conv_w', 'new_m_ssd_conv_b', 'new_m_ssd_dt_bias', 'new_m_ssd_a_log', 'new_m_ssd_d', 'new_m_ssd_out_norm', 'new_m_w_out', 'new_m_post_mix_norm', 'new_m_pre_ffn_norm', 'new_m_w_gate', 'new_m_w_up', 'new_m_w_down', 'new_m_post_ffn_norm', 'new_v_pre_mix_norm', 'new_v_w_in', 'new_v_lru_conv_w', 'new_v_lru_conv_b', 'new_v_lru_wa', 'new_v_lru_ba', 'new_v_lru_wx', 'new_v_lru_bx', 'new_v_lru_lambda', 'new_v_lru_out_norm', 'new_v_ssd_conv_w', 'new_v_ssd_conv_b', 'new_v_ssd_dt_bias', 'new_v_ssd_a_log', 'new_v_ssd_d', 'new_v_ssd_out_norm', 'new_v_w_out', 'new_v_post_mix_norm', 'new_v_pre_ffn_norm', 'new_v_w_gate', 'new_v_w_up', 'new_v_w_down', 'new_v_post_ffn_norm']
TWIN_LEAF_KINDS = {'loss': 'loss', 'grad_x': 'grad_x', 'grad_pre_mix_norm': 'grad_w', 'grad_w_in': 'grad_w', 'grad_lru_conv_w': 'grad_w', 'grad_lru_conv_b': 'grad_w', 'grad_lru_wa': 'grad_w', 'grad_lru_ba': 'grad_w', 'grad_lru_wx': 'grad_w', 'grad_lru_bx': 'grad_w', 'grad_lru_lambda': 'grad_w', 'grad_lru_out_norm': 'grad_w', 'grad_ssd_conv_w': 'grad_w', 'grad_ssd_conv_b': 'grad_w', 'grad_ssd_dt_bias': 'grad_w', 'grad_ssd_a_log': 'grad_w', 'grad_ssd_d': 'grad_w', 'grad_ssd_out_norm': 'grad_w', 'grad_w_out': 'grad_w', 'grad_post_mix_norm': 'grad_w', 'grad_pre_ffn_norm': 'grad_w', 'grad_w_gate': 'grad_w', 'grad_w_up': 'grad_w', 'grad_w_down': 'grad_w', 'grad_post_ffn_norm': 'grad_w', 'delta_pre_mix_norm': 'delta_w', 'delta_w_in': 'delta_w', 'delta_lru_conv_w': 'delta_w', 'delta_lru_conv_b': 'delta_w', 'delta_lru_wa': 'delta_w', 'delta_lru_ba': 'delta_w', 'delta_lru_wx': 'delta_w', 'delta_lru_bx': 'delta_w', 'delta_lru_lambda': 'delta_w', 'delta_lru_out_norm': 'delta_w', 'delta_ssd_conv_w': 'delta_w', 'delta_ssd_conv_b': 'delta_w', 'delta_ssd_dt_bias': 'delta_w', 'delta_ssd_a_log': 'delta_w', 'delta_ssd_d': 'delta_w', 'delta_ssd_out_norm': 'delta_w', 'delta_w_out': 'delta_w', 'delta_post_mix_norm': 'delta_w', 'delta_pre_ffn_norm': 'delta_w', 'delta_w_gate': 'delta_w', 'delta_w_up': 'delta_w', 'delta_w_down': 'delta_w', 'delta_post_ffn_norm': 'delta_w', 'new_m_pre_mix_norm': 'new_m', 'new_m_w_in': 'new_m', 'new_m_lru_conv_w': 'new_m', 'new_m_lru_conv_b': 'new_m', 'new_m_lru_wa': 'new_m', 'new_m_lru_ba': 'new_m', 'new_m_lru_wx': 'new_m', 'new_m_lru_bx': 'new_m', 'new_m_lru_lambda': 'new_m', 'new_m_lru_out_norm': 'new_m', 'new_m_ssd_conv_w': 'new_m', 'new_m_ssd_conv_b': 'new_m', 'new_m_ssd_dt_bias': 'new_m', 'new_m_ssd_a_log': 'new_m', 'new_m_ssd_d': 'new_m', 'new_m_ssd_out_norm': 'new_m', 'new_m_w_out': 'new_m', 'new_m_post_mix_norm': 'new_m', 'new_m_pre_ffn_norm': 'new_m', 'new_m_w_gate': 'new_m', 'new_m_w_up': 'new_m', 'new_m_w_down': 'new_m', 'new_m_post_ffn_norm': 'new_m', 'new_v_pre_mix_norm': 'new_v', 'new_v_w_in': 'new_v', 'new_v_lru_conv_w': 'new_v', 'new_v_lru_conv_b': 'new_v', 'new_v_lru_wa': 'new_v', 'new_v_lru_ba': 'new_v', 'new_v_lru_wx': 'new_v', 'new_v_lru_bx': 'new_v', 'new_v_lru_lambda': 'new_v', 'new_v_lru_out_norm': 'new_v', 'new_v_ssd_conv_w': 'new_v', 'new_v_ssd_conv_b': 'new_v', 'new_v_ssd_dt_bias': 'new_v', 'new_v_ssd_a_log': 'new_v', 'new_v_ssd_d': 'new_v', 'new_v_ssd_out_norm': 'new_v', 'new_v_w_out': 'new_v', 'new_v_post_mix_norm': 'new_v', 'new_v_pre_ffn_norm': 'new_v', 'new_v_w_gate': 'new_v', 'new_v_w_up': 'new_v', 'new_v_w_down': 'new_v', 'new_v_post_ffn_norm': 'new_v'}


def _forward(args):
    return _fwd_reference(*[args[k] for k in FWD_PARAMS])


def _output_shape():
    out = _jax.eval_shape(lambda: _forward(_fwd_setup_inputs(0)))
    return out.shape, out.dtype

N_MICROBATCH = 1
ADAM_LR = 0.001
ADAM_B1 = 0.9
ADAM_B2 = 0.999
ADAM_EPS = 1e-08
ADAM_WD = 0.01
ADAM_STEP = 10
PER_EXAMPLE_BATCH_AXIS = {'x': 0, 'loss_target': 0}
SHARED_INPUTS = []
_WEIGHT_DTYPES = {'pre_mix_norm': _jnp.float32, 'w_in': _jnp.float32, 'lru_conv_w': _jnp.float32, 'lru_conv_b': _jnp.float32, 'lru_wa': _jnp.float32, 'lru_ba': _jnp.float32, 'lru_wx': _jnp.float32, 'lru_bx': _jnp.float32, 'lru_lambda': _jnp.float32, 'lru_out_norm': _jnp.float32, 'ssd_conv_w': _jnp.float32, 'ssd_conv_b': _jnp.float32, 'ssd_dt_bias': _jnp.float32, 'ssd_a_log': _jnp.float32, 'ssd_d': _jnp.float32, 'ssd_out_norm': _jnp.float32, 'w_out': _jnp.float32, 'post_mix_norm': _jnp.float32, 'pre_ffn_norm': _jnp.float32, 'w_gate': _jnp.float32, 'w_up': _jnp.float32, 'w_down': _jnp.float32, 'post_ffn_norm': _jnp.float32}
MOMENT_SCALE = {'pre_mix_norm': 9.339614e-01, 'w_in': 4.586736e-01, 'lru_conv_w': 6.637143e-01, 'lru_conv_b': 1.545462e+01, 'lru_wa': 4.428030e-01, 'lru_ba': 3.198983e-01, 'lru_wx': 8.358611e-01, 'lru_bx': 1.870127e-01, 'lru_lambda': 4.228967e-01, 'lru_out_norm': 8.809542e-01, 'ssd_conv_w': 4.981860e-01, 'ssd_conv_b': 1.201663e+00, 'ssd_dt_bias': 1.196631e+00, 'ssd_a_log': 1.202812e+00, 'ssd_d': 2.671061e+00, 'ssd_out_norm': 7.843822e-01, 'w_out': 1.168503e+00, 'post_mix_norm': 6.400072e+01, 'pre_ffn_norm': 1.047587e+00, 'w_gate': 3.525331e-01, 'w_up': 5.672271e-01, 'w_down': 9.318513e-01, 'post_ffn_norm': 6.401656e+01}


def _to_microbatches(a, axis):
    t = _jnp.moveaxis(a, axis, 0)
    t = t.reshape((N_MICROBATCH, t.shape[0] // N_MICROBATCH) + t.shape[1:])
    return _jnp.moveaxis(t, 1, axis + 1)


def setup_inputs(seed: int = 0) -> dict:
    inp = _fwd_setup_inputs(seed)
    key = _jax.random.fold_in(_jax.random.key(seed), 7919)
    shape, _ = _output_shape()
    out = dict(inp)
    out["loss_target"] = _jax.random.normal(_jax.random.fold_in(key, 0), shape, _jnp.float32)
    for i, name in enumerate(TWIN_WEIGHTS):
        w = inp[name].astype(_jnp.float32)
        if MOMENT_SCALE is None:
            s = _jnp.sqrt(_jnp.mean(_jnp.square(w)) + 1e-30)
        else:
            s = MOMENT_SCALE[name]
        km, kv = _jax.random.split(_jax.random.fold_in(key, i + 1))
        out[name] = w
        out["m_" + name] = s * _jax.random.normal(km, w.shape, _jnp.float32)
        out["v_" + name] = (s * s) * _jax.random.uniform(kv, w.shape, _jnp.float32, 0.5, 1.5)
    if N_MICROBATCH > 1:
        for name, axis in PER_EXAMPLE_BATCH_AXIS.items():
            out[name] = _to_microbatches(out[name], axis)
    return {'x': out['x'], 'pre_mix_norm': out['pre_mix_norm'], 'w_in': out['w_in'], 'lru_conv_w': out['lru_conv_w'], 'lru_conv_b': out['lru_conv_b'], 'lru_wa': out['lru_wa'], 'lru_ba': out['lru_ba'], 'lru_wx': out['lru_wx'], 'lru_bx': out['lru_bx'], 'lru_lambda': out['lru_lambda'], 'lru_out_norm': out['lru_out_norm'], 'ssd_conv_w': out['ssd_conv_w'], 'ssd_conv_b': out['ssd_conv_b'], 'ssd_dt_bias': out['ssd_dt_bias'], 'ssd_a_log': out['ssd_a_log'], 'ssd_d': out['ssd_d'], 'ssd_out_norm': out['ssd_out_norm'], 'w_out': out['w_out'], 'post_mix_norm': out['post_mix_norm'], 'pre_ffn_norm': out['pre_ffn_norm'], 'w_gate': out['w_gate'], 'w_up': out['w_up'], 'w_down': out['w_down'], 'post_ffn_norm': out['post_ffn_norm'], 'loss_target': out['loss_target'], 'm_pre_mix_norm': out['m_pre_mix_norm'], 'm_w_in': out['m_w_in'], 'm_lru_conv_w': out['m_lru_conv_w'], 'm_lru_conv_b': out['m_lru_conv_b'], 'm_lru_wa': out['m_lru_wa'], 'm_lru_ba': out['m_lru_ba'], 'm_lru_wx': out['m_lru_wx'], 'm_lru_bx': out['m_lru_bx'], 'm_lru_lambda': out['m_lru_lambda'], 'm_lru_out_norm': out['m_lru_out_norm'], 'm_ssd_conv_w': out['m_ssd_conv_w'], 'm_ssd_conv_b': out['m_ssd_conv_b'], 'm_ssd_dt_bias': out['m_ssd_dt_bias'], 'm_ssd_a_log': out['m_ssd_a_log'], 'm_ssd_d': out['m_ssd_d'], 'm_ssd_out_norm': out['m_ssd_out_norm'], 'm_w_out': out['m_w_out'], 'm_post_mix_norm': out['m_post_mix_norm'], 'm_pre_ffn_norm': out['m_pre_ffn_norm'], 'm_w_gate': out['m_w_gate'], 'm_w_up': out['m_w_up'], 'm_w_down': out['m_w_down'], 'm_post_ffn_norm': out['m_post_ffn_norm'], 'v_pre_mix_norm': out['v_pre_mix_norm'], 'v_w_in': out['v_w_in'], 'v_lru_conv_w': out['v_lru_conv_w'], 'v_lru_conv_b': out['v_lru_conv_b'], 'v_lru_wa': out['v_lru_wa'], 'v_lru_ba': out['v_lru_ba'], 'v_lru_wx': out['v_lru_wx'], 'v_lru_bx': out['v_lru_bx'], 'v_lru_lambda': out['v_lru_lambda'], 'v_lru_out_norm': out['v_lru_out_norm'], 'v_ssd_conv_w': out['v_ssd_conv_w'], 'v_ssd_conv_b': out['v_ssd_conv_b'], 'v_ssd_dt_bias': out['v_ssd_dt_bias'], 'v_ssd_a_log': out['v_ssd_a_log'], 'v_ssd_d': out['v_ssd_d'], 'v_ssd_out_norm': out['v_ssd_out_norm'], 'v_w_out': out['v_w_out'], 'v_post_mix_norm': out['v_post_mix_norm'], 'v_pre_ffn_norm': out['v_pre_ffn_norm'], 'v_w_gate': out['v_w_gate'], 'v_w_up': out['v_w_up'], 'v_w_down': out['v_w_down'], 'v_post_ffn_norm': out['v_post_ffn_norm']}


def _loss(weights, diff, rest, loss_target):
    with _jax.named_scope("forward"):
        args = {**rest, TWIN_DIFF_INPUT: diff, **{k: w.astype(_WEIGHT_DTYPES[k]) for k, w in weights.items()}}
        y = _forward(args)
    with _jax.named_scope("loss_head"):
        err = _jnp.square(y.astype(_jnp.float32) - loss_target)
        return 0.5 * _jnp.sum(_jnp.mean(err, axis=-1)) if err.ndim else 0.5 * err


def _adamw(w, g, m, v):
    m = ADAM_B1 * m + (1.0 - ADAM_B1) * g
    v = ADAM_B2 * v + (1.0 - ADAM_B2) * _jnp.square(g)
    m_hat = m / (1.0 - ADAM_B1 ** ADAM_STEP)
    v_hat = v / (1.0 - ADAM_B2 ** ADAM_STEP)
    delta = -ADAM_LR * (m_hat / (_jnp.sqrt(v_hat) + ADAM_EPS) + ADAM_WD * w)
    return delta, m, v


def reference(x, pre_mix_norm, w_in, lru_conv_w, lru_conv_b, lru_wa, lru_ba, lru_wx, lru_bx, lru_lambda, lru_out_norm, ssd_conv_w, ssd_conv_b, ssd_dt_bias, ssd_a_log, ssd_d, ssd_out_norm, w_out, post_mix_norm, pre_ffn_norm, w_gate, w_up, w_down, post_ffn_norm, loss_target, m_pre_mix_norm, m_w_in, m_lru_conv_w, m_lru_conv_b, m_lru_wa, m_lru_ba, m_lru_wx, m_lru_bx, m_lru_lambda, m_lru_out_norm, m_ssd_conv_w, m_ssd_conv_b, m_ssd_dt_bias, m_ssd_a_log, m_ssd_d, m_ssd_out_norm, m_w_out, m_post_mix_norm, m_pre_ffn_norm, m_w_gate, m_w_up, m_w_down, m_post_ffn_norm, v_pre_mix_norm, v_w_in, v_lru_conv_w, v_lru_conv_b, v_lru_wa, v_lru_ba, v_lru_wx, v_lru_bx, v_lru_lambda, v_lru_out_norm, v_ssd_conv_w, v_ssd_conv_b, v_ssd_dt_bias, v_ssd_a_log, v_ssd_d, v_ssd_out_norm, v_w_out, v_post_mix_norm, v_pre_ffn_norm, v_w_gate, v_w_up, v_w_down, v_post_ffn_norm):
    given = dict(x=x, pre_mix_norm=pre_mix_norm, w_in=w_in, lru_conv_w=lru_conv_w, lru_conv_b=lru_conv_b, lru_wa=lru_wa, lru_ba=lru_ba, lru_wx=lru_wx, lru_bx=lru_bx, lru_lambda=lru_lambda, lru_out_norm=lru_out_norm, ssd_conv_w=ssd_conv_w, ssd_conv_b=ssd_conv_b, ssd_dt_bias=ssd_dt_bias, ssd_a_log=ssd_a_log, ssd_d=ssd_d, ssd_out_norm=ssd_out_norm, w_out=w_out, post_mix_norm=post_mix_norm, pre_ffn_norm=pre_ffn_norm, w_gate=w_gate, w_up=w_up, w_down=w_down, post_ffn_norm=post_ffn_norm, loss_target=loss_target, m_pre_mix_norm=m_pre_mix_norm, m_w_in=m_w_in, m_lru_conv_w=m_lru_conv_w, m_lru_conv_b=m_lru_conv_b, m_lru_wa=m_lru_wa, m_lru_ba=m_lru_ba, m_lru_wx=m_lru_wx, m_lru_bx=m_lru_bx, m_lru_lambda=m_lru_lambda, m_lru_out_norm=m_lru_out_norm, m_ssd_conv_w=m_ssd_conv_w, m_ssd_conv_b=m_ssd_conv_b, m_ssd_dt_bias=m_ssd_dt_bias, m_ssd_a_log=m_ssd_a_log, m_ssd_d=m_ssd_d, m_ssd_out_norm=m_ssd_out_norm, m_w_out=m_w_out, m_post_mix_norm=m_post_mix_norm, m_pre_ffn_norm=m_pre_ffn_norm, m_w_gate=m_w_gate, m_w_up=m_w_up, m_w_down=m_w_down, m_post_ffn_norm=m_post_ffn_norm, v_pre_mix_norm=v_pre_mix_norm, v_w_in=v_w_in, v_lru_conv_w=v_lru_conv_w, v_lru_conv_b=v_lru_conv_b, v_lru_wa=v_lru_wa, v_lru_ba=v_lru_ba, v_lru_wx=v_lru_wx, v_lru_bx=v_lru_bx, v_lru_lambda=v_lru_lambda, v_lru_out_norm=v_lru_out_norm, v_ssd_conv_w=v_ssd_conv_w, v_ssd_conv_b=v_ssd_conv_b, v_ssd_dt_bias=v_ssd_dt_bias, v_ssd_a_log=v_ssd_a_log, v_ssd_d=v_ssd_d, v_ssd_out_norm=v_ssd_out_norm, v_w_out=v_w_out, v_post_mix_norm=v_post_mix_norm, v_pre_ffn_norm=v_pre_ffn_norm, v_w_gate=v_w_gate, v_w_up=v_w_up, v_w_down=v_w_down, v_post_ffn_norm=v_post_ffn_norm)
    weights = {n: given[n] for n in TWIN_WEIGHTS}
    shared = {n: given[n] for n in SHARED_INPUTS}
    per_example = {n: given[n] for n in ['x']}
    grad_fn = _jax.value_and_grad(_loss, argnums=(0, 1))

    def one_microbatch(ex, loss_target):
        ex = dict(ex)
        diff = ex.pop(TWIN_DIFF_INPUT)
        return grad_fn(weights, diff, {**shared, **ex}, loss_target)

    if N_MICROBATCH == 1:
        loss, (grad_w, grad_x) = one_microbatch(per_example, given["loss_target"])
    else:
        def body(carry, xs):
            loss_sum, grad_sum = carry
            l_k, (gw_k, gx_k) = one_microbatch(xs[0], xs[1])
            with _jax.named_scope("update"):
                return (loss_sum + l_k, _jax.tree.map(_jnp.add, grad_sum, gw_k)), gx_k

        init = (_jnp.zeros((), _jnp.float32), _jax.tree.map(_jnp.zeros_like, weights))
        (loss, grad_w), grad_x = _jax.lax.scan(body, init, (per_example, given["loss_target"]))
    with _jax.named_scope("update"):
        delta_w, new_m, new_v = {}, {}, {}
        for n in TWIN_WEIGHTS:
            delta_w[n], new_m[n], new_v[n] = _adamw(weights[n], grad_w[n], given["m_" + n], given["v_" + n])
    return (loss, grad_x, *[grad_w[n] for n in TWIN_WEIGHTS], *[delta_w[n] for n in TWIN_WEIGHTS],
            *[new_m[n] for n in TWIN_WEIGHTS], *[new_v[n] for n in TWIN_WEIGHTS])
```

```python
import functools

import jax
import jax.numpy as jnp
from jax import lax
from jax.experimental import pallas as pl
from jax.experimental.pallas import tpu as pltpu

F32 = jnp.float32
BF = jnp.bfloat16

D = 1024
LW = 1024
NBLK = 16
BW = 64
SI = 1024
NH = 16
HD = 64
NG = 2
HPG = NH // NG
NS = 128
CH = 128
XBC = SI + 2 * NG * NS
DTP = 128
PC = 3 * 1024 + XBC + DTP
DFF = 2816
IN_COLS = 4624
EPS = 1e-6
LRU_C = 8.0
CONV_K = 4
TT = 256
TK = 512
VMEM_LIMIT = 56 * 1024 * 1024

ADAM_LR, ADAM_B1, ADAM_B2, ADAM_EPS, ADAM_WD, ADAM_STEP = 0.001, 0.9, 0.999, 1e-08, 0.01, 10

MESH = pl.DeviceIdType.MESH


def _mm(a, b):
    return jnp.dot(a.astype(BF), b.astype(BF), preferred_element_type=F32)


def _mm_nt(a, b):
    return lax.dot_general(a.astype(BF), b.astype(BF), (((1,), (1,)), ((), ())), preferred_element_type=F32)


def _mm_tn(a, b):
    return lax.dot_general(a.astype(BF), b.astype(BF), (((0,), (0,)), ((), ())), preferred_element_type=F32)


def _sigmoid(x):
    return jax.nn.sigmoid(x)


def _softplus(x):
    return jnp.maximum(x, 0.0) + jnp.log1p(jnp.exp(-jnp.abs(x)))


_GELU_C = 0.7978845608028654
_GELU_K = 0.044715


def _gelu(x):
    t = jnp.tanh(_GELU_C * (x + _GELU_K * x * x * x))
    return 0.5 * x * (1.0 + t)


def _gelu_grad(x):
    t = jnp.tanh(_GELU_C * (x + _GELU_K * x * x * x))
    return 0.5 * (1.0 + t) + 0.5 * x * (1.0 - t * t) * _GELU_C * (1.0 + 3.0 * _GELU_K * x * x)


def _rms_fwd(x, g):
    r = lax.rsqrt(jnp.mean(x * x, axis=-1, keepdims=True) + EPS)
    return x * r * g


def _rms_bwd(x, g, dy):
    r = lax.rsqrt(jnp.mean(x * x, axis=-1, keepdims=True) + EPS)
    xh = x * r
    dxh = dy * g
    dg = jnp.sum(dy * xh, axis=0, keepdims=True)
    dx = r * (dxh - xh * jnp.mean(dxh * xh, axis=-1, keepdims=True))
    return dx, dg


def _sum_all(x):
    return jnp.sum(jnp.sum(x, axis=1, keepdims=True), axis=0, keepdims=True)


def _cumsum_rows(x, n):
    row = lax.broadcasted_iota(jnp.int32, x.shape, 0)
    k = 1
    while k < n:
        x = x + jnp.where(row >= k, pltpu.roll(x, k, 0), 0.0)
        k *= 2
    return x


def _rev_cumsum_rows(x, n):
    row = lax.broadcasted_iota(jnp.int32, x.shape, 0)
    k = 1
    while k < n:
        x = x + jnp.where(row < n - k, pltpu.roll(x, n - k, 0), 0.0)
        k *= 2
    return x


def _load_once(pairs, sem):
    @pl.when(pl.program_id(0) == 0)
    def _():
        for k, (src, dst) in enumerate(pairs):
            pltpu.make_async_copy(src, dst, sem.at[k]).start()
        for k, (src, dst) in enumerate(pairs):
            pltpu.make_async_copy(src, dst, sem.at[k]).wait()


def _params(n_axes=1):
    return pltpu.CompilerParams(dimension_semantics=("arbitrary",) * n_axes, vmem_limit_bytes=VMEM_LIMIT)


def _rows(n, width, rev_of=None):
    if rev_of is None:
        return pl.BlockSpec((n, width), lambda i: (i, 0))
    return pl.BlockSpec((n, width), lambda i: (rev_of - 1 - i, 0))


def _whole(shape):
    nd = len(shape)
    return pl.BlockSpec(shape, lambda i: (0,) * nd)


ANY = pl.BlockSpec(memory_space=pl.ANY)
S = jax.ShapeDtypeStruct


def _inproj(x, g0, wcat):
    T = x.shape[0]

    def body(x_ref, g_ref, w_hbm, h0_ref, lx_ref, lg_ref, z_ref, xbc_ref, dt_ref, w_vm, sem):
        _load_once([(w_hbm, w_vm)], sem)
        h = _rms_fwd(x_ref[...], g_ref[...]).astype(BF)
        h0_ref[...] = h
        lx_ref[...] = jnp.dot(h, w_vm[:, 0:1024], preferred_element_type=F32)
        lg_ref[...] = jnp.dot(h, w_vm[:, 1024:2048], preferred_element_type=F32)
        z_ref[...] = jnp.dot(h, w_vm[:, 2048:3072], preferred_element_type=F32)
        xbc_ref[...] = jnp.dot(h, w_vm[:, 3072:3072 + XBC], preferred_element_type=F32)
        dt_ref[...] = jnp.dot(h, w_vm[:, 3072 + XBC:PC], preferred_element_type=F32)

    return pl.pallas_call(
        body, name="inproj", grid=(T // TT,),
        in_specs=[_rows(TT, D), _whole((1, D)), ANY],
        out_specs=[_rows(TT, D), _rows(TT, 1024), _rows(TT, 1024), _rows(TT, 1024), _rows(TT, XBC), _rows(TT, DTP)],
        out_shape=[S((T, D), BF), S((T, 1024), F32), S((T, 1024), F32), S((T, 1024), F32), S((T, XBC), F32), S((T, DTP), F32)],
        scratch_shapes=[pltpu.VMEM((D, PC), BF), pltpu.SemaphoreType.DMA((1,))],
        compiler_params=_params(),
    )(x, g0, wcat)


def _blockdiag_mm(v, w4_ref):
    return jnp.concatenate([_mm(v[:, 256 * j:256 * (j + 1)], w4_ref[j]) for j in range(4)], axis=1)


def _lru_gates(lx, p_ref, wa_ref, wx_ref):
    r = _sigmoid(_blockdiag_mm(lx, wa_ref) + p_ref[5:6, :])
    i = _sigmoid(_blockdiag_mm(lx, wx_ref) + p_ref[6:7, :])
    sp = _softplus(-p_ref[7:8, :])
    la = -LRU_C * r * sp
    a = jnp.exp(la)
    th = jnp.tanh(la)
    mult = jnp.sqrt(-2.0 * th / (1.0 - th))
    return r, i, sp, a, mult


def _conv_from(xp_ref, p_ref, n):
    acc = p_ref[4:5, :] + p_ref[0:1, :] * xp_ref[pl.ds(8 - CONV_K + 1, n), :]
    for k in range(1, CONV_K):
        acc = acc + p_ref[k:k + 1, :] * xp_ref[pl.ds(8 - CONV_K + 1 + k, n), :]
    return acc


def _lru_fwd(lxr, lg, p_lru, wa4, wx4):
    T = lxr.shape[0]

    def body(lx_ref, lg_ref, p_ref, wa_ref, wx_ref, h_ref, y_ref, xp, a_s, u_s, hc):
        @pl.when(pl.program_id(0) == 0)
        def _():
            xp[0:8, :] = jnp.zeros((8, LW), F32)
            hc[...] = jnp.zeros_like(hc)

        xp[8:8 + TT, :] = lx_ref[...]
        lx = _conv_from(xp, p_ref, TT)
        xp[0:8, :] = xp[TT:TT + 8, :]
        r, i, sp, a, mult = _lru_gates(lx, p_ref, wa_ref, wx_ref)
        a_s[...] = a
        u_s[...] = mult * (i * lx)

        def step(t, h):
            h = a_s[pl.ds(t, 1), :] * h + u_s[pl.ds(t, 1), :]
            h_ref[pl.ds(t, 1), :] = h
            return h

        hc[0:1, :] = lax.fori_loop(0, TT, step, hc[0:1, :], unroll=8)
        gated = h_ref[...] * _gelu(lg_ref[...])
        y_ref[...] = _rms_fwd(gated, p_ref[8:9, :]).astype(BF)

    return pl.pallas_call(
        body, name="lru_fwd", grid=(T // TT,),
        in_specs=[_rows(TT, LW), _rows(TT, LW), _whole((16, LW)), _whole((4, 256, 256)), _whole((4, 256, 256))],
        out_specs=[_rows(TT, LW), _rows(TT, LW)],
        out_shape=[S((T, LW), F32), S((T, LW), BF)],
        scratch_shapes=[pltpu.VMEM((TT + 8, LW), F32), pltpu.VMEM((TT, LW), F32), pltpu.VMEM((TT, LW), F32),
                        pltpu.VMEM((8, LW), F32)],
        compiler_params=_params(),
    )(lxr, lg, p_lru, wa4, wx4)


def _ssd_prep(xp, xr_ref, dt_ref, cw_ref, hp_ref):
    xp[8:8 + CH, :] = xr_ref[...]
    cv = _conv_from(xp, cw_ref, CH)
    sg = _sigmoid(cv)
    xbc = cv * sg
    lane = lax.broadcasted_iota(jnp.int32, (CH, DTP), 1)
    raw = dt_ref[...] + hp_ref[0:1, :]
    dtv = jnp.where(lane < NH, _softplus(raw), 0.0)
    A = jnp.where(lane[0:1, :] < NH, -jnp.exp(hp_ref[1:2, :]), 0.0)
    cs = _cumsum_rows(dtv * A, CH)
    return cv, sg, xbc, raw, dtv, A, cs


def _ssd_fwd(xbcr, z, dtr, cw_ssd, hp_ssd, g_ssd):
    T = xbcr.shape[0]
    NC = T // CH

    def body(xr_ref, z_ref, dt_ref, cw_ref, hp_ref, g_ref, y_ref, yn_ref, st_ref, xp, st):
        @pl.when(pl.program_id(0) == 0)
        def _():
            xp[0:8, :] = jnp.zeros((8, XBC), F32)
            st[...] = jnp.zeros_like(st)

        cv, sg, xbc, raw, dtv, A, cs = _ssd_prep(xp, xr_ref, dt_ref, cw_ref, hp_ref)
        xp[0:8, :] = xp[CH:CH + 8, :]
        st_ref[0] = st[...]
        csT = cs.T
        E = jnp.exp(cs)
        cl = cs[CH - 1:CH, :]
        dsm = jnp.exp(cl - cs)
        El = jnp.exp(cl)
        tril = lax.broadcasted_iota(jnp.int32, (CH, CH), 0) >= lax.broadcasted_iota(jnp.int32, (CH, CH), 1)
        for g in range(NG):
            Bg = xbc[:, SI + NS * g:SI + NS * (g + 1)]
            Cg = xbc[:, SI + NG * NS + NS * g:SI + NG * NS + NS * (g + 1)]
            G = _mm_nt(Cg, Bg)
            for hh in range(HPG):
                h = g * HPG + hh
                Lm = jnp.exp(jnp.where(tril, cs[:, h:h + 1] - csT[h:h + 1, :], -1e30))
                Xh = xbc[:, HD * h:HD * (h + 1)]
                xs = Xh * dtv[:, h:h + 1]
                Sp = st[HD * h:HD * (h + 1), :]
                Y = _mm(G * Lm, xs) + _mm_nt(Cg, Sp) * E[:, h:h + 1] + hp_ref[2:3, h:h + 1] * Xh
                y_ref[:, HD * h:HD * (h + 1)] = Y
                st[HD * h:HD * (h + 1), :] = El[:, h:h + 1] * Sp + _mm_tn(xs * dsm[:, h:h + 1], Bg)
        zz = z_ref[...]
        gated = y_ref[...] * (zz * _sigmoid(zz))
        yn_ref[...] = _rms_fwd(gated, g_ref[...]).astype(BF)

    return pl.pallas_call(
        body, name="ssd_fwd", grid=(NC,),
        in_specs=[_rows(CH, XBC), _rows(CH, SI), _rows(CH, DTP), _whole((8, XBC)), _whole((8, DTP)), _whole((1, SI))],
        out_specs=[_rows(CH, SI), _rows(CH, SI), pl.BlockSpec((1, NH * HD, NS), lambda i: (i, 0, 0))],
        out_shape=[S((T, SI), F32), S((T, SI), BF), S((NC, NH * HD, NS), F32)],
        scratch_shapes=[pltpu.VMEM((CH + 8, XBC), F32), pltpu.VMEM((NH * HD, NS), F32)],
        compiler_params=_params(),
    )(xbcr, z, dtr, cw_ssd, hp_ssd, g_ssd)


def _outproj(ylru, yssd, x, wout, g_pm, g_pf):
    T = x.shape[0]

    def body(yl_ref, ys_ref, x_ref, w_hbm, gpm_ref, gpf_ref, mix_ref, x1_ref, h2_ref, w_vm, sem):
        _load_once([(w_hbm, w_vm)], sem)
        mix = (jnp.dot(yl_ref[...], w_vm[0:LW, :], preferred_element_type=F32)
               + jnp.dot(ys_ref[...], w_vm[LW:LW + SI, :], preferred_element_type=F32))
        mix_ref[...] = mix
        x1 = x_ref[...] + _rms_fwd(mix, gpm_ref[...])
        x1_ref[...] = x1
        h2_ref[...] = _rms_fwd(x1, gpf_ref[...]).astype(BF)

    return pl.pallas_call(
        body, name="outproj", grid=(T // TT,),
        in_specs=[_rows(TT, LW), _rows(TT, SI), _rows(TT, D), ANY, _whole((1, D)), _whole((1, D))],
        out_specs=[_rows(TT, D), _rows(TT, D), _rows(TT, D)],
        out_shape=[S((T, D), F32), S((T, D), F32), S((T, D), BF)],
        scratch_shapes=[pltpu.VMEM((LW + SI, D), BF), pltpu.SemaphoreType.DMA((1,))],
        compiler_params=_params(),
    )(ylru, yssd, x, wout, g_pm, g_pf)


def _ffn_fwd(h2, x1, tgt, wg, wu, wd, g_pff):
    T = x1.shape[0]

    def body(h2_ref, x1_ref, t_ref, wg_hbm, wu_hbm, wd_hbm, g_ref,
             gate_ref, up_ref, act_ref, df_ref, dx2_ref, st_ref, wg_vm, wu_vm, wd_vm, sem):
        _load_once([(wg_hbm, wg_vm), (wu_hbm, wu_vm), (wd_hbm, wd_vm)], sem)

        @pl.when(pl.program_id(0) == 0)
        def _():
            st_ref[...] = jnp.zeros_like(st_ref)

        h2 = h2_ref[...]
        gate = jnp.dot(h2, wg_vm[...], preferred_element_type=F32)
        up = jnp.dot(h2, wu_vm[...], preferred_element_type=F32)
        gate_ref[...] = gate
        up_ref[...] = up
        act = (gate * _sigmoid(gate) * up).astype(BF)
        act_ref[...] = act
        f = jnp.dot(act, wd_vm[...], preferred_element_type=F32)
        g = g_ref[...]
        x2 = x1_ref[...] + _rms_fwd(f, g)
        err = x2 - t_ref[...]
        st_ref[0:1, :] += 0.5 * jnp.sum(err * err, axis=0, keepdims=True) * (1.0 / D)
        dx2 = err * (1.0 / D)
        dx2_ref[...] = dx2
        df, dg = _rms_bwd(f, g, dx2)
        df_ref[...] = df.astype(BF)
        st_ref[1:2, :] += dg

    return pl.pallas_call(
        body, name="ffn_fwd", grid=(T // TT,),
        in_specs=[_rows(TT, D), _rows(TT, D), _rows(TT, D), ANY, ANY, ANY, _whole((1, D))],
        out_specs=[_rows(TT, DFF), _rows(TT, DFF), _rows(TT, DFF), _rows(TT, D), _rows(TT, D), _whole((8, D))],
        out_shape=[S((T, DFF), F32), S((T, DFF), F32), S((T, DFF), BF), S((T, D), BF), S((T, D), F32), S((8, D), F32)],
        scratch_shapes=[pltpu.VMEM((D, DFF), BF), pltpu.VMEM((D, DFF), BF), pltpu.VMEM((DFF, D), BF),
                        pltpu.SemaphoreType.DMA((3,))],
        compiler_params=_params(),
    )(h2, x1, tgt, wg, wu, wd, g_pff)


def _ffn_bwd(df, gate, up, wdT, wgT, wuT):
    T = df.shape[0]

    def body(df_ref, gate_ref, up_ref, wd_hbm, wg_hbm, wu_hbm, dgate_ref, dup_ref, dh2_ref, wd_vm, wg_vm, wu_vm, sem):
        _load_once([(wd_hbm, wd_vm), (wg_hbm, wg_vm), (wu_hbm, wu_vm)], sem)
        dact = jnp.dot(df_ref[...], wd_vm[...], preferred_element_type=F32)
        gate = gate_ref[...]
        s = _sigmoid(gate)
        dup = (dact * (gate * s)).astype(BF)
        dgate = (dact * up_ref[...] * (s + gate * s * (1.0 - s))).astype(BF)
        dup_ref[...] = dup
        dgate_ref[...] = dgate
        dh2_ref[...] = (jnp.dot(dgate, wg_vm[...], preferred_element_type=F32)
                        + jnp.dot(dup, wu_vm[...], preferred_element_type=F32))

    return pl.pallas_call(
        body, name="ffn_bwd", grid=(T // TT,),
        in_specs=[_rows(TT, D), _rows(TT, DFF), _rows(TT, DFF), ANY, ANY, ANY],
        out_specs=[_rows(TT, DFF), _rows(TT, DFF), _rows(TT, D)],
        out_shape=[S((T, DFF), BF), S((T, DFF), BF), S((T, D), F32)],
        scratch_shapes=[pltpu.VMEM((D, DFF), BF), pltpu.VMEM((DFF, D), BF), pltpu.VMEM((DFF, D), BF),
                        pltpu.SemaphoreType.DMA((3,))],
        compiler_params=_params(),
    )(df, gate, up, wdT, wgT, wuT)


def _mix_bwd(dh2, x1, dx2, mix, woutT, g_pf, g_pm):
    T = x1.shape[0]

    def body(dh2_ref, x1_ref, dx2_ref, mix_ref, w_hbm, gpf_ref, gpm_ref,
             dx1_ref, dmix_ref, dyl_ref, dys_ref, st_ref, w_vm, sem):
        _load_once([(w_hbm, w_vm)], sem)

        @pl.when(pl.program_id(0) == 0)
        def _():
            st_ref[...] = jnp.zeros_like(st_ref)

        dxa, dgpf = _rms_bwd(x1_ref[...], gpf_ref[...], dh2_ref[...])
        dx1 = dx2_ref[...] + dxa
        dx1_ref[...] = dx1
        dmix, dgpm = _rms_bwd(mix_ref[...], gpm_ref[...], dx1)
        dmix = dmix.astype(BF)
        dmix_ref[...] = dmix
        st_ref[0:1, :] += dgpf
        st_ref[1:2, :] += dgpm
        dyl_ref[...] = jnp.dot(dmix, w_vm[:, 0:LW], preferred_element_type=F32)
        dys_ref[...] = jnp.dot(dmix, w_vm[:, LW:LW + SI], preferred_element_type=F32)

    return pl.pallas_call(
        body, name="mix_bwd", grid=(T // TT,),
        in_specs=[_rows(TT, D), _rows(TT, D), _rows(TT, D), _rows(TT, D), ANY, _whole((1, D)), _whole((1, D))],
        out_specs=[_rows(TT, D), _rows(TT, D), _rows(TT, LW), _rows(TT, SI), _whole((8, D))],
        out_shape=[S((T, D), F32), S((T, D), BF), S((T, LW), F32), S((T, SI), F32), S((8, D), F32)],
        scratch_shapes=[pltpu.VMEM((D, LW + SI), BF), pltpu.SemaphoreType.DMA((1,))],
        compiler_params=_params(),
    )(dh2, x1, dx2, mix, woutT, g_pf, g_pm)


def _halo(width, n_tiles, tile):
    per = tile // 8
    return pl.BlockSpec((8, width), lambda i: (jnp.maximum((n_tiles - 1 - i) * per - 1, 0), 0))


def _lru_bwd(dy, lxr, lg, h, p_lru, wa4, wx4, wa4T, wx4T):
    T = dy.shape[0]
    NT = T // TT

    def body(dy_ref, lx_ref, lxh_ref, lg_ref, h_ref, hh_ref, p_ref, wa_ref, wx_ref, waT_ref, wxT_ref,
             dlx_ref, dlg_ref, st_ref, dwa_ref, dwx_ref, xp, hp, dp, a_s, d_s, g_s, cc):
        first = pl.program_id(0) == 0
        top = pl.program_id(0) == NT - 1

        @pl.when(first)
        def _():
            st_ref[...] = jnp.zeros_like(st_ref)
            dwa_ref[...] = jnp.zeros_like(dwa_ref)
            dwx_ref[...] = jnp.zeros_like(dwx_ref)
            dp[TT:TT + 8, :] = jnp.zeros((8, LW), F32)
            cc[...] = jnp.zeros_like(cc)

        keep = jnp.where(top, 0.0, 1.0)
        xp[0:8, :] = lxh_ref[...] * keep
        xp[8:8 + TT, :] = lx_ref[...]
        hp[0:8, :] = hh_ref[...] * keep
        hp[8:8 + TT, :] = h_ref[...]
        lx = _conv_from(xp, p_ref, TT)
        r, i, sp, a, mult = _lru_gates(lx, p_ref, wa_ref, wx_ref)

        lg = lg_ref[...]
        hcur = h_ref[...]
        ge = _gelu(lg)
        dgated, dgn = _rms_bwd(hcur * ge, p_ref[8:9, :], dy_ref[...])
        st_ref[8:9, :] += dgn
        dlg_ref[...] = (dgated * hcur * _gelu_grad(lg)).astype(BF)
        a_s[...] = a
        d_s[...] = dgated * ge

        def step(k, c):
            t = TT - 1 - k
            g = d_s[pl.ds(t, 1), :] + c
            g_s[pl.ds(t, 1), :] = g
            return a_s[pl.ds(t, 1), :] * g

        cc[0:1, :] = lax.fori_loop(0, TT, step, cc[0:1, :], unroll=8)
        gt = g_s[...]
        da = gt * hp[pl.ds(7, TT), :]
        dmult = gt * i * lx
        di = gt * mult * lx
        dlxc = gt * mult * i
        dla = da * a - dmult * (a * a) / mult
        dr = dla * (-LRU_C * sp)
        st_ref[7:8, :] += jnp.sum(dla * (-LRU_C * r), axis=0, keepdims=True) * (-_sigmoid(-p_ref[7:8, :]))
        dzr = dr * r * (1.0 - r)
        dzi = di * i * (1.0 - i)
        st_ref[5:6, :] += jnp.sum(dzr, axis=0, keepdims=True)
        st_ref[6:7, :] += jnp.sum(dzi, axis=0, keepdims=True)
        dlxc = dlxc + _blockdiag_mm(dzr, waT_ref) + _blockdiag_mm(dzi, wxT_ref)
        for j in range(4):
            sl = slice(256 * j, 256 * (j + 1))
            dwa_ref[j] += _mm_tn(lx[:, sl], dzr[:, sl])
            dwx_ref[j] += _mm_tn(lx[:, sl], dzi[:, sl])
        dp[0:TT, :] = dlxc
        acc = p_ref[0:1, :] * dp[pl.ds(CONV_K - 1, TT), :]
        for k in range(1, CONV_K):
            acc = acc + p_ref[k:k + 1, :] * dp[pl.ds(CONV_K - 1 - k, TT), :]
        dlx_ref[...] = acc.astype(BF)
        dp[TT:TT + 8, :] = dp[0:8, :]
        for k in range(CONV_K):
            st_ref[k:k + 1, :] += jnp.sum(dlxc * xp[pl.ds(8 - CONV_K + 1 + k, TT), :], axis=0, keepdims=True)
        st_ref[4:5, :] += jnp.sum(dlxc, axis=0, keepdims=True)

    w4 = _whole((4, 256, 256))
    return pl.pallas_call(
        body, name="lru_bwd", grid=(NT,),
        in_specs=[_rows(TT, LW, NT), _rows(TT, LW, NT), _halo(LW, NT, TT), _rows(TT, LW, NT), _rows(TT, LW, NT),
                  _halo(LW, NT, TT), _whole((16, LW)), w4, w4, w4, w4],
        out_specs=[_rows(TT, LW, NT), _rows(TT, LW, NT), _whole((16, LW)), w4, w4],
        out_shape=[S((T, LW), BF), S((T, LW), BF), S((16, LW), F32), S((4, 256, 256), F32), S((4, 256, 256), F32)],
        scratch_shapes=[pltpu.VMEM((TT + 8, LW), F32), pltpu.VMEM((TT + 8, LW), F32), pltpu.VMEM((TT + 8, LW), F32),
                        pltpu.VMEM((TT, LW), F32), pltpu.VMEM((TT, LW), F32), pltpu.VMEM((TT, LW), F32),
                        pltpu.VMEM((8, LW), F32)],
        compiler_params=_params(),
    )(dy, lxr, lxr, lg, h, h, p_lru, wa4, wx4, wa4T, wx4T)


def _ssd_bwd(dyn, xbcr, z, dtr, y, states, cw_ssd, hp_ssd, g_ssd):
    T = dyn.shape[0]
    NC = T // CH

    def body(dyn_ref, xr_ref, xh_ref, z_ref, dt_ref, y_ref, st_ref, cw_ref, hp_ref, g_ref,
             dxbc_ref, dz_ref, ddt_ref, cst_ref, hst_ref, gst_ref, xp, dp, dS, dxb):
        first = pl.program_id(0) == 0
        top = pl.program_id(0) == NC - 1

        @pl.when(first)
        def _():
            cst_ref[...] = jnp.zeros_like(cst_ref)
            hst_ref[...] = jnp.zeros_like(hst_ref)
            gst_ref[...] = jnp.zeros_like(gst_ref)
            dp[CH:CH + 8, :] = jnp.zeros((8, XBC), F32)
            dS[...] = jnp.zeros_like(dS)

        xp[0:8, :] = xh_ref[...] * jnp.where(top, 0.0, 1.0)
        cv, sg, xbc, raw, dtv, A, cs = _ssd_prep(xp, xr_ref, dt_ref, cw_ref, hp_ref)
        csT = cs.T
        E = jnp.exp(cs)
        cl = cs[CH - 1:CH, :]
        dsm = jnp.exp(cl - cs)
        El = jnp.exp(cl)
        row_i = lax.broadcasted_iota(jnp.int32, (CH, CH), 0)
        col_i = lax.broadcasted_iota(jnp.int32, (CH, CH), 1)
        tril = row_i >= col_i

        zz = z_ref[...]
        sz = _sigmoid(zz)
        yv = y_ref[...]
        dgn, dg = _rms_bwd(yv * (zz * sz), g_ref[...], dyn_ref[...])
        gst_ref[0:1, :] += dg
        dz_ref[...] = (dgn * yv * (sz + zz * sz * (1.0 - sz))).astype(BF)
        dY = dgn * (zz * sz)

        dcs_col = jnp.zeros((CH, DTP), F32)
        dcs_row = jnp.zeros((CH, DTP), F32)
        ddt_col = jnp.zeros((CH, DTP), F32)
        dD = jnp.zeros((1, DTP), F32)
        lane1 = col_i[0:1, :]
        last_row = row_i == CH - 1
        for g in range(NG):
            Bg = xbc[:, SI + NS * g:SI + NS * (g + 1)]
            Cg = xbc[:, SI + NG * NS + NS * g:SI + NG * NS + NS * (g + 1)]
            G = _mm_nt(Cg, Bg)
            dG = jnp.zeros((CH, CH), F32)
            dBg = jnp.zeros((CH, NS), F32)
            dCg = jnp.zeros((CH, NS), F32)
            for hh in range(HPG):
                h = g * HPG + hh
                hs = slice(HD * h, HD * (h + 1))
                Lm = jnp.exp(jnp.where(tril, cs[:, h:h + 1] - csT[h:h + 1, :], -1e30))
                M = G * Lm
                Xh = xbc[:, hs]
                dtc = dtv[:, h:h + 1]
                xs = Xh * dtc
                dYh = dY[:, hs]
                Dh = hp_ref[2:3, h:h + 1]
                dD = dD + jnp.where(lane1 == h, _sum_all(dYh * Xh), 0.0)
                Sp = st_ref[0, hs, :]
                Ec = E[:, h:h + 1]
                Yo = _mm_nt(Cg, Sp) * Ec
                dcs_h = jnp.sum(dYh * Yo, axis=1, keepdims=True)
                dP = dYh * Ec
                dCg = dCg + _mm(dP, Sp)
                dSp = _mm_tn(dP, Cg)
                dSe = dS[hs, :]
                elh = El[:, h:h + 1]
                dSp = dSp + elh * dSe
                dcl = _sum_all(dSe * Sp) * elh
                Q = _mm_nt(Bg, dSe)
                dsc = dsm[:, h:h + 1]
                dxs = Q * dsc
                dds = jnp.sum(Q * xs, axis=1, keepdims=True) * dsc
                dBg = dBg + _mm(xs * dsc, dSe)
                dcs_h = dcs_h - dds
                dcl = dcl + jnp.sum(dds, axis=0, keepdims=True)
                dM = _mm_nt(dYh, xs)
                dxs = dxs + _mm_tn(M, dYh)
                Wm = dM * M
                dcs_h = dcs_h + jnp.sum(Wm, axis=1, keepdims=True)
                dcs_h = dcs_h + jnp.where(last_row[:, 0:1], dcl, 0.0)
                dcs_col = dcs_col + jnp.where(col_i == h, dcs_h, 0.0)
                dcs_row = dcs_row + jnp.where(row_i == h, -jnp.sum(Wm, axis=0, keepdims=True), 0.0)
                dG = dG + dM * Lm
                dxb[:, hs] = Dh * dYh + dxs * dtc
                ddt_col = ddt_col + jnp.where(col_i == h, jnp.sum(dxs * Xh, axis=1, keepdims=True), 0.0)
                dS[hs, :] = dSp
            dxb[:, SI + NS * g:SI + NS * (g + 1)] = dBg + _mm_tn(dG, Cg)
            dxb[:, SI + NG * NS + NS * g:SI + NG * NS + NS * (g + 1)] = dCg + _mm(dG, Bg)

        da = _rev_cumsum_rows(dcs_col + dcs_row.T, CH)
        ddt_col = ddt_col + da * A
        hst_ref[1:2, :] += jnp.sum(da * dtv, axis=0, keepdims=True) * A
        hst_ref[2:3, :] += dD
        draw = jnp.where(col_i < NH, ddt_col * _sigmoid(raw), 0.0)
        ddt_ref[...] = draw.astype(BF)
        hst_ref[0:1, :] += jnp.sum(draw, axis=0, keepdims=True)

        dcv = dxb[...] * (sg + cv * sg * (1.0 - sg))
        dp[0:CH, :] = dcv
        acc = cw_ref[0:1, :] * dp[pl.ds(CONV_K - 1, CH), :]
        for k in range(1, CONV_K):
            acc = acc + cw_ref[k:k + 1, :] * dp[pl.ds(CONV_K - 1 - k, CH), :]
        dxbc_ref[...] = acc.astype(BF)
        dp[CH:CH + 8, :] = dp[0:8, :]
        for k in range(CONV_K):
            cst_ref[k:k + 1, :] += jnp.sum(dcv * xp[pl.ds(8 - CONV_K + 1 + k, CH), :], axis=0, keepdims=True)
        cst_ref[4:5, :] += jnp.sum(dcv, axis=0, keepdims=True)

    return pl.pallas_call(
        body, name="ssd_bwd", grid=(NC,),
        in_specs=[_rows(CH, SI, NC), _rows(CH, XBC, NC), _halo(XBC, NC, CH), _rows(CH, SI, NC), _rows(CH, DTP, NC),
                  _rows(CH, SI, NC), pl.BlockSpec((1, NH * HD, NS), lambda i: (NC - 1 - i, 0, 0)),
                  _whole((8, XBC)), _whole((8, DTP)), _whole((1, SI))],
        out_specs=[_rows(CH, XBC, NC), _rows(CH, SI, NC), _rows(CH, DTP, NC), _whole((8, XBC)), _whole((8, DTP)),
                   _whole((8, SI))],
        out_shape=[S((T, XBC), BF), S((T, SI), BF), S((T, DTP), BF), S((8, XBC), F32), S((8, DTP), F32), S((8, SI), F32)],
        scratch_shapes=[pltpu.VMEM((CH + 8, XBC), F32), pltpu.VMEM((CH + 8, XBC), F32), pltpu.VMEM((NH * HD, NS), F32),
                        pltpu.VMEM((CH, XBC), F32)],
        compiler_params=_params(),
    )(dyn, xbcr, xbcr, z, dtr, y, states, cw_ssd, hp_ssd, g_ssd)


def _inproj_bwd(dlx, dlg, dz, dxbc, ddt, x, dx1, wcatT, g0):
    T = x.shape[0]

    def body(dlx_ref, dlg_ref, dz_ref, dxbc_ref, ddt_ref, x_ref, dx1_ref, w_hbm, g_ref, dx_ref, st_ref, w_vm, sem):
        _load_once([(w_hbm, w_vm)], sem)

        @pl.when(pl.program_id(0) == 0)
        def _():
            st_ref[...] = jnp.zeros_like(st_ref)

        dh = jnp.dot(dlx_ref[...], w_vm[0:1024, :], preferred_element_type=F32)
        dh = dh + jnp.dot(dlg_ref[...], w_vm[1024:2048, :], preferred_element_type=F32)
        dh = dh + jnp.dot(dz_ref[...], w_vm[2048:3072, :], preferred_element_type=F32)
        dh = dh + jnp.dot(dxbc_ref[...], w_vm[3072:3072 + XBC, :], preferred_element_type=F32)
        dh = dh + jnp.dot(ddt_ref[...], w_vm[3072 + XBC:PC, :], preferred_element_type=F32)
        dx, dg = _rms_bwd(x_ref[...], g_ref[...], dh)
        dx_ref[...] = dx1_ref[...] + dx
        st_ref[0:1, :] += dg

    return pl.pallas_call(
        body, name="inproj_bwd", grid=(T // TT,),
        in_specs=[_rows(TT, 1024), _rows(TT, 1024), _rows(TT, 1024), _rows(TT, XBC), _rows(TT, DTP), _rows(TT, D),
                  _rows(TT, D), ANY, _whole((1, D))],
        out_specs=[_rows(TT, D), _whole((8, D))],
        out_shape=[S((T, D), F32), S((8, D), F32)],
        scratch_shapes=[pltpu.VMEM((PC, D), BF), pltpu.SemaphoreType.DMA((1,))],
        compiler_params=_params(),
    )(dlx, dlg, dz, dxbc, ddt, x, dx1, wcatT, g0)


def _wgrad(name, a, bs):
    T, M = a.shape
    nb = len(bs)
    widths = [b.shape[1] for b in bs]
    nk = T // TK

    def body(*refs):
        a_ref = refs[0]
        b_refs = refs[1:1 + nb]
        o_refs = refs[1 + nb:1 + 2 * nb]
        accs = refs[1 + 2 * nb:1 + 3 * nb]
        sem = refs[1 + 3 * nb]
        k = pl.program_id(0)
        av = a_ref[...]
        for b_ref, acc in zip(b_refs, accs):
            p = lax.dot_general(av, b_ref[...], (((0,), (0,)), ((), ())), preferred_element_type=F32)

            @pl.when(k == 0)
            def _():
                acc[...] = p

            @pl.when(k > 0)
            def _():
                acc[...] += p

        @pl.when(k == nk - 1)
        def _():
            for j, (acc, o_ref) in enumerate(zip(accs, o_refs)):
                pltpu.make_async_copy(acc, o_ref, sem.at[j]).start()
            for j, (acc, o_ref) in enumerate(zip(accs, o_refs)):
                pltpu.make_async_copy(acc, o_ref, sem.at[j]).wait()

    return pl.pallas_call(
        body, name=name, grid=(nk,),
        in_specs=[_rows(TK, M)] + [_rows(TK, w) for w in widths],
        out_specs=[ANY] * nb,
        out_shape=[S((M, w), F32) for w in widths],
        scratch_shapes=[pltpu.VMEM((M, w), F32) for w in widths] + [pltpu.SemaphoreType.DMA((nb,))],
        compiler_params=_params(),
    )(a, *bs)


def _adamw(name, w, g, m, v):
    R, C = w.shape
    tr = _row_tile(R, C)
    c1 = 1.0 - ADAM_B1 ** ADAM_STEP
    c2 = 1.0 - ADAM_B2 ** ADAM_STEP

    def body(w_ref, g_ref, m_ref, v_ref, d_ref, nm_ref, nv_ref):
        gg = g_ref[...]
        mm = ADAM_B1 * m_ref[...] + (1.0 - ADAM_B1) * gg
        vv = ADAM_B2 * v_ref[...] + (1.0 - ADAM_B2) * (gg * gg)
        nm_ref[...] = mm
        nv_ref[...] = vv
        d_ref[...] = -ADAM_LR * ((mm / c1) / (jnp.sqrt(vv / c2) + ADAM_EPS) + ADAM_WD * w_ref[...])

    blk = pl.BlockSpec((tr, C), lambda i: (i, 0))
    return pl.pallas_call(
        body, name=name, grid=(R // tr,),
        in_specs=[blk] * 4, out_specs=[blk] * 3, out_shape=[S((R, C), F32)] * 3,
        compiler_params=_params(),
    )(w, g, m, v)


def _pos():
    return lax.axis_index("x"), lax.axis_index("y"), lax.axis_index("c")


def _other_chips(x, y):
    return [(1 - x, y), (x, 1 - y), (1 - x, 1 - y)]


def _half(ref, c, hr):
    sl = pl.ds(pl.multiple_of(c * hr, 8), hr)
    return ref.at[:, sl, :] if len(ref.shape) == 3 else ref.at[sl, :]


def _remote(src, dst, send_sem, recv_sem, to):
    return pltpu.make_async_remote_copy(src_ref=src, dst_ref=dst, send_sem=send_sem, recv_sem=recv_sem,
                                        device_id=to, device_id_type=MESH)


def _allgather_weights(p):
    R = p.shape[0]
    hr = R // 2

    def body(p_ref, out_ref, send_sems, recv_sems, local_sem):
        x, y, c = _pos()
        me = 2 * x + y
        chips = _other_chips(x, y)
        mine = pltpu.make_async_copy(p_ref, out_ref.at[me], local_sem)
        mine.start()
        my_half = pl.ds(pl.multiple_of(c * hr, 16), hr)
        sib_half = pl.ds(pl.multiple_of((1 - c) * hr, 16), hr)
        first = [_remote(p_ref.at[my_half, :], out_ref.at[me, my_half, :], send_sems.at[k], recv_sems.at[k], (cx, cy, c))
                 for k, (cx, cy) in enumerate(chips)]
        for cp in first:
            cp.start()
        passed = []
        for k, (cx, cy) in enumerate(chips):
            blk = out_ref.at[2 * cx + cy, my_half, :]
            _remote(blk, blk, send_sems.at[k], recv_sems.at[k], (cx, cy, c)).wait_recv()
            fw = _remote(blk, blk, send_sems.at[3 + k], recv_sems.at[3 + k], (x, y, 1 - c))
            fw.start()
            passed.append(fw)
        for k, (cx, cy) in enumerate(chips):
            blk = out_ref.at[2 * cx + cy, sib_half, :]
            _remote(blk, blk, send_sems.at[3 + k], recv_sems.at[3 + k], (x, y, 1 - c)).wait_recv()
        for cp in first + passed:
            cp.wait_send()
        mine.wait()

    return pl.pallas_call(
        body, name="allgather_weights", in_specs=[ANY], out_specs=ANY, out_shape=S((4,) + p.shape, p.dtype),
        scratch_shapes=[pltpu.SemaphoreType.DMA((6,)), pltpu.SemaphoreType.DMA((6,)), pltpu.SemaphoreType.DMA],
    )(p)


def _pair_exchange(bufs):
    n = len(bufs)

    def half_shape(b):
        return b.shape[:-2] + (b.shape[-2] // 2, b.shape[-1])

    def body(*refs):
        ins, outs = refs[:n], refs[n:2 * n]
        send_sems, recv_sems = refs[2 * n], refs[2 * n + 1]
        x, y, c = _pos()
        copies = [_remote(_half(src, 1 - c, src.shape[-2] // 2), dst, send_sems.at[k], recv_sems.at[k], (x, y, 1 - c))
                  for k, (src, dst) in enumerate(zip(ins, outs))]
        for cp in copies:
            cp.start()
        for cp in copies:
            cp.wait()

    return pl.pallas_call(
        body, name="pair_exchange", in_specs=[ANY] * n, out_specs=[ANY] * n,
        out_shape=[S(half_shape(b), b.dtype) for b in bufs],
        scratch_shapes=[pltpu.SemaphoreType.DMA((n,)), pltpu.SemaphoreType.DMA((n,))],
    )(*bufs)


def _quad_exchange(bufs, scatter):
    n = len(bufs)

    def body(*refs):
        ins, outs = refs[:n], refs[n:2 * n]
        send_sems, recv_sems, local_sems = refs[2 * n], refs[2 * n + 1], refs[2 * n + 2]
        x, y, c = _pos()
        me = 2 * x + y
        chips = _other_chips(x, y)
        copies = []
        for k, (src, dst) in enumerate(zip(ins, outs)):
            own = pltpu.make_async_copy(src.at[me] if scatter[k] else src, dst.at[me], local_sems.at[k])
            own.start()
            copies.append(own)
            for j, (cx, cy) in enumerate(chips):
                piece = src.at[2 * cx + cy] if scatter[k] else src
                cp = _remote(piece, dst.at[me], send_sems.at[3 * k + j], recv_sems.at[3 * k + j], (cx, cy, c))
                cp.start()
                copies.append(cp)
        for k, (src, dst) in enumerate(zip(ins, outs)):
            for j, (cx, cy) in enumerate(chips):
                blk = dst.at[2 * cx + cy]
                _remote(blk, blk, send_sems.at[3 * k + j], recv_sems.at[3 * k + j], (cx, cy, c)).wait_recv()
        for k in range(n):
            copies[4 * k].wait()
            for j in range(3):
                copies[4 * k + 1 + j].wait_send()

    return pl.pallas_call(
        body, name="quad_exchange", in_specs=[ANY] * n, out_specs=[ANY] * n,
        out_shape=[S((4,) + (b.shape[1:] if sc else b.shape), b.dtype) for b, sc in zip(bufs, scatter)],
        scratch_shapes=[pltpu.SemaphoreType.DMA((3 * n,)), pltpu.SemaphoreType.DMA((3 * n,)), pltpu.SemaphoreType.DMA((n,))],
    )(*bufs)


def _pair_gather(bufs):
    n = len(bufs)

    def body(*refs):
        ins, outs = refs[:n], refs[n:2 * n]
        send_sems, recv_sems, local_sems = refs[2 * n], refs[2 * n + 1], refs[2 * n + 2]
        x, y, c = _pos()
        copies = []
        for k, (src, dst) in enumerate(zip(ins, outs)):
            mine = _half(dst, c, src.shape[0])
            own = pltpu.make_async_copy(src, mine, local_sems.at[k])
            own.start()
            cp = _remote(src, mine, send_sems.at[k], recv_sems.at[k], (x, y, 1 - c))
            cp.start()
            copies.append((own, cp))
        for k, (src, dst) in enumerate(zip(ins, outs)):
            theirs = _half(dst, 1 - c, src.shape[0])
            _remote(theirs, theirs, send_sems.at[k], recv_sems.at[k], (x, y, 1 - c)).wait_recv()
        for own, cp in copies:
            own.wait()
            cp.wait_send()

    return pl.pallas_call(
        body, name="pair_gather", in_specs=[ANY] * n, out_specs=[ANY] * n,
        out_shape=[S((2 * b.shape[0], b.shape[1]), b.dtype) for b in bufs],
        scratch_shapes=[pltpu.SemaphoreType.DMA((n,)), pltpu.SemaphoreType.DMA((n,)), pltpu.SemaphoreType.DMA((n,))],
    )(*bufs)


def _row_tile(rows, cols):
    best = 8
    for t in range(8, rows + 1, 8):
        if rows % t == 0 and t * cols * 4 <= (1 << 20):
            best = t
    return best


def _add_own_half(name, full, got, c):
    three = len(full.shape) == 3
    lead = full.shape[0] if three else 1
    rows, cols = full.shape[-2], full.shape[-1]
    hr = rows // 2
    tr = _row_tile(hr, cols)
    per = hr // tr

    def body(c_ref, a_ref, b_ref, o_ref):
        o_ref[...] = a_ref[...] + b_ref[...]

    if three:
        a_spec = pl.BlockSpec((1, tr, cols), lambda s, i, c_ref: (s, c_ref[0] * per + i, 0))
        o_spec = pl.BlockSpec((1, tr, cols), lambda s, i, c_ref: (s, i, 0))
    else:
        a_spec = pl.BlockSpec((tr, cols), lambda s, i, c_ref: (c_ref[0] * per + i, 0))
        o_spec = pl.BlockSpec((tr, cols), lambda s, i, c_ref: (i, 0))
    return pl.pallas_call(
        body, name=name,
        grid_spec=pltpu.PrefetchScalarGridSpec(num_scalar_prefetch=1, grid=(lead, per), in_specs=[a_spec, o_spec],
                                               out_specs=o_spec),
        out_shape=S(got.shape, F32), compiler_params=_params(2),
    )(jnp.reshape(c, (1,)).astype(jnp.int32), full, got)


def _sum_slots(name, slots):
    _, rows, cols = slots.shape
    tr = _row_tile(rows, cols)

    def body(s_ref, o_ref):
        o_ref[...] = ((s_ref[0] + s_ref[1]) + s_ref[2]) + s_ref[3]

    return pl.pallas_call(
        body, name=name, grid=(rows // tr,),
        in_specs=[pl.BlockSpec((4, tr, cols), lambda i: (0, i, 0))], out_specs=pl.BlockSpec((tr, cols), lambda i: (i, 0)),
        out_shape=S((rows, cols), F32), compiler_params=_params(),
    )(slots)


SMALL = (("pre_mix_norm", 1024), ("lru_conv_w", 4096), ("lru_conv_b", 1024), ("lru_wa", 65536), ("lru_ba", 1024),
         ("lru_wx", 65536), ("lru_bx", 1024), ("lru_lambda", 1024), ("lru_out_norm", 1024), ("ssd_conv_w", 6144),
         ("ssd_conv_b", 1536), ("ssd_dt_bias", 16), ("ssd_a_log", 16), ("ssd_d", 16), ("ssd_out_norm", 1024),
         ("post_mix_norm", 1024), ("pre_ffn_norm", 1024), ("post_ffn_norm", 1024))
SMALL_ROWS = 1200
BIG = ("w_in", "w_out", "w_gate", "w_up", "w_down")


def _diag4(w):
    w = w.reshape(4, 4, BW, BW)
    out = jnp.zeros((4, 4, BW, 4, BW), w.dtype)
    for a in range(4):
        out = out.at[:, a, :, a, :].set(w[:, a])
    return out.reshape(4, 4 * BW, 4 * BW)


def _undiag4(w4):
    w4 = w4.reshape(4, 4, BW, 4, BW)
    return jnp.stack([w4[:, a, :, a, :] for a in range(4)], axis=1).reshape(NBLK, BW, BW)


def _pack_small(parts):
    flat = jnp.concatenate([parts[name].reshape(-1).astype(F32) for name, _ in SMALL])
    return jnp.pad(flat, (0, SMALL_ROWS * 128 - flat.shape[0])).reshape(SMALL_ROWS, 128)


def _unpack_small(buf):
    flat = buf.reshape(-1)
    out, off = {}, 0
    for name, size in SMALL:
        out[name] = flat[off:off + size]
        off += size
    return out


def _gather_weights(w_in, w_out, w_gate, w_up, w_down, lru_conv_w, ssd_conv_w):
    conv = jnp.concatenate([lru_conv_w.reshape(-1), ssd_conv_w.reshape(-1)]).astype(F32)
    parts = [w.reshape(-1).astype(jnp.bfloat16) for w in (w_in, w_out, w_gate, w_up, w_down)]
    hi = conv.astype(jnp.bfloat16)
    mid = (conv - hi.astype(F32)).astype(jnp.bfloat16)
    lo = (conv - hi.astype(F32) - mid.astype(F32)).astype(jnp.bfloat16)
    parts.append(jnp.concatenate([hi, mid, lo]))
    sizes = [p.shape[0] for p in parts]
    flat = jnp.concatenate(parts)
    rows = -(-flat.shape[0] // (128 * 32)) * 32
    flat = jnp.pad(flat, (0, rows * 128 - flat.shape[0])).reshape(rows, 128)
    full = _allgather_weights(flat).reshape(4, rows * 128)
    offs = [0]
    for s in sizes:
        offs.append(offs[-1] + s)
    seg = [full[:, offs[k]:offs[k + 1]] for k in range(len(sizes))]
    sh = w_in.shape[1]
    win_f = seg[0].reshape(4, D, sh).transpose(1, 0, 2).reshape(D, 4 * sh)
    wout_f = seg[1].reshape(LW + SI, D)
    sh = w_gate.shape[1]
    wg_f = seg[2].reshape(4, D, sh).transpose(1, 0, 2).reshape(D, 4 * sh)
    wu_f = seg[3].reshape(4, D, sh).transpose(1, 0, 2).reshape(D, 4 * sh)
    wd_f = seg[4].reshape(DFF, D)
    terms = seg[5].reshape(4, 3, -1).astype(F32)
    conv_f = (terms[:, 0] + terms[:, 1]) + terms[:, 2]
    n1 = lru_conv_w.size
    lcw = conv_f[:, :n1].reshape(4, CONV_K, -1).transpose(1, 0, 2).reshape(CONV_K, LW)
    scw = conv_f[:, n1:].reshape(4, CONV_K, -1).transpose(1, 0, 2).reshape(CONV_K, XBC)
    return win_f, wout_f, wg_f, wu_f, wd_f, lcw, scw


def _local_step(x, tgt, win_f, wout_f, wg_f, wu_f, wd_f, lcw, scw, sp):
    mm = lambda w: w.astype(BF)
    wcat = jnp.concatenate([mm(win_f), jnp.zeros((D, PC - IN_COLS), BF)], axis=1)
    row = lambda v: v.reshape(1, -1).astype(F32)
    p_lru = jnp.concatenate([lcw, row(sp["lru_conv_b"]), row(sp["lru_ba"]), row(sp["lru_bx"]), row(sp["lru_lambda"]),
                             row(sp["lru_out_norm"]), jnp.zeros((7, LW), F32)], axis=0)
    wa4, wx4 = mm(_diag4(sp["lru_wa"][0])), mm(_diag4(sp["lru_wx"][0]))
    wa4T, wx4T = wa4.transpose(0, 2, 1), wx4.transpose(0, 2, 1)
    cw_ssd = jnp.concatenate([scw, row(sp["ssd_conv_b"]), jnp.zeros((3, XBC), F32)], axis=0)
    padh = lambda v: jnp.pad(row(v), ((0, 0), (0, DTP - NH)))
    hp_ssd = jnp.concatenate([padh(sp["ssd_dt_bias"]), padh(sp["ssd_a_log"]), padh(sp["ssd_d"]), jnp.zeros((5, DTP), F32)], axis=0)
    g0, g_ssd = row(sp["pre_mix_norm"]), row(sp["ssd_out_norm"])
    g_pm, g_pf, g_pff = row(sp["post_mix_norm"]), row(sp["pre_ffn_norm"]), row(sp["post_ffn_norm"])
    wout, wg, wu, wd = mm(wout_f), mm(wg_f), mm(wu_f), mm(wd_f)

    h0, lxr, lg, z, xbcr, dtr = _inproj(x, g0, wcat)
    h, ylru = _lru_fwd(lxr, lg, p_lru, wa4, wx4)
    y, yssd, states = _ssd_fwd(xbcr, z, dtr, cw_ssd, hp_ssd, g_ssd)
    mix, x1, h2 = _outproj(ylru, yssd, x, wout, g_pm, g_pf)
    gate, up, act, df, dx2, st_ffn = _ffn_fwd(h2, x1, tgt, wg, wu, wd, g_pff)
    dgate, dup, dh2 = _ffn_bwd(df, gate, up, wd.T, wg.T, wu.T)
    dx1, dmix, dyl, dys, st_mix = _mix_bwd(dh2, x1, dx2, mix, wout.T, g_pf, g_pm)
    dlx, dlg, st_lru, dwa4, dwx4 = _lru_bwd(dyl, lxr, lg, h, p_lru, wa4, wx4, wa4T, wx4T)
    dxbc, dz, ddt, cst, hst, gst = _ssd_bwd(dys, xbcr, z, dtr, y, states, cw_ssd, hp_ssd, g_ssd)
    gx, st_in = _inproj_bwd(dlx, dlg, dz, dxbc, ddt, x, dx1, wcat.T, g0)

    (dwd,) = _wgrad("wgrad_down", act, [df])
    dwg, dwu = _wgrad("wgrad_gate_up", h2, [dgate, dup])
    (dwo_l,) = _wgrad("wgrad_out_lru", ylru, [dmix])
    (dwo_s,) = _wgrad("wgrad_out_ssd", yssd, [dmix])
    pin = _wgrad("wgrad_in", h0, [dlx, dlg, dz, dxbc, ddt])
    dwin = jnp.concatenate([pin[0], pin[1], pin[2], pin[3], pin[4][:, :NH]], axis=1)
    big = {"w_in": dwin, "w_out": jnp.concatenate([dwo_l, dwo_s], axis=0), "w_gate": dwg, "w_up": dwu, "w_down": dwd}
    small = {
        "pre_mix_norm": st_in[0], "lru_conv_w": st_lru[0:4], "lru_conv_b": st_lru[4], "lru_wa": _undiag4(dwa4),
        "lru_ba": st_lru[5], "lru_wx": _undiag4(dwx4), "lru_bx": st_lru[6], "lru_lambda": st_lru[7],
        "lru_out_norm": st_lru[8], "ssd_conv_w": cst[0:4], "ssd_conv_b": cst[4], "ssd_dt_bias": hst[0, :NH],
        "ssd_a_log": hst[1, :NH], "ssd_d": hst[2, :NH], "ssd_out_norm": gst[0], "post_mix_norm": st_mix[1],
        "pre_ffn_norm": st_mix[0], "post_ffn_norm": st_ffn[1],
    }
    return jnp.sum(st_ffn[0]), gx, big, small


def _reduce_grads(big, small_buf):
    c = lax.axis_index("c")
    bufs = []
    for name in BIG:
        g = big[name]
        if name in ("w_in", "w_gate", "w_up"):
            bufs.append(g.reshape(g.shape[0], 4, g.shape[1] // 4).transpose(1, 0, 2))
        else:
            bufs.append(g.reshape(4, g.shape[0] // 4, g.shape[1]))
    bufs.append(small_buf)
    got = _pair_exchange(bufs)
    part = [_add_own_half("pair_add_%d" % k, b, r, c) for k, (b, r) in enumerate(zip(bufs, got))]
    slots = _quad_exchange(part, [True] * len(BIG) + [False])
    red = [_sum_slots("quad_sum_%d" % k, s) for k, s in enumerate(slots)]
    return _pair_gather(red)


def kernel(x, pre_mix_norm, w_in, lru_conv_w, lru_conv_b, lru_wa, lru_ba, lru_wx, lru_bx, lru_lambda, lru_out_norm, ssd_conv_w, ssd_conv_b, ssd_dt_bias, ssd_a_log, ssd_d, ssd_out_norm, w_out, post_mix_norm, pre_ffn_norm, w_gate, w_up, w_down, post_ffn_norm, loss_target, m_pre_mix_norm, m_w_in, m_lru_conv_w, m_lru_conv_b, m_lru_wa, m_lru_ba, m_lru_wx, m_lru_bx, m_lru_lambda, m_lru_out_norm, m_ssd_conv_w, m_ssd_conv_b, m_ssd_dt_bias, m_ssd_a_log, m_ssd_d, m_ssd_out_norm, m_w_out, m_post_mix_norm, m_pre_ffn_norm, m_w_gate, m_w_up, m_w_down, m_post_ffn_norm, v_pre_mix_norm, v_w_in, v_lru_conv_w, v_lru_conv_b, v_lru_wa, v_lru_ba, v_lru_wx, v_lru_bx, v_lru_lambda, v_lru_out_norm, v_ssd_conv_w, v_ssd_conv_b, v_ssd_dt_bias, v_ssd_a_log, v_ssd_d, v_ssd_out_norm, v_w_out, v_post_mix_norm, v_pre_ffn_norm, v_w_gate, v_w_up, v_w_down, v_post_ffn_norm):
    args = dict(locals())
    names = [n for n, _ in SMALL] + list(BIG)
    w = {n: args[n] for n in names}
    m = {n: args["m_" + n] for n in names}
    v = {n: args["v_" + n] for n in names}
    chip = 2 * lax.axis_index("x") + lax.axis_index("y")

    win_f, wout_f, wg_f, wu_f, wd_f, lcw, scw = _gather_weights(w_in[0], w_out[0], w_gate[0], w_up[0], w_down[0],
                                                                lru_conv_w[0], ssd_conv_w[0])
    sp = {n: w[n] for n, _ in SMALL}
    loss_part, gx, big, small = _local_step(x[0], loss_target[0], win_f, wout_f, wg_f, wu_f, wd_f, lcw, scw, sp)
    loss = lax.psum(loss_part, ("x", "y", "c"))

    red = _reduce_grads(big, _pack_small(small))
    gsmall = _unpack_small(red[len(BIG)])

    grads, delta, new_m, new_v = {}, {}, {}, {}
    for k, n in enumerate(BIG):
        g = red[k]
        d, nm, nv = _adamw("adamw_" + n, w[n][0], g, m[n][0], v[n][0])
        grads[n], delta[n], new_m[n], new_v[n] = g[None], d[None], nm[None], nv[None]

    def local_part(n, full):
        shape = w[n].shape
        if n in ("lru_conv_w", "ssd_conv_w"):
            per = shape[-1]
            return lax.dynamic_slice_in_dim(full.reshape(CONV_K, -1), chip * per, per, axis=1).reshape(shape)
        return full.reshape(shape)

    gl = {n: local_part(n, gsmall[n]) for n, _ in SMALL}

    def pack_local(d):
        flat = jnp.concatenate([d[n].reshape(-1) for n, _ in SMALL])
        rows = -(-flat.shape[0] // (128 * 8)) * 8
        return jnp.pad(flat, (0, rows * 128 - flat.shape[0])).reshape(rows, 128), flat.shape[0]

    wp, nflat = pack_local({n: w[n] for n, _ in SMALL})
    gp, _ = pack_local(gl)
    mp, _ = pack_local({n: m[n] for n, _ in SMALL})
    vp, _ = pack_local({n: v[n] for n, _ in SMALL})
    pad_mask = (jnp.arange(wp.size).reshape(wp.shape) >= nflat)
    dp_, nmp, nvp = _adamw("adamw_small", wp, gp, mp, jnp.where(pad_mask, 1.0, vp))
    off = 0
    for n, _ in SMALL:
        size = w[n].size
        grads[n] = gl[n]
        delta[n] = dp_.reshape(-1)[off:off + size].reshape(w[n].shape)
        new_m[n] = nmp.reshape(-1)[off:off + size].reshape(w[n].shape)
        new_v[n] = nvp.reshape(-1)[off:off + size].reshape(w[n].shape)
        off += size

    order = ["pre_mix_norm", "w_in", "lru_conv_w", "lru_conv_b", "lru_wa", "lru_ba", "lru_wx", "lru_bx", "lru_lambda",
             "lru_out_norm", "ssd_conv_w", "ssd_conv_b", "ssd_dt_bias", "ssd_a_log", "ssd_d", "ssd_out_norm", "w_out",
             "post_mix_norm", "pre_ffn_norm", "w_gate", "w_up", "w_down", "post_ffn_norm"]
    return (loss, gx[None], *[grads[n] for n in order], *[delta[n] for n in order],
            *[new_m[n] for n in order], *[new_v[n] for n in order])
```

```python
import functools

import jax
import jax.numpy as jnp
from jax import lax
from jax.experimental import pallas as pl
from jax.experimental.pallas import tpu as pltpu

F32 = jnp.float32
BF = jnp.bfloat16

D = 1024
LW = 1024
NBLK = 16
BW = 64
SI = 1024
NH = 16
HD = 64
NG = 2
HPG = NH // NG
NS = 128
CH = 128
XBC = SI + 2 * NG * NS
DTP = 128
PC = 3 * 1024 + XBC + DTP
DFF = 2816
IN_COLS = 4624
EPS = 1e-6
LRU_C = 8.0
CONV_K = 4
TT = 256
TK = 512
VMEM_LIMIT = 56 * 1024 * 1024

ADAM_LR, ADAM_B1, ADAM_B2, ADAM_EPS, ADAM_WD, ADAM_STEP = 0.001, 0.9, 0.999, 1e-08, 0.01, 10

MESH = pl.DeviceIdType.MESH


def _mm(a, b):
    return jnp.dot(a.astype(BF), b.astype(BF), preferred_element_type=F32)


def _mm_nt(a, b):
    return lax.dot_general(a.astype(BF), b.astype(BF), (((1,), (1,)), ((), ())), preferred_element_type=F32)


def _mm_tn(a, b):
    return lax.dot_general(a.astype(BF), b.astype(BF), (((0,), (0,)), ((), ())), preferred_element_type=F32)


def _sigmoid(x):
    return jax.nn.sigmoid(x)


def _softplus(x):
    return jnp.maximum(x, 0.0) + jnp.log1p(jnp.exp(-jnp.abs(x)))


_GELU_C = 0.7978845608028654
_GELU_K = 0.044715


def _gelu(x):
    t = jnp.tanh(_GELU_C * (x + _GELU_K * x * x * x))
    return 0.5 * x * (1.0 + t)


def _gelu_grad(x):
    t = jnp.tanh(_GELU_C * (x + _GELU_K * x * x * x))
    return 0.5 * (1.0 + t) + 0.5 * x * (1.0 - t * t) * _GELU_C * (1.0 + 3.0 * _GELU_K * x * x)


def _rms_fwd(x, g):
    r = lax.rsqrt(jnp.mean(x * x, axis=-1, keepdims=True) + EPS)
    return x * r * g


def _rms_bwd(x, g, dy):
    r = lax.rsqrt(jnp.mean(x * x, axis=-1, keepdims=True) + EPS)
    xh = x * r
    dxh = dy * g
    dg = jnp.sum(dy * xh, axis=0, keepdims=True)
    dx = r * (dxh - xh * jnp.mean(dxh * xh, axis=-1, keepdims=True))
    return dx, dg


def _sum_all(x):
    return jnp.sum(jnp.sum(x, axis=1, keepdims=True), axis=0, keepdims=True)


def _cumsum_rows(x, n):
    row = lax.broadcasted_iota(jnp.int32, x.shape, 0)
    k = 1
    while k < n:
        x = x + jnp.where(row >= k, pltpu.roll(x, k, 0), 0.0)
        k *= 2
    return x


def _rev_cumsum_rows(x, n):
    row = lax.broadcasted_iota(jnp.int32, x.shape, 0)
    k = 1
    while k < n:
        x = x + jnp.where(row < n - k, pltpu.roll(x, n - k, 0), 0.0)
        k *= 2
    return x


def _load_once(pairs, sem):
    @pl.when(pl.program_id(0) == 0)
    def _():
        for k, (src, dst) in enumerate(pairs):
            pltpu.make_async_copy(src, dst, sem.at[k]).start()
        for k, (src, dst) in enumerate(pairs):
            pltpu.make_async_copy(src, dst, sem.at[k]).wait()


def _params(n_axes=1):
    return pltpu.CompilerParams(dimension_semantics=("arbitrary",) * n_axes, vmem_limit_bytes=VMEM_LIMIT)


def _rows(n, width, rev_of=None):
    if rev_of is None:
        return pl.BlockSpec((n, width), lambda i: (i, 0))
    return pl.BlockSpec((n, width), lambda i: (rev_of - 1 - i, 0))


def _whole(shape):
    nd = len(shape)
    return pl.BlockSpec(shape, lambda i: (0,) * nd)


ANY = pl.BlockSpec(memory_space=pl.ANY)
S = jax.ShapeDtypeStruct


def _inproj(x, g0, wcat):
    T = x.shape[0]

    def body(x_ref, g_ref, w_hbm, h0_ref, lx_ref, lg_ref, z_ref, xbc_ref, dt_ref, w_vm, sem):
        _load_once([(w_hbm, w_vm)], sem)
        h = _rms_fwd(x_ref[...], g_ref[...]).astype(BF)
        h0_ref[...] = h
        lx_ref[...] = jnp.dot(h, w_vm[:, 0:1024], preferred_element_type=F32)
        lg_ref[...] = jnp.dot(h, w_vm[:, 1024:2048], preferred_element_type=F32)
        z_ref[...] = jnp.dot(h, w_vm[:, 2048:3072], preferred_element_type=F32)
        xbc_ref[...] = jnp.dot(h, w_vm[:, 3072:3072 + XBC], preferred_element_type=F32)
        dt_ref[...] = jnp.dot(h, w_vm[:, 3072 + XBC:PC], preferred_element_type=F32)

    return pl.pallas_call(
        body, name="inproj", grid=(T // TT,),
        in_specs=[_rows(TT, D), _whole((1, D)), ANY],
        out_specs=[_rows(TT, D), _rows(TT, 1024), _rows(TT, 1024), _rows(TT, 1024), _rows(TT, XBC), _rows(TT, DTP)],
        out_shape=[S((T, D), BF), S((T, 1024), F32), S((T, 1024), F32), S((T, 1024), F32), S((T, XBC), F32), S((T, DTP), F32)],
        scratch_shapes=[pltpu.VMEM((D, PC), BF), pltpu.SemaphoreType.DMA((1,))],
        compiler_params=_params(),
    )(x, g0, wcat)


def _blockdiag_mm(v, w4_ref):
    return jnp.concatenate([_mm(v[:, 256 * j:256 * (j + 1)], w4_ref[j]) for j in range(4)], axis=1)


def _lru_gates(lx, p_ref, wa_ref, wx_ref):
    r = _sigmoid(_blockdiag_mm(lx, wa_ref) + p_ref[5:6, :])
    i = _sigmoid(_blockdiag_mm(lx, wx_ref) + p_ref[6:7, :])
    sp = _softplus(-p_ref[7:8, :])
    la = -LRU_C * r * sp
    a = jnp.exp(la)
    th = jnp.tanh(la)
    mult = jnp.sqrt(-2.0 * th / (1.0 - th))
    return r, i, sp, a, mult


def _conv_from(xp_ref, p_ref, n):
    acc = p_ref[4:5, :] + p_ref[0:1, :] * xp_ref[pl.ds(8 - CONV_K + 1, n), :]
    for k in range(1, CONV_K):
        acc = acc + p_ref[k:k + 1, :] * xp_ref[pl.ds(8 - CONV_K + 1 + k, n), :]
    return acc


def _lru_fwd(lxr, lg, p_lru, wa4, wx4):
    T = lxr.shape[0]

    def body(lx_ref, lg_ref, p_ref, wa_ref, wx_ref, h_ref, y_ref, xp, a_s, u_s, hc):
        @pl.when(pl.program_id(0) == 0)
        def _():
            xp[0:8, :] = jnp.zeros((8, LW), F32)
            hc[...] = jnp.zeros_like(hc)

        xp[8:8 + TT, :] = lx_ref[...]
        lx = _conv_from(xp, p_ref, TT)
        xp[0:8, :] = xp[TT:TT + 8, :]
        r, i, sp, a, mult = _lru_gates(lx, p_ref, wa_ref, wx_ref)
        a_s[...] = a
        u_s[...] = mult * (i * lx)

        def step(t, h):
            h = a_s[pl.ds(t, 1), :] * h + u_s[pl.ds(t, 1), :]
            h_ref[pl.ds(t, 1), :] = h
            return h

        hc[0:1, :] = lax.fori_loop(0, TT, step, hc[0:1, :], unroll=8)
        gated = h_ref[...] * _gelu(lg_ref[...])
        y_ref[...] = _rms_fwd(gated, p_ref[8:9, :]).astype(BF)

    return pl.pallas_call(
        body, name="lru_fwd", grid=(T // TT,),
        in_specs=[_rows(TT, LW), _rows(TT, LW), _whole((16, LW)), _whole((4, 256, 256)), _whole((4, 256, 256))],
        out_specs=[_rows(TT, LW), _rows(TT, LW)],
        out_shape=[S((T, LW), F32), S((T, LW), BF)],
        scratch_shapes=[pltpu.VMEM((TT + 8, LW), F32), pltpu.VMEM((TT, LW), F32), pltpu.VMEM((TT, LW), F32),
                        pltpu.VMEM((8, LW), F32)],
        compiler_params=_params(),
    )(lxr, lg, p_lru, wa4, wx4)


def _ssd_prep(xp, xr_ref, dt_ref, cw_ref, hp_ref):
    xp[8:8 + CH, :] = xr_ref[...]
    cv = _conv_from(xp, cw_ref, CH)
    sg = _sigmoid(cv)
    xbc = cv * sg
    lane = lax.broadcasted_iota(jnp.int32, (CH, DTP), 1)
    raw = dt_ref[...] + hp_ref[0:1, :]
    dtv = jnp.where(lane < NH, _softplus(raw), 0.0)
    A = jnp.where(lane[0:1, :] < NH, -jnp.exp(hp_ref[1:2, :]), 0.0)
    cs = _cumsum_rows(dtv * A, CH)
    return cv, sg, xbc, raw, dtv, A, cs


def _ssd_fwd(xbcr, z, dtr, cw_ssd, hp_ssd, g_ssd):
    T = xbcr.shape[0]
    NC = T // CH

    def body(xr_ref, z_ref, dt_ref, cw_ref, hp_ref, g_ref, y_ref, yn_ref, st_ref, xp, st):
        @pl.when(pl.program_id(0) == 0)
        def _():
            xp[0:8, :] = jnp.zeros((8, XBC), F32)
            st[...] = jnp.zeros_like(st)

        cv, sg, xbc, raw, dtv, A, cs = _ssd_prep(xp, xr_ref, dt_ref, cw_ref, hp_ref)
        xp[0:8, :] = xp[CH:CH + 8, :]
        st_ref[0] = st[...]
        csT = cs.T
        E = jnp.exp(cs)
        cl = cs[CH - 1:CH, :]
        dsm = jnp.exp(cl - cs)
        El = jnp.exp(cl)
        tril = lax.broadcasted_iota(jnp.int32, (CH, CH), 0) >= lax.broadcasted_iota(jnp.int32, (CH, CH), 1)
        for g in range(NG):
            Bg = xbc[:, SI + NS * g:SI + NS * (g + 1)]
            Cg = xbc[:, SI + NG * NS + NS * g:SI + NG * NS + NS * (g + 1)]
            G = _mm_nt(Cg, Bg)
            for hh in range(HPG):
                h = g * HPG + hh
                Lm = jnp.exp(jnp.where(tril, cs[:, h:h + 1] - csT[h:h + 1, :], -1e30))
                Xh = xbc[:, HD * h:HD * (h + 1)]
                xs = Xh * dtv[:, h:h + 1]
                Sp = st[HD * h:HD * (h + 1), :]
                Y = _mm(G * Lm, xs) + _mm_nt(Cg, Sp) * E[:, h:h + 1] + hp_ref[2:3, h:h + 1] * Xh
                y_ref[:, HD * h:HD * (h + 1)] = Y
                st[HD * h:HD * (h + 1), :] = El[:, h:h + 1] * Sp + _mm_tn(xs * dsm[:, h:h + 1], Bg)
        zz = z_ref[...]
        gated = y_ref[...] * (zz * _sigmoid(zz))
        yn_ref[...] = _rms_fwd(gated, g_ref[...]).astype(BF)

    return pl.pallas_call(
        body, name="ssd_fwd", grid=(NC,),
        in_specs=[_rows(CH, XBC), _rows(CH, SI), _rows(CH, DTP), _whole((8, XBC)), _whole((8, DTP)), _whole((1, SI))],
        out_specs=[_rows(CH, SI), _rows(CH, SI), pl.BlockSpec((1, NH * HD, NS), lambda i: (i, 0, 0))],
        out_shape=[S((T, SI), F32), S((T, SI), BF), S((NC, NH * HD, NS), F32)],
        scratch_shapes=[pltpu.VMEM((CH + 8, XBC), F32), pltpu.VMEM((NH * HD, NS), F32)],
        compiler_params=_params(),
    )(xbcr, z, dtr, cw_ssd, hp_ssd, g_ssd)


def _outproj(ylru, yssd, x, wout, g_pm, g_pf):
    T = x.shape[0]

    def body(yl_ref, ys_ref, x_ref, w_hbm, gpm_ref, gpf_ref, mix_ref, x1_ref, h2_ref, w_vm, sem):
        _load_once([(w_hbm, w_vm)], sem)
        mix = (jnp.dot(yl_ref[...], w_vm[0:LW, :], preferred_element_type=F32)
               + jnp.dot(ys_ref[...], w_vm[LW:LW + SI, :], preferred_element_type=F32))
        mix_ref[...] = mix
        x1 = x_ref[...] + _rms_fwd(mix, gpm_ref[...])
        x1_ref[...] = x1
        h2_ref[...] = _rms_fwd(x1, gpf_ref[...]).astype(BF)

    return pl.pallas_call(
        body, name="outproj", grid=(T // TT,),
        in_specs=[_rows(TT, LW), _rows(TT, SI), _rows(TT, D), ANY, _whole((1, D)), _whole((1, D))],
        out_specs=[_rows(TT, D), _rows(TT, D), _rows(TT, D)],
        out_shape=[S((T, D), F32), S((T, D), F32), S((T, D), BF)],
        scratch_shapes=[pltpu.VMEM((LW + SI, D), BF), pltpu.SemaphoreType.DMA((1,))],
        compiler_params=_params(),
    )(ylru, yssd, x, wout, g_pm, g_pf)


def _ffn_fwd(h2, x1, tgt, wg, wu, wd, g_pff):
    T = x1.shape[0]

    def body(h2_ref, x1_ref, t_ref, wg_hbm, wu_hbm, wd_hbm, g_ref,
             gate_ref, up_ref, act_ref, df_ref, dx2_ref, st_ref, wg_vm, wu_vm, wd_vm, sem):
        _load_once([(wg_hbm, wg_vm), (wu_hbm, wu_vm), (wd_hbm, wd_vm)], sem)

        @pl.when(pl.program_id(0) == 0)
        def _():
            st_ref[...] = jnp.zeros_like(st_ref)

        h2 = h2_ref[...]
        gate = jnp.dot(h2, wg_vm[...], preferred_element_type=F32)
        up = jnp.dot(h2, wu_vm[...], preferred_element_type=F32)
        gate_ref[...] = gate
        up_ref[...] = up
        act = (gate * _sigmoid(gate) * up).astype(BF)
        act_ref[...] = act
        f = jnp.dot(act, wd_vm[...], preferred_element_type=F32)
        g = g_ref[...]
        x2 = x1_ref[...] + _rms_fwd(f, g)
        err = x2 - t_ref[...]
        st_ref[0:1, :] += 0.5 * jnp.sum(err * err, axis=0, keepdims=True) * (1.0 / D)
        dx2 = err * (1.0 / D)
        dx2_ref[...] = dx2
        df, dg = _rms_bwd(f, g, dx2)
        df_ref[...] = df.astype(BF)
        st_ref[1:2, :] += dg

    return pl.pallas_call(
        body, name="ffn_fwd", grid=(T // TT,),
        in_specs=[_rows(TT, D), _rows(TT, D), _rows(TT, D), ANY, ANY, ANY, _whole((1, D))],
        out_specs=[_rows(TT, DFF), _rows(TT, DFF), _rows(TT, DFF), _rows(TT, D), _rows(TT, D), _whole((8, D))],
        out_shape=[S((T, DFF), F32), S((T, DFF), F32), S((T, DFF), BF), S((T, D), BF), S((T, D), F32), S((8, D), F32)],
        scratch_shapes=[pltpu.VMEM((D, DFF), BF), pltpu.VMEM((D, DFF), BF), pltpu.VMEM((DFF, D), BF),
                        pltpu.SemaphoreType.DMA((3,))],
        compiler_params=_params(),
    )(h2, x1, tgt, wg, wu, wd, g_pff)


def _ffn_bwd(df, gate, up, wdT, wgT, wuT):
    T = df.shape[0]

    def body(df_ref, gate_ref, up_ref, wd_hbm, wg_hbm, wu_hbm, dgate_ref, dup_ref, dh2_ref, wd_vm, wg_vm, wu_vm, sem):
        _load_once([(wd_hbm, wd_vm), (wg_hbm, wg_vm), (wu_hbm, wu_vm)], sem)
        dact = jnp.dot(df_ref[...], wd_vm[...], preferred_element_type=F32)
        gate = gate_ref[...]
        s = _sigmoid(gate)
        dup = (dact * (gate * s)).astype(BF)
        dgate = (dact * up_ref[...] * (s + gate * s * (1.0 - s))).astype(BF)
        dup_ref[...] = dup
        dgate_ref[...] = dgate
        dh2_ref[...] = (jnp.dot(dgate, wg_vm[...], preferred_element_type=F32)
                        + jnp.dot(dup, wu_vm[...], preferred_element_type=F32))

    return pl.pallas_call(
        body, name="ffn_bwd", grid=(T // TT,),
        in_specs=[_rows(TT, D), _rows(TT, DFF), _rows(TT, DFF), ANY, ANY, ANY],
        out_specs=[_rows(TT, DFF), _rows(TT, DFF), _rows(TT, D)],
        out_shape=[S((T, DFF), BF), S((T, DFF), BF), S((T, D), F32)],
        scratch_shapes=[pltpu.VMEM((D, DFF), BF), pltpu.VMEM((DFF, D), BF), pltpu.VMEM((DFF, D), BF),
                        pltpu.SemaphoreType.DMA((3,))],
        compiler_params=_params(),
    )(df, gate, up, wdT, wgT, wuT)


def _mix_bwd(dh2, x1, dx2, mix, woutT, g_pf, g_pm):
    T = x1.shape[0]

    def body(dh2_ref, x1_ref, dx2_ref, mix_ref, w_hbm, gpf_ref, gpm_ref,
             dx1_ref, dmix_ref, dyl_ref, dys_ref, st_ref, w_vm, sem):
        _load_once([(w_hbm, w_vm)], sem)

        @pl.when(pl.program_id(0) == 0)
        def _():
            st_ref[...] = jnp.zeros_like(st_ref)

        dxa, dgpf = _rms_bwd(x1_ref[...], gpf_ref[...], dh2_ref[...])
        dx1 = dx2_ref[...] + dxa
        dx1_ref[...] = dx1
        dmix, dgpm = _rms_bwd(mix_ref[...], gpm_ref[...], dx1)
        dmix = dmix.astype(BF)
        dmix_ref[...] = dmix
        st_ref[0:1, :] += dgpf
        st_ref[1:2, :] += dgpm
        dyl_ref[...] = jnp.dot(dmix, w_vm[:, 0:LW], preferred_element_type=F32)
        dys_ref[...] = jnp.dot(dmix, w_vm[:, LW:LW + SI], preferred_element_type=F32)

    return pl.pallas_call(
        body, name="mix_bwd", grid=(T // TT,),
        in_specs=[_rows(TT, D), _rows(TT, D), _rows(TT, D), _rows(TT, D), ANY, _whole((1, D)), _whole((1, D))],
        out_specs=[_rows(TT, D), _rows(TT, D), _rows(TT, LW), _rows(TT, SI), _whole((8, D))],
        out_shape=[S((T, D), F32), S((T, D), BF), S((T, LW), F32), S((T, SI), F32), S((8, D), F32)],
        scratch_shapes=[pltpu.VMEM((D, LW + SI), BF), pltpu.SemaphoreType.DMA((1,))],
        compiler_params=_params(),
    )(dh2, x1, dx2, mix, woutT, g_pf, g_pm)


def _halo(width, n_tiles, tile):
    per = tile // 8
    return pl.BlockSpec((8, width), lambda i: (jnp.maximum((n_tiles - 1 - i) * per - 1, 0), 0))


def _lru_bwd(dy, lxr, lg, h, p_lru, wa4, wx4, wa4T, wx4T):
    T = dy.shape[0]
    NT = T // TT

    def body(dy_ref, lx_ref, lxh_ref, lg_ref, h_ref, hh_ref, p_ref, wa_ref, wx_ref, waT_ref, wxT_ref,
             dlx_ref, dlg_ref, st_ref, dwa_ref, dwx_ref, xp, hp, dp, a_s, d_s, g_s, cc):
        first = pl.program_id(0) == 0
        top = pl.program_id(0) == NT - 1

        @pl.when(first)
        def _():
            st_ref[...] = jnp.zeros_like(st_ref)
            dwa_ref[...] = jnp.zeros_like(dwa_ref)
            dwx_ref[...] = jnp.zeros_like(dwx_ref)
            dp[TT:TT + 8, :] = jnp.zeros((8, LW), F32)
            cc[...] = jnp.zeros_like(cc)

        keep = jnp.where(top, 0.0, 1.0)
        xp[0:8, :] = lxh_ref[...] * keep
        xp[8:8 + TT, :] = lx_ref[...]
        hp[0:8, :] = hh_ref[...] * keep
        hp[8:8 + TT, :] = h_ref[...]
        lx = _conv_from(xp, p_ref, TT)
        r, i, sp, a, mult = _lru_gates(lx, p_ref, wa_ref, wx_ref)

        lg = lg_ref[...]
        hcur = h_ref[...]
        ge = _gelu(lg)
        dgated, dgn = _rms_bwd(hcur * ge, p_ref[8:9, :], dy_ref[...])
        st_ref[8:9, :] += dgn
        dlg_ref[...] = (dgated * hcur * _gelu_grad(lg)).astype(BF)
        a_s[...] = a
        d_s[...] = dgated * ge

        def step(k, c):
            t = TT - 1 - k
            g = d_s[pl.ds(t, 1), :] + c
            g_s[pl.ds(t, 1), :] = g
            return a_s[pl.ds(t, 1), :] * g

        cc[0:1, :] = lax.fori_loop(0, TT, step, cc[0:1, :], unroll=8)
        gt = g_s[...]
        da = gt * hp[pl.ds(7, TT), :]
        dmult = gt * i * lx
        di = gt * mult * lx
        dlxc = gt * mult * i
        dla = da * a - dmult * (a * a) / mult
        dr = dla * (-LRU_C * sp)
        st_ref[7:8, :] += jnp.sum(dla * (-LRU_C * r), axis=0, keepdims=True) * (-_sigmoid(-p_ref[7:8, :]))
        dzr = dr * r * (1.0 - r)
        dzi = di * i * (1.0 - i)
        st_ref[5:6, :] += jnp.sum(dzr, axis=0, keepdims=True)
        st_ref[6:7, :] += jnp.sum(dzi, axis=0, keepdims=True)
        dlxc = dlxc + _blockdiag_mm(dzr, waT_ref) + _blockdiag_mm(dzi, wxT_ref)
        for j in range(4):
            sl = slice(256 * j, 256 * (j + 1))
            dwa_ref[j] += _mm_tn(lx[:, sl], dzr[:, sl])
            dwx_ref[j] += _mm_tn(lx[:, sl], dzi[:, sl])
        dp[0:TT, :] = dlxc
        acc = p_ref[0:1, :] * dp[pl.ds(CONV_K - 1, TT), :]
        for k in range(1, CONV_K):
            acc = acc + p_ref[k:k + 1, :] * dp[pl.ds(CONV_K - 1 - k, TT), :]
        dlx_ref[...] = acc.astype(BF)
        dp[TT:TT + 8, :] = dp[0:8, :]
        for k in range(CONV_K):
            st_ref[k:k + 1, :] += jnp.sum(dlxc * xp[pl.ds(8 - CONV_K + 1 + k, TT), :], axis=0, keepdims=True)
        st_ref[4:5, :] += jnp.sum(dlxc, axis=0, keepdims=True)

    w4 = _whole((4, 256, 256))
    return pl.pallas_call(
        body, name="lru_bwd", grid=(NT,),
        in_specs=[_rows(TT, LW, NT), _rows(TT, LW, NT), _halo(LW, NT, TT), _rows(TT, LW, NT), _rows(TT, LW, NT),
                  _halo(LW, NT, TT), _whole((16, LW)), w4, w4, w4, w4],
        out_specs=[_rows(TT, LW, NT), _rows(TT, LW, NT), _whole((16, LW)), w4, w4],
        out_shape=[S((T, LW), BF), S((T, LW), BF), S((16, LW), F32), S((4, 256, 256), F32), S((4, 256, 256), F32)],
        scratch_shapes=[pltpu.VMEM((TT + 8, LW), F32), pltpu.VMEM((TT + 8, LW), F32), pltpu.VMEM((TT + 8, LW), F32),
                        pltpu.VMEM((TT, LW), F32), pltpu.VMEM((TT, LW), F32), pltpu.VMEM((TT, LW), F32),
                        pltpu.VMEM((8, LW), F32)],
        compiler_params=_params(),
    )(dy, lxr, lxr, lg, h, h, p_lru, wa4, wx4, wa4T, wx4T)


def _ssd_bwd(dyn, xbcr, z, dtr, y, states, cw_ssd, hp_ssd, g_ssd):
    T = dyn.shape[0]
    NC = T // CH

    def body(dyn_ref, xr_ref, xh_ref, z_ref, dt_ref, y_ref, st_ref, cw_ref, hp_ref, g_ref,
             dxbc_ref, dz_ref, ddt_ref, cst_ref, hst_ref, gst_ref, xp, dp, dS, dxb):
        first = pl.program_id(0) == 0
        top = pl.program_id(0) == NC - 1

        @pl.when(first)
        def _():
            cst_ref[...] = jnp.zeros_like(cst_ref)
            hst_ref[...] = jnp.zeros_like(hst_ref)
            gst_ref[...] = jnp.zeros_like(gst_ref)
            dp[CH:CH + 8, :] = jnp.zeros((8, XBC), F32)
            dS[...] = jnp.zeros_like(dS)

        xp[0:8, :] = xh_ref[...] * jnp.where(top, 0.0, 1.0)
        cv, sg, xbc, raw, dtv, A, cs = _ssd_prep(xp, xr_ref, dt_ref, cw_ref, hp_ref)
        csT = cs.T
        E = jnp.exp(cs)
        cl = cs[CH - 1:CH, :]
        dsm = jnp.exp(cl - cs)
        El = jnp.exp(cl)
        row_i = lax.broadcasted_iota(jnp.int32, (CH, CH), 0)
        col_i = lax.broadcasted_iota(jnp.int32, (CH, CH), 1)
        tril = row_i >= col_i

        zz = z_ref[...]
        sz = _sigmoid(zz)
        yv = y_ref[...]
        dgn, dg = _rms_bwd(yv * (zz * sz), g_ref[...], dyn_ref[...])
        gst_ref[0:1, :] += dg
        dz_ref[...] = (dgn * yv * (sz + zz * sz * (1.0 - sz))).astype(BF)
        dY = dgn * (zz * sz)

        dcs_col = jnp.zeros((CH, DTP), F32)
        dcs_row = jnp.zeros((CH, DTP), F32)
        ddt_col = jnp.zeros((CH, DTP), F32)
        dD = jnp.zeros((1, DTP), F32)
        lane1 = col_i[0:1, :]
        last_row = row_i == CH - 1
        for g in range(NG):
            Bg = xbc[:, SI + NS * g:SI + NS * (g + 1)]
            Cg = xbc[:, SI + NG * NS + NS * g:SI + NG * NS + NS * (g + 1)]
            G = _mm_nt(Cg, Bg)
            dG = jnp.zeros((CH, CH), F32)
            dBg = jnp.zeros((CH, NS), F32)
            dCg = jnp.zeros((CH, NS), F32)
            for hh in range(HPG):
                h = g * HPG + hh
                hs = slice(HD * h, HD * (h + 1))
                Lm = jnp.exp(jnp.where(tril, cs[:, h:h + 1] - csT[h:h + 1, :], -1e30))
                M = G * Lm
                Xh = xbc[:, hs]
                dtc = dtv[:, h:h + 1]
                xs = Xh * dtc
                dYh = dY[:, hs]
                Dh = hp_ref[2:3, h:h + 1]
                dD = dD + jnp.where(lane1 == h, _sum_all(dYh * Xh), 0.0)
                Sp = st_ref[0, hs, :]
                Ec = E[:, h:h + 1]
                Yo = _mm_nt(Cg, Sp) * Ec
                dcs_h = jnp.sum(dYh * Yo, axis=1, keepdims=True)
                dP = dYh * Ec
                dCg = dCg + _mm(dP, Sp)
                dSp = _mm_tn(dP, Cg)
                dSe = dS[hs, :]
                elh = El[:, h:h + 1]
                dSp = dSp + elh * dSe
                dcl = _sum_all(dSe * Sp) * elh
                Q = _mm_nt(Bg, dSe)
                dsc = dsm[:, h:h + 1]
                dxs = Q * dsc
                dds = jnp.sum(Q * xs, axis=1, keepdims=True) * dsc
                dBg = dBg + _mm(xs * dsc, dSe)
                dcs_h = dcs_h - dds
                dcl = dcl + jnp.sum(dds, axis=0, keepdims=True)
                dM = _mm_nt(dYh, xs)
                dxs = dxs + _mm_tn(M, dYh)
                Wm = dM * M
                dcs_h = dcs_h + jnp.sum(Wm, axis=1, keepdims=True)
                dcs_h = dcs_h + jnp.where(last_row[:, 0:1], dcl, 0.0)
                dcs_col = dcs_col + jnp.where(col_i == h, dcs_h, 0.0)
                dcs_row = dcs_row + jnp.where(row_i == h, -jnp.sum(Wm, axis=0, keepdims=True), 0.0)
                dG = dG + dM * Lm
                dxb[:, hs] = Dh * dYh + dxs * dtc
                ddt_col = ddt_col + jnp.where(col_i == h, jnp.sum(dxs * Xh, axis=1, keepdims=True), 0.0)
                dS[hs, :] = dSp
            dxb[:, SI + NS * g:SI + NS * (g + 1)] = dBg + _mm_tn(dG, Cg)
            dxb[:, SI + NG * NS + NS * g:SI + NG * NS + NS * (g + 1)] = dCg + _mm(dG, Bg)

        da = _rev_cumsum_rows(dcs_col + dcs_row.T, CH)
        ddt_col = ddt_col + da * A
        hst_ref[1:2, :] += jnp.sum(da * dtv, axis=0, keepdims=True) * A
        hst_ref[2:3, :] += dD
        draw = jnp.where(col_i < NH, ddt_col * _sigmoid(raw), 0.0)
        ddt_ref[...] = draw.astype(BF)
        hst_ref[0:1, :] += jnp.sum(draw, axis=0, keepdims=True)

        dcv = dxb[...] * (sg + cv * sg * (1.0 - sg))
        dp[0:CH, :] = dcv
        acc = cw_ref[0:1, :] * dp[pl.ds(CONV_K - 1, CH), :]
        for k in range(1, CONV_K):
            acc = acc + cw_ref[k:k + 1, :] * dp[pl.ds(CONV_K - 1 - k, CH), :]
        dxbc_ref[...] = acc.astype(BF)
        dp[CH:CH + 8, :] = dp[0:8, :]
        for k in range(CONV_K):
            cst_ref[k:k + 1, :] += jnp.sum(dcv * xp[pl.ds(8 - CONV_K + 1 + k, CH), :], axis=0, keepdims=True)
        cst_ref[4:5, :] += jnp.sum(dcv, axis=0, keepdims=True)

    return pl.pallas_call(
        body, name="ssd_bwd", grid=(NC,),
        in_specs=[_rows(CH, SI, NC), _rows(CH, XBC, NC), _halo(XBC, NC, CH), _rows(CH, SI, NC), _rows(CH, DTP, NC),
                  _rows(CH, SI, NC), pl.BlockSpec((1, NH * HD, NS), lambda i: (NC - 1 - i, 0, 0)),
                  _whole((8, XBC)), _whole((8, DTP)), _whole((1, SI))],
        out_specs=[_rows(CH, XBC, NC), _rows(CH, SI, NC), _rows(CH, DTP, NC), _whole((8, XBC)), _whole((8, DTP)),
                   _whole((8, SI))],
        out_shape=[S((T, XBC), BF), S((T, SI), BF), S((T, DTP), BF), S((8, XBC), F32), S((8, DTP), F32), S((8, SI), F32)],
        scratch_shapes=[pltpu.VMEM((CH + 8, XBC), F32), pltpu.VMEM((CH + 8, XBC), F32), pltpu.VMEM((NH * HD, NS), F32),
                        pltpu.VMEM((CH, XBC), F32)],
        compiler_params=_params(),
    )(dyn, xbcr, xbcr, z, dtr, y, states, cw_ssd, hp_ssd, g_ssd)


def _inproj_bwd(dlx, dlg, dz, dxbc, ddt, x, dx1, wcatT, g0):
    T = x.shape[0]

    def body(dlx_ref, dlg_ref, dz_ref, dxbc_ref, ddt_ref, x_ref, dx1_ref, w_hbm, g_ref, dx_ref, st_ref, w_vm, sem):
        _load_once([(w_hbm, w_vm)], sem)

        @pl.when(pl.program_id(0) == 0)
        def _():
            st_ref[...] = jnp.zeros_like(st_ref)

        dh = jnp.dot(dlx_ref[...], w_vm[0:1024, :], preferred_element_type=F32)
        dh = dh + jnp.dot(dlg_ref[...], w_vm[1024:2048, :], preferred_element_type=F32)
        dh = dh + jnp.dot(dz_ref[...], w_vm[2048:3072, :], preferred_element_type=F32)
        dh = dh + jnp.dot(dxbc_ref[...], w_vm[3072:3072 + XBC, :], preferred_element_type=F32)
        dh = dh + jnp.dot(ddt_ref[...], w_vm[3072 + XBC:PC, :], preferred_element_type=F32)
        dx, dg = _rms_bwd(x_ref[...], g_ref[...], dh)
        dx_ref[...] = dx1_ref[...] + dx
        st_ref[0:1, :] += dg

    return pl.pallas_call(
        body, name="inproj_bwd", grid=(T // TT,),
        in_specs=[_rows(TT, 1024), _rows(TT, 1024), _rows(TT, 1024), _rows(TT, XBC), _rows(TT, DTP), _rows(TT, D),
                  _rows(TT, D), ANY, _whole((1, D))],
        out_specs=[_rows(TT, D), _whole((8, D))],
        out_shape=[S((T, D), F32), S((8, D), F32)],
        scratch_shapes=[pltpu.VMEM((PC, D), BF), pltpu.SemaphoreType.DMA((1,))],
        compiler_params=_params(),
    )(dlx, dlg, dz, dxbc, ddt, x, dx1, wcatT, g0)


def _wgrad(name, a, bs):
    T, M = a.shape
    nb = len(bs)
    widths = [b.shape[1] for b in bs]
    offs = [sum(widths[:j]) for j in range(nb)]
    nk = T // TK

    def body(*refs):
        a_ref = refs[0]
        b_refs = refs[1:1 + nb]
        o_ref, acc, sem = refs[1 + nb], refs[2 + nb], refs[3 + nb]
        k = pl.program_id(0)
        av = a_ref[...]
        for b_ref, off, w in zip(b_refs, offs, widths):
            p = lax.dot_general(av, b_ref[...], (((0,), (0,)), ((), ())), preferred_element_type=F32)

            @pl.when(k == 0)
            def _():
                acc[:, off:off + w] = p

            @pl.when(k > 0)
            def _():
                acc[:, off:off + w] += p

        @pl.when(k == nk - 1)
        def _():
            cp = pltpu.make_async_copy(acc, o_ref, sem)
            cp.start()
            cp.wait()

    return pl.pallas_call(
        body, name=name, grid=(nk,),
        in_specs=[_rows(TK, M)] + [_rows(TK, w) for w in widths],
        out_specs=ANY, out_shape=S((M, sum(widths)), F32),
        scratch_shapes=[pltpu.VMEM((M, sum(widths)), F32), pltpu.SemaphoreType.DMA],
        compiler_params=_params(),
    )(a, *bs)


def _adamw(name, w, g, m, v):
    R, C = w.shape
    tr = _row_tile(R, C)
    c1 = 1.0 - ADAM_B1 ** ADAM_STEP
    c2 = 1.0 - ADAM_B2 ** ADAM_STEP

    def body(w_ref, g_ref, m_ref, v_ref, d_ref, nm_ref, nv_ref):
        gg = g_ref[...]
        mm = ADAM_B1 * m_ref[...] + (1.0 - ADAM_B1) * gg
        vv = ADAM_B2 * v_ref[...] + (1.0 - ADAM_B2) * (gg * gg)
        nm_ref[...] = mm
        nv_ref[...] = vv
        d_ref[...] = -ADAM_LR * ((mm / c1) / (jnp.sqrt(vv / c2) + ADAM_EPS) + ADAM_WD * w_ref[...])

    blk = pl.BlockSpec((tr, C), lambda i: (i, 0))
    return pl.pallas_call(
        body, name=name, grid=(R // tr,),
        in_specs=[blk] * 4, out_specs=[blk] * 3, out_shape=[S((R, C), F32)] * 3,
        compiler_params=_params(),
    )(w, g, m, v)


def _pos():
    return lax.axis_index("x"), lax.axis_index("y"), lax.axis_index("c")


def _other_chips(x, y):
    return [(1 - x, y), (x, 1 - y), (1 - x, 1 - y)]


def _half(ref, c, hr):
    sl = pl.ds(pl.multiple_of(c * hr, 8), hr)
    return ref.at[:, sl, :] if len(ref.shape) == 3 else ref.at[sl, :]


def _remote(src, dst, send_sem, recv_sem, to):
    return pltpu.make_async_remote_copy(src_ref=src, dst_ref=dst, send_sem=send_sem, recv_sem=recv_sem,
                                        device_id=to, device_id_type=MESH)


def _allgather_weights(shards):
    n = len(shards)

    def body(*refs):
        ins, outs = refs[:n], refs[n:2 * n]
        send_sems, recv_sems = refs[2 * n], refs[2 * n + 1]
        x, y, c = _pos()
        me = 2 * x + y
        chips = _other_chips(x, y)
        first, passed = [], []
        for i, (src, dst) in enumerate(zip(ins, outs)):
            hr = src.shape[0] // 2
            my_half = pl.ds(pl.multiple_of(c * hr, 16), hr)
            for k, (cx, cy) in enumerate(chips):
                cp = _remote(src.at[my_half, :], dst.at[me, my_half, :], send_sems.at[6 * i + k], recv_sems.at[6 * i + k],
                             (cx, cy, c))
                cp.start()
                first.append(cp)
        for i, (src, dst) in enumerate(zip(ins, outs)):
            hr = src.shape[0] // 2
            my_half = pl.ds(pl.multiple_of(c * hr, 16), hr)
            for k, (cx, cy) in enumerate(chips):
                blk = dst.at[2 * cx + cy, my_half, :]
                _remote(blk, blk, send_sems.at[6 * i + k], recv_sems.at[6 * i + k], (cx, cy, c)).wait_recv()
                fw = _remote(blk, blk, send_sems.at[6 * i + 3 + k], recv_sems.at[6 * i + 3 + k], (x, y, 1 - c))
                fw.start()
                passed.append(fw)
        for i, (src, dst) in enumerate(zip(ins, outs)):
            hr = src.shape[0] // 2
            sib_half = pl.ds(pl.multiple_of((1 - c) * hr, 16), hr)
            for k, (cx, cy) in enumerate(chips):
                blk = dst.at[2 * cx + cy, sib_half, :]
                _remote(blk, blk, send_sems.at[6 * i + 3 + k], recv_sems.at[6 * i + 3 + k], (x, y, 1 - c)).wait_recv()
        for cp in first + passed:
            cp.wait_send()

    return pl.pallas_call(
        body, name="allgather_weights", in_specs=[ANY] * n, out_specs=[ANY] * n,
        out_shape=[S((4,) + s.shape, s.dtype) for s in shards],
        scratch_shapes=[pltpu.SemaphoreType.DMA((6 * n,)), pltpu.SemaphoreType.DMA((6 * n,))],
    )(*shards)


def _pair_exchange(bufs):
    n = len(bufs)

    def half_shape(b):
        return b.shape[:-2] + (b.shape[-2] // 2, b.shape[-1])

    def body(*refs):
        ins, outs = refs[:n], refs[n:2 * n]
        send_sems, recv_sems = refs[2 * n], refs[2 * n + 1]
        x, y, c = _pos()
        copies = [_remote(_half(src, 1 - c, src.shape[-2] // 2), dst, send_sems.at[k], recv_sems.at[k], (x, y, 1 - c))
                  for k, (src, dst) in enumerate(zip(ins, outs))]
        for cp in copies:
            cp.start()
        for cp in copies:
            cp.wait()

    return pl.pallas_call(
        body, name="pair_exchange", in_specs=[ANY] * n, out_specs=[ANY] * n,
        out_shape=[S(half_shape(b), b.dtype) for b in bufs],
        scratch_shapes=[pltpu.SemaphoreType.DMA((n,)), pltpu.SemaphoreType.DMA((n,))],
    )(*bufs)


def _quad_exchange(bufs, scatter):
    n = len(bufs)

    def body(*refs):
        ins, outs = refs[:n], refs[n:2 * n]
        send_sems, recv_sems = refs[2 * n], refs[2 * n + 1]
        x, y, c = _pos()
        me = 2 * x + y
        chips = _other_chips(x, y)
        copies = []
        for k, (src, dst) in enumerate(zip(ins, outs)):
            for j, (cx, cy) in enumerate(chips):
                piece = src.at[2 * cx + cy] if scatter[k] else src
                cp = _remote(piece, dst.at[me], send_sems.at[3 * k + j], recv_sems.at[3 * k + j], (cx, cy, c))
                cp.start()
                copies.append(cp)
        for k, (src, dst) in enumerate(zip(ins, outs)):
            for j, (cx, cy) in enumerate(chips):
                blk = dst.at[2 * cx + cy]
                _remote(blk, blk, send_sems.at[3 * k + j], recv_sems.at[3 * k + j], (cx, cy, c)).wait_recv()
        for cp in copies:
            cp.wait_send()

    return pl.pallas_call(
        body, name="quad_exchange", in_specs=[ANY] * n, out_specs=[ANY] * n,
        out_shape=[S((4,) + (b.shape[1:] if sc else b.shape), b.dtype) for b, sc in zip(bufs, scatter)],
        scratch_shapes=[pltpu.SemaphoreType.DMA((3 * n,)), pltpu.SemaphoreType.DMA((3 * n,))],
    )(*bufs)


def _pair_gather(bufs):
    n = len(bufs)

    def body(*refs):
        ins, outs = refs[:n], refs[n:2 * n]
        send_sems, recv_sems = refs[2 * n], refs[2 * n + 1]
        x, y, c = _pos()
        copies = []
        for k, buf in enumerate(outs):
            mine = _half(buf, c, buf.shape[0] // 2)
            cp = _remote(mine, mine, send_sems.at[k], recv_sems.at[k], (x, y, 1 - c))
            cp.start()
            copies.append(cp)
        for k, buf in enumerate(outs):
            theirs = _half(buf, 1 - c, buf.shape[0] // 2)
            _remote(theirs, theirs, send_sems.at[k], recv_sems.at[k], (x, y, 1 - c)).wait_recv()
        for cp in copies:
            cp.wait_send()

    return pl.pallas_call(
        body, name="pair_gather", in_specs=[ANY] * n, out_specs=[ANY] * n,
        out_shape=[S(b.shape, b.dtype) for b in bufs], input_output_aliases={k: k for k in range(n)},
        scratch_shapes=[pltpu.SemaphoreType.DMA((n,)), pltpu.SemaphoreType.DMA((n,))],
    )(*bufs)


def _row_tile(rows, cols, mult=8):
    best = mult
    for t in range(mult, rows + 1, mult):
        if rows % t == 0 and t * cols * 4 <= (1 << 20):
            best = t
    return best


def _add_own_half(name, full, got, c, out_dtype):
    three = len(full.shape) == 3
    lead = full.shape[0] if three else 1
    rows, cols = full.shape[-2], full.shape[-1]
    hr = rows // 2
    tr = _row_tile(hr, cols, 16 if out_dtype == jnp.bfloat16 else 8)
    per = hr // tr

    def body(c_ref, a_ref, b_ref, o_ref):
        o_ref[...] = (a_ref[...] + b_ref[...]).astype(out_dtype)

    if three:
        a_spec = pl.BlockSpec((1, tr, cols), lambda s, i, c_ref: (s, c_ref[0] * per + i, 0))
        o_spec = pl.BlockSpec((1, tr, cols), lambda s, i, c_ref: (s, i, 0))
    else:
        a_spec = pl.BlockSpec((tr, cols), lambda s, i, c_ref: (c_ref[0] * per + i, 0))
        o_spec = pl.BlockSpec((tr, cols), lambda s, i, c_ref: (i, 0))
    return pl.pallas_call(
        body, name=name,
        grid_spec=pltpu.PrefetchScalarGridSpec(num_scalar_prefetch=1, grid=(lead, per), in_specs=[a_spec, o_spec],
                                               out_specs=o_spec),
        out_shape=S(got.shape, out_dtype), compiler_params=_params(2),
    )(jnp.reshape(c, (1,)).astype(jnp.int32), full, got)


def _sum_slots(name, own, slots, me, c):
    _, rows, cols = slots.shape
    tr = _row_tile(rows, cols, 16 if slots.dtype == jnp.bfloat16 else 8)
    per = rows // tr
    three = len(own.shape) == 3

    def body(p_ref, own_ref, s0, s1, s2, s3, o_ref):
        mine = own_ref[0] if three else own_ref[...]
        acc = None
        for j, s_ref in enumerate((s0, s1, s2, s3)):
            v = jnp.where(p_ref[0] == j, mine, s_ref[0]).astype(F32)
            acc = v if acc is None else acc + v
        o_ref[...] = acc

    def slot_spec(j):
        return pl.BlockSpec((1, tr, cols), lambda i, p: (jnp.where(p[0] == j, (j + 1) % 4, j), i, 0))

    own_spec = (pl.BlockSpec((1, tr, cols), lambda i, p: (p[0], i, 0)) if three
                else pl.BlockSpec((tr, cols), lambda i, p: (i, 0)))
    return pl.pallas_call(
        body, name=name,
        grid_spec=pltpu.PrefetchScalarGridSpec(
            num_scalar_prefetch=1, grid=(per,), in_specs=[own_spec] + [slot_spec(j) for j in range(4)],
            out_specs=pl.BlockSpec((tr, cols), lambda i, p: (p[1] * per + i, 0))),
        out_shape=S((2 * rows, cols), F32), compiler_params=_params(),
    )(jnp.stack([me, c]).astype(jnp.int32), own, slots, slots, slots, slots)


SMALL = (("pre_mix_norm", 1024), ("lru_conv_w", 4096), ("lru_conv_b", 1024), ("lru_wa", 65536), ("lru_ba", 1024),
         ("lru_wx", 65536), ("lru_bx", 1024), ("lru_lambda", 1024), ("lru_out_norm", 1024), ("ssd_conv_w", 6144),
         ("ssd_conv_b", 1536), ("ssd_dt_bias", 16), ("ssd_a_log", 16), ("ssd_d", 16), ("ssd_out_norm", 1024),
         ("post_mix_norm", 1024), ("pre_ffn_norm", 1024), ("post_ffn_norm", 1024))
SMALL_ROWS = 1200
BIG = ("w_in", "w_out", "w_gate", "w_up", "w_down")


def _diag4(w):
    eye = jnp.eye(4, dtype=w.dtype).reshape(1, 4, 1, 4, 1)
    return (w.reshape(4, 4, BW, 1, BW) * eye).reshape(4, 4 * BW, 4 * BW)


def _undiag4(w4):
    w4 = w4.reshape(4, 4, BW, 4, BW)
    return jnp.stack([w4[:, a, :, a, :] for a in range(4)], axis=1).reshape(NBLK, BW, BW)


def _pack_small(parts, loss):
    flat = jnp.concatenate([parts[name].reshape(-1).astype(F32) for name, _ in SMALL])
    flat = jnp.pad(flat, (0, SMALL_ROWS * 128 - 1 - flat.shape[0]))
    return jnp.concatenate([flat, loss.reshape(1)]).reshape(SMALL_ROWS, 128)


def _unpack_small(buf):
    flat = buf.reshape(-1)
    out, off = {}, 0
    for name, size in SMALL:
        out[name] = flat[off:off + size]
        off += size
    return out


def _gather_weights(w_in, w_out, w_gate, w_up, w_down, lru_conv_w, ssd_conv_w):
    conv = jnp.concatenate([lru_conv_w.reshape(-1), ssd_conv_w.reshape(-1)]).astype(F32)
    hi = conv.astype(jnp.bfloat16)
    mid = (conv - hi.astype(F32)).astype(jnp.bfloat16)
    lo = (conv - hi.astype(F32) - mid.astype(F32)).astype(jnp.bfloat16)
    terms = jnp.concatenate([hi, mid, lo])
    n_terms = terms.shape[0]
    conv_rows = -(-n_terms // (128 * 32)) * 32
    terms = jnp.pad(terms, (0, conv_rows * 128 - n_terms)).reshape(conv_rows, 128)
    own = [w.astype(jnp.bfloat16) for w in (w_in, w_out, w_gate, w_up, w_down)] + [terms]
    got = _allgather_weights(own)
    chip = 2 * lax.axis_index("x") + lax.axis_index("y")
    here = (jnp.arange(4) == chip).reshape(4, 1, 1)
    full = [jnp.where(here, o[None], g) for o, g in zip(own, got)]
    cols = lambda f: f.transpose(1, 0, 2).reshape(f.shape[1], 4 * f.shape[2])
    rows = lambda f: f.reshape(4 * f.shape[1], f.shape[2])
    win_f, wout_f, wg_f, wu_f, wd_f = cols(full[0]), rows(full[1]), cols(full[2]), cols(full[3]), rows(full[4])
    t3 = full[5].reshape(4, -1)[:, :n_terms].reshape(4, 3, -1).astype(F32)
    conv_f = (t3[:, 0] + t3[:, 1]) + t3[:, 2]
    n1 = lru_conv_w.size
    lcw = conv_f[:, :n1].reshape(4, CONV_K, -1).transpose(1, 0, 2).reshape(CONV_K, LW)
    scw = conv_f[:, n1:].reshape(4, CONV_K, -1).transpose(1, 0, 2).reshape(CONV_K, XBC)
    return win_f, wout_f, wg_f, wu_f, wd_f, lcw, scw


def _local_step(x, tgt, win_f, wout_f, wg_f, wu_f, wd_f, lcw, scw, sp):
    mm = lambda w: w.astype(BF)
    wcat = jnp.concatenate([mm(win_f), jnp.zeros((D, PC - IN_COLS), BF)], axis=1)
    row = lambda v: v.reshape(1, -1).astype(F32)
    p_lru = jnp.concatenate([lcw, row(sp["lru_conv_b"]), row(sp["lru_ba"]), row(sp["lru_bx"]), row(sp["lru_lambda"]),
                             row(sp["lru_out_norm"]), jnp.zeros((7, LW), F32)], axis=0)
    wa4, wx4 = mm(_diag4(sp["lru_wa"][0])), mm(_diag4(sp["lru_wx"][0]))
    wa4T, wx4T = wa4.transpose(0, 2, 1), wx4.transpose(0, 2, 1)
    cw_ssd = jnp.concatenate([scw, row(sp["ssd_conv_b"]), jnp.zeros((3, XBC), F32)], axis=0)
    padh = lambda v: jnp.pad(row(v), ((0, 0), (0, DTP - NH)))
    hp_ssd = jnp.concatenate([padh(sp["ssd_dt_bias"]), padh(sp["ssd_a_log"]), padh(sp["ssd_d"]), jnp.zeros((5, DTP), F32)], axis=0)
    g0, g_ssd = row(sp["pre_mix_norm"]), row(sp["ssd_out_norm"])
    g_pm, g_pf, g_pff = row(sp["post_mix_norm"]), row(sp["pre_ffn_norm"]), row(sp["post_ffn_norm"])
    wout, wg, wu, wd = mm(wout_f), mm(wg_f), mm(wu_f), mm(wd_f)

    h0, lxr, lg, z, xbcr, dtr = _inproj(x, g0, wcat)
    h, ylru = _lru_fwd(lxr, lg, p_lru, wa4, wx4)
    y, yssd, states = _ssd_fwd(xbcr, z, dtr, cw_ssd, hp_ssd, g_ssd)
    mix, x1, h2 = _outproj(ylru, yssd, x, wout, g_pm, g_pf)
    gate, up, act, df, dx2, st_ffn = _ffn_fwd(h2, x1, tgt, wg, wu, wd, g_pff)
    dgate, dup, dh2 = _ffn_bwd(df, gate, up, wd.T, wg.T, wu.T)
    dx1, dmix, dyl, dys, st_mix = _mix_bwd(dh2, x1, dx2, mix, wout.T, g_pf, g_pm)
    dlx, dlg, st_lru, dwa4, dwx4 = _lru_bwd(dyl, lxr, lg, h, p_lru, wa4, wx4, wa4T, wx4T)
    dxbc, dz, ddt, cst, hst, gst = _ssd_bwd(dys, xbcr, z, dtr, y, states, cw_ssd, hp_ssd, g_ssd)
    gx, st_in = _inproj_bwd(dlx, dlg, dz, dxbc, ddt, x, dx1, wcat.T, g0)

    dwd = _wgrad("wgrad_down", act, [df])
    dwgu = _wgrad("wgrad_gate_up", h2, [dgate, dup])
    dwo_l = _wgrad("wgrad_out_lru", ylru, [dmix])
    dwo_s = _wgrad("wgrad_out_ssd", yssd, [dmix])
    dwin = _wgrad("wgrad_in", h0, [dlx, dlg, dz, dxbc, ddt])[:, :IN_COLS]
    big = {"w_in": dwin, "w_out": jnp.concatenate([dwo_l, dwo_s], axis=0), "w_gate": dwgu[:, :DFF], "w_up": dwgu[:, DFF:],
           "w_down": dwd}
    small = {
        "pre_mix_norm": st_in[0], "lru_conv_w": st_lru[0:4], "lru_conv_b": st_lru[4], "lru_wa": _undiag4(dwa4),
        "lru_ba": st_lru[5], "lru_wx": _undiag4(dwx4), "lru_bx": st_lru[6], "lru_lambda": st_lru[7],
        "lru_out_norm": st_lru[8], "ssd_conv_w": cst[0:4], "ssd_conv_b": cst[4], "ssd_dt_bias": hst[0, :NH],
        "ssd_a_log": hst[1, :NH], "ssd_d": hst[2, :NH], "ssd_out_norm": gst[0], "post_mix_norm": st_mix[1],
        "pre_ffn_norm": st_mix[0], "post_ffn_norm": st_ffn[1],
    }
    return jnp.sum(st_ffn[0]), gx, big, small


def _reduce_grads(big, small_buf):
    c = lax.axis_index("c")
    bufs = []
    for name in BIG:
        g = big[name]
        if name in ("w_in", "w_gate", "w_up"):
            bufs.append(g.reshape(g.shape[0], 4, g.shape[1] // 4).transpose(1, 0, 2))
        else:
            bufs.append(g.reshape(4, g.shape[0] // 4, g.shape[1]))
    bufs.append(small_buf)
    me = 2 * lax.axis_index("x") + lax.axis_index("y")
    got = _pair_exchange(bufs)
    wire = [jnp.bfloat16] * len(BIG) + [F32]
    part = [_add_own_half("pair_add_%d" % k, b, r, c, dt) for k, (b, r, dt) in enumerate(zip(bufs, got, wire))]
    slots = _quad_exchange(part, [True] * len(BIG) + [False])
    red = [_sum_slots("quad_sum_%d" % k, p, s, me, c) for k, (p, s) in enumerate(zip(part, slots))]
    return _pair_gather(red)


def kernel(x, pre_mix_norm, w_in, lru_conv_w, lru_conv_b, lru_wa, lru_ba, lru_wx, lru_bx, lru_lambda, lru_out_norm, ssd_conv_w, ssd_conv_b, ssd_dt_bias, ssd_a_log, ssd_d, ssd_out_norm, w_out, post_mix_norm, pre_ffn_norm, w_gate, w_up, w_down, post_ffn_norm, loss_target, m_pre_mix_norm, m_w_in, m_lru_conv_w, m_lru_conv_b, m_lru_wa, m_lru_ba, m_lru_wx, m_lru_bx, m_lru_lambda, m_lru_out_norm, m_ssd_conv_w, m_ssd_conv_b, m_ssd_dt_bias, m_ssd_a_log, m_ssd_d, m_ssd_out_norm, m_w_out, m_post_mix_norm, m_pre_ffn_norm, m_w_gate, m_w_up, m_w_down, m_post_ffn_norm, v_pre_mix_norm, v_w_in, v_lru_conv_w, v_lru_conv_b, v_lru_wa, v_lru_ba, v_lru_wx, v_lru_bx, v_lru_lambda, v_lru_out_norm, v_ssd_conv_w, v_ssd_conv_b, v_ssd_dt_bias, v_ssd_a_log, v_ssd_d, v_ssd_out_norm, v_w_out, v_post_mix_norm, v_pre_ffn_norm, v_w_gate, v_w_up, v_w_down, v_post_ffn_norm):
    args = dict(locals())
    names = [n for n, _ in SMALL] + list(BIG)
    w = {n: args[n] for n in names}
    m = {n: args["m_" + n] for n in names}
    v = {n: args["v_" + n] for n in names}
    chip = 2 * lax.axis_index("x") + lax.axis_index("y")

    win_f, wout_f, wg_f, wu_f, wd_f, lcw, scw = _gather_weights(w_in[0], w_out[0], w_gate[0], w_up[0], w_down[0],
                                                                lru_conv_w[0], ssd_conv_w[0])
    sp = {n: w[n] for n, _ in SMALL}
    loss_part, gx, big, small = _local_step(x[0], loss_target[0], win_f, wout_f, wg_f, wu_f, wd_f, lcw, scw, sp)
    red = _reduce_grads(big, _pack_small(small, loss_part))
    gsmall = _unpack_small(red[len(BIG)])
    loss = red[len(BIG)][SMALL_ROWS - 1, 127]

    grads, delta, new_m, new_v = {}, {}, {}, {}
    for k, n in enumerate(BIG):
        g = red[k]
        d, nm, nv = _adamw("adamw_" + n, w[n][0], g, m[n][0], v[n][0])
        grads[n], delta[n], new_m[n], new_v[n] = g[None], d[None], nm[None], nv[None]

    def local_part(n, full):
        shape = w[n].shape
        if n in ("lru_conv_w", "ssd_conv_w"):
            per = shape[-1]
            return lax.dynamic_slice_in_dim(full.reshape(CONV_K, -1), chip * per, per, axis=1).reshape(shape)
        return full.reshape(shape)

    gl = {n: local_part(n, gsmall[n]) for n, _ in SMALL}

    def pack_local(d):
        flat = jnp.concatenate([d[n].reshape(-1) for n, _ in SMALL])
        rows = -(-flat.shape[0] // (128 * 8)) * 8
        return jnp.pad(flat, (0, rows * 128 - flat.shape[0])).reshape(rows, 128), flat.shape[0]

    wp, nflat = pack_local({n: w[n] for n, _ in SMALL})
    gp, _ = pack_local(gl)
    mp, _ = pack_local({n: m[n] for n, _ in SMALL})
    vp, _ = pack_local({n: v[n] for n, _ in SMALL})
    pad_mask = (jnp.arange(wp.size).reshape(wp.shape) >= nflat)
    dp_, nmp, nvp = _adamw("adamw_small", wp, gp, mp, jnp.where(pad_mask, 1.0, vp))
    off = 0
    for n, _ in SMALL:
        size = w[n].size
        grads[n] = gl[n]
        delta[n] = dp_.reshape(-1)[off:off + size].reshape(w[n].shape)
        new_m[n] = nmp.reshape(-1)[off:off + size].reshape(w[n].shape)
        new_v[n] = nvp.reshape(-1)[off:off + size].reshape(w[n].shape)
        off += size

    order = ["pre_mix_norm", "w_in", "lru_conv_w", "lru_conv_b", "lru_wa", "lru_ba", "lru_wx", "lru_bx", "lru_lambda",
             "lru_out_norm", "ssd_conv_w", "ssd_conv_b", "ssd_dt_bias", "ssd_a_log", "ssd_d", "ssd_out_norm", "w_out",
             "post_mix_norm", "pre_ffn_norm", "w_gate", "w_up", "w_down", "post_ffn_norm"]
    return (loss, gx[None], *[grads[n] for n in order], *[delta[n] for n in order],
            *[new_m[n] for n in order], *[new_v[n] for n in order])
```

```python
import functools

import jax
import jax.numpy as jnp
from jax import lax
from jax.experimental import pallas as pl
from jax.experimental.pallas import tpu as pltpu

F32 = jnp.float32
BF = jnp.bfloat16

D = 1024
LW = 1024
NBLK = 16
BW = 64
SI = 1024
NH = 16
HD = 64
NG = 2
HPG = NH // NG
NS = 128
CH = 128
XBC = SI + 2 * NG * NS
DTP = 128
PC = 3 * 1024 + XBC + DTP
DFF = 2816
IN_COLS = 4624
EPS = 1e-6
LRU_C = 8.0
CONV_K = 4
TT = 256
TK = 512
VMEM_LIMIT = 56 * 1024 * 1024

ADAM_LR, ADAM_B1, ADAM_B2, ADAM_EPS, ADAM_WD, ADAM_STEP = 0.001, 0.9, 0.999, 1e-08, 0.01, 10

MESH = pl.DeviceIdType.MESH


def _mm(a, b):
    return jnp.dot(a.astype(BF), b.astype(BF), preferred_element_type=F32)


def _mm_nt(a, b):
    return lax.dot_general(a.astype(BF), b.astype(BF), (((1,), (1,)), ((), ())), preferred_element_type=F32)


def _mm_tn(a, b):
    return lax.dot_general(a.astype(BF), b.astype(BF), (((0,), (0,)), ((), ())), preferred_element_type=F32)


def _sigmoid(x):
    return jax.nn.sigmoid(x)


def _softplus(x):
    return jnp.maximum(x, 0.0) + jnp.log1p(jnp.exp(-jnp.abs(x)))


_GELU_C = 0.7978845608028654
_GELU_K = 0.044715


def _gelu(x):
    t = jnp.tanh(_GELU_C * (x + _GELU_K * x * x * x))
    return 0.5 * x * (1.0 + t)


def _gelu_grad(x):
    t = jnp.tanh(_GELU_C * (x + _GELU_K * x * x * x))
    return 0.5 * (1.0 + t) + 0.5 * x * (1.0 - t * t) * _GELU_C * (1.0 + 3.0 * _GELU_K * x * x)


def _rms_fwd(x, g):
    r = lax.rsqrt(jnp.mean(x * x, axis=-1, keepdims=True) + EPS)
    return x * r * g


def _rms_bwd(x, g, dy):
    r = lax.rsqrt(jnp.mean(x * x, axis=-1, keepdims=True) + EPS)
    xh = x * r
    dxh = dy * g
    dg = jnp.sum(dy * xh, axis=0, keepdims=True)
    dx = r * (dxh - xh * jnp.mean(dxh * xh, axis=-1, keepdims=True))
    return dx, dg


def _sum_all(x):
    return jnp.sum(jnp.sum(x, axis=1, keepdims=True), axis=0, keepdims=True)


def _cumsum_rows(x, n):
    row = lax.broadcasted_iota(jnp.int32, x.shape, 0)
    k = 1
    while k < n:
        x = x + jnp.where(row >= k, pltpu.roll(x, k, 0), 0.0)
        k *= 2
    return x


def _rev_cumsum_rows(x, n):
    row = lax.broadcasted_iota(jnp.int32, x.shape, 0)
    k = 1
    while k < n:
        x = x + jnp.where(row < n - k, pltpu.roll(x, n - k, 0), 0.0)
        k *= 2
    return x


def _load_once(pairs, sem):
    @pl.when(pl.program_id(0) == 0)
    def _():
        for k, (src, dst) in enumerate(pairs):
            pltpu.make_async_copy(src, dst, sem.at[k]).start()
        for k, (src, dst) in enumerate(pairs):
            pltpu.make_async_copy(src, dst, sem.at[k]).wait()


def _params(n_axes=1):
    return pltpu.CompilerParams(dimension_semantics=("arbitrary",) * n_axes, vmem_limit_bytes=VMEM_LIMIT)


def _rows(n, width, rev_of=None):
    if rev_of is None:
        return pl.BlockSpec((n, width), lambda i: (i, 0))
    return pl.BlockSpec((n, width), lambda i: (rev_of - 1 - i, 0))


def _whole(shape):
    nd = len(shape)
    return pl.BlockSpec(shape, lambda i: (0,) * nd)


ANY = pl.BlockSpec(memory_space=pl.ANY)
S = jax.ShapeDtypeStruct


def _inproj(x, g0, wcat):
    T = x.shape[0]

    def body(x_ref, g_ref, w_hbm, h0_ref, lx_ref, lg_ref, z_ref, xbc_ref, dt_ref, w_vm, sem):
        _load_once([(w_hbm, w_vm)], sem)
        h = _rms_fwd(x_ref[...], g_ref[...]).astype(BF)
        h0_ref[...] = h
        lx_ref[...] = jnp.dot(h, w_vm[:, 0:1024], preferred_element_type=F32)
        lg_ref[...] = jnp.dot(h, w_vm[:, 1024:2048], preferred_element_type=F32)
        z_ref[...] = jnp.dot(h, w_vm[:, 2048:3072], preferred_element_type=F32)
        xbc_ref[...] = jnp.dot(h, w_vm[:, 3072:3072 + XBC], preferred_element_type=F32)
        dt_ref[...] = jnp.dot(h, w_vm[:, 3072 + XBC:PC], preferred_element_type=F32)

    return pl.pallas_call(
        body, name="inproj", grid=(T // TT,),
        in_specs=[_rows(TT, D), _whole((1, D)), ANY],
        out_specs=[_rows(TT, D), _rows(TT, 1024), _rows(TT, 1024), _rows(TT, 1024), _rows(TT, XBC), _rows(TT, DTP)],
        out_shape=[S((T, D), BF), S((T, 1024), F32), S((T, 1024), F32), S((T, 1024), F32), S((T, XBC), F32), S((T, DTP), F32)],
        scratch_shapes=[pltpu.VMEM((D, PC), BF), pltpu.SemaphoreType.DMA((1,))],
        compiler_params=_params(),
    )(x, g0, wcat)


def _blockdiag_mm(v, w4_ref):
    return jnp.concatenate([_mm(v[:, 256 * j:256 * (j + 1)], w4_ref[j]) for j in range(4)], axis=1)


def _lru_gates(lx, p_ref, wa_ref, wx_ref):
    r = _sigmoid(_blockdiag_mm(lx, wa_ref) + p_ref[5:6, :])
    i = _sigmoid(_blockdiag_mm(lx, wx_ref) + p_ref[6:7, :])
    sp = _softplus(-p_ref[7:8, :])
    la = -LRU_C * r * sp
    a = jnp.exp(la)
    th = jnp.tanh(la)
    mult = jnp.sqrt(-2.0 * th / (1.0 - th))
    return r, i, sp, a, mult


def _conv_from(xp_ref, p_ref, n):
    acc = p_ref[4:5, :] + p_ref[0:1, :] * xp_ref[pl.ds(8 - CONV_K + 1, n), :]
    for k in range(1, CONV_K):
        acc = acc + p_ref[k:k + 1, :] * xp_ref[pl.ds(8 - CONV_K + 1 + k, n), :]
    return acc


def _lru_fwd(lxr, lg, p_lru, wa4, wx4):
    T = lxr.shape[0]

    def body(lx_ref, lg_ref, p_ref, wa_ref, wx_ref, h_ref, y_ref, xp, a_s, u_s, hc):
        @pl.when(pl.program_id(0) == 0)
        def _():
            xp[0:8, :] = jnp.zeros((8, LW), F32)
            hc[...] = jnp.zeros_like(hc)

        xp[8:8 + TT, :] = lx_ref[...]
        lx = _conv_from(xp, p_ref, TT)
        xp[0:8, :] = xp[TT:TT + 8, :]
        r, i, sp, a, mult = _lru_gates(lx, p_ref, wa_ref, wx_ref)
        a_s[...] = a
        u_s[...] = mult * (i * lx)

        def step(t, h):
            h = a_s[pl.ds(t, 1), :] * h + u_s[pl.ds(t, 1), :]
            h_ref[pl.ds(t, 1), :] = h
            return h

        hc[0:1, :] = lax.fori_loop(0, TT, step, hc[0:1, :], unroll=8)
        gated = h_ref[...] * _gelu(lg_ref[...])
        y_ref[...] = _rms_fwd(gated, p_ref[8:9, :]).astype(BF)

    return pl.pallas_call(
        body, name="lru_fwd", grid=(T // TT,),
        in_specs=[_rows(TT, LW), _rows(TT, LW), _whole((16, LW)), _whole((4, 256, 256)), _whole((4, 256, 256))],
        out_specs=[_rows(TT, LW), _rows(TT, LW)],
        out_shape=[S((T, LW), F32), S((T, LW), BF)],
        scratch_shapes=[pltpu.VMEM((TT + 8, LW), F32), pltpu.VMEM((TT, LW), F32), pltpu.VMEM((TT, LW), F32),
                        pltpu.VMEM((8, LW), F32)],
        compiler_params=_params(),
    )(lxr, lg, p_lru, wa4, wx4)


def _ssd_prep(xp, xr_ref, dt_ref, cw_ref, hp_ref):
    xp[8:8 + CH, :] = xr_ref[...]
    cv = _conv_from(xp, cw_ref, CH)
    sg = _sigmoid(cv)
    xbc = cv * sg
    lane = lax.broadcasted_iota(jnp.int32, (CH, DTP), 1)
    raw = dt_ref[...] + hp_ref[0:1, :]
    dtv = jnp.where(lane < NH, _softplus(raw), 0.0)
    A = jnp.where(lane[0:1, :] < NH, -jnp.exp(hp_ref[1:2, :]), 0.0)
    cs = _cumsum_rows(dtv * A, CH)
    return cv, sg, xbc, raw, dtv, A, cs


def _per_head_lanes(v):
    r = v.shape[0]
    first = lax.broadcasted_iota(jnp.int32, (r, 2 * HD), 1) < HD
    pairs = [jnp.where(first, jnp.broadcast_to(v[:, 2 * j:2 * j + 1], (r, 2 * HD)),
                       jnp.broadcast_to(v[:, 2 * j + 1:2 * j + 2], (r, 2 * HD))) for j in range(NH // 2)]
    return jnp.concatenate(pairs, axis=1)


def _per_head_rows(col, g):
    return jnp.concatenate([jnp.broadcast_to(col[g * HPG + k:g * HPG + k + 1, :], (HD, NS)) for k in range(HPG)], axis=0)


def _ssd_decays(cs):
    csT = cs.T
    cl = cs[CH - 1:CH, :]
    E_x = _per_head_lanes(jnp.exp(cs))
    dsm = jnp.exp(cl - cs)
    ds_x = _per_head_lanes(dsm)
    El_rows = jnp.broadcast_to(jnp.exp(csT[0:NH, CH - 1:CH]), (NH, NS))
    return csT, dsm, E_x, ds_x, El_rows


def _ssd_fwd(xbcr, z, dtr, cw_ssd, hp_ssd, g_ssd):
    T = xbcr.shape[0]
    NC = T // CH

    def body(xr_ref, z_ref, dt_ref, cw_ref, hp_ref, g_ref, y_ref, yn_ref, st_ref, xp, st):
        @pl.when(pl.program_id(0) == 0)
        def _():
            xp[0:8, :] = jnp.zeros((8, XBC), F32)
            st[...] = jnp.zeros_like(st)

        cv, sg, xbc, raw, dtv, A, cs = _ssd_prep(xp, xr_ref, dt_ref, cw_ref, hp_ref)
        xp[0:8, :] = xp[CH:CH + 8, :]
        st_ref[0] = st[...]
        csT, dsm, E_x, ds_x, El_rows = _ssd_decays(cs)
        X = xbc[:, 0:SI]
        xs = X * _per_head_lanes(dtv)
        xsd = (xs * ds_x).astype(BF)
        DX = _per_head_lanes(hp_ref[...])[2:3, :] * X
        tril = lax.broadcasted_iota(jnp.int32, (CH, CH), 0) >= lax.broadcasted_iota(jnp.int32, (CH, CH), 1)
        first = lax.broadcasted_iota(jnp.int32, (CH, 2 * HD), 1) < HD
        GW = HPG * HD
        for g in range(NG):
            Bg = xbc[:, SI + NS * g:SI + NS * (g + 1)].astype(BF)
            Cg = xbc[:, SI + NG * NS + NS * g:SI + NG * NS + NS * (g + 1)].astype(BF)
            G = _mm_nt(Cg, Bg)
            Sg = st[GW * g:GW * (g + 1), :]
            Yo = _mm_nt(Cg, Sg) * E_x[:, GW * g:GW * (g + 1)]
            st[GW * g:GW * (g + 1), :] = _per_head_rows(El_rows, g) * Sg + _mm_tn(xsd[:, GW * g:GW * (g + 1)], Bg)
            for jj in range(HPG // 2):
                j = g * (HPG // 2) + jj
                ps = slice(2 * HD * j, 2 * HD * (j + 1))
                xs_pair = xs[:, ps]
                acc = Yo[:, 2 * HD * jj:2 * HD * (jj + 1)] + DX[:, ps]
                for e in range(2):
                    h = 2 * j + e
                    Lm = jnp.exp(jnp.where(tril, cs[:, h:h + 1] - csT[h:h + 1, :], -1e30))
                    acc = acc + _mm(G * Lm, jnp.where(first if e == 0 else ~first, xs_pair, 0.0))
                y_ref[:, ps] = acc
        zz = z_ref[...]
        gated = y_ref[...] * (zz * _sigmoid(zz))
        yn_ref[...] = _rms_fwd(gated, g_ref[...]).astype(BF)

    return pl.pallas_call(
        body, name="ssd_fwd", grid=(NC,),
        in_specs=[_rows(CH, XBC), _rows(CH, SI), _rows(CH, DTP), _whole((8, XBC)), _whole((8, DTP)), _whole((1, SI))],
        out_specs=[_rows(CH, SI), _rows(CH, SI), pl.BlockSpec((1, NH * HD, NS), lambda i: (i, 0, 0))],
        out_shape=[S((T, SI), F32), S((T, SI), BF), S((NC, NH * HD, NS), F32)],
        scratch_shapes=[pltpu.VMEM((CH + 8, XBC), F32), pltpu.VMEM((NH * HD, NS), F32)],
        compiler_params=_params(),
    )(xbcr, z, dtr, cw_ssd, hp_ssd, g_ssd)


def _outproj(ylru, yssd, x, wout, g_pm, g_pf):
    T = x.shape[0]

    def body(yl_ref, ys_ref, x_ref, w_hbm, gpm_ref, gpf_ref, mix_ref, x1_ref, h2_ref, w_vm, sem):
        _load_once([(w_hbm, w_vm)], sem)
        mix = (jnp.dot(yl_ref[...], w_vm[0:LW, :], preferred_element_type=F32)
               + jnp.dot(ys_ref[...], w_vm[LW:LW + SI, :], preferred_element_type=F32))
        mix_ref[...] = mix
        x1 = x_ref[...] + _rms_fwd(mix, gpm_ref[...])
        x1_ref[...] = x1
        h2_ref[...] = _rms_fwd(x1, gpf_ref[...]).astype(BF)

    return pl.pallas_call(
        body, name="outproj", grid=(T // TT,),
        in_specs=[_rows(TT, LW), _rows(TT, SI), _rows(TT, D), ANY, _whole((1, D)), _whole((1, D))],
        out_specs=[_rows(TT, D), _rows(TT, D), _rows(TT, D)],
        out_shape=[S((T, D), F32), S((T, D), F32), S((T, D), BF)],
        scratch_shapes=[pltpu.VMEM((LW + SI, D), BF), pltpu.SemaphoreType.DMA((1,))],
        compiler_params=_params(),
    )(ylru, yssd, x, wout, g_pm, g_pf)


def _ffn_fwd(h2, x1, tgt, wg, wu, wd, g_pff):
    T = x1.shape[0]

    def body(h2_ref, x1_ref, t_ref, wg_hbm, wu_hbm, wd_hbm, g_ref,
             gate_ref, up_ref, act_ref, df_ref, dx2_ref, st_ref, wg_vm, wu_vm, wd_vm, sem):
        _load_once([(wg_hbm, wg_vm), (wu_hbm, wu_vm), (wd_hbm, wd_vm)], sem)

        @pl.when(pl.program_id(0) == 0)
        def _():
            st_ref[...] = jnp.zeros_like(st_ref)

        h2 = h2_ref[...]
        gate = jnp.dot(h2, wg_vm[...], preferred_element_type=F32)
        up = jnp.dot(h2, wu_vm[...], preferred_element_type=F32)
        gate_ref[...] = gate
        up_ref[...] = up
        act = (gate * _sigmoid(gate) * up).astype(BF)
        act_ref[...] = act
        f = jnp.dot(act, wd_vm[...], preferred_element_type=F32)
        g = g_ref[...]
        x2 = x1_ref[...] + _rms_fwd(f, g)
        err = x2 - t_ref[...]
        st_ref[0:1, :] += 0.5 * jnp.sum(err * err, axis=0, keepdims=True) * (1.0 / D)
        dx2 = err * (1.0 / D)
        dx2_ref[...] = dx2
        df, dg = _rms_bwd(f, g, dx2)
        df_ref[...] = df.astype(BF)
        st_ref[1:2, :] += dg

    return pl.pallas_call(
        body, name="ffn_fwd", grid=(T // TT,),
        in_specs=[_rows(TT, D), _rows(TT, D), _rows(TT, D), ANY, ANY, ANY, _whole((1, D))],
        out_specs=[_rows(TT, DFF), _rows(TT, DFF), _rows(TT, DFF), _rows(TT, D), _rows(TT, D), _whole((8, D))],
        out_shape=[S((T, DFF), F32), S((T, DFF), F32), S((T, DFF), BF), S((T, D), BF), S((T, D), F32), S((8, D), F32)],
        scratch_shapes=[pltpu.VMEM((D, DFF), BF), pltpu.VMEM((D, DFF), BF), pltpu.VMEM((DFF, D), BF),
                        pltpu.SemaphoreType.DMA((3,))],
        compiler_params=_params(),
    )(h2, x1, tgt, wg, wu, wd, g_pff)


def _ffn_bwd(df, gate, up, wdT, wgT, wuT):
    T = df.shape[0]

    def body(df_ref, gate_ref, up_ref, wd_hbm, wg_hbm, wu_hbm, dgate_ref, dup_ref, dh2_ref, wd_vm, wg_vm, wu_vm, sem):
        _load_once([(wd_hbm, wd_vm), (wg_hbm, wg_vm), (wu_hbm, wu_vm)], sem)
        dact = jnp.dot(df_ref[...], wd_vm[...], preferred_element_type=F32)
        gate = gate_ref[...]
        s = _sigmoid(gate)
        dup = (dact * (gate * s)).astype(BF)
        dgate = (dact * up_ref[...] * (s + gate * s * (1.0 - s))).astype(BF)
        dup_ref[...] = dup
        dgate_ref[...] = dgate
        dh2_ref[...] = (jnp.dot(dgate, wg_vm[...], preferred_element_type=F32)
                        + jnp.dot(dup, wu_vm[...], preferred_element_type=F32))

    return pl.pallas_call(
        body, name="ffn_bwd", grid=(T // TT,),
        in_specs=[_rows(TT, D), _rows(TT, DFF), _rows(TT, DFF), ANY, ANY, ANY],
        out_specs=[_rows(TT, DFF), _rows(TT, DFF), _rows(TT, D)],
        out_shape=[S((T, DFF), BF), S((T, DFF), BF), S((T, D), F32)],
        scratch_shapes=[pltpu.VMEM((D, DFF), BF), pltpu.VMEM((DFF, D), BF), pltpu.VMEM((DFF, D), BF),
                        pltpu.SemaphoreType.DMA((3,))],
        compiler_params=_params(),
    )(df, gate, up, wdT, wgT, wuT)


def _mix_bwd(dh2, x1, dx2, mix, woutT, g_pf, g_pm):
    T = x1.shape[0]

    def body(dh2_ref, x1_ref, dx2_ref, mix_ref, w_hbm, gpf_ref, gpm_ref,
             dx1_ref, dmix_ref, dyl_ref, dys_ref, st_ref, w_vm, sem):
        _load_once([(w_hbm, w_vm)], sem)

        @pl.when(pl.program_id(0) == 0)
        def _():
            st_ref[...] = jnp.zeros_like(st_ref)

        dxa, dgpf = _rms_bwd(x1_ref[...], gpf_ref[...], dh2_ref[...])
        dx1 = dx2_ref[...] + dxa
        dx1_ref[...] = dx1
        dmix, dgpm = _rms_bwd(mix_ref[...], gpm_ref[...], dx1)
        dmix = dmix.astype(BF)
        dmix_ref[...] = dmix
        st_ref[0:1, :] += dgpf
        st_ref[1:2, :] += dgpm
        dyl_ref[...] = jnp.dot(dmix, w_vm[:, 0:LW], preferred_element_type=F32)
        dys_ref[...] = jnp.dot(dmix, w_vm[:, LW:LW + SI], preferred_element_type=F32)

    return pl.pallas_call(
        body, name="mix_bwd", grid=(T // TT,),
        in_specs=[_rows(TT, D), _rows(TT, D), _rows(TT, D), _rows(TT, D), ANY, _whole((1, D)), _whole((1, D))],
        out_specs=[_rows(TT, D), _rows(TT, D), _rows(TT, LW), _rows(TT, SI), _whole((8, D))],
        out_shape=[S((T, D), F32), S((T, D), BF), S((T, LW), F32), S((T, SI), F32), S((8, D), F32)],
        scratch_shapes=[pltpu.VMEM((D, LW + SI), BF), pltpu.SemaphoreType.DMA((1,))],
        compiler_params=_params(),
    )(dh2, x1, dx2, mix, woutT, g_pf, g_pm)


def _halo(width, n_tiles, tile):
    per = tile // 8
    return pl.BlockSpec((8, width), lambda i: (jnp.maximum((n_tiles - 1 - i) * per - 1, 0), 0))


def _lru_bwd(dy, lxr, lg, h, p_lru, wa4, wx4, wa4T, wx4T):
    T = dy.shape[0]
    NT = T // TT

    def body(dy_ref, lx_ref, lxh_ref, lg_ref, h_ref, hh_ref, p_ref, wa_ref, wx_ref, waT_ref, wxT_ref,
             dlx_ref, dlg_ref, st_ref, dwa_ref, dwx_ref, xp, hp, dp, a_s, d_s, g_s, cc):
        first = pl.program_id(0) == 0
        top = pl.program_id(0) == NT - 1

        @pl.when(first)
        def _():
            st_ref[...] = jnp.zeros_like(st_ref)
            dwa_ref[...] = jnp.zeros_like(dwa_ref)
            dwx_ref[...] = jnp.zeros_like(dwx_ref)
            dp[TT:TT + 8, :] = jnp.zeros((8, LW), F32)
            cc[...] = jnp.zeros_like(cc)

        keep = jnp.where(top, 0.0, 1.0)
        xp[0:8, :] = lxh_ref[...] * keep
        xp[8:8 + TT, :] = lx_ref[...]
        hp[0:8, :] = hh_ref[...] * keep
        hp[8:8 + TT, :] = h_ref[...]
        lx = _conv_from(xp, p_ref, TT)
        r, i, sp, a, mult = _lru_gates(lx, p_ref, wa_ref, wx_ref)

        lg = lg_ref[...]
        hcur = h_ref[...]
        ge = _gelu(lg)
        dgated, dgn = _rms_bwd(hcur * ge, p_ref[8:9, :], dy_ref[...])
        st_ref[8:9, :] += dgn
        dlg_ref[...] = (dgated * hcur * _gelu_grad(lg)).astype(BF)
        a_s[...] = a
        d_s[...] = dgated * ge

        def step(k, c):
            t = TT - 1 - k
            g = d_s[pl.ds(t, 1), :] + c
            g_s[pl.ds(t, 1), :] = g
            return a_s[pl.ds(t, 1), :] * g

        cc[0:1, :] = lax.fori_loop(0, TT, step, cc[0:1, :], unroll=8)
        gt = g_s[...]
        da = gt * hp[pl.ds(7, TT), :]
        dmult = gt * i * lx
        di = gt * mult * lx
        dlxc = gt * mult * i
        dla = da * a - dmult * (a * a) / mult
        dr = dla * (-LRU_C * sp)
        st_ref[7:8, :] += jnp.sum(dla * (-LRU_C * r), axis=0, keepdims=True) * (-_sigmoid(-p_ref[7:8, :]))
        dzr = dr * r * (1.0 - r)
        dzi = di * i * (1.0 - i)
        st_ref[5:6, :] += jnp.sum(dzr, axis=0, keepdims=True)
        st_ref[6:7, :] += jnp.sum(dzi, axis=0, keepdims=True)
        dlxc = dlxc + _blockdiag_mm(dzr, waT_ref) + _blockdiag_mm(dzi, wxT_ref)
        for j in range(4):
            sl = slice(256 * j, 256 * (j + 1))
            dwa_ref[j] += _mm_tn(lx[:, sl], dzr[:, sl])
            dwx_ref[j] += _mm_tn(lx[:, sl], dzi[:, sl])
        dp[0:TT, :] = dlxc
        acc = p_ref[0:1, :] * dp[pl.ds(CONV_K - 1, TT), :]
        for k in range(1, CONV_K):
            acc = acc + p_ref[k:k + 1, :] * dp[pl.ds(CONV_K - 1 - k, TT), :]
        dlx_ref[...] = acc.astype(BF)
        dp[TT:TT + 8, :] = dp[0:8, :]
        for k in range(CONV_K):
            st_ref[k:k + 1, :] += jnp.sum(dlxc * xp[pl.ds(8 - CONV_K + 1 + k, TT), :], axis=0, keepdims=True)
        st_ref[4:5, :] += jnp.sum(dlxc, axis=0, keepdims=True)

    w4 = _whole((4, 256, 256))
    return pl.pallas_call(
        body, name="lru_bwd", grid=(NT,),
        in_specs=[_rows(TT, LW, NT), _rows(TT, LW, NT), _halo(LW, NT, TT), _rows(TT, LW, NT), _rows(TT, LW, NT),
                  _halo(LW, NT, TT), _whole((16, LW)), w4, w4, w4, w4],
        out_specs=[_rows(TT, LW, NT), _rows(TT, LW, NT), _whole((16, LW)), w4, w4],
        out_shape=[S((T, LW), BF), S((T, LW), BF), S((16, LW), F32), S((4, 256, 256), F32), S((4, 256, 256), F32)],
        scratch_shapes=[pltpu.VMEM((TT + 8, LW), F32), pltpu.VMEM((TT + 8, LW), F32), pltpu.VMEM((TT + 8, LW), F32),
                        pltpu.VMEM((TT, LW), F32), pltpu.VMEM((TT, LW), F32), pltpu.VMEM((TT, LW), F32),
                        pltpu.VMEM((8, LW), F32)],
        compiler_params=_params(),
    )(dy, lxr, lxr, lg, h, h, p_lru, wa4, wx4, wa4T, wx4T)


def _ssd_bwd(dyn, xbcr, z, dtr, y, states, cw_ssd, hp_ssd, g_ssd):
    T = dyn.shape[0]
    NC = T // CH

    def body(dyn_ref, xr_ref, xh_ref, z_ref, dt_ref, y_ref, st_ref, cw_ref, hp_ref, g_ref,
             dxbc_ref, dz_ref, ddt_ref, cst_ref, hst_ref, gst_ref, xp, dp, dS, dxb, yo_s, q_s, dxs_s, t1_s):
        first = pl.program_id(0) == 0
        top = pl.program_id(0) == NC - 1

        @pl.when(first)
        def _():
            cst_ref[...] = jnp.zeros_like(cst_ref)
            hst_ref[...] = jnp.zeros_like(hst_ref)
            gst_ref[...] = jnp.zeros_like(gst_ref)
            dp[CH:CH + 8, :] = jnp.zeros((8, XBC), F32)
            dS[...] = jnp.zeros_like(dS)

        xp[0:8, :] = xh_ref[...] * jnp.where(top, 0.0, 1.0)
        cv, sg, xbc, raw, dtv, A, cs = _ssd_prep(xp, xr_ref, dt_ref, cw_ref, hp_ref)
        csT, dsm, E_x, ds_x, El_rows = _ssd_decays(cs)
        row_i = lax.broadcasted_iota(jnp.int32, (CH, CH), 0)
        col_i = lax.broadcasted_iota(jnp.int32, (CH, CH), 1)
        tril = row_i >= col_i
        first = col_i < HD
        head_of = ((lax.broadcasted_iota(jnp.int32, (DTP, SI), 1) >> 6)
                   == lax.broadcasted_iota(jnp.int32, (DTP, SI), 0)).astype(BF)
        head_ofT = ((lax.broadcasted_iota(jnp.int32, (SI, DTP), 0) >> 6)
                    == lax.broadcasted_iota(jnp.int32, (SI, DTP), 1)).astype(BF)

        def hi_lo(v):
            hi = v.astype(BF)
            return hi, (v - hi.astype(F32)).astype(BF)

        def lane_sums(v):
            hi, lo = hi_lo(v)
            return _mm(hi, head_ofT) + _mm(lo, head_ofT)

        zz = z_ref[...]
        sz = _sigmoid(zz)
        yv = y_ref[...]
        dgn, dg = _rms_bwd(yv * (zz * sz), g_ref[...], dyn_ref[...])
        gst_ref[0:1, :] += dg
        dz_ref[...] = (dgn * yv * (sz + zz * sz * (1.0 - sz))).astype(BF)
        dY = dgn * (zz * sz)

        X = xbc[:, 0:SI]
        dt_x = _per_head_lanes(dtv)
        xs = X * dt_x
        xsd = (xs * ds_x).astype(BF)
        D_x = _per_head_lanes(hp_ref[...])[2:3, :]
        dcs_col = jnp.zeros((CH, DTP), F32)
        dcs_row = jnp.zeros((CH, DTP), F32)
        GW = HPG * HD
        for g in range(NG):
            gs = slice(GW * g, GW * (g + 1))
            Bg = xbc[:, SI + NS * g:SI + NS * (g + 1)].astype(BF)
            Cg = xbc[:, SI + NG * NS + NS * g:SI + NG * NS + NS * (g + 1)].astype(BF)
            G = _mm_nt(Cg, Bg)
            Sg = st_ref[0, gs, :]
            dSe = dS[gs, :]
            dYg = dY[:, gs]
            yo_s[:, gs] = _mm_nt(Cg, Sg) * E_x[:, gs]
            dP = dYg * E_x[:, gs]
            dCg = _mm(dP, Sg)
            dS[gs, :] = _mm_tn(dP, Cg) + _per_head_rows(El_rows, g) * dSe
            t1_s[gs, :] = dSe * Sg
            Q = _mm_nt(Bg, dSe)
            q_s[:, gs] = Q
            dBg = _mm(xsd[:, gs], dSe)
            dG = jnp.zeros((CH, CH), F32)
            for jj in range(HPG // 2):
                j = g * (HPG // 2) + jj
                ps = slice(2 * HD * j, 2 * HD * (j + 1))
                xs_pair = xs[:, ps]
                dxs_pair = Q[:, 2 * HD * jj:2 * HD * (jj + 1)] * ds_x[:, ps]
                for e in range(2):
                    h = 2 * j + e
                    Lm = jnp.exp(jnp.where(tril, cs[:, h:h + 1] - csT[h:h + 1, :], -1e30))
                    M = G * Lm
                    dYm = jnp.where(first if e == 0 else ~first, dY[:, ps], 0.0).astype(BF)
                    dM = _mm_nt(dYm, xs_pair)
                    dxs_pair = dxs_pair + _mm_tn(M, dYm)
                    Wm = dM * M
                    dcs_col = dcs_col + jnp.where(col_i == h, jnp.sum(Wm, axis=1, keepdims=True), 0.0)
                    dcs_row = dcs_row + jnp.where(row_i == h, -jnp.sum(Wm, axis=0, keepdims=True), 0.0)
                    dG = dG + dM * Lm
                dxs_s[:, ps] = dxs_pair
            dxb[:, SI + NS * g:SI + NS * (g + 1)] = dBg + _mm_tn(dG, Cg)
            dxb[:, SI + NG * NS + NS * g:SI + NG * NS + NS * (g + 1)] = dCg + _mm(dG, Bg)

        dxs = dxs_s[...]
        dxb[:, 0:SI] = D_x * dY + dxs * dt_x
        dds = lane_sums(q_s[...] * xs) * dsm
        dcs_col = dcs_col + lane_sums(dY * yo_s[...]) - dds
        ddt_col = lane_sums(dxs * X)
        dD = jnp.sum(lane_sums(dY * X), axis=0, keepdims=True)
        t_hi, t_lo = hi_lo(t1_s[...])
        dcl_rows = jnp.sum(_mm(head_of, t_hi) + _mm(head_of, t_lo), axis=1, keepdims=True) * jnp.exp(csT[:, CH - 1:CH])
        dcs_row = dcs_row + jnp.where(col_i == CH - 1, dcl_rows, 0.0)
        dcs_col = dcs_col + jnp.where(row_i == CH - 1, jnp.sum(dds, axis=0, keepdims=True), 0.0)

        da = _rev_cumsum_rows(dcs_col + dcs_row.T, CH)
        ddt_col = ddt_col + da * A
        hst_ref[1:2, :] += jnp.sum(da * dtv, axis=0, keepdims=True) * A
        hst_ref[2:3, :] += dD
        draw = jnp.where(col_i < NH, ddt_col * _sigmoid(raw), 0.0)
        ddt_ref[...] = draw.astype(BF)
        hst_ref[0:1, :] += jnp.sum(draw, axis=0, keepdims=True)

        dcv = dxb[...] * (sg + cv * sg * (1.0 - sg))
        dp[0:CH, :] = dcv
        acc = cw_ref[0:1, :] * dp[pl.ds(CONV_K - 1, CH), :]
        for k in range(1, CONV_K):
            acc = acc + cw_ref[k:k + 1, :] * dp[pl.ds(CONV_K - 1 - k, CH), :]
        dxbc_ref[...] = acc.astype(BF)
        dp[CH:CH + 8, :] = dp[0:8, :]
        for k in range(CONV_K):
            cst_ref[k:k + 1, :] += jnp.sum(dcv * xp[pl.ds(8 - CONV_K + 1 + k, CH), :], axis=0, keepdims=True)
        cst_ref[4:5, :] += jnp.sum(dcv, axis=0, keepdims=True)

    return pl.pallas_call(
        body, name="ssd_bwd", grid=(NC,),
        in_specs=[_rows(CH, SI, NC), _rows(CH, XBC, NC), _halo(XBC, NC, CH), _rows(CH, SI, NC), _rows(CH, DTP, NC),
                  _rows(CH, SI, NC), pl.BlockSpec((1, NH * HD, NS), lambda i: (NC - 1 - i, 0, 0)),
                  _whole((8, XBC)), _whole((8, DTP)), _whole((1, SI))],
        out_specs=[_rows(CH, XBC, NC), _rows(CH, SI, NC), _rows(CH, DTP, NC), _whole((8, XBC)), _whole((8, DTP)),
                   _whole((8, SI))],
        out_shape=[S((T, XBC), BF), S((T, SI), BF), S((T, DTP), BF), S((8, XBC), F32), S((8, DTP), F32), S((8, SI), F32)],
        scratch_shapes=[pltpu.VMEM((CH + 8, XBC), F32), pltpu.VMEM((CH + 8, XBC), F32), pltpu.VMEM((NH * HD, NS), F32),
                        pltpu.VMEM((CH, XBC), F32), pltpu.VMEM((CH, SI), F32), pltpu.VMEM((CH, SI), F32),
                        pltpu.VMEM((CH, SI), F32), pltpu.VMEM((NH * HD, NS), F32)],
        compiler_params=_params(),
    )(dyn, xbcr, xbcr, z, dtr, y, states, cw_ssd, hp_ssd, g_ssd)


def _inproj_bwd(dlx, dlg, dz, dxbc, ddt, x, dx1, wcatT, g0):
    T = x.shape[0]

    def body(dlx_ref, dlg_ref, dz_ref, dxbc_ref, ddt_ref, x_ref, dx1_ref, w_hbm, g_ref, dx_ref, st_ref, w_vm, sem):
        _load_once([(w_hbm, w_vm)], sem)

        @pl.when(pl.program_id(0) == 0)
        def _():
            st_ref[...] = jnp.zeros_like(st_ref)

        dh = jnp.dot(dlx_ref[...], w_vm[0:1024, :], preferred_element_type=F32)
        dh = dh + jnp.dot(dlg_ref[...], w_vm[1024:2048, :], preferred_element_type=F32)
        dh = dh + jnp.dot(dz_ref[...], w_vm[2048:3072, :], preferred_element_type=F32)
        dh = dh + jnp.dot(dxbc_ref[...], w_vm[3072:3072 + XBC, :], preferred_element_type=F32)
        dh = dh + jnp.dot(ddt_ref[...], w_vm[3072 + XBC:PC, :], preferred_element_type=F32)
        dx, dg = _rms_bwd(x_ref[...], g_ref[...], dh)
        dx_ref[...] = dx1_ref[...] + dx
        st_ref[0:1, :] += dg

    return pl.pallas_call(
        body, name="inproj_bwd", grid=(T // TT,),
        in_specs=[_rows(TT, 1024), _rows(TT, 1024), _rows(TT, 1024), _rows(TT, XBC), _rows(TT, DTP), _rows(TT, D),
                  _rows(TT, D), ANY, _whole((1, D))],
        out_specs=[_rows(TT, D), _whole((8, D))],
        out_shape=[S((T, D), F32), S((8, D), F32)],
        scratch_shapes=[pltpu.VMEM((PC, D), BF), pltpu.SemaphoreType.DMA((1,))],
        compiler_params=_params(),
    )(dlx, dlg, dz, dxbc, ddt, x, dx1, wcatT, g0)


def _wgrad(name, a, bs):
    T, M = a.shape
    nb = len(bs)
    widths = [b.shape[1] for b in bs]
    offs = [sum(widths[:j]) for j in range(nb)]
    nk = T // TK

    def body(*refs):
        a_ref = refs[0]
        b_refs = refs[1:1 + nb]
        o_ref, acc, sem = refs[1 + nb], refs[2 + nb], refs[3 + nb]
        k = pl.program_id(0)
        av = a_ref[...]
        for b_ref, off, w in zip(b_refs, offs, widths):
            p = lax.dot_general(av, b_ref[...], (((0,), (0,)), ((), ())), preferred_element_type=F32)

            @pl.when(k == 0)
            def _():
                acc[:, off:off + w] = p

            @pl.when(k > 0)
            def _():
                acc[:, off:off + w] += p

        @pl.when(k == nk - 1)
        def _():
            cp = pltpu.make_async_copy(acc, o_ref, sem)
            cp.start()
            cp.wait()

    return pl.pallas_call(
        body, name=name, grid=(nk,),
        in_specs=[_rows(TK, M)] + [_rows(TK, w) for w in widths],
        out_specs=ANY, out_shape=S((M, sum(widths)), F32),
        scratch_shapes=[pltpu.VMEM((M, sum(widths)), F32), pltpu.SemaphoreType.DMA],
        compiler_params=_params(),
    )(a, *bs)


def _adamw(name, w, g, m, v):
    R, C = w.shape
    tr = _row_tile(R, C)
    c1 = 1.0 - ADAM_B1 ** ADAM_STEP
    c2 = 1.0 - ADAM_B2 ** ADAM_STEP

    def body(w_ref, g_ref, m_ref, v_ref, d_ref, nm_ref, nv_ref):
        gg = g_ref[...]
        mm = ADAM_B1 * m_ref[...] + (1.0 - ADAM_B1) * gg
        vv = ADAM_B2 * v_ref[...] + (1.0 - ADAM_B2) * (gg * gg)
        nm_ref[...] = mm
        nv_ref[...] = vv
        d_ref[...] = -ADAM_LR * ((mm / c1) / (jnp.sqrt(vv / c2) + ADAM_EPS) + ADAM_WD * w_ref[...])

    blk = pl.BlockSpec((tr, C), lambda i: (i, 0))
    return pl.pallas_call(
        body, name=name, grid=(R // tr,),
        in_specs=[blk] * 4, out_specs=[blk] * 3, out_shape=[S((R, C), F32)] * 3,
        compiler_params=_params(),
    )(w, g, m, v)


def _pos():
    return lax.axis_index("x"), lax.axis_index("y"), lax.axis_index("c")


def _other_chips(x, y):
    return [(1 - x, y), (x, 1 - y), (1 - x, 1 - y)]


def _half(ref, c, hr):
    sl = pl.ds(pl.multiple_of(c * hr, 8), hr)
    return ref.at[:, sl, :] if len(ref.shape) == 3 else ref.at[sl, :]


def _remote(src, dst, send_sem, recv_sem, to):
    return pltpu.make_async_remote_copy(src_ref=src, dst_ref=dst, send_sem=send_sem, recv_sem=recv_sem,
                                        device_id=to, device_id_type=MESH)


def _allgather_weights(shards):
    n = len(shards)

    def body(*refs):
        ins, outs = refs[:n], refs[n:2 * n]
        send_sems, recv_sems = refs[2 * n], refs[2 * n + 1]
        x, y, c = _pos()
        me = 2 * x + y
        chips = _other_chips(x, y)
        first, passed = [], []
        for i, (src, dst) in enumerate(zip(ins, outs)):
            hr = src.shape[0] // 2
            my_half = pl.ds(pl.multiple_of(c * hr, 16), hr)
            for k, (cx, cy) in enumerate(chips):
                cp = _remote(src.at[my_half, :], dst.at[me, my_half, :], send_sems.at[6 * i + k], recv_sems.at[6 * i + k],
                             (cx, cy, c))
                cp.start()
                first.append(cp)
        for i, (src, dst) in enumerate(zip(ins, outs)):
            hr = src.shape[0] // 2
            my_half = pl.ds(pl.multiple_of(c * hr, 16), hr)
            for k, (cx, cy) in enumerate(chips):
                blk = dst.at[2 * cx + cy, my_half, :]
                _remote(blk, blk, send_sems.at[6 * i + k], recv_sems.at[6 * i + k], (cx, cy, c)).wait_recv()
                fw = _remote(blk, blk, send_sems.at[6 * i + 3 + k], recv_sems.at[6 * i + 3 + k], (x, y, 1 - c))
                fw.start()
                passed.append(fw)
        for i, (src, dst) in enumerate(zip(ins, outs)):
            hr = src.shape[0] // 2
            sib_half = pl.ds(pl.multiple_of((1 - c) * hr, 16), hr)
            for k, (cx, cy) in enumerate(chips):
                blk = dst.at[2 * cx + cy, sib_half, :]
                _remote(blk, blk, send_sems.at[6 * i + 3 + k], recv_sems.at[6 * i + 3 + k], (x, y, 1 - c)).wait_recv()
        for cp in first + passed:
            cp.wait_send()

    return pl.pallas_call(
        body, name="allgather_weights", in_specs=[ANY] * n, out_specs=[ANY] * n,
        out_shape=[S((4,) + s.shape, s.dtype) for s in shards],
        scratch_shapes=[pltpu.SemaphoreType.DMA((6 * n,)), pltpu.SemaphoreType.DMA((6 * n,))],
    )(*shards)


def _pair_exchange(bufs):
    n = len(bufs)

    def half_shape(b):
        return b.shape[:-2] + (b.shape[-2] // 2, b.shape[-1])

    def body(*refs):
        ins, outs = refs[:n], refs[n:2 * n]
        send_sems, recv_sems = refs[2 * n], refs[2 * n + 1]
        x, y, c = _pos()
        copies = [_remote(_half(src, 1 - c, src.shape[-2] // 2), dst, send_sems.at[k], recv_sems.at[k], (x, y, 1 - c))
                  for k, (src, dst) in enumerate(zip(ins, outs))]
        for cp in copies:
            cp.start()
        for cp in copies:
            cp.wait()

    return pl.pallas_call(
        body, name="pair_exchange", in_specs=[ANY] * n, out_specs=[ANY] * n,
        out_shape=[S(half_shape(b), b.dtype) for b in bufs],
        scratch_shapes=[pltpu.SemaphoreType.DMA((n,)), pltpu.SemaphoreType.DMA((n,))],
    )(*bufs)


def _quad_exchange(bufs, scatter):
    n = len(bufs)

    def body(*refs):
        ins, outs = refs[:n], refs[n:2 * n]
        send_sems, recv_sems = refs[2 * n], refs[2 * n + 1]
        x, y, c = _pos()
        me = 2 * x + y
        chips = _other_chips(x, y)
        copies = []
        for k, (src, dst) in enumerate(zip(ins, outs)):
            for j, (cx, cy) in enumerate(chips):
                piece = src.at[2 * cx + cy] if scatter[k] else src
                cp = _remote(piece, dst.at[me], send_sems.at[3 * k + j], recv_sems.at[3 * k + j], (cx, cy, c))
                cp.start()
                copies.append(cp)
        for k, (src, dst) in enumerate(zip(ins, outs)):
            for j, (cx, cy) in enumerate(chips):
                blk = dst.at[2 * cx + cy]
                _remote(blk, blk, send_sems.at[3 * k + j], recv_sems.at[3 * k + j], (cx, cy, c)).wait_recv()
        for cp in copies:
            cp.wait_send()

    return pl.pallas_call(
        body, name="quad_exchange", in_specs=[ANY] * n, out_specs=[ANY] * n,
        out_shape=[S((4,) + (b.shape[1:] if sc else b.shape), b.dtype) for b, sc in zip(bufs, scatter)],
        scratch_shapes=[pltpu.SemaphoreType.DMA((3 * n,)), pltpu.SemaphoreType.DMA((3 * n,))],
    )(*bufs)


def _pair_gather(bufs):
    n = len(bufs)

    def body(*refs):
        ins, outs = refs[:n], refs[n:2 * n]
        send_sems, recv_sems = refs[2 * n], refs[2 * n + 1]
        x, y, c = _pos()
        copies = []
        for k, buf in enumerate(outs):
            mine = _half(buf, c, buf.shape[0] // 2)
            cp = _remote(mine, mine, send_sems.at[k], recv_sems.at[k], (x, y, 1 - c))
            cp.start()
            copies.append(cp)
        for k, buf in enumerate(outs):
            theirs = _half(buf, 1 - c, buf.shape[0] // 2)
            _remote(theirs, theirs, send_sems.at[k], recv_sems.at[k], (x, y, 1 - c)).wait_recv()
        for cp in copies:
            cp.wait_send()

    return pl.pallas_call(
        body, name="pair_gather", in_specs=[ANY] * n, out_specs=[ANY] * n,
        out_shape=[S(b.shape, b.dtype) for b in bufs], input_output_aliases={k: k for k in range(n)},
        scratch_shapes=[pltpu.SemaphoreType.DMA((n,)), pltpu.SemaphoreType.DMA((n,))],
    )(*bufs)


def _row_tile(rows, cols, mult=8):
    best = mult
    for t in range(mult, rows + 1, mult):
        if rows % t == 0 and t * cols * 4 <= (1 << 20):
            best = t
    return best


def _add_own_half(name, full, got, c, out_dtype):
    three = len(full.shape) == 3
    lead = full.shape[0] if three else 1
    rows, cols = full.shape[-2], full.shape[-1]
    hr = rows // 2
    tr = _row_tile(hr, cols, 16 if out_dtype == jnp.bfloat16 else 8)
    per = hr // tr

    def body(c_ref, a_ref, b_ref, o_ref):
        o_ref[...] = (a_ref[...] + b_ref[...]).astype(out_dtype)

    if three:
        a_spec = pl.BlockSpec((1, tr, cols), lambda s, i, c_ref: (s, c_ref[0] * per + i, 0))
        o_spec = pl.BlockSpec((1, tr, cols), lambda s, i, c_ref: (s, i, 0))
    else:
        a_spec = pl.BlockSpec((tr, cols), lambda s, i, c_ref: (c_ref[0] * per + i, 0))
        o_spec = pl.BlockSpec((tr, cols), lambda s, i, c_ref: (i, 0))
    return pl.pallas_call(
        body, name=name,
        grid_spec=pltpu.PrefetchScalarGridSpec(num_scalar_prefetch=1, grid=(lead, per), in_specs=[a_spec, o_spec],
                                               out_specs=o_spec),
        out_shape=S(got.shape, out_dtype), compiler_params=_params(2),
    )(jnp.reshape(c, (1,)).astype(jnp.int32), full, got)


def _sum_slots(name, own, slots, me, c):
    _, rows, cols = slots.shape
    tr = _row_tile(rows, cols, 16 if slots.dtype == jnp.bfloat16 else 8)
    per = rows // tr
    three = len(own.shape) == 3

    def body(p_ref, own_ref, s0, s1, s2, s3, o_ref):
        mine = own_ref[0] if three else own_ref[...]
        acc = None
        for j, s_ref in enumerate((s0, s1, s2, s3)):
            v = jnp.where(p_ref[0] == j, mine, s_ref[0]).astype(F32)
            acc = v if acc is None else acc + v
        o_ref[...] = acc

    def slot_spec(j):
        return pl.BlockSpec((1, tr, cols), lambda i, p: (jnp.where(p[0] == j, (j + 1) % 4, j), i, 0))

    own_spec = (pl.BlockSpec((1, tr, cols), lambda i, p: (p[0], i, 0)) if three
                else pl.BlockSpec((tr, cols), lambda i, p: (i, 0)))
    return pl.pallas_call(
        body, name=name,
        grid_spec=pltpu.PrefetchScalarGridSpec(
            num_scalar_prefetch=1, grid=(per,), in_specs=[own_spec] + [slot_spec(j) for j in range(4)],
            out_specs=pl.BlockSpec((tr, cols), lambda i, p: (p[1] * per + i, 0))),
        out_shape=S((2 * rows, cols), F32), compiler_params=_params(),
    )(jnp.stack([me, c]).astype(jnp.int32), own, slots, slots, slots, slots)


SMALL = (("pre_mix_norm", 1024), ("lru_conv_w", 4096), ("lru_conv_b", 1024), ("lru_wa", 65536), ("lru_ba", 1024),
         ("lru_wx", 65536), ("lru_bx", 1024), ("lru_lambda", 1024), ("lru_out_norm", 1024), ("ssd_conv_w", 6144),
         ("ssd_conv_b", 1536), ("ssd_dt_bias", 16), ("ssd_a_log", 16), ("ssd_d", 16), ("ssd_out_norm", 1024),
         ("post_mix_norm", 1024), ("pre_ffn_norm", 1024), ("post_ffn_norm", 1024))
SMALL_ROWS = 1200
BIG = ("w_in", "w_out", "w_gate", "w_up", "w_down")


def _diag4(w):
    eye = jnp.eye(4, dtype=w.dtype).reshape(1, 4, 1, 4, 1)
    return (w.reshape(4, 4, BW, 1, BW) * eye).reshape(4, 4 * BW, 4 * BW)


def _undiag4(w4):
    w4 = w4.reshape(4, 4, BW, 4, BW)
    return jnp.stack([w4[:, a, :, a, :] for a in range(4)], axis=1).reshape(NBLK, BW, BW)


def _pack_small(parts, loss):
    flat = jnp.concatenate([parts[name].reshape(-1).astype(F32) for name, _ in SMALL])
    flat = jnp.pad(flat, (0, SMALL_ROWS * 128 - 1 - flat.shape[0]))
    return jnp.concatenate([flat, loss.reshape(1)]).reshape(SMALL_ROWS, 128)


def _unpack_small(buf):
    flat = buf.reshape(-1)
    out, off = {}, 0
    for name, size in SMALL:
        out[name] = flat[off:off + size]
        off += size
    return out


def _gather_weights(w_in, w_out, w_gate, w_up, w_down, lru_conv_w, ssd_conv_w):
    conv = jnp.concatenate([lru_conv_w.reshape(-1), ssd_conv_w.reshape(-1)]).astype(F32)
    hi = conv.astype(jnp.bfloat16)
    mid = (conv - hi.astype(F32)).astype(jnp.bfloat16)
    lo = (conv - hi.astype(F32) - mid.astype(F32)).astype(jnp.bfloat16)
    terms = jnp.concatenate([hi, mid, lo])
    n_terms = terms.shape[0]
    conv_rows = -(-n_terms // (128 * 32)) * 32
    terms = jnp.pad(terms, (0, conv_rows * 128 - n_terms)).reshape(conv_rows, 128)
    own = [w.astype(jnp.bfloat16) for w in (w_in, w_out, w_gate, w_up, w_down)] + [terms]
    got = _allgather_weights(own)
    chip = 2 * lax.axis_index("x") + lax.axis_index("y")
    here = (jnp.arange(4) == chip).reshape(4, 1, 1)
    full = [jnp.where(here, o[None], g) for o, g in zip(own, got)]
    cols = lambda f: f.transpose(1, 0, 2).reshape(f.shape[1], 4 * f.shape[2])
    rows = lambda f: f.reshape(4 * f.shape[1], f.shape[2])
    win_f, wout_f, wg_f, wu_f, wd_f = cols(full[0]), rows(full[1]), cols(full[2]), cols(full[3]), rows(full[4])
    t3 = full[5].reshape(4, -1)[:, :n_terms].reshape(4, 3, -1).astype(F32)
    conv_f = (t3[:, 0] + t3[:, 1]) + t3[:, 2]
    n1 = lru_conv_w.size
    lcw = conv_f[:, :n1].reshape(4, CONV_K, -1).transpose(1, 0, 2).reshape(CONV_K, LW)
    scw = conv_f[:, n1:].reshape(4, CONV_K, -1).transpose(1, 0, 2).reshape(CONV_K, XBC)
    return win_f, wout_f, wg_f, wu_f, wd_f, lcw, scw


def _local_step(x, tgt, win_f, wout_f, wg_f, wu_f, wd_f, lcw, scw, sp):
    mm = lambda w: w.astype(BF)
    wcat = jnp.concatenate([mm(win_f), jnp.zeros((D, PC - IN_COLS), BF)], axis=1)
    row = lambda v: v.reshape(1, -1).astype(F32)
    p_lru = jnp.concatenate([lcw, row(sp["lru_conv_b"]), row(sp["lru_ba"]), row(sp["lru_bx"]), row(sp["lru_lambda"]),
                             row(sp["lru_out_norm"]), jnp.zeros((7, LW), F32)], axis=0)
    wa4, wx4 = mm(_diag4(sp["lru_wa"][0])), mm(_diag4(sp["lru_wx"][0]))
    wa4T, wx4T = wa4.transpose(0, 2, 1), wx4.transpose(0, 2, 1)
    cw_ssd = jnp.concatenate([scw, row(sp["ssd_conv_b"]), jnp.zeros((3, XBC), F32)], axis=0)
    padh = lambda v: jnp.pad(row(v), ((0, 0), (0, DTP - NH)))
    hp_ssd = jnp.concatenate([padh(sp["ssd_dt_bias"]), padh(sp["ssd_a_log"]), padh(sp["ssd_d"]), jnp.zeros((5, DTP), F32)], axis=0)
    g0, g_ssd = row(sp["pre_mix_norm"]), row(sp["ssd_out_norm"])
    g_pm, g_pf, g_pff = row(sp["post_mix_norm"]), row(sp["pre_ffn_norm"]), row(sp["post_ffn_norm"])
    wout, wg, wu, wd = mm(wout_f), mm(wg_f), mm(wu_f), mm(wd_f)

    h0, lxr, lg, z, xbcr, dtr = _inproj(x, g0, wcat)
    h, ylru = _lru_fwd(lxr, lg, p_lru, wa4, wx4)
    y, yssd, states = _ssd_fwd(xbcr, z, dtr, cw_ssd, hp_ssd, g_ssd)
    mix, x1, h2 = _outproj(ylru, yssd, x, wout, g_pm, g_pf)
    gate, up, act, df, dx2, st_ffn = _ffn_fwd(h2, x1, tgt, wg, wu, wd, g_pff)
    dgate, dup, dh2 = _ffn_bwd(df, gate, up, wd.T, wg.T, wu.T)
    dx1, dmix, dyl, dys, st_mix = _mix_bwd(dh2, x1, dx2, mix, wout.T, g_pf, g_pm)
    dlx, dlg, st_lru, dwa4, dwx4 = _lru_bwd(dyl, lxr, lg, h, p_lru, wa4, wx4, wa4T, wx4T)
    dxbc, dz, ddt, cst, hst, gst = _ssd_bwd(dys, xbcr, z, dtr, y, states, cw_ssd, hp_ssd, g_ssd)
    gx, st_in = _inproj_bwd(dlx, dlg, dz, dxbc, ddt, x, dx1, wcat.T, g0)

    dwd = _wgrad("wgrad_down", act, [df])
    dwgu = _wgrad("wgrad_gate_up", h2, [dgate, dup])
    dwo_l = _wgrad("wgrad_out_lru", ylru, [dmix])
    dwo_s = _wgrad("wgrad_out_ssd", yssd, [dmix])
    dwin = _wgrad("wgrad_in", h0, [dlx, dlg, dz, dxbc, ddt])[:, :IN_COLS]
    big = {"w_in": dwin, "w_out": jnp.concatenate([dwo_l, dwo_s], axis=0), "w_gate": dwgu[:, :DFF], "w_up": dwgu[:, DFF:],
           "w_down": dwd}
    small = {
        "pre_mix_norm": st_in[0], "lru_conv_w": st_lru[0:4], "lru_conv_b": st_lru[4], "lru_wa": _undiag4(dwa4),
        "lru_ba": st_lru[5], "lru_wx": _undiag4(dwx4), "lru_bx": st_lru[6], "lru_lambda": st_lru[7],
        "lru_out_norm": st_lru[8], "ssd_conv_w": cst[0:4], "ssd_conv_b": cst[4], "ssd_dt_bias": hst[0, :NH],
        "ssd_a_log": hst[1, :NH], "ssd_d": hst[2, :NH], "ssd_out_norm": gst[0], "post_mix_norm": st_mix[1],
        "pre_ffn_norm": st_mix[0], "post_ffn_norm": st_ffn[1],
    }
    return jnp.sum(st_ffn[0]), gx, big, small


def _reduce_grads(big, small_buf):
    c = lax.axis_index("c")
    bufs = []
    for name in BIG:
        g = big[name]
        if name in ("w_in", "w_gate", "w_up"):
            bufs.append(g.reshape(g.shape[0], 4, g.shape[1] // 4).transpose(1, 0, 2))
        else:
            bufs.append(g.reshape(4, g.shape[0] // 4, g.shape[1]))
    bufs.append(small_buf)
    me = 2 * lax.axis_index("x") + lax.axis_index("y")
    got = _pair_exchange(bufs)
    wire = [jnp.bfloat16] * len(BIG) + [F32]
    part = [_add_own_half("pair_add_%d" % k, b, r, c, dt) for k, (b, r, dt) in enumerate(zip(bufs, got, wire))]
    slots = _quad_exchange(part, [True] * len(BIG) + [False])
    red = [_sum_slots("quad_sum_%d" % k, p, s, me, c) for k, (p, s) in enumerate(zip(part, slots))]
    return _pair_gather(red)


def kernel(x, pre_mix_norm, w_in, lru_conv_w, lru_conv_b, lru_wa, lru_ba, lru_wx, lru_bx, lru_lambda, lru_out_norm, ssd_conv_w, ssd_conv_b, ssd_dt_bias, ssd_a_log, ssd_d, ssd_out_norm, w_out, post_mix_norm, pre_ffn_norm, w_gate, w_up, w_down, post_ffn_norm, loss_target, m_pre_mix_norm, m_w_in, m_lru_conv_w, m_lru_conv_b, m_lru_wa, m_lru_ba, m_lru_wx, m_lru_bx, m_lru_lambda, m_lru_out_norm, m_ssd_conv_w, m_ssd_conv_b, m_ssd_dt_bias, m_ssd_a_log, m_ssd_d, m_ssd_out_norm, m_w_out, m_post_mix_norm, m_pre_ffn_norm, m_w_gate, m_w_up, m_w_down, m_post_ffn_norm, v_pre_mix_norm, v_w_in, v_lru_conv_w, v_lru_conv_b, v_lru_wa, v_lru_ba, v_lru_wx, v_lru_bx, v_lru_lambda, v_lru_out_norm, v_ssd_conv_w, v_ssd_conv_b, v_ssd_dt_bias, v_ssd_a_log, v_ssd_d, v_ssd_out_norm, v_w_out, v_post_mix_norm, v_pre_ffn_norm, v_w_gate, v_w_up, v_w_down, v_post_ffn_norm):
    args = dict(locals())
    names = [n for n, _ in SMALL] + list(BIG)
    w = {n: args[n] for n in names}
    m = {n: args["m_" + n] for n in names}
    v = {n: args["v_" + n] for n in names}
    chip = 2 * lax.axis_index("x") + lax.axis_index("y")

    win_f, wout_f, wg_f, wu_f, wd_f, lcw, scw = _gather_weights(w_in[0], w_out[0], w_gate[0], w_up[0], w_down[0],
                                                                lru_conv_w[0], ssd_conv_w[0])
    sp = {n: w[n] for n, _ in SMALL}
    loss_part, gx, big, small = _local_step(x[0], loss_target[0], win_f, wout_f, wg_f, wu_f, wd_f, lcw, scw, sp)
    red = _reduce_grads(big, _pack_small(small, loss_part))
    gsmall = _unpack_small(red[len(BIG)])
    loss = red[len(BIG)][SMALL_ROWS - 1, 127]

    grads, delta, new_m, new_v = {}, {}, {}, {}
    for k, n in enumerate(BIG):
        g = red[k]
        d, nm, nv = _adamw("adamw_" + n, w[n][0], g, m[n][0], v[n][0])
        grads[n], delta[n], new_m[n], new_v[n] = g[None], d[None], nm[None], nv[None]

    def local_part(n, full):
        shape = w[n].shape
        if n in ("lru_conv_w", "ssd_conv_w"):
            per = shape[-1]
            return lax.dynamic_slice_in_dim(full.reshape(CONV_K, -1), chip * per, per, axis=1).reshape(shape)
        return full.reshape(shape)

    gl = {n: local_part(n, gsmall[n]) for n, _ in SMALL}

    def pack_local(d):
        flat = jnp.concatenate([d[n].reshape(-1) for n, _ in SMALL])
        rows = -(-flat.shape[0] // (128 * 8)) * 8
        return jnp.pad(flat, (0, rows * 128 - flat.shape[0])).reshape(rows, 128), flat.shape[0]

    wp, nflat = pack_local({n: w[n] for n, _ in SMALL})
    gp, _ = pack_local(gl)
    mp, _ = pack_local({n: m[n] for n, _ in SMALL})
    vp, _ = pack_local({n: v[n] for n, _ in SMALL})
    pad_mask = (jnp.arange(wp.size).reshape(wp.shape) >= nflat)
    dp_, nmp, nvp = _adamw("adamw_small", wp, gp, mp, jnp.where(pad_mask, 1.0, vp))
    off = 0
    for n, _ in SMALL:
        size = w[n].size
        grads[n] = gl[n]
        delta[n] = dp_.reshape(-1)[off:off + size].reshape(w[n].shape)
        new_m[n] = nmp.reshape(-1)[off:off + size].reshape(w[n].shape)
        new_v[n] = nvp.reshape(-1)[off:off + size].reshape(w[n].shape)
        off += size

    order = ["pre_mix_norm", "w_in", "lru_conv_w", "lru_conv_b", "lru_wa", "lru_ba", "lru_wx", "lru_bx", "lru_lambda",
             "lru_out_norm", "ssd_conv_w", "ssd_conv_b", "ssd_dt_bias", "ssd_a_log", "ssd_d", "ssd_out_norm", "w_out",
             "post_mix_norm", "pre_ffn_norm", "w_gate", "w_up", "w_down", "post_ffn_norm"]
    return (loss, gx[None], *[grads[n] for n in order], *[delta[n] for n in order],
            *[new_m[n] for n in order], *[new_v[n] for n in order])
```

```python
import functools

import jax
import jax.numpy as jnp
from jax import lax
from jax.experimental import pallas as pl
from jax.experimental.pallas import tpu as pltpu

F32 = jnp.float32
BF = jnp.bfloat16

D = 1024
LW = 1024
NBLK = 16
BW = 64
SI = 1024
NH = 16
HD = 64
NG = 2
HPG = NH // NG
NS = 128
CH = 128
XBC = SI + 2 * NG * NS
DTP = 128
PC = 3 * 1024 + XBC + DTP
DFF = 2816
IN_COLS = 4624
EPS = 1e-6
LRU_C = 8.0
CONV_K = 4
TT = 256
VMEM_LIMIT = 56 * 1024 * 1024

ADAM_LR, ADAM_B1, ADAM_B2, ADAM_EPS, ADAM_WD, ADAM_STEP = 0.001, 0.9, 0.999, 1e-08, 0.01, 10

MESH = pl.DeviceIdType.MESH


def _mm(a, b):
    return jnp.dot(a.astype(BF), b.astype(BF), preferred_element_type=F32)


def _mm_nt(a, b):
    return lax.dot_general(a.astype(BF), b.astype(BF), (((1,), (1,)), ((), ())), preferred_element_type=F32)


def _mm_tn(a, b):
    return lax.dot_general(a.astype(BF), b.astype(BF), (((0,), (0,)), ((), ())), preferred_element_type=F32)


def _sigmoid(x):
    return 0.5 * jnp.tanh(0.5 * x) + 0.5


def _softplus(x):
    return jnp.maximum(x, 0.0) + jnp.log1p(jnp.exp(-jnp.abs(x)))


_GELU_C = 0.7978845608028654
_GELU_K = 0.044715


def _gelu(x):
    t = jnp.tanh(_GELU_C * (x + _GELU_K * x * x * x))
    return 0.5 * x * (1.0 + t)


def _gelu_grad(x):
    t = jnp.tanh(_GELU_C * (x + _GELU_K * x * x * x))
    return 0.5 * (1.0 + t) + 0.5 * x * (1.0 - t * t) * _GELU_C * (1.0 + 3.0 * _GELU_K * x * x)


def _rms_fwd(x, g):
    r = lax.rsqrt(jnp.mean(x * x, axis=-1, keepdims=True) + EPS)
    return x * r * g


def _rms_bwd(x, g, dy):
    r = lax.rsqrt(jnp.mean(x * x, axis=-1, keepdims=True) + EPS)
    xh = x * r
    dxh = dy * g
    dg = jnp.sum(dy * xh, axis=0, keepdims=True)
    dx = r * (dxh - xh * jnp.mean(dxh * xh, axis=-1, keepdims=True))
    return dx, dg


def _sum_all(x):
    return jnp.sum(jnp.sum(x, axis=1, keepdims=True), axis=0, keepdims=True)


def _cumsum_rows(x, n):
    row = lax.broadcasted_iota(jnp.int32, x.shape, 0)
    k = 1
    while k < n:
        x = x + jnp.where(row >= k, pltpu.roll(x, k, 0), 0.0)
        k *= 2
    return x


def _rev_cumsum_rows(x, n):
    row = lax.broadcasted_iota(jnp.int32, x.shape, 0)
    k = 1
    while k < n:
        x = x + jnp.where(row < n - k, pltpu.roll(x, n - k, 0), 0.0)
        k *= 2
    return x


def _load_once(pairs, sem):
    @pl.when(pl.program_id(0) == 0)
    def _():
        for k, (src, dst) in enumerate(pairs):
            pltpu.make_async_copy(src, dst, sem.at[k]).start()
        for k, (src, dst) in enumerate(pairs):
            pltpu.make_async_copy(src, dst, sem.at[k]).wait()


def _params(n_axes=1):
    return pltpu.CompilerParams(dimension_semantics=("arbitrary",) * n_axes, vmem_limit_bytes=VMEM_LIMIT)


def _rows(n, width, rev_of=None):
    if rev_of is None:
        return pl.BlockSpec((n, width), lambda i: (i, 0))
    return pl.BlockSpec((n, width), lambda i: (rev_of - 1 - i, 0))


def _whole(shape):
    nd = len(shape)
    return pl.BlockSpec(shape, lambda i: (0,) * nd)


ANY = pl.BlockSpec(memory_space=pl.ANY)
S = jax.ShapeDtypeStruct


def _inproj(x, g0, wcat):
    T = x.shape[0]

    def body(x_ref, g_ref, w_hbm, h0_ref, lx_ref, lg_ref, z_ref, xbc_ref, dt_ref, w_vm, sem):
        _load_once([(w_hbm, w_vm)], sem)
        h = _rms_fwd(x_ref[...], g_ref[...]).astype(BF)
        h0_ref[...] = h
        lx_ref[...] = jnp.dot(h, w_vm[:, 0:1024], preferred_element_type=F32)
        lg_ref[...] = jnp.dot(h, w_vm[:, 1024:2048], preferred_element_type=F32)
        z_ref[...] = jnp.dot(h, w_vm[:, 2048:3072], preferred_element_type=F32)
        xbc_ref[...] = jnp.dot(h, w_vm[:, 3072:3072 + XBC], preferred_element_type=F32)
        dt_ref[...] = jnp.dot(h, w_vm[:, 3072 + XBC:PC], preferred_element_type=F32)

    return pl.pallas_call(
        body, name="inproj", grid=(T // TT,),
        in_specs=[_rows(TT, D), _whole((1, D)), ANY],
        out_specs=[_rows(TT, D), _rows(TT, 1024), _rows(TT, 1024), _rows(TT, 1024), _rows(TT, XBC), _rows(TT, DTP)],
        out_shape=[S((T, D), BF), S((T, 1024), F32), S((T, 1024), F32), S((T, 1024), F32), S((T, XBC), F32), S((T, DTP), F32)],
        scratch_shapes=[pltpu.VMEM((D, PC), BF), pltpu.SemaphoreType.DMA((1,))],
        compiler_params=_params(),
    )(x, g0, wcat)


def _blockdiag_mm(v, w4_ref):
    return jnp.concatenate([_mm(v[:, 256 * j:256 * (j + 1)], w4_ref[j]) for j in range(4)], axis=1)


def _lru_gates(lx, p_ref, wa_ref, wx_ref):
    r = _sigmoid(_blockdiag_mm(lx, wa_ref) + p_ref[5:6, :])
    i = _sigmoid(_blockdiag_mm(lx, wx_ref) + p_ref[6:7, :])
    sp = _softplus(-p_ref[7:8, :])
    la = -LRU_C * r * sp
    a = jnp.exp(la)
    th = jnp.tanh(la)
    mult = jnp.sqrt(-2.0 * th / (1.0 - th))
    return r, i, sp, a, mult


def _conv_from(xp_ref, p_ref, n):
    taps = [xp_ref[pl.ds(8 - CONV_K + 1 + k, n), :] for k in range(CONV_K)]
    acc = p_ref[4:5, :] + p_ref[0:1, :] * taps[0]
    for k in range(1, CONV_K):
        acc = acc + p_ref[k:k + 1, :] * taps[k]
    return acc, taps


def _lru_fwd(lxr, lg, p_lru, wa4, wx4):
    T = lxr.shape[0]

    def body(lx_ref, lg_ref, p_ref, wa_ref, wx_ref, h_ref, y_ref, xp, a_s, u_s, hc):
        @pl.when(pl.program_id(0) == 0)
        def _():
            xp[0:8, :] = jnp.zeros((8, LW), F32)
            hc[...] = jnp.zeros_like(hc)

        xp[8:8 + TT, :] = lx_ref[...]
        lx, _ = _conv_from(xp, p_ref, TT)
        xp[0:8, :] = xp[TT:TT + 8, :]
        r, i, sp, a, mult = _lru_gates(lx, p_ref, wa_ref, wx_ref)
        a_s[...] = a
        u_s[...] = mult * (i * lx)

        def step(t, h):
            h = a_s[pl.ds(t, 1), :] * h + u_s[pl.ds(t, 1), :]
            h_ref[pl.ds(t, 1), :] = h
            return h

        hc[0:1, :] = lax.fori_loop(0, TT, step, hc[0:1, :], unroll=8)
        gated = h_ref[...] * _gelu(lg_ref[...])
        y_ref[...] = _rms_fwd(gated, p_ref[8:9, :]).astype(BF)

    return pl.pallas_call(
        body, name="lru_fwd", grid=(T // TT,),
        in_specs=[_rows(TT, LW), _rows(TT, LW), _whole((16, LW)), _whole((4, 256, 256)), _whole((4, 256, 256))],
        out_specs=[_rows(TT, LW), _rows(TT, LW)],
        out_shape=[S((T, LW), F32), S((T, LW), BF)],
        scratch_shapes=[pltpu.VMEM((TT + 8, LW), F32), pltpu.VMEM((TT, LW), F32), pltpu.VMEM((TT, LW), F32),
                        pltpu.VMEM((8, LW), F32)],
        compiler_params=_params(),
    )(lxr, lg, p_lru, wa4, wx4)


def _ssd_prep(xp, xr_ref, dt_ref, cw_ref, hp_ref):
    xp[8:8 + CH, :] = xr_ref[...]
    cv, taps = _conv_from(xp, cw_ref, CH)
    sg = _sigmoid(cv)
    xbc = cv * sg
    lane = lax.broadcasted_iota(jnp.int32, (CH, DTP), 1)
    raw = dt_ref[...] + hp_ref[0:1, :]
    dtv = jnp.where(lane < NH, _softplus(raw), 0.0)
    A = jnp.where(lane[0:1, :] < NH, -jnp.exp(hp_ref[1:2, :]), 0.0)
    cs = _cumsum_rows(dtv * A, CH)
    return cv, sg, xbc, raw, dtv, A, cs, taps


def _per_head_lanes(v):
    r = v.shape[0]
    first = lax.broadcasted_iota(jnp.int32, (r, 2 * HD), 1) < HD
    pairs = [jnp.where(first, jnp.broadcast_to(v[:, 2 * j:2 * j + 1], (r, 2 * HD)),
                       jnp.broadcast_to(v[:, 2 * j + 1:2 * j + 2], (r, 2 * HD))) for j in range(NH // 2)]
    return jnp.concatenate(pairs, axis=1)


def _per_head_rows(col, g):
    return jnp.concatenate([jnp.broadcast_to(col[g * HPG + k:g * HPG + k + 1, :], (HD, NS)) for k in range(HPG)], axis=0)


def _ssd_decays(cs):
    csT = cs.T
    cl = cs[CH - 1:CH, :]
    E_x = _per_head_lanes(jnp.exp(cs))
    dsm = jnp.exp(cl - cs)
    ds_x = _per_head_lanes(dsm)
    El_rows = jnp.broadcast_to(jnp.exp(csT[0:NH, CH - 1:CH]), (NH, NS))
    return csT, dsm, E_x, ds_x, El_rows


def _ssd_fwd(xbcr, z, dtr, cw_ssd, hp_ssd, g_ssd):
    T = xbcr.shape[0]
    NC = T // CH

    def body(xr_ref, z_ref, dt_ref, cw_ref, hp_ref, g_ref, y_ref, yn_ref, st_ref, xp, st):
        @pl.when(pl.program_id(0) == 0)
        def _():
            xp[0:8, :] = jnp.zeros((8, XBC), F32)
            st[...] = jnp.zeros_like(st)

        cv, sg, xbc, raw, dtv, A, cs, _ = _ssd_prep(xp, xr_ref, dt_ref, cw_ref, hp_ref)
        xp[0:8, :] = xp[CH:CH + 8, :]
        st_ref[0] = st[...]
        csT, dsm, E_x, ds_x, El_rows = _ssd_decays(cs)
        X = xbc[:, 0:SI]
        xs = X * _per_head_lanes(dtv)
        xsd = (xs * ds_x).astype(BF)
        DX = _per_head_lanes(hp_ref[...])[2:3, :] * X
        tril = lax.broadcasted_iota(jnp.int32, (CH, CH), 0) >= lax.broadcasted_iota(jnp.int32, (CH, CH), 1)
        first = lax.broadcasted_iota(jnp.int32, (CH, 2 * HD), 1) < HD
        GW = HPG * HD
        for g in range(NG):
            Bg = xbc[:, SI + NS * g:SI + NS * (g + 1)].astype(BF)
            Cg = xbc[:, SI + NG * NS + NS * g:SI + NG * NS + NS * (g + 1)].astype(BF)
            G = _mm_nt(Cg, Bg)
            Sg = st[GW * g:GW * (g + 1), :]
            Yo = _mm_nt(Cg, Sg) * E_x[:, GW * g:GW * (g + 1)]
            st[GW * g:GW * (g + 1), :] = _per_head_rows(El_rows, g) * Sg + _mm_tn(xsd[:, GW * g:GW * (g + 1)], Bg)
            for jj in range(HPG // 2):
                j = g * (HPG // 2) + jj
                ps = slice(2 * HD * j, 2 * HD * (j + 1))
                xs_pair = xs[:, ps]
                acc = Yo[:, 2 * HD * jj:2 * HD * (jj + 1)] + DX[:, ps]
                for e in range(2):
                    h = 2 * j + e
                    Lm = jnp.exp(jnp.where(tril, cs[:, h:h + 1] - csT[h:h + 1, :], -1e30))
                    acc = acc + _mm(G * Lm, jnp.where(first if e == 0 else ~first, xs_pair, 0.0))
                y_ref[:, ps] = acc
        zz = z_ref[...]
        gated = y_ref[...] * (zz * _sigmoid(zz))
        yn_ref[...] = _rms_fwd(gated, g_ref[...]).astype(BF)

    return pl.pallas_call(
        body, name="ssd_fwd", grid=(NC,),
        in_specs=[_rows(CH, XBC), _rows(CH, SI), _rows(CH, DTP), _whole((8, XBC)), _whole((8, DTP)), _whole((1, SI))],
        out_specs=[_rows(CH, SI), _rows(CH, SI), pl.BlockSpec((1, NH * HD, NS), lambda i: (i, 0, 0))],
        out_shape=[S((T, SI), F32), S((T, SI), BF), S((NC, NH * HD, NS), F32)],
        scratch_shapes=[pltpu.VMEM((CH + 8, XBC), F32), pltpu.VMEM((NH * HD, NS), F32)],
        compiler_params=_params(),
    )(xbcr, z, dtr, cw_ssd, hp_ssd, g_ssd)


def _outproj(ylru, yssd, x, wout, g_pm, g_pf):
    T = x.shape[0]

    def body(yl_ref, ys_ref, x_ref, w_hbm, gpm_ref, gpf_ref, mix_ref, x1_ref, h2_ref, w_vm, sem):
        _load_once([(w_hbm, w_vm)], sem)
        mix = (jnp.dot(yl_ref[...], w_vm[0:LW, :], preferred_element_type=F32)
               + jnp.dot(ys_ref[...], w_vm[LW:LW + SI, :], preferred_element_type=F32))
        mix_ref[...] = mix
        x1 = x_ref[...] + _rms_fwd(mix, gpm_ref[...])
        x1_ref[...] = x1
        h2_ref[...] = _rms_fwd(x1, gpf_ref[...]).astype(BF)

    return pl.pallas_call(
        body, name="outproj", grid=(T // TT,),
        in_specs=[_rows(TT, LW), _rows(TT, SI), _rows(TT, D), ANY, _whole((1, D)), _whole((1, D))],
        out_specs=[_rows(TT, D), _rows(TT, D), _rows(TT, D)],
        out_shape=[S((T, D), F32), S((T, D), F32), S((T, D), BF)],
        scratch_shapes=[pltpu.VMEM((LW + SI, D), BF), pltpu.SemaphoreType.DMA((1,))],
        compiler_params=_params(),
    )(ylru, yssd, x, wout, g_pm, g_pf)


def _ffn_fwd(h2, x1, tgt, wg, wu, wd, g_pff):
    T = x1.shape[0]

    def body(h2_ref, x1_ref, t_ref, wg_hbm, wu_hbm, wd_hbm, g_ref,
             gate_ref, up_ref, act_ref, df_ref, dx2_ref, st_ref, wg_vm, wu_vm, wd_vm, sem):
        _load_once([(wg_hbm, wg_vm), (wu_hbm, wu_vm), (wd_hbm, wd_vm)], sem)

        @pl.when(pl.program_id(0) == 0)
        def _():
            st_ref[...] = jnp.zeros_like(st_ref)

        h2 = h2_ref[...]
        gate = jnp.dot(h2, wg_vm[...], preferred_element_type=F32)
        up = jnp.dot(h2, wu_vm[...], preferred_element_type=F32)
        gate_ref[...] = gate
        up_ref[...] = up
        act = (gate * _sigmoid(gate) * up).astype(BF)
        act_ref[...] = act
        f = jnp.dot(act, wd_vm[...], preferred_element_type=F32)
        g = g_ref[...]
        x2 = x1_ref[...] + _rms_fwd(f, g)
        err = x2 - t_ref[...]
        st_ref[0:1, :] += 0.5 * jnp.sum(err * err, axis=0, keepdims=True) * (1.0 / D)
        dx2 = err * (1.0 / D)
        dx2_ref[...] = dx2
        df, dg = _rms_bwd(f, g, dx2)
        df_ref[...] = df.astype(BF)
        st_ref[1:2, :] += dg

    return pl.pallas_call(
        body, name="ffn_fwd", grid=(T // TT,),
        in_specs=[_rows(TT, D), _rows(TT, D), _rows(TT, D), ANY, ANY, ANY, _whole((1, D))],
        out_specs=[_rows(TT, DFF), _rows(TT, DFF), _rows(TT, DFF), _rows(TT, D), _rows(TT, D), _whole((8, D))],
        out_shape=[S((T, DFF), F32), S((T, DFF), F32), S((T, DFF), BF), S((T, D), BF), S((T, D), F32), S((8, D), F32)],
        scratch_shapes=[pltpu.VMEM((D, DFF), BF), pltpu.VMEM((D, DFF), BF), pltpu.VMEM((DFF, D), BF),
                        pltpu.SemaphoreType.DMA((3,))],
        compiler_params=_params(),
    )(h2, x1, tgt, wg, wu, wd, g_pff)


def _ffn_bwd(df, gate, up, wdT, wgT, wuT):
    T = df.shape[0]

    def body(df_ref, gate_ref, up_ref, wd_hbm, wg_hbm, wu_hbm, dgate_ref, dup_ref, dh2_ref, wd_vm, wg_vm, wu_vm, sem):
        _load_once([(wd_hbm, wd_vm), (wg_hbm, wg_vm), (wu_hbm, wu_vm)], sem)
        dact = jnp.dot(df_ref[...], wd_vm[...], preferred_element_type=F32)
        gate = gate_ref[...]
        s = _sigmoid(gate)
        dup = (dact * (gate * s)).astype(BF)
        dgate = (dact * up_ref[...] * (s + gate * s * (1.0 - s))).astype(BF)
        dup_ref[...] = dup
        dgate_ref[...] = dgate
        dh2_ref[...] = (jnp.dot(dgate, wg_vm[...], preferred_element_type=F32)
                        + jnp.dot(dup, wu_vm[...], preferred_element_type=F32))

    return pl.pallas_call(
        body, name="ffn_bwd", grid=(T // TT,),
        in_specs=[_rows(TT, D), _rows(TT, DFF), _rows(TT, DFF), ANY, ANY, ANY],
        out_specs=[_rows(TT, DFF), _rows(TT, DFF), _rows(TT, D)],
        out_shape=[S((T, DFF), BF), S((T, DFF), BF), S((T, D), F32)],
        scratch_shapes=[pltpu.VMEM((D, DFF), BF), pltpu.VMEM((DFF, D), BF), pltpu.VMEM((DFF, D), BF),
                        pltpu.SemaphoreType.DMA((3,))],
        compiler_params=_params(),
    )(df, gate, up, wdT, wgT, wuT)


def _mix_bwd(dh2, x1, dx2, mix, woutT, g_pf, g_pm):
    T = x1.shape[0]

    def body(dh2_ref, x1_ref, dx2_ref, mix_ref, w_hbm, gpf_ref, gpm_ref,
             dx1_ref, dmix_ref, dyl_ref, dys_ref, st_ref, w_vm, sem):
        _load_once([(w_hbm, w_vm)], sem)

        @pl.when(pl.program_id(0) == 0)
        def _():
            st_ref[...] = jnp.zeros_like(st_ref)

        dxa, dgpf = _rms_bwd(x1_ref[...], gpf_ref[...], dh2_ref[...])
        dx1 = dx2_ref[...] + dxa
        dx1_ref[...] = dx1
        dmix, dgpm = _rms_bwd(mix_ref[...], gpm_ref[...], dx1)
        dmix = dmix.astype(BF)
        dmix_ref[...] = dmix
        st_ref[0:1, :] += dgpf
        st_ref[1:2, :] += dgpm
        dyl_ref[...] = jnp.dot(dmix, w_vm[:, 0:LW], preferred_element_type=F32)
        dys_ref[...] = jnp.dot(dmix, w_vm[:, LW:LW + SI], preferred_element_type=F32)

    return pl.pallas_call(
        body, name="mix_bwd", grid=(T // TT,),
        in_specs=[_rows(TT, D), _rows(TT, D), _rows(TT, D), _rows(TT, D), ANY, _whole((1, D)), _whole((1, D))],
        out_specs=[_rows(TT, D), _rows(TT, D), _rows(TT, LW), _rows(TT, SI), _whole((8, D))],
        out_shape=[S((T, D), F32), S((T, D), BF), S((T, LW), F32), S((T, SI), F32), S((8, D), F32)],
        scratch_shapes=[pltpu.VMEM((D, LW + SI), BF), pltpu.SemaphoreType.DMA((1,))],
        compiler_params=_params(),
    )(dh2, x1, dx2, mix, woutT, g_pf, g_pm)


def _halo(width, n_tiles, tile):
    per = tile // 8
    return pl.BlockSpec((8, width), lambda i: (jnp.maximum((n_tiles - 1 - i) * per - 1, 0), 0))


def _lru_bwd(dy, lxr, lg, h, p_lru, wa4, wx4, wa4T, wx4T):
    T = dy.shape[0]
    NT = T // TT

    def body(dy_ref, lx_ref, lxh_ref, lg_ref, h_ref, hh_ref, p_ref, wa_ref, wx_ref, waT_ref, wxT_ref,
             dlx_ref, dlg_ref, st_ref, dwa_ref, dwx_ref, xp, hp, dp, a_s, d_s, g_s, cc):
        first = pl.program_id(0) == 0
        top = pl.program_id(0) == NT - 1

        @pl.when(first)
        def _():
            st_ref[...] = jnp.zeros_like(st_ref)
            dwa_ref[...] = jnp.zeros_like(dwa_ref)
            dwx_ref[...] = jnp.zeros_like(dwx_ref)
            dp[TT:TT + 8, :] = jnp.zeros((8, LW), F32)
            cc[...] = jnp.zeros_like(cc)

        keep = jnp.where(top, 0.0, 1.0)
        xp[0:8, :] = lxh_ref[...] * keep
        xp[8:8 + TT, :] = lx_ref[...]
        hp[0:8, :] = hh_ref[...] * keep
        hp[8:8 + TT, :] = h_ref[...]
        lx, taps = _conv_from(xp, p_ref, TT)
        r, i, sp, a, mult = _lru_gates(lx, p_ref, wa_ref, wx_ref)

        lg = lg_ref[...]
        hcur = h_ref[...]
        ge = _gelu(lg)
        dgated, dgn = _rms_bwd(hcur * ge, p_ref[8:9, :], dy_ref[...])
        st_ref[8:9, :] += dgn
        dlg_ref[...] = (dgated * hcur * _gelu_grad(lg)).astype(BF)
        a_s[...] = a
        d_s[...] = dgated * ge

        def step(k, c):
            t = TT - 1 - k
            g = d_s[pl.ds(t, 1), :] + c
            g_s[pl.ds(t, 1), :] = g
            return a_s[pl.ds(t, 1), :] * g

        cc[0:1, :] = lax.fori_loop(0, TT, step, cc[0:1, :], unroll=8)
        gt = g_s[...]
        da = gt * hp[pl.ds(7, TT), :]
        dmult = gt * i * lx
        di = gt * mult * lx
        dlxc = gt * mult * i
        dla = da * a - dmult * (a * a) / mult
        dr = dla * (-LRU_C * sp)
        st_ref[7:8, :] += jnp.sum(dla * (-LRU_C * r), axis=0, keepdims=True) * (-_sigmoid(-p_ref[7:8, :]))
        dzr = dr * r * (1.0 - r)
        dzi = di * i * (1.0 - i)
        st_ref[5:6, :] += jnp.sum(dzr, axis=0, keepdims=True)
        st_ref[6:7, :] += jnp.sum(dzi, axis=0, keepdims=True)
        dlxc = dlxc + _blockdiag_mm(dzr, waT_ref) + _blockdiag_mm(dzi, wxT_ref)
        for j in range(4):
            sl = slice(256 * j, 256 * (j + 1))
            dwa_ref[j] += _mm_tn(lx[:, sl], dzr[:, sl])
            dwx_ref[j] += _mm_tn(lx[:, sl], dzi[:, sl])
        dp[0:TT, :] = dlxc
        acc = p_ref[0:1, :] * dp[pl.ds(CONV_K - 1, TT), :]
        for k in range(1, CONV_K):
            acc = acc + p_ref[k:k + 1, :] * dp[pl.ds(CONV_K - 1 - k, TT), :]
        dlx_ref[...] = acc.astype(BF)
        dp[TT:TT + 8, :] = dp[0:8, :]
        for k in range(CONV_K):
            st_ref[k:k + 1, :] += jnp.sum(dlxc * taps[k], axis=0, keepdims=True)
        st_ref[4:5, :] += jnp.sum(dlxc, axis=0, keepdims=True)

    w4 = _whole((4, 256, 256))
    return pl.pallas_call(
        body, name="lru_bwd", grid=(NT,),
        in_specs=[_rows(TT, LW, NT), _rows(TT, LW, NT), _halo(LW, NT, TT), _rows(TT, LW, NT), _rows(TT, LW, NT),
                  _halo(LW, NT, TT), _whole((16, LW)), w4, w4, w4, w4],
        out_specs=[_rows(TT, LW, NT), _rows(TT, LW, NT), _whole((16, LW)), w4, w4],
        out_shape=[S((T, LW), BF), S((T, LW), BF), S((16, LW), F32), S((4, 256, 256), F32), S((4, 256, 256), F32)],
        scratch_shapes=[pltpu.VMEM((TT + 8, LW), F32), pltpu.VMEM((TT + 8, LW), F32), pltpu.VMEM((TT + 8, LW), F32),
                        pltpu.VMEM((TT, LW), F32), pltpu.VMEM((TT, LW), F32), pltpu.VMEM((TT, LW), F32),
                        pltpu.VMEM((8, LW), F32)],
        compiler_params=_params(),
    )(dy, lxr, lxr, lg, h, h, p_lru, wa4, wx4, wa4T, wx4T)


def _ssd_bwd(dyn, xbcr, z, dtr, y, states, cw_ssd, hp_ssd, g_ssd):
    T = dyn.shape[0]
    NC = T // CH

    def body(dyn_ref, xr_ref, xh_ref, z_ref, dt_ref, y_ref, st_ref, cw_ref, hp_ref, g_ref,
             dxbc_ref, dz_ref, ddt_ref, cst_ref, hst_ref, gst_ref, xp, dp, dS, dxb, yo_s, q_s, dxs_s, t1_s):
        first = pl.program_id(0) == 0
        top = pl.program_id(0) == NC - 1

        @pl.when(first)
        def _():
            cst_ref[...] = jnp.zeros_like(cst_ref)
            hst_ref[...] = jnp.zeros_like(hst_ref)
            gst_ref[...] = jnp.zeros_like(gst_ref)
            dp[CH:CH + 8, :] = jnp.zeros((8, XBC), F32)
            dS[...] = jnp.zeros_like(dS)

        xp[0:8, :] = xh_ref[...] * jnp.where(top, 0.0, 1.0)
        cv, sg, xbc, raw, dtv, A, cs, taps = _ssd_prep(xp, xr_ref, dt_ref, cw_ref, hp_ref)
        csT, dsm, E_x, ds_x, El_rows = _ssd_decays(cs)
        row_i = lax.broadcasted_iota(jnp.int32, (CH, CH), 0)
        col_i = lax.broadcasted_iota(jnp.int32, (CH, CH), 1)
        tril = row_i >= col_i
        first = col_i < HD
        head_of = ((lax.broadcasted_iota(jnp.int32, (DTP, SI), 1) >> 6)
                   == lax.broadcasted_iota(jnp.int32, (DTP, SI), 0)).astype(BF)
        head_ofT = ((lax.broadcasted_iota(jnp.int32, (SI, DTP), 0) >> 6)
                    == lax.broadcasted_iota(jnp.int32, (SI, DTP), 1)).astype(BF)

        def hi_lo(v):
            hi = v.astype(BF)
            return hi, (v - hi.astype(F32)).astype(BF)

        def lane_sums(v):
            hi, lo = hi_lo(v)
            return _mm(hi, head_ofT) + _mm(lo, head_ofT)

        zz = z_ref[...]
        sz = _sigmoid(zz)
        yv = y_ref[...]
        dgn, dg = _rms_bwd(yv * (zz * sz), g_ref[...], dyn_ref[...])
        gst_ref[0:1, :] += dg
        dz_ref[...] = (dgn * yv * (sz + zz * sz * (1.0 - sz))).astype(BF)
        dY = dgn * (zz * sz)

        X = xbc[:, 0:SI]
        dt_x = _per_head_lanes(dtv)
        xs = X * dt_x
        xsd = (xs * ds_x).astype(BF)
        D_x = _per_head_lanes(hp_ref[...])[2:3, :]
        dcs_col = jnp.zeros((CH, DTP), F32)
        dcs_row = jnp.zeros((CH, DTP), F32)
        GW = HPG * HD
        for g in range(NG):
            gs = slice(GW * g, GW * (g + 1))
            Bg = xbc[:, SI + NS * g:SI + NS * (g + 1)].astype(BF)
            Cg = xbc[:, SI + NG * NS + NS * g:SI + NG * NS + NS * (g + 1)].astype(BF)
            G = _mm_nt(Cg, Bg)
            Sg = st_ref[0, gs, :]
            dSe = dS[gs, :]
            dYg = dY[:, gs]
            yo_s[:, gs] = _mm_nt(Cg, Sg) * E_x[:, gs]
            dP = dYg * E_x[:, gs]
            dCg = _mm(dP, Sg)
            dS[gs, :] = _mm_tn(dP, Cg) + _per_head_rows(El_rows, g) * dSe
            t1_s[gs, :] = dSe * Sg
            Q = _mm_nt(Bg, dSe)
            q_s[:, gs] = Q
            dBg = _mm(xsd[:, gs], dSe)
            dG = jnp.zeros((CH, CH), F32)
            for jj in range(HPG // 2):
                j = g * (HPG // 2) + jj
                ps = slice(2 * HD * j, 2 * HD * (j + 1))
                xs_pair = xs[:, ps]
                dxs_pair = Q[:, 2 * HD * jj:2 * HD * (jj + 1)] * ds_x[:, ps]
                for e in range(2):
                    h = 2 * j + e
                    Lm = jnp.exp(jnp.where(tril, cs[:, h:h + 1] - csT[h:h + 1, :], -1e30))
                    M = G * Lm
                    dYm = jnp.where(first if e == 0 else ~first, dY[:, ps], 0.0).astype(BF)
                    dM = _mm_nt(dYm, xs_pair)
                    dxs_pair = dxs_pair + _mm_tn(M, dYm)
                    Wm = dM * M
                    dcs_col = dcs_col + jnp.where(col_i == h, jnp.sum(Wm, axis=1, keepdims=True), 0.0)
                    dcs_row = dcs_row + jnp.where(row_i == h, -jnp.sum(Wm, axis=0, keepdims=True), 0.0)
                    dG = dG + dM * Lm
                dxs_s[:, ps] = dxs_pair
            dxb[:, SI + NS * g:SI + NS * (g + 1)] = dBg + _mm_tn(dG, Cg)
            dxb[:, SI + NG * NS + NS * g:SI + NG * NS + NS * (g + 1)] = dCg + _mm(dG, Bg)

        dxs = dxs_s[...]
        dxb[:, 0:SI] = D_x * dY + dxs * dt_x
        dds = lane_sums(q_s[...] * xs) * dsm
        dcs_col = dcs_col + lane_sums(dY * yo_s[...]) - dds
        ddt_col = lane_sums(dxs * X)
        dD = jnp.sum(lane_sums(dY * X), axis=0, keepdims=True)
        t_hi, t_lo = hi_lo(t1_s[...])
        dcl_rows = jnp.sum(_mm(head_of, t_hi) + _mm(head_of, t_lo), axis=1, keepdims=True) * jnp.exp(csT[:, CH - 1:CH])
        dcs_row = dcs_row + jnp.where(col_i == CH - 1, dcl_rows, 0.0)
        dcs_col = dcs_col + jnp.where(row_i == CH - 1, jnp.sum(dds, axis=0, keepdims=True), 0.0)

        da = _rev_cumsum_rows(dcs_col + dcs_row.T, CH)
        ddt_col = ddt_col + da * A
        hst_ref[1:2, :] += jnp.sum(da * dtv, axis=0, keepdims=True) * A
        hst_ref[2:3, :] += dD
        draw = jnp.where(col_i < NH, ddt_col * _sigmoid(raw), 0.0)
        ddt_ref[...] = draw.astype(BF)
        hst_ref[0:1, :] += jnp.sum(draw, axis=0, keepdims=True)

        dcv = dxb[...] * (sg + cv * sg * (1.0 - sg))
        dp[0:CH, :] = dcv
        acc = cw_ref[0:1, :] * dp[pl.ds(CONV_K - 1, CH), :]
        for k in range(1, CONV_K):
            acc = acc + cw_ref[k:k + 1, :] * dp[pl.ds(CONV_K - 1 - k, CH), :]
        dxbc_ref[...] = acc.astype(BF)
        dp[CH:CH + 8, :] = dp[0:8, :]
        for k in range(CONV_K):
            cst_ref[k:k + 1, :] += jnp.sum(dcv * taps[k], axis=0, keepdims=True)
        cst_ref[4:5, :] += jnp.sum(dcv, axis=0, keepdims=True)

    return pl.pallas_call(
        body, name="ssd_bwd", grid=(NC,),
        in_specs=[_rows(CH, SI, NC), _rows(CH, XBC, NC), _halo(XBC, NC, CH), _rows(CH, SI, NC), _rows(CH, DTP, NC),
                  _rows(CH, SI, NC), pl.BlockSpec((1, NH * HD, NS), lambda i: (NC - 1 - i, 0, 0)),
                  _whole((8, XBC)), _whole((8, DTP)), _whole((1, SI))],
        out_specs=[_rows(CH, XBC, NC), _rows(CH, SI, NC), _rows(CH, DTP, NC), _whole((8, XBC)), _whole((8, DTP)),
                   _whole((8, SI))],
        out_shape=[S((T, XBC), BF), S((T, SI), BF), S((T, DTP), BF), S((8, XBC), F32), S((8, DTP), F32), S((8, SI), F32)],
        scratch_shapes=[pltpu.VMEM((CH + 8, XBC), F32), pltpu.VMEM((CH + 8, XBC), F32), pltpu.VMEM((NH * HD, NS), F32),
                        pltpu.VMEM((CH, XBC), F32), pltpu.VMEM((CH, SI), F32), pltpu.VMEM((CH, SI), F32),
                        pltpu.VMEM((CH, SI), F32), pltpu.VMEM((NH * HD, NS), F32)],
        compiler_params=_params(),
    )(dyn, xbcr, xbcr, z, dtr, y, states, cw_ssd, hp_ssd, g_ssd)


def _inproj_bwd(dlx, dlg, dz, dxbc, ddt, x, dx1, wcatT, g0):
    T = x.shape[0]

    def body(dlx_ref, dlg_ref, dz_ref, dxbc_ref, ddt_ref, x_ref, dx1_ref, w_hbm, g_ref, dx_ref, st_ref, w_vm, sem):
        _load_once([(w_hbm, w_vm)], sem)

        @pl.when(pl.program_id(0) == 0)
        def _():
            st_ref[...] = jnp.zeros_like(st_ref)

        dh = jnp.dot(dlx_ref[...], w_vm[0:1024, :], preferred_element_type=F32)
        dh = dh + jnp.dot(dlg_ref[...], w_vm[1024:2048, :], preferred_element_type=F32)
        dh = dh + jnp.dot(dz_ref[...], w_vm[2048:3072, :], preferred_element_type=F32)
        dh = dh + jnp.dot(dxbc_ref[...], w_vm[3072:3072 + XBC, :], preferred_element_type=F32)
        dh = dh + jnp.dot(ddt_ref[...], w_vm[3072 + XBC:PC, :], preferred_element_type=F32)
        dx, dg = _rms_bwd(x_ref[...], g_ref[...], dh)
        dx_ref[...] = dx1_ref[...] + dx
        st_ref[0:1, :] += dg

    return pl.pallas_call(
        body, name="inproj_bwd", grid=(T // TT,),
        in_specs=[_rows(TT, 1024), _rows(TT, 1024), _rows(TT, 1024), _rows(TT, XBC), _rows(TT, DTP), _rows(TT, D),
                  _rows(TT, D), ANY, _whole((1, D))],
        out_specs=[_rows(TT, D), _whole((8, D))],
        out_shape=[S((T, D), F32), S((8, D), F32)],
        scratch_shapes=[pltpu.VMEM((PC, D), BF), pltpu.SemaphoreType.DMA((1,))],
        compiler_params=_params(),
    )(dlx, dlg, dz, dxbc, ddt, x, dx1, wcatT, g0)


def _wgrad(name, a, b):
    T, M = a.shape
    N = b.shape[1]
    tk = min(T, 2048 if M <= 1024 else 1024)
    tn = N
    while M * tn * 4 > (6 << 20) and tn % 256 == 0:
        tn //= 2

    def body(a_ref, b_ref, o_ref):
        p = lax.dot_general(a_ref[...], b_ref[...], (((0,), (0,)), ((), ())), preferred_element_type=F32)

        @pl.when(pl.program_id(1) == 0)
        def _():
            o_ref[...] = p

        @pl.when(pl.program_id(1) > 0)
        def _():
            o_ref[...] += p

    return pl.pallas_call(
        body, name=name, grid=(N // tn, T // tk),
        in_specs=[pl.BlockSpec((tk, M), lambda j, k: (k, 0)), pl.BlockSpec((tk, tn), lambda j, k: (k, j))],
        out_specs=pl.BlockSpec((M, tn), lambda j, k: (0, j)), out_shape=S((M, N), F32),
        compiler_params=_params(2),
    )(a, b)


def _adamw(name, w, g, m, v):
    R, C = w.shape
    tr = _row_tile(R, C)
    c1 = 1.0 - ADAM_B1 ** ADAM_STEP
    c2 = 1.0 - ADAM_B2 ** ADAM_STEP

    def body(w_ref, g_ref, m_ref, v_ref, d_ref, nm_ref, nv_ref):
        gg = g_ref[...]
        mm = ADAM_B1 * m_ref[...] + (1.0 - ADAM_B1) * gg
        vv = ADAM_B2 * v_ref[...] + (1.0 - ADAM_B2) * (gg * gg)
        nm_ref[...] = mm
        nv_ref[...] = vv
        d_ref[...] = -ADAM_LR * ((mm / c1) / (jnp.sqrt(vv / c2) + ADAM_EPS) + ADAM_WD * w_ref[...])

    blk = pl.BlockSpec((tr, C), lambda i: (i, 0))
    return pl.pallas_call(
        body, name=name, grid=(R // tr,),
        in_specs=[blk] * 4, out_specs=[blk] * 3, out_shape=[S((R, C), F32)] * 3,
        compiler_params=_params(),
    )(w, g, m, v)


def _pos():
    return lax.axis_index("x"), lax.axis_index("y"), lax.axis_index("c")


def _other_chips(x, y):
    return [(1 - x, y), (x, 1 - y), (1 - x, 1 - y)]


def _half(ref, c, hr):
    sl = pl.ds(pl.multiple_of(c * hr, 8), hr)
    return ref.at[:, sl, :] if len(ref.shape) == 3 else ref.at[sl, :]


def _remote(src, dst, send_sem, recv_sem, to):
    return pltpu.make_async_remote_copy(src_ref=src, dst_ref=dst, send_sem=send_sem, recv_sem=recv_sem,
                                        device_id=to, device_id_type=MESH)


def _allgather_weights(shards):
    n = len(shards)

    def body(*refs):
        ins, outs = refs[:n], refs[n:2 * n]
        send_sems, recv_sems = refs[2 * n], refs[2 * n + 1]
        x, y, c = _pos()
        me = 2 * x + y
        chips = _other_chips(x, y)
        first, passed = [], []
        for i, (src, dst) in enumerate(zip(ins, outs)):
            hr = src.shape[0] // 2
            my_half = pl.ds(pl.multiple_of(c * hr, 16), hr)
            for k, (cx, cy) in enumerate(chips):
                cp = _remote(src.at[my_half, :], dst.at[me, my_half, :], send_sems.at[6 * i + k], recv_sems.at[6 * i + k],
                             (cx, cy, c))
                cp.start()
                first.append(cp)
        for i, (src, dst) in enumerate(zip(ins, outs)):
            hr = src.shape[0] // 2
            my_half = pl.ds(pl.multiple_of(c * hr, 16), hr)
            for k, (cx, cy) in enumerate(chips):
                blk = dst.at[2 * cx + cy, my_half, :]
                _remote(blk, blk, send_sems.at[6 * i + k], recv_sems.at[6 * i + k], (cx, cy, c)).wait_recv()
                fw = _remote(blk, blk, send_sems.at[6 * i + 3 + k], recv_sems.at[6 * i + 3 + k], (x, y, 1 - c))
                fw.start()
                passed.append(fw)
        for i, (src, dst) in enumerate(zip(ins, outs)):
            hr = src.shape[0] // 2
            sib_half = pl.ds(pl.multiple_of((1 - c) * hr, 16), hr)
            for k, (cx, cy) in enumerate(chips):
                blk = dst.at[2 * cx + cy, sib_half, :]
                _remote(blk, blk, send_sems.at[6 * i + 3 + k], recv_sems.at[6 * i + 3 + k], (x, y, 1 - c)).wait_recv()
        for cp in first + passed:
            cp.wait_send()

    return pl.pallas_call(
        body, name="allgather_weights", in_specs=[ANY] * n, out_specs=[ANY] * n,
        out_shape=[S((4,) + s.shape, s.dtype) for s in shards],
        scratch_shapes=[pltpu.SemaphoreType.DMA((6 * n,)), pltpu.SemaphoreType.DMA((6 * n,))],
    )(*shards)


def _pair_exchange(bufs):
    n = len(bufs)

    def half_shape(b):
        return b.shape[:-2] + (b.shape[-2] // 2, b.shape[-1])

    def body(*refs):
        ins, outs = refs[:n], refs[n:2 * n]
        send_sems, recv_sems = refs[2 * n], refs[2 * n + 1]
        x, y, c = _pos()
        copies = [_remote(_half(src, 1 - c, src.shape[-2] // 2), dst, send_sems.at[k], recv_sems.at[k], (x, y, 1 - c))
                  for k, (src, dst) in enumerate(zip(ins, outs))]
        for cp in copies:
            cp.start()
        for cp in copies:
            cp.wait()

    return pl.pallas_call(
        body, name="pair_exchange", in_specs=[ANY] * n, out_specs=[ANY] * n,
        out_shape=[S(half_shape(b), b.dtype) for b in bufs],
        scratch_shapes=[pltpu.SemaphoreType.DMA((n,)), pltpu.SemaphoreType.DMA((n,))],
    )(*bufs)


def _quad_exchange(bufs, scatter):
    n = len(bufs)

    def body(*refs):
        ins, outs = refs[:n], refs[n:2 * n]
        send_sems, recv_sems = refs[2 * n], refs[2 * n + 1]
        x, y, c = _pos()
        me = 2 * x + y
        chips = _other_chips(x, y)
        copies = []
        for k, (src, dst) in enumerate(zip(ins, outs)):
            for j, (cx, cy) in enumerate(chips):
                piece = src.at[2 * cx + cy] if scatter[k] else src
                cp = _remote(piece, dst.at[me], send_sems.at[3 * k + j], recv_sems.at[3 * k + j], (cx, cy, c))
                cp.start()
                copies.append(cp)
        for k, (src, dst) in enumerate(zip(ins, outs)):
            for j, (cx, cy) in enumerate(chips):
                blk = dst.at[2 * cx + cy]
                _remote(blk, blk, send_sems.at[3 * k + j], recv_sems.at[3 * k + j], (cx, cy, c)).wait_recv()
        for cp in copies:
            cp.wait_send()

    return pl.pallas_call(
        body, name="quad_exchange", in_specs=[ANY] * n, out_specs=[ANY] * n,
        out_shape=[S((4,) + (b.shape[1:] if sc else b.shape), b.dtype) for b, sc in zip(bufs, scatter)],
        scratch_shapes=[pltpu.SemaphoreType.DMA((3 * n,)), pltpu.SemaphoreType.DMA((3 * n,))],
    )(*bufs)


def _pair_gather(bufs):
    n = len(bufs)

    def body(*refs):
        ins, outs = refs[:n], refs[n:2 * n]
        send_sems, recv_sems = refs[2 * n], refs[2 * n + 1]
        x, y, c = _pos()
        copies = []
        for k, buf in enumerate(outs):
            mine = _half(buf, c, buf.shape[0] // 2)
            cp = _remote(mine, mine, send_sems.at[k], recv_sems.at[k], (x, y, 1 - c))
            cp.start()
            copies.append(cp)
        for k, buf in enumerate(outs):
            theirs = _half(buf, 1 - c, buf.shape[0] // 2)
            _remote(theirs, theirs, send_sems.at[k], recv_sems.at[k], (x, y, 1 - c)).wait_recv()
        for cp in copies:
            cp.wait_send()

    return pl.pallas_call(
        body, name="pair_gather", in_specs=[ANY] * n, out_specs=[ANY] * n,
        out_shape=[S(b.shape, b.dtype) for b in bufs], input_output_aliases={k: k for k in range(n)},
        scratch_shapes=[pltpu.SemaphoreType.DMA((n,)), pltpu.SemaphoreType.DMA((n,))],
    )(*bufs)


def _row_tile(rows, cols, mult=8):
    best = mult
    for t in range(mult, rows + 1, mult):
        if rows % t == 0 and t * cols * 4 <= (1 << 20):
            best = t
    return best


def _add_own_half(name, full, got, c, out_dtype):
    three = len(full.shape) == 3
    lead = full.shape[0] if three else 1
    rows, cols = full.shape[-2], full.shape[-1]
    hr = rows // 2
    tr = _row_tile(hr, cols, 16 if out_dtype == jnp.bfloat16 else 8)
    per = hr // tr

    def body(c_ref, a_ref, b_ref, o_ref):
        o_ref[...] = (a_ref[...] + b_ref[...]).astype(out_dtype)

    if three:
        a_spec = pl.BlockSpec((1, tr, cols), lambda s, i, c_ref: (s, c_ref[0] * per + i, 0))
        o_spec = pl.BlockSpec((1, tr, cols), lambda s, i, c_ref: (s, i, 0))
    else:
        a_spec = pl.BlockSpec((tr, cols), lambda s, i, c_ref: (c_ref[0] * per + i, 0))
        o_spec = pl.BlockSpec((tr, cols), lambda s, i, c_ref: (i, 0))
    return pl.pallas_call(
        body, name=name,
        grid_spec=pltpu.PrefetchScalarGridSpec(num_scalar_prefetch=1, grid=(lead, per), in_specs=[a_spec, o_spec],
                                               out_specs=o_spec),
        out_shape=S(got.shape, out_dtype), compiler_params=_params(2),
    )(jnp.reshape(c, (1,)).astype(jnp.int32), full, got)


def _sum_slots(name, own, slots, me, c):
    _, rows, cols = slots.shape
    tr = _row_tile(rows, cols, 16 if slots.dtype == jnp.bfloat16 else 8)
    per = rows // tr
    three = len(own.shape) == 3

    def body(p_ref, own_ref, s0, s1, s2, s3, o_ref):
        mine = own_ref[0] if three else own_ref[...]
        acc = None
        for j, s_ref in enumerate((s0, s1, s2, s3)):
            v = jnp.where(p_ref[0] == j, mine, s_ref[0]).astype(F32)
            acc = v if acc is None else acc + v
        o_ref[...] = acc

    def slot_spec(j):
        return pl.BlockSpec((1, tr, cols), lambda i, p: (jnp.where(p[0] == j, (j + 1) % 4, j), i, 0))

    own_spec = (pl.BlockSpec((1, tr, cols), lambda i, p: (p[0], i, 0)) if three
                else pl.BlockSpec((tr, cols), lambda i, p: (i, 0)))
    return pl.pallas_call(
        body, name=name,
        grid_spec=pltpu.PrefetchScalarGridSpec(
            num_scalar_prefetch=1, grid=(per,), in_specs=[own_spec] + [slot_spec(j) for j in range(4)],
            out_specs=pl.BlockSpec((tr, cols), lambda i, p: (p[1] * per + i, 0))),
        out_shape=S((2 * rows, cols), F32), compiler_params=_params(),
    )(jnp.stack([me, c]).astype(jnp.int32), own, slots, slots, slots, slots)


SMALL = (("pre_mix_norm", 1024), ("lru_conv_w", 4096), ("lru_conv_b", 1024), ("lru_wa", 65536), ("lru_ba", 1024),
         ("lru_wx", 65536), ("lru_bx", 1024), ("lru_lambda", 1024), ("lru_out_norm", 1024), ("ssd_conv_w", 6144),
         ("ssd_conv_b", 1536), ("ssd_dt_bias", 16), ("ssd_a_log", 16), ("ssd_d", 16), ("ssd_out_norm", 1024),
         ("post_mix_norm", 1024), ("pre_ffn_norm", 1024), ("post_ffn_norm", 1024))
SMALL_ROWS = 1200
BIG = ("w_in", "w_out", "w_gate", "w_up", "w_down")


def _diag4(w):
    eye = jnp.eye(4, dtype=w.dtype).reshape(1, 4, 1, 4, 1)
    return (w.reshape(4, 4, BW, 1, BW) * eye).reshape(4, 4 * BW, 4 * BW)


def _undiag4(w4):
    w4 = w4.reshape(4, 4, BW, 4, BW)
    return jnp.stack([w4[:, a, :, a, :] for a in range(4)], axis=1).reshape(NBLK, BW, BW)


def _pack_small(parts, loss):
    flat = jnp.concatenate([parts[name].reshape(-1).astype(F32) for name, _ in SMALL])
    flat = jnp.pad(flat, (0, SMALL_ROWS * 128 - 1 - flat.shape[0]))
    return jnp.concatenate([flat, loss.reshape(1)]).reshape(SMALL_ROWS, 128)


def _unpack_small(buf):
    flat = buf.reshape(-1)
    out, off = {}, 0
    for name, size in SMALL:
        out[name] = flat[off:off + size]
        off += size
    return out


def _gather_weights(w_in, w_out, w_gate, w_up, w_down, lru_conv_w, ssd_conv_w):
    conv = jnp.concatenate([lru_conv_w.reshape(-1), ssd_conv_w.reshape(-1)]).astype(F32)
    hi = conv.astype(jnp.bfloat16)
    mid = (conv - hi.astype(F32)).astype(jnp.bfloat16)
    lo = (conv - hi.astype(F32) - mid.astype(F32)).astype(jnp.bfloat16)
    terms = jnp.concatenate([hi, mid, lo])
    n_terms = terms.shape[0]
    conv_rows = -(-n_terms // (128 * 32)) * 32
    terms = jnp.pad(terms, (0, conv_rows * 128 - n_terms)).reshape(conv_rows, 128)
    own = [w.astype(jnp.bfloat16) for w in (w_in, w_out, w_gate, w_up, w_down)] + [terms]
    got = _allgather_weights(own)
    chip = 2 * lax.axis_index("x") + lax.axis_index("y")
    here = (jnp.arange(4) == chip).reshape(4, 1, 1)
    full = [jnp.where(here, o[None], g) for o, g in zip(own, got)]
    cols = lambda f: f.transpose(1, 0, 2).reshape(f.shape[1], 4 * f.shape[2])
    rows = lambda f: f.reshape(4 * f.shape[1], f.shape[2])
    win_f, wout_f, wg_f, wu_f, wd_f = cols(full[0]), rows(full[1]), cols(full[2]), cols(full[3]), rows(full[4])
    t3 = full[5].reshape(4, -1)[:, :n_terms].reshape(4, 3, -1).astype(F32)
    conv_f = (t3[:, 0] + t3[:, 1]) + t3[:, 2]
    n1 = lru_conv_w.size
    lcw = conv_f[:, :n1].reshape(4, CONV_K, -1).transpose(1, 0, 2).reshape(CONV_K, LW)
    scw = conv_f[:, n1:].reshape(4, CONV_K, -1).transpose(1, 0, 2).reshape(CONV_K, XBC)
    return win_f, wout_f, wg_f, wu_f, wd_f, lcw, scw


def _local_step(x, tgt, win_f, wout_f, wg_f, wu_f, wd_f, lcw, scw, sp):
    mm = lambda w: w.astype(BF)
    wcat = jnp.concatenate([mm(win_f), jnp.zeros((D, PC - IN_COLS), BF)], axis=1)
    row = lambda v: v.reshape(1, -1).astype(F32)
    p_lru = jnp.concatenate([lcw, row(sp["lru_conv_b"]), row(sp["lru_ba"]), row(sp["lru_bx"]), row(sp["lru_lambda"]),
                             row(sp["lru_out_norm"]), jnp.zeros((7, LW), F32)], axis=0)
    wa4, wx4 = mm(_diag4(sp["lru_wa"][0])), mm(_diag4(sp["lru_wx"][0]))
    wa4T, wx4T = wa4.transpose(0, 2, 1), wx4.transpose(0, 2, 1)
    cw_ssd = jnp.concatenate([scw, row(sp["ssd_conv_b"]), jnp.zeros((3, XBC), F32)], axis=0)
    padh = lambda v: jnp.pad(row(v), ((0, 0), (0, DTP - NH)))
    hp_ssd = jnp.concatenate([padh(sp["ssd_dt_bias"]), padh(sp["ssd_a_log"]), padh(sp["ssd_d"]), jnp.zeros((5, DTP), F32)], axis=0)
    g0, g_ssd = row(sp["pre_mix_norm"]), row(sp["ssd_out_norm"])
    g_pm, g_pf, g_pff = row(sp["post_mix_norm"]), row(sp["pre_ffn_norm"]), row(sp["post_ffn_norm"])
    wout, wg, wu, wd = mm(wout_f), mm(wg_f), mm(wu_f), mm(wd_f)

    h0, lxr, lg, z, xbcr, dtr = _inproj(x, g0, wcat)
    h, ylru = _lru_fwd(lxr, lg, p_lru, wa4, wx4)
    y, yssd, states = _ssd_fwd(xbcr, z, dtr, cw_ssd, hp_ssd, g_ssd)
    mix, x1, h2 = _outproj(ylru, yssd, x, wout, g_pm, g_pf)
    gate, up, act, df, dx2, st_ffn = _ffn_fwd(h2, x1, tgt, wg, wu, wd, g_pff)
    dgate, dup, dh2 = _ffn_bwd(df, gate, up, wd.T, wg.T, wu.T)
    dx1, dmix, dyl, dys, st_mix = _mix_bwd(dh2, x1, dx2, mix, wout.T, g_pf, g_pm)
    dlx, dlg, st_lru, dwa4, dwx4 = _lru_bwd(dyl, lxr, lg, h, p_lru, wa4, wx4, wa4T, wx4T)
    dxbc, dz, ddt, cst, hst, gst = _ssd_bwd(dys, xbcr, z, dtr, y, states, cw_ssd, hp_ssd, g_ssd)
    gx, st_in = _inproj_bwd(dlx, dlg, dz, dxbc, ddt, x, dx1, wcat.T, g0)

    dwd = _wgrad("wgrad_down", act, df)
    dwg = _wgrad("wgrad_gate", h2, dgate)
    dwu = _wgrad("wgrad_up", h2, dup)
    dwo_l = _wgrad("wgrad_out_lru", ylru, dmix)
    dwo_s = _wgrad("wgrad_out_ssd", yssd, dmix)
    pin = [_wgrad("wgrad_in_%d" % k, h0, b) for k, b in enumerate((dlx, dlg, dz, dxbc, ddt))]
    dwin = jnp.concatenate(pin[:4] + [pin[4][:, :NH]], axis=1)
    big = {"w_in": dwin, "w_out": jnp.concatenate([dwo_l, dwo_s], axis=0), "w_gate": dwg, "w_up": dwu, "w_down": dwd}
    small = {
        "pre_mix_norm": st_in[0], "lru_conv_w": st_lru[0:4], "lru_conv_b": st_lru[4], "lru_wa": _undiag4(dwa4),
        "lru_ba": st_lru[5], "lru_wx": _undiag4(dwx4), "lru_bx": st_lru[6], "lru_lambda": st_lru[7],
        "lru_out_norm": st_lru[8], "ssd_conv_w": cst[0:4], "ssd_conv_b": cst[4], "ssd_dt_bias": hst[0, :NH],
        "ssd_a_log": hst[1, :NH], "ssd_d": hst[2, :NH], "ssd_out_norm": gst[0], "post_mix_norm": st_mix[1],
        "pre_ffn_norm": st_mix[0], "post_ffn_norm": st_ffn[1],
    }
    return jnp.sum(st_ffn[0]), gx, big, small


def _reduce_grads(big, small_buf):
    c = lax.axis_index("c")
    bufs = []
    for name in BIG:
        g = big[name]
        if name in ("w_in", "w_gate", "w_up"):
            bufs.append(g.reshape(g.shape[0], 4, g.shape[1] // 4).transpose(1, 0, 2))
        else:
            bufs.append(g.reshape(4, g.shape[0] // 4, g.shape[1]))
    bufs.append(small_buf)
    me = 2 * lax.axis_index("x") + lax.axis_index("y")
    got = _pair_exchange(bufs)
    wire = [jnp.bfloat16] * len(BIG) + [F32]
    part = [_add_own_half("pair_add_%d" % k, b, r, c, dt) for k, (b, r, dt) in enumerate(zip(bufs, got, wire))]
    slots = _quad_exchange(part, [True] * len(BIG) + [False])
    red = [_sum_slots("quad_sum_%d" % k, p, s, me, c) for k, (p, s) in enumerate(zip(part, slots))]
    return _pair_gather(red)


def kernel(x, pre_mix_norm, w_in, lru_conv_w, lru_conv_b, lru_wa, lru_ba, lru_wx, lru_bx, lru_lambda, lru_out_norm, ssd_conv_w, ssd_conv_b, ssd_dt_bias, ssd_a_log, ssd_d, ssd_out_norm, w_out, post_mix_norm, pre_ffn_norm, w_gate, w_up, w_down, post_ffn_norm, loss_target, m_pre_mix_norm, m_w_in, m_lru_conv_w, m_lru_conv_b, m_lru_wa, m_lru_ba, m_lru_wx, m_lru_bx, m_lru_lambda, m_lru_out_norm, m_ssd_conv_w, m_ssd_conv_b, m_ssd_dt_bias, m_ssd_a_log, m_ssd_d, m_ssd_out_norm, m_w_out, m_post_mix_norm, m_pre_ffn_norm, m_w_gate, m_w_up, m_w_down, m_post_ffn_norm, v_pre_mix_norm, v_w_in, v_lru_conv_w, v_lru_conv_b, v_lru_wa, v_lru_ba, v_lru_wx, v_lru_bx, v_lru_lambda, v_lru_out_norm, v_ssd_conv_w, v_ssd_conv_b, v_ssd_dt_bias, v_ssd_a_log, v_ssd_d, v_ssd_out_norm, v_w_out, v_post_mix_norm, v_pre_ffn_norm, v_w_gate, v_w_up, v_w_down, v_post_ffn_norm):
    args = dict(locals())
    names = [n for n, _ in SMALL] + list(BIG)
    w = {n: args[n] for n in names}
    m = {n: args["m_" + n] for n in names}
    v = {n: args["v_" + n] for n in names}
    chip = 2 * lax.axis_index("x") + lax.axis_index("y")

    win_f, wout_f, wg_f, wu_f, wd_f, lcw, scw = _gather_weights(w_in[0], w_out[0], w_gate[0], w_up[0], w_down[0],
                                                                lru_conv_w[0], ssd_conv_w[0])
    sp = {n: w[n] for n, _ in SMALL}
    loss_part, gx, big, small = _local_step(x[0], loss_target[0], win_f, wout_f, wg_f, wu_f, wd_f, lcw, scw, sp)
    red = _reduce_grads(big, _pack_small(small, loss_part))
    gsmall = _unpack_small(red[len(BIG)])
    loss = red[len(BIG)][SMALL_ROWS - 1, 127]

    grads, delta, new_m, new_v = {}, {}, {}, {}
    for k, n in enumerate(BIG):
        g = red[k]
        d, nm, nv = _adamw("adamw_" + n, w[n][0], g, m[n][0], v[n][0])
        grads[n], delta[n], new_m[n], new_v[n] = g[None], d[None], nm[None], nv[None]

    def local_part(n, full):
        shape = w[n].shape
        if n in ("lru_conv_w", "ssd_conv_w"):
            per = shape[-1]
            return lax.dynamic_slice_in_dim(full.reshape(CONV_K, -1), chip * per, per, axis=1).reshape(shape)
        return full.reshape(shape)

    gl = {n: local_part(n, gsmall[n]) for n, _ in SMALL}

    def pack_local(d):
        flat = jnp.concatenate([d[n].reshape(-1) for n, _ in SMALL])
        rows = -(-flat.shape[0] // (128 * 8)) * 8
        return jnp.pad(flat, (0, rows * 128 - flat.shape[0])).reshape(rows, 128), flat.shape[0]

    wp, nflat = pack_local({n: w[n] for n, _ in SMALL})
    gp, _ = pack_local(gl)
    mp, _ = pack_local({n: m[n] for n, _ in SMALL})
    vp, _ = pack_local({n: v[n] for n, _ in SMALL})
    pad_mask = (jnp.arange(wp.size).reshape(wp.shape) >= nflat)
    dp_, nmp, nvp = _adamw("adamw_small", wp, gp, mp, jnp.where(pad_mask, 1.0, vp))
    off = 0
    for n, _ in SMALL:
        size = w[n].size
        grads[n] = gl[n]
        delta[n] = dp_.reshape(-1)[off:off + size].reshape(w[n].shape)
        new_m[n] = nmp.reshape(-1)[off:off + size].reshape(w[n].shape)
        new_v[n] = nvp.reshape(-1)[off:off + size].reshape(w[n].shape)
        off += size

    order = ["pre_mix_norm", "w_in", "lru_conv_w", "lru_conv_b", "lru_wa", "lru_ba", "lru_wx", "lru_bx", "lru_lambda",
             "lru_out_norm", "ssd_conv_w", "ssd_conv_b", "ssd_dt_bias", "ssd_a_log", "ssd_d", "ssd_out_norm", "w_out",
             "post_mix_norm", "pre_ffn_norm", "w_gate", "w_up", "w_down", "post_ffn_norm"]
    return (loss, gx[None], *[grads[n] for n in order], *[delta[n] for n in order],
            *[new_m[n] for n in order], *[new_v[n] for n in order])
```

```python
import functools

import jax
import jax.numpy as jnp
from jax import lax
from jax.experimental import pallas as pl
from jax.experimental.pallas import tpu as pltpu

F32 = jnp.float32
BF = jnp.bfloat16

D = 1024
LW = 1024
NBLK = 16
BW = 64
SI = 1024
NH = 16
HD = 64
NG = 2
HPG = NH // NG
NS = 128
CH = 128
XBC = SI + 2 * NG * NS
DTP = 128
PC = 3 * 1024 + XBC + DTP
DFF = 2816
IN_COLS = 4624
EPS = 1e-6
LRU_C = 8.0
CONV_K = 4
TT = 256
VMEM_LIMIT = 56 * 1024 * 1024

ADAM_LR, ADAM_B1, ADAM_B2, ADAM_EPS, ADAM_WD, ADAM_STEP = 0.001, 0.9, 0.999, 1e-08, 0.01, 10

MESH = pl.DeviceIdType.MESH


def _mm(a, b):
    return jnp.dot(a.astype(BF), b.astype(BF), preferred_element_type=F32)


def _mm_nt(a, b):
    return lax.dot_general(a.astype(BF), b.astype(BF), (((1,), (1,)), ((), ())), preferred_element_type=F32)


def _mm_tn(a, b):
    return lax.dot_general(a.astype(BF), b.astype(BF), (((0,), (0,)), ((), ())), preferred_element_type=F32)


def _sigmoid(x):
    return 0.5 * jnp.tanh(0.5 * x) + 0.5


def _softplus(x):
    return jnp.maximum(x, 0.0) + jnp.log1p(jnp.exp(-jnp.abs(x)))


_GELU_C = 0.7978845608028654
_GELU_K = 0.044715


def _gelu(x):
    t = jnp.tanh(_GELU_C * (x + _GELU_K * x * x * x))
    return 0.5 * x * (1.0 + t)


def _gelu_grad(x):
    t = jnp.tanh(_GELU_C * (x + _GELU_K * x * x * x))
    return 0.5 * (1.0 + t) + 0.5 * x * (1.0 - t * t) * _GELU_C * (1.0 + 3.0 * _GELU_K * x * x)


def _rms_fwd(x, g):
    r = lax.rsqrt(jnp.mean(x * x, axis=-1, keepdims=True) + EPS)
    return x * r * g


def _rms_bwd(x, g, dy):
    r = lax.rsqrt(jnp.mean(x * x, axis=-1, keepdims=True) + EPS)
    xh = x * r
    dxh = dy * g
    dg = jnp.sum(dy * xh, axis=0, keepdims=True)
    dx = r * (dxh - xh * jnp.mean(dxh * xh, axis=-1, keepdims=True))
    return dx, dg


def _sum_all(x):
    return jnp.sum(jnp.sum(x, axis=1, keepdims=True), axis=0, keepdims=True)


def _cumsum_rows(x, n):
    row = lax.broadcasted_iota(jnp.int32, x.shape, 0)
    k = 1
    while k < n:
        x = x + jnp.where(row >= k, pltpu.roll(x, k, 0), 0.0)
        k *= 2
    return x


def _rev_cumsum_rows(x, n):
    row = lax.broadcasted_iota(jnp.int32, x.shape, 0)
    k = 1
    while k < n:
        x = x + jnp.where(row < n - k, pltpu.roll(x, n - k, 0), 0.0)
        k *= 2
    return x


def _load_once(pairs, sem):
    @pl.when(pl.program_id(0) == 0)
    def _():
        for k, (src, dst) in enumerate(pairs):
            pltpu.make_async_copy(src, dst, sem.at[k]).start()
        for k, (src, dst) in enumerate(pairs):
            pltpu.make_async_copy(src, dst, sem.at[k]).wait()


def _params(n_axes=1):
    return pltpu.CompilerParams(dimension_semantics=("arbitrary",) * n_axes, vmem_limit_bytes=VMEM_LIMIT)


def _rows(n, width, rev_of=None):
    if rev_of is None:
        return pl.BlockSpec((n, width), lambda i: (i, 0))
    return pl.BlockSpec((n, width), lambda i: (rev_of - 1 - i, 0))


def _whole(shape):
    nd = len(shape)
    return pl.BlockSpec(shape, lambda i: (0,) * nd)


ANY = pl.BlockSpec(memory_space=pl.ANY)
S = jax.ShapeDtypeStruct


def _inproj(x, g0, wcat):
    T = x.shape[0]

    def body(x_ref, g_ref, w_hbm, h0_ref, lx_ref, lg_ref, z_ref, xbc_ref, dt_ref, w_vm, sem):
        _load_once([(w_hbm, w_vm)], sem)
        h = _rms_fwd(x_ref[...], g_ref[...]).astype(BF)
        h0_ref[...] = h
        lx_ref[...] = jnp.dot(h, w_vm[:, 0:1024], preferred_element_type=F32)
        lg_ref[...] = jnp.dot(h, w_vm[:, 1024:2048], preferred_element_type=F32)
        z_ref[...] = jnp.dot(h, w_vm[:, 2048:3072], preferred_element_type=F32)
        xbc_ref[...] = jnp.dot(h, w_vm[:, 3072:3072 + XBC], preferred_element_type=F32)
        dt_ref[...] = jnp.dot(h, w_vm[:, 3072 + XBC:PC], preferred_element_type=F32)

    return pl.pallas_call(
        body, name="inproj", grid=(T // TT,),
        in_specs=[_rows(TT, D), _whole((1, D)), ANY],
        out_specs=[_rows(TT, D), _rows(TT, 1024), _rows(TT, 1024), _rows(TT, 1024), _rows(TT, XBC), _rows(TT, DTP)],
        out_shape=[S((T, D), BF), S((T, 1024), F32), S((T, 1024), F32), S((T, 1024), F32), S((T, XBC), F32), S((T, DTP), F32)],
        scratch_shapes=[pltpu.VMEM((D, PC), BF), pltpu.SemaphoreType.DMA((1,))],
        compiler_params=_params(),
    )(x, g0, wcat)


def _blockdiag_mm(v, w4_ref):
    return jnp.concatenate([_mm(v[:, 256 * j:256 * (j + 1)], w4_ref[j]) for j in range(4)], axis=1)


def _lru_gates(lx, p_ref, wa_ref, wx_ref):
    r = _sigmoid(_blockdiag_mm(lx, wa_ref) + p_ref[5:6, :])
    i = _sigmoid(_blockdiag_mm(lx, wx_ref) + p_ref[6:7, :])
    sp = _softplus(-p_ref[7:8, :])
    la = -LRU_C * r * sp
    a = jnp.exp(la)
    th = jnp.tanh(la)
    mult = jnp.sqrt(-2.0 * th / (1.0 - th))
    return r, i, sp, a, mult


def _conv_from(xp_ref, p_ref, n):
    taps = [xp_ref[pl.ds(8 - CONV_K + 1 + k, n), :] for k in range(CONV_K)]
    acc = p_ref[4:5, :] + p_ref[0:1, :] * taps[0]
    for k in range(1, CONV_K):
        acc = acc + p_ref[k:k + 1, :] * taps[k]
    return acc, taps


def _lru_fwd(lxr, lg, p_lru, wa4, wx4):
    T = lxr.shape[0]

    def body(lx_ref, lg_ref, p_ref, wa_ref, wx_ref, h_ref, y_ref, xp, a_s, u_s, hc):
        @pl.when(pl.program_id(0) == 0)
        def _():
            xp[0:8, :] = jnp.zeros((8, LW), F32)
            hc[...] = jnp.zeros_like(hc)

        xp[8:8 + TT, :] = lx_ref[...]
        lx, _ = _conv_from(xp, p_ref, TT)
        xp[0:8, :] = xp[TT:TT + 8, :]
        r, i, sp, a, mult = _lru_gates(lx, p_ref, wa_ref, wx_ref)
        a_s[...] = a
        u_s[...] = mult * (i * lx)

        def step(t, h):
            h = a_s[pl.ds(t, 1), :] * h + u_s[pl.ds(t, 1), :]
            h_ref[pl.ds(t, 1), :] = h
            return h

        hc[0:1, :] = lax.fori_loop(0, TT, step, hc[0:1, :], unroll=8)
        gated = h_ref[...] * _gelu(lg_ref[...])
        y_ref[...] = _rms_fwd(gated, p_ref[8:9, :]).astype(BF)

    return pl.pallas_call(
        body, name="lru_fwd", grid=(T // TT,),
        in_specs=[_rows(TT, LW), _rows(TT, LW), _whole((16, LW)), _whole((4, 256, 256)), _whole((4, 256, 256))],
        out_specs=[_rows(TT, LW), _rows(TT, LW)],
        out_shape=[S((T, LW), F32), S((T, LW), BF)],
        scratch_shapes=[pltpu.VMEM((TT + 8, LW), F32), pltpu.VMEM((TT, LW), F32), pltpu.VMEM((TT, LW), F32),
                        pltpu.VMEM((8, LW), F32)],
        compiler_params=_params(),
    )(lxr, lg, p_lru, wa4, wx4)


def _ssd_prep(xp, xr_ref, dt_ref, cw_ref, hp_ref):
    xp[8:8 + CH, :] = xr_ref[...]
    cv, taps = _conv_from(xp, cw_ref, CH)
    sg = _sigmoid(cv)
    xbc = cv * sg
    lane = lax.broadcasted_iota(jnp.int32, (CH, DTP), 1)
    raw = dt_ref[...] + hp_ref[0:1, :]
    dtv = jnp.where(lane < NH, _softplus(raw), 0.0)
    A = jnp.where(lane[0:1, :] < NH, -jnp.exp(hp_ref[1:2, :]), 0.0)
    cs = _cumsum_rows(dtv * A, CH)
    return cv, sg, xbc, raw, dtv, A, cs, taps


def _per_head_lanes(v):
    r = v.shape[0]
    first = lax.broadcasted_iota(jnp.int32, (r, 2 * HD), 1) < HD
    pairs = [jnp.where(first, jnp.broadcast_to(v[:, 2 * j:2 * j + 1], (r, 2 * HD)),
                       jnp.broadcast_to(v[:, 2 * j + 1:2 * j + 2], (r, 2 * HD))) for j in range(NH // 2)]
    return jnp.concatenate(pairs, axis=1)


def _per_head_rows(col, g):
    return jnp.concatenate([jnp.broadcast_to(col[g * HPG + k:g * HPG + k + 1, :], (HD, NS)) for k in range(HPG)], axis=0)


def _ssd_decays(cs):
    csT = cs.T
    cl = cs[CH - 1:CH, :]
    E_x = _per_head_lanes(jnp.exp(cs))
    dsm = jnp.exp(cl - cs)
    ds_x = _per_head_lanes(dsm)
    El_rows = jnp.broadcast_to(jnp.exp(csT[0:NH, CH - 1:CH]), (NH, NS))
    return csT, dsm, E_x, ds_x, El_rows


def _ssd_fwd(xbcr, z, dtr, cw_ssd, hp_ssd, g_ssd):
    T = xbcr.shape[0]
    NC = T // CH

    def body(xr_ref, z_ref, dt_ref, cw_ref, hp_ref, g_ref, y_ref, yn_ref, st_ref, xp, st):
        @pl.when(pl.program_id(0) == 0)
        def _():
            xp[0:8, :] = jnp.zeros((8, XBC), F32)
            st[...] = jnp.zeros_like(st)

        cv, sg, xbc, raw, dtv, A, cs, _ = _ssd_prep(xp, xr_ref, dt_ref, cw_ref, hp_ref)
        xp[0:8, :] = xp[CH:CH + 8, :]
        st_ref[0] = st[...]
        csT, dsm, E_x, ds_x, El_rows = _ssd_decays(cs)
        X = xbc[:, 0:SI]
        xs = X * _per_head_lanes(dtv)
        xsd = (xs * ds_x).astype(BF)
        DX = _per_head_lanes(hp_ref[...])[2:3, :] * X
        tril = lax.broadcasted_iota(jnp.int32, (CH, CH), 0) >= lax.broadcasted_iota(jnp.int32, (CH, CH), 1)
        first = lax.broadcasted_iota(jnp.int32, (CH, 2 * HD), 1) < HD
        GW = HPG * HD
        for g in range(NG):
            Bg = xbc[:, SI + NS * g:SI + NS * (g + 1)].astype(BF)
            Cg = xbc[:, SI + NG * NS + NS * g:SI + NG * NS + NS * (g + 1)].astype(BF)
            G = _mm_nt(Cg, Bg)
            Sg = st[GW * g:GW * (g + 1), :]
            Yo = _mm_nt(Cg, Sg) * E_x[:, GW * g:GW * (g + 1)]
            st[GW * g:GW * (g + 1), :] = _per_head_rows(El_rows, g) * Sg + _mm_tn(xsd[:, GW * g:GW * (g + 1)], Bg)
            for jj in range(HPG // 2):
                j = g * (HPG // 2) + jj
                ps = slice(2 * HD * j, 2 * HD * (j + 1))
                xs_pair = xs[:, ps]
                acc = Yo[:, 2 * HD * jj:2 * HD * (jj + 1)] + DX[:, ps]
                for e in range(2):
                    h = 2 * j + e
                    Lm = jnp.exp(jnp.where(tril, cs[:, h:h + 1] - csT[h:h + 1, :], -1e30))
                    acc = acc + _mm(G * Lm, jnp.where(first if e == 0 else ~first, xs_pair, 0.0))
                y_ref[:, ps] = acc
        zz = z_ref[...]
        gated = y_ref[...] * (zz * _sigmoid(zz))
        yn_ref[...] = _rms_fwd(gated, g_ref[...]).astype(BF)

    return pl.pallas_call(
        body, name="ssd_fwd", grid=(NC,),
        in_specs=[_rows(CH, XBC), _rows(CH, SI), _rows(CH, DTP), _whole((8, XBC)), _whole((8, DTP)), _whole((1, SI))],
        out_specs=[_rows(CH, SI), _rows(CH, SI), pl.BlockSpec((1, NH * HD, NS), lambda i: (i, 0, 0))],
        out_shape=[S((T, SI), F32), S((T, SI), BF), S((NC, NH * HD, NS), F32)],
        scratch_shapes=[pltpu.VMEM((CH + 8, XBC), F32), pltpu.VMEM((NH * HD, NS), F32)],
        compiler_params=_params(),
    )(xbcr, z, dtr, cw_ssd, hp_ssd, g_ssd)


def _outproj(ylru, yssd, x, wout, g_pm, g_pf):
    T = x.shape[0]

    def body(yl_ref, ys_ref, x_ref, w_hbm, gpm_ref, gpf_ref, mix_ref, x1_ref, h2_ref, w_vm, sem):
        _load_once([(w_hbm, w_vm)], sem)
        mix = (jnp.dot(yl_ref[...], w_vm[0:LW, :], preferred_element_type=F32)
               + jnp.dot(ys_ref[...], w_vm[LW:LW + SI, :], preferred_element_type=F32))
        mix_ref[...] = mix
        x1 = x_ref[...] + _rms_fwd(mix, gpm_ref[...])
        x1_ref[...] = x1
        h2_ref[...] = _rms_fwd(x1, gpf_ref[...]).astype(BF)

    return pl.pallas_call(
        body, name="outproj", grid=(T // TT,),
        in_specs=[_rows(TT, LW), _rows(TT, SI), _rows(TT, D), ANY, _whole((1, D)), _whole((1, D))],
        out_specs=[_rows(TT, D), _rows(TT, D), _rows(TT, D)],
        out_shape=[S((T, D), F32), S((T, D), F32), S((T, D), BF)],
        scratch_shapes=[pltpu.VMEM((LW + SI, D), BF), pltpu.SemaphoreType.DMA((1,))],
        compiler_params=_params(),
    )(ylru, yssd, x, wout, g_pm, g_pf)


def _ffn_fwd(h2, x1, tgt, wg, wu, wd, g_pff):
    T = x1.shape[0]

    def body(h2_ref, x1_ref, t_ref, wg_hbm, wu_hbm, wd_hbm, g_ref,
             gate_ref, up_ref, act_ref, df_ref, dx2_ref, st_ref, wg_vm, wu_vm, wd_vm, sem):
        _load_once([(wg_hbm, wg_vm), (wu_hbm, wu_vm), (wd_hbm, wd_vm)], sem)

        @pl.when(pl.program_id(0) == 0)
        def _():
            st_ref[...] = jnp.zeros_like(st_ref)

        h2 = h2_ref[...]
        gate = jnp.dot(h2, wg_vm[...], preferred_element_type=F32)
        up = jnp.dot(h2, wu_vm[...], preferred_element_type=F32)
        gate_ref[...] = gate
        up_ref[...] = up
        act = (gate * _sigmoid(gate) * up).astype(BF)
        act_ref[...] = act
        f = jnp.dot(act, wd_vm[...], preferred_element_type=F32)
        g = g_ref[...]
        x2 = x1_ref[...] + _rms_fwd(f, g)
        err = x2 - t_ref[...]
        st_ref[0:1, :] += 0.5 * jnp.sum(err * err, axis=0, keepdims=True) * (1.0 / D)
        dx2 = err * (1.0 / D)
        dx2_ref[...] = dx2
        df, dg = _rms_bwd(f, g, dx2)
        df_ref[...] = df.astype(BF)
        st_ref[1:2, :] += dg

    return pl.pallas_call(
        body, name="ffn_fwd", grid=(T // TT,),
        in_specs=[_rows(TT, D), _rows(TT, D), _rows(TT, D), ANY, ANY, ANY, _whole((1, D))],
        out_specs=[_rows(TT, DFF), _rows(TT, DFF), _rows(TT, DFF), _rows(TT, D), _rows(TT, D), _whole((8, D))],
        out_shape=[S((T, DFF), F32), S((T, DFF), F32), S((T, DFF), BF), S((T, D), BF), S((T, D), F32), S((8, D), F32)],
        scratch_shapes=[pltpu.VMEM((D, DFF), BF), pltpu.VMEM((D, DFF), BF), pltpu.VMEM((DFF, D), BF),
                        pltpu.SemaphoreType.DMA((3,))],
        compiler_params=_params(),
    )(h2, x1, tgt, wg, wu, wd, g_pff)


def _ffn_bwd(df, gate, up, wdT, wgT, wuT):
    T = df.shape[0]

    def body(df_ref, gate_ref, up_ref, wd_hbm, wg_hbm, wu_hbm, dgate_ref, dup_ref, dh2_ref, wd_vm, wg_vm, wu_vm, sem):
        _load_once([(wd_hbm, wd_vm), (wg_hbm, wg_vm), (wu_hbm, wu_vm)], sem)
        dact = jnp.dot(df_ref[...], wd_vm[...], preferred_element_type=F32)
        gate = gate_ref[...]
        s = _sigmoid(gate)
        dup = (dact * (gate * s)).astype(BF)
        dgate = (dact * up_ref[...] * (s + gate * s * (1.0 - s))).astype(BF)
        dup_ref[...] = dup
        dgate_ref[...] = dgate
        dh2_ref[...] = (jnp.dot(dgate, wg_vm[...], preferred_element_type=F32)
                        + jnp.dot(dup, wu_vm[...], preferred_element_type=F32))

    return pl.pallas_call(
        body, name="ffn_bwd", grid=(T // TT,),
        in_specs=[_rows(TT, D), _rows(TT, DFF), _rows(TT, DFF), ANY, ANY, ANY],
        out_specs=[_rows(TT, DFF), _rows(TT, DFF), _rows(TT, D)],
        out_shape=[S((T, DFF), BF), S((T, DFF), BF), S((T, D), F32)],
        scratch_shapes=[pltpu.VMEM((D, DFF), BF), pltpu.VMEM((DFF, D), BF), pltpu.VMEM((DFF, D), BF),
                        pltpu.SemaphoreType.DMA((3,))],
        compiler_params=_params(),
    )(df, gate, up, wdT, wgT, wuT)


def _mix_bwd(dh2, x1, dx2, mix, woutT, g_pf, g_pm):
    T = x1.shape[0]

    def body(dh2_ref, x1_ref, dx2_ref, mix_ref, w_hbm, gpf_ref, gpm_ref,
             dx1_ref, dmix_ref, dyl_ref, dys_ref, st_ref, w_vm, sem):
        _load_once([(w_hbm, w_vm)], sem)

        @pl.when(pl.program_id(0) == 0)
        def _():
            st_ref[...] = jnp.zeros_like(st_ref)

        dxa, dgpf = _rms_bwd(x1_ref[...], gpf_ref[...], dh2_ref[...])
        dx1 = dx2_ref[...] + dxa
        dx1_ref[...] = dx1
        dmix, dgpm = _rms_bwd(mix_ref[...], gpm_ref[...], dx1)
        dmix = dmix.astype(BF)
        dmix_ref[...] = dmix
        st_ref[0:1, :] += dgpf
        st_ref[1:2, :] += dgpm
        dyl_ref[...] = jnp.dot(dmix, w_vm[:, 0:LW], preferred_element_type=F32)
        dys_ref[...] = jnp.dot(dmix, w_vm[:, LW:LW + SI], preferred_element_type=F32)

    return pl.pallas_call(
        body, name="mix_bwd", grid=(T // TT,),
        in_specs=[_rows(TT, D), _rows(TT, D), _rows(TT, D), _rows(TT, D), ANY, _whole((1, D)), _whole((1, D))],
        out_specs=[_rows(TT, D), _rows(TT, D), _rows(TT, LW), _rows(TT, SI), _whole((8, D))],
        out_shape=[S((T, D), F32), S((T, D), BF), S((T, LW), F32), S((T, SI), F32), S((8, D), F32)],
        scratch_shapes=[pltpu.VMEM((D, LW + SI), BF), pltpu.SemaphoreType.DMA((1,))],
        compiler_params=_params(),
    )(dh2, x1, dx2, mix, woutT, g_pf, g_pm)


def _halo(width, n_tiles, tile):
    per = tile // 8
    return pl.BlockSpec((8, width), lambda i: (jnp.maximum((n_tiles - 1 - i) * per - 1, 0), 0))


def _lru_bwd(dy, lxr, lg, h, p_lru, wa4, wx4, wa4T, wx4T):
    T = dy.shape[0]
    NT = T // TT

    def body(dy_ref, lx_ref, lxh_ref, lg_ref, h_ref, hh_ref, p_ref, wa_ref, wx_ref, waT_ref, wxT_ref,
             dlx_ref, dlg_ref, st_ref, dwa_ref, dwx_ref, xp, hp, dp, a_s, d_s, g_s, cc):
        first = pl.program_id(0) == 0
        top = pl.program_id(0) == NT - 1

        @pl.when(first)
        def _():
            st_ref[...] = jnp.zeros_like(st_ref)
            dwa_ref[...] = jnp.zeros_like(dwa_ref)
            dwx_ref[...] = jnp.zeros_like(dwx_ref)
            dp[TT:TT + 8, :] = jnp.zeros((8, LW), F32)
            cc[...] = jnp.zeros_like(cc)

        keep = jnp.where(top, 0.0, 1.0)
        xp[0:8, :] = lxh_ref[...] * keep
        xp[8:8 + TT, :] = lx_ref[...]
        hp[0:8, :] = hh_ref[...] * keep
        hp[8:8 + TT, :] = h_ref[...]
        lx, taps = _conv_from(xp, p_ref, TT)
        r, i, sp, a, mult = _lru_gates(lx, p_ref, wa_ref, wx_ref)

        lg = lg_ref[...]
        hcur = h_ref[...]
        ge = _gelu(lg)
        dgated, dgn = _rms_bwd(hcur * ge, p_ref[8:9, :], dy_ref[...])
        st_ref[8:9, :] += dgn
        dlg_ref[...] = (dgated * hcur * _gelu_grad(lg)).astype(BF)
        a_s[...] = a
        d_s[...] = dgated * ge

        def step(k, c):
            t = TT - 1 - k
            g = d_s[pl.ds(t, 1), :] + c
            g_s[pl.ds(t, 1), :] = g
            return a_s[pl.ds(t, 1), :] * g

        cc[0:1, :] = lax.fori_loop(0, TT, step, cc[0:1, :], unroll=8)
        gt = g_s[...]
        da = gt * hp[pl.ds(7, TT), :]
        dmult = gt * i * lx
        di = gt * mult * lx
        dlxc = gt * mult * i
        dla = da * a - dmult * (a * a) / mult
        dr = dla * (-LRU_C * sp)
        st_ref[7:8, :] += jnp.sum(dla * (-LRU_C * r), axis=0, keepdims=True) * (-_sigmoid(-p_ref[7:8, :]))
        dzr = dr * r * (1.0 - r)
        dzi = di * i * (1.0 - i)
        st_ref[5:6, :] += jnp.sum(dzr, axis=0, keepdims=True)
        st_ref[6:7, :] += jnp.sum(dzi, axis=0, keepdims=True)
        dlxc = dlxc + _blockdiag_mm(dzr, waT_ref) + _blockdiag_mm(dzi, wxT_ref)
        for j in range(4):
            sl = slice(256 * j, 256 * (j + 1))
            pa = _mm_tn(lx[:, sl], dzr[:, sl])
            px = _mm_tn(lx[:, sl], dzi[:, sl])
            for b in range(4):
                bs = slice(BW * b, BW * (b + 1))
                dwa_ref[4 * j + b] += pa[bs, bs]
                dwx_ref[4 * j + b] += px[bs, bs]
        dp[0:TT, :] = dlxc
        acc = p_ref[0:1, :] * dp[pl.ds(CONV_K - 1, TT), :]
        for k in range(1, CONV_K):
            acc = acc + p_ref[k:k + 1, :] * dp[pl.ds(CONV_K - 1 - k, TT), :]
        dlx_ref[...] = acc.astype(BF)
        dp[TT:TT + 8, :] = dp[0:8, :]
        for k in range(CONV_K):
            st_ref[k:k + 1, :] += jnp.sum(dlxc * taps[k], axis=0, keepdims=True)
        st_ref[4:5, :] += jnp.sum(dlxc, axis=0, keepdims=True)

    w4 = _whole((4, 256, 256))
    return pl.pallas_call(
        body, name="lru_bwd", grid=(NT,),
        in_specs=[_rows(TT, LW, NT), _rows(TT, LW, NT), _halo(LW, NT, TT), _rows(TT, LW, NT), _rows(TT, LW, NT),
                  _halo(LW, NT, TT), _whole((16, LW)), w4, w4, w4, w4],
        out_specs=[_rows(TT, LW, NT), _rows(TT, LW, NT), _whole((16, LW)), _whole((NBLK, BW, BW)), _whole((NBLK, BW, BW))],
        out_shape=[S((T, LW), BF), S((T, LW), BF), S((16, LW), F32), S((NBLK, BW, BW), F32), S((NBLK, BW, BW), F32)],
        scratch_shapes=[pltpu.VMEM((TT + 8, LW), F32), pltpu.VMEM((TT + 8, LW), F32), pltpu.VMEM((TT + 8, LW), F32),
                        pltpu.VMEM((TT, LW), F32), pltpu.VMEM((TT, LW), F32), pltpu.VMEM((TT, LW), F32),
                        pltpu.VMEM((8, LW), F32)],
        compiler_params=_params(),
    )(dy, lxr, lxr, lg, h, h, p_lru, wa4, wx4, wa4T, wx4T)


def _ssd_bwd(dyn, xbcr, z, dtr, y, states, cw_ssd, hp_ssd, g_ssd):
    T = dyn.shape[0]
    NC = T // CH

    def body(dyn_ref, xr_ref, xh_ref, z_ref, dt_ref, y_ref, st_ref, cw_ref, hp_ref, g_ref,
             dxbc_ref, dz_ref, ddt_ref, cst_ref, hst_ref, gst_ref, xp, dp, dS, dxb, yo_s, q_s, dxs_s, t1_s):
        first = pl.program_id(0) == 0
        top = pl.program_id(0) == NC - 1

        @pl.when(first)
        def _():
            cst_ref[...] = jnp.zeros_like(cst_ref)
            hst_ref[...] = jnp.zeros_like(hst_ref)
            gst_ref[...] = jnp.zeros_like(gst_ref)
            dp[CH:CH + 8, :] = jnp.zeros((8, XBC), F32)
            dS[...] = jnp.zeros_like(dS)

        xp[0:8, :] = xh_ref[...] * jnp.where(top, 0.0, 1.0)
        cv, sg, xbc, raw, dtv, A, cs, taps = _ssd_prep(xp, xr_ref, dt_ref, cw_ref, hp_ref)
        csT, dsm, E_x, ds_x, El_rows = _ssd_decays(cs)
        row_i = lax.broadcasted_iota(jnp.int32, (CH, CH), 0)
        col_i = lax.broadcasted_iota(jnp.int32, (CH, CH), 1)
        tril = row_i >= col_i
        first = col_i < HD
        head_of = ((lax.broadcasted_iota(jnp.int32, (DTP, SI), 1) >> 6)
                   == lax.broadcasted_iota(jnp.int32, (DTP, SI), 0)).astype(BF)
        head_ofT = ((lax.broadcasted_iota(jnp.int32, (SI, DTP), 0) >> 6)
                    == lax.broadcasted_iota(jnp.int32, (SI, DTP), 1)).astype(BF)

        def hi_lo(v):
            hi = v.astype(BF)
            return hi, (v - hi.astype(F32)).astype(BF)

        def lane_sums(v):
            hi, lo = hi_lo(v)
            return _mm(hi, head_ofT) + _mm(lo, head_ofT)

        zz = z_ref[...]
        sz = _sigmoid(zz)
        yv = y_ref[...]
        dgn, dg = _rms_bwd(yv * (zz * sz), g_ref[...], dyn_ref[...])
        gst_ref[0:1, :] += dg
        dz_ref[...] = (dgn * yv * (sz + zz * sz * (1.0 - sz))).astype(BF)
        dY = dgn * (zz * sz)

        X = xbc[:, 0:SI]
        dt_x = _per_head_lanes(dtv)
        xs = X * dt_x
        xsd = (xs * ds_x).astype(BF)
        D_x = _per_head_lanes(hp_ref[...])[2:3, :]
        dcs_col = jnp.zeros((CH, DTP), F32)
        dcs_row = jnp.zeros((CH, DTP), F32)
        GW = HPG * HD
        for g in range(NG):
            gs = slice(GW * g, GW * (g + 1))
            Bg = xbc[:, SI + NS * g:SI + NS * (g + 1)].astype(BF)
            Cg = xbc[:, SI + NG * NS + NS * g:SI + NG * NS + NS * (g + 1)].astype(BF)
            G = _mm_nt(Cg, Bg)
            Sg = st_ref[0, gs, :]
            dSe = dS[gs, :]
            dYg = dY[:, gs]
            yo_s[:, gs] = _mm_nt(Cg, Sg) * E_x[:, gs]
            dP = dYg * E_x[:, gs]
            dCg = _mm(dP, Sg)
            dS[gs, :] = _mm_tn(dP, Cg) + _per_head_rows(El_rows, g) * dSe
            t1_s[gs, :] = dSe * Sg
            Q = _mm_nt(Bg, dSe)
            q_s[:, gs] = Q
            dBg = _mm(xsd[:, gs], dSe)
            dG = jnp.zeros((CH, CH), F32)
            for jj in range(HPG // 2):
                j = g * (HPG // 2) + jj
                ps = slice(2 * HD * j, 2 * HD * (j + 1))
                xs_pair = xs[:, ps]
                dxs_pair = Q[:, 2 * HD * jj:2 * HD * (jj + 1)] * ds_x[:, ps]
                for e in range(2):
                    h = 2 * j + e
                    Lm = jnp.exp(jnp.where(tril, cs[:, h:h + 1] - csT[h:h + 1, :], -1e30))
                    M = G * Lm
                    dYm = jnp.where(first if e == 0 else ~first, dY[:, ps], 0.0).astype(BF)
                    dM = _mm_nt(dYm, xs_pair)
                    dxs_pair = dxs_pair + _mm_tn(M, dYm)
                    Wm = dM * M
                    dcs_col = dcs_col + jnp.where(col_i == h, jnp.sum(Wm, axis=1, keepdims=True), 0.0)
                    dcs_row = dcs_row + jnp.where(row_i == h, -jnp.sum(Wm, axis=0, keepdims=True), 0.0)
                    dG = dG + dM * Lm
                dxs_s[:, ps] = dxs_pair
            dxb[:, SI + NS * g:SI + NS * (g + 1)] = dBg + _mm_tn(dG, Cg)
            dxb[:, SI + NG * NS + NS * g:SI + NG * NS + NS * (g + 1)] = dCg + _mm(dG, Bg)

        dxs = dxs_s[...]
        dxb[:, 0:SI] = D_x * dY + dxs * dt_x
        dds = lane_sums(q_s[...] * xs) * dsm
        dcs_col = dcs_col + lane_sums(dY * yo_s[...]) - dds
        ddt_col = lane_sums(dxs * X)
        dD = jnp.sum(lane_sums(dY * X), axis=0, keepdims=True)
        t_hi, t_lo = hi_lo(t1_s[...])
        dcl_rows = jnp.sum(_mm(head_of, t_hi) + _mm(head_of, t_lo), axis=1, keepdims=True) * jnp.exp(csT[:, CH - 1:CH])
        dcs_row = dcs_row + jnp.where(col_i == CH - 1, dcl_rows, 0.0)
        dcs_col = dcs_col + jnp.where(row_i == CH - 1, jnp.sum(dds, axis=0, keepdims=True), 0.0)

        da = _rev_cumsum_rows(dcs_col + dcs_row.T, CH)
        ddt_col = ddt_col + da * A
        hst_ref[1:2, :] += jnp.sum(da * dtv, axis=0, keepdims=True) * A
        hst_ref[2:3, :] += dD
        draw = jnp.where(col_i < NH, ddt_col * _sigmoid(raw), 0.0)
        ddt_ref[...] = draw.astype(BF)
        hst_ref[0:1, :] += jnp.sum(draw, axis=0, keepdims=True)

        dcv = dxb[...] * (sg + cv * sg * (1.0 - sg))
        dp[0:CH, :] = dcv
        acc = cw_ref[0:1, :] * dp[pl.ds(CONV_K - 1, CH), :]
        for k in range(1, CONV_K):
            acc = acc + cw_ref[k:k + 1, :] * dp[pl.ds(CONV_K - 1 - k, CH), :]
        dxbc_ref[...] = acc.astype(BF)
        dp[CH:CH + 8, :] = dp[0:8, :]
        for k in range(CONV_K):
            cst_ref[k:k + 1, :] += jnp.sum(dcv * taps[k], axis=0, keepdims=True)
        cst_ref[4:5, :] += jnp.sum(dcv, axis=0, keepdims=True)

    return pl.pallas_call(
        body, name="ssd_bwd", grid=(NC,),
        in_specs=[_rows(CH, SI, NC), _rows(CH, XBC, NC), _halo(XBC, NC, CH), _rows(CH, SI, NC), _rows(CH, DTP, NC),
                  _rows(CH, SI, NC), pl.BlockSpec((1, NH * HD, NS), lambda i: (NC - 1 - i, 0, 0)),
                  _whole((8, XBC)), _whole((8, DTP)), _whole((1, SI))],
        out_specs=[_rows(CH, XBC, NC), _rows(CH, SI, NC), _rows(CH, DTP, NC), _whole((16, XBC)), _whole((16, DTP)),
                   _whole((8, SI))],
        out_shape=[S((T, XBC), BF), S((T, SI), BF), S((T, DTP), BF), S((16, XBC), F32), S((16, DTP), F32), S((8, SI), F32)],
        scratch_shapes=[pltpu.VMEM((CH + 8, XBC), F32), pltpu.VMEM((CH + 8, XBC), F32), pltpu.VMEM((NH * HD, NS), F32),
                        pltpu.VMEM((CH, XBC), F32), pltpu.VMEM((CH, SI), F32), pltpu.VMEM((CH, SI), F32),
                        pltpu.VMEM((CH, SI), F32), pltpu.VMEM((NH * HD, NS), F32)],
        compiler_params=_params(),
    )(dyn, xbcr, xbcr, z, dtr, y, states, cw_ssd, hp_ssd, g_ssd)


def _inproj_bwd(dlx, dlg, dz, dxbc, ddt, x, dx1, wcatT, g0):
    T = x.shape[0]

    def body(dlx_ref, dlg_ref, dz_ref, dxbc_ref, ddt_ref, x_ref, dx1_ref, w_hbm, g_ref, dx_ref, st_ref, w_vm, sem):
        _load_once([(w_hbm, w_vm)], sem)

        @pl.when(pl.program_id(0) == 0)
        def _():
            st_ref[...] = jnp.zeros_like(st_ref)

        dh = jnp.dot(dlx_ref[...], w_vm[0:1024, :], preferred_element_type=F32)
        dh = dh + jnp.dot(dlg_ref[...], w_vm[1024:2048, :], preferred_element_type=F32)
        dh = dh + jnp.dot(dz_ref[...], w_vm[2048:3072, :], preferred_element_type=F32)
        dh = dh + jnp.dot(dxbc_ref[...], w_vm[3072:3072 + XBC, :], preferred_element_type=F32)
        dh = dh + jnp.dot(ddt_ref[...], w_vm[3072 + XBC:PC, :], preferred_element_type=F32)
        dx, dg = _rms_bwd(x_ref[...], g_ref[...], dh)
        dx_ref[...] = dx1_ref[...] + dx
        st_ref[0:1, :] += dg

    return pl.pallas_call(
        body, name="inproj_bwd", grid=(T // TT,),
        in_specs=[_rows(TT, 1024), _rows(TT, 1024), _rows(TT, 1024), _rows(TT, XBC), _rows(TT, DTP), _rows(TT, D),
                  _rows(TT, D), ANY, _whole((1, D))],
        out_specs=[_rows(TT, D), _whole((8, D))],
        out_shape=[S((T, D), F32), S((8, D), F32)],
        scratch_shapes=[pltpu.VMEM((PC, D), BF), pltpu.SemaphoreType.DMA((1,))],
        compiler_params=_params(),
    )(dlx, dlg, dz, dxbc, ddt, x, dx1, wcatT, g0)


def _wgrad(name, a, b):
    T, M = a.shape
    N = b.shape[1]
    tk = min(T, 2048 if M <= 1024 else 1024)
    tn = N
    while M * tn * 4 > (6 << 20) and tn % 256 == 0:
        tn //= 2

    def body(a_ref, b_ref, o_ref):
        p = lax.dot_general(a_ref[...], b_ref[...], (((0,), (0,)), ((), ())), preferred_element_type=F32)

        @pl.when(pl.program_id(1) == 0)
        def _():
            o_ref[...] = p

        @pl.when(pl.program_id(1) > 0)
        def _():
            o_ref[...] += p

    return pl.pallas_call(
        body, name=name, grid=(N // tn, T // tk),
        in_specs=[pl.BlockSpec((tk, M), lambda j, k: (k, 0)), pl.BlockSpec((tk, tn), lambda j, k: (k, j))],
        out_specs=pl.BlockSpec((M, tn), lambda j, k: (0, j)), out_shape=S((M, N), F32),
        compiler_params=_params(2),
    )(a, b)


def _adamw(name, w, g, m, v):
    R, C = w.shape
    tr = _row_tile(R, C)

    def body(w_ref, g_ref, m_ref, v_ref, d_ref, nm_ref, nv_ref):
        d_ref[...], nm_ref[...], nv_ref[...] = _adam_math(w_ref[...], g_ref[...], m_ref[...], v_ref[...])

    blk = pl.BlockSpec((tr, C), lambda i: (i, 0))
    return pl.pallas_call(
        body, name=name, grid=(R // tr,),
        in_specs=[blk] * 4, out_specs=[blk] * 3, out_shape=[S((R, C), F32)] * 3,
        compiler_params=_params(),
    )(w, g, m, v)


def _pos():
    return lax.axis_index("x"), lax.axis_index("y"), lax.axis_index("c")


def _other_chips(x, y):
    return [(1 - x, y), (x, 1 - y), (1 - x, 1 - y)]


def _half(ref, c, hr):
    sl = pl.ds(pl.multiple_of(c * hr, 8), hr)
    return ref.at[:, sl, :] if len(ref.shape) == 3 else ref.at[sl, :]


def _remote(src, dst, send_sem, recv_sem, to):
    return pltpu.make_async_remote_copy(src_ref=src, dst_ref=dst, send_sem=send_sem, recv_sem=recv_sem,
                                        device_id=to, device_id_type=MESH)


def _allgather_weights(shards):
    n = len(shards)

    def body(*refs):
        ins, outs = refs[:n], refs[n:2 * n]
        send_sems, recv_sems = refs[2 * n], refs[2 * n + 1]
        x, y, c = _pos()
        me = 2 * x + y
        chips = _other_chips(x, y)
        first, passed = [], []
        for i, (src, dst) in enumerate(zip(ins, outs)):
            hr = src.shape[0] // 2
            my_half = pl.ds(pl.multiple_of(c * hr, 16), hr)
            for k, (cx, cy) in enumerate(chips):
                cp = _remote(src.at[my_half, :], dst.at[me, my_half, :], send_sems.at[6 * i + k], recv_sems.at[6 * i + k],
                             (cx, cy, c))
                cp.start()
                first.append(cp)
        for i, (src, dst) in enumerate(zip(ins, outs)):
            hr = src.shape[0] // 2
            my_half = pl.ds(pl.multiple_of(c * hr, 16), hr)
            for k, (cx, cy) in enumerate(chips):
                blk = dst.at[2 * cx + cy, my_half, :]
                _remote(blk, blk, send_sems.at[6 * i + k], recv_sems.at[6 * i + k], (cx, cy, c)).wait_recv()
                fw = _remote(blk, blk, send_sems.at[6 * i + 3 + k], recv_sems.at[6 * i + 3 + k], (x, y, 1 - c))
                fw.start()
                passed.append(fw)
        for i, (src, dst) in enumerate(zip(ins, outs)):
            hr = src.shape[0] // 2
            sib_half = pl.ds(pl.multiple_of((1 - c) * hr, 16), hr)
            for k, (cx, cy) in enumerate(chips):
                blk = dst.at[2 * cx + cy, sib_half, :]
                _remote(blk, blk, send_sems.at[6 * i + 3 + k], recv_sems.at[6 * i + 3 + k], (x, y, 1 - c)).wait_recv()
        for cp in first + passed:
            cp.wait_send()

    return pl.pallas_call(
        body, name="allgather_weights", in_specs=[ANY] * n, out_specs=[ANY] * n,
        out_shape=[S((4,) + s.shape, s.dtype) for s in shards],
        scratch_shapes=[pltpu.SemaphoreType.DMA((6 * n,)), pltpu.SemaphoreType.DMA((6 * n,))],
    )(*shards)


def _pair_exchange(bufs):
    n = len(bufs)

    def half_shape(b):
        return b.shape[:-2] + (b.shape[-2] // 2, b.shape[-1])

    def body(*refs):
        ins, outs = refs[:n], refs[n:2 * n]
        send_sems, recv_sems = refs[2 * n], refs[2 * n + 1]
        x, y, c = _pos()
        copies = [_remote(_half(src, 1 - c, src.shape[-2] // 2), dst, send_sems.at[k], recv_sems.at[k], (x, y, 1 - c))
                  for k, (src, dst) in enumerate(zip(ins, outs))]
        for cp in copies:
            cp.start()
        for cp in copies:
            cp.wait()

    return pl.pallas_call(
        body, name="pair_exchange", in_specs=[ANY] * n, out_specs=[ANY] * n,
        out_shape=[S(half_shape(b), b.dtype) for b in bufs],
        scratch_shapes=[pltpu.SemaphoreType.DMA((n,)), pltpu.SemaphoreType.DMA((n,))],
    )(*bufs)


def _quad_exchange(bufs, scatter):
    n = len(bufs)

    def body(*refs):
        ins, outs = refs[:n], refs[n:2 * n]
        send_sems, recv_sems, local_sems = refs[2 * n], refs[2 * n + 1], refs[2 * n + 2]
        x, y, c = _pos()
        me = 2 * x + y
        chips = _other_chips(x, y)
        copies, locals_ = [], []
        for k, (src, dst) in enumerate(zip(ins, outs)):
            if not scatter[k]:
                own = pltpu.make_async_copy(src, dst.at[me], local_sems.at[k])
                own.start()
                locals_.append(own)
            for j, (cx, cy) in enumerate(chips):
                piece = src.at[2 * cx + cy] if scatter[k] else src
                cp = _remote(piece, dst.at[me], send_sems.at[3 * k + j], recv_sems.at[3 * k + j], (cx, cy, c))
                cp.start()
                copies.append(cp)
        for k, (src, dst) in enumerate(zip(ins, outs)):
            for j, (cx, cy) in enumerate(chips):
                blk = dst.at[2 * cx + cy]
                _remote(blk, blk, send_sems.at[3 * k + j], recv_sems.at[3 * k + j], (cx, cy, c)).wait_recv()
        for cp in copies:
            cp.wait_send()
        for cp in locals_:
            cp.wait()

    return pl.pallas_call(
        body, name="quad_exchange", in_specs=[ANY] * n, out_specs=[ANY] * n,
        out_shape=[S((4,) + (b.shape[1:] if sc else b.shape), b.dtype) for b, sc in zip(bufs, scatter)],
        scratch_shapes=[pltpu.SemaphoreType.DMA((3 * n,)), pltpu.SemaphoreType.DMA((3 * n,)), pltpu.SemaphoreType.DMA((n,))],
    )(*bufs)


def _pair_gather(bufs):
    n = len(bufs)

    def body(*refs):
        ins, outs = refs[:n], refs[n:2 * n]
        send_sems, recv_sems = refs[2 * n], refs[2 * n + 1]
        x, y, c = _pos()
        copies = []
        for k, buf in enumerate(outs):
            mine = _half(buf, c, buf.shape[0] // 2)
            cp = _remote(mine, mine, send_sems.at[k], recv_sems.at[k], (x, y, 1 - c))
            cp.start()
            copies.append(cp)
        for k, buf in enumerate(outs):
            theirs = _half(buf, 1 - c, buf.shape[0] // 2)
            _remote(theirs, theirs, send_sems.at[k], recv_sems.at[k], (x, y, 1 - c)).wait_recv()
        for cp in copies:
            cp.wait_send()

    return pl.pallas_call(
        body, name="pair_gather", in_specs=[ANY] * n, out_specs=[ANY] * n,
        out_shape=[S(b.shape, b.dtype) for b in bufs], input_output_aliases={k: k for k in range(n)},
        scratch_shapes=[pltpu.SemaphoreType.DMA((n,)), pltpu.SemaphoreType.DMA((n,))],
    )(*bufs)


def _row_tile(rows, cols, mult=8):
    best = mult
    for t in range(mult, rows + 1, mult):
        if rows % t == 0 and t * cols * 4 <= (1 << 20):
            best = t
    return best


def _add_own_half(name, full, got, c, out_dtype, by_columns):
    hr = got.shape[-2]
    wide = got.shape[-1]
    cols = wide // 4 if by_columns else wide
    tr = _row_tile(hr, wide, 16)
    per = hr // tr

    if by_columns:
        def body(c_ref, a_ref, b_ref, o_ref):
            v = a_ref[...] + b_ref[...]
            for j in range(4):
                o_ref[j] = v[:, j * cols:(j + 1) * cols].astype(out_dtype)

        in_specs = [pl.BlockSpec((tr, wide), lambda i, c_ref: (c_ref[0] * per + i, 0)),
                    pl.BlockSpec((tr, wide), lambda i, c_ref: (i, 0))]
        out_specs = pl.BlockSpec((4, tr, cols), lambda i, c_ref: (0, i, 0))
        grid = (per,)
    else:
        def body(c_ref, a_ref, b_ref, o_ref):
            o_ref[...] = (a_ref[...] + b_ref[...]).astype(out_dtype)

        in_specs = [pl.BlockSpec((1, tr, cols), lambda s, i, c_ref: (s, c_ref[0] * per + i, 0)),
                    pl.BlockSpec((1, tr, cols), lambda s, i, c_ref: (s, i, 0))]
        out_specs = pl.BlockSpec((1, tr, cols), lambda s, i, c_ref: (s, i, 0))
        grid = (4, per)
    return pl.pallas_call(
        body, name=name,
        grid_spec=pltpu.PrefetchScalarGridSpec(num_scalar_prefetch=1, grid=grid, in_specs=in_specs, out_specs=out_specs),
        out_shape=S((4, hr, cols), out_dtype), compiler_params=_params(len(grid)),
    )(jnp.reshape(c, (1,)).astype(jnp.int32), full, got)


def _small_add_own_half(fulls, gots, c):
    n = len(fulls)

    def body(c_ref, *refs):
        for a_ref, b_ref, o_ref in zip(refs[:n], refs[n:2 * n], refs[2 * n:]):
            hr = b_ref.shape[0]
            o_ref[...] = a_ref[pl.ds(pl.multiple_of(c_ref[0] * hr, 8), hr), :] + b_ref[...]

    specs = lambda arrs: [pl.BlockSpec(a.shape, lambda i, c_ref: (0, 0)) for a in arrs]
    return pl.pallas_call(
        body, name="small_pair_add",
        grid_spec=pltpu.PrefetchScalarGridSpec(num_scalar_prefetch=1, grid=(1,), in_specs=specs(fulls) + specs(gots),
                                               out_specs=specs(gots)),
        out_shape=[S(g.shape, F32) for g in gots], compiler_params=_params(),
    )(jnp.reshape(c, (1,)).astype(jnp.int32), *fulls, *gots)


def _small_sum_slots(slots, c):
    n = len(slots)

    def body(c_ref, *refs):
        for s_ref, o_ref in zip(refs[:n], refs[n:]):
            hr = s_ref.shape[1]
            o_ref[pl.ds(pl.multiple_of(c_ref[0] * hr, 8), hr), :] = ((s_ref[0] + s_ref[1]) + s_ref[2]) + s_ref[3]

    outs = [S((2 * s.shape[1], s.shape[2]), F32) for s in slots]
    return pl.pallas_call(
        body, name="small_quad_sum",
        grid_spec=pltpu.PrefetchScalarGridSpec(
            num_scalar_prefetch=1, grid=(1,),
            in_specs=[pl.BlockSpec(s.shape, lambda i, c_ref: (0, 0, 0)) for s in slots],
            out_specs=[pl.BlockSpec(o.shape, lambda i, c_ref: (0, 0)) for o in outs]),
        out_shape=outs, compiler_params=_params(),
    )(jnp.reshape(c, (1,)).astype(jnp.int32), *slots)


def _sum_slots(name, own, slots, me, c):
    _, rows, cols = slots.shape
    tr = _row_tile(rows, cols, 16 if slots.dtype == jnp.bfloat16 else 8)
    per = rows // tr
    three = len(own.shape) == 3

    def body(p_ref, own_ref, s0, s1, s2, s3, o_ref):
        mine = own_ref[0] if three else own_ref[...]
        acc = None
        for j, s_ref in enumerate((s0, s1, s2, s3)):
            v = jnp.where(p_ref[0] == j, mine, s_ref[0]).astype(F32)
            acc = v if acc is None else acc + v
        o_ref[...] = acc

    def slot_spec(j):
        return pl.BlockSpec((1, tr, cols), lambda i, p: (jnp.where(p[0] == j, (j + 1) % 4, j), i, 0))

    own_spec = (pl.BlockSpec((1, tr, cols), lambda i, p: (p[0], i, 0)) if three
                else pl.BlockSpec((tr, cols), lambda i, p: (i, 0)))
    return pl.pallas_call(
        body, name=name,
        grid_spec=pltpu.PrefetchScalarGridSpec(
            num_scalar_prefetch=1, grid=(per,), in_specs=[own_spec] + [slot_spec(j) for j in range(4)],
            out_specs=pl.BlockSpec((tr, cols), lambda i, p: (p[1] * per + i, 0))),
        out_shape=S((2 * rows, cols), F32), compiler_params=_params(),
    )(jnp.stack([me, c]).astype(jnp.int32), own, slots, slots, slots, slots)


BIG = ("w_in", "w_out", "w_gate", "w_up", "w_down")
ROW_PARAMS = (("pre_mix_norm", 0), ("lru_conv_b", 12), ("lru_ba", 13), ("lru_bx", 14), ("lru_lambda", 15),
              ("lru_out_norm", 16), ("ssd_out_norm", 24), ("post_mix_norm", 33), ("pre_ffn_norm", 32), ("post_ffn_norm", 41))
LRU_CONV_ROWS = (8, 12)
LOSS_ROW = 40
HEAD_PARAMS = (("ssd_dt_bias", 0), ("ssd_a_log", 1), ("ssd_d", 2))
SMALL = tuple(n for n, _ in ROW_PARAMS) + ("ssd_conv_b",) + tuple(n for n, _ in HEAD_PARAMS) + (
    "lru_wa", "lru_wx", "lru_conv_w", "ssd_conv_w")


def _diag4(w):
    eye = jnp.eye(4, dtype=w.dtype).reshape(1, 4, 1, 4, 1)
    return (w.reshape(4, 4, BW, 1, BW) * eye).reshape(4, 4 * BW, 4 * BW)


def _adam_math(w, g, m, v):
    mm = ADAM_B1 * m + (1.0 - ADAM_B1) * g
    vv = ADAM_B2 * v + (1.0 - ADAM_B2) * (g * g)
    c1 = 1.0 - ADAM_B1 ** ADAM_STEP
    c2 = 1.0 - ADAM_B2 ** ADAM_STEP
    return -ADAM_LR * ((mm / c1) / (jnp.sqrt(vv / c2) + ADAM_EPS) + ADAM_WD * w), mm, vv


def _adamw_small(rows, cst, hst, dwa, dwx, glcw, gscw, w, m, v):
    def grad_of(name, refs):
        rows_ref, cst_ref, hst_ref, dwa_ref, dwx_ref, glcw_ref, gscw_ref = refs
        for n, r in ROW_PARAMS:
            if n == name:
                return rows_ref[r:r + 1, :]
        for n, r in HEAD_PARAMS:
            if n == name:
                return hst_ref[r:r + 1, 0:NH]
        return {"ssd_conv_b": lambda: cst_ref[4:5, :], "lru_wa": lambda: dwa_ref[...], "lru_wx": lambda: dwx_ref[...],
                "lru_conv_w": lambda: glcw_ref[...], "ssd_conv_w": lambda: gscw_ref[...]}[name]()

    shapes = {n: (w[n].shape[1:] if len(w[n].shape) > 2 else w[n].shape) for n in SMALL}
    flat = lambda d: [d[n].reshape(shapes[n]) for n in SMALL]
    ns = len(SMALL)

    def body(*refs):
        srcs, rest = refs[:7], refs[7:]
        w_refs, m_refs, v_refs = rest[:ns], rest[ns:2 * ns], rest[2 * ns:3 * ns]
        outs = rest[3 * ns:]
        for k, name in enumerate(SMALL):
            g = grad_of(name, srcs)
            d, mm, vv = _adam_math(w_refs[k][...], g, m_refs[k][...], v_refs[k][...])
            outs[4 * k][...] = g
            outs[4 * k + 1][...] = d
            outs[4 * k + 2][...] = mm
            outs[4 * k + 3][...] = vv

    res = pl.pallas_call(
        body, name="adamw_small",
        out_shape=[S(shapes[n], F32) for n in SMALL for _ in range(4)],
        compiler_params=pltpu.CompilerParams(vmem_limit_bytes=VMEM_LIMIT),
    )(rows, cst, hst, dwa, dwx, glcw, gscw, *flat(w), *flat(m), *flat(v))
    return {n: tuple(res[4 * k + i].reshape(w[n].shape) for i in range(4)) for k, n in enumerate(SMALL)}


def _gather_weights(w_in, w_out, w_gate, w_up, w_down, lru_conv_w, ssd_conv_w):
    conv = jnp.concatenate([lru_conv_w.reshape(-1), ssd_conv_w.reshape(-1)]).astype(F32)
    hi = conv.astype(jnp.bfloat16)
    mid = (conv - hi.astype(F32)).astype(jnp.bfloat16)
    lo = (conv - hi.astype(F32) - mid.astype(F32)).astype(jnp.bfloat16)
    terms = jnp.concatenate([hi, mid, lo])
    n_terms = terms.shape[0]
    conv_rows = -(-n_terms // (128 * 32)) * 32
    terms = jnp.pad(terms, (0, conv_rows * 128 - n_terms)).reshape(conv_rows, 128)
    own = [w.astype(jnp.bfloat16) for w in (w_in, w_out, w_gate, w_up, w_down)] + [terms]
    got = _allgather_weights(own)
    chip = 2 * lax.axis_index("x") + lax.axis_index("y")
    here = (jnp.arange(4) == chip).reshape(4, 1, 1)
    full = [jnp.where(here, o[None], g) for o, g in zip(own, got)]
    cols = lambda f: f.transpose(1, 0, 2).reshape(f.shape[1], 4 * f.shape[2])
    rows = lambda f: f.reshape(4 * f.shape[1], f.shape[2])
    win_f, wout_f, wg_f, wu_f, wd_f = cols(full[0]), rows(full[1]), cols(full[2]), cols(full[3]), rows(full[4])
    t3 = full[5].reshape(4, -1)[:, :n_terms].reshape(4, 3, -1).astype(F32)
    conv_f = (t3[:, 0] + t3[:, 1]) + t3[:, 2]
    n1 = lru_conv_w.size
    lcw = conv_f[:, :n1].reshape(4, CONV_K, -1).transpose(1, 0, 2).reshape(CONV_K, LW)
    scw = conv_f[:, n1:].reshape(4, CONV_K, -1).transpose(1, 0, 2).reshape(CONV_K, XBC)
    return win_f, wout_f, wg_f, wu_f, wd_f, lcw, scw


def _local_step(x, tgt, win_f, wout_f, wg_f, wu_f, wd_f, lcw, scw, sp):
    mm = lambda w: w.astype(BF)
    wcat = jnp.concatenate([mm(win_f), jnp.zeros((D, PC - IN_COLS), BF)], axis=1)
    row = lambda v: v.reshape(1, -1).astype(F32)
    p_lru = jnp.concatenate([lcw, row(sp["lru_conv_b"]), row(sp["lru_ba"]), row(sp["lru_bx"]), row(sp["lru_lambda"]),
                             row(sp["lru_out_norm"]), jnp.zeros((7, LW), F32)], axis=0)
    wa4, wx4 = mm(_diag4(sp["lru_wa"][0])), mm(_diag4(sp["lru_wx"][0]))
    wa4T, wx4T = wa4.transpose(0, 2, 1), wx4.transpose(0, 2, 1)
    cw_ssd = jnp.concatenate([scw, row(sp["ssd_conv_b"]), jnp.zeros((3, XBC), F32)], axis=0)
    padh = lambda v: jnp.pad(row(v), ((0, 0), (0, DTP - NH)))
    hp_ssd = jnp.concatenate([padh(sp["ssd_dt_bias"]), padh(sp["ssd_a_log"]), padh(sp["ssd_d"]), jnp.zeros((5, DTP), F32)], axis=0)
    g0, g_ssd = row(sp["pre_mix_norm"]), row(sp["ssd_out_norm"])
    g_pm, g_pf, g_pff = row(sp["post_mix_norm"]), row(sp["pre_ffn_norm"]), row(sp["post_ffn_norm"])
    wout, wg, wu, wd = mm(wout_f), mm(wg_f), mm(wu_f), mm(wd_f)

    h0, lxr, lg, z, xbcr, dtr = _inproj(x, g0, wcat)
    h, ylru = _lru_fwd(lxr, lg, p_lru, wa4, wx4)
    y, yssd, states = _ssd_fwd(xbcr, z, dtr, cw_ssd, hp_ssd, g_ssd)
    mix, x1, h2 = _outproj(ylru, yssd, x, wout, g_pm, g_pf)
    gate, up, act, df, dx2, st_ffn = _ffn_fwd(h2, x1, tgt, wg, wu, wd, g_pff)
    dgate, dup, dh2 = _ffn_bwd(df, gate, up, wd.T, wg.T, wu.T)
    dx1, dmix, dyl, dys, st_mix = _mix_bwd(dh2, x1, dx2, mix, wout.T, g_pf, g_pm)
    dlx, dlg, st_lru, dwa, dwx = _lru_bwd(dyl, lxr, lg, h, p_lru, wa4, wx4, wa4T, wx4T)
    dxbc, dz, ddt, cst, hst, gst = _ssd_bwd(dys, xbcr, z, dtr, y, states, cw_ssd, hp_ssd, g_ssd)
    gx, st_in = _inproj_bwd(dlx, dlg, dz, dxbc, ddt, x, dx1, wcat.T, g0)

    dwd = _wgrad("wgrad_down", act, df)
    dwg = _wgrad("wgrad_gate", h2, dgate)
    dwu = _wgrad("wgrad_up", h2, dup)
    dwo_l = _wgrad("wgrad_out_lru", ylru, dmix)
    dwo_s = _wgrad("wgrad_out_ssd", yssd, dmix)
    pin = [_wgrad("wgrad_in_%d" % k, h0, b) for k, b in enumerate((dlx, dlg, dz, dxbc, ddt))]
    dwin = jnp.concatenate(pin[:4] + [pin[4][:, :NH]], axis=1)
    big = {"w_in": dwin, "w_out": jnp.concatenate([dwo_l, dwo_s], axis=0), "w_gate": dwg, "w_up": dwu, "w_down": dwd}
    rows = jnp.concatenate([st_in, st_lru, gst, st_mix, st_ffn], axis=0)
    small = [rows, cst, hst, dwa.reshape(NBLK * BW, BW), dwx.reshape(NBLK * BW, BW)]
    return gx, big, small


def _reduce_grads(big, small):
    c = lax.axis_index("c")
    me = 2 * lax.axis_index("x") + lax.axis_index("y")
    nb = len(BIG)
    by_columns = [name in ("w_in", "w_gate", "w_up") for name in BIG]
    bufs = [big[n] if bc else big[n].reshape(4, big[n].shape[0] // 4, big[n].shape[1]) for n, bc in zip(BIG, by_columns)]
    got = list(_pair_exchange(bufs + list(small)))
    part = [_add_own_half("pair_add_%d" % k, b, r, c, jnp.bfloat16, bc)
            for k, (b, r, bc) in enumerate(zip(bufs, got[:nb], by_columns))]
    part_small = list(_small_add_own_half(list(small), got[nb:], c))
    slots = list(_quad_exchange(part + part_small, [True] * nb + [False] * len(small)))
    red = [_sum_slots("quad_sum_%d" % k, p, s, me, c) for k, (p, s) in enumerate(zip(part, slots[:nb]))]
    red_small = list(_small_sum_slots(slots[nb:], c))
    out = list(_pair_gather(red + red_small))
    return out[:nb], out[nb:]


def kernel(x, pre_mix_norm, w_in, lru_conv_w, lru_conv_b, lru_wa, lru_ba, lru_wx, lru_bx, lru_lambda, lru_out_norm, ssd_conv_w, ssd_conv_b, ssd_dt_bias, ssd_a_log, ssd_d, ssd_out_norm, w_out, post_mix_norm, pre_ffn_norm, w_gate, w_up, w_down, post_ffn_norm, loss_target, m_pre_mix_norm, m_w_in, m_lru_conv_w, m_lru_conv_b, m_lru_wa, m_lru_ba, m_lru_wx, m_lru_bx, m_lru_lambda, m_lru_out_norm, m_ssd_conv_w, m_ssd_conv_b, m_ssd_dt_bias, m_ssd_a_log, m_ssd_d, m_ssd_out_norm, m_w_out, m_post_mix_norm, m_pre_ffn_norm, m_w_gate, m_w_up, m_w_down, m_post_ffn_norm, v_pre_mix_norm, v_w_in, v_lru_conv_w, v_lru_conv_b, v_lru_wa, v_lru_ba, v_lru_wx, v_lru_bx, v_lru_lambda, v_lru_out_norm, v_ssd_conv_w, v_ssd_conv_b, v_ssd_dt_bias, v_ssd_a_log, v_ssd_d, v_ssd_out_norm, v_w_out, v_post_mix_norm, v_pre_ffn_norm, v_w_gate, v_w_up, v_w_down, v_post_ffn_norm):
    args = dict(locals())
    names = list(SMALL) + list(BIG)
    w = {n: args[n] for n in names}
    m = {n: args["m_" + n] for n in names}
    v = {n: args["v_" + n] for n in names}
    chip = 2 * lax.axis_index("x") + lax.axis_index("y")

    win_f, wout_f, wg_f, wu_f, wd_f, lcw, scw = _gather_weights(w_in[0], w_out[0], w_gate[0], w_up[0], w_down[0],
                                                                lru_conv_w[0], ssd_conv_w[0])
    sp = {n: w[n] for n in SMALL}
    gx, big, small = _local_step(x[0], loss_target[0], win_f, wout_f, wg_f, wu_f, wd_f, lcw, scw, sp)
    red, (rows, cst, hst, dwa, dwx) = _reduce_grads(big, small)
    loss = jnp.sum(rows[LOSS_ROW])

    grads, delta, new_m, new_v = {}, {}, {}, {}
    for k, n in enumerate(BIG):
        g = red[k]
        d, nm, nv = _adamw("adamw_" + n, w[n][0], g, m[n][0], v[n][0])
        grads[n], delta[n], new_m[n], new_v[n] = g[None], d[None], nm[None], nv[None]

    lc, sc = lru_conv_w.shape[-1], ssd_conv_w.shape[-1]
    glcw = lax.dynamic_slice_in_dim(rows[LRU_CONV_ROWS[0]:LRU_CONV_ROWS[1]], chip * lc, lc, axis=1)
    gscw = lax.dynamic_slice_in_dim(cst[0:CONV_K], chip * sc, sc, axis=1)
    res = _adamw_small(rows, cst, hst, dwa.reshape(NBLK, BW, BW), dwx.reshape(NBLK, BW, BW), glcw, gscw,
                       {n: w[n] for n in SMALL}, {n: m[n] for n in SMALL}, {n: v[n] for n in SMALL})
    for n in SMALL:
        grads[n], delta[n], new_m[n], new_v[n] = res[n]

    order = ["pre_mix_norm", "w_in", "lru_conv_w", "lru_conv_b", "lru_wa", "lru_ba", "lru_wx", "lru_bx", "lru_lambda",
             "lru_out_norm", "ssd_conv_w", "ssd_conv_b", "ssd_dt_bias", "ssd_a_log", "ssd_d", "ssd_out_norm", "w_out",
             "post_mix_norm", "pre_ffn_norm", "w_gate", "w_up", "w_down", "post_ffn_norm"]
    return (loss, gx[None], *[grads[n] for n in order], *[delta[n] for n in order],
            *[new_m[n] for n in order], *[new_v[n] for n in order])
```

```python
import functools

import jax
import jax.numpy as jnp
from jax import lax
from jax.experimental import pallas as pl
from jax.experimental.pallas import tpu as pltpu

F32 = jnp.float32
BF = jnp.bfloat16

D = 1024
LW = 1024
NBLK = 16
BW = 64
SI = 1024
NH = 16
HD = 64
NG = 2
HPG = NH // NG
NS = 128
CH = 128
XBC = SI + 2 * NG * NS
DTP = 128
PC = 3 * 1024 + XBC + DTP
DFF = 2816
IN_COLS = 4624
EPS = 1e-6
LRU_C = 8.0
CONV_K = 4
TT = 256
VMEM_LIMIT = 56 * 1024 * 1024

ADAM_LR, ADAM_B1, ADAM_B2, ADAM_EPS, ADAM_WD, ADAM_STEP = 0.001, 0.9, 0.999, 1e-08, 0.01, 10

MESH = pl.DeviceIdType.MESH


def _mm(a, b):
    return jnp.dot(a.astype(BF), b.astype(BF), preferred_element_type=F32)


def _mm_nt(a, b):
    return lax.dot_general(a.astype(BF), b.astype(BF), (((1,), (1,)), ((), ())), preferred_element_type=F32)


def _mm_tn(a, b):
    return lax.dot_general(a.astype(BF), b.astype(BF), (((0,), (0,)), ((), ())), preferred_element_type=F32)


def _sigmoid(x):
    return 0.5 * jnp.tanh(0.5 * x) + 0.5


def _softplus(x):
    return jnp.maximum(x, 0.0) + jnp.log1p(jnp.exp(-jnp.abs(x)))


_GELU_C = 0.7978845608028654
_GELU_K = 0.044715


def _gelu(x):
    t = jnp.tanh(_GELU_C * (x + _GELU_K * x * x * x))
    return 0.5 * x * (1.0 + t)


def _gelu_grad(x):
    t = jnp.tanh(_GELU_C * (x + _GELU_K * x * x * x))
    return 0.5 * (1.0 + t) + 0.5 * x * (1.0 - t * t) * _GELU_C * (1.0 + 3.0 * _GELU_K * x * x)


def _rms_fwd(x, g):
    r = lax.rsqrt(jnp.mean(x * x, axis=-1, keepdims=True) + EPS)
    return x * r * g


def _rms_bwd(x, g, dy):
    r = lax.rsqrt(jnp.mean(x * x, axis=-1, keepdims=True) + EPS)
    xh = x * r
    dxh = dy * g
    dg = jnp.sum(dy * xh, axis=0, keepdims=True)
    dx = r * (dxh - xh * jnp.mean(dxh * xh, axis=-1, keepdims=True))
    return dx, dg


def _sum_all(x):
    return jnp.sum(jnp.sum(x, axis=1, keepdims=True), axis=0, keepdims=True)


def _cumsum_rows(x, n):
    row = lax.broadcasted_iota(jnp.int32, x.shape, 0)
    k = 1
    while k < n:
        x = x + jnp.where(row >= k, pltpu.roll(x, k, 0), 0.0)
        k *= 2
    return x


def _rev_cumsum_rows(x, n):
    row = lax.broadcasted_iota(jnp.int32, x.shape, 0)
    k = 1
    while k < n:
        x = x + jnp.where(row < n - k, pltpu.roll(x, n - k, 0), 0.0)
        k *= 2
    return x


def _load_once(pairs, sem):
    @pl.when(pl.program_id(0) == 0)
    def _():
        for k, (src, dst) in enumerate(pairs):
            pltpu.make_async_copy(src, dst, sem.at[k]).start()
        for k, (src, dst) in enumerate(pairs):
            pltpu.make_async_copy(src, dst, sem.at[k]).wait()


def _params(n_axes=1):
    return pltpu.CompilerParams(dimension_semantics=("arbitrary",) * n_axes, vmem_limit_bytes=VMEM_LIMIT)


def _rows(n, width, rev_of=None):
    if rev_of is None:
        return pl.BlockSpec((n, width), lambda i: (i, 0))
    return pl.BlockSpec((n, width), lambda i: (rev_of - 1 - i, 0))


def _whole(shape):
    nd = len(shape)
    return pl.BlockSpec(shape, lambda i: (0,) * nd)


ANY = pl.BlockSpec(memory_space=pl.ANY)
S = jax.ShapeDtypeStruct
WIRE = jnp.bfloat16


def _pos():
    return lax.axis_index("x"), lax.axis_index("y"), lax.axis_index("c")


def _other_chips(x, y):
    return [(1 - x, y), (x, 1 - y), (1 - x, 1 - y)]


def _remote(src, dst, send_sem, recv_sem, to):
    return pltpu.make_async_remote_copy(src_ref=src, dst_ref=dst, send_sem=send_sem, recv_sem=recv_sem,
                                        device_id=to, device_id_type=MESH)


def _gather_phase(phase, ins, outs, send_sems, recv_sems):
    x, y, c = _pos()
    me = 2 * x + y
    chips = _other_chips(x, y)
    for i, (src, dst) in enumerate(zip(ins, outs)):
        hr = src.shape[0] // 2
        my_half = pl.ds(pl.multiple_of(c * hr, 16), hr)
        sib_half = pl.ds(pl.multiple_of((1 - c) * hr, 16), hr)
        for k, (cx, cy) in enumerate(chips):
            s1, r1 = send_sems.at[6 * i + k], recv_sems.at[6 * i + k]
            s2, r2 = send_sems.at[6 * i + 3 + k], recv_sems.at[6 * i + 3 + k]
            first = lambda: _remote(src.at[my_half, :], dst.at[me, my_half, :], s1, r1, (cx, cy, c))
            landed = dst.at[2 * cx + cy, my_half, :]
            passed = lambda: _remote(landed, landed, s2, r2, (x, y, 1 - c))
            if phase == 0:
                first().start()
            elif phase == 1:
                _remote(landed, landed, s1, r1, (cx, cy, c)).wait_recv()
                passed().start()
            else:
                theirs = dst.at[2 * cx + cy, sib_half, :]
                _remote(theirs, theirs, s2, r2, (x, y, 1 - c)).wait_recv()
                first().wait_send()
                passed().wait_send()


def _quad_phase(phase, ins, outs, send_sems, recv_sems):
    x, y, c = _pos()
    me = 2 * x + y
    for i, (src, dst) in enumerate(zip(ins, outs)):
        for k, (cx, cy) in enumerate(_other_chips(x, y)):
            cp = _remote(src.at[2 * cx + cy], dst.at[me], send_sems.at[3 * i + k], recv_sems.at[3 * i + k], (cx, cy, c))
            if phase == 0:
                cp.start()
            else:
                got = dst.at[2 * cx + cy]
                _remote(got, got, send_sems.at[3 * i + k], recv_sems.at[3 * i + k], (cx, cy, c)).wait_recv()
                cp.wait_send()


def _inproj(x, g0, wcat):
    T = x.shape[0]

    def body(x_ref, g_ref, w_hbm, h0_ref, lx_ref, lg_ref, z_ref, xbc_ref, dt_ref, w_vm, sem):
        _load_once([(w_hbm, w_vm)], sem)
        h = _rms_fwd(x_ref[...], g_ref[...]).astype(BF)
        h0_ref[...] = h
        lx_ref[...] = jnp.dot(h, w_vm[:, 0:1024], preferred_element_type=F32)
        lg_ref[...] = jnp.dot(h, w_vm[:, 1024:2048], preferred_element_type=F32)
        z_ref[...] = jnp.dot(h, w_vm[:, 2048:3072], preferred_element_type=F32)
        xbc_ref[...] = jnp.dot(h, w_vm[:, 3072:3072 + XBC], preferred_element_type=F32)
        dt_ref[...] = jnp.dot(h, w_vm[:, 3072 + XBC:PC], preferred_element_type=F32)

    return pl.pallas_call(
        body, name="inproj", grid=(T // TT,),
        in_specs=[_rows(TT, D), _whole((1, D)), ANY],
        out_specs=[_rows(TT, D), _rows(TT, 1024), _rows(TT, 1024), _rows(TT, 1024), _rows(TT, XBC), _rows(TT, DTP)],
        out_shape=[S((T, D), BF), S((T, 1024), F32), S((T, 1024), F32), S((T, 1024), F32), S((T, XBC), F32), S((T, DTP), F32)],
        scratch_shapes=[pltpu.VMEM((D, PC), BF), pltpu.SemaphoreType.DMA((1,))],
        compiler_params=_params(),
    )(x, g0, wcat)


def _blockdiag_mm(v, w4_ref):
    return jnp.concatenate([_mm(v[:, 256 * j:256 * (j + 1)], w4_ref[j]) for j in range(4)], axis=1)


def _lru_gates(lx, p_ref, wa_ref, wx_ref):
    r = _sigmoid(_blockdiag_mm(lx, wa_ref) + p_ref[5:6, :])
    i = _sigmoid(_blockdiag_mm(lx, wx_ref) + p_ref[6:7, :])
    sp = _softplus(-p_ref[7:8, :])
    la = -LRU_C * r * sp
    a = jnp.exp(la)
    th = jnp.tanh(la)
    mult = jnp.sqrt(-2.0 * th / (1.0 - th))
    return r, i, sp, a, mult


def _conv_from(xp_ref, p_ref, n):
    taps = [xp_ref[pl.ds(8 - CONV_K + 1 + k, n), :] for k in range(CONV_K)]
    acc = p_ref[4:5, :] + p_ref[0:1, :] * taps[0]
    for k in range(1, CONV_K):
        acc = acc + p_ref[k:k + 1, :] * taps[k]
    return acc, taps


def _lru_fwd(lxr, lg, p_lru, wa4, wx4):
    T = lxr.shape[0]

    def body(lx_ref, lg_ref, p_ref, wa_ref, wx_ref, h_ref, y_ref, xp, a_s, u_s, hc):
        @pl.when(pl.program_id(0) == 0)
        def _():
            xp[0:8, :] = jnp.zeros((8, LW), F32)
            hc[...] = jnp.zeros_like(hc)

        xp[8:8 + TT, :] = lx_ref[...]
        lx, _ = _conv_from(xp, p_ref, TT)
        xp[0:8, :] = xp[TT:TT + 8, :]
        r, i, sp, a, mult = _lru_gates(lx, p_ref, wa_ref, wx_ref)
        a_s[...] = a
        u_s[...] = mult * (i * lx)

        def step(t, h):
            h = a_s[pl.ds(t, 1), :] * h + u_s[pl.ds(t, 1), :]
            h_ref[pl.ds(t, 1), :] = h
            return h

        hc[0:1, :] = lax.fori_loop(0, TT, step, hc[0:1, :], unroll=8)
        gated = h_ref[...] * _gelu(lg_ref[...])
        y_ref[...] = _rms_fwd(gated, p_ref[8:9, :]).astype(BF)

    return pl.pallas_call(
        body, name="lru_fwd", grid=(T // TT,),
        in_specs=[_rows(TT, LW), _rows(TT, LW), _whole((16, LW)), _whole((4, 256, 256)), _whole((4, 256, 256))],
        out_specs=[_rows(TT, LW), _rows(TT, LW)],
        out_shape=[S((T, LW), F32), S((T, LW), BF)],
        scratch_shapes=[pltpu.VMEM((TT + 8, LW), F32), pltpu.VMEM((TT, LW), F32), pltpu.VMEM((TT, LW), F32),
                        pltpu.VMEM((8, LW), F32)],
        compiler_params=_params(),
    )(lxr, lg, p_lru, wa4, wx4)


def _ssd_prep(xp, xr_ref, dt_ref, cw_ref, hp_ref):
    xp[8:8 + CH, :] = xr_ref[...]
    cv, taps = _conv_from(xp, cw_ref, CH)
    sg = _sigmoid(cv)
    xbc = cv * sg
    lane = lax.broadcasted_iota(jnp.int32, (CH, DTP), 1)
    raw = dt_ref[...] + hp_ref[0:1, :]
    dtv = jnp.where(lane < NH, _softplus(raw), 0.0)
    A = jnp.where(lane[0:1, :] < NH, -jnp.exp(hp_ref[1:2, :]), 0.0)
    cs = _cumsum_rows(dtv * A, CH)
    return cv, sg, xbc, raw, dtv, A, cs, taps


def _per_head_lanes(v):
    r = v.shape[0]
    first = lax.broadcasted_iota(jnp.int32, (r, 2 * HD), 1) < HD
    pairs = [jnp.where(first, jnp.broadcast_to(v[:, 2 * j:2 * j + 1], (r, 2 * HD)),
                       jnp.broadcast_to(v[:, 2 * j + 1:2 * j + 2], (r, 2 * HD))) for j in range(NH // 2)]
    return jnp.concatenate(pairs, axis=1)


def _per_head_rows(col, g):
    return jnp.concatenate([jnp.broadcast_to(col[g * HPG + k:g * HPG + k + 1, :], (HD, NS)) for k in range(HPG)], axis=0)


def _ssd_decays(cs):
    csT = cs.T
    cl = cs[CH - 1:CH, :]
    E_x = _per_head_lanes(jnp.exp(cs))
    dsm = jnp.exp(cl - cs)
    ds_x = _per_head_lanes(dsm)
    El_rows = jnp.broadcast_to(jnp.exp(csT[0:NH, CH - 1:CH]), (NH, NS))
    return csT, dsm, E_x, ds_x, El_rows


def _ssd_fwd(xbcr, z, dtr, cw_ssd, hp_ssd, g_ssd, shards):
    T = xbcr.shape[0]
    NC = T // CH
    ng = len(shards)

    def body(*refs):
        xr_ref, z_ref, dt_ref, cw_ref, hp_ref, g_ref = refs[:6]
        sh_in = refs[6:6 + ng]
        y_ref, yn_ref, st_ref = refs[6 + ng:9 + ng]
        sh_out = refs[9 + ng:9 + 2 * ng]
        xp, st, send_sems, recv_sems = refs[9 + 2 * ng:]
        for phase, step in enumerate((0, NC // 2, NC - 1)):
            @pl.when(pl.program_id(0) == step)
            def _():
                _gather_phase(phase, sh_in, sh_out, send_sems, recv_sems)

        @pl.when(pl.program_id(0) == 0)
        def _():
            xp[0:8, :] = jnp.zeros((8, XBC), F32)
            st[...] = jnp.zeros_like(st)

        cv, sg, xbc, raw, dtv, A, cs, _ = _ssd_prep(xp, xr_ref, dt_ref, cw_ref, hp_ref)
        xp[0:8, :] = xp[CH:CH + 8, :]
        st_ref[0] = st[...]
        csT, dsm, E_x, ds_x, El_rows = _ssd_decays(cs)
        X = xbc[:, 0:SI]
        xs = X * _per_head_lanes(dtv)
        xsd = (xs * ds_x).astype(BF)
        DX = _per_head_lanes(hp_ref[...])[2:3, :] * X
        tril = lax.broadcasted_iota(jnp.int32, (CH, CH), 0) >= lax.broadcasted_iota(jnp.int32, (CH, CH), 1)
        first = lax.broadcasted_iota(jnp.int32, (CH, 2 * HD), 1) < HD
        GW = HPG * HD
        for g in range(NG):
            Bg = xbc[:, SI + NS * g:SI + NS * (g + 1)].astype(BF)
            Cg = xbc[:, SI + NG * NS + NS * g:SI + NG * NS + NS * (g + 1)].astype(BF)
            G = _mm_nt(Cg, Bg)
            Sg = st[GW * g:GW * (g + 1), :]
            Yo = _mm_nt(Cg, Sg) * E_x[:, GW * g:GW * (g + 1)]
            st[GW * g:GW * (g + 1), :] = _per_head_rows(El_rows, g) * Sg + _mm_tn(xsd[:, GW * g:GW * (g + 1)], Bg)
            for jj in range(HPG // 2):
                j = g * (HPG // 2) + jj
                ps = slice(2 * HD * j, 2 * HD * (j + 1))
                xs_pair = xs[:, ps]
                acc = Yo[:, 2 * HD * jj:2 * HD * (jj + 1)] + DX[:, ps]
                for e in range(2):
                    h = 2 * j + e
                    Lm = jnp.exp(jnp.where(tril, cs[:, h:h + 1] - csT[h:h + 1, :], -1e30))
                    acc = acc + _mm(G * Lm, jnp.where(first if e == 0 else ~first, xs_pair, 0.0))
                y_ref[:, ps] = acc
        zz = z_ref[...]
        gated = y_ref[...] * (zz * _sigmoid(zz))
        yn_ref[...] = _rms_fwd(gated, g_ref[...]).astype(BF)

    return pl.pallas_call(
        body, name="ssd_fwd", grid=(NC,),
        in_specs=[_rows(CH, XBC), _rows(CH, SI), _rows(CH, DTP), _whole((8, XBC)), _whole((8, DTP)), _whole((1, SI))]
        + [ANY] * ng,
        out_specs=[_rows(CH, SI), _rows(CH, SI), pl.BlockSpec((1, NH * HD, NS), lambda i: (i, 0, 0))] + [ANY] * ng,
        out_shape=[S((T, SI), F32), S((T, SI), BF), S((NC, NH * HD, NS), F32)] + [S((4,) + s.shape, s.dtype) for s in shards],
        scratch_shapes=[pltpu.VMEM((CH + 8, XBC), F32), pltpu.VMEM((NH * HD, NS), F32),
                        pltpu.SemaphoreType.DMA((6 * ng,)), pltpu.SemaphoreType.DMA((6 * ng,))],
        compiler_params=_params(),
    )(xbcr, z, dtr, cw_ssd, hp_ssd, g_ssd, *shards)


def _outproj(ylru, yssd, x, wout, g_pm, g_pf):
    T = x.shape[0]

    def body(yl_ref, ys_ref, x_ref, w_hbm, gpm_ref, gpf_ref, mix_ref, x1_ref, h2_ref, w_vm, sem):
        _load_once([(w_hbm, w_vm)], sem)
        mix = (jnp.dot(yl_ref[...], w_vm[0:LW, :], preferred_element_type=F32)
               + jnp.dot(ys_ref[...], w_vm[LW:LW + SI, :], preferred_element_type=F32))
        mix_ref[...] = mix
        x1 = x_ref[...] + _rms_fwd(mix, gpm_ref[...])
        x1_ref[...] = x1
        h2_ref[...] = _rms_fwd(x1, gpf_ref[...]).astype(BF)

    return pl.pallas_call(
        body, name="outproj", grid=(T // TT,),
        in_specs=[_rows(TT, LW), _rows(TT, SI), _rows(TT, D), ANY, _whole((1, D)), _whole((1, D))],
        out_specs=[_rows(TT, D), _rows(TT, D), _rows(TT, D)],
        out_shape=[S((T, D), F32), S((T, D), F32), S((T, D), BF)],
        scratch_shapes=[pltpu.VMEM((LW + SI, D), BF), pltpu.SemaphoreType.DMA((1,))],
        compiler_params=_params(),
    )(ylru, yssd, x, wout, g_pm, g_pf)


def _ffn_fwd(h2, x1, tgt, wg, wu, wd, g_pff):
    T = x1.shape[0]

    def body(h2_ref, x1_ref, t_ref, wg_hbm, wu_hbm, wd_hbm, g_ref,
             gate_ref, up_ref, act_ref, df_ref, dx2_ref, st_ref, wg_vm, wu_vm, wd_vm, sem):
        _load_once([(wg_hbm, wg_vm), (wu_hbm, wu_vm), (wd_hbm, wd_vm)], sem)

        @pl.when(pl.program_id(0) == 0)
        def _():
            st_ref[...] = jnp.zeros_like(st_ref)

        h2 = h2_ref[...]
        gate = jnp.dot(h2, wg_vm[...], preferred_element_type=F32)
        up = jnp.dot(h2, wu_vm[...], preferred_element_type=F32)
        gate_ref[...] = gate
        up_ref[...] = up
        act = (gate * _sigmoid(gate) * up).astype(BF)
        act_ref[...] = act
        f = jnp.dot(act, wd_vm[...], preferred_element_type=F32)
        g = g_ref[...]
        x2 = x1_ref[...] + _rms_fwd(f, g)
        err = x2 - t_ref[...]
        st_ref[0:1, :] += 0.5 * jnp.sum(err * err, axis=0, keepdims=True) * (1.0 / D)
        dx2 = err * (1.0 / D)
        dx2_ref[...] = dx2
        df, dg = _rms_bwd(f, g, dx2)
        df_ref[...] = df.astype(BF)
        st_ref[1:2, :] += dg

    return pl.pallas_call(
        body, name="ffn_fwd", grid=(T // TT,),
        in_specs=[_rows(TT, D), _rows(TT, D), _rows(TT, D), ANY, ANY, ANY, _whole((1, D))],
        out_specs=[_rows(TT, DFF), _rows(TT, DFF), _rows(TT, DFF), _rows(TT, D), _rows(TT, D), _whole((8, D))],
        out_shape=[S((T, DFF), F32), S((T, DFF), F32), S((T, DFF), BF), S((T, D), BF), S((T, D), F32), S((8, D), F32)],
        scratch_shapes=[pltpu.VMEM((D, DFF), BF), pltpu.VMEM((D, DFF), BF), pltpu.VMEM((DFF, D), BF),
                        pltpu.SemaphoreType.DMA((3,))],
        compiler_params=_params(),
    )(h2, x1, tgt, wg, wu, wd, g_pff)


def _ffn_bwd(df, gate, up, wdT, wgT, wuT):
    T = df.shape[0]

    def body(df_ref, gate_ref, up_ref, wd_hbm, wg_hbm, wu_hbm, dgate_ref, dup_ref, dh2_ref, wd_vm, wg_vm, wu_vm, sem):
        _load_once([(wd_hbm, wd_vm), (wg_hbm, wg_vm), (wu_hbm, wu_vm)], sem)
        dact = jnp.dot(df_ref[...], wd_vm[...], preferred_element_type=F32)
        gate = gate_ref[...]
        s = _sigmoid(gate)
        dup = (dact * (gate * s)).astype(BF)
        dgate = (dact * up_ref[...] * (s + gate * s * (1.0 - s))).astype(BF)
        dup_ref[...] = dup
        dgate_ref[...] = dgate
        dh2_ref[...] = (jnp.dot(dgate, wg_vm[...], preferred_element_type=F32)
                        + jnp.dot(dup, wu_vm[...], preferred_element_type=F32))

    return pl.pallas_call(
        body, name="ffn_bwd", grid=(T // TT,),
        in_specs=[_rows(TT, D), _rows(TT, DFF), _rows(TT, DFF), ANY, ANY, ANY],
        out_specs=[_rows(TT, DFF), _rows(TT, DFF), _rows(TT, D)],
        out_shape=[S((T, DFF), BF), S((T, DFF), BF), S((T, D), F32)],
        scratch_shapes=[pltpu.VMEM((D, DFF), BF), pltpu.VMEM((DFF, D), BF), pltpu.VMEM((DFF, D), BF),
                        pltpu.SemaphoreType.DMA((3,))],
        compiler_params=_params(),
    )(df, gate, up, wdT, wgT, wuT)


def _mix_bwd(dh2, x1, dx2, mix, woutT, g_pf, g_pm):
    T = x1.shape[0]

    def body(dh2_ref, x1_ref, dx2_ref, mix_ref, w_hbm, gpf_ref, gpm_ref,
             dx1_ref, dmix_ref, dyl_ref, dys_ref, st_ref, w_vm, sem):
        _load_once([(w_hbm, w_vm)], sem)

        @pl.when(pl.program_id(0) == 0)
        def _():
            st_ref[...] = jnp.zeros_like(st_ref)

        dxa, dgpf = _rms_bwd(x1_ref[...], gpf_ref[...], dh2_ref[...])
        dx1 = dx2_ref[...] + dxa
        dx1_ref[...] = dx1
        dmix, dgpm = _rms_bwd(mix_ref[...], gpm_ref[...], dx1)
        dmix = dmix.astype(BF)
        dmix_ref[...] = dmix
        st_ref[0:1, :] += dgpf
        st_ref[1:2, :] += dgpm
        dyl_ref[...] = jnp.dot(dmix, w_vm[:, 0:LW], preferred_element_type=F32)
        dys_ref[...] = jnp.dot(dmix, w_vm[:, LW:LW + SI], preferred_element_type=F32)

    return pl.pallas_call(
        body, name="mix_bwd", grid=(T // TT,),
        in_specs=[_rows(TT, D), _rows(TT, D), _rows(TT, D), _rows(TT, D), ANY, _whole((1, D)), _whole((1, D))],
        out_specs=[_rows(TT, D), _rows(TT, D), _rows(TT, LW), _rows(TT, SI), _whole((8, D))],
        out_shape=[S((T, D), F32), S((T, D), BF), S((T, LW), F32), S((T, SI), F32), S((8, D), F32)],
        scratch_shapes=[pltpu.VMEM((D, LW + SI), BF), pltpu.SemaphoreType.DMA((1,))],
        compiler_params=_params(),
    )(dh2, x1, dx2, mix, woutT, g_pf, g_pm)


def _halo(width, n_tiles, tile):
    per = tile // 8
    return pl.BlockSpec((8, width), lambda i: (jnp.maximum((n_tiles - 1 - i) * per - 1, 0), 0))


def _lru_bwd(dy, lxr, lg, h, p_lru, wa4, wx4, wa4T, wx4T):
    T = dy.shape[0]
    NT = T // TT

    def body(dy_ref, lx_ref, lxh_ref, lg_ref, h_ref, hh_ref, p_ref, wa_ref, wx_ref, waT_ref, wxT_ref,
             dlx_ref, dlg_ref, st_ref, dwa_ref, dwx_ref, xp, hp, dp, a_s, d_s, g_s, cc):
        first = pl.program_id(0) == 0
        top = pl.program_id(0) == NT - 1

        @pl.when(first)
        def _():
            st_ref[...] = jnp.zeros_like(st_ref)
            dwa_ref[...] = jnp.zeros_like(dwa_ref)
            dwx_ref[...] = jnp.zeros_like(dwx_ref)
            dp[TT:TT + 8, :] = jnp.zeros((8, LW), F32)
            cc[...] = jnp.zeros_like(cc)

        keep = jnp.where(top, 0.0, 1.0)
        xp[0:8, :] = lxh_ref[...] * keep
        xp[8:8 + TT, :] = lx_ref[...]
        hp[0:8, :] = hh_ref[...] * keep
        hp[8:8 + TT, :] = h_ref[...]
        lx, taps = _conv_from(xp, p_ref, TT)
        r, i, sp, a, mult = _lru_gates(lx, p_ref, wa_ref, wx_ref)

        lg = lg_ref[...]
        hcur = h_ref[...]
        ge = _gelu(lg)
        dgated, dgn = _rms_bwd(hcur * ge, p_ref[8:9, :], dy_ref[...])
        st_ref[8:9, :] += dgn
        dlg_ref[...] = (dgated * hcur * _gelu_grad(lg)).astype(BF)
        a_s[...] = a
        d_s[...] = dgated * ge

        def step(k, c):
            t = TT - 1 - k
            g = d_s[pl.ds(t, 1), :] + c
            g_s[pl.ds(t, 1), :] = g
            return a_s[pl.ds(t, 1), :] * g

        cc[0:1, :] = lax.fori_loop(0, TT, step, cc[0:1, :], unroll=8)
        gt = g_s[...]
        da = gt * hp[pl.ds(7, TT), :]
        dmult = gt * i * lx
        di = gt * mult * lx
        dlxc = gt * mult * i
        dla = da * a - dmult * (a * a) / mult
        dr = dla * (-LRU_C * sp)
        st_ref[7:8, :] += jnp.sum(dla * (-LRU_C * r), axis=0, keepdims=True) * (-_sigmoid(-p_ref[7:8, :]))
        dzr = dr * r * (1.0 - r)
        dzi = di * i * (1.0 - i)
        st_ref[5:6, :] += jnp.sum(dzr, axis=0, keepdims=True)
        st_ref[6:7, :] += jnp.sum(dzi, axis=0, keepdims=True)
        dlxc = dlxc + _blockdiag_mm(dzr, waT_ref) + _blockdiag_mm(dzi, wxT_ref)
        for j in range(4):
            sl = slice(256 * j, 256 * (j + 1))
            pa = _mm_tn(lx[:, sl], dzr[:, sl])
            px = _mm_tn(lx[:, sl], dzi[:, sl])
            for b in range(4):
                bs = slice(BW * b, BW * (b + 1))
                dwa_ref[4 * j + b] += pa[bs, bs]
                dwx_ref[4 * j + b] += px[bs, bs]
        dp[0:TT, :] = dlxc
        acc = p_ref[0:1, :] * dp[pl.ds(CONV_K - 1, TT), :]
        for k in range(1, CONV_K):
            acc = acc + p_ref[k:k + 1, :] * dp[pl.ds(CONV_K - 1 - k, TT), :]
        dlx_ref[...] = acc.astype(BF)
        dp[TT:TT + 8, :] = dp[0:8, :]
        for k in range(CONV_K):
            st_ref[k:k + 1, :] += jnp.sum(dlxc * taps[k], axis=0, keepdims=True)
        st_ref[4:5, :] += jnp.sum(dlxc, axis=0, keepdims=True)

    w4 = _whole((4, 256, 256))
    return pl.pallas_call(
        body, name="lru_bwd", grid=(NT,),
        in_specs=[_rows(TT, LW, NT), _rows(TT, LW, NT), _halo(LW, NT, TT), _rows(TT, LW, NT), _rows(TT, LW, NT),
                  _halo(LW, NT, TT), _whole((16, LW)), w4, w4, w4, w4],
        out_specs=[_rows(TT, LW, NT), _rows(TT, LW, NT), _whole((16, LW)), _whole((NBLK, BW, BW)), _whole((NBLK, BW, BW))],
        out_shape=[S((T, LW), BF), S((T, LW), BF), S((16, LW), F32), S((NBLK, BW, BW), F32), S((NBLK, BW, BW), F32)],
        scratch_shapes=[pltpu.VMEM((TT + 8, LW), F32), pltpu.VMEM((TT + 8, LW), F32), pltpu.VMEM((TT + 8, LW), F32),
                        pltpu.VMEM((TT, LW), F32), pltpu.VMEM((TT, LW), F32), pltpu.VMEM((TT, LW), F32),
                        pltpu.VMEM((8, LW), F32)],
        compiler_params=_params(),
    )(dy, lxr, lxr, lg, h, h, p_lru, wa4, wx4, wa4T, wx4T)


def _ssd_bwd(dyn, xbcr, z, dtr, y, states, cw_ssd, hp_ssd, g_ssd, parts):
    T = dyn.shape[0]
    NC = T // CH
    nq = len(parts)

    def body(*refs):
        dyn_ref, xr_ref, xh_ref, z_ref, dt_ref, y_ref, st_ref, cw_ref, hp_ref, g_ref = refs[:10]
        q_in = refs[10:10 + nq]
        dxbc_ref, dz_ref, ddt_ref, cst_ref, hst_ref, gst_ref = refs[10 + nq:16 + nq]
        q_out = refs[16 + nq:16 + 2 * nq]
        xp, dp, dS, dxb, yo_s, q_s, dxs_s, t1_s, send_sems, recv_sems = refs[16 + 2 * nq:]
        first = pl.program_id(0) == 0
        top = pl.program_id(0) == NC - 1
        for phase, step in enumerate((0, NC - 1)):
            @pl.when(pl.program_id(0) == step)
            def _():
                _quad_phase(phase, q_in, q_out, send_sems, recv_sems)

        @pl.when(first)
        def _():
            cst_ref[...] = jnp.zeros_like(cst_ref)
            hst_ref[...] = jnp.zeros_like(hst_ref)
            gst_ref[...] = jnp.zeros_like(gst_ref)
            dp[CH:CH + 8, :] = jnp.zeros((8, XBC), F32)
            dS[...] = jnp.zeros_like(dS)

        xp[0:8, :] = xh_ref[...] * jnp.where(top, 0.0, 1.0)
        cv, sg, xbc, raw, dtv, A, cs, taps = _ssd_prep(xp, xr_ref, dt_ref, cw_ref, hp_ref)
        csT, dsm, E_x, ds_x, El_rows = _ssd_decays(cs)
        row_i = lax.broadcasted_iota(jnp.int32, (CH, CH), 0)
        col_i = lax.broadcasted_iota(jnp.int32, (CH, CH), 1)
        tril = row_i >= col_i
        first = col_i < HD
        head_of = ((lax.broadcasted_iota(jnp.int32, (DTP, SI), 1) >> 6)
                   == lax.broadcasted_iota(jnp.int32, (DTP, SI), 0)).astype(BF)
        head_ofT = ((lax.broadcasted_iota(jnp.int32, (SI, DTP), 0) >> 6)
                    == lax.broadcasted_iota(jnp.int32, (SI, DTP), 1)).astype(BF)

        def hi_lo(v):
            hi = v.astype(BF)
            return hi, (v - hi.astype(F32)).astype(BF)

        def lane_sums(v):
            hi, lo = hi_lo(v)
            return _mm(hi, head_ofT) + _mm(lo, head_ofT)

        zz = z_ref[...]
        sz = _sigmoid(zz)
        yv = y_ref[...]
        dgn, dg = _rms_bwd(yv * (zz * sz), g_ref[...], dyn_ref[...])
        gst_ref[0:1, :] += dg
        dz_ref[...] = (dgn * yv * (sz + zz * sz * (1.0 - sz))).astype(BF)
        dY = dgn * (zz * sz)

        X = xbc[:, 0:SI]
        dt_x = _per_head_lanes(dtv)
        xs = X * dt_x
        xsd = (xs * ds_x).astype(BF)
        D_x = _per_head_lanes(hp_ref[...])[2:3, :]
        dcs_col = jnp.zeros((CH, DTP), F32)
        dcs_row = jnp.zeros((CH, DTP), F32)
        GW = HPG * HD
        for g in range(NG):
            gs = slice(GW * g, GW * (g + 1))
            Bg = xbc[:, SI + NS * g:SI + NS * (g + 1)].astype(BF)
            Cg = xbc[:, SI + NG * NS + NS * g:SI + NG * NS + NS * (g + 1)].astype(BF)
            G = _mm_nt(Cg, Bg)
            Sg = st_ref[0, gs, :]
            dSe = dS[gs, :]
            dYg = dY[:, gs]
            yo_s[:, gs] = _mm_nt(Cg, Sg) * E_x[:, gs]
            dP = dYg * E_x[:, gs]
            dCg = _mm(dP, Sg)
            dS[gs, :] = _mm_tn(dP, Cg) + _per_head_rows(El_rows, g) * dSe
            t1_s[gs, :] = dSe * Sg
            Q = _mm_nt(Bg, dSe)
            q_s[:, gs] = Q
            dBg = _mm(xsd[:, gs], dSe)
            dG = jnp.zeros((CH, CH), F32)
            for jj in range(HPG // 2):
                j = g * (HPG // 2) + jj
                ps = slice(2 * HD * j, 2 * HD * (j + 1))
                xs_pair = xs[:, ps]
                dxs_pair = Q[:, 2 * HD * jj:2 * HD * (jj + 1)] * ds_x[:, ps]
                for e in range(2):
                    h = 2 * j + e
                    Lm = jnp.exp(jnp.where(tril, cs[:, h:h + 1] - csT[h:h + 1, :], -1e30))
                    M = G * Lm
                    dYm = jnp.where(first if e == 0 else ~first, dY[:, ps], 0.0).astype(BF)
                    dM = _mm_nt(dYm, xs_pair)
                    dxs_pair = dxs_pair + _mm_tn(M, dYm)
                    Wm = dM * M
                    dcs_col = dcs_col + jnp.where(col_i == h, jnp.sum(Wm, axis=1, keepdims=True), 0.0)
                    dcs_row = dcs_row + jnp.where(row_i == h, -jnp.sum(Wm, axis=0, keepdims=True), 0.0)
                    dG = dG + dM * Lm
                dxs_s[:, ps] = dxs_pair
            dxb[:, SI + NS * g:SI + NS * (g + 1)] = dBg + _mm_tn(dG, Cg)
            dxb[:, SI + NG * NS + NS * g:SI + NG * NS + NS * (g + 1)] = dCg + _mm(dG, Bg)

        dxs = dxs_s[...]
        dxb[:, 0:SI] = D_x * dY + dxs * dt_x
        dds = lane_sums(q_s[...] * xs) * dsm
        dcs_col = dcs_col + lane_sums(dY * yo_s[...]) - dds
        ddt_col = lane_sums(dxs * X)
        dD = jnp.sum(lane_sums(dY * X), axis=0, keepdims=True)
        t_hi, t_lo = hi_lo(t1_s[...])
        dcl_rows = jnp.sum(_mm(head_of, t_hi) + _mm(head_of, t_lo), axis=1, keepdims=True) * jnp.exp(csT[:, CH - 1:CH])
        dcs_row = dcs_row + jnp.where(col_i == CH - 1, dcl_rows, 0.0)
        dcs_col = dcs_col + jnp.where(row_i == CH - 1, jnp.sum(dds, axis=0, keepdims=True), 0.0)

        da = _rev_cumsum_rows(dcs_col + dcs_row.T, CH)
        ddt_col = ddt_col + da * A
        hst_ref[1:2, :] += jnp.sum(da * dtv, axis=0, keepdims=True) * A
        hst_ref[2:3, :] += dD
        draw = jnp.where(col_i < NH, ddt_col * _sigmoid(raw), 0.0)
        ddt_ref[...] = draw.astype(BF)
        hst_ref[0:1, :] += jnp.sum(draw, axis=0, keepdims=True)

        dcv = dxb[...] * (sg + cv * sg * (1.0 - sg))
        dp[0:CH, :] = dcv
        acc = cw_ref[0:1, :] * dp[pl.ds(CONV_K - 1, CH), :]
        for k in range(1, CONV_K):
            acc = acc + cw_ref[k:k + 1, :] * dp[pl.ds(CONV_K - 1 - k, CH), :]
        dxbc_ref[...] = acc.astype(BF)
        dp[CH:CH + 8, :] = dp[0:8, :]
        for k in range(CONV_K):
            cst_ref[k:k + 1, :] += jnp.sum(dcv * taps[k], axis=0, keepdims=True)
        cst_ref[4:5, :] += jnp.sum(dcv, axis=0, keepdims=True)

    return pl.pallas_call(
        body, name="ssd_bwd", grid=(NC,),
        in_specs=[_rows(CH, SI, NC), _rows(CH, XBC, NC), _halo(XBC, NC, CH), _rows(CH, SI, NC), _rows(CH, DTP, NC),
                  _rows(CH, SI, NC), pl.BlockSpec((1, NH * HD, NS), lambda i: (NC - 1 - i, 0, 0)),
                  _whole((8, XBC)), _whole((8, DTP)), _whole((1, SI))] + [ANY] * nq,
        out_specs=[_rows(CH, XBC, NC), _rows(CH, SI, NC), _rows(CH, DTP, NC), _whole((16, XBC)), _whole((16, DTP)),
                   _whole((8, SI))] + [ANY] * nq,
        out_shape=[S((T, XBC), BF), S((T, SI), BF), S((T, DTP), BF), S((16, XBC), F32), S((16, DTP), F32), S((8, SI), F32)]
        + [S(p.shape, p.dtype) for p in parts],
        scratch_shapes=[pltpu.VMEM((CH + 8, XBC), F32), pltpu.VMEM((CH + 8, XBC), F32), pltpu.VMEM((NH * HD, NS), F32),
                        pltpu.VMEM((CH, XBC), F32), pltpu.VMEM((CH, SI), F32), pltpu.VMEM((CH, SI), F32),
                        pltpu.VMEM((CH, SI), F32), pltpu.VMEM((NH * HD, NS), F32),
                        pltpu.SemaphoreType.DMA((3 * nq,)), pltpu.SemaphoreType.DMA((3 * nq,))],
        compiler_params=_params(),
    )(dyn, xbcr, xbcr, z, dtr, y, states, cw_ssd, hp_ssd, g_ssd, *parts)


def _inproj_bwd(dlx, dlg, dz, dxbc, ddt, x, dx1, wcatT, g0):
    T = x.shape[0]

    def body(dlx_ref, dlg_ref, dz_ref, dxbc_ref, ddt_ref, x_ref, dx1_ref, w_hbm, g_ref, dx_ref, st_ref, w_vm, sem):
        _load_once([(w_hbm, w_vm)], sem)

        @pl.when(pl.program_id(0) == 0)
        def _():
            st_ref[...] = jnp.zeros_like(st_ref)

        dh = jnp.dot(dlx_ref[...], w_vm[0:1024, :], preferred_element_type=F32)
        dh = dh + jnp.dot(dlg_ref[...], w_vm[1024:2048, :], preferred_element_type=F32)
        dh = dh + jnp.dot(dz_ref[...], w_vm[2048:3072, :], preferred_element_type=F32)
        dh = dh + jnp.dot(dxbc_ref[...], w_vm[3072:3072 + XBC, :], preferred_element_type=F32)
        dh = dh + jnp.dot(ddt_ref[...], w_vm[3072 + XBC:PC, :], preferred_element_type=F32)
        dx, dg = _rms_bwd(x_ref[...], g_ref[...], dh)
        dx_ref[...] = dx1_ref[...] + dx
        st_ref[0:1, :] += dg

    return pl.pallas_call(
        body, name="inproj_bwd", grid=(T // TT,),
        in_specs=[_rows(TT, 1024), _rows(TT, 1024), _rows(TT, 1024), _rows(TT, XBC), _rows(TT, DTP), _rows(TT, D),
                  _rows(TT, D), ANY, _whole((1, D))],
        out_specs=[_rows(TT, D), _whole((8, D))],
        out_shape=[S((T, D), F32), S((8, D), F32)],
        scratch_shapes=[pltpu.VMEM((PC, D), BF), pltpu.SemaphoreType.DMA((1,))],
        compiler_params=_params(),
    )(dlx, dlg, dz, dxbc, ddt, x, dx1, wcatT, g0)


def _wgrad(name, a, b):
    T, M = a.shape
    N = b.shape[1]
    tk = min(T, 2048 if M <= 1024 else 1024)
    tn = N
    while M * tn * 4 > (6 << 20) and tn % 256 == 0:
        tn //= 2

    def body(a_ref, b_ref, o_ref):
        p = lax.dot_general(a_ref[...], b_ref[...], (((0,), (0,)), ((), ())), preferred_element_type=F32)

        @pl.when(pl.program_id(1) == 0)
        def _():
            o_ref[...] = p

        @pl.when(pl.program_id(1) > 0)
        def _():
            o_ref[...] += p

    return pl.pallas_call(
        body, name=name, grid=(N // tn, T // tk),
        in_specs=[pl.BlockSpec((tk, M), lambda j, k: (k, 0)), pl.BlockSpec((tk, tn), lambda j, k: (k, j))],
        out_specs=pl.BlockSpec((M, tn), lambda j, k: (0, j)), out_shape=S((M, N), F32),
        compiler_params=_params(2),
    )(a, b)


def _adamw(name, w, g, m, v):
    R, C = w.shape
    tr = _row_tile(R, C)

    def body(w_ref, g_ref, m_ref, v_ref, d_ref, nm_ref, nv_ref):
        d_ref[...], nm_ref[...], nv_ref[...] = _adam_math(w_ref[...], g_ref[...], m_ref[...], v_ref[...])

    blk = pl.BlockSpec((tr, C), lambda i: (i, 0))
    return pl.pallas_call(
        body, name=name, grid=(R // tr,),
        in_specs=[blk] * 4, out_specs=[blk] * 3, out_shape=[S((R, C), F32)] * 3,
        compiler_params=_params(),
    )(w, g, m, v)


def _half(ref, c, hr):
    sl = pl.ds(pl.multiple_of(c * hr, 8), hr)
    return ref.at[:, sl, :] if len(ref.shape) == 3 else ref.at[sl, :]


def _allgather_weights(shards):
    n = len(shards)

    def body(*refs):
        for phase in range(3):
            _gather_phase(phase, refs[:n], refs[n:2 * n], refs[2 * n], refs[2 * n + 1])

    return pl.pallas_call(
        body, name="allgather_weights", in_specs=[ANY] * n, out_specs=[ANY] * n,
        out_shape=[S((4,) + s.shape, s.dtype) for s in shards],
        scratch_shapes=[pltpu.SemaphoreType.DMA((6 * n,)), pltpu.SemaphoreType.DMA((6 * n,))],
    )(*shards)


def _pair_exchange(name, bufs):
    n = len(bufs)

    def half_shape(b):
        return b.shape[:-2] + (b.shape[-2] // 2, b.shape[-1])

    def body(*refs):
        ins, outs = refs[:n], refs[n:2 * n]
        send_sems, recv_sems = refs[2 * n], refs[2 * n + 1]
        x, y, c = _pos()
        copies = [_remote(_half(src, 1 - c, src.shape[-2] // 2), dst, send_sems.at[k], recv_sems.at[k], (x, y, 1 - c))
                  for k, (src, dst) in enumerate(zip(ins, outs))]
        for cp in copies:
            cp.start()
        for cp in copies:
            cp.wait()

    return pl.pallas_call(
        body, name=name, in_specs=[ANY] * n, out_specs=[ANY] * n,
        out_shape=[S(half_shape(b), b.dtype) for b in bufs],
        scratch_shapes=[pltpu.SemaphoreType.DMA((n,)), pltpu.SemaphoreType.DMA((n,))],
    )(*bufs)


def _quad_exchange(bufs, scatter):
    n = len(bufs)

    def body(*refs):
        ins, outs = refs[:n], refs[n:2 * n]
        send_sems, recv_sems, local_sems = refs[2 * n], refs[2 * n + 1], refs[2 * n + 2]
        x, y, c = _pos()
        me = 2 * x + y
        chips = _other_chips(x, y)
        copies, locals_ = [], []
        for k, (src, dst) in enumerate(zip(ins, outs)):
            if not scatter[k]:
                own = pltpu.make_async_copy(src, dst.at[me], local_sems.at[k])
                own.start()
                locals_.append(own)
            for j, (cx, cy) in enumerate(chips):
                piece = src.at[2 * cx + cy] if scatter[k] else src
                cp = _remote(piece, dst.at[me], send_sems.at[3 * k + j], recv_sems.at[3 * k + j], (cx, cy, c))
                cp.start()
                copies.append(cp)
        for k, (src, dst) in enumerate(zip(ins, outs)):
            for j, (cx, cy) in enumerate(chips):
                blk = dst.at[2 * cx + cy]
                _remote(blk, blk, send_sems.at[3 * k + j], recv_sems.at[3 * k + j], (cx, cy, c)).wait_recv()
        for cp in copies:
            cp.wait_send()
        for cp in locals_:
            cp.wait()

    return pl.pallas_call(
        body, name="quad_exchange", in_specs=[ANY] * n, out_specs=[ANY] * n,
        out_shape=[S((4,) + (b.shape[1:] if sc else b.shape), b.dtype) for b, sc in zip(bufs, scatter)],
        scratch_shapes=[pltpu.SemaphoreType.DMA((3 * n,)), pltpu.SemaphoreType.DMA((3 * n,)), pltpu.SemaphoreType.DMA((n,))],
    )(*bufs)


def _pair_gather(bufs):
    n = len(bufs)

    def body(*refs):
        ins, outs = refs[:n], refs[n:2 * n]
        send_sems, recv_sems = refs[2 * n], refs[2 * n + 1]
        x, y, c = _pos()
        copies = []
        for k, buf in enumerate(outs):
            mine = _half(buf, c, buf.shape[0] // 2)
            cp = _remote(mine, mine, send_sems.at[k], recv_sems.at[k], (x, y, 1 - c))
            cp.start()
            copies.append(cp)
        for k, buf in enumerate(outs):
            theirs = _half(buf, 1 - c, buf.shape[0] // 2)
            _remote(theirs, theirs, send_sems.at[k], recv_sems.at[k], (x, y, 1 - c)).wait_recv()
        for cp in copies:
            cp.wait_send()

    return pl.pallas_call(
        body, name="pair_gather", in_specs=[ANY] * n, out_specs=[ANY] * n,
        out_shape=[S(b.shape, b.dtype) for b in bufs], input_output_aliases={k: k for k in range(n)},
        scratch_shapes=[pltpu.SemaphoreType.DMA((n,)), pltpu.SemaphoreType.DMA((n,))],
    )(*bufs)


def _row_tile(rows, cols, mult=8):
    best = mult
    for t in range(mult, rows + 1, mult):
        if rows % t == 0 and t * cols * 4 <= (1 << 20):
            best = t
    return best


def _add_own_half(name, full, got, c, out_dtype, by_columns):
    hr = got.shape[-2]
    wide = got.shape[-1]
    cols = wide // 4 if by_columns else wide
    tr = _row_tile(hr, wide, 16)
    per = hr // tr

    if by_columns:
        def body(c_ref, a_ref, b_ref, o_ref):
            v = a_ref[...] + b_ref[...]
            for j in range(4):
                o_ref[j] = v[:, j * cols:(j + 1) * cols].astype(out_dtype)

        in_specs = [pl.BlockSpec((tr, wide), lambda i, c_ref: (c_ref[0] * per + i, 0)),
                    pl.BlockSpec((tr, wide), lambda i, c_ref: (i, 0))]
        out_specs = pl.BlockSpec((4, tr, cols), lambda i, c_ref: (0, i, 0))
        grid = (per,)
    else:
        def body(c_ref, a_ref, b_ref, o_ref):
            o_ref[...] = (a_ref[...] + b_ref[...]).astype(out_dtype)

        in_specs = [pl.BlockSpec((1, tr, cols), lambda s, i, c_ref: (s, c_ref[0] * per + i, 0)),
                    pl.BlockSpec((1, tr, cols), lambda s, i, c_ref: (s, i, 0))]
        out_specs = pl.BlockSpec((1, tr, cols), lambda s, i, c_ref: (s, i, 0))
        grid = (4, per)
    return pl.pallas_call(
        body, name=name,
        grid_spec=pltpu.PrefetchScalarGridSpec(num_scalar_prefetch=1, grid=grid, in_specs=in_specs, out_specs=out_specs),
        out_shape=S((4, hr, cols), out_dtype), compiler_params=_params(len(grid)),
    )(jnp.reshape(c, (1,)).astype(jnp.int32), full, got)


def _small_add_own_half(fulls, gots, c):
    n = len(fulls)

    def body(c_ref, *refs):
        for a_ref, b_ref, o_ref in zip(refs[:n], refs[n:2 * n], refs[2 * n:]):
            hr = b_ref.shape[0]
            o_ref[...] = a_ref[pl.ds(pl.multiple_of(c_ref[0] * hr, 8), hr), :] + b_ref[...]

    specs = lambda arrs: [pl.BlockSpec(a.shape, lambda i, c_ref: (0, 0)) for a in arrs]
    return pl.pallas_call(
        body, name="small_pair_add",
        grid_spec=pltpu.PrefetchScalarGridSpec(num_scalar_prefetch=1, grid=(1,), in_specs=specs(fulls) + specs(gots),
                                               out_specs=specs(gots)),
        out_shape=[S(g.shape, F32) for g in gots], compiler_params=_params(),
    )(jnp.reshape(c, (1,)).astype(jnp.int32), *fulls, *gots)


def _small_sum_slots(slots, c):
    n = len(slots)

    def body(c_ref, *refs):
        for s_ref, o_ref in zip(refs[:n], refs[n:]):
            hr = s_ref.shape[1]
            o_ref[pl.ds(pl.multiple_of(c_ref[0] * hr, 8), hr), :] = ((s_ref[0] + s_ref[1]) + s_ref[2]) + s_ref[3]

    outs = [S((2 * s.shape[1], s.shape[2]), F32) for s in slots]
    return pl.pallas_call(
        body, name="small_quad_sum",
        grid_spec=pltpu.PrefetchScalarGridSpec(
            num_scalar_prefetch=1, grid=(1,),
            in_specs=[pl.BlockSpec(s.shape, lambda i, c_ref: (0, 0, 0)) for s in slots],
            out_specs=[pl.BlockSpec(o.shape, lambda i, c_ref: (0, 0)) for o in outs]),
        out_shape=outs, compiler_params=_params(),
    )(jnp.reshape(c, (1,)).astype(jnp.int32), *slots)


def _sum_slots(name, own, slots, me, c):
    _, rows, cols = slots.shape
    tr = _row_tile(rows, cols, 16 if slots.dtype == jnp.bfloat16 else 8)
    per = rows // tr
    three = len(own.shape) == 3

    def body(p_ref, own_ref, s0, s1, s2, s3, o_ref):
        mine = own_ref[0] if three else own_ref[...]
        acc = None
        for j, s_ref in enumerate((s0, s1, s2, s3)):
            v = jnp.where(p_ref[0] == j, mine, s_ref[0]).astype(F32)
            acc = v if acc is None else acc + v
        o_ref[...] = acc

    def slot_spec(j):
        return pl.BlockSpec((1, tr, cols), lambda i, p: (jnp.where(p[0] == j, (j + 1) % 4, j), i, 0))

    own_spec = (pl.BlockSpec((1, tr, cols), lambda i, p: (p[0], i, 0)) if three
                else pl.BlockSpec((tr, cols), lambda i, p: (i, 0)))
    return pl.pallas_call(
        body, name=name,
        grid_spec=pltpu.PrefetchScalarGridSpec(
            num_scalar_prefetch=1, grid=(per,), in_specs=[own_spec] + [slot_spec(j) for j in range(4)],
            out_specs=pl.BlockSpec((tr, cols), lambda i, p: (p[1] * per + i, 0))),
        out_shape=S((2 * rows, cols), F32), compiler_params=_params(),
    )(jnp.stack([me, c]).astype(jnp.int32), own, slots, slots, slots, slots)


BIG = ("w_in", "w_out", "w_gate", "w_up", "w_down")
ROW_PARAMS = (("pre_mix_norm", 0), ("lru_conv_b", 12), ("lru_ba", 13), ("lru_bx", 14), ("lru_lambda", 15),
              ("lru_out_norm", 16), ("ssd_out_norm", 24), ("post_mix_norm", 33), ("pre_ffn_norm", 32), ("post_ffn_norm", 41))
LRU_CONV_ROWS = (8, 12)
LOSS_ROW = 40
HEAD_PARAMS = (("ssd_dt_bias", 0), ("ssd_a_log", 1), ("ssd_d", 2))
SMALL = tuple(n for n, _ in ROW_PARAMS) + ("ssd_conv_b",) + tuple(n for n, _ in HEAD_PARAMS) + (
    "lru_wa", "lru_wx", "lru_conv_w", "ssd_conv_w")


def _diag4(w):
    eye = jnp.eye(4, dtype=w.dtype).reshape(1, 4, 1, 4, 1)
    return (w.reshape(4, 4, BW, 1, BW) * eye).reshape(4, 4 * BW, 4 * BW)


def _adam_math(w, g, m, v):
    mm = ADAM_B1 * m + (1.0 - ADAM_B1) * g
    vv = ADAM_B2 * v + (1.0 - ADAM_B2) * (g * g)
    c1 = 1.0 - ADAM_B1 ** ADAM_STEP
    c2 = 1.0 - ADAM_B2 ** ADAM_STEP
    return -ADAM_LR * ((mm / c1) / (jnp.sqrt(vv / c2) + ADAM_EPS) + ADAM_WD * w), mm, vv


def _adamw_small(rows, cst, hst, dwa, dwx, glcw, gscw, w, m, v):
    def grad_of(name, refs):
        rows_ref, cst_ref, hst_ref, dwa_ref, dwx_ref, glcw_ref, gscw_ref = refs
        for n, r in ROW_PARAMS:
            if n == name:
                return rows_ref[r:r + 1, :]
        for n, r in HEAD_PARAMS:
            if n == name:
                return hst_ref[r:r + 1, 0:NH]
        return {"ssd_conv_b": lambda: cst_ref[4:5, :], "lru_wa": lambda: dwa_ref[...], "lru_wx": lambda: dwx_ref[...],
                "lru_conv_w": lambda: glcw_ref[...], "ssd_conv_w": lambda: gscw_ref[...]}[name]()

    shapes = {n: (w[n].shape[1:] if len(w[n].shape) > 2 else w[n].shape) for n in SMALL}
    flat = lambda d: [d[n].reshape(shapes[n]) for n in SMALL]
    ns = len(SMALL)

    def body(*refs):
        srcs, rest = refs[:7], refs[7:]
        w_refs, m_refs, v_refs = rest[:ns], rest[ns:2 * ns], rest[2 * ns:3 * ns]
        outs = rest[3 * ns:]
        for k, name in enumerate(SMALL):
            g = grad_of(name, srcs)
            d, mm, vv = _adam_math(w_refs[k][...], g, m_refs[k][...], v_refs[k][...])
            outs[4 * k][...] = g
            outs[4 * k + 1][...] = d
            outs[4 * k + 2][...] = mm
            outs[4 * k + 3][...] = vv

    res = pl.pallas_call(
        body, name="adamw_small",
        out_shape=[S(shapes[n], F32) for n in SMALL for _ in range(4)],
        compiler_params=pltpu.CompilerParams(vmem_limit_bytes=VMEM_LIMIT),
    )(rows, cst, hst, dwa, dwx, glcw, gscw, *flat(w), *flat(m), *flat(v))
    return {n: tuple(res[4 * k + i].reshape(w[n].shape) for i in range(4)) for k, n in enumerate(SMALL)}


def _with_own(own, got):
    chip = 2 * lax.axis_index("x") + lax.axis_index("y")
    return jnp.where((jnp.arange(4) == chip).reshape(4, 1, 1), own[None], got)


def _side_by_side(f):
    return f.transpose(1, 0, 2).reshape(f.shape[1], 4 * f.shape[2])


def _stacked(f):
    return f.reshape(4 * f.shape[1], f.shape[2])


def _gather_first_weights(w_in, lru_conv_w, ssd_conv_w):
    conv = jnp.concatenate([lru_conv_w.reshape(-1), ssd_conv_w.reshape(-1)]).astype(F32)
    hi = conv.astype(jnp.bfloat16)
    mid = (conv - hi.astype(F32)).astype(jnp.bfloat16)
    lo = (conv - hi.astype(F32) - mid.astype(F32)).astype(jnp.bfloat16)
    terms = jnp.concatenate([hi, mid, lo])
    n_terms = terms.shape[0]
    conv_rows = -(-n_terms // (128 * 32)) * 32
    terms = jnp.pad(terms, (0, conv_rows * 128 - n_terms)).reshape(conv_rows, 128)
    own = [w_in.astype(WIRE), terms]
    got = _allgather_weights(own)
    win_f = _side_by_side(_with_own(own[0], got[0]))
    t3 = _with_own(own[1], got[1]).reshape(4, -1)[:, :n_terms].reshape(4, 3, -1).astype(F32)
    conv_f = (t3[:, 0] + t3[:, 1]) + t3[:, 2]
    n1 = lru_conv_w.size
    lcw = conv_f[:, :n1].reshape(4, CONV_K, -1).transpose(1, 0, 2).reshape(CONV_K, LW)
    scw = conv_f[:, n1:].reshape(4, CONV_K, -1).transpose(1, 0, 2).reshape(CONV_K, XBC)
    return win_f, lcw, scw


def _pair_stage(tag, bufs, by_columns, small, c):
    nb = len(bufs)
    got = list(_pair_exchange("pair_exchange_" + tag, list(bufs) + list(small)))
    part = [_add_own_half("pair_add_%s%d" % (tag, k), b, r, c, WIRE, bc)
            for k, (b, r, bc) in enumerate(zip(bufs, got[:nb], by_columns))]
    part_small = list(_small_add_own_half(list(small), got[nb:], c)) if small else []
    return part, part_small


def _step(x, tgt, win_f, lcw, scw, sp, late):
    c = lax.axis_index("c")
    me = 2 * lax.axis_index("x") + lax.axis_index("y")
    mm = lambda w: w.astype(BF)
    wcat = jnp.concatenate([mm(win_f), jnp.zeros((D, PC - IN_COLS), BF)], axis=1)
    row = lambda v: v.reshape(1, -1).astype(F32)
    p_lru = jnp.concatenate([lcw, row(sp["lru_conv_b"]), row(sp["lru_ba"]), row(sp["lru_bx"]), row(sp["lru_lambda"]),
                             row(sp["lru_out_norm"]), jnp.zeros((7, LW), F32)], axis=0)
    wa4, wx4 = mm(_diag4(sp["lru_wa"][0])), mm(_diag4(sp["lru_wx"][0]))
    wa4T, wx4T = wa4.transpose(0, 2, 1), wx4.transpose(0, 2, 1)
    cw_ssd = jnp.concatenate([scw, row(sp["ssd_conv_b"]), jnp.zeros((3, XBC), F32)], axis=0)
    padh = lambda v: jnp.pad(row(v), ((0, 0), (0, DTP - NH)))
    hp_ssd = jnp.concatenate([padh(sp["ssd_dt_bias"]), padh(sp["ssd_a_log"]), padh(sp["ssd_d"]), jnp.zeros((5, DTP), F32)], axis=0)
    g0, g_ssd = row(sp["pre_mix_norm"]), row(sp["ssd_out_norm"])
    g_pm, g_pf, g_pff = row(sp["post_mix_norm"]), row(sp["pre_ffn_norm"]), row(sp["post_ffn_norm"])

    h0, lxr, lg, z, xbcr, dtr = _inproj(x, g0, wcat)
    h, ylru = _lru_fwd(lxr, lg, p_lru, wa4, wx4)
    y, yssd, states, *got = _ssd_fwd(xbcr, z, dtr, cw_ssd, hp_ssd, g_ssd, late)
    full = [_with_own(o, g) for o, g in zip(late, got)]
    wout, wg, wu, wd = mm(_stacked(full[0])), mm(_side_by_side(full[1])), mm(_side_by_side(full[2])), mm(_stacked(full[3]))
    mix, x1, h2 = _outproj(ylru, yssd, x, wout, g_pm, g_pf)
    gate, up, act, df, dx2, st_ffn = _ffn_fwd(h2, x1, tgt, wg, wu, wd, g_pff)
    dgate, dup, dh2 = _ffn_bwd(df, gate, up, wd.T, wg.T, wu.T)

    dwg = _wgrad("wgrad_gate", h2, dgate)
    dwu = _wgrad("wgrad_up", h2, dup)
    dwd = _wgrad("wgrad_down", act, df)
    part_ffn, _ = _pair_stage("ffn", [dwg, dwu, dwd.reshape(4, DFF // 4, D)], [True, True, False], [], c)

    dx1, dmix, dyl, dys, st_mix = _mix_bwd(dh2, x1, dx2, mix, wout.T, g_pf, g_pm)
    dlx, dlg, st_lru, dwa, dwx = _lru_bwd(dyl, lxr, lg, h, p_lru, wa4, wx4, wa4T, wx4T)
    dxbc, dz, ddt, cst, hst, gst, *slots_ffn = _ssd_bwd(dys, xbcr, z, dtr, y, states, cw_ssd, hp_ssd, g_ssd, part_ffn)
    gx, st_in = _inproj_bwd(dlx, dlg, dz, dxbc, ddt, x, dx1, wcat.T, g0)
    red_ffn = [_sum_slots("quad_sum_ffn%d" % k, p, s, me, c) for k, (p, s) in enumerate(zip(part_ffn, slots_ffn))]

    dwo_l = _wgrad("wgrad_out_lru", ylru, dmix)
    dwo_s = _wgrad("wgrad_out_ssd", yssd, dmix)
    pin = [_wgrad("wgrad_in_%d" % k, h0, b) for k, b in enumerate((dlx, dlg, dz, dxbc, ddt))]
    dwin = jnp.concatenate(pin[:4] + [pin[4][:, :NH]], axis=1)
    dwo = jnp.concatenate([dwo_l, dwo_s], axis=0).reshape(4, (LW + SI) // 4, D)
    rows = jnp.concatenate([st_in, st_lru, gst, st_mix, st_ffn], axis=0)
    small = [rows, cst, hst, dwa.reshape(NBLK * BW, BW), dwx.reshape(NBLK * BW, BW)]
    part, part_small = _pair_stage("mix", [dwin, dwo], [True, False], small, c)
    slots = list(_quad_exchange(part + part_small, [True] * 2 + [False] * len(small)))
    red = [_sum_slots("quad_sum_mix%d" % k, p, s, me, c) for k, (p, s) in enumerate(zip(part, slots[:2]))]
    red_small = list(_small_sum_slots(slots[2:], c))
    out = list(_pair_gather(red + red_ffn + red_small))
    big = dict(zip(("w_in", "w_out", "w_gate", "w_up", "w_down"), out[:5]))
    return gx, big, out[5:]


def kernel(x, pre_mix_norm, w_in, lru_conv_w, lru_conv_b, lru_wa, lru_ba, lru_wx, lru_bx, lru_lambda, lru_out_norm, ssd_conv_w, ssd_conv_b, ssd_dt_bias, ssd_a_log, ssd_d, ssd_out_norm, w_out, post_mix_norm, pre_ffn_norm, w_gate, w_up, w_down, post_ffn_norm, loss_target, m_pre_mix_norm, m_w_in, m_lru_conv_w, m_lru_conv_b, m_lru_wa, m_lru_ba, m_lru_wx, m_lru_bx, m_lru_lambda, m_lru_out_norm, m_ssd_conv_w, m_ssd_conv_b, m_ssd_dt_bias, m_ssd_a_log, m_ssd_d, m_ssd_out_norm, m_w_out, m_post_mix_norm, m_pre_ffn_norm, m_w_gate, m_w_up, m_w_down, m_post_ffn_norm, v_pre_mix_norm, v_w_in, v_lru_conv_w, v_lru_conv_b, v_lru_wa, v_lru_ba, v_lru_wx, v_lru_bx, v_lru_lambda, v_lru_out_norm, v_ssd_conv_w, v_ssd_conv_b, v_ssd_dt_bias, v_ssd_a_log, v_ssd_d, v_ssd_out_norm, v_w_out, v_post_mix_norm, v_pre_ffn_norm, v_w_gate, v_w_up, v_w_down, v_post_ffn_norm):
    args = dict(locals())
    names = list(SMALL) + list(BIG)
    w = {n: args[n] for n in names}
    m = {n: args["m_" + n] for n in names}
    v = {n: args["v_" + n] for n in names}
    chip = 2 * lax.axis_index("x") + lax.axis_index("y")

    win_f, lcw, scw = _gather_first_weights(w_in[0], lru_conv_w[0], ssd_conv_w[0])
    late = [a[0].astype(WIRE) for a in (w_out, w_gate, w_up, w_down)]
    gx, red, (rows, cst, hst, dwa, dwx) = _step(x[0], loss_target[0], win_f, lcw, scw, {n: w[n] for n in SMALL}, late)
    loss = jnp.sum(rows[LOSS_ROW])

    grads, delta, new_m, new_v = {}, {}, {}, {}
    for n in BIG:
        g = red[n]
        d, nm, nv = _adamw("adamw_" + n, w[n][0], g, m[n][0], v[n][0])
        grads[n], delta[n], new_m[n], new_v[n] = g[None], d[None], nm[None], nv[None]

    lc, sc = lru_conv_w.shape[-1], ssd_conv_w.shape[-1]
    glcw = lax.dynamic_slice_in_dim(rows[LRU_CONV_ROWS[0]:LRU_CONV_ROWS[1]], chip * lc, lc, axis=1)
    gscw = lax.dynamic_slice_in_dim(cst[0:CONV_K], chip * sc, sc, axis=1)
    res = _adamw_small(rows, cst, hst, dwa.reshape(NBLK, BW, BW), dwx.reshape(NBLK, BW, BW), glcw, gscw,
                       {n: w[n] for n in SMALL}, {n: m[n] for n in SMALL}, {n: v[n] for n in SMALL})
    for n in SMALL:
        grads[n], delta[n], new_m[n], new_v[n] = res[n]

    order = ["pre_mix_norm", "w_in", "lru_conv_w", "lru_conv_b", "lru_wa", "lru_ba", "lru_wx", "lru_bx", "lru_lambda",
             "lru_out_norm", "ssd_conv_w", "ssd_conv_b", "ssd_dt_bias", "ssd_a_log", "ssd_d", "ssd_out_norm", "w_out",
             "post_mix_norm", "pre_ffn_norm", "w_gate", "w_up", "w_down", "post_ffn_norm"]
    return (loss, gx[None], *[grads[n] for n in order], *[delta[n] for n in order],
            *[new_m[n] for n in order], *[new_v[n] for n in order])
```

```python
import functools

import jax
import jax.numpy as jnp
from jax import lax
from jax.experimental import pallas as pl
from jax.experimental.pallas import tpu as pltpu

F32 = jnp.float32
BF = jnp.bfloat16

D = 1024
LW = 1024
NBLK = 16
BW = 64
SI = 1024
NH = 16
HD = 64
NG = 2
HPG = NH // NG
NS = 128
CH = 128
XBC = SI + 2 * NG * NS
DTP = 128
PC = 3 * 1024 + XBC + DTP
DFF = 2816
IN_COLS = 4624
EPS = 1e-6
LRU_C = 8.0
CONV_K = 4
TT = 256
VMEM_LIMIT = 56 * 1024 * 1024

ADAM_LR, ADAM_B1, ADAM_B2, ADAM_EPS, ADAM_WD, ADAM_STEP = 0.001, 0.9, 0.999, 1e-08, 0.01, 10

MESH = pl.DeviceIdType.MESH


def _mm(a, b):
    return jnp.dot(a.astype(BF), b.astype(BF), preferred_element_type=F32)


def _mm_nt(a, b):
    return lax.dot_general(a.astype(BF), b.astype(BF), (((1,), (1,)), ((), ())), preferred_element_type=F32)


def _mm_tn(a, b):
    return lax.dot_general(a.astype(BF), b.astype(BF), (((0,), (0,)), ((), ())), preferred_element_type=F32)


def _sigmoid(x):
    return 0.5 * jnp.tanh(0.5 * x) + 0.5


def _softplus(x):
    return jnp.maximum(x, 0.0) + jnp.log1p(jnp.exp(-jnp.abs(x)))


_GELU_C = 0.7978845608028654
_GELU_K = 0.044715


def _gelu(x):
    t = jnp.tanh(_GELU_C * (x + _GELU_K * x * x * x))
    return 0.5 * x * (1.0 + t)


def _gelu_grad(x):
    t = jnp.tanh(_GELU_C * (x + _GELU_K * x * x * x))
    return 0.5 * (1.0 + t) + 0.5 * x * (1.0 - t * t) * _GELU_C * (1.0 + 3.0 * _GELU_K * x * x)


def _rms_fwd(x, g):
    r = lax.rsqrt(jnp.mean(x * x, axis=-1, keepdims=True) + EPS)
    return x * r * g


def _rms_bwd(x, g, dy):
    r = lax.rsqrt(jnp.mean(x * x, axis=-1, keepdims=True) + EPS)
    xh = x * r
    dxh = dy * g
    dg = jnp.sum(dy * xh, axis=0, keepdims=True)
    dx = r * (dxh - xh * jnp.mean(dxh * xh, axis=-1, keepdims=True))
    return dx, dg


def _sum_all(x):
    return jnp.sum(jnp.sum(x, axis=1, keepdims=True), axis=0, keepdims=True)


def _cumsum_rows(x, n):
    row = lax.broadcasted_iota(jnp.int32, x.shape, 0)
    k = 1
    while k < n:
        x = x + jnp.where(row >= k, pltpu.roll(x, k, 0), 0.0)
        k *= 2
    return x


def _rev_cumsum_rows(x, n):
    row = lax.broadcasted_iota(jnp.int32, x.shape, 0)
    k = 1
    while k < n:
        x = x + jnp.where(row < n - k, pltpu.roll(x, n - k, 0), 0.0)
        k *= 2
    return x


def _load_once(pairs, sem):
    @pl.when(pl.program_id(0) == 0)
    def _():
        for k, (src, dst) in enumerate(pairs):
            pltpu.make_async_copy(src, dst, sem.at[k]).start()
        for k, (src, dst) in enumerate(pairs):
            pltpu.make_async_copy(src, dst, sem.at[k]).wait()


def _params(n_axes=1):
    return pltpu.CompilerParams(dimension_semantics=("arbitrary",) * n_axes, vmem_limit_bytes=VMEM_LIMIT)


def _rows(n, width, rev_of=None):
    if rev_of is None:
        return pl.BlockSpec((n, width), lambda i: (i, 0))
    return pl.BlockSpec((n, width), lambda i: (rev_of - 1 - i, 0))


def _whole(shape):
    nd = len(shape)
    return pl.BlockSpec(shape, lambda i: (0,) * nd)


ANY = pl.BlockSpec(memory_space=pl.ANY)
S = jax.ShapeDtypeStruct
WIRE = jnp.bfloat16


def _pos():
    return lax.axis_index("x"), lax.axis_index("y"), lax.axis_index("c")


def _other_chips(x, y):
    return [(1 - x, y), (x, 1 - y), (1 - x, 1 - y)]


def _remote(src, dst, send_sem, recv_sem, to):
    return pltpu.make_async_remote_copy(src_ref=src, dst_ref=dst, send_sem=send_sem, recv_sem=recv_sem,
                                        device_id=to, device_id_type=MESH)


def _gather_phase(phase, ins, outs, send_sems, recv_sems):
    x, y, c = _pos()
    me = 2 * x + y
    chips = _other_chips(x, y)
    for i, (src, dst) in enumerate(zip(ins, outs)):
        hr = src.shape[0] // 2
        my_half = pl.ds(pl.multiple_of(c * hr, 16), hr)
        sib_half = pl.ds(pl.multiple_of((1 - c) * hr, 16), hr)
        for k, (cx, cy) in enumerate(chips):
            s1, r1 = send_sems.at[6 * i + k], recv_sems.at[6 * i + k]
            s2, r2 = send_sems.at[6 * i + 3 + k], recv_sems.at[6 * i + 3 + k]
            first = lambda: _remote(src.at[my_half, :], dst.at[me, my_half, :], s1, r1, (cx, cy, c))
            landed = dst.at[2 * cx + cy, my_half, :]
            passed = lambda: _remote(landed, landed, s2, r2, (x, y, 1 - c))
            if phase == 0:
                first().start()
            elif phase == 1:
                _remote(landed, landed, s1, r1, (cx, cy, c)).wait_recv()
                passed().start()
            else:
                theirs = dst.at[2 * cx + cy, sib_half, :]
                _remote(theirs, theirs, s2, r2, (x, y, 1 - c)).wait_recv()
                first().wait_send()
                passed().wait_send()


def _quad_phase(phase, ins, outs, send_sems, recv_sems):
    x, y, c = _pos()
    me = 2 * x + y
    for i, (src, dst) in enumerate(zip(ins, outs)):
        for k, (cx, cy) in enumerate(_other_chips(x, y)):
            cp = _remote(src.at[2 * cx + cy], dst.at[me], send_sems.at[3 * i + k], recv_sems.at[3 * i + k], (cx, cy, c))
            if phase == 0:
                cp.start()
            else:
                got = dst.at[2 * cx + cy]
                _remote(got, got, send_sems.at[3 * i + k], recv_sems.at[3 * i + k], (cx, cy, c)).wait_recv()
                cp.wait_send()


def _inproj(x, g0, wcat):
    T = x.shape[0]

    def body(x_ref, g_ref, w_hbm, h0_ref, lx_ref, lg_ref, z_ref, xbc_ref, dt_ref, w_vm, sem):
        _load_once([(w_hbm, w_vm)], sem)
        h = _rms_fwd(x_ref[...], g_ref[...]).astype(BF)
        h0_ref[...] = h
        lx_ref[...] = jnp.dot(h, w_vm[:, 0:1024], preferred_element_type=F32)
        lg_ref[...] = jnp.dot(h, w_vm[:, 1024:2048], preferred_element_type=F32)
        z_ref[...] = jnp.dot(h, w_vm[:, 2048:3072], preferred_element_type=F32)
        xbc_ref[...] = jnp.dot(h, w_vm[:, 3072:3072 + XBC], preferred_element_type=F32)
        dt_ref[...] = jnp.dot(h, w_vm[:, 3072 + XBC:PC], preferred_element_type=F32)

    return pl.pallas_call(
        body, name="inproj", grid=(T // TT,),
        in_specs=[_rows(TT, D), _whole((1, D)), ANY],
        out_specs=[_rows(TT, D), _rows(TT, 1024), _rows(TT, 1024), _rows(TT, 1024), _rows(TT, XBC), _rows(TT, DTP)],
        out_shape=[S((T, D), BF), S((T, 1024), F32), S((T, 1024), F32), S((T, 1024), F32), S((T, XBC), F32), S((T, DTP), F32)],
        scratch_shapes=[pltpu.VMEM((D, PC), BF), pltpu.SemaphoreType.DMA((1,))],
        compiler_params=_params(),
    )(x, g0, wcat)


def _blockdiag_mm(v, w4_ref):
    return jnp.concatenate([_mm(v[:, 256 * j:256 * (j + 1)], w4_ref[j]) for j in range(4)], axis=1)


def _lru_gates(lx, p_ref, wa_ref, wx_ref):
    r = _sigmoid(_blockdiag_mm(lx, wa_ref) + p_ref[5:6, :])
    i = _sigmoid(_blockdiag_mm(lx, wx_ref) + p_ref[6:7, :])
    sp = _softplus(-p_ref[7:8, :])
    la = -LRU_C * r * sp
    a = jnp.exp(la)
    th = jnp.tanh(la)
    mult = jnp.sqrt(-2.0 * th / (1.0 - th))
    return r, i, sp, a, mult


def _conv_from(xp_ref, p_ref, n):
    taps = [xp_ref[pl.ds(8 - CONV_K + 1 + k, n), :] for k in range(CONV_K)]
    acc = p_ref[4:5, :] + p_ref[0:1, :] * taps[0]
    for k in range(1, CONV_K):
        acc = acc + p_ref[k:k + 1, :] * taps[k]
    return acc, taps


def _lru_fwd(lxr, lg, p_lru, wa4, wx4, shards):
    T = lxr.shape[0]
    NT = T // TT
    ng = len(shards)

    def body(*refs):
        lx_ref, lg_ref, p_ref, wa_ref, wx_ref = refs[:5]
        sh_in = refs[5:5 + ng]
        h_ref, y_ref = refs[5 + ng:7 + ng]
        sh_out = refs[7 + ng:7 + 2 * ng]
        xp, a_s, u_s, hc, send_sems, recv_sems = refs[7 + 2 * ng:]
        for phase, step in enumerate((0, NT // 2, NT - 1)):
            @pl.when(pl.program_id(0) == step)
            def _():
                _gather_phase(phase, sh_in, sh_out, send_sems, recv_sems)

        @pl.when(pl.program_id(0) == 0)
        def _():
            xp[0:8, :] = jnp.zeros((8, LW), F32)
            hc[...] = jnp.zeros_like(hc)

        xp[8:8 + TT, :] = lx_ref[...]
        lx, _ = _conv_from(xp, p_ref, TT)
        xp[0:8, :] = xp[TT:TT + 8, :]
        r, i, sp, a, mult = _lru_gates(lx, p_ref, wa_ref, wx_ref)
        a_s[...] = a
        u_s[...] = mult * (i * lx)

        def step(t, h):
            h = a_s[pl.ds(t, 1), :] * h + u_s[pl.ds(t, 1), :]
            h_ref[pl.ds(t, 1), :] = h
            return h

        hc[0:1, :] = lax.fori_loop(0, TT, step, hc[0:1, :], unroll=8)
        gated = h_ref[...] * _gelu(lg_ref[...])
        y_ref[...] = _rms_fwd(gated, p_ref[8:9, :]).astype(BF)

    return pl.pallas_call(
        body, name="lru_fwd", grid=(NT,),
        in_specs=[_rows(TT, LW), _rows(TT, LW), _whole((16, LW)), _whole((4, 256, 256)), _whole((4, 256, 256))] + [ANY] * ng,
        out_specs=[_rows(TT, LW), _rows(TT, LW)] + [ANY] * ng,
        out_shape=[S((T, LW), F32), S((T, LW), BF)] + [S((4,) + s.shape, s.dtype) for s in shards],
        scratch_shapes=[pltpu.VMEM((TT + 8, LW), F32), pltpu.VMEM((TT, LW), F32), pltpu.VMEM((TT, LW), F32),
                        pltpu.VMEM((8, LW), F32), pltpu.SemaphoreType.DMA((6 * ng,)), pltpu.SemaphoreType.DMA((6 * ng,))],
        compiler_params=_params(),
    )(lxr, lg, p_lru, wa4, wx4, *shards)


def _ssd_prep(xp, xr_ref, dt_ref, cw_ref, hp_ref):
    xp[8:8 + CH, :] = xr_ref[...]
    cv, taps = _conv_from(xp, cw_ref, CH)
    sg = _sigmoid(cv)
    xbc = cv * sg
    lane = lax.broadcasted_iota(jnp.int32, (CH, DTP), 1)
    raw = dt_ref[...] + hp_ref[0:1, :]
    dtv = jnp.where(lane < NH, _softplus(raw), 0.0)
    A = jnp.where(lane[0:1, :] < NH, -jnp.exp(hp_ref[1:2, :]), 0.0)
    cs = _cumsum_rows(dtv * A, CH)
    return cv, sg, xbc, raw, dtv, A, cs, taps


def _per_head_lanes(v):
    r = v.shape[0]
    first = lax.broadcasted_iota(jnp.int32, (r, 2 * HD), 1) < HD
    pairs = [jnp.where(first, jnp.broadcast_to(v[:, 2 * j:2 * j + 1], (r, 2 * HD)),
                       jnp.broadcast_to(v[:, 2 * j + 1:2 * j + 2], (r, 2 * HD))) for j in range(NH // 2)]
    return jnp.concatenate(pairs, axis=1)


def _per_head_rows(col, g):
    return jnp.concatenate([jnp.broadcast_to(col[g * HPG + k:g * HPG + k + 1, :], (HD, NS)) for k in range(HPG)], axis=0)


def _ssd_decays(cs):
    csT = cs.T
    cl = cs[CH - 1:CH, :]
    E_x = _per_head_lanes(jnp.exp(cs))
    dsm = jnp.exp(cl - cs)
    ds_x = _per_head_lanes(dsm)
    El_rows = jnp.broadcast_to(jnp.exp(csT[0:NH, CH - 1:CH]), (NH, NS))
    return csT, dsm, E_x, ds_x, El_rows


def _ssd_fwd(xbcr, z, dtr, cw_ssd, hp_ssd, g_ssd, shards):
    T = xbcr.shape[0]
    NC = T // CH
    ng = len(shards)

    def body(*refs):
        xr_ref, z_ref, dt_ref, cw_ref, hp_ref, g_ref = refs[:6]
        sh_in = refs[6:6 + ng]
        y_ref, yn_ref, st_ref = refs[6 + ng:9 + ng]
        sh_out = refs[9 + ng:9 + 2 * ng]
        xp, st, send_sems, recv_sems = refs[9 + 2 * ng:]
        for phase, step in enumerate((0, NC // 2, NC - 1)):
            @pl.when(pl.program_id(0) == step)
            def _():
                _gather_phase(phase, sh_in, sh_out, send_sems, recv_sems)

        @pl.when(pl.program_id(0) == 0)
        def _():
            xp[0:8, :] = jnp.zeros((8, XBC), F32)
            st[...] = jnp.zeros_like(st)

        cv, sg, xbc, raw, dtv, A, cs, _ = _ssd_prep(xp, xr_ref, dt_ref, cw_ref, hp_ref)
        xp[0:8, :] = xp[CH:CH + 8, :]
        st_ref[0] = st[...]
        csT, dsm, E_x, ds_x, El_rows = _ssd_decays(cs)
        X = xbc[:, 0:SI]
        xs = X * _per_head_lanes(dtv)
        xsd = (xs * ds_x).astype(BF)
        DX = _per_head_lanes(hp_ref[...])[2:3, :] * X
        tril = lax.broadcasted_iota(jnp.int32, (CH, CH), 0) >= lax.broadcasted_iota(jnp.int32, (CH, CH), 1)
        first = lax.broadcasted_iota(jnp.int32, (CH, 2 * HD), 1) < HD
        GW = HPG * HD
        for g in range(NG):
            Bg = xbc[:, SI + NS * g:SI + NS * (g + 1)].astype(BF)
            Cg = xbc[:, SI + NG * NS + NS * g:SI + NG * NS + NS * (g + 1)].astype(BF)
            G = _mm_nt(Cg, Bg)
            Sg = st[GW * g:GW * (g + 1), :]
            Yo = _mm_nt(Cg, Sg) * E_x[:, GW * g:GW * (g + 1)]
            st[GW * g:GW * (g + 1), :] = _per_head_rows(El_rows, g) * Sg + _mm_tn(xsd[:, GW * g:GW * (g + 1)], Bg)
            for jj in range(HPG // 2):
                j = g * (HPG // 2) + jj
                ps = slice(2 * HD * j, 2 * HD * (j + 1))
                xs_pair = xs[:, ps]
                acc = Yo[:, 2 * HD * jj:2 * HD * (jj + 1)] + DX[:, ps]
                for e in range(2):
                    h = 2 * j + e
                    Lm = jnp.exp(jnp.where(tril, cs[:, h:h + 1] - csT[h:h + 1, :], -1e30))
                    acc = acc + _mm(G * Lm, jnp.where(first if e == 0 else ~first, xs_pair, 0.0))
                y_ref[:, ps] = acc
        zz = z_ref[...]
        gated = y_ref[...] * (zz * _sigmoid(zz))
        yn_ref[...] = _rms_fwd(gated, g_ref[...]).astype(BF)

    return pl.pallas_call(
        body, name="ssd_fwd", grid=(NC,),
        in_specs=[_rows(CH, XBC), _rows(CH, SI), _rows(CH, DTP), _whole((8, XBC)), _whole((8, DTP)), _whole((1, SI))]
        + [ANY] * ng,
        out_specs=[_rows(CH, SI), _rows(CH, SI), pl.BlockSpec((1, NH * HD, NS), lambda i: (i, 0, 0))] + [ANY] * ng,
        out_shape=[S((T, SI), F32), S((T, SI), BF), S((NC, NH * HD, NS), F32)] + [S((4,) + s.shape, s.dtype) for s in shards],
        scratch_shapes=[pltpu.VMEM((CH + 8, XBC), F32), pltpu.VMEM((NH * HD, NS), F32),
                        pltpu.SemaphoreType.DMA((6 * ng,)), pltpu.SemaphoreType.DMA((6 * ng,))],
        compiler_params=_params(),
    )(xbcr, z, dtr, cw_ssd, hp_ssd, g_ssd, *shards)


def _outproj(ylru, yssd, x, wout, g_pm, g_pf):
    T = x.shape[0]

    def body(yl_ref, ys_ref, x_ref, w_hbm, gpm_ref, gpf_ref, mix_ref, x1_ref, h2_ref, w_vm, sem):
        _load_once([(w_hbm, w_vm)], sem)
        mix = (jnp.dot(yl_ref[...], w_vm[0:LW, :], preferred_element_type=F32)
               + jnp.dot(ys_ref[...], w_vm[LW:LW + SI, :], preferred_element_type=F32))
        mix_ref[...] = mix
        x1 = x_ref[...] + _rms_fwd(mix, gpm_ref[...])
        x1_ref[...] = x1
        h2_ref[...] = _rms_fwd(x1, gpf_ref[...]).astype(BF)

    return pl.pallas_call(
        body, name="outproj", grid=(T // TT,),
        in_specs=[_rows(TT, LW), _rows(TT, SI), _rows(TT, D), ANY, _whole((1, D)), _whole((1, D))],
        out_specs=[_rows(TT, D), _rows(TT, D), _rows(TT, D)],
        out_shape=[S((T, D), F32), S((T, D), F32), S((T, D), BF)],
        scratch_shapes=[pltpu.VMEM((LW + SI, D), BF), pltpu.SemaphoreType.DMA((1,))],
        compiler_params=_params(),
    )(ylru, yssd, x, wout, g_pm, g_pf)


def _ffn_fwd(h2, x1, tgt, wg, wu, wd, g_pff):
    T = x1.shape[0]

    def body(h2_ref, x1_ref, t_ref, wg_hbm, wu_hbm, wd_hbm, g_ref,
             gate_ref, up_ref, act_ref, df_ref, dx2_ref, st_ref, wg_vm, wu_vm, wd_vm, sem):
        _load_once([(wg_hbm, wg_vm), (wu_hbm, wu_vm), (wd_hbm, wd_vm)], sem)

        @pl.when(pl.program_id(0) == 0)
        def _():
            st_ref[...] = jnp.zeros_like(st_ref)

        h2 = h2_ref[...]
        gate = jnp.dot(h2, wg_vm[...], preferred_element_type=F32)
        up = jnp.dot(h2, wu_vm[...], preferred_element_type=F32)
        gate_ref[...] = gate
        up_ref[...] = up
        act = (gate * _sigmoid(gate) * up).astype(BF)
        act_ref[...] = act
        f = jnp.dot(act, wd_vm[...], preferred_element_type=F32)
        g = g_ref[...]
        x2 = x1_ref[...] + _rms_fwd(f, g)
        err = x2 - t_ref[...]
        st_ref[0:1, :] += 0.5 * jnp.sum(err * err, axis=0, keepdims=True) * (1.0 / D)
        dx2 = err * (1.0 / D)
        dx2_ref[...] = dx2
        df, dg = _rms_bwd(f, g, dx2)
        df_ref[...] = df.astype(BF)
        st_ref[1:2, :] += dg

    return pl.pallas_call(
        body, name="ffn_fwd", grid=(T // TT,),
        in_specs=[_rows(TT, D), _rows(TT, D), _rows(TT, D), ANY, ANY, ANY, _whole((1, D))],
        out_specs=[_rows(TT, DFF), _rows(TT, DFF), _rows(TT, DFF), _rows(TT, D), _rows(TT, D), _whole((8, D))],
        out_shape=[S((T, DFF), F32), S((T, DFF), F32), S((T, DFF), BF), S((T, D), BF), S((T, D), F32), S((8, D), F32)],
        scratch_shapes=[pltpu.VMEM((D, DFF), BF), pltpu.VMEM((D, DFF), BF), pltpu.VMEM((DFF, D), BF),
                        pltpu.SemaphoreType.DMA((3,))],
        compiler_params=_params(),
    )(h2, x1, tgt, wg, wu, wd, g_pff)


def _ffn_bwd(df, gate, up, wdT, wgT, wuT):
    T = df.shape[0]

    def body(df_ref, gate_ref, up_ref, wd_hbm, wg_hbm, wu_hbm, dgate_ref, dup_ref, dh2_ref, wd_vm, wg_vm, wu_vm, sem):
        _load_once([(wd_hbm, wd_vm), (wg_hbm, wg_vm), (wu_hbm, wu_vm)], sem)
        dact = jnp.dot(df_ref[...], wd_vm[...], preferred_element_type=F32)
        gate = gate_ref[...]
        s = _sigmoid(gate)
        dup = (dact * (gate * s)).astype(BF)
        dgate = (dact * up_ref[...] * (s + gate * s * (1.0 - s))).astype(BF)
        dup_ref[...] = dup
        dgate_ref[...] = dgate
        dh2_ref[...] = (jnp.dot(dgate, wg_vm[...], preferred_element_type=F32)
                        + jnp.dot(dup, wu_vm[...], preferred_element_type=F32))

    return pl.pallas_call(
        body, name="ffn_bwd", grid=(T // TT,),
        in_specs=[_rows(TT, D), _rows(TT, DFF), _rows(TT, DFF), ANY, ANY, ANY],
        out_specs=[_rows(TT, DFF), _rows(TT, DFF), _rows(TT, D)],
        out_shape=[S((T, DFF), BF), S((T, DFF), BF), S((T, D), F32)],
        scratch_shapes=[pltpu.VMEM((D, DFF), BF), pltpu.VMEM((DFF, D), BF), pltpu.VMEM((DFF, D), BF),
                        pltpu.SemaphoreType.DMA((3,))],
        compiler_params=_params(),
    )(df, gate, up, wdT, wgT, wuT)


def _mix_bwd(dh2, x1, dx2, mix, woutT, g_pf, g_pm):
    T = x1.shape[0]

    def body(dh2_ref, x1_ref, dx2_ref, mix_ref, w_hbm, gpf_ref, gpm_ref,
             dx1_ref, dmix_ref, dyl_ref, dys_ref, st_ref, w_vm, sem):
        _load_once([(w_hbm, w_vm)], sem)

        @pl.when(pl.program_id(0) == 0)
        def _():
            st_ref[...] = jnp.zeros_like(st_ref)

        dxa, dgpf = _rms_bwd(x1_ref[...], gpf_ref[...], dh2_ref[...])
        dx1 = dx2_ref[...] + dxa
        dx1_ref[...] = dx1
        dmix, dgpm = _rms_bwd(mix_ref[...], gpm_ref[...], dx1)
        dmix = dmix.astype(BF)
        dmix_ref[...] = dmix
        st_ref[0:1, :] += dgpf
        st_ref[1:2, :] += dgpm
        dyl_ref[...] = jnp.dot(dmix, w_vm[:, 0:LW], preferred_element_type=F32)
        dys_ref[...] = jnp.dot(dmix, w_vm[:, LW:LW + SI], preferred_element_type=F32)

    return pl.pallas_call(
        body, name="mix_bwd", grid=(T // TT,),
        in_specs=[_rows(TT, D), _rows(TT, D), _rows(TT, D), _rows(TT, D), ANY, _whole((1, D)), _whole((1, D))],
        out_specs=[_rows(TT, D), _rows(TT, D), _rows(TT, LW), _rows(TT, SI), _whole((8, D))],
        out_shape=[S((T, D), F32), S((T, D), BF), S((T, LW), F32), S((T, SI), F32), S((8, D), F32)],
        scratch_shapes=[pltpu.VMEM((D, LW + SI), BF), pltpu.SemaphoreType.DMA((1,))],
        compiler_params=_params(),
    )(dh2, x1, dx2, mix, woutT, g_pf, g_pm)


def _halo(width, n_tiles, tile):
    per = tile // 8
    return pl.BlockSpec((8, width), lambda i: (jnp.maximum((n_tiles - 1 - i) * per - 1, 0), 0))


def _lru_bwd(dy, lxr, lg, h, p_lru, wa4, wx4, wa4T, wx4T):
    T = dy.shape[0]
    NT = T // TT

    def body(dy_ref, lx_ref, lxh_ref, lg_ref, h_ref, hh_ref, p_ref, wa_ref, wx_ref, waT_ref, wxT_ref,
             dlx_ref, dlg_ref, st_ref, dwa_ref, dwx_ref, xp, hp, dp, a_s, d_s, g_s, cc):
        first = pl.program_id(0) == 0
        top = pl.program_id(0) == NT - 1

        @pl.when(first)
        def _():
            st_ref[...] = jnp.zeros_like(st_ref)
            dwa_ref[...] = jnp.zeros_like(dwa_ref)
            dwx_ref[...] = jnp.zeros_like(dwx_ref)
            dp[TT:TT + 8, :] = jnp.zeros((8, LW), F32)
            cc[...] = jnp.zeros_like(cc)

        keep = jnp.where(top, 0.0, 1.0)
        xp[0:8, :] = lxh_ref[...] * keep
        xp[8:8 + TT, :] = lx_ref[...]
        hp[0:8, :] = hh_ref[...] * keep
        hp[8:8 + TT, :] = h_ref[...]
        lx, taps = _conv_from(xp, p_ref, TT)
        r, i, sp, a, mult = _lru_gates(lx, p_ref, wa_ref, wx_ref)

        lg = lg_ref[...]
        hcur = h_ref[...]
        ge = _gelu(lg)
        dgated, dgn = _rms_bwd(hcur * ge, p_ref[8:9, :], dy_ref[...])
        st_ref[8:9, :] += dgn
        dlg_ref[...] = (dgated * hcur * _gelu_grad(lg)).astype(BF)
        a_s[...] = a
        d_s[...] = dgated * ge

        def step(k, c):
            t = TT - 1 - k
            g = d_s[pl.ds(t, 1), :] + c
            g_s[pl.ds(t, 1), :] = g
            return a_s[pl.ds(t, 1), :] * g

        cc[0:1, :] = lax.fori_loop(0, TT, step, cc[0:1, :], unroll=8)
        gt = g_s[...]
        da = gt * hp[pl.ds(7, TT), :]
        dmult = gt * i * lx
        di = gt * mult * lx
        dlxc = gt * mult * i
        dla = da * a - dmult * (a * a) / mult
        dr = dla * (-LRU_C * sp)
        st_ref[7:8, :] += jnp.sum(dla * (-LRU_C * r), axis=0, keepdims=True) * (-_sigmoid(-p_ref[7:8, :]))
        dzr = dr * r * (1.0 - r)
        dzi = di * i * (1.0 - i)
        st_ref[5:6, :] += jnp.sum(dzr, axis=0, keepdims=True)
        st_ref[6:7, :] += jnp.sum(dzi, axis=0, keepdims=True)
        dlxc = dlxc + _blockdiag_mm(dzr, waT_ref) + _blockdiag_mm(dzi, wxT_ref)
        for j in range(4):
            sl = slice(256 * j, 256 * (j + 1))
            pa = _mm_tn(lx[:, sl], dzr[:, sl])
            px = _mm_tn(lx[:, sl], dzi[:, sl])
            for b in range(4):
                bs = slice(BW * b, BW * (b + 1))
                dwa_ref[4 * j + b] += pa[bs, bs]
                dwx_ref[4 * j + b] += px[bs, bs]
        dp[0:TT, :] = dlxc
        acc = p_ref[0:1, :] * dp[pl.ds(CONV_K - 1, TT), :]
        for k in range(1, CONV_K):
            acc = acc + p_ref[k:k + 1, :] * dp[pl.ds(CONV_K - 1 - k, TT), :]
        dlx_ref[...] = acc.astype(BF)
        dp[TT:TT + 8, :] = dp[0:8, :]
        for k in range(CONV_K):
            st_ref[k:k + 1, :] += jnp.sum(dlxc * taps[k], axis=0, keepdims=True)
        st_ref[4:5, :] += jnp.sum(dlxc, axis=0, keepdims=True)

    w4 = _whole((4, 256, 256))
    return pl.pallas_call(
        body, name="lru_bwd", grid=(NT,),
        in_specs=[_rows(TT, LW, NT), _rows(TT, LW, NT), _halo(LW, NT, TT), _rows(TT, LW, NT), _rows(TT, LW, NT),
                  _halo(LW, NT, TT), _whole((16, LW)), w4, w4, w4, w4],
        out_specs=[_rows(TT, LW, NT), _rows(TT, LW, NT), _whole((16, LW)), _whole((NBLK, BW, BW)), _whole((NBLK, BW, BW))],
        out_shape=[S((T, LW), BF), S((T, LW), BF), S((16, LW), F32), S((NBLK, BW, BW), F32), S((NBLK, BW, BW), F32)],
        scratch_shapes=[pltpu.VMEM((TT + 8, LW), F32), pltpu.VMEM((TT + 8, LW), F32), pltpu.VMEM((TT + 8, LW), F32),
                        pltpu.VMEM((TT, LW), F32), pltpu.VMEM((TT, LW), F32), pltpu.VMEM((TT, LW), F32),
                        pltpu.VMEM((8, LW), F32)],
        compiler_params=_params(),
    )(dy, lxr, lxr, lg, h, h, p_lru, wa4, wx4, wa4T, wx4T)


def _ssd_bwd(dyn, xbcr, z, dtr, y, states, cw_ssd, hp_ssd, g_ssd, parts):
    T = dyn.shape[0]
    NC = T // CH
    nq = len(parts)

    def body(*refs):
        dyn_ref, xr_ref, xh_ref, z_ref, dt_ref, y_ref, st_ref, cw_ref, hp_ref, g_ref = refs[:10]
        q_in = refs[10:10 + nq]
        dxbc_ref, dz_ref, ddt_ref, cst_ref, hst_ref, gst_ref = refs[10 + nq:16 + nq]
        q_out = refs[16 + nq:16 + 2 * nq]
        xp, dp, dS, dxb, yo_s, q_s, dxs_s, t1_s, send_sems, recv_sems = refs[16 + 2 * nq:]
        first = pl.program_id(0) == 0
        top = pl.program_id(0) == NC - 1
        for phase, step in enumerate((0, NC - 1)):
            @pl.when(pl.program_id(0) == step)
            def _():
                _quad_phase(phase, q_in, q_out, send_sems, recv_sems)

        @pl.when(first)
        def _():
            cst_ref[...] = jnp.zeros_like(cst_ref)
            hst_ref[...] = jnp.zeros_like(hst_ref)
            gst_ref[...] = jnp.zeros_like(gst_ref)
            dp[CH:CH + 8, :] = jnp.zeros((8, XBC), F32)
            dS[...] = jnp.zeros_like(dS)

        xp[0:8, :] = xh_ref[...] * jnp.where(top, 0.0, 1.0)
        cv, sg, xbc, raw, dtv, A, cs, taps = _ssd_prep(xp, xr_ref, dt_ref, cw_ref, hp_ref)
        csT, dsm, E_x, ds_x, El_rows = _ssd_decays(cs)
        row_i = lax.broadcasted_iota(jnp.int32, (CH, CH), 0)
        col_i = lax.broadcasted_iota(jnp.int32, (CH, CH), 1)
        tril = row_i >= col_i
        first = col_i < HD
        head_of = ((lax.broadcasted_iota(jnp.int32, (DTP, SI), 1) >> 6)
                   == lax.broadcasted_iota(jnp.int32, (DTP, SI), 0)).astype(BF)
        head_ofT = ((lax.broadcasted_iota(jnp.int32, (SI, DTP), 0) >> 6)
                    == lax.broadcasted_iota(jnp.int32, (SI, DTP), 1)).astype(BF)

        def hi_lo(v):
            hi = v.astype(BF)
            return hi, (v - hi.astype(F32)).astype(BF)

        def lane_sums(v):
            hi, lo = hi_lo(v)
            return _mm(hi, head_ofT) + _mm(lo, head_ofT)

        zz = z_ref[...]
        sz = _sigmoid(zz)
        yv = y_ref[...]
        dgn, dg = _rms_bwd(yv * (zz * sz), g_ref[...], dyn_ref[...])
        gst_ref[0:1, :] += dg
        dz_ref[...] = (dgn * yv * (sz + zz * sz * (1.0 - sz))).astype(BF)
        dY = dgn * (zz * sz)

        X = xbc[:, 0:SI]
        dt_x = _per_head_lanes(dtv)
        xs = X * dt_x
        xsd = (xs * ds_x).astype(BF)
        D_x = _per_head_lanes(hp_ref[...])[2:3, :]
        dcs_col = jnp.zeros((CH, DTP), F32)
        dcs_row = jnp.zeros((CH, DTP), F32)
        GW = HPG * HD
        for g in range(NG):
            gs = slice(GW * g, GW * (g + 1))
            Bg = xbc[:, SI + NS * g:SI + NS * (g + 1)].astype(BF)
            Cg = xbc[:, SI + NG * NS + NS * g:SI + NG * NS + NS * (g + 1)].astype(BF)
            G = _mm_nt(Cg, Bg)
            Sg = st_ref[0, gs, :]
            dSe = dS[gs, :]
            dYg = dY[:, gs]
            yo_s[:, gs] = _mm_nt(Cg, Sg) * E_x[:, gs]
            dP = dYg * E_x[:, gs]
            dCg = _mm(dP, Sg)
            dS[gs, :] = _mm_tn(dP, Cg) + _per_head_rows(El_rows, g) * dSe
            t1_s[gs, :] = dSe * Sg
            Q = _mm_nt(Bg, dSe)
            q_s[:, gs] = Q
            dBg = _mm(xsd[:, gs], dSe)
            dG = jnp.zeros((CH, CH), F32)
            for jj in range(HPG // 2):
                j = g * (HPG // 2) + jj
                ps = slice(2 * HD * j, 2 * HD * (j + 1))
                xs_pair = xs[:, ps]
                dxs_pair = Q[:, 2 * HD * jj:2 * HD * (jj + 1)] * ds_x[:, ps]
                for e in range(2):
                    h = 2 * j + e
                    Lm = jnp.exp(jnp.where(tril, cs[:, h:h + 1] - csT[h:h + 1, :], -1e30))
                    M = G * Lm
                    dYm = jnp.where(first if e == 0 else ~first, dY[:, ps], 0.0).astype(BF)
                    dM = _mm_nt(dYm, xs_pair)
                    dxs_pair = dxs_pair + _mm_tn(M, dYm)
                    Wm = dM * M
                    dcs_col = dcs_col + jnp.where(col_i == h, jnp.sum(Wm, axis=1, keepdims=True), 0.0)
                    dcs_row = dcs_row + jnp.where(row_i == h, -jnp.sum(Wm, axis=0, keepdims=True), 0.0)
                    dG = dG + dM * Lm
                dxs_s[:, ps] = dxs_pair
            dxb[:, SI + NS * g:SI + NS * (g + 1)] = dBg + _mm_tn(dG, Cg)
            dxb[:, SI + NG * NS + NS * g:SI + NG * NS + NS * (g + 1)] = dCg + _mm(dG, Bg)

        dxs = dxs_s[...]
        dxb[:, 0:SI] = D_x * dY + dxs * dt_x
        dds = lane_sums(q_s[...] * xs) * dsm
        dcs_col = dcs_col + lane_sums(dY * yo_s[...]) - dds
        ddt_col = lane_sums(dxs * X)
        dD = jnp.sum(lane_sums(dY * X), axis=0, keepdims=True)
        t_hi, t_lo = hi_lo(t1_s[...])
        dcl_rows = jnp.sum(_mm(head_of, t_hi) + _mm(head_of, t_lo), axis=1, keepdims=True) * jnp.exp(csT[:, CH - 1:CH])
        dcs_row = dcs_row + jnp.where(col_i == CH - 1, dcl_rows, 0.0)
        dcs_col = dcs_col + jnp.where(row_i == CH - 1, jnp.sum(dds, axis=0, keepdims=True), 0.0)

        da = _rev_cumsum_rows(dcs_col + dcs_row.T, CH)
        ddt_col = ddt_col + da * A
        hst_ref[1:2, :] += jnp.sum(da * dtv, axis=0, keepdims=True) * A
        hst_ref[2:3, :] += dD
        draw = jnp.where(col_i < NH, ddt_col * _sigmoid(raw), 0.0)
        ddt_ref[...] = draw.astype(BF)
        hst_ref[0:1, :] += jnp.sum(draw, axis=0, keepdims=True)

        dcv = dxb[...] * (sg + cv * sg * (1.0 - sg))
        dp[0:CH, :] = dcv
        acc = cw_ref[0:1, :] * dp[pl.ds(CONV_K - 1, CH), :]
        for k in range(1, CONV_K):
            acc = acc + cw_ref[k:k + 1, :] * dp[pl.ds(CONV_K - 1 - k, CH), :]
        dxbc_ref[...] = acc.astype(BF)
        dp[CH:CH + 8, :] = dp[0:8, :]
        for k in range(CONV_K):
            cst_ref[k:k + 1, :] += jnp.sum(dcv * taps[k], axis=0, keepdims=True)
        cst_ref[4:5, :] += jnp.sum(dcv, axis=0, keepdims=True)

    return pl.pallas_call(
        body, name="ssd_bwd", grid=(NC,),
        in_specs=[_rows(CH, SI, NC), _rows(CH, XBC, NC), _halo(XBC, NC, CH), _rows(CH, SI, NC), _rows(CH, DTP, NC),
                  _rows(CH, SI, NC), pl.BlockSpec((1, NH * HD, NS), lambda i: (NC - 1 - i, 0, 0)),
                  _whole((8, XBC)), _whole((8, DTP)), _whole((1, SI))] + [ANY] * nq,
        out_specs=[_rows(CH, XBC, NC), _rows(CH, SI, NC), _rows(CH, DTP, NC), _whole((16, XBC)), _whole((16, DTP)),
                   _whole((8, SI))] + [ANY] * nq,
        out_shape=[S((T, XBC), BF), S((T, SI), BF), S((T, DTP), BF), S((16, XBC), F32), S((16, DTP), F32), S((8, SI), F32)]
        + [S(p.shape, p.dtype) for p in parts],
        scratch_shapes=[pltpu.VMEM((CH + 8, XBC), F32), pltpu.VMEM((CH + 8, XBC), F32), pltpu.VMEM((NH * HD, NS), F32),
                        pltpu.VMEM((CH, XBC), F32), pltpu.VMEM((CH, SI), F32), pltpu.VMEM((CH, SI), F32),
                        pltpu.VMEM((CH, SI), F32), pltpu.VMEM((NH * HD, NS), F32),
                        pltpu.SemaphoreType.DMA((3 * nq,)), pltpu.SemaphoreType.DMA((3 * nq,))],
        compiler_params=_params(),
    )(dyn, xbcr, xbcr, z, dtr, y, states, cw_ssd, hp_ssd, g_ssd, *parts)


def _inproj_bwd(dlx, dlg, dz, dxbc, ddt, x, dx1, wcatT, g0):
    T = x.shape[0]

    def body(dlx_ref, dlg_ref, dz_ref, dxbc_ref, ddt_ref, x_ref, dx1_ref, w_hbm, g_ref, dx_ref, st_ref, w_vm, sem):
        _load_once([(w_hbm, w_vm)], sem)

        @pl.when(pl.program_id(0) == 0)
        def _():
            st_ref[...] = jnp.zeros_like(st_ref)

        dh = jnp.dot(dlx_ref[...], w_vm[0:1024, :], preferred_element_type=F32)
        dh = dh + jnp.dot(dlg_ref[...], w_vm[1024:2048, :], preferred_element_type=F32)
        dh = dh + jnp.dot(dz_ref[...], w_vm[2048:3072, :], preferred_element_type=F32)
        dh = dh + jnp.dot(dxbc_ref[...], w_vm[3072:3072 + XBC, :], preferred_element_type=F32)
        dh = dh + jnp.dot(ddt_ref[...], w_vm[3072 + XBC:PC, :], preferred_element_type=F32)
        dx, dg = _rms_bwd(x_ref[...], g_ref[...], dh)
        dx_ref[...] = dx1_ref[...] + dx
        st_ref[0:1, :] += dg

    return pl.pallas_call(
        body, name="inproj_bwd", grid=(T // TT,),
        in_specs=[_rows(TT, 1024), _rows(TT, 1024), _rows(TT, 1024), _rows(TT, XBC), _rows(TT, DTP), _rows(TT, D),
                  _rows(TT, D), ANY, _whole((1, D))],
        out_specs=[_rows(TT, D), _whole((8, D))],
        out_shape=[S((T, D), F32), S((8, D), F32)],
        scratch_shapes=[pltpu.VMEM((PC, D), BF), pltpu.SemaphoreType.DMA((1,))],
        compiler_params=_params(),
    )(dlx, dlg, dz, dxbc, ddt, x, dx1, wcatT, g0)


def _wgrad(name, a, b):
    T, M = a.shape
    N = b.shape[1]
    tk = min(T, 2048 if M <= 1024 else 1024)
    tn = N
    while M * tn * 4 > (6 << 20) and tn % 256 == 0:
        tn //= 2

    def body(a_ref, b_ref, o_ref):
        p = lax.dot_general(a_ref[...], b_ref[...], (((0,), (0,)), ((), ())), preferred_element_type=F32)

        @pl.when(pl.program_id(1) == 0)
        def _():
            o_ref[...] = p

        @pl.when(pl.program_id(1) > 0)
        def _():
            o_ref[...] += p

    return pl.pallas_call(
        body, name=name, grid=(N // tn, T // tk),
        in_specs=[pl.BlockSpec((tk, M), lambda j, k: (k, 0)), pl.BlockSpec((tk, tn), lambda j, k: (k, j))],
        out_specs=pl.BlockSpec((M, tn), lambda j, k: (0, j)), out_shape=S((M, N), F32),
        compiler_params=_params(2),
    )(a, b)


def _adamw(name, w, g, m, v):
    _, R, C = w.shape
    tr = _row_tile(R, C)

    def body(w_ref, g_ref, m_ref, v_ref, d_ref, nm_ref, nv_ref):
        d_ref[0], nm_ref[0], nv_ref[0] = _adam_math(w_ref[0], g_ref[...], m_ref[0], v_ref[0])

    blk = pl.BlockSpec((1, tr, C), lambda i: (0, i, 0))
    return pl.pallas_call(
        body, name=name, grid=(R // tr,),
        in_specs=[blk, pl.BlockSpec((tr, C), lambda i: (i, 0)), blk, blk], out_specs=[blk] * 3,
        out_shape=[S((1, R, C), F32)] * 3, compiler_params=_params(),
    )(w, g, m, v)


def _half(ref, c, hr):
    sl = pl.ds(pl.multiple_of(c * hr, 8), hr)
    return ref.at[:, sl, :] if len(ref.shape) == 3 else ref.at[sl, :]


def _allgather_weights(shards):
    n = len(shards)

    def body(*refs):
        for phase in range(3):
            _gather_phase(phase, refs[:n], refs[n:2 * n], refs[2 * n], refs[2 * n + 1])

    return pl.pallas_call(
        body, name="allgather_weights", in_specs=[ANY] * n, out_specs=[ANY] * n,
        out_shape=[S((4,) + s.shape, s.dtype) for s in shards],
        scratch_shapes=[pltpu.SemaphoreType.DMA((6 * n,)), pltpu.SemaphoreType.DMA((6 * n,))],
    )(*shards)


def _pair_exchange(name, bufs):
    n = len(bufs)

    def half_shape(b):
        return b.shape[:-2] + (b.shape[-2] // 2, b.shape[-1])

    def body(*refs):
        ins, outs = refs[:n], refs[n:2 * n]
        send_sems, recv_sems = refs[2 * n], refs[2 * n + 1]
        x, y, c = _pos()
        copies = [_remote(_half(src, 1 - c, src.shape[-2] // 2), dst, send_sems.at[k], recv_sems.at[k], (x, y, 1 - c))
                  for k, (src, dst) in enumerate(zip(ins, outs))]
        for cp in copies:
            cp.start()
        for cp in copies:
            cp.wait()

    return pl.pallas_call(
        body, name=name, in_specs=[ANY] * n, out_specs=[ANY] * n,
        out_shape=[S(half_shape(b), b.dtype) for b in bufs],
        scratch_shapes=[pltpu.SemaphoreType.DMA((n,)), pltpu.SemaphoreType.DMA((n,))],
    )(*bufs)


def _quad_exchange(bufs, scatter):
    n = len(bufs)

    def body(*refs):
        ins, outs = refs[:n], refs[n:2 * n]
        send_sems, recv_sems, local_sems = refs[2 * n], refs[2 * n + 1], refs[2 * n + 2]
        x, y, c = _pos()
        me = 2 * x + y
        chips = _other_chips(x, y)
        copies, locals_ = [], []
        for k, (src, dst) in enumerate(zip(ins, outs)):
            if not scatter[k]:
                own = pltpu.make_async_copy(src, dst.at[me], local_sems.at[k])
                own.start()
                locals_.append(own)
            for j, (cx, cy) in enumerate(chips):
                piece = src.at[2 * cx + cy] if scatter[k] else src
                cp = _remote(piece, dst.at[me], send_sems.at[3 * k + j], recv_sems.at[3 * k + j], (cx, cy, c))
                cp.start()
                copies.append(cp)
        for k, (src, dst) in enumerate(zip(ins, outs)):
            for j, (cx, cy) in enumerate(chips):
                blk = dst.at[2 * cx + cy]
                _remote(blk, blk, send_sems.at[3 * k + j], recv_sems.at[3 * k + j], (cx, cy, c)).wait_recv()
        for cp in copies:
            cp.wait_send()
        for cp in locals_:
            cp.wait()

    return pl.pallas_call(
        body, name="quad_exchange", in_specs=[ANY] * n, out_specs=[ANY] * n,
        out_shape=[S((4,) + (b.shape[1:] if sc else b.shape), b.dtype) for b, sc in zip(bufs, scatter)],
        scratch_shapes=[pltpu.SemaphoreType.DMA((3 * n,)), pltpu.SemaphoreType.DMA((3 * n,)), pltpu.SemaphoreType.DMA((n,))],
    )(*bufs)


def _pair_gather(bufs):
    n = len(bufs)

    def body(*refs):
        ins, outs = refs[:n], refs[n:2 * n]
        send_sems, recv_sems = refs[2 * n], refs[2 * n + 1]
        x, y, c = _pos()
        copies = []
        for k, buf in enumerate(outs):
            mine = _half(buf, c, buf.shape[0] // 2)
            cp = _remote(mine, mine, send_sems.at[k], recv_sems.at[k], (x, y, 1 - c))
            cp.start()
            copies.append(cp)
        for k, buf in enumerate(outs):
            theirs = _half(buf, 1 - c, buf.shape[0] // 2)
            _remote(theirs, theirs, send_sems.at[k], recv_sems.at[k], (x, y, 1 - c)).wait_recv()
        for cp in copies:
            cp.wait_send()

    return pl.pallas_call(
        body, name="pair_gather", in_specs=[ANY] * n, out_specs=[ANY] * n,
        out_shape=[S(b.shape, b.dtype) for b in bufs], input_output_aliases={k: k for k in range(n)},
        scratch_shapes=[pltpu.SemaphoreType.DMA((n,)), pltpu.SemaphoreType.DMA((n,))],
    )(*bufs)


def _row_tile(rows, cols, mult=8):
    best = mult
    for t in range(mult, rows + 1, mult):
        if rows % t == 0 and t * cols * 4 <= (1 << 20):
            best = t
    return best


def _add_own_half(name, full, got, c, out_dtype, by_columns):
    hr = got.shape[-2]
    wide = got.shape[-1]
    cols = wide // 4 if by_columns else wide
    tr = _row_tile(hr, wide, 16)
    per = hr // tr

    if by_columns:
        def body(c_ref, a_ref, b_ref, o_ref):
            v = a_ref[...] + b_ref[...]
            for j in range(4):
                o_ref[j] = v[:, j * cols:(j + 1) * cols].astype(out_dtype)

        in_specs = [pl.BlockSpec((tr, wide), lambda i, c_ref: (c_ref[0] * per + i, 0)),
                    pl.BlockSpec((tr, wide), lambda i, c_ref: (i, 0))]
        out_specs = pl.BlockSpec((4, tr, cols), lambda i, c_ref: (0, i, 0))
        grid = (per,)
    else:
        def body(c_ref, a_ref, b_ref, o_ref):
            o_ref[...] = (a_ref[...] + b_ref[...]).astype(out_dtype)

        in_specs = [pl.BlockSpec((1, tr, cols), lambda s, i, c_ref: (s, c_ref[0] * per + i, 0)),
                    pl.BlockSpec((1, tr, cols), lambda s, i, c_ref: (s, i, 0))]
        out_specs = pl.BlockSpec((1, tr, cols), lambda s, i, c_ref: (s, i, 0))
        grid = (4, per)
    return pl.pallas_call(
        body, name=name,
        grid_spec=pltpu.PrefetchScalarGridSpec(num_scalar_prefetch=1, grid=grid, in_specs=in_specs, out_specs=out_specs),
        out_shape=S((4, hr, cols), out_dtype), compiler_params=_params(len(grid)),
    )(jnp.reshape(c, (1,)).astype(jnp.int32), full, got)


def _small_add_own_half(fulls, gots, c):
    n = len(fulls)

    def body(c_ref, *refs):
        for a_ref, b_ref, o_ref in zip(refs[:n], refs[n:2 * n], refs[2 * n:]):
            hr = b_ref.shape[0]
            o_ref[...] = a_ref[pl.ds(pl.multiple_of(c_ref[0] * hr, 8), hr), :] + b_ref[...]

    specs = lambda arrs: [pl.BlockSpec(a.shape, lambda i, c_ref: (0, 0)) for a in arrs]
    return pl.pallas_call(
        body, name="small_pair_add",
        grid_spec=pltpu.PrefetchScalarGridSpec(num_scalar_prefetch=1, grid=(1,), in_specs=specs(fulls) + specs(gots),
                                               out_specs=specs(gots)),
        out_shape=[S(g.shape, F32) for g in gots], compiler_params=_params(),
    )(jnp.reshape(c, (1,)).astype(jnp.int32), *fulls, *gots)


def _small_sum_slots(slots, c):
    n = len(slots)

    def body(c_ref, *refs):
        for s_ref, o_ref in zip(refs[:n], refs[n:]):
            hr = s_ref.shape[1]
            o_ref[pl.ds(pl.multiple_of(c_ref[0] * hr, 8), hr), :] = ((s_ref[0] + s_ref[1]) + s_ref[2]) + s_ref[3]

    outs = [S((2 * s.shape[1], s.shape[2]), F32) for s in slots]
    return pl.pallas_call(
        body, name="small_quad_sum",
        grid_spec=pltpu.PrefetchScalarGridSpec(
            num_scalar_prefetch=1, grid=(1,),
            in_specs=[pl.BlockSpec(s.shape, lambda i, c_ref: (0, 0, 0)) for s in slots],
            out_specs=[pl.BlockSpec(o.shape, lambda i, c_ref: (0, 0)) for o in outs]),
        out_shape=outs, compiler_params=_params(),
    )(jnp.reshape(c, (1,)).astype(jnp.int32), *slots)


def _sum_slots(name, own, slots, me, c):
    _, rows, cols = slots.shape
    tr = _row_tile(rows, cols, 16 if slots.dtype == jnp.bfloat16 else 8)
    per = rows // tr
    three = len(own.shape) == 3

    def body(p_ref, own_ref, s0, s1, s2, s3, o_ref):
        mine = own_ref[0] if three else own_ref[...]
        acc = None
        for j, s_ref in enumerate((s0, s1, s2, s3)):
            v = jnp.where(p_ref[0] == j, mine, s_ref[0]).astype(F32)
            acc = v if acc is None else acc + v
        o_ref[...] = acc

    def slot_spec(j):
        return pl.BlockSpec((1, tr, cols), lambda i, p: (jnp.where(p[0] == j, (j + 1) % 4, j), i, 0))

    own_spec = (pl.BlockSpec((1, tr, cols), lambda i, p: (p[0], i, 0)) if three
                else pl.BlockSpec((tr, cols), lambda i, p: (i, 0)))
    return pl.pallas_call(
        body, name=name,
        grid_spec=pltpu.PrefetchScalarGridSpec(
            num_scalar_prefetch=1, grid=(per,), in_specs=[own_spec] + [slot_spec(j) for j in range(4)],
            out_specs=pl.BlockSpec((tr, cols), lambda i, p: (p[1] * per + i, 0))),
        out_shape=S((2 * rows, cols), F32), compiler_params=_params(),
    )(jnp.stack([me, c]).astype(jnp.int32), own, slots, slots, slots, slots)


BIG = ("w_in", "w_out", "w_gate", "w_up", "w_down")
ROW_PARAMS = (("pre_mix_norm", 0), ("lru_conv_b", 12), ("lru_ba", 13), ("lru_bx", 14), ("lru_lambda", 15),
              ("lru_out_norm", 16), ("ssd_out_norm", 24), ("post_mix_norm", 33), ("pre_ffn_norm", 32), ("post_ffn_norm", 41))
LRU_CONV_ROWS = (8, 12)
LOSS_ROW = 40
HEAD_PARAMS = (("ssd_dt_bias", 0), ("ssd_a_log", 1), ("ssd_d", 2))
SMALL = tuple(n for n, _ in ROW_PARAMS) + ("ssd_conv_b",) + tuple(n for n, _ in HEAD_PARAMS) + (
    "lru_wa", "lru_wx", "lru_conv_w", "ssd_conv_w")


def _diag4(w):
    eye = jnp.eye(4, dtype=w.dtype).reshape(1, 4, 1, 4, 1)
    return (w.reshape(4, 4, BW, 1, BW) * eye).reshape(4, 4 * BW, 4 * BW)


def _adam_math(w, g, m, v):
    mm = ADAM_B1 * m + (1.0 - ADAM_B1) * g
    vv = ADAM_B2 * v + (1.0 - ADAM_B2) * (g * g)
    c1 = 1.0 - ADAM_B1 ** ADAM_STEP
    c2 = 1.0 - ADAM_B2 ** ADAM_STEP
    return -ADAM_LR * ((mm / c1) / (jnp.sqrt(vv / c2) + ADAM_EPS) + ADAM_WD * w), mm, vv


def _adamw_small(rows, cst, hst, dwa, dwx, glcw, gscw, w, m, v):
    def grad_of(name, refs):
        rows_ref, cst_ref, hst_ref, dwa_ref, dwx_ref, glcw_ref, gscw_ref = refs
        for n, r in ROW_PARAMS:
            if n == name:
                return rows_ref[r:r + 1, :]
        for n, r in HEAD_PARAMS:
            if n == name:
                return hst_ref[r:r + 1, 0:NH]
        return {"ssd_conv_b": lambda: cst_ref[4:5, :], "lru_wa": lambda: dwa_ref[...], "lru_wx": lambda: dwx_ref[...],
                "lru_conv_w": lambda: glcw_ref[...], "ssd_conv_w": lambda: gscw_ref[...]}[name]()

    shapes = {n: (w[n].shape[1:] if len(w[n].shape) > 2 else w[n].shape) for n in SMALL}
    flat = lambda d: [d[n].reshape(shapes[n]) for n in SMALL]
    ns = len(SMALL)

    def body(*refs):
        srcs, rest = refs[:7], refs[7:]
        w_refs, m_refs, v_refs = rest[:ns], rest[ns:2 * ns], rest[2 * ns:3 * ns]
        outs = rest[3 * ns:]
        for k, name in enumerate(SMALL):
            g = grad_of(name, srcs)
            d, mm, vv = _adam_math(w_refs[k][...], g, m_refs[k][...], v_refs[k][...])
            outs[4 * k][...] = g
            outs[4 * k + 1][...] = d
            outs[4 * k + 2][...] = mm
            outs[4 * k + 3][...] = vv

    res = pl.pallas_call(
        body, name="adamw_small",
        out_shape=[S(shapes[n], F32) for n in SMALL for _ in range(4)],
        compiler_params=pltpu.CompilerParams(vmem_limit_bytes=VMEM_LIMIT),
    )(rows, cst, hst, dwa, dwx, glcw, gscw, *flat(w), *flat(m), *flat(v))
    return {n: tuple(res[4 * k + i].reshape(w[n].shape) for i in range(4)) for k, n in enumerate(SMALL)}


def _with_own(own, got):
    chip = 2 * lax.axis_index("x") + lax.axis_index("y")
    return jnp.where((jnp.arange(4) == chip).reshape(4, 1, 1), own[None], got)


def _side_by_side(f):
    return f.transpose(1, 0, 2).reshape(f.shape[1], 4 * f.shape[2])


def _stacked(f):
    return f.reshape(4 * f.shape[1], f.shape[2])


def _gather_first_weights(w_in, lru_conv_w, ssd_conv_w):
    conv = jnp.concatenate([lru_conv_w.reshape(-1), ssd_conv_w.reshape(-1)]).astype(F32)
    hi = conv.astype(jnp.bfloat16)
    mid = (conv - hi.astype(F32)).astype(jnp.bfloat16)
    lo = (conv - hi.astype(F32) - mid.astype(F32)).astype(jnp.bfloat16)
    terms = jnp.concatenate([hi, mid, lo])
    n_terms = terms.shape[0]
    conv_rows = -(-n_terms // (128 * 32)) * 32
    terms = jnp.pad(terms, (0, conv_rows * 128 - n_terms)).reshape(conv_rows, 128)
    own = [w_in.astype(WIRE), terms]
    got = _allgather_weights(own)
    win_f = _side_by_side(_with_own(own[0], got[0]))
    t3 = _with_own(own[1], got[1]).reshape(4, -1)[:, :n_terms].reshape(4, 3, -1).astype(F32)
    conv_f = (t3[:, 0] + t3[:, 1]) + t3[:, 2]
    n1 = lru_conv_w.size
    lcw = conv_f[:, :n1].reshape(4, CONV_K, -1).transpose(1, 0, 2).reshape(CONV_K, LW)
    scw = conv_f[:, n1:].reshape(4, CONV_K, -1).transpose(1, 0, 2).reshape(CONV_K, XBC)
    return win_f, lcw, scw


def _pair_stage(tag, bufs, by_columns, small, c):
    nb = len(bufs)
    got = list(_pair_exchange("pair_exchange_" + tag, list(bufs) + list(small)))
    part = [_add_own_half("pair_add_%s%d" % (tag, k), b, r, c, WIRE, bc)
            for k, (b, r, bc) in enumerate(zip(bufs, got[:nb], by_columns))]
    part_small = list(_small_add_own_half(list(small), got[nb:], c)) if small else []
    return part, part_small


def _step(x, tgt, win_f, lcw, scw, sp, late):
    c = lax.axis_index("c")
    me = 2 * lax.axis_index("x") + lax.axis_index("y")
    mm = lambda w: w.astype(BF)
    wcat = jnp.concatenate([mm(win_f), jnp.zeros((D, PC - IN_COLS), BF)], axis=1)
    row = lambda v: v.reshape(1, -1).astype(F32)
    p_lru = jnp.concatenate([lcw, row(sp["lru_conv_b"]), row(sp["lru_ba"]), row(sp["lru_bx"]), row(sp["lru_lambda"]),
                             row(sp["lru_out_norm"]), jnp.zeros((7, LW), F32)], axis=0)
    wa4, wx4 = mm(_diag4(sp["lru_wa"][0])), mm(_diag4(sp["lru_wx"][0]))
    wa4T, wx4T = wa4.transpose(0, 2, 1), wx4.transpose(0, 2, 1)
    cw_ssd = jnp.concatenate([scw, row(sp["ssd_conv_b"]), jnp.zeros((3, XBC), F32)], axis=0)
    padh = lambda v: jnp.pad(row(v), ((0, 0), (0, DTP - NH)))
    hp_ssd = jnp.concatenate([padh(sp["ssd_dt_bias"]), padh(sp["ssd_a_log"]), padh(sp["ssd_d"]), jnp.zeros((5, DTP), F32)], axis=0)
    g0, g_ssd = row(sp["pre_mix_norm"]), row(sp["ssd_out_norm"])
    g_pm, g_pf, g_pff = row(sp["post_mix_norm"]), row(sp["pre_ffn_norm"]), row(sp["post_ffn_norm"])

    h0, lxr, lg, z, xbcr, dtr = _inproj(x, g0, wcat)
    h, ylru, *got_a = _lru_fwd(lxr, lg, p_lru, wa4, wx4, [late[0], late[3]])
    y, yssd, states, *got_b = _ssd_fwd(xbcr, z, dtr, cw_ssd, hp_ssd, g_ssd, [late[1], late[2]])
    wout, wd = mm(_stacked(_with_own(late[0], got_a[0]))), mm(_stacked(_with_own(late[3], got_a[1])))
    wg, wu = mm(_side_by_side(_with_own(late[1], got_b[0]))), mm(_side_by_side(_with_own(late[2], got_b[1])))
    mix, x1, h2 = _outproj(ylru, yssd, x, wout, g_pm, g_pf)
    gate, up, act, df, dx2, st_ffn = _ffn_fwd(h2, x1, tgt, wg, wu, wd, g_pff)
    dgate, dup, dh2 = _ffn_bwd(df, gate, up, wd.T, wg.T, wu.T)
    dx1, dmix, dyl, dys, st_mix = _mix_bwd(dh2, x1, dx2, mix, wout.T, g_pf, g_pm)

    dwg = _wgrad("wgrad_gate", h2, dgate)
    dwu = _wgrad("wgrad_up", h2, dup)
    dwd = _wgrad("wgrad_down", act, df)
    dwo = jnp.concatenate([_wgrad("wgrad_out_lru", ylru, dmix), _wgrad("wgrad_out_ssd", yssd, dmix)], axis=0)
    early = [dwo.reshape(4, (LW + SI) // 4, D), dwg, dwu, dwd.reshape(4, DFF // 4, D)]
    part_early, _ = _pair_stage("early", early, [False, True, True, False], [], c)

    dlx, dlg, st_lru, dwa, dwx = _lru_bwd(dyl, lxr, lg, h, p_lru, wa4, wx4, wa4T, wx4T)
    dxbc, dz, ddt, cst, hst, gst, *slots_early = _ssd_bwd(dys, xbcr, z, dtr, y, states, cw_ssd, hp_ssd, g_ssd, part_early)
    gx, st_in = _inproj_bwd(dlx, dlg, dz, dxbc, ddt, x, dx1, wcat.T, g0)
    red_early = [_sum_slots("quad_sum_early%d" % k, p, s, me, c) for k, (p, s) in enumerate(zip(part_early, slots_early))]

    pin = [_wgrad("wgrad_in_%d" % k, h0, b) for k, b in enumerate((dlx, dlg, dz, dxbc, ddt))]
    dwin = jnp.concatenate(pin[:4] + [pin[4][:, :NH]], axis=1)
    rows = jnp.concatenate([st_in, st_lru, gst, st_mix, st_ffn], axis=0)
    small = [rows, cst, hst, dwa.reshape(NBLK * BW, BW), dwx.reshape(NBLK * BW, BW)]
    part, part_small = _pair_stage("late", [dwin], [True], small, c)
    slots = list(_quad_exchange(part + part_small, [True] + [False] * len(small)))
    red = [_sum_slots("quad_sum_late", part[0], slots[0], me, c)]
    red_small = list(_small_sum_slots(slots[1:], c))
    out = list(_pair_gather(red + red_early + red_small))
    big = dict(zip(("w_in", "w_out", "w_gate", "w_up", "w_down"), out[:5]))
    return gx, big, out[5:]


def kernel(x, pre_mix_norm, w_in, lru_conv_w, lru_conv_b, lru_wa, lru_ba, lru_wx, lru_bx, lru_lambda, lru_out_norm, ssd_conv_w, ssd_conv_b, ssd_dt_bias, ssd_a_log, ssd_d, ssd_out_norm, w_out, post_mix_norm, pre_ffn_norm, w_gate, w_up, w_down, post_ffn_norm, loss_target, m_pre_mix_norm, m_w_in, m_lru_conv_w, m_lru_conv_b, m_lru_wa, m_lru_ba, m_lru_wx, m_lru_bx, m_lru_lambda, m_lru_out_norm, m_ssd_conv_w, m_ssd_conv_b, m_ssd_dt_bias, m_ssd_a_log, m_ssd_d, m_ssd_out_norm, m_w_out, m_post_mix_norm, m_pre_ffn_norm, m_w_gate, m_w_up, m_w_down, m_post_ffn_norm, v_pre_mix_norm, v_w_in, v_lru_conv_w, v_lru_conv_b, v_lru_wa, v_lru_ba, v_lru_wx, v_lru_bx, v_lru_lambda, v_lru_out_norm, v_ssd_conv_w, v_ssd_conv_b, v_ssd_dt_bias, v_ssd_a_log, v_ssd_d, v_ssd_out_norm, v_w_out, v_post_mix_norm, v_pre_ffn_norm, v_w_gate, v_w_up, v_w_down, v_post_ffn_norm):
    args = dict(locals())
    names = list(SMALL) + list(BIG)
    w = {n: args[n] for n in names}
    m = {n: args["m_" + n] for n in names}
    v = {n: args["v_" + n] for n in names}
    chip = 2 * lax.axis_index("x") + lax.axis_index("y")

    win_f, lcw, scw = _gather_first_weights(w_in[0], lru_conv_w[0], ssd_conv_w[0])
    late = [a[0].astype(WIRE) for a in (w_out, w_gate, w_up, w_down)]
    gx, red, (rows, cst, hst, dwa, dwx) = _step(x[0], loss_target[0], win_f, lcw, scw, {n: w[n] for n in SMALL}, late)
    loss = jnp.sum(rows[LOSS_ROW])

    grads, delta, new_m, new_v = {}, {}, {}, {}
    for n in BIG:
        g = red[n]
        delta[n], new_m[n], new_v[n] = _adamw("adamw_" + n, w[n], g, m[n], v[n])
        grads[n] = g[None]

    lc, sc = lru_conv_w.shape[-1], ssd_conv_w.shape[-1]
    glcw = lax.dynamic_slice_in_dim(rows[LRU_CONV_ROWS[0]:LRU_CONV_ROWS[1]], chip * lc, lc, axis=1)
    gscw = lax.dynamic_slice_in_dim(cst[0:CONV_K], chip * sc, sc, axis=1)
    res = _adamw_small(rows, cst, hst, dwa.reshape(NBLK, BW, BW), dwx.reshape(NBLK, BW, BW), glcw, gscw,
                       {n: w[n] for n in SMALL}, {n: m[n] for n in SMALL}, {n: v[n] for n in SMALL})
    for n in SMALL:
        grads[n], delta[n], new_m[n], new_v[n] = res[n]

    order = ["pre_mix_norm", "w_in", "lru_conv_w", "lru_conv_b", "lru_wa", "lru_ba", "lru_wx", "lru_bx", "lru_lambda",
             "lru_out_norm", "ssd_conv_w", "ssd_conv_b", "ssd_dt_bias", "ssd_a_log", "ssd_d", "ssd_out_norm", "w_out",
             "post_mix_norm", "pre_ffn_norm", "w_gate", "w_up", "w_down", "post_ffn_norm"]
    return (loss, gx[None], *[grads[n] for n in order], *[delta[n] for n in order],
            *[new_m[n] for n in order], *[new_v[n] for n in order])
```

```python
import functools

import jax
import jax.numpy as jnp
from jax import lax
from jax.experimental import pallas as pl
from jax.experimental.pallas import tpu as pltpu

F32 = jnp.float32
BF = jnp.bfloat16

D = 1024
LW = 1024
NBLK = 16
BW = 64
SI = 1024
NH = 16
HD = 64
NG = 2
HPG = NH // NG
NS = 128
CH = 128
XBC = SI + 2 * NG * NS
DTP = 128
PC = 3 * 1024 + XBC + DTP
DFF = 2816
IN_COLS = 4624
EPS = 1e-6
LRU_C = 8.0
CONV_K = 4
TT = 256
VMEM_LIMIT = 56 * 1024 * 1024

ADAM_LR, ADAM_B1, ADAM_B2, ADAM_EPS, ADAM_WD, ADAM_STEP = 0.001, 0.9, 0.999, 1e-08, 0.01, 10

MESH = pl.DeviceIdType.MESH


def _mm(a, b):
    return jnp.dot(a.astype(BF), b.astype(BF), preferred_element_type=F32)


def _mm_nt(a, b):
    return lax.dot_general(a.astype(BF), b.astype(BF), (((1,), (1,)), ((), ())), preferred_element_type=F32)


def _mm_tn(a, b):
    return lax.dot_general(a.astype(BF), b.astype(BF), (((0,), (0,)), ((), ())), preferred_element_type=F32)


def _sigmoid(x):
    return 0.5 * jnp.tanh(0.5 * x) + 0.5


def _softplus(x):
    return jnp.maximum(x, 0.0) + jnp.log1p(jnp.exp(-jnp.abs(x)))


_GELU_C = 0.7978845608028654
_GELU_K = 0.044715


def _gelu(x):
    t = jnp.tanh(_GELU_C * (x + _GELU_K * x * x * x))
    return 0.5 * x * (1.0 + t)


def _gelu_grad(x):
    t = jnp.tanh(_GELU_C * (x + _GELU_K * x * x * x))
    return 0.5 * (1.0 + t) + 0.5 * x * (1.0 - t * t) * _GELU_C * (1.0 + 3.0 * _GELU_K * x * x)


def _rms_fwd(x, g):
    r = lax.rsqrt(jnp.mean(x * x, axis=-1, keepdims=True) + EPS)
    return x * r * g


def _rms_bwd(x, g, dy):
    r = lax.rsqrt(jnp.mean(x * x, axis=-1, keepdims=True) + EPS)
    xh = x * r
    dxh = dy * g
    dg = jnp.sum(dy * xh, axis=0, keepdims=True)
    dx = r * (dxh - xh * jnp.mean(dxh * xh, axis=-1, keepdims=True))
    return dx, dg


def _sum_all(x):
    return jnp.sum(jnp.sum(x, axis=1, keepdims=True), axis=0, keepdims=True)


def _cumsum_rows(x, n):
    row = lax.broadcasted_iota(jnp.int32, x.shape, 0)
    k = 1
    while k < n:
        x = x + jnp.where(row >= k, pltpu.roll(x, k, 0), 0.0)
        k *= 2
    return x


def _rev_cumsum_rows(x, n):
    row = lax.broadcasted_iota(jnp.int32, x.shape, 0)
    k = 1
    while k < n:
        x = x + jnp.where(row < n - k, pltpu.roll(x, n - k, 0), 0.0)
        k *= 2
    return x


def _load_once(pairs, sem):
    @pl.when(pl.program_id(0) == 0)
    def _():
        for k, (src, dst) in enumerate(pairs):
            pltpu.make_async_copy(src, dst, sem.at[k]).start()
        for k, (src, dst) in enumerate(pairs):
            pltpu.make_async_copy(src, dst, sem.at[k]).wait()


def _params(n_axes=1):
    return pltpu.CompilerParams(dimension_semantics=("arbitrary",) * n_axes, vmem_limit_bytes=VMEM_LIMIT)


def _rows(n, width, rev_of=None):
    if rev_of is None:
        return pl.BlockSpec((n, width), lambda i: (i, 0))
    return pl.BlockSpec((n, width), lambda i: (rev_of - 1 - i, 0))


def _whole(shape):
    nd = len(shape)
    return pl.BlockSpec(shape, lambda i: (0,) * nd)


ANY = pl.BlockSpec(memory_space=pl.ANY)
S = jax.ShapeDtypeStruct
WIRE = jnp.bfloat16


def _pos():
    return lax.axis_index("x"), lax.axis_index("y"), lax.axis_index("c")


def _other_chips(x, y):
    return [(1 - x, y), (x, 1 - y), (1 - x, 1 - y)]


def _remote(src, dst, send_sem, recv_sem, to):
    return pltpu.make_async_remote_copy(src_ref=src, dst_ref=dst, send_sem=send_sem, recv_sem=recv_sem,
                                        device_id=to, device_id_type=MESH)


def _gather_phase(phase, ins, outs, send_sems, recv_sems):
    x, y, c = _pos()
    me = 2 * x + y
    chips = _other_chips(x, y)
    for i, (src, dst) in enumerate(zip(ins, outs)):
        hr = src.shape[0] // 2
        my_half = pl.ds(pl.multiple_of(c * hr, 16), hr)
        sib_half = pl.ds(pl.multiple_of((1 - c) * hr, 16), hr)
        for k, (cx, cy) in enumerate(chips):
            s1, r1 = send_sems.at[6 * i + k], recv_sems.at[6 * i + k]
            s2, r2 = send_sems.at[6 * i + 3 + k], recv_sems.at[6 * i + 3 + k]
            first = lambda: _remote(src.at[my_half, :], dst.at[me, my_half, :], s1, r1, (cx, cy, c))
            landed = dst.at[2 * cx + cy, my_half, :]
            passed = lambda: _remote(landed, landed, s2, r2, (x, y, 1 - c))
            if phase == 0:
                first().start()
            elif phase == 1:
                _remote(landed, landed, s1, r1, (cx, cy, c)).wait_recv()
                passed().start()
            else:
                theirs = dst.at[2 * cx + cy, sib_half, :]
                _remote(theirs, theirs, s2, r2, (x, y, 1 - c)).wait_recv()
                first().wait_send()
                passed().wait_send()


def _quad_phase(phase, ins, outs, send_sems, recv_sems):
    x, y, c = _pos()
    me = 2 * x + y
    for i, (src, dst) in enumerate(zip(ins, outs)):
        for k, (cx, cy) in enumerate(_other_chips(x, y)):
            cp = _remote(src.at[2 * cx + cy], dst.at[me], send_sems.at[3 * i + k], recv_sems.at[3 * i + k], (cx, cy, c))
            if phase == 0:
                cp.start()
            else:
                got = dst.at[2 * cx + cy]
                _remote(got, got, send_sems.at[3 * i + k], recv_sems.at[3 * i + k], (cx, cy, c)).wait_recv()
                cp.wait_send()


def _inproj(x, g0, wcat):
    T = x.shape[0]

    def body(x_ref, g_ref, w_hbm, h0_ref, lx_ref, lg_ref, z_ref, xbc_ref, dt_ref, w_vm, sem):
        _load_once([(w_hbm, w_vm)], sem)
        h = _rms_fwd(x_ref[...], g_ref[...]).astype(BF)
        h0_ref[...] = h
        lx_ref[...] = jnp.dot(h, w_vm[:, 0:1024], preferred_element_type=F32)
        lg_ref[...] = jnp.dot(h, w_vm[:, 1024:2048], preferred_element_type=F32)
        z_ref[...] = jnp.dot(h, w_vm[:, 2048:3072], preferred_element_type=F32)
        xbc_ref[...] = jnp.dot(h, w_vm[:, 3072:3072 + XBC], preferred_element_type=F32)
        dt_ref[...] = jnp.dot(h, w_vm[:, 3072 + XBC:PC], preferred_element_type=F32)

    return pl.pallas_call(
        body, name="inproj", grid=(T // TT,),
        in_specs=[_rows(TT, D), _whole((1, D)), ANY],
        out_specs=[_rows(TT, D), _rows(TT, 1024), _rows(TT, 1024), _rows(TT, 1024), _rows(TT, XBC), _rows(TT, DTP)],
        out_shape=[S((T, D), BF), S((T, 1024), F32), S((T, 1024), F32), S((T, 1024), F32), S((T, XBC), F32), S((T, DTP), F32)],
        scratch_shapes=[pltpu.VMEM((D, PC), BF), pltpu.SemaphoreType.DMA((1,))],
        compiler_params=_params(),
    )(x, g0, wcat)


def _blockdiag_mm(v, w4_ref):
    return jnp.concatenate([_mm(v[:, 256 * j:256 * (j + 1)], w4_ref[j]) for j in range(4)], axis=1)


def _lru_gates(lx, p_ref, wa_ref, wx_ref):
    r = _sigmoid(_blockdiag_mm(lx, wa_ref) + p_ref[5:6, :])
    i = _sigmoid(_blockdiag_mm(lx, wx_ref) + p_ref[6:7, :])
    sp = _softplus(-p_ref[7:8, :])
    la = -LRU_C * r * sp
    a = jnp.exp(la)
    th = jnp.tanh(la)
    mult = jnp.sqrt(-2.0 * th / (1.0 - th))
    return r, i, sp, a, mult


def _conv_from(xp_ref, p_ref, n):
    acc = p_ref[4:5, :] + p_ref[0:1, :] * xp_ref[pl.ds(8 - CONV_K + 1, n), :]
    for k in range(1, CONV_K):
        acc = acc + p_ref[k:k + 1, :] * xp_ref[pl.ds(8 - CONV_K + 1 + k, n), :]
    return acc


def _conv_bwd(dp_ref, dconv, x, p_ref, st_ref, n):
    dp_ref[0:n, :] = dconv
    acc = None
    for k in range(CONV_K):
        g = dp_ref[pl.ds(CONV_K - 1 - k, n), :]
        acc = p_ref[k:k + 1, :] * g if acc is None else acc + p_ref[k:k + 1, :] * g
        st_ref[k:k + 1, :] += jnp.sum(g * x, axis=0, keepdims=True)
    st_ref[4:5, :] += jnp.sum(dconv, axis=0, keepdims=True)
    dp_ref[n:n + 8, :] = dp_ref[0:8, :]
    return acc


def _lru_fwd(lxr, lg, p_lru, wa4, wx4, shards):
    T = lxr.shape[0]
    NT = T // TT
    ng = len(shards)

    def body(*refs):
        lx_ref, lg_ref, p_ref, wa_ref, wx_ref = refs[:5]
        sh_in = refs[5:5 + ng]
        h_ref, y_ref, lxc_ref = refs[5 + ng:8 + ng]
        sh_out = refs[8 + ng:8 + 2 * ng]
        xp, a_s, u_s, hc, send_sems, recv_sems = refs[8 + 2 * ng:]
        for phase, step in enumerate((0, NT // 2, NT - 1)):
            @pl.when(pl.program_id(0) == step)
            def _():
                _gather_phase(phase, sh_in, sh_out, send_sems, recv_sems)

        @pl.when(pl.program_id(0) == 0)
        def _():
            xp[0:8, :] = jnp.zeros((8, LW), F32)
            hc[...] = jnp.zeros_like(hc)

        xp[8:8 + TT, :] = lx_ref[...]
        lx = _conv_from(xp, p_ref, TT)
        lxc_ref[...] = lx
        xp[0:8, :] = xp[TT:TT + 8, :]
        r, i, sp, a, mult = _lru_gates(lx, p_ref, wa_ref, wx_ref)
        a_s[...] = a
        u_s[...] = mult * (i * lx)

        def step(t, h):
            h = a_s[pl.ds(t, 1), :] * h + u_s[pl.ds(t, 1), :]
            h_ref[pl.ds(t, 1), :] = h
            return h

        hc[0:1, :] = lax.fori_loop(0, TT, step, hc[0:1, :], unroll=8)
        gated = h_ref[...] * _gelu(lg_ref[...])
        y_ref[...] = _rms_fwd(gated, p_ref[8:9, :]).astype(BF)

    return pl.pallas_call(
        body, name="lru_fwd", grid=(NT,),
        in_specs=[_rows(TT, LW), _rows(TT, LW), _whole((16, LW)), _whole((4, 256, 256)), _whole((4, 256, 256))] + [ANY] * ng,
        out_specs=[_rows(TT, LW), _rows(TT, LW), _rows(TT, LW)] + [ANY] * ng,
        out_shape=[S((T, LW), F32), S((T, LW), BF), S((T, LW), F32)] + [S((4,) + s.shape, s.dtype) for s in shards],
        scratch_shapes=[pltpu.VMEM((TT + 8, LW), F32), pltpu.VMEM((TT, LW), F32), pltpu.VMEM((TT, LW), F32),
                        pltpu.VMEM((8, LW), F32), pltpu.SemaphoreType.DMA((6 * ng,)), pltpu.SemaphoreType.DMA((6 * ng,))],
        compiler_params=_params(),
    )(lxr, lg, p_lru, wa4, wx4, *shards)


def _ssd_prep(cv, dt_ref, hp_ref):
    sg = _sigmoid(cv)
    xbc = cv * sg
    lane = lax.broadcasted_iota(jnp.int32, (CH, DTP), 1)
    raw = dt_ref[...] + hp_ref[0:1, :]
    dtv = jnp.where(lane < NH, _softplus(raw), 0.0)
    A = jnp.where(lane[0:1, :] < NH, -jnp.exp(hp_ref[1:2, :]), 0.0)
    cs = _cumsum_rows(dtv * A, CH)
    return sg, xbc, raw, dtv, A, cs


def _per_head_lanes(v):
    r = v.shape[0]
    first = lax.broadcasted_iota(jnp.int32, (r, 2 * HD), 1) < HD
    pairs = [jnp.where(first, jnp.broadcast_to(v[:, 2 * j:2 * j + 1], (r, 2 * HD)),
                       jnp.broadcast_to(v[:, 2 * j + 1:2 * j + 2], (r, 2 * HD))) for j in range(NH // 2)]
    return jnp.concatenate(pairs, axis=1)


def _per_head_rows(col, g):
    return jnp.concatenate([jnp.broadcast_to(col[g * HPG + k:g * HPG + k + 1, :], (HD, NS)) for k in range(HPG)], axis=0)


def _ssd_decays(cs):
    csT = cs.T
    cl = cs[CH - 1:CH, :]
    E_x = _per_head_lanes(jnp.exp(cs))
    dsm = jnp.exp(cl - cs)
    ds_x = _per_head_lanes(dsm)
    El_rows = jnp.broadcast_to(jnp.exp(csT[0:NH, CH - 1:CH]), (NH, NS))
    return csT, dsm, E_x, ds_x, El_rows


def _ssd_fwd(xbcr, z, dtr, cw_ssd, hp_ssd, g_ssd, shards):
    T = xbcr.shape[0]
    NC = T // CH
    ng = len(shards)

    def body(*refs):
        xr_ref, z_ref, dt_ref, cw_ref, hp_ref, g_ref = refs[:6]
        sh_in = refs[6:6 + ng]
        y_ref, yn_ref, st_ref, cv_ref = refs[6 + ng:10 + ng]
        sh_out = refs[10 + ng:10 + 2 * ng]
        xp, st, send_sems, recv_sems = refs[10 + 2 * ng:]
        for phase, step in enumerate((0, NC // 2, NC - 1)):
            @pl.when(pl.program_id(0) == step)
            def _():
                _gather_phase(phase, sh_in, sh_out, send_sems, recv_sems)

        @pl.when(pl.program_id(0) == 0)
        def _():
            xp[0:8, :] = jnp.zeros((8, XBC), F32)
            st[...] = jnp.zeros_like(st)

        xp[8:8 + CH, :] = xr_ref[...]
        cv = _conv_from(xp, cw_ref, CH)
        cv_ref[...] = cv
        sg, xbc, raw, dtv, A, cs = _ssd_prep(cv, dt_ref, hp_ref)
        xp[0:8, :] = xp[CH:CH + 8, :]
        st_ref[0] = st[...]
        csT, dsm, E_x, ds_x, El_rows = _ssd_decays(cs)
        X = xbc[:, 0:SI]
        xs = X * _per_head_lanes(dtv)
        xsd = (xs * ds_x).astype(BF)
        DX = _per_head_lanes(hp_ref[...])[2:3, :] * X
        tril = lax.broadcasted_iota(jnp.int32, (CH, CH), 0) >= lax.broadcasted_iota(jnp.int32, (CH, CH), 1)
        first = lax.broadcasted_iota(jnp.int32, (CH, 2 * HD), 1) < HD
        GW = HPG * HD
        for g in range(NG):
            Bg = xbc[:, SI + NS * g:SI + NS * (g + 1)].astype(BF)
            Cg = xbc[:, SI + NG * NS + NS * g:SI + NG * NS + NS * (g + 1)].astype(BF)
            G = _mm_nt(Cg, Bg)
            Sg = st[GW * g:GW * (g + 1), :]
            Yo = _mm_nt(Cg, Sg) * E_x[:, GW * g:GW * (g + 1)]
            st[GW * g:GW * (g + 1), :] = _per_head_rows(El_rows, g) * Sg + _mm_tn(xsd[:, GW * g:GW * (g + 1)], Bg)
            for jj in range(HPG // 2):
                j = g * (HPG // 2) + jj
                ps = slice(2 * HD * j, 2 * HD * (j + 1))
                xs_pair = xs[:, ps]
                acc = Yo[:, 2 * HD * jj:2 * HD * (jj + 1)] + DX[:, ps]
                for e in range(2):
                    h = 2 * j + e
                    Lm = jnp.exp(jnp.where(tril, cs[:, h:h + 1] - csT[h:h + 1, :], -1e30))
                    acc = acc + _mm(G * Lm, jnp.where(first if e == 0 else ~first, xs_pair, 0.0))
                y_ref[:, ps] = acc
        zz = z_ref[...]
        gated = y_ref[...] * (zz * _sigmoid(zz))
        yn_ref[...] = _rms_fwd(gated, g_ref[...]).astype(BF)

    return pl.pallas_call(
        body, name="ssd_fwd", grid=(NC,),
        in_specs=[_rows(CH, XBC), _rows(CH, SI), _rows(CH, DTP), _whole((8, XBC)), _whole((8, DTP)), _whole((1, SI))]
        + [ANY] * ng,
        out_specs=[_rows(CH, SI), _rows(CH, SI), pl.BlockSpec((1, NH * HD, NS), lambda i: (i, 0, 0)), _rows(CH, XBC)]
        + [ANY] * ng,
        out_shape=[S((T, SI), F32), S((T, SI), BF), S((NC, NH * HD, NS), F32), S((T, XBC), F32)]
        + [S((4,) + s.shape, s.dtype) for s in shards],
        scratch_shapes=[pltpu.VMEM((CH + 8, XBC), F32), pltpu.VMEM((NH * HD, NS), F32),
                        pltpu.SemaphoreType.DMA((6 * ng,)), pltpu.SemaphoreType.DMA((6 * ng,))],
        compiler_params=_params(),
    )(xbcr, z, dtr, cw_ssd, hp_ssd, g_ssd, *shards)


def _outproj(ylru, yssd, x, wout, g_pm, g_pf):
    T = x.shape[0]

    def body(yl_ref, ys_ref, x_ref, w_hbm, gpm_ref, gpf_ref, mix_ref, x1_ref, h2_ref, w_vm, sem):
        _load_once([(w_hbm, w_vm)], sem)
        mix = (jnp.dot(yl_ref[...], w_vm[0:LW, :], preferred_element_type=F32)
               + jnp.dot(ys_ref[...], w_vm[LW:LW + SI, :], preferred_element_type=F32))
        mix_ref[...] = mix
        x1 = x_ref[...] + _rms_fwd(mix, gpm_ref[...])
        x1_ref[...] = x1
        h2_ref[...] = _rms_fwd(x1, gpf_ref[...]).astype(BF)

    return pl.pallas_call(
        body, name="outproj", grid=(T // TT,),
        in_specs=[_rows(TT, LW), _rows(TT, SI), _rows(TT, D), ANY, _whole((1, D)), _whole((1, D))],
        out_specs=[_rows(TT, D), _rows(TT, D), _rows(TT, D)],
        out_shape=[S((T, D), F32), S((T, D), F32), S((T, D), BF)],
        scratch_shapes=[pltpu.VMEM((LW + SI, D), BF), pltpu.SemaphoreType.DMA((1,))],
        compiler_params=_params(),
    )(ylru, yssd, x, wout, g_pm, g_pf)


def _ffn_fwd(h2, x1, tgt, wg, wu, wd, g_pff):
    T = x1.shape[0]

    def body(h2_ref, x1_ref, t_ref, wg_hbm, wu_hbm, wd_hbm, g_ref,
             gate_ref, up_ref, act_ref, df_ref, dx2_ref, st_ref, wg_vm, wu_vm, wd_vm, sem):
        _load_once([(wg_hbm, wg_vm), (wu_hbm, wu_vm), (wd_hbm, wd_vm)], sem)

        @pl.when(pl.program_id(0) == 0)
        def _():
            st_ref[...] = jnp.zeros_like(st_ref)

        h2 = h2_ref[...]
        gate = jnp.dot(h2, wg_vm[...], preferred_element_type=F32)
        up = jnp.dot(h2, wu_vm[...], preferred_element_type=F32)
        gate_ref[...] = gate
        up_ref[...] = up
        act = (gate * _sigmoid(gate) * up).astype(BF)
        act_ref[...] = act
        f = jnp.dot(act, wd_vm[...], preferred_element_type=F32)
        g = g_ref[...]
        x2 = x1_ref[...] + _rms_fwd(f, g)
        err = x2 - t_ref[...]
        st_ref[0:1, :] += 0.5 * jnp.sum(err * err, axis=0, keepdims=True) * (1.0 / D)
        dx2 = err * (1.0 / D)
        dx2_ref[...] = dx2
        df, dg = _rms_bwd(f, g, dx2)
        df_ref[...] = df.astype(BF)
        st_ref[1:2, :] += dg

    return pl.pallas_call(
        body, name="ffn_fwd", grid=(T // TT,),
        in_specs=[_rows(TT, D), _rows(TT, D), _rows(TT, D), ANY, ANY, ANY, _whole((1, D))],
        out_specs=[_rows(TT, DFF), _rows(TT, DFF), _rows(TT, DFF), _rows(TT, D), _rows(TT, D), _whole((8, D))],
        out_shape=[S((T, DFF), F32), S((T, DFF), F32), S((T, DFF), BF), S((T, D), BF), S((T, D), F32), S((8, D), F32)],
        scratch_shapes=[pltpu.VMEM((D, DFF), BF), pltpu.VMEM((D, DFF), BF), pltpu.VMEM((DFF, D), BF),
                        pltpu.SemaphoreType.DMA((3,))],
        compiler_params=_params(),
    )(h2, x1, tgt, wg, wu, wd, g_pff)


def _ffn_bwd(df, gate, up, wdT, wgT, wuT):
    T = df.shape[0]

    def body(df_ref, gate_ref, up_ref, wd_hbm, wg_hbm, wu_hbm, dgate_ref, dup_ref, dh2_ref, wd_vm, wg_vm, wu_vm, sem):
        _load_once([(wd_hbm, wd_vm), (wg_hbm, wg_vm), (wu_hbm, wu_vm)], sem)
        dact = jnp.dot(df_ref[...], wd_vm[...], preferred_element_type=F32)
        gate = gate_ref[...]
        s = _sigmoid(gate)
        dup = (dact * (gate * s)).astype(BF)
        dgate = (dact * up_ref[...] * (s + gate * s * (1.0 - s))).astype(BF)
        dup_ref[...] = dup
        dgate_ref[...] = dgate
        dh2_ref[...] = (jnp.dot(dgate, wg_vm[...], preferred_element_type=F32)
                        + jnp.dot(dup, wu_vm[...], preferred_element_type=F32))

    return pl.pallas_call(
        body, name="ffn_bwd", grid=(T // TT,),
        in_specs=[_rows(TT, D), _rows(TT, DFF), _rows(TT, DFF), ANY, ANY, ANY],
        out_specs=[_rows(TT, DFF), _rows(TT, DFF), _rows(TT, D)],
        out_shape=[S((T, DFF), BF), S((T, DFF), BF), S((T, D), F32)],
        scratch_shapes=[pltpu.VMEM((D, DFF), BF), pltpu.VMEM((DFF, D), BF), pltpu.VMEM((DFF, D), BF),
                        pltpu.SemaphoreType.DMA((3,))],
        compiler_params=_params(),
    )(df, gate, up, wdT, wgT, wuT)


def _mix_bwd(dh2, x1, dx2, mix, woutT, g_pf, g_pm):
    T = x1.shape[0]

    def body(dh2_ref, x1_ref, dx2_ref, mix_ref, w_hbm, gpf_ref, gpm_ref,
             dx1_ref, dmix_ref, dyl_ref, dys_ref, st_ref, w_vm, sem):
        _load_once([(w_hbm, w_vm)], sem)

        @pl.when(pl.program_id(0) == 0)
        def _():
            st_ref[...] = jnp.zeros_like(st_ref)

        dxa, dgpf = _rms_bwd(x1_ref[...], gpf_ref[...], dh2_ref[...])
        dx1 = dx2_ref[...] + dxa
        dx1_ref[...] = dx1
        dmix, dgpm = _rms_bwd(mix_ref[...], gpm_ref[...], dx1)
        dmix = dmix.astype(BF)
        dmix_ref[...] = dmix
        st_ref[0:1, :] += dgpf
        st_ref[1:2, :] += dgpm
        dyl_ref[...] = jnp.dot(dmix, w_vm[:, 0:LW], preferred_element_type=F32)
        dys_ref[...] = jnp.dot(dmix, w_vm[:, LW:LW + SI], preferred_element_type=F32)

    return pl.pallas_call(
        body, name="mix_bwd", grid=(T // TT,),
        in_specs=[_rows(TT, D), _rows(TT, D), _rows(TT, D), _rows(TT, D), ANY, _whole((1, D)), _whole((1, D))],
        out_specs=[_rows(TT, D), _rows(TT, D), _rows(TT, LW), _rows(TT, SI), _whole((8, D))],
        out_shape=[S((T, D), F32), S((T, D), BF), S((T, LW), F32), S((T, SI), F32), S((8, D), F32)],
        scratch_shapes=[pltpu.VMEM((D, LW + SI), BF), pltpu.SemaphoreType.DMA((1,))],
        compiler_params=_params(),
    )(dh2, x1, dx2, mix, woutT, g_pf, g_pm)


def _halo(width, n_tiles, tile):
    per = tile // 8
    return pl.BlockSpec((8, width), lambda i: (jnp.maximum((n_tiles - 1 - i) * per - 1, 0), 0))


def _lru_bwd(dy, lxr, lxc, lg, h, p_lru, wa4, wx4, wa4T, wx4T):
    T = dy.shape[0]
    NT = T // TT

    def body(dy_ref, lxr_ref, lxc_ref, lg_ref, h_ref, hh_ref, p_ref, wa_ref, wx_ref, waT_ref, wxT_ref,
             dlx_ref, dlg_ref, st_ref, dwa_ref, dwx_ref, hp, dp, a_s, d_s, g_s, cc):
        first = pl.program_id(0) == 0
        top = pl.program_id(0) == NT - 1

        @pl.when(first)
        def _():
            st_ref[...] = jnp.zeros_like(st_ref)
            dwa_ref[...] = jnp.zeros_like(dwa_ref)
            dwx_ref[...] = jnp.zeros_like(dwx_ref)
            dp[TT:TT + 8, :] = jnp.zeros((8, LW), F32)
            cc[...] = jnp.zeros_like(cc)

        hp[0:8, :] = hh_ref[...] * jnp.where(top, 0.0, 1.0)
        hp[8:8 + TT, :] = h_ref[...]
        lx = lxc_ref[...]
        r, i, sp, a, mult = _lru_gates(lx, p_ref, wa_ref, wx_ref)

        lg = lg_ref[...]
        hcur = h_ref[...]
        ge = _gelu(lg)
        dgated, dgn = _rms_bwd(hcur * ge, p_ref[8:9, :], dy_ref[...])
        st_ref[8:9, :] += dgn
        dlg_ref[...] = (dgated * hcur * _gelu_grad(lg)).astype(BF)
        a_s[...] = a
        d_s[...] = dgated * ge

        def step(k, c):
            t = TT - 1 - k
            g = d_s[pl.ds(t, 1), :] + c
            g_s[pl.ds(t, 1), :] = g
            return a_s[pl.ds(t, 1), :] * g

        cc[0:1, :] = lax.fori_loop(0, TT, step, cc[0:1, :], unroll=8)
        gt = g_s[...]
        da = gt * hp[pl.ds(7, TT), :]
        dmult = gt * i * lx
        di = gt * mult * lx
        dlxc = gt * mult * i
        dla = da * a - dmult * (a * a) / mult
        dr = dla * (-LRU_C * sp)
        st_ref[7:8, :] += jnp.sum(dla * (-LRU_C * r), axis=0, keepdims=True) * (-_sigmoid(-p_ref[7:8, :]))
        dzr = dr * r * (1.0 - r)
        dzi = di * i * (1.0 - i)
        st_ref[5:6, :] += jnp.sum(dzr, axis=0, keepdims=True)
        st_ref[6:7, :] += jnp.sum(dzi, axis=0, keepdims=True)
        dlxc = dlxc + _blockdiag_mm(dzr, waT_ref) + _blockdiag_mm(dzi, wxT_ref)
        for j in range(4):
            sl = slice(256 * j, 256 * (j + 1))
            pa = _mm_tn(lx[:, sl], dzr[:, sl])
            px = _mm_tn(lx[:, sl], dzi[:, sl])
            for b in range(4):
                bs = slice(BW * b, BW * (b + 1))
                dwa_ref[4 * j + b] += pa[bs, bs]
                dwx_ref[4 * j + b] += px[bs, bs]
        dlx_ref[...] = _conv_bwd(dp, dlxc, lxr_ref[...], p_ref, st_ref, TT).astype(BF)

    w4 = _whole((4, 256, 256))
    return pl.pallas_call(
        body, name="lru_bwd", grid=(NT,),
        in_specs=[_rows(TT, LW, NT), _rows(TT, LW, NT), _rows(TT, LW, NT), _rows(TT, LW, NT), _rows(TT, LW, NT),
                  _halo(LW, NT, TT), _whole((16, LW)), w4, w4, w4, w4],
        out_specs=[_rows(TT, LW, NT), _rows(TT, LW, NT), _whole((16, LW)), _whole((NBLK, BW, BW)), _whole((NBLK, BW, BW))],
        out_shape=[S((T, LW), BF), S((T, LW), BF), S((16, LW), F32), S((NBLK, BW, BW), F32), S((NBLK, BW, BW), F32)],
        scratch_shapes=[pltpu.VMEM((TT + 8, LW), F32), pltpu.VMEM((TT + 8, LW), F32),
                        pltpu.VMEM((TT, LW), F32), pltpu.VMEM((TT, LW), F32), pltpu.VMEM((TT, LW), F32),
                        pltpu.VMEM((8, LW), F32)],
        compiler_params=_params(),
    )(dy, lxr, lxc, lg, h, h, p_lru, wa4, wx4, wa4T, wx4T)


def _ssd_bwd(dyn, xbcr, cv, z, dtr, y, states, cw_ssd, hp_ssd, g_ssd, parts):
    T = dyn.shape[0]
    NC = T // CH
    nq = len(parts)

    def body(*refs):
        dyn_ref, xr_ref, cv_ref, z_ref, dt_ref, y_ref, st_ref, cw_ref, hp_ref, g_ref = refs[:10]
        q_in = refs[10:10 + nq]
        dxbc_ref, dz_ref, ddt_ref, cst_ref, hst_ref, gst_ref = refs[10 + nq:16 + nq]
        q_out = refs[16 + nq:16 + 2 * nq]
        dp, dS, dxb, yo_s, q_s, dxs_s, t1_s, send_sems, recv_sems = refs[16 + 2 * nq:]
        first = pl.program_id(0) == 0
        for phase, step in enumerate((0, NC - 1)):
            @pl.when(pl.program_id(0) == step)
            def _():
                _quad_phase(phase, q_in, q_out, send_sems, recv_sems)

        @pl.when(first)
        def _():
            cst_ref[...] = jnp.zeros_like(cst_ref)
            hst_ref[...] = jnp.zeros_like(hst_ref)
            gst_ref[...] = jnp.zeros_like(gst_ref)
            dp[CH:CH + 8, :] = jnp.zeros((8, XBC), F32)
            dS[...] = jnp.zeros_like(dS)

        cv = cv_ref[...]
        sg, xbc, raw, dtv, A, cs = _ssd_prep(cv, dt_ref, hp_ref)
        csT, dsm, E_x, ds_x, El_rows = _ssd_decays(cs)
        row_i = lax.broadcasted_iota(jnp.int32, (CH, CH), 0)
        col_i = lax.broadcasted_iota(jnp.int32, (CH, CH), 1)
        tril = row_i >= col_i
        first = col_i < HD
        head_of = ((lax.broadcasted_iota(jnp.int32, (DTP, SI), 1) >> 6)
                   == lax.broadcasted_iota(jnp.int32, (DTP, SI), 0)).astype(BF)
        head_ofT = ((lax.broadcasted_iota(jnp.int32, (SI, DTP), 0) >> 6)
                    == lax.broadcasted_iota(jnp.int32, (SI, DTP), 1)).astype(BF)

        def hi_lo(v):
            hi = v.astype(BF)
            return hi, (v - hi.astype(F32)).astype(BF)

        def lane_sums(v):
            hi, lo = hi_lo(v)
            return _mm(hi, head_ofT) + _mm(lo, head_ofT)

        zz = z_ref[...]
        sz = _sigmoid(zz)
        yv = y_ref[...]
        dgn, dg = _rms_bwd(yv * (zz * sz), g_ref[...], dyn_ref[...])
        gst_ref[0:1, :] += dg
        dz_ref[...] = (dgn * yv * (sz + zz * sz * (1.0 - sz))).astype(BF)
        dY = dgn * (zz * sz)

        X = xbc[:, 0:SI]
        dt_x = _per_head_lanes(dtv)
        xs = X * dt_x
        xsd = (xs * ds_x).astype(BF)
        D_x = _per_head_lanes(hp_ref[...])[2:3, :]
        dcs_col = jnp.zeros((CH, DTP), F32)
        dcs_row = jnp.zeros((CH, DTP), F32)
        GW = HPG * HD
        for g in range(NG):
            gs = slice(GW * g, GW * (g + 1))
            Bg = xbc[:, SI + NS * g:SI + NS * (g + 1)].astype(BF)
            Cg = xbc[:, SI + NG * NS + NS * g:SI + NG * NS + NS * (g + 1)].astype(BF)
            G = _mm_nt(Cg, Bg)
            Sg = st_ref[0, gs, :]
            dSe = dS[gs, :]
            dYg = dY[:, gs]
            yo_s[:, gs] = _mm_nt(Cg, Sg) * E_x[:, gs]
            dP = dYg * E_x[:, gs]
            dCg = _mm(dP, Sg)
            dS[gs, :] = _mm_tn(dP, Cg) + _per_head_rows(El_rows, g) * dSe
            t1_s[gs, :] = dSe * Sg
            Q = _mm_nt(Bg, dSe)
            q_s[:, gs] = Q
            dBg = _mm(xsd[:, gs], dSe)
            dG = jnp.zeros((CH, CH), F32)
            for jj in range(HPG // 2):
                j = g * (HPG // 2) + jj
                ps = slice(2 * HD * j, 2 * HD * (j + 1))
                xs_pair = xs[:, ps]
                dxs_pair = Q[:, 2 * HD * jj:2 * HD * (jj + 1)] * ds_x[:, ps]
                for e in range(2):
                    h = 2 * j + e
                    Lm = jnp.exp(jnp.where(tril, cs[:, h:h + 1] - csT[h:h + 1, :], -1e30))
                    M = G * Lm
                    dYm = jnp.where(first if e == 0 else ~first, dY[:, ps], 0.0).astype(BF)
                    dM = _mm_nt(dYm, xs_pair)
                    dxs_pair = dxs_pair + _mm_tn(M, dYm)
                    Wm = dM * M
                    dcs_col = dcs_col + jnp.where(col_i == h, jnp.sum(Wm, axis=1, keepdims=True), 0.0)
                    dcs_row = dcs_row + jnp.where(row_i == h, -jnp.sum(Wm, axis=0, keepdims=True), 0.0)
                    dG = dG + dM * Lm
                dxs_s[:, ps] = dxs_pair
            dxb[:, SI + NS * g:SI + NS * (g + 1)] = dBg + _mm_tn(dG, Cg)
            dxb[:, SI + NG * NS + NS * g:SI + NG * NS + NS * (g + 1)] = dCg + _mm(dG, Bg)

        dxs = dxs_s[...]
        dxb[:, 0:SI] = D_x * dY + dxs * dt_x
        dds = lane_sums(q_s[...] * xs) * dsm
        dcs_col = dcs_col + lane_sums(dY * yo_s[...]) - dds
        ddt_col = lane_sums(dxs * X)
        dD = jnp.sum(lane_sums(dY * X), axis=0, keepdims=True)
        t_hi, t_lo = hi_lo(t1_s[...])
        dcl_rows = jnp.sum(_mm(head_of, t_hi) + _mm(head_of, t_lo), axis=1, keepdims=True) * jnp.exp(csT[:, CH - 1:CH])
        dcs_row = dcs_row + jnp.where(col_i == CH - 1, dcl_rows, 0.0)
        dcs_col = dcs_col + jnp.where(row_i == CH - 1, jnp.sum(dds, axis=0, keepdims=True), 0.0)

        da = _rev_cumsum_rows(dcs_col + dcs_row.T, CH)
        ddt_col = ddt_col + da * A
        hst_ref[1:2, :] += jnp.sum(da * dtv, axis=0, keepdims=True) * A
        hst_ref[2:3, :] += dD
        draw = jnp.where(col_i < NH, ddt_col * _sigmoid(raw), 0.0)
        ddt_ref[...] = draw.astype(BF)
        hst_ref[0:1, :] += jnp.sum(draw, axis=0, keepdims=True)

        dcv = dxb[...] * (sg + cv * sg * (1.0 - sg))
        dxbc_ref[...] = _conv_bwd(dp, dcv, xr_ref[...], cw_ref, cst_ref, CH).astype(BF)

    return pl.pallas_call(
        body, name="ssd_bwd", grid=(NC,),
        in_specs=[_rows(CH, SI, NC), _rows(CH, XBC, NC), _rows(CH, XBC, NC), _rows(CH, SI, NC), _rows(CH, DTP, NC),
                  _rows(CH, SI, NC), pl.BlockSpec((1, NH * HD, NS), lambda i: (NC - 1 - i, 0, 0)),
                  _whole((8, XBC)), _whole((8, DTP)), _whole((1, SI))] + [ANY] * nq,
        out_specs=[_rows(CH, XBC, NC), _rows(CH, SI, NC), _rows(CH, DTP, NC), _whole((16, XBC)), _whole((16, DTP)),
                   _whole((8, SI))] + [ANY] * nq,
        out_shape=[S((T, XBC), BF), S((T, SI), BF), S((T, DTP), BF), S((16, XBC), F32), S((16, DTP), F32), S((8, SI), F32)]
        + [S(p.shape, p.dtype) for p in parts],
        scratch_shapes=[pltpu.VMEM((CH + 8, XBC), F32), pltpu.VMEM((NH * HD, NS), F32),
                        pltpu.VMEM((CH, XBC), F32), pltpu.VMEM((CH, SI), F32), pltpu.VMEM((CH, SI), F32),
                        pltpu.VMEM((CH, SI), F32), pltpu.VMEM((NH * HD, NS), F32),
                        pltpu.SemaphoreType.DMA((3 * nq,)), pltpu.SemaphoreType.DMA((3 * nq,))],
        compiler_params=_params(),
    )(dyn, xbcr, cv, z, dtr, y, states, cw_ssd, hp_ssd, g_ssd, *parts)


def _inproj_bwd(dlx, dlg, dz, dxbc, ddt, x, dx1, wcatT, g0):
    T = x.shape[0]

    def body(dlx_ref, dlg_ref, dz_ref, dxbc_ref, ddt_ref, x_ref, dx1_ref, w_hbm, g_ref, dx_ref, st_ref, w_vm, sem):
        _load_once([(w_hbm, w_vm)], sem)

        @pl.when(pl.program_id(0) == 0)
        def _():
            st_ref[...] = jnp.zeros_like(st_ref)

        dh = jnp.dot(dlx_ref[...], w_vm[0:1024, :], preferred_element_type=F32)
        dh = dh + jnp.dot(dlg_ref[...], w_vm[1024:2048, :], preferred_element_type=F32)
        dh = dh + jnp.dot(dz_ref[...], w_vm[2048:3072, :], preferred_element_type=F32)
        dh = dh + jnp.dot(dxbc_ref[...], w_vm[3072:3072 + XBC, :], preferred_element_type=F32)
        dh = dh + jnp.dot(ddt_ref[...], w_vm[3072 + XBC:PC, :], preferred_element_type=F32)
        dx, dg = _rms_bwd(x_ref[...], g_ref[...], dh)
        dx_ref[...] = dx1_ref[...] + dx
        st_ref[0:1, :] += dg

    return pl.pallas_call(
        body, name="inproj_bwd", grid=(T // TT,),
        in_specs=[_rows(TT, 1024), _rows(TT, 1024), _rows(TT, 1024), _rows(TT, XBC), _rows(TT, DTP), _rows(TT, D),
                  _rows(TT, D), ANY, _whole((1, D))],
        out_specs=[_rows(TT, D), _whole((8, D))],
        out_shape=[S((T, D), F32), S((8, D), F32)],
        scratch_shapes=[pltpu.VMEM((PC, D), BF), pltpu.SemaphoreType.DMA((1,))],
        compiler_params=_params(),
    )(dlx, dlg, dz, dxbc, ddt, x, dx1, wcatT, g0)


def _wgrad(name, a, b):
    T, M = a.shape
    N = b.shape[1]
    tk = min(T, 2048 if M <= 1024 else 1024)
    tn = N
    while M * tn * 4 > (6 << 20) and tn % 256 == 0:
        tn //= 2

    def body(a_ref, b_ref, o_ref):
        p = lax.dot_general(a_ref[...], b_ref[...], (((0,), (0,)), ((), ())), preferred_element_type=F32)

        @pl.when(pl.program_id(1) == 0)
        def _():
            o_ref[...] = p

        @pl.when(pl.program_id(1) > 0)
        def _():
            o_ref[...] += p

    return pl.pallas_call(
        body, name=name, grid=(N // tn, T // tk),
        in_specs=[pl.BlockSpec((tk, M), lambda j, k: (k, 0)), pl.BlockSpec((tk, tn), lambda j, k: (k, j))],
        out_specs=pl.BlockSpec((M, tn), lambda j, k: (0, j)), out_shape=S((M, N), F32),
        compiler_params=_params(2),
    )(a, b)


def _adamw(name, w, g, m, v):
    _, R, C = w.shape
    tr = _row_tile(R, C)

    def body(w_ref, g_ref, m_ref, v_ref, d_ref, nm_ref, nv_ref):
        d_ref[0], nm_ref[0], nv_ref[0] = _adam_math(w_ref[0], g_ref[...], m_ref[0], v_ref[0])

    blk = pl.BlockSpec((1, tr, C), lambda i: (0, i, 0))
    return pl.pallas_call(
        body, name=name, grid=(R // tr,),
        in_specs=[blk, pl.BlockSpec((tr, C), lambda i: (i, 0)), blk, blk], out_specs=[blk] * 3,
        out_shape=[S((1, R, C), F32)] * 3, compiler_params=_params(),
    )(w, g, m, v)


def _half(ref, c, hr):
    sl = pl.ds(pl.multiple_of(c * hr, 8), hr)
    return ref.at[:, sl, :] if len(ref.shape) == 3 else ref.at[sl, :]


def _allgather_weights(shards):
    n = len(shards)

    def body(*refs):
        for phase in range(3):
            _gather_phase(phase, refs[:n], refs[n:2 * n], refs[2 * n], refs[2 * n + 1])

    return pl.pallas_call(
        body, name="allgather_weights", in_specs=[ANY] * n, out_specs=[ANY] * n,
        out_shape=[S((4,) + s.shape, s.dtype) for s in shards],
        scratch_shapes=[pltpu.SemaphoreType.DMA((6 * n,)), pltpu.SemaphoreType.DMA((6 * n,))],
    )(*shards)


def _pair_exchange(name, bufs):
    n = len(bufs)

    def half_shape(b):
        return b.shape[:-2] + (b.shape[-2] // 2, b.shape[-1])

    def body(*refs):
        ins, outs = refs[:n], refs[n:2 * n]
        send_sems, recv_sems = refs[2 * n], refs[2 * n + 1]
        x, y, c = _pos()
        copies = [_remote(_half(src, 1 - c, src.shape[-2] // 2), dst, send_sems.at[k], recv_sems.at[k], (x, y, 1 - c))
                  for k, (src, dst) in enumerate(zip(ins, outs))]
        for cp in copies:
            cp.start()
        for cp in copies:
            cp.wait()

    return pl.pallas_call(
        body, name=name, in_specs=[ANY] * n, out_specs=[ANY] * n,
        out_shape=[S(half_shape(b), b.dtype) for b in bufs],
        scratch_shapes=[pltpu.SemaphoreType.DMA((n,)), pltpu.SemaphoreType.DMA((n,))],
    )(*bufs)


def _quad_exchange(bufs, scatter):
    n = len(bufs)

    def body(*refs):
        ins, outs = refs[:n], refs[n:2 * n]
        send_sems, recv_sems, local_sems = refs[2 * n], refs[2 * n + 1], refs[2 * n + 2]
        x, y, c = _pos()
        me = 2 * x + y
        chips = _other_chips(x, y)
        copies, locals_ = [], []
        for k, (src, dst) in enumerate(zip(ins, outs)):
            if not scatter[k]:
                own = pltpu.make_async_copy(src, dst.at[me], local_sems.at[k])
                own.start()
                locals_.append(own)
            for j, (cx, cy) in enumerate(chips):
                piece = src.at[2 * cx + cy] if scatter[k] else src
                cp = _remote(piece, dst.at[me], send_sems.at[3 * k + j], recv_sems.at[3 * k + j], (cx, cy, c))
                cp.start()
                copies.append(cp)
        for k, (src, dst) in enumerate(zip(ins, outs)):
            for j, (cx, cy) in enumerate(chips):
                blk = dst.at[2 * cx + cy]
                _remote(blk, blk, send_sems.at[3 * k + j], recv_sems.at[3 * k + j], (cx, cy, c)).wait_recv()
        for cp in copies:
            cp.wait_send()
        for cp in locals_:
            cp.wait()

    return pl.pallas_call(
        body, name="quad_exchange", in_specs=[ANY] * n, out_specs=[ANY] * n,
        out_shape=[S((4,) + (b.shape[1:] if sc else b.shape), b.dtype) for b, sc in zip(bufs, scatter)],
        scratch_shapes=[pltpu.SemaphoreType.DMA((3 * n,)), pltpu.SemaphoreType.DMA((3 * n,)), pltpu.SemaphoreType.DMA((n,))],
    )(*bufs)


def _pair_gather(bufs):
    n = len(bufs)

    def body(*refs):
        ins, outs = refs[:n], refs[n:2 * n]
        send_sems, recv_sems = refs[2 * n], refs[2 * n + 1]
        x, y, c = _pos()
        copies = []
        for k, buf in enumerate(outs):
            mine = _half(buf, c, buf.shape[0] // 2)
            cp = _remote(mine, mine, send_sems.at[k], recv_sems.at[k], (x, y, 1 - c))
            cp.start()
            copies.append(cp)
        for k, buf in enumerate(outs):
            theirs = _half(buf, 1 - c, buf.shape[0] // 2)
            _remote(theirs, theirs, send_sems.at[k], recv_sems.at[k], (x, y, 1 - c)).wait_recv()
        for cp in copies:
            cp.wait_send()

    return pl.pallas_call(
        body, name="pair_gather", in_specs=[ANY] * n, out_specs=[ANY] * n,
        out_shape=[S(b.shape, b.dtype) for b in bufs], input_output_aliases={k: k for k in range(n)},
        scratch_shapes=[pltpu.SemaphoreType.DMA((n,)), pltpu.SemaphoreType.DMA((n,))],
    )(*bufs)


def _row_tile(rows, cols, mult=8):
    best = mult
    for t in range(mult, rows + 1, mult):
        if rows % t == 0 and t * cols * 4 <= (1 << 20):
            best = t
    return best


def _add_own_half(name, full, got, c, out_dtype, by_columns):
    hr = got.shape[-2]
    wide = got.shape[-1]
    cols = wide // 4 if by_columns else wide
    tr = _row_tile(hr, wide, 16)
    per = hr // tr

    if by_columns:
        def body(c_ref, a_ref, b_ref, o_ref):
            v = a_ref[...] + b_ref[...]
            for j in range(4):
                o_ref[j] = v[:, j * cols:(j + 1) * cols].astype(out_dtype)

        in_specs = [pl.BlockSpec((tr, wide), lambda i, c_ref: (c_ref[0] * per + i, 0)),
                    pl.BlockSpec((tr, wide), lambda i, c_ref: (i, 0))]
        out_specs = pl.BlockSpec((4, tr, cols), lambda i, c_ref: (0, i, 0))
        grid = (per,)
    else:
        def body(c_ref, a_ref, b_ref, o_ref):
            o_ref[...] = (a_ref[...] + b_ref[...]).astype(out_dtype)

        in_specs = [pl.BlockSpec((1, tr, cols), lambda s, i, c_ref: (s, c_ref[0] * per + i, 0)),
                    pl.BlockSpec((1, tr, cols), lambda s, i, c_ref: (s, i, 0))]
        out_specs = pl.BlockSpec((1, tr, cols), lambda s, i, c_ref: (s, i, 0))
        grid = (4, per)
    return pl.pallas_call(
        body, name=name,
        grid_spec=pltpu.PrefetchScalarGridSpec(num_scalar_prefetch=1, grid=grid, in_specs=in_specs, out_specs=out_specs),
        out_shape=S((4, hr, cols), out_dtype), compiler_params=_params(len(grid)),
    )(jnp.reshape(c, (1,)).astype(jnp.int32), full, got)


def _small_add_own_half(fulls, gots, c):
    n = len(fulls)

    def body(c_ref, *refs):
        for a_ref, b_ref, o_ref in zip(refs[:n], refs[n:2 * n], refs[2 * n:]):
            hr = b_ref.shape[0]
            o_ref[...] = a_ref[pl.ds(pl.multiple_of(c_ref[0] * hr, 8), hr), :] + b_ref[...]

    specs = lambda arrs: [pl.BlockSpec(a.shape, lambda i, c_ref: (0, 0)) for a in arrs]
    return pl.pallas_call(
        body, name="small_pair_add",
        grid_spec=pltpu.PrefetchScalarGridSpec(num_scalar_prefetch=1, grid=(1,), in_specs=specs(fulls) + specs(gots),
                                               out_specs=specs(gots)),
        out_shape=[S(g.shape, F32) for g in gots], compiler_params=_params(),
    )(jnp.reshape(c, (1,)).astype(jnp.int32), *fulls, *gots)


def _small_sum_slots(slots, c):
    n = len(slots)

    def body(c_ref, *refs):
        for s_ref, o_ref in zip(refs[:n], refs[n:]):
            hr = s_ref.shape[1]
            o_ref[pl.ds(pl.multiple_of(c_ref[0] * hr, 8), hr), :] = ((s_ref[0] + s_ref[1]) + s_ref[2]) + s_ref[3]

    outs = [S((2 * s.shape[1], s.shape[2]), F32) for s in slots]
    return pl.pallas_call(
        body, name="small_quad_sum",
        grid_spec=pltpu.PrefetchScalarGridSpec(
            num_scalar_prefetch=1, grid=(1,),
            in_specs=[pl.BlockSpec(s.shape, lambda i, c_ref: (0, 0, 0)) for s in slots],
            out_specs=[pl.BlockSpec(o.shape, lambda i, c_ref: (0, 0)) for o in outs]),
        out_shape=outs, compiler_params=_params(),
    )(jnp.reshape(c, (1,)).astype(jnp.int32), *slots)


def _sum_slots(name, own, slots, me, c):
    _, rows, cols = slots.shape
    tr = _row_tile(rows, cols, 16 if slots.dtype == jnp.bfloat16 else 8)
    per = rows // tr
    three = len(own.shape) == 3

    def body(p_ref, own_ref, s0, s1, s2, s3, o_ref):
        mine = own_ref[0] if three else own_ref[...]
        acc = None
        for j, s_ref in enumerate((s0, s1, s2, s3)):
            v = jnp.where(p_ref[0] == j, mine, s_ref[0]).astype(F32)
            acc = v if acc is None else acc + v
        o_ref[...] = acc

    def slot_spec(j):
        return pl.BlockSpec((1, tr, cols), lambda i, p: (jnp.where(p[0] == j, (j + 1) % 4, j), i, 0))

    own_spec = (pl.BlockSpec((1, tr, cols), lambda i, p: (p[0], i, 0)) if three
                else pl.BlockSpec((tr, cols), lambda i, p: (i, 0)))
    return pl.pallas_call(
        body, name=name,
        grid_spec=pltpu.PrefetchScalarGridSpec(
            num_scalar_prefetch=1, grid=(per,), in_specs=[own_spec] + [slot_spec(j) for j in range(4)],
            out_specs=pl.BlockSpec((tr, cols), lambda i, p: (p[1] * per + i, 0))),
        out_shape=S((2 * rows, cols), F32), compiler_params=_params(),
    )(jnp.stack([me, c]).astype(jnp.int32), own, slots, slots, slots, slots)


BIG = ("w_in", "w_out", "w_gate", "w_up", "w_down")
ROW_PARAMS = (("pre_mix_norm", 0), ("lru_conv_b", 12), ("lru_ba", 13), ("lru_bx", 14), ("lru_lambda", 15),
              ("lru_out_norm", 16), ("ssd_out_norm", 24), ("post_mix_norm", 33), ("pre_ffn_norm", 32), ("post_ffn_norm", 41))
LRU_CONV_ROWS = (8, 12)
LOSS_ROW = 40
HEAD_PARAMS = (("ssd_dt_bias", 0), ("ssd_a_log", 1), ("ssd_d", 2))
SMALL = tuple(n for n, _ in ROW_PARAMS) + ("ssd_conv_b",) + tuple(n for n, _ in HEAD_PARAMS) + (
    "lru_wa", "lru_wx", "lru_conv_w", "ssd_conv_w")


def _diag4(w):
    eye = jnp.eye(4, dtype=w.dtype).reshape(1, 4, 1, 4, 1)
    return (w.reshape(4, 4, BW, 1, BW) * eye).reshape(4, 4 * BW, 4 * BW)


def _adam_math(w, g, m, v):
    mm = ADAM_B1 * m + (1.0 - ADAM_B1) * g
    vv = ADAM_B2 * v + (1.0 - ADAM_B2) * (g * g)
    c1 = 1.0 - ADAM_B1 ** ADAM_STEP
    c2 = 1.0 - ADAM_B2 ** ADAM_STEP
    return -ADAM_LR * ((mm / c1) / (jnp.sqrt(vv / c2) + ADAM_EPS) + ADAM_WD * w), mm, vv


def _adamw_small(rows, cst, hst, dwa, dwx, glcw, gscw, w, m, v):
    def grad_of(name, refs):
        rows_ref, cst_ref, hst_ref, dwa_ref, dwx_ref, glcw_ref, gscw_ref = refs
        for n, r in ROW_PARAMS:
            if n == name:
                return rows_ref[r:r + 1, :]
        for n, r in HEAD_PARAMS:
            if n == name:
                return hst_ref[r:r + 1, 0:NH]
        return {"ssd_conv_b": lambda: cst_ref[4:5, :], "lru_wa": lambda: dwa_ref[...], "lru_wx": lambda: dwx_ref[...],
                "lru_conv_w": lambda: glcw_ref[...], "ssd_conv_w": lambda: gscw_ref[...]}[name]()

    shapes = {n: (w[n].shape[1:] if len(w[n].shape) > 2 else w[n].shape) for n in SMALL}
    flat = lambda d: [d[n].reshape(shapes[n]) for n in SMALL]
    ns = len(SMALL)

    def body(*refs):
        srcs, rest = refs[:7], refs[7:]
        w_refs, m_refs, v_refs = rest[:ns], rest[ns:2 * ns], rest[2 * ns:3 * ns]
        outs = rest[3 * ns:]
        for k, name in enumerate(SMALL):
            g = grad_of(name, srcs)
            d, mm, vv = _adam_math(w_refs[k][...], g, m_refs[k][...], v_refs[k][...])
            outs[4 * k][...] = g
            outs[4 * k + 1][...] = d
            outs[4 * k + 2][...] = mm
            outs[4 * k + 3][...] = vv

    res = pl.pallas_call(
        body, name="adamw_small",
        out_shape=[S(shapes[n], F32) for n in SMALL for _ in range(4)],
        compiler_params=pltpu.CompilerParams(vmem_limit_bytes=VMEM_LIMIT),
    )(rows, cst, hst, dwa, dwx, glcw, gscw, *flat(w), *flat(m), *flat(v))
    return {n: tuple(res[4 * k + i].reshape(w[n].shape) for i in range(4)) for k, n in enumerate(SMALL)}


def _with_own(own, got):
    chip = 2 * lax.axis_index("x") + lax.axis_index("y")
    return jnp.where((jnp.arange(4) == chip).reshape(4, 1, 1), own[None], got)


def _side_by_side(f):
    return f.transpose(1, 0, 2).reshape(f.shape[1], 4 * f.shape[2])


def _stacked(f):
    return f.reshape(4 * f.shape[1], f.shape[2])


def _gather_first_weights(w_in, lru_conv_w, ssd_conv_w):
    conv = jnp.concatenate([lru_conv_w.reshape(-1), ssd_conv_w.reshape(-1)]).astype(F32)
    hi = conv.astype(jnp.bfloat16)
    mid = (conv - hi.astype(F32)).astype(jnp.bfloat16)
    lo = (conv - hi.astype(F32) - mid.astype(F32)).astype(jnp.bfloat16)
    terms = jnp.concatenate([hi, mid, lo])
    n_terms = terms.shape[0]
    conv_rows = -(-n_terms // (128 * 32)) * 32
    terms = jnp.pad(terms, (0, conv_rows * 128 - n_terms)).reshape(conv_rows, 128)
    own = [w_in.astype(WIRE), terms]
    got = _allgather_weights(own)
    win_f = _side_by_side(_with_own(own[0], got[0]))
    t3 = _with_own(own[1], got[1]).reshape(4, -1)[:, :n_terms].reshape(4, 3, -1).astype(F32)
    conv_f = (t3[:, 0] + t3[:, 1]) + t3[:, 2]
    n1 = lru_conv_w.size
    lcw = conv_f[:, :n1].reshape(4, CONV_K, -1).transpose(1, 0, 2).reshape(CONV_K, LW)
    scw = conv_f[:, n1:].reshape(4, CONV_K, -1).transpose(1, 0, 2).reshape(CONV_K, XBC)
    return win_f, lcw, scw


def _pair_stage(tag, bufs, by_columns, small, c):
    nb = len(bufs)
    got = list(_pair_exchange("pair_exchange_" + tag, list(bufs) + list(small)))
    part = [_add_own_half("pair_add_%s%d" % (tag, k), b, r, c, WIRE, bc)
            for k, (b, r, bc) in enumerate(zip(bufs, got[:nb], by_columns))]
    part_small = list(_small_add_own_half(list(small), got[nb:], c)) if small else []
    return part, part_small


def _step(x, tgt, win_f, lcw, scw, sp, late):
    c = lax.axis_index("c")
    me = 2 * lax.axis_index("x") + lax.axis_index("y")
    mm = lambda w: w.astype(BF)
    wcat = jnp.concatenate([mm(win_f), jnp.zeros((D, PC - IN_COLS), BF)], axis=1)
    row = lambda v: v.reshape(1, -1).astype(F32)
    p_lru = jnp.concatenate([lcw, row(sp["lru_conv_b"]), row(sp["lru_ba"]), row(sp["lru_bx"]), row(sp["lru_lambda"]),
                             row(sp["lru_out_norm"]), jnp.zeros((7, LW), F32)], axis=0)
    wa4, wx4 = mm(_diag4(sp["lru_wa"][0])), mm(_diag4(sp["lru_wx"][0]))
    wa4T, wx4T = wa4.transpose(0, 2, 1), wx4.transpose(0, 2, 1)
    cw_ssd = jnp.concatenate([scw, row(sp["ssd_conv_b"]), jnp.zeros((3, XBC), F32)], axis=0)
    padh = lambda v: jnp.pad(row(v), ((0, 0), (0, DTP - NH)))
    hp_ssd = jnp.concatenate([padh(sp["ssd_dt_bias"]), padh(sp["ssd_a_log"]), padh(sp["ssd_d"]), jnp.zeros((5, DTP), F32)], axis=0)
    g0, g_ssd = row(sp["pre_mix_norm"]), row(sp["ssd_out_norm"])
    g_pm, g_pf, g_pff = row(sp["post_mix_norm"]), row(sp["pre_ffn_norm"]), row(sp["post_ffn_norm"])

    h0, lxr, lg, z, xbcr, dtr = _inproj(x, g0, wcat)
    h, ylru, lxc, *got_a = _lru_fwd(lxr, lg, p_lru, wa4, wx4, [late[0], late[3]])
    y, yssd, states, cv, *got_b = _ssd_fwd(xbcr, z, dtr, cw_ssd, hp_ssd, g_ssd, [late[1], late[2]])
    wout, wd = mm(_stacked(_with_own(late[0], got_a[0]))), mm(_stacked(_with_own(late[3], got_a[1])))
    wg, wu = mm(_side_by_side(_with_own(late[1], got_b[0]))), mm(_side_by_side(_with_own(late[2], got_b[1])))
    mix, x1, h2 = _outproj(ylru, yssd, x, wout, g_pm, g_pf)
    gate, up, act, df, dx2, st_ffn = _ffn_fwd(h2, x1, tgt, wg, wu, wd, g_pff)
    dgate, dup, dh2 = _ffn_bwd(df, gate, up, wd.T, wg.T, wu.T)
    dx1, dmix, dyl, dys, st_mix = _mix_bwd(dh2, x1, dx2, mix, wout.T, g_pf, g_pm)

    dwg = _wgrad("wgrad_gate", h2, dgate)
    dwu = _wgrad("wgrad_up", h2, dup)
    dwd = _wgrad("wgrad_down", act, df)
    dwo = jnp.concatenate([_wgrad("wgrad_out_lru", ylru, dmix), _wgrad("wgrad_out_ssd", yssd, dmix)], axis=0)
    early = [dwo.reshape(4, (LW + SI) // 4, D), dwg, dwu, dwd.reshape(4, DFF // 4, D)]
    part_early, _ = _pair_stage("early", early, [False, True, True, False], [], c)

    dlx, dlg, st_lru, dwa, dwx = _lru_bwd(dyl, lxr, lxc, lg, h, p_lru, wa4, wx4, wa4T, wx4T)
    dxbc, dz, ddt, cst, hst, gst, *slots_early = _ssd_bwd(dys, xbcr, cv, z, dtr, y, states, cw_ssd, hp_ssd, g_ssd,
                                                          part_early)
    gx, st_in = _inproj_bwd(dlx, dlg, dz, dxbc, ddt, x, dx1, wcat.T, g0)
    red_early = [_sum_slots("quad_sum_early%d" % k, p, s, me, c) for k, (p, s) in enumerate(zip(part_early, slots_early))]

    pin = [_wgrad("wgrad_in_%d" % k, h0, b) for k, b in enumerate((dlx, dlg, dz, dxbc, ddt))]
    dwin = jnp.concatenate(pin[:4] + [pin[4][:, :NH]], axis=1)
    rows = jnp.concatenate([st_in, st_lru, gst, st_mix, st_ffn], axis=0)
    small = [rows, cst, hst, dwa.reshape(NBLK * BW, BW), dwx.reshape(NBLK * BW, BW)]
    part, part_small = _pair_stage("late", [dwin], [True], small, c)
    slots = list(_quad_exchange(part + part_small, [True] + [False] * len(small)))
    red = [_sum_slots("quad_sum_late", part[0], slots[0], me, c)]
    red_small = list(_small_sum_slots(slots[1:], c))
    out = list(_pair_gather(red + red_early + red_small))
    big = dict(zip(("w_in", "w_out", "w_gate", "w_up", "w_down"), out[:5]))
    return gx, big, out[5:]


def kernel(x, pre_mix_norm, w_in, lru_conv_w, lru_conv_b, lru_wa, lru_ba, lru_wx, lru_bx, lru_lambda, lru_out_norm, ssd_conv_w, ssd_conv_b, ssd_dt_bias, ssd_a_log, ssd_d, ssd_out_norm, w_out, post_mix_norm, pre_ffn_norm, w_gate, w_up, w_down, post_ffn_norm, loss_target, m_pre_mix_norm, m_w_in, m_lru_conv_w, m_lru_conv_b, m_lru_wa, m_lru_ba, m_lru_wx, m_lru_bx, m_lru_lambda, m_lru_out_norm, m_ssd_conv_w, m_ssd_conv_b, m_ssd_dt_bias, m_ssd_a_log, m_ssd_d, m_ssd_out_norm, m_w_out, m_post_mix_norm, m_pre_ffn_norm, m_w_gate, m_w_up, m_w_down, m_post_ffn_norm, v_pre_mix_norm, v_w_in, v_lru_conv_w, v_lru_conv_b, v_lru_wa, v_lru_ba, v_lru_wx, v_lru_bx, v_lru_lambda, v_lru_out_norm, v_ssd_conv_w, v_ssd_conv_b, v_ssd_dt_bias, v_ssd_a_log, v_ssd_d, v_ssd_out_norm, v_w_out, v_post_mix_norm, v_pre_ffn_norm, v_w_gate, v_w_up, v_w_down, v_post_ffn_norm):
    args = dict(locals())
    names = list(SMALL) + list(BIG)
    w = {n: args[n] for n in names}
    m = {n: args["m_" + n] for n in names}
    v = {n: args["v_" + n] for n in names}
    chip = 2 * lax.axis_index("x") + lax.axis_index("y")

    win_f, lcw, scw = _gather_first_weights(w_in[0], lru_conv_w[0], ssd_conv_w[0])
    late = [a[0].astype(WIRE) for a in (w_out, w_gate, w_up, w_down)]
    gx, red, (rows, cst, hst, dwa, dwx) = _step(x[0], loss_target[0], win_f, lcw, scw, {n: w[n] for n in SMALL}, late)
    loss = jnp.sum(rows[LOSS_ROW])

    grads, delta, new_m, new_v = {}, {}, {}, {}
    for n in BIG:
        g = red[n]
        delta[n], new_m[n], new_v[n] = _adamw("adamw_" + n, w[n], g, m[n], v[n])
        grads[n] = g[None]

    lc, sc = lru_conv_w.shape[-1], ssd_conv_w.shape[-1]
    glcw = lax.dynamic_slice_in_dim(rows[LRU_CONV_ROWS[0]:LRU_CONV_ROWS[1]], chip * lc, lc, axis=1)
    gscw = lax.dynamic_slice_in_dim(cst[0:CONV_K], chip * sc, sc, axis=1)
    res = _adamw_small(rows, cst, hst, dwa.reshape(NBLK, BW, BW), dwx.reshape(NBLK, BW, BW), glcw, gscw,
                       {n: w[n] for n in SMALL}, {n: m[n] for n in SMALL}, {n: v[n] for n in SMALL})
    for n in SMALL:
        grads[n], delta[n], new_m[n], new_v[n] = res[n]

    order = ["pre_mix_norm", "w_in", "lru_conv_w", "lru_conv_b", "lru_wa", "lru_ba", "lru_wx", "lru_bx", "lru_lambda",
             "lru_out_norm", "ssd_conv_w", "ssd_conv_b", "ssd_dt_bias", "ssd_a_log", "ssd_d", "ssd_out_norm", "w_out",
             "post_mix_norm", "pre_ffn_norm", "w_gate", "w_up", "w_down", "post_ffn_norm"]
    return (loss, gx[None], *[grads[n] for n in order], *[delta[n] for n in order],
            *[new_m[n] for n in order], *[new_v[n] for n in order])
```

```python
import functools

import jax
import jax.numpy as jnp
from jax import lax
from jax.experimental import pallas as pl
from jax.experimental.pallas import tpu as pltpu

F32 = jnp.float32
BF = jnp.bfloat16

D = 1024
LW = 1024
NBLK = 16
BW = 64
SI = 1024
NH = 16
HD = 64
NG = 2
HPG = NH // NG
NS = 128
CH = 128
XBC = SI + 2 * NG * NS
DTP = 128
PC = 3 * 1024 + XBC + DTP
DFF = 2816
IN_COLS = 4624
EPS = 1e-6
LRU_C = 8.0
CONV_K = 4
TT = 256
VMEM_LIMIT = 56 * 1024 * 1024

ADAM_LR, ADAM_B1, ADAM_B2, ADAM_EPS, ADAM_WD, ADAM_STEP = 0.001, 0.9, 0.999, 1e-08, 0.01, 10

MESH = pl.DeviceIdType.MESH


def _mm(a, b):
    return jnp.dot(a.astype(BF), b.astype(BF), preferred_element_type=F32)


def _mm_nt(a, b):
    return lax.dot_general(a.astype(BF), b.astype(BF), (((1,), (1,)), ((), ())), preferred_element_type=F32)


def _mm_tn(a, b):
    return lax.dot_general(a.astype(BF), b.astype(BF), (((0,), (0,)), ((), ())), preferred_element_type=F32)


def _sigmoid(x):
    return 0.5 * jnp.tanh(0.5 * x) + 0.5


def _softplus(x):
    return jnp.maximum(x, 0.0) + jnp.log1p(jnp.exp(-jnp.abs(x)))


_GELU_C = 0.7978845608028654
_GELU_K = 0.044715


def _gelu(x):
    t = jnp.tanh(_GELU_C * (x + _GELU_K * x * x * x))
    return 0.5 * x * (1.0 + t)


def _gelu_grad(x):
    t = jnp.tanh(_GELU_C * (x + _GELU_K * x * x * x))
    return 0.5 * (1.0 + t) + 0.5 * x * (1.0 - t * t) * _GELU_C * (1.0 + 3.0 * _GELU_K * x * x)


def _rms_fwd(x, g):
    r = lax.rsqrt(jnp.mean(x * x, axis=-1, keepdims=True) + EPS)
    return x * r * g


def _rms_bwd(x, g, dy):
    r = lax.rsqrt(jnp.mean(x * x, axis=-1, keepdims=True) + EPS)
    xh = x * r
    dxh = dy * g
    dg = jnp.sum(dy * xh, axis=0, keepdims=True)
    dx = r * (dxh - xh * jnp.mean(dxh * xh, axis=-1, keepdims=True))
    return dx, dg


def _sum_all(x):
    return jnp.sum(jnp.sum(x, axis=1, keepdims=True), axis=0, keepdims=True)


def _cumsum_rows(x, n):
    row = lax.broadcasted_iota(jnp.int32, x.shape, 0)
    k = 1
    while k < n:
        x = x + jnp.where(row >= k, pltpu.roll(x, k, 0), 0.0)
        k *= 2
    return x


def _rev_cumsum_rows(x, n):
    row = lax.broadcasted_iota(jnp.int32, x.shape, 0)
    k = 1
    while k < n:
        x = x + jnp.where(row < n - k, pltpu.roll(x, n - k, 0), 0.0)
        k *= 2
    return x


def _load_once(pairs, sem):
    @pl.when(pl.program_id(0) == 0)
    def _():
        for k, (src, dst) in enumerate(pairs):
            pltpu.make_async_copy(src, dst, sem.at[k]).start()
        for k, (src, dst) in enumerate(pairs):
            pltpu.make_async_copy(src, dst, sem.at[k]).wait()


def _params(n_axes=1):
    return pltpu.CompilerParams(dimension_semantics=("arbitrary",) * n_axes, vmem_limit_bytes=VMEM_LIMIT)


def _rows(n, width, rev_of=None):
    if rev_of is None:
        return pl.BlockSpec((n, width), lambda i: (i, 0))
    return pl.BlockSpec((n, width), lambda i: (rev_of - 1 - i, 0))


def _whole(shape):
    nd = len(shape)
    return pl.BlockSpec(shape, lambda i: (0,) * nd)


ANY = pl.BlockSpec(memory_space=pl.ANY)
S = jax.ShapeDtypeStruct
WIRE = jnp.bfloat16


def _pos():
    return lax.axis_index("x"), lax.axis_index("y"), lax.axis_index("c")


def _other_chips(x, y):
    return [(1 - x, y), (x, 1 - y), (1 - x, 1 - y)]


def _remote(src, dst, send_sem, recv_sem, to):
    return pltpu.make_async_remote_copy(src_ref=src, dst_ref=dst, send_sem=send_sem, recv_sem=recv_sem,
                                        device_id=to, device_id_type=MESH)


def _gather_phase(phase, ins, outs, send_sems, recv_sems):
    x, y, c = _pos()
    me = 2 * x + y
    chips = _other_chips(x, y)
    for i, (src, dst) in enumerate(zip(ins, outs)):
        hr = src.shape[0] // 2
        my_half = pl.ds(pl.multiple_of(c * hr, 16), hr)
        sib_half = pl.ds(pl.multiple_of((1 - c) * hr, 16), hr)
        for k, (cx, cy) in enumerate(chips):
            s1, r1 = send_sems.at[6 * i + k], recv_sems.at[6 * i + k]
            s2, r2 = send_sems.at[6 * i + 3 + k], recv_sems.at[6 * i + 3 + k]
            first = lambda: _remote(src.at[my_half, :], dst.at[me, my_half, :], s1, r1, (cx, cy, c))
            landed = dst.at[2 * cx + cy, my_half, :]
            passed = lambda: _remote(landed, landed, s2, r2, (x, y, 1 - c))
            if phase == 0:
                first().start()
            elif phase == 1:
                _remote(landed, landed, s1, r1, (cx, cy, c)).wait_recv()
                passed().start()
            else:
                theirs = dst.at[2 * cx + cy, sib_half, :]
                _remote(theirs, theirs, s2, r2, (x, y, 1 - c)).wait_recv()
                first().wait_send()
                passed().wait_send()


def _quad_phase(phase, ins, outs, send_sems, recv_sems):
    x, y, c = _pos()
    me = 2 * x + y
    for i, (src, dst) in enumerate(zip(ins, outs)):
        for k, (cx, cy) in enumerate(_other_chips(x, y)):
            cp = _remote(src.at[2 * cx + cy], dst.at[me], send_sems.at[3 * i + k], recv_sems.at[3 * i + k], (cx, cy, c))
            if phase == 0:
                cp.start()
            else:
                got = dst.at[2 * cx + cy]
                _remote(got, got, send_sems.at[3 * i + k], recv_sems.at[3 * i + k], (cx, cy, c)).wait_recv()
                cp.wait_send()


def _prenorm(x, g0):
    T = x.shape[0]
    tt = 2 * TT

    def body(x_ref, g_ref, h0_ref):
        h0_ref[...] = _rms_fwd(x_ref[...], g_ref[...]).astype(BF)

    return pl.pallas_call(
        body, name="prenorm", grid=(T // tt,),
        in_specs=[_rows(tt, D), _whole((1, D))], out_specs=_rows(tt, D), out_shape=S((T, D), BF),
        compiler_params=_params(),
    )(x, g0)


def _blockdiag_mm(v, w4_ref):
    return jnp.concatenate([_mm(v[:, 256 * j:256 * (j + 1)], w4_ref[j]) for j in range(4)], axis=1)


def _lru_gates(lx, p_ref, wa_ref, wx_ref):
    r = _sigmoid(_blockdiag_mm(lx, wa_ref) + p_ref[5:6, :])
    i = _sigmoid(_blockdiag_mm(lx, wx_ref) + p_ref[6:7, :])
    sp = _softplus(-p_ref[7:8, :])
    la = -LRU_C * r * sp
    a = jnp.exp(la)
    th = jnp.tanh(la)
    mult = jnp.sqrt(-2.0 * th / (1.0 - th))
    return r, i, sp, a, mult


def _conv_from(xp_ref, p_ref, n):
    acc = p_ref[4:5, :] + p_ref[0:1, :] * xp_ref[pl.ds(8 - CONV_K + 1, n), :]
    for k in range(1, CONV_K):
        acc = acc + p_ref[k:k + 1, :] * xp_ref[pl.ds(8 - CONV_K + 1 + k, n), :]
    return acc


def _conv_bwd(dp_ref, dconv, x, p_ref, st_ref, n):
    dp_ref[0:n, :] = dconv
    acc = None
    for k in range(CONV_K):
        g = dp_ref[pl.ds(CONV_K - 1 - k, n), :]
        acc = p_ref[k:k + 1, :] * g if acc is None else acc + p_ref[k:k + 1, :] * g
        st_ref[k:k + 1, :] += jnp.sum(g * x, axis=0, keepdims=True)
    st_ref[4:5, :] += jnp.sum(dconv, axis=0, keepdims=True)
    dp_ref[n:n + 8, :] = dp_ref[0:8, :]
    return acc


def _lru_fwd(h0, wcat, p_lru, wa4, wx4, shards):
    T = h0.shape[0]
    NT = T // TT
    ng = len(shards)

    def body(*refs):
        h0_ref, w_hbm, p_ref, wa_ref, wx_ref = refs[:5]
        sh_in = refs[5:5 + ng]
        h_ref, y_ref, lxc_ref, lxr_ref, lg_ref = refs[5 + ng:10 + ng]
        sh_out = refs[10 + ng:10 + 2 * ng]
        xp, a_s, u_s, hc, w_vm, wsem, send_sems, recv_sems = refs[10 + 2 * ng:]
        _load_once([(w_hbm.at[:, 0:2 * LW], w_vm)], wsem)
        for phase, step in enumerate((0, NT // 2, NT - 1)):
            @pl.when(pl.program_id(0) == step)
            def _():
                _gather_phase(phase, sh_in, sh_out, send_sems, recv_sems)

        @pl.when(pl.program_id(0) == 0)
        def _():
            xp[0:8, :] = jnp.zeros((8, LW), F32)
            hc[...] = jnp.zeros_like(hc)

        hv = h0_ref[...]
        lxr = jnp.dot(hv, w_vm[:, 0:LW], preferred_element_type=F32)
        lxr_ref[...] = lxr
        lg_ref[...] = jnp.dot(hv, w_vm[:, LW:2 * LW], preferred_element_type=F32)
        xp[8:8 + TT, :] = lxr
        lx = _conv_from(xp, p_ref, TT)
        lxc_ref[...] = lx
        xp[0:8, :] = xp[TT:TT + 8, :]
        r, i, sp, a, mult = _lru_gates(lx, p_ref, wa_ref, wx_ref)
        a_s[...] = a
        u_s[...] = mult * (i * lx)

        def step(t, h):
            h = a_s[pl.ds(t, 1), :] * h + u_s[pl.ds(t, 1), :]
            h_ref[pl.ds(t, 1), :] = h
            return h

        hc[0:1, :] = lax.fori_loop(0, TT, step, hc[0:1, :], unroll=8)
        gated = h_ref[...] * _gelu(lg_ref[...])
        y_ref[...] = _rms_fwd(gated, p_ref[8:9, :]).astype(BF)

    return pl.pallas_call(
        body, name="lru_fwd", grid=(NT,),
        in_specs=[_rows(TT, D), ANY, _whole((16, LW)), _whole((4, 256, 256)), _whole((4, 256, 256))] + [ANY] * ng,
        out_specs=[_rows(TT, LW), _rows(TT, LW), _rows(TT, LW), _rows(TT, LW), _rows(TT, LW)] + [ANY] * ng,
        out_shape=[S((T, LW), F32), S((T, LW), BF), S((T, LW), F32), S((T, LW), F32), S((T, LW), F32)]
        + [S((4,) + s.shape, s.dtype) for s in shards],
        scratch_shapes=[pltpu.VMEM((TT + 8, LW), F32), pltpu.VMEM((TT, LW), F32), pltpu.VMEM((TT, LW), F32),
                        pltpu.VMEM((8, LW), F32), pltpu.VMEM((D, 2 * LW), BF), pltpu.SemaphoreType.DMA((1,)),
                        pltpu.SemaphoreType.DMA((6 * ng,)), pltpu.SemaphoreType.DMA((6 * ng,))],
        compiler_params=_params(),
    )(h0, wcat, p_lru, wa4, wx4, *shards)


def _ssd_prep(cv, dt_ref, hp_ref):
    sg = _sigmoid(cv)
    xbc = cv * sg
    lane = lax.broadcasted_iota(jnp.int32, (CH, DTP), 1)
    raw = dt_ref[...] + hp_ref[0:1, :]
    dtv = jnp.where(lane < NH, _softplus(raw), 0.0)
    A = jnp.where(lane[0:1, :] < NH, -jnp.exp(hp_ref[1:2, :]), 0.0)
    cs = _cumsum_rows(dtv * A, CH)
    return sg, xbc, raw, dtv, A, cs


def _per_head_lanes(v):
    r = v.shape[0]
    first = lax.broadcasted_iota(jnp.int32, (r, 2 * HD), 1) < HD
    pairs = [jnp.where(first, jnp.broadcast_to(v[:, 2 * j:2 * j + 1], (r, 2 * HD)),
                       jnp.broadcast_to(v[:, 2 * j + 1:2 * j + 2], (r, 2 * HD))) for j in range(NH // 2)]
    return jnp.concatenate(pairs, axis=1)


def _per_head_rows(col, g):
    return jnp.concatenate([jnp.broadcast_to(col[g * HPG + k:g * HPG + k + 1, :], (HD, NS)) for k in range(HPG)], axis=0)


def _ssd_decays(cs):
    csT = cs.T
    cl = cs[CH - 1:CH, :]
    E_x = _per_head_lanes(jnp.exp(cs))
    dsm = jnp.exp(cl - cs)
    ds_x = _per_head_lanes(dsm)
    El_rows = jnp.broadcast_to(jnp.exp(csT[0:NH, CH - 1:CH]), (NH, NS))
    return csT, dsm, E_x, ds_x, El_rows


def _ssd_fwd(h0, wcat, cw_ssd, hp_ssd, g_ssd, shards):
    T = h0.shape[0]
    NC = T // CH
    ng = len(shards)
    c0 = 2 * LW

    def body(*refs):
        h0_ref, w_hbm, cw_ref, hp_ref, g_ref = refs[:5]
        sh_in = refs[5:5 + ng]
        y_ref, yn_ref, st_ref, cv_ref, z_ref, xr_ref, dt_ref = refs[5 + ng:12 + ng]
        sh_out = refs[12 + ng:12 + 2 * ng]
        xp, st, w_vm, wsem, send_sems, recv_sems = refs[12 + 2 * ng:]
        _load_once([(w_hbm.at[:, c0:PC], w_vm)], wsem)
        for phase, step in enumerate((0, NC // 2, NC - 1)):
            @pl.when(pl.program_id(0) == step)
            def _():
                _gather_phase(phase, sh_in, sh_out, send_sems, recv_sems)

        @pl.when(pl.program_id(0) == 0)
        def _():
            xp[0:8, :] = jnp.zeros((8, XBC), F32)
            st[...] = jnp.zeros_like(st)

        hv = h0_ref[...]
        z_ref[...] = jnp.dot(hv, w_vm[:, 0:SI], preferred_element_type=F32)
        xraw = jnp.dot(hv, w_vm[:, SI:SI + XBC], preferred_element_type=F32)
        xr_ref[...] = xraw
        dt_ref[...] = jnp.dot(hv, w_vm[:, SI + XBC:SI + XBC + DTP], preferred_element_type=F32)
        xp[8:8 + CH, :] = xraw
        cv = _conv_from(xp, cw_ref, CH)
        cv_ref[...] = cv
        sg, xbc, raw, dtv, A, cs = _ssd_prep(cv, dt_ref, hp_ref)
        xp[0:8, :] = xp[CH:CH + 8, :]
        st_ref[0] = st[...]
        csT, dsm, E_x, ds_x, El_rows = _ssd_decays(cs)
        X = xbc[:, 0:SI]
        xs = X * _per_head_lanes(dtv)
        xsd = (xs * ds_x).astype(BF)
        DX = _per_head_lanes(hp_ref[...])[2:3, :] * X
        tril = lax.broadcasted_iota(jnp.int32, (CH, CH), 0) >= lax.broadcasted_iota(jnp.int32, (CH, CH), 1)
        first = lax.broadcasted_iota(jnp.int32, (CH, 2 * HD), 1) < HD
        GW = HPG * HD
        for g in range(NG):
            Bg = xbc[:, SI + NS * g:SI + NS * (g + 1)].astype(BF)
            Cg = xbc[:, SI + NG * NS + NS * g:SI + NG * NS + NS * (g + 1)].astype(BF)
            G = _mm_nt(Cg, Bg)
            Sg = st[GW * g:GW * (g + 1), :]
            Yo = _mm_nt(Cg, Sg) * E_x[:, GW * g:GW * (g + 1)]
            st[GW * g:GW * (g + 1), :] = _per_head_rows(El_rows, g) * Sg + _mm_tn(xsd[:, GW * g:GW * (g + 1)], Bg)
            for jj in range(HPG // 2):
                j = g * (HPG // 2) + jj
                ps = slice(2 * HD * j, 2 * HD * (j + 1))
                xs_pair = xs[:, ps]
                acc = Yo[:, 2 * HD * jj:2 * HD * (jj + 1)] + DX[:, ps]
                for e in range(2):
                    h = 2 * j + e
                    Lm = jnp.exp(jnp.where(tril, cs[:, h:h + 1] - csT[h:h + 1, :], -1e30))
                    acc = acc + _mm(G * Lm, jnp.where(first if e == 0 else ~first, xs_pair, 0.0))
                y_ref[:, ps] = acc
        zz = z_ref[...]
        gated = y_ref[...] * (zz * _sigmoid(zz))
        yn_ref[...] = _rms_fwd(gated, g_ref[...]).astype(BF)

    return pl.pallas_call(
        body, name="ssd_fwd", grid=(NC,),
        in_specs=[_rows(CH, D), ANY, _whole((8, XBC)), _whole((8, DTP)), _whole((1, SI))] + [ANY] * ng,
        out_specs=[_rows(CH, SI), _rows(CH, SI), pl.BlockSpec((1, NH * HD, NS), lambda i: (i, 0, 0)), _rows(CH, XBC),
                   _rows(CH, SI), _rows(CH, XBC), _rows(CH, DTP)] + [ANY] * ng,
        out_shape=[S((T, SI), F32), S((T, SI), BF), S((NC, NH * HD, NS), F32), S((T, XBC), F32),
                   S((T, SI), F32), S((T, XBC), F32), S((T, DTP), F32)] + [S((4,) + s.shape, s.dtype) for s in shards],
        scratch_shapes=[pltpu.VMEM((CH + 8, XBC), F32), pltpu.VMEM((NH * HD, NS), F32),
                        pltpu.VMEM((D, PC - c0), BF), pltpu.SemaphoreType.DMA((1,)),
                        pltpu.SemaphoreType.DMA((6 * ng,)), pltpu.SemaphoreType.DMA((6 * ng,))],
        compiler_params=_params(),
    )(h0, wcat, cw_ssd, hp_ssd, g_ssd, *shards)


def _outproj(ylru, yssd, x, wout, g_pm, g_pf):
    T = x.shape[0]

    def body(yl_ref, ys_ref, x_ref, w_hbm, gpm_ref, gpf_ref, mix_ref, x1_ref, h2_ref, w_vm, sem):
        _load_once([(w_hbm, w_vm)], sem)
        mix = (jnp.dot(yl_ref[...], w_vm[0:LW, :], preferred_element_type=F32)
               + jnp.dot(ys_ref[...], w_vm[LW:LW + SI, :], preferred_element_type=F32))
        mix_ref[...] = mix
        x1 = x_ref[...] + _rms_fwd(mix, gpm_ref[...])
        x1_ref[...] = x1
        h2_ref[...] = _rms_fwd(x1, gpf_ref[...]).astype(BF)

    return pl.pallas_call(
        body, name="outproj", grid=(T // TT,),
        in_specs=[_rows(TT, LW), _rows(TT, SI), _rows(TT, D), ANY, _whole((1, D)), _whole((1, D))],
        out_specs=[_rows(TT, D), _rows(TT, D), _rows(TT, D)],
        out_shape=[S((T, D), F32), S((T, D), F32), S((T, D), BF)],
        scratch_shapes=[pltpu.VMEM((LW + SI, D), BF), pltpu.SemaphoreType.DMA((1,))],
        compiler_params=_params(),
    )(ylru, yssd, x, wout, g_pm, g_pf)


def _ffn_fwd(h2, x1, tgt, wg, wu, wd, g_pff):
    T = x1.shape[0]

    def body(h2_ref, x1_ref, t_ref, wg_hbm, wu_hbm, wd_hbm, g_ref,
             gate_ref, up_ref, act_ref, df_ref, dx2_ref, st_ref, wg_vm, wu_vm, wd_vm, sem):
        _load_once([(wg_hbm, wg_vm), (wu_hbm, wu_vm), (wd_hbm, wd_vm)], sem)

        @pl.when(pl.program_id(0) == 0)
        def _():
            st_ref[...] = jnp.zeros_like(st_ref)

        h2 = h2_ref[...]
        gate = jnp.dot(h2, wg_vm[...], preferred_element_type=F32)
        up = jnp.dot(h2, wu_vm[...], preferred_element_type=F32)
        gate_ref[...] = gate
        up_ref[...] = up
        act = (gate * _sigmoid(gate) * up).astype(BF)
        act_ref[...] = act
        f = jnp.dot(act, wd_vm[...], preferred_element_type=F32)
        g = g_ref[...]
        x2 = x1_ref[...] + _rms_fwd(f, g)
        err = x2 - t_ref[...]
        st_ref[0:1, :] += 0.5 * jnp.sum(err * err, axis=0, keepdims=True) * (1.0 / D)
        dx2 = err * (1.0 / D)
        dx2_ref[...] = dx2
        df, dg = _rms_bwd(f, g, dx2)
        df_ref[...] = df.astype(BF)
        st_ref[1:2, :] += dg

    return pl.pallas_call(
        body, name="ffn_fwd", grid=(T // TT,),
        in_specs=[_rows(TT, D), _rows(TT, D), _rows(TT, D), ANY, ANY, ANY, _whole((1, D))],
        out_specs=[_rows(TT, DFF), _rows(TT, DFF), _rows(TT, DFF), _rows(TT, D), _rows(TT, D), _whole((8, D))],
        out_shape=[S((T, DFF), F32), S((T, DFF), F32), S((T, DFF), BF), S((T, D), BF), S((T, D), F32), S((8, D), F32)],
        scratch_shapes=[pltpu.VMEM((D, DFF), BF), pltpu.VMEM((D, DFF), BF), pltpu.VMEM((DFF, D), BF),
                        pltpu.SemaphoreType.DMA((3,))],
        compiler_params=_params(),
    )(h2, x1, tgt, wg, wu, wd, g_pff)


def _ffn_bwd(df, gate, up, wdT, wgT, wuT):
    T = df.shape[0]

    def body(df_ref, gate_ref, up_ref, wd_hbm, wg_hbm, wu_hbm, dgate_ref, dup_ref, dh2_ref, wd_vm, wg_vm, wu_vm, sem):
        _load_once([(wd_hbm, wd_vm), (wg_hbm, wg_vm), (wu_hbm, wu_vm)], sem)
        dact = jnp.dot(df_ref[...], wd_vm[...], preferred_element_type=F32)
        gate = gate_ref[...]
        s = _sigmoid(gate)
        dup = (dact * (gate * s)).astype(BF)
        dgate = (dact * up_ref[...] * (s + gate * s * (1.0 - s))).astype(BF)
        dup_ref[...] = dup
        dgate_ref[...] = dgate
        dh2_ref[...] = (jnp.dot(dgate, wg_vm[...], preferred_element_type=F32)
                        + jnp.dot(dup, wu_vm[...], preferred_element_type=F32))

    return pl.pallas_call(
        body, name="ffn_bwd", grid=(T // TT,),
        in_specs=[_rows(TT, D), _rows(TT, DFF), _rows(TT, DFF), ANY, ANY, ANY],
        out_specs=[_rows(TT, DFF), _rows(TT, DFF), _rows(TT, D)],
        out_shape=[S((T, DFF), BF), S((T, DFF), BF), S((T, D), F32)],
        scratch_shapes=[pltpu.VMEM((D, DFF), BF), pltpu.VMEM((DFF, D), BF), pltpu.VMEM((DFF, D), BF),
                        pltpu.SemaphoreType.DMA((3,))],
        compiler_params=_params(),
    )(df, gate, up, wdT, wgT, wuT)


def _mix_bwd(dh2, x1, dx2, mix, g_pf, g_pm):
    T = x1.shape[0]

    def body(dh2_ref, x1_ref, dx2_ref, mix_ref, gpf_ref, gpm_ref, dx1_ref, dmix_ref, st_ref):
        @pl.when(pl.program_id(0) == 0)
        def _():
            st_ref[...] = jnp.zeros_like(st_ref)

        dxa, dgpf = _rms_bwd(x1_ref[...], gpf_ref[...], dh2_ref[...])
        dx1 = dx2_ref[...] + dxa
        dx1_ref[...] = dx1
        dmix, dgpm = _rms_bwd(mix_ref[...], gpm_ref[...], dx1)
        dmix_ref[...] = dmix.astype(BF)
        st_ref[0:1, :] += dgpf
        st_ref[1:2, :] += dgpm

    return pl.pallas_call(
        body, name="mix_bwd", grid=(T // TT,),
        in_specs=[_rows(TT, D), _rows(TT, D), _rows(TT, D), _rows(TT, D), _whole((1, D)), _whole((1, D))],
        out_specs=[_rows(TT, D), _rows(TT, D), _whole((8, D))],
        out_shape=[S((T, D), F32), S((T, D), BF), S((8, D), F32)],
        compiler_params=_params(),
    )(dh2, x1, dx2, mix, g_pf, g_pm)


def _halo(width, n_tiles, tile):
    per = tile // 8
    return pl.BlockSpec((8, width), lambda i: (jnp.maximum((n_tiles - 1 - i) * per - 1, 0), 0))


def _lru_bwd(dmix, woutT, lxr, lxc, lg, h, p_lru, wa4, wx4, wa4T, wx4T):
    T = dmix.shape[0]
    NT = T // TT

    def body(dmix_ref, wo_hbm, lxr_ref, lxc_ref, lg_ref, h_ref, hh_ref, p_ref, wa_ref, wx_ref, waT_ref, wxT_ref,
             dlx_ref, dlg_ref, st_ref, dwa_ref, dwx_ref, hp, dp, a_s, d_s, g_s, cc, wo_vm, wsem):
        _load_once([(wo_hbm.at[:, 0:LW], wo_vm)], wsem)
        dy = jnp.dot(dmix_ref[...], wo_vm[...], preferred_element_type=F32)
        first = pl.program_id(0) == 0
        top = pl.program_id(0) == NT - 1

        @pl.when(first)
        def _():
            st_ref[...] = jnp.zeros_like(st_ref)
            dwa_ref[...] = jnp.zeros_like(dwa_ref)
            dwx_ref[...] = jnp.zeros_like(dwx_ref)
            dp[TT:TT + 8, :] = jnp.zeros((8, LW), F32)
            cc[...] = jnp.zeros_like(cc)

        hp[0:8, :] = hh_ref[...] * jnp.where(top, 0.0, 1.0)
        hp[8:8 + TT, :] = h_ref[...]
        lx = lxc_ref[...]
        r, i, sp, a, mult = _lru_gates(lx, p_ref, wa_ref, wx_ref)

        lg = lg_ref[...]
        hcur = h_ref[...]
        ge = _gelu(lg)
        dgated, dgn = _rms_bwd(hcur * ge, p_ref[8:9, :], dy)
        st_ref[8:9, :] += dgn
        dlg_ref[...] = (dgated * hcur * _gelu_grad(lg)).astype(BF)
        a_s[...] = a
        d_s[...] = dgated * ge

        def step(k, c):
            t = TT - 1 - k
            g = d_s[pl.ds(t, 1), :] + c
            g_s[pl.ds(t, 1), :] = g
            return a_s[pl.ds(t, 1), :] * g

        cc[0:1, :] = lax.fori_loop(0, TT, step, cc[0:1, :], unroll=8)
        gt = g_s[...]
        da = gt * hp[pl.ds(7, TT), :]
        dmult = gt * i * lx
        di = gt * mult * lx
        dlxc = gt * mult * i
        dla = da * a - dmult * (a * a) / mult
        dr = dla * (-LRU_C * sp)
        st_ref[7:8, :] += jnp.sum(dla * (-LRU_C * r), axis=0, keepdims=True) * (-_sigmoid(-p_ref[7:8, :]))
        dzr = dr * r * (1.0 - r)
        dzi = di * i * (1.0 - i)
        st_ref[5:6, :] += jnp.sum(dzr, axis=0, keepdims=True)
        st_ref[6:7, :] += jnp.sum(dzi, axis=0, keepdims=True)
        dlxc = dlxc + _blockdiag_mm(dzr, waT_ref) + _blockdiag_mm(dzi, wxT_ref)
        for j in range(4):
            sl = slice(256 * j, 256 * (j + 1))
            pa = _mm_tn(lx[:, sl], dzr[:, sl])
            px = _mm_tn(lx[:, sl], dzi[:, sl])
            for b in range(4):
                bs = slice(BW * b, BW * (b + 1))
                dwa_ref[4 * j + b] += pa[bs, bs]
                dwx_ref[4 * j + b] += px[bs, bs]
        dlx_ref[...] = _conv_bwd(dp, dlxc, lxr_ref[...], p_ref, st_ref, TT).astype(BF)

    w4 = _whole((4, 256, 256))
    return pl.pallas_call(
        body, name="lru_bwd", grid=(NT,),
        in_specs=[_rows(TT, D, NT), ANY, _rows(TT, LW, NT), _rows(TT, LW, NT), _rows(TT, LW, NT), _rows(TT, LW, NT),
                  _halo(LW, NT, TT), _whole((16, LW)), w4, w4, w4, w4],
        out_specs=[_rows(TT, LW, NT), _rows(TT, LW, NT), _whole((16, LW)), _whole((NBLK, BW, BW)), _whole((NBLK, BW, BW))],
        out_shape=[S((T, LW), BF), S((T, LW), BF), S((16, LW), F32), S((NBLK, BW, BW), F32), S((NBLK, BW, BW), F32)],
        scratch_shapes=[pltpu.VMEM((TT + 8, LW), F32), pltpu.VMEM((TT + 8, LW), F32),
                        pltpu.VMEM((TT, LW), F32), pltpu.VMEM((TT, LW), F32), pltpu.VMEM((TT, LW), F32),
                        pltpu.VMEM((8, LW), F32), pltpu.VMEM((D, LW), BF), pltpu.SemaphoreType.DMA((1,))],
        compiler_params=_params(),
    )(dmix, woutT, lxr, lxc, lg, h, h, p_lru, wa4, wx4, wa4T, wx4T)


def _ssd_bwd(dmix, woutT, xbcr, cv, z, dtr, y, states, cw_ssd, hp_ssd, g_ssd, parts):
    T = dmix.shape[0]
    NC = T // CH
    nq = len(parts)

    def body(*refs):
        dmix_ref, wo_hbm, xr_ref, cv_ref, z_ref, dt_ref, y_ref, st_ref, cw_ref, hp_ref, g_ref = refs[:11]
        q_in = refs[11:11 + nq]
        dxbc_ref, dz_ref, ddt_ref, cst_ref, hst_ref, gst_ref = refs[11 + nq:17 + nq]
        q_out = refs[17 + nq:17 + 2 * nq]
        dp, dS, dxb, yo_s, q_s, dxs_s, t1_s, wo_vm, wsem, send_sems, recv_sems = refs[17 + 2 * nq:]
        _load_once([(wo_hbm.at[:, LW:LW + SI], wo_vm)], wsem)
        dyn = jnp.dot(dmix_ref[...], wo_vm[...], preferred_element_type=F32)
        first = pl.program_id(0) == 0
        for phase, step in enumerate((0, NC - 1)):
            @pl.when(pl.program_id(0) == step)
            def _():
                _quad_phase(phase, q_in, q_out, send_sems, recv_sems)

        @pl.when(first)
        def _():
            cst_ref[...] = jnp.zeros_like(cst_ref)
            hst_ref[...] = jnp.zeros_like(hst_ref)
            gst_ref[...] = jnp.zeros_like(gst_ref)
            dp[CH:CH + 8, :] = jnp.zeros((8, XBC), F32)
            dS[...] = jnp.zeros_like(dS)

        cv = cv_ref[...]
        sg, xbc, raw, dtv, A, cs = _ssd_prep(cv, dt_ref, hp_ref)
        csT, dsm, E_x, ds_x, El_rows = _ssd_decays(cs)
        row_i = lax.broadcasted_iota(jnp.int32, (CH, CH), 0)
        col_i = lax.broadcasted_iota(jnp.int32, (CH, CH), 1)
        tril = row_i >= col_i
        first = col_i < HD
        head_of = ((lax.broadcasted_iota(jnp.int32, (DTP, SI), 1) >> 6)
                   == lax.broadcasted_iota(jnp.int32, (DTP, SI), 0)).astype(BF)
        head_ofT = ((lax.broadcasted_iota(jnp.int32, (SI, DTP), 0) >> 6)
                    == lax.broadcasted_iota(jnp.int32, (SI, DTP), 1)).astype(BF)

        def hi_lo(v):
            hi = v.astype(BF)
            return hi, (v - hi.astype(F32)).astype(BF)

        def lane_sums(v):
            hi, lo = hi_lo(v)
            return _mm(hi, head_ofT) + _mm(lo, head_ofT)

        zz = z_ref[...]
        sz = _sigmoid(zz)
        yv = y_ref[...]
        dgn, dg = _rms_bwd(yv * (zz * sz), g_ref[...], dyn)
        gst_ref[0:1, :] += dg
        dz_ref[...] = (dgn * yv * (sz + zz * sz * (1.0 - sz))).astype(BF)
        dY = dgn * (zz * sz)

        X = xbc[:, 0:SI]
        dt_x = _per_head_lanes(dtv)
        xs = X * dt_x
        xsd = (xs * ds_x).astype(BF)
        D_x = _per_head_lanes(hp_ref[...])[2:3, :]
        dcs_col = jnp.zeros((CH, DTP), F32)
        dcs_row = jnp.zeros((CH, DTP), F32)
        GW = HPG * HD
        for g in range(NG):
            gs = slice(GW * g, GW * (g + 1))
            Bg = xbc[:, SI + NS * g:SI + NS * (g + 1)].astype(BF)
            Cg = xbc[:, SI + NG * NS + NS * g:SI + NG * NS + NS * (g + 1)].astype(BF)
            G = _mm_nt(Cg, Bg)
            Sg = st_ref[0, gs, :]
            dSe = dS[gs, :]
            dYg = dY[:, gs]
            yo_s[:, gs] = _mm_nt(Cg, Sg) * E_x[:, gs]
            dP = dYg * E_x[:, gs]
            dCg = _mm(dP, Sg)
            dS[gs, :] = _mm_tn(dP, Cg) + _per_head_rows(El_rows, g) * dSe
            t1_s[gs, :] = dSe * Sg
            Q = _mm_nt(Bg, dSe)
            q_s[:, gs] = Q
            dBg = _mm(xsd[:, gs], dSe)
            dG = jnp.zeros((CH, CH), F32)
            for jj in range(HPG // 2):
                j = g * (HPG // 2) + jj
                ps = slice(2 * HD * j, 2 * HD * (j + 1))
                xs_pair = xs[:, ps]
                dxs_pair = Q[:, 2 * HD * jj:2 * HD * (jj + 1)] * ds_x[:, ps]
                for e in range(2):
                    h = 2 * j + e
                    Lm = jnp.exp(jnp.where(tril, cs[:, h:h + 1] - csT[h:h + 1, :], -1e30))
                    M = G * Lm
                    dYm = jnp.where(first if e == 0 else ~first, dY[:, ps], 0.0).astype(BF)
                    dM = _mm_nt(dYm, xs_pair)
                    dxs_pair = dxs_pair + _mm_tn(M, dYm)
                    Wm = dM * M
                    dcs_col = dcs_col + jnp.where(col_i == h, jnp.sum(Wm, axis=1, keepdims=True), 0.0)
                    dcs_row = dcs_row + jnp.where(row_i == h, -jnp.sum(Wm, axis=0, keepdims=True), 0.0)
                    dG = dG + dM * Lm
                dxs_s[:, ps] = dxs_pair
            dxb[:, SI + NS * g:SI + NS * (g + 1)] = dBg + _mm_tn(dG, Cg)
            dxb[:, SI + NG * NS + NS * g:SI + NG * NS + NS * (g + 1)] = dCg + _mm(dG, Bg)

        dxs = dxs_s[...]
        dxb[:, 0:SI] = D_x * dY + dxs * dt_x
        dds = lane_sums(q_s[...] * xs) * dsm
        dcs_col = dcs_col + lane_sums(dY * yo_s[...]) - dds
        ddt_col = lane_sums(dxs * X)
        dD = jnp.sum(lane_sums(dY * X), axis=0, keepdims=True)
        t_hi, t_lo = hi_lo(t1_s[...])
        dcl_rows = jnp.sum(_mm(head_of, t_hi) + _mm(head_of, t_lo), axis=1, keepdims=True) * jnp.exp(csT[:, CH - 1:CH])
        dcs_row = dcs_row + jnp.where(col_i == CH - 1, dcl_rows, 0.0)
        dcs_col = dcs_col + jnp.where(row_i == CH - 1, jnp.sum(dds, axis=0, keepdims=True), 0.0)

        da = _rev_cumsum_rows(dcs_col + dcs_row.T, CH)
        ddt_col = ddt_col + da * A
        hst_ref[1:2, :] += jnp.sum(da * dtv, axis=0, keepdims=True) * A
        hst_ref[2:3, :] += dD
        draw = jnp.where(col_i < NH, ddt_col * _sigmoid(raw), 0.0)
        ddt_ref[...] = draw.astype(BF)
        hst_ref[0:1, :] += jnp.sum(draw, axis=0, keepdims=True)

        dcv = dxb[...] * (sg + cv * sg * (1.0 - sg))
        dxbc_ref[...] = _conv_bwd(dp, dcv, xr_ref[...], cw_ref, cst_ref, CH).astype(BF)

    return pl.pallas_call(
        body, name="ssd_bwd", grid=(NC,),
        in_specs=[_rows(CH, D, NC), ANY, _rows(CH, XBC, NC), _rows(CH, XBC, NC), _rows(CH, SI, NC), _rows(CH, DTP, NC),
                  _rows(CH, SI, NC), pl.BlockSpec((1, NH * HD, NS), lambda i: (NC - 1 - i, 0, 0)),
                  _whole((8, XBC)), _whole((8, DTP)), _whole((1, SI))] + [ANY] * nq,
        out_specs=[_rows(CH, XBC, NC), _rows(CH, SI, NC), _rows(CH, DTP, NC), _whole((16, XBC)), _whole((16, DTP)),
                   _whole((8, SI))] + [ANY] * nq,
        out_shape=[S((T, XBC), BF), S((T, SI), BF), S((T, DTP), BF), S((16, XBC), F32), S((16, DTP), F32), S((8, SI), F32)]
        + [S(p.shape, p.dtype) for p in parts],
        scratch_shapes=[pltpu.VMEM((CH + 8, XBC), F32), pltpu.VMEM((NH * HD, NS), F32),
                        pltpu.VMEM((CH, XBC), F32), pltpu.VMEM((CH, SI), F32), pltpu.VMEM((CH, SI), F32),
                        pltpu.VMEM((CH, SI), F32), pltpu.VMEM((NH * HD, NS), F32),
                        pltpu.VMEM((D, SI), BF), pltpu.SemaphoreType.DMA((1,)),
                        pltpu.SemaphoreType.DMA((3 * nq,)), pltpu.SemaphoreType.DMA((3 * nq,))],
        compiler_params=_params(),
    )(dmix, woutT, xbcr, cv, z, dtr, y, states, cw_ssd, hp_ssd, g_ssd, *parts)


def _inproj_bwd(dlx, dlg, dz, dxbc, ddt, x, dx1, wcatT, g0):
    T = x.shape[0]

    def body(dlx_ref, dlg_ref, dz_ref, dxbc_ref, ddt_ref, x_ref, dx1_ref, w_hbm, g_ref, dx_ref, st_ref, w_vm, sem):
        _load_once([(w_hbm, w_vm)], sem)

        @pl.when(pl.program_id(0) == 0)
        def _():
            st_ref[...] = jnp.zeros_like(st_ref)

        dh = jnp.dot(dlx_ref[...], w_vm[0:1024, :], preferred_element_type=F32)
        dh = dh + jnp.dot(dlg_ref[...], w_vm[1024:2048, :], preferred_element_type=F32)
        dh = dh + jnp.dot(dz_ref[...], w_vm[2048:3072, :], preferred_element_type=F32)
        dh = dh + jnp.dot(dxbc_ref[...], w_vm[3072:3072 + XBC, :], preferred_element_type=F32)
        dh = dh + jnp.dot(ddt_ref[...], w_vm[3072 + XBC:PC, :], preferred_element_type=F32)
        dx, dg = _rms_bwd(x_ref[...], g_ref[...], dh)
        dx_ref[...] = dx1_ref[...] + dx
        st_ref[0:1, :] += dg

    return pl.pallas_call(
        body, name="inproj_bwd", grid=(T // TT,),
        in_specs=[_rows(TT, 1024), _rows(TT, 1024), _rows(TT, 1024), _rows(TT, XBC), _rows(TT, DTP), _rows(TT, D),
                  _rows(TT, D), ANY, _whole((1, D))],
        out_specs=[_rows(TT, D), _whole((8, D))],
        out_shape=[S((T, D), F32), S((8, D), F32)],
        scratch_shapes=[pltpu.VMEM((PC, D), BF), pltpu.SemaphoreType.DMA((1,))],
        compiler_params=_params(),
    )(dlx, dlg, dz, dxbc, ddt, x, dx1, wcatT, g0)


def _wgrad(name, a, b):
    T, M = a.shape
    N = b.shape[1]
    tk = min(T, 2048 if M <= 1024 else 1024)
    tn = N
    while M * tn * 4 > (6 << 20) and tn % 256 == 0:
        tn //= 2

    def body(a_ref, b_ref, o_ref):
        p = lax.dot_general(a_ref[...], b_ref[...], (((0,), (0,)), ((), ())), preferred_element_type=F32)

        @pl.when(pl.program_id(1) == 0)
        def _():
            o_ref[...] = p

        @pl.when(pl.program_id(1) > 0)
        def _():
            o_ref[...] += p

    return pl.pallas_call(
        body, name=name, grid=(N // tn, T // tk),
        in_specs=[pl.BlockSpec((tk, M), lambda j, k: (k, 0)), pl.BlockSpec((tk, tn), lambda j, k: (k, j))],
        out_specs=pl.BlockSpec((M, tn), lambda j, k: (0, j)), out_shape=S((M, N), F32),
        compiler_params=_params(2),
    )(a, b)


def _adamw(name, w, g, m, v):
    _, R, C = w.shape
    tr = _row_tile(R, C)

    def body(w_ref, g_ref, m_ref, v_ref, d_ref, nm_ref, nv_ref):
        d_ref[0], nm_ref[0], nv_ref[0] = _adam_math(w_ref[0], g_ref[...], m_ref[0], v_ref[0])

    blk = pl.BlockSpec((1, tr, C), lambda i: (0, i, 0))
    return pl.pallas_call(
        body, name=name, grid=(R // tr,),
        in_specs=[blk, pl.BlockSpec((tr, C), lambda i: (i, 0)), blk, blk], out_specs=[blk] * 3,
        out_shape=[S((1, R, C), F32)] * 3, compiler_params=_params(),
    )(w, g, m, v)


def _half(ref, c, hr):
    sl = pl.ds(pl.multiple_of(c * hr, 8), hr)
    return ref.at[:, sl, :] if len(ref.shape) == 3 else ref.at[sl, :]


def _allgather_weights(shards):
    n = len(shards)

    def body(*refs):
        for phase in range(3):
            _gather_phase(phase, refs[:n], refs[n:2 * n], refs[2 * n], refs[2 * n + 1])

    return pl.pallas_call(
        body, name="allgather_weights", in_specs=[ANY] * n, out_specs=[ANY] * n,
        out_shape=[S((4,) + s.shape, s.dtype) for s in shards],
        scratch_shapes=[pltpu.SemaphoreType.DMA((6 * n,)), pltpu.SemaphoreType.DMA((6 * n,))],
    )(*shards)


def _pair_exchange(name, bufs):
    n = len(bufs)

    def half_shape(b):
        return b.shape[:-2] + (b.shape[-2] // 2, b.shape[-1])

    def body(*refs):
        ins, outs = refs[:n], refs[n:2 * n]
        send_sems, recv_sems = refs[2 * n], refs[2 * n + 1]
        x, y, c = _pos()
        copies = [_remote(_half(src, 1 - c, src.shape[-2] // 2), dst, send_sems.at[k], recv_sems.at[k], (x, y, 1 - c))
                  for k, (src, dst) in enumerate(zip(ins, outs))]
        for cp in copies:
            cp.start()
        for cp in copies:
            cp.wait()

    return pl.pallas_call(
        body, name=name, in_specs=[ANY] * n, out_specs=[ANY] * n,
        out_shape=[S(half_shape(b), b.dtype) for b in bufs],
        scratch_shapes=[pltpu.SemaphoreType.DMA((n,)), pltpu.SemaphoreType.DMA((n,))],
    )(*bufs)


def _quad_exchange(bufs, scatter):
    n = len(bufs)

    def body(*refs):
        ins, outs = refs[:n], refs[n:2 * n]
        send_sems, recv_sems, local_sems = refs[2 * n], refs[2 * n + 1], refs[2 * n + 2]
        x, y, c = _pos()
        me = 2 * x + y
        chips = _other_chips(x, y)
        copies, locals_ = [], []
        for k, (src, dst) in enumerate(zip(ins, outs)):
            if not scatter[k]:
                own = pltpu.make_async_copy(src, dst.at[me], local_sems.at[k])
                own.start()
                locals_.append(own)
            for j, (cx, cy) in enumerate(chips):
                piece = src.at[2 * cx + cy] if scatter[k] else src
                cp = _remote(piece, dst.at[me], send_sems.at[3 * k + j], recv_sems.at[3 * k + j], (cx, cy, c))
                cp.start()
                copies.append(cp)
        for k, (src, dst) in enumerate(zip(ins, outs)):
            for j, (cx, cy) in enumerate(chips):
                blk = dst.at[2 * cx + cy]
                _remote(blk, blk, send_sems.at[3 * k + j], recv_sems.at[3 * k + j], (cx, cy, c)).wait_recv()
        for cp in copies:
            cp.wait_send()
        for cp in locals_:
            cp.wait()

    return pl.pallas_call(
        body, name="quad_exchange", in_specs=[ANY] * n, out_specs=[ANY] * n,
        out_shape=[S((4,) + (b.shape[1:] if sc else b.shape), b.dtype) for b, sc in zip(bufs, scatter)],
        scratch_shapes=[pltpu.SemaphoreType.DMA((3 * n,)), pltpu.SemaphoreType.DMA((3 * n,)), pltpu.SemaphoreType.DMA((n,))],
    )(*bufs)


def _pair_gather(bufs):
    n = len(bufs)

    def body(*refs):
        ins, outs = refs[:n], refs[n:2 * n]
        send_sems, recv_sems = refs[2 * n], refs[2 * n + 1]
        x, y, c = _pos()
        copies = []
        for k, buf in enumerate(outs):
            mine = _half(buf, c, buf.shape[0] // 2)
            cp = _remote(mine, mine, send_sems.at[k], recv_sems.at[k], (x, y, 1 - c))
            cp.start()
            copies.append(cp)
        for k, buf in enumerate(outs):
            theirs = _half(buf, 1 - c, buf.shape[0] // 2)
            _remote(theirs, theirs, send_sems.at[k], recv_sems.at[k], (x, y, 1 - c)).wait_recv()
        for cp in copies:
            cp.wait_send()

    return pl.pallas_call(
        body, name="pair_gather", in_specs=[ANY] * n, out_specs=[ANY] * n,
        out_shape=[S(b.shape, b.dtype) for b in bufs], input_output_aliases={k: k for k in range(n)},
        scratch_shapes=[pltpu.SemaphoreType.DMA((n,)), pltpu.SemaphoreType.DMA((n,))],
    )(*bufs)


def _row_tile(rows, cols, mult=8):
    best = mult
    for t in range(mult, rows + 1, mult):
        if rows % t == 0 and t * cols * 4 <= (1 << 20):
            best = t
    return best


def _add_own_half(name, full, got, c, out_dtype, by_columns):
    hr = got.shape[-2]
    wide = got.shape[-1]
    cols = wide // 4 if by_columns else wide
    tr = _row_tile(hr, wide, 16)
    per = hr // tr

    if by_columns:
        def body(c_ref, a_ref, b_ref, o_ref):
            v = a_ref[...] + b_ref[...]
            for j in range(4):
                o_ref[j] = v[:, j * cols:(j + 1) * cols].astype(out_dtype)

        in_specs = [pl.BlockSpec((tr, wide), lambda i, c_ref: (c_ref[0] * per + i, 0)),
                    pl.BlockSpec((tr, wide), lambda i, c_ref: (i, 0))]
        out_specs = pl.BlockSpec((4, tr, cols), lambda i, c_ref: (0, i, 0))
        grid = (per,)
    else:
        def body(c_ref, a_ref, b_ref, o_ref):
            o_ref[...] = (a_ref[...] + b_ref[...]).astype(out_dtype)

        in_specs = [pl.BlockSpec((1, tr, cols), lambda s, i, c_ref: (s, c_ref[0] * per + i, 0)),
                    pl.BlockSpec((1, tr, cols), lambda s, i, c_ref: (s, i, 0))]
        out_specs = pl.BlockSpec((1, tr, cols), lambda s, i, c_ref: (s, i, 0))
        grid = (4, per)
    return pl.pallas_call(
        body, name=name,
        grid_spec=pltpu.PrefetchScalarGridSpec(num_scalar_prefetch=1, grid=grid, in_specs=in_specs, out_specs=out_specs),
        out_shape=S((4, hr, cols), out_dtype), compiler_params=_params(len(grid)),
    )(jnp.reshape(c, (1,)).astype(jnp.int32), full, got)


def _small_add_own_half(fulls, gots, c):
    n = len(fulls)

    def body(c_ref, *refs):
        for a_ref, b_ref, o_ref in zip(refs[:n], refs[n:2 * n], refs[2 * n:]):
            hr = b_ref.shape[0]
            o_ref[...] = a_ref[pl.ds(pl.multiple_of(c_ref[0] * hr, 8), hr), :] + b_ref[...]

    specs = lambda arrs: [pl.BlockSpec(a.shape, lambda i, c_ref: (0, 0)) for a in arrs]
    return pl.pallas_call(
        body, name="small_pair_add",
        grid_spec=pltpu.PrefetchScalarGridSpec(num_scalar_prefetch=1, grid=(1,), in_specs=specs(fulls) + specs(gots),
                                               out_specs=specs(gots)),
        out_shape=[S(g.shape, F32) for g in gots], compiler_params=_params(),
    )(jnp.reshape(c, (1,)).astype(jnp.int32), *fulls, *gots)


def _small_sum_slots(slots, c):
    n = len(slots)

    def body(c_ref, *refs):
        for s_ref, o_ref in zip(refs[:n], refs[n:]):
            hr = s_ref.shape[1]
            o_ref[pl.ds(pl.multiple_of(c_ref[0] * hr, 8), hr), :] = ((s_ref[0] + s_ref[1]) + s_ref[2]) + s_ref[3]

    outs = [S((2 * s.shape[1], s.shape[2]), F32) for s in slots]
    return pl.pallas_call(
        body, name="small_quad_sum",
        grid_spec=pltpu.PrefetchScalarGridSpec(
            num_scalar_prefetch=1, grid=(1,),
            in_specs=[pl.BlockSpec(s.shape, lambda i, c_ref: (0, 0, 0)) for s in slots],
            out_specs=[pl.BlockSpec(o.shape, lambda i, c_ref: (0, 0)) for o in outs]),
        out_shape=outs, compiler_params=_params(),
    )(jnp.reshape(c, (1,)).astype(jnp.int32), *slots)


def _sum_slots(name, own, slots, me, c):
    _, rows, cols = slots.shape
    tr = _row_tile(rows, cols, 16 if slots.dtype == jnp.bfloat16 else 8)
    per = rows // tr
    three = len(own.shape) == 3

    def body(p_ref, own_ref, s0, s1, s2, s3, o_ref):
        mine = own_ref[0] if three else own_ref[...]
        acc = None
        for j, s_ref in enumerate((s0, s1, s2, s3)):
            v = jnp.where(p_ref[0] == j, mine, s_ref[0]).astype(F32)
            acc = v if acc is None else acc + v
        o_ref[...] = acc

    def slot_spec(j):
        return pl.BlockSpec((1, tr, cols), lambda i, p: (jnp.where(p[0] == j, (j + 1) % 4, j), i, 0))

    own_spec = (pl.BlockSpec((1, tr, cols), lambda i, p: (p[0], i, 0)) if three
                else pl.BlockSpec((tr, cols), lambda i, p: (i, 0)))
    return pl.pallas_call(
        body, name=name,
        grid_spec=pltpu.PrefetchScalarGridSpec(
            num_scalar_prefetch=1, grid=(per,), in_specs=[own_spec] + [slot_spec(j) for j in range(4)],
            out_specs=pl.BlockSpec((tr, cols), lambda i, p: (p[1] * per + i, 0))),
        out_shape=S((2 * rows, cols), F32), compiler_params=_params(),
    )(jnp.stack([me, c]).astype(jnp.int32), own, slots, slots, slots, slots)


BIG = ("w_in", "w_out", "w_gate", "w_up", "w_down")
ROW_PARAMS = (("pre_mix_norm", 0), ("lru_conv_b", 12), ("lru_ba", 13), ("lru_bx", 14), ("lru_lambda", 15),
              ("lru_out_norm", 16), ("ssd_out_norm", 24), ("post_mix_norm", 33), ("pre_ffn_norm", 32), ("post_ffn_norm", 41))
LRU_CONV_ROWS = (8, 12)
LOSS_ROW = 40
HEAD_PARAMS = (("ssd_dt_bias", 0), ("ssd_a_log", 1), ("ssd_d", 2))
SMALL = tuple(n for n, _ in ROW_PARAMS) + ("ssd_conv_b",) + tuple(n for n, _ in HEAD_PARAMS) + (
    "lru_wa", "lru_wx", "lru_conv_w", "ssd_conv_w")


def _diag4(w):
    eye = jnp.eye(4, dtype=w.dtype).reshape(1, 4, 1, 4, 1)
    return (w.reshape(4, 4, BW, 1, BW) * eye).reshape(4, 4 * BW, 4 * BW)


def _adam_math(w, g, m, v):
    mm = ADAM_B1 * m + (1.0 - ADAM_B1) * g
    vv = ADAM_B2 * v + (1.0 - ADAM_B2) * (g * g)
    c1 = 1.0 - ADAM_B1 ** ADAM_STEP
    c2 = 1.0 - ADAM_B2 ** ADAM_STEP
    return -ADAM_LR * ((mm / c1) / (jnp.sqrt(vv / c2) + ADAM_EPS) + ADAM_WD * w), mm, vv


def _adamw_small(rows, cst, hst, dwa, dwx, glcw, gscw, w, m, v):
    def grad_of(name, refs):
        rows_ref, cst_ref, hst_ref, dwa_ref, dwx_ref, glcw_ref, gscw_ref = refs
        for n, r in ROW_PARAMS:
            if n == name:
                return rows_ref[r:r + 1, :]
        for n, r in HEAD_PARAMS:
            if n == name:
                return hst_ref[r:r + 1, 0:NH]
        return {"ssd_conv_b": lambda: cst_ref[4:5, :], "lru_wa": lambda: dwa_ref[...], "lru_wx": lambda: dwx_ref[...],
                "lru_conv_w": lambda: glcw_ref[...], "ssd_conv_w": lambda: gscw_ref[...]}[name]()

    shapes = {n: (w[n].shape[1:] if len(w[n].shape) > 2 else w[n].shape) for n in SMALL}
    flat = lambda d: [d[n].reshape(shapes[n]) for n in SMALL]
    ns = len(SMALL)

    def body(*refs):
        srcs, rest = refs[:7], refs[7:]
        w_refs, m_refs, v_refs = rest[:ns], rest[ns:2 * ns], rest[2 * ns:3 * ns]
        outs = rest[3 * ns:]
        for k, name in enumerate(SMALL):
            g = grad_of(name, srcs)
            d, mm, vv = _adam_math(w_refs[k][...], g, m_refs[k][...], v_refs[k][...])
            outs[4 * k][...] = g
            outs[4 * k + 1][...] = d
            outs[4 * k + 2][...] = mm
            outs[4 * k + 3][...] = vv

    res = pl.pallas_call(
        body, name="adamw_small",
        out_shape=[S(shapes[n], F32) for n in SMALL for _ in range(4)],
        compiler_params=pltpu.CompilerParams(vmem_limit_bytes=VMEM_LIMIT),
    )(rows, cst, hst, dwa, dwx, glcw, gscw, *flat(w), *flat(m), *flat(v))
    return {n: tuple(res[4 * k + i].reshape(w[n].shape) for i in range(4)) for k, n in enumerate(SMALL)}


def _with_own(own, got):
    chip = 2 * lax.axis_index("x") + lax.axis_index("y")
    return jnp.where((jnp.arange(4) == chip).reshape(4, 1, 1), own[None], got)


def _side_by_side(f):
    return f.transpose(1, 0, 2).reshape(f.shape[1], 4 * f.shape[2])


def _stacked(f):
    return f.reshape(4 * f.shape[1], f.shape[2])


def _gather_first_weights(w_in, lru_conv_w, ssd_conv_w):
    conv = jnp.concatenate([lru_conv_w.reshape(-1), ssd_conv_w.reshape(-1)]).astype(F32)
    hi = conv.astype(jnp.bfloat16)
    mid = (conv - hi.astype(F32)).astype(jnp.bfloat16)
    lo = (conv - hi.astype(F32) - mid.astype(F32)).astype(jnp.bfloat16)
    terms = jnp.concatenate([hi, mid, lo])
    n_terms = terms.shape[0]
    conv_rows = -(-n_terms // (128 * 32)) * 32
    terms = jnp.pad(terms, (0, conv_rows * 128 - n_terms)).reshape(conv_rows, 128)
    own = [w_in.astype(WIRE), terms]
    got = _allgather_weights(own)
    win_f = _side_by_side(_with_own(own[0], got[0]))
    t3 = _with_own(own[1], got[1]).reshape(4, -1)[:, :n_terms].reshape(4, 3, -1).astype(F32)
    conv_f = (t3[:, 0] + t3[:, 1]) + t3[:, 2]
    n1 = lru_conv_w.size
    lcw = conv_f[:, :n1].reshape(4, CONV_K, -1).transpose(1, 0, 2).reshape(CONV_K, LW)
    scw = conv_f[:, n1:].reshape(4, CONV_K, -1).transpose(1, 0, 2).reshape(CONV_K, XBC)
    return win_f, lcw, scw


def _pair_stage(tag, bufs, by_columns, small, c):
    nb = len(bufs)
    got = list(_pair_exchange("pair_exchange_" + tag, list(bufs) + list(small)))
    part = [_add_own_half("pair_add_%s%d" % (tag, k), b, r, c, WIRE, bc)
            for k, (b, r, bc) in enumerate(zip(bufs, got[:nb], by_columns))]
    part_small = list(_small_add_own_half(list(small), got[nb:], c)) if small else []
    return part, part_small


def _step(x, tgt, win_f, lcw, scw, sp, late):
    c = lax.axis_index("c")
    me = 2 * lax.axis_index("x") + lax.axis_index("y")
    mm = lambda w: w.astype(BF)
    wcat = jnp.concatenate([mm(win_f), jnp.zeros((D, PC - IN_COLS), BF)], axis=1)
    row = lambda v: v.reshape(1, -1).astype(F32)
    p_lru = jnp.concatenate([lcw, row(sp["lru_conv_b"]), row(sp["lru_ba"]), row(sp["lru_bx"]), row(sp["lru_lambda"]),
                             row(sp["lru_out_norm"]), jnp.zeros((7, LW), F32)], axis=0)
    wa4, wx4 = mm(_diag4(sp["lru_wa"][0])), mm(_diag4(sp["lru_wx"][0]))
    wa4T, wx4T = wa4.transpose(0, 2, 1), wx4.transpose(0, 2, 1)
    cw_ssd = jnp.concatenate([scw, row(sp["ssd_conv_b"]), jnp.zeros((3, XBC), F32)], axis=0)
    padh = lambda v: jnp.pad(row(v), ((0, 0), (0, DTP - NH)))
    hp_ssd = jnp.concatenate([padh(sp["ssd_dt_bias"]), padh(sp["ssd_a_log"]), padh(sp["ssd_d"]), jnp.zeros((5, DTP), F32)], axis=0)
    g0, g_ssd = row(sp["pre_mix_norm"]), row(sp["ssd_out_norm"])
    g_pm, g_pf, g_pff = row(sp["post_mix_norm"]), row(sp["pre_ffn_norm"]), row(sp["post_ffn_norm"])

    h0 = _prenorm(x, g0)
    h, ylru, lxc, lxr, lg, *got_a = _lru_fwd(h0, wcat, p_lru, wa4, wx4, [late[0], late[3]])
    y, yssd, states, cv, z, xbcr, dtr, *got_b = _ssd_fwd(h0, wcat, cw_ssd, hp_ssd, g_ssd, [late[1], late[2]])
    wout, wd = mm(_stacked(_with_own(late[0], got_a[0]))), mm(_stacked(_with_own(late[3], got_a[1])))
    wg, wu = mm(_side_by_side(_with_own(late[1], got_b[0]))), mm(_side_by_side(_with_own(late[2], got_b[1])))
    mix, x1, h2 = _outproj(ylru, yssd, x, wout, g_pm, g_pf)
    gate, up, act, df, dx2, st_ffn = _ffn_fwd(h2, x1, tgt, wg, wu, wd, g_pff)
    dgate, dup, dh2 = _ffn_bwd(df, gate, up, wd.T, wg.T, wu.T)
    dx1, dmix, st_mix = _mix_bwd(dh2, x1, dx2, mix, g_pf, g_pm)
    woutT = wout.T

    dwg = _wgrad("wgrad_gate", h2, dgate)
    dwu = _wgrad("wgrad_up", h2, dup)
    dwd = _wgrad("wgrad_down", act, df)
    dwo = jnp.concatenate([_wgrad("wgrad_out_lru", ylru, dmix), _wgrad("wgrad_out_ssd", yssd, dmix)], axis=0)
    early = [dwo.reshape(4, (LW + SI) // 4, D), dwg, dwu, dwd.reshape(4, DFF // 4, D)]
    part_early, _ = _pair_stage("early", early, [False, True, True, False], [], c)

    dlx, dlg, st_lru, dwa, dwx = _lru_bwd(dmix, woutT, lxr, lxc, lg, h, p_lru, wa4, wx4, wa4T, wx4T)
    dxbc, dz, ddt, cst, hst, gst, *slots_early = _ssd_bwd(dmix, woutT, xbcr, cv, z, dtr, y, states, cw_ssd, hp_ssd,
                                                          g_ssd, part_early)
    gx, st_in = _inproj_bwd(dlx, dlg, dz, dxbc, ddt, x, dx1, wcat.T, g0)
    red_early = [_sum_slots("quad_sum_early%d" % k, p, s, me, c) for k, (p, s) in enumerate(zip(part_early, slots_early))]

    pin = [_wgrad("wgrad_in_%d" % k, h0, b) for k, b in enumerate((dlx, dlg, dz, dxbc, ddt))]
    dwin = jnp.concatenate(pin[:4] + [pin[4][:, :NH]], axis=1)
    rows = jnp.concatenate([st_in, st_lru, gst, st_mix, st_ffn], axis=0)
    small = [rows, cst, hst, dwa.reshape(NBLK * BW, BW), dwx.reshape(NBLK * BW, BW)]
    part, part_small = _pair_stage("late", [dwin], [True], small, c)
    slots = list(_quad_exchange(part + part_small, [True] + [False] * len(small)))
    red = [_sum_slots("quad_sum_late", part[0], slots[0], me, c)]
    red_small = list(_small_sum_slots(slots[1:], c))
    out = list(_pair_gather(red + red_early + red_small))
    big = dict(zip(("w_in", "w_out", "w_gate", "w_up", "w_down"), out[:5]))
    return gx, big, out[5:]


def kernel(x, pre_mix_norm, w_in, lru_conv_w, lru_conv_b, lru_wa, lru_ba, lru_wx, lru_bx, lru_lambda, lru_out_norm, ssd_conv_w, ssd_conv_b, ssd_dt_bias, ssd_a_log, ssd_d, ssd_out_norm, w_out, post_mix_norm, pre_ffn_norm, w_gate, w_up, w_down, post_ffn_norm, loss_target, m_pre_mix_norm, m_w_in, m_lru_conv_w, m_lru_conv_b, m_lru_wa, m_lru_ba, m_lru_wx, m_lru_bx, m_lru_lambda, m_lru_out_norm, m_ssd_conv_w, m_ssd_conv_b, m_ssd_dt_bias, m_ssd_a_log, m_ssd_d, m_ssd_out_norm, m_w_out, m_post_mix_norm, m_pre_ffn_norm, m_w_gate, m_w_up, m_w_down, m_post_ffn_norm, v_pre_mix_norm, v_w_in, v_lru_conv_w, v_lru_conv_b, v_lru_wa, v_lru_ba, v_lru_wx, v_lru_bx, v_lru_lambda, v_lru_out_norm, v_ssd_conv_w, v_ssd_conv_b, v_ssd_dt_bias, v_ssd_a_log, v_ssd_d, v_ssd_out_norm, v_w_out, v_post_mix_norm, v_pre_ffn_norm, v_w_gate, v_w_up, v_w_down, v_post_ffn_norm):
    args = dict(locals())
    names = list(SMALL) + list(BIG)
    w = {n: args[n] for n in names}
    m = {n: args["m_" + n] for n in names}
    v = {n: args["v_" + n] for n in names}
    chip = 2 * lax.axis_index("x") + lax.axis_index("y")

    win_f, lcw, scw = _gather_first_weights(w_in[0], lru_conv_w[0], ssd_conv_w[0])
    late = [a[0].astype(WIRE) for a in (w_out, w_gate, w_up, w_down)]
    gx, red, (rows, cst, hst, dwa, dwx) = _step(x[0], loss_target[0], win_f, lcw, scw, {n: w[n] for n in SMALL}, late)
    loss = jnp.sum(rows[LOSS_ROW])

    grads, delta, new_m, new_v = {}, {}, {}, {}
    for n in BIG:
        g = red[n]
        delta[n], new_m[n], new_v[n] = _adamw("adamw_" + n, w[n], g, m[n], v[n])
        grads[n] = g[None]

    lc, sc = lru_conv_w.shape[-1], ssd_conv_w.shape[-1]
    glcw = lax.dynamic_slice_in_dim(rows[LRU_CONV_ROWS[0]:LRU_CONV_ROWS[1]], chip * lc, lc, axis=1)
    gscw = lax.dynamic_slice_in_dim(cst[0:CONV_K], chip * sc, sc, axis=1)
    res = _adamw_small(rows, cst, hst, dwa.reshape(NBLK, BW, BW), dwx.reshape(NBLK, BW, BW), glcw, gscw,
                       {n: w[n] for n in SMALL}, {n: m[n] for n in SMALL}, {n: v[n] for n in SMALL})
    for n in SMALL:
        grads[n], delta[n], new_m[n], new_v[n] = res[n]

    order = ["pre_mix_norm", "w_in", "lru_conv_w", "lru_conv_b", "lru_wa", "lru_ba", "lru_wx", "lru_bx", "lru_lambda",
             "lru_out_norm", "ssd_conv_w", "ssd_conv_b", "ssd_dt_bias", "ssd_a_log", "ssd_d", "ssd_out_norm", "w_out",
             "post_mix_norm", "pre_ffn_norm", "w_gate", "w_up", "w_down", "post_ffn_norm"]
    return (loss, gx[None], *[grads[n] for n in order], *[delta[n] for n in order],
            *[new_m[n] for n in order], *[new_v[n] for n in order])
```

```python
import functools

import jax
import jax.numpy as jnp
from jax import lax
from jax.experimental import pallas as pl
from jax.experimental.pallas import tpu as pltpu

F32 = jnp.float32
BF = jnp.bfloat16

D = 1024
LW = 1024
NBLK = 16
BW = 64
SI = 1024
NH = 16
HD = 64
NG = 2
HPG = NH // NG
NS = 128
CH = 128
XBC = SI + 2 * NG * NS
DTP = 128
PC = 3 * 1024 + XBC + DTP
DFF = 2816
IN_COLS = 4624
EPS = 1e-6
LRU_C = 8.0
CONV_K = 4
TT = 256
VMEM_LIMIT = 56 * 1024 * 1024

ADAM_LR, ADAM_B1, ADAM_B2, ADAM_EPS, ADAM_WD, ADAM_STEP = 0.001, 0.9, 0.999, 1e-08, 0.01, 10

MESH = pl.DeviceIdType.MESH


def _mm(a, b):
    return jnp.dot(a.astype(BF), b.astype(BF), preferred_element_type=F32)


def _mm_nt(a, b):
    return lax.dot_general(a.astype(BF), b.astype(BF), (((1,), (1,)), ((), ())), preferred_element_type=F32)


def _mm_tn(a, b):
    return lax.dot_general(a.astype(BF), b.astype(BF), (((0,), (0,)), ((), ())), preferred_element_type=F32)


def _sigmoid(x):
    return 0.5 * jnp.tanh(0.5 * x) + 0.5


def _softplus(x):
    return jnp.maximum(x, 0.0) + jnp.log1p(jnp.exp(-jnp.abs(x)))


_GELU_C = 0.7978845608028654
_GELU_K = 0.044715


def _gelu(x):
    t = jnp.tanh(_GELU_C * (x + _GELU_K * x * x * x))
    return 0.5 * x * (1.0 + t)


def _gelu_grad(x):
    t = jnp.tanh(_GELU_C * (x + _GELU_K * x * x * x))
    return 0.5 * (1.0 + t) + 0.5 * x * (1.0 - t * t) * _GELU_C * (1.0 + 3.0 * _GELU_K * x * x)


def _rms_fwd(x, g):
    r = lax.rsqrt(jnp.mean(x * x, axis=-1, keepdims=True) + EPS)
    return x * r * g


def _rms_bwd(x, g, dy):
    r = lax.rsqrt(jnp.mean(x * x, axis=-1, keepdims=True) + EPS)
    xh = x * r
    dxh = dy * g
    dg = jnp.sum(dy * xh, axis=0, keepdims=True)
    dx = r * (dxh - xh * jnp.mean(dxh * xh, axis=-1, keepdims=True))
    return dx, dg


def _sum_all(x):
    return jnp.sum(jnp.sum(x, axis=1, keepdims=True), axis=0, keepdims=True)


def _cumsum_rows(x, n):
    row = lax.broadcasted_iota(jnp.int32, x.shape, 0)
    k = 1
    while k < n:
        x = x + jnp.where(row >= k, pltpu.roll(x, k, 0), 0.0)
        k *= 2
    return x


def _rev_cumsum_rows(x, n):
    row = lax.broadcasted_iota(jnp.int32, x.shape, 0)
    k = 1
    while k < n:
        x = x + jnp.where(row < n - k, pltpu.roll(x, n - k, 0), 0.0)
        k *= 2
    return x


def _load_once(pairs, sem):
    @pl.when(pl.program_id(0) == 0)
    def _():
        for k, (src, dst) in enumerate(pairs):
            pltpu.make_async_copy(src, dst, sem.at[k]).start()
        for k, (src, dst) in enumerate(pairs):
            pltpu.make_async_copy(src, dst, sem.at[k]).wait()


def _params(n_axes=1):
    return pltpu.CompilerParams(dimension_semantics=("arbitrary",) * n_axes, vmem_limit_bytes=VMEM_LIMIT)


def _rows(n, width, rev_of=None):
    if rev_of is None:
        return pl.BlockSpec((n, width), lambda i: (i, 0))
    return pl.BlockSpec((n, width), lambda i: (rev_of - 1 - i, 0))


def _whole(shape):
    nd = len(shape)
    return pl.BlockSpec(shape, lambda i: (0,) * nd)


ANY = pl.BlockSpec(memory_space=pl.ANY)
S = jax.ShapeDtypeStruct
WIRE = jnp.bfloat16


def _pos():
    return lax.axis_index("x"), lax.axis_index("y"), lax.axis_index("c")


def _other_chips(x, y):
    return [(1 - x, y), (x, 1 - y), (1 - x, 1 - y)]


def _remote(src, dst, send_sem, recv_sem, to):
    return pltpu.make_async_remote_copy(src_ref=src, dst_ref=dst, send_sem=send_sem, recv_sem=recv_sem,
                                        device_id=to, device_id_type=MESH)


def _gather_phase(phase, ins, outs, send_sems, recv_sems):
    x, y, c = _pos()
    me = 2 * x + y
    chips = _other_chips(x, y)
    for i, (src, dst) in enumerate(zip(ins, outs)):
        hr = src.shape[0] // 2
        my_half = pl.ds(pl.multiple_of(c * hr, 16), hr)
        sib_half = pl.ds(pl.multiple_of((1 - c) * hr, 16), hr)
        for k, (cx, cy) in enumerate(chips):
            s1, r1 = send_sems.at[6 * i + k], recv_sems.at[6 * i + k]
            s2, r2 = send_sems.at[6 * i + 3 + k], recv_sems.at[6 * i + 3 + k]
            first = lambda: _remote(src.at[my_half, :], dst.at[me, my_half, :], s1, r1, (cx, cy, c))
            landed = dst.at[2 * cx + cy, my_half, :]
            passed = lambda: _remote(landed, landed, s2, r2, (x, y, 1 - c))
            if phase == 0:
                first().start()
            elif phase == 1:
                _remote(landed, landed, s1, r1, (cx, cy, c)).wait_recv()
                passed().start()
            else:
                theirs = dst.at[2 * cx + cy, sib_half, :]
                _remote(theirs, theirs, s2, r2, (x, y, 1 - c)).wait_recv()
                first().wait_send()
                passed().wait_send()


def _quad_phase(phase, ins, outs, send_sems, recv_sems):
    x, y, c = _pos()
    me = 2 * x + y
    for i, (src, dst) in enumerate(zip(ins, outs)):
        for k, (cx, cy) in enumerate(_other_chips(x, y)):
            cp = _remote(src.at[2 * cx + cy], dst.at[me], send_sems.at[3 * i + k], recv_sems.at[3 * i + k], (cx, cy, c))
            if phase == 0:
                cp.start()
            else:
                got = dst.at[2 * cx + cy]
                _remote(got, got, send_sems.at[3 * i + k], recv_sems.at[3 * i + k], (cx, cy, c)).wait_recv()
                cp.wait_send()


def _prenorm(x, g0):
    T = x.shape[0]
    tt = 2 * TT

    def body(x_ref, g_ref, h0_ref):
        h0_ref[...] = _rms_fwd(x_ref[...], g_ref[...]).astype(BF)

    return pl.pallas_call(
        body, name="prenorm", grid=(T // tt,),
        in_specs=[_rows(tt, D), _whole((1, D))], out_specs=_rows(tt, D), out_shape=S((T, D), BF),
        compiler_params=_params(),
    )(x, g0)


def _blockdiag_mm(v, w4_ref):
    return jnp.concatenate([_mm(v[:, 256 * j:256 * (j + 1)], w4_ref[j]) for j in range(4)], axis=1)


def _lru_gates(lx, p_ref, wa_ref, wx_ref):
    r = _sigmoid(_blockdiag_mm(lx, wa_ref) + p_ref[5:6, :])
    i = _sigmoid(_blockdiag_mm(lx, wx_ref) + p_ref[6:7, :])
    sp = _softplus(-p_ref[7:8, :])
    la = -LRU_C * r * sp
    a = jnp.exp(la)
    th = jnp.tanh(la)
    mult = jnp.sqrt(-2.0 * th / (1.0 - th))
    return r, i, sp, a, mult


def _conv_from(xp_ref, p_ref, n):
    acc = p_ref[4:5, :] + p_ref[0:1, :] * xp_ref[pl.ds(8 - CONV_K + 1, n), :]
    for k in range(1, CONV_K):
        acc = acc + p_ref[k:k + 1, :] * xp_ref[pl.ds(8 - CONV_K + 1 + k, n), :]
    return acc


def _conv_bwd(dp_ref, dconv, x, p_ref, st_ref, n):
    dp_ref[0:n, :] = dconv
    acc = None
    for k in range(CONV_K):
        g = dp_ref[pl.ds(CONV_K - 1 - k, n), :]
        acc = p_ref[k:k + 1, :] * g if acc is None else acc + p_ref[k:k + 1, :] * g
        st_ref[k:k + 1, :] += jnp.sum(g * x, axis=0, keepdims=True)
    st_ref[4:5, :] += jnp.sum(dconv, axis=0, keepdims=True)
    dp_ref[n:n + 8, :] = dp_ref[0:8, :]
    return acc


def _lru_fwd(h0, wcat, p_lru, wa4, wx4, shards):
    T = h0.shape[0]
    NT = T // TT
    ng = len(shards)

    def body(*refs):
        h0_ref, w_hbm, p_ref, wa_ref, wx_ref = refs[:5]
        sh_in = refs[5:5 + ng]
        h_ref, y_ref, lxc_ref, lxr_ref, lg_ref = refs[5 + ng:10 + ng]
        sh_out = refs[10 + ng:10 + 2 * ng]
        xp, a_s, u_s, hc, w_vm, wsem, send_sems, recv_sems = refs[10 + 2 * ng:]
        _load_once([(w_hbm.at[:, 0:2 * LW], w_vm)], wsem)
        for phase, step in enumerate((0, NT // 2, NT - 1)):
            @pl.when(pl.program_id(0) == step)
            def _():
                _gather_phase(phase, sh_in, sh_out, send_sems, recv_sems)

        @pl.when(pl.program_id(0) == 0)
        def _():
            xp[0:8, :] = jnp.zeros((8, LW), F32)
            hc[...] = jnp.zeros_like(hc)

        hv = h0_ref[...]
        lxr = jnp.dot(hv, w_vm[:, 0:LW], preferred_element_type=F32)
        lxr_ref[...] = lxr
        lg_ref[...] = jnp.dot(hv, w_vm[:, LW:2 * LW], preferred_element_type=F32)
        xp[8:8 + TT, :] = lxr
        lx = _conv_from(xp, p_ref, TT)
        lxc_ref[...] = lx
        xp[0:8, :] = xp[TT:TT + 8, :]
        r, i, sp, a, mult = _lru_gates(lx, p_ref, wa_ref, wx_ref)
        a_s[...] = a
        u_s[...] = mult * (i * lx)

        def step(t, h):
            h = a_s[pl.ds(t, 1), :] * h + u_s[pl.ds(t, 1), :]
            h_ref[pl.ds(t, 1), :] = h
            return h

        hc[0:1, :] = lax.fori_loop(0, TT, step, hc[0:1, :], unroll=8)
        gated = h_ref[...] * _gelu(lg_ref[...])
        y_ref[...] = _rms_fwd(gated, p_ref[8:9, :]).astype(BF)

    return pl.pallas_call(
        body, name="lru_fwd", grid=(NT,),
        in_specs=[_rows(TT, D), ANY, _whole((16, LW)), _whole((4, 256, 256)), _whole((4, 256, 256))] + [ANY] * ng,
        out_specs=[_rows(TT, LW), _rows(TT, LW), _rows(TT, LW), _rows(TT, LW), _rows(TT, LW)] + [ANY] * ng,
        out_shape=[S((T, LW), F32), S((T, LW), BF), S((T, LW), F32), S((T, LW), F32), S((T, LW), F32)]
        + [S((4,) + s.shape, s.dtype) for s in shards],
        scratch_shapes=[pltpu.VMEM((TT + 8, LW), F32), pltpu.VMEM((TT, LW), F32), pltpu.VMEM((TT, LW), F32),
                        pltpu.VMEM((8, LW), F32), pltpu.VMEM((D, 2 * LW), BF), pltpu.SemaphoreType.DMA((1,)),
                        pltpu.SemaphoreType.DMA((6 * ng,)), pltpu.SemaphoreType.DMA((6 * ng,))],
        compiler_params=_params(),
    )(h0, wcat, p_lru, wa4, wx4, *shards)


def _ssd_prep(cv, dt_ref, hp_ref):
    sg = _sigmoid(cv)
    xbc = cv * sg
    lane = lax.broadcasted_iota(jnp.int32, (CH, DTP), 1)
    raw = dt_ref[...] + hp_ref[0:1, :]
    dtv = jnp.where(lane < NH, _softplus(raw), 0.0)
    A = jnp.where(lane[0:1, :] < NH, -jnp.exp(hp_ref[1:2, :]), 0.0)
    cs = _cumsum_rows(dtv * A, CH)
    return sg, xbc, raw, dtv, A, cs


def _per_head_lanes(v):
    r = v.shape[0]
    first = lax.broadcasted_iota(jnp.int32, (r, 2 * HD), 1) < HD
    pairs = [jnp.where(first, jnp.broadcast_to(v[:, 2 * j:2 * j + 1], (r, 2 * HD)),
                       jnp.broadcast_to(v[:, 2 * j + 1:2 * j + 2], (r, 2 * HD))) for j in range(NH // 2)]
    return jnp.concatenate(pairs, axis=1)


def _per_head_rows(col, g):
    return jnp.concatenate([jnp.broadcast_to(col[g * HPG + k:g * HPG + k + 1, :], (HD, NS)) for k in range(HPG)], axis=0)


def _ssd_decays(cs):
    csT = cs.T
    cl = cs[CH - 1:CH, :]
    E_x = _per_head_lanes(jnp.exp(cs))
    dsm = jnp.exp(cl - cs)
    ds_x = _per_head_lanes(dsm)
    El_rows = jnp.broadcast_to(jnp.exp(csT[0:NH, CH - 1:CH]), (NH, NS))
    return csT, dsm, E_x, ds_x, El_rows


def _ssd_fwd(h0, wcat, cw_ssd, hp_ssd, g_ssd, shards):
    T = h0.shape[0]
    NC = T // CH
    ng = len(shards)
    c0 = 2 * LW

    def body(*refs):
        h0_ref, w_hbm, cw_ref, hp_ref, g_ref = refs[:5]
        sh_in = refs[5:5 + ng]
        y_ref, yn_ref, st_ref, cv_ref, z_ref, xr_ref, dt_ref = refs[5 + ng:12 + ng]
        sh_out = refs[12 + ng:12 + 2 * ng]
        xp, st, w_vm, wsem, send_sems, recv_sems = refs[12 + 2 * ng:]
        _load_once([(w_hbm.at[:, c0:PC], w_vm)], wsem)
        for phase, step in enumerate((0, NC // 2, NC - 1)):
            @pl.when(pl.program_id(0) == step)
            def _():
                _gather_phase(phase, sh_in, sh_out, send_sems, recv_sems)

        @pl.when(pl.program_id(0) == 0)
        def _():
            xp[0:8, :] = jnp.zeros((8, XBC), F32)
            st[...] = jnp.zeros_like(st)

        hv = h0_ref[...]
        z_ref[...] = jnp.dot(hv, w_vm[:, 0:SI], preferred_element_type=F32)
        xraw = jnp.dot(hv, w_vm[:, SI:SI + XBC], preferred_element_type=F32)
        xr_ref[...] = xraw
        dt_ref[...] = jnp.dot(hv, w_vm[:, SI + XBC:SI + XBC + DTP], preferred_element_type=F32)
        xp[8:8 + CH, :] = xraw
        cv = _conv_from(xp, cw_ref, CH)
        cv_ref[...] = cv
        sg, xbc, raw, dtv, A, cs = _ssd_prep(cv, dt_ref, hp_ref)
        xp[0:8, :] = xp[CH:CH + 8, :]
        st_ref[0] = st[...]
        csT, dsm, E_x, ds_x, El_rows = _ssd_decays(cs)
        X = xbc[:, 0:SI]
        xs = X * _per_head_lanes(dtv)
        xsd = (xs * ds_x).astype(BF)
        DX = _per_head_lanes(hp_ref[...])[2:3, :] * X
        tril = lax.broadcasted_iota(jnp.int32, (CH, CH), 0) >= lax.broadcasted_iota(jnp.int32, (CH, CH), 1)
        first = lax.broadcasted_iota(jnp.int32, (CH, 2 * HD), 1) < HD
        GW = HPG * HD
        for g in range(NG):
            Bg = xbc[:, SI + NS * g:SI + NS * (g + 1)].astype(BF)
            Cg = xbc[:, SI + NG * NS + NS * g:SI + NG * NS + NS * (g + 1)].astype(BF)
            G = _mm_nt(Cg, Bg)
            Sg = st[GW * g:GW * (g + 1), :]
            Yo = _mm_nt(Cg, Sg) * E_x[:, GW * g:GW * (g + 1)]
            st[GW * g:GW * (g + 1), :] = _per_head_rows(El_rows, g) * Sg + _mm_tn(xsd[:, GW * g:GW * (g + 1)], Bg)
            for jj in range(HPG // 2):
                j = g * (HPG // 2) + jj
                ps = slice(2 * HD * j, 2 * HD * (j + 1))
                xs_pair = xs[:, ps]
                acc = Yo[:, 2 * HD * jj:2 * HD * (jj + 1)] + DX[:, ps]
                for e in range(2):
                    h = 2 * j + e
                    Lm = jnp.exp(jnp.where(tril, cs[:, h:h + 1] - csT[h:h + 1, :], -1e30))
                    acc = acc + _mm(G * Lm, jnp.where(first if e == 0 else ~first, xs_pair, 0.0))
                y_ref[:, ps] = acc
        zz = z_ref[...]
        gated = y_ref[...] * (zz * _sigmoid(zz))
        yn_ref[...] = _rms_fwd(gated, g_ref[...]).astype(BF)

    return pl.pallas_call(
        body, name="ssd_fwd", grid=(NC,),
        in_specs=[_rows(CH, D), ANY, _whole((8, XBC)), _whole((8, DTP)), _whole((1, SI))] + [ANY] * ng,
        out_specs=[_rows(CH, SI), _rows(CH, SI), pl.BlockSpec((1, NH * HD, NS), lambda i: (i, 0, 0)), _rows(CH, XBC),
                   _rows(CH, SI), _rows(CH, XBC), _rows(CH, DTP)] + [ANY] * ng,
        out_shape=[S((T, SI), F32), S((T, SI), BF), S((NC, NH * HD, NS), F32), S((T, XBC), F32),
                   S((T, SI), F32), S((T, XBC), F32), S((T, DTP), F32)] + [S((4,) + s.shape, s.dtype) for s in shards],
        scratch_shapes=[pltpu.VMEM((CH + 8, XBC), F32), pltpu.VMEM((NH * HD, NS), F32),
                        pltpu.VMEM((D, PC - c0), BF), pltpu.SemaphoreType.DMA((1,)),
                        pltpu.SemaphoreType.DMA((6 * ng,)), pltpu.SemaphoreType.DMA((6 * ng,))],
        compiler_params=_params(),
    )(h0, wcat, cw_ssd, hp_ssd, g_ssd, *shards)


def _outproj(ylru, yssd, x, wout, g_pm, g_pf):
    T = x.shape[0]

    def body(yl_ref, ys_ref, x_ref, w_hbm, gpm_ref, gpf_ref, mix_ref, x1_ref, h2_ref, w_vm, sem):
        _load_once([(w_hbm, w_vm)], sem)
        mix = (jnp.dot(yl_ref[...], w_vm[0:LW, :], preferred_element_type=F32)
               + jnp.dot(ys_ref[...], w_vm[LW:LW + SI, :], preferred_element_type=F32))
        mix_ref[...] = mix
        x1 = x_ref[...] + _rms_fwd(mix, gpm_ref[...])
        x1_ref[...] = x1
        h2_ref[...] = _rms_fwd(x1, gpf_ref[...]).astype(BF)

    return pl.pallas_call(
        body, name="outproj", grid=(T // TT,),
        in_specs=[_rows(TT, LW), _rows(TT, SI), _rows(TT, D), ANY, _whole((1, D)), _whole((1, D))],
        out_specs=[_rows(TT, D), _rows(TT, D), _rows(TT, D)],
        out_shape=[S((T, D), F32), S((T, D), F32), S((T, D), BF)],
        scratch_shapes=[pltpu.VMEM((LW + SI, D), BF), pltpu.SemaphoreType.DMA((1,))],
        compiler_params=_params(),
    )(ylru, yssd, x, wout, g_pm, g_pf)


def _ffn_fwd(h2, x1, tgt, wg, wu, wd, g_pff):
    T = x1.shape[0]

    def body(h2_ref, x1_ref, t_ref, wg_hbm, wu_hbm, wd_hbm, g_ref,
             gate_ref, up_ref, act_ref, df_ref, dx2_ref, st_ref, wg_vm, wu_vm, wd_vm, sem):
        _load_once([(wg_hbm, wg_vm), (wu_hbm, wu_vm), (wd_hbm, wd_vm)], sem)

        @pl.when(pl.program_id(0) == 0)
        def _():
            st_ref[...] = jnp.zeros_like(st_ref)

        h2 = h2_ref[...]
        gate = jnp.dot(h2, wg_vm[...], preferred_element_type=F32)
        up = jnp.dot(h2, wu_vm[...], preferred_element_type=F32)
        gate_ref[...] = gate
        up_ref[...] = up
        act = (gate * _sigmoid(gate) * up).astype(BF)
        act_ref[...] = act
        f = jnp.dot(act, wd_vm[...], preferred_element_type=F32)
        g = g_ref[...]
        x2 = x1_ref[...] + _rms_fwd(f, g)
        err = x2 - t_ref[...]
        st_ref[0:1, :] += 0.5 * jnp.sum(err * err, axis=0, keepdims=True) * (1.0 / D)
        dx2 = err * (1.0 / D)
        dx2_ref[...] = dx2
        df, dg = _rms_bwd(f, g, dx2)
        df_ref[...] = df.astype(BF)
        st_ref[1:2, :] += dg

    return pl.pallas_call(
        body, name="ffn_fwd", grid=(T // TT,),
        in_specs=[_rows(TT, D), _rows(TT, D), _rows(TT, D), ANY, ANY, ANY, _whole((1, D))],
        out_specs=[_rows(TT, DFF), _rows(TT, DFF), _rows(TT, DFF), _rows(TT, D), _rows(TT, D), _whole((8, D))],
        out_shape=[S((T, DFF), F32), S((T, DFF), F32), S((T, DFF), BF), S((T, D), BF), S((T, D), F32), S((8, D), F32)],
        scratch_shapes=[pltpu.VMEM((D, DFF), BF), pltpu.VMEM((D, DFF), BF), pltpu.VMEM((DFF, D), BF),
                        pltpu.SemaphoreType.DMA((3,))],
        compiler_params=_params(),
    )(h2, x1, tgt, wg, wu, wd, g_pff)


def _ffn_bwd(df, gate, up, wdT, wgT, wuT):
    T = df.shape[0]

    def body(df_ref, gate_ref, up_ref, wd_hbm, wg_hbm, wu_hbm, dgate_ref, dup_ref, dh2_ref, wd_vm, wg_vm, wu_vm, sem):
        _load_once([(wd_hbm, wd_vm), (wg_hbm, wg_vm), (wu_hbm, wu_vm)], sem)
        dact = jnp.dot(df_ref[...], wd_vm[...], preferred_element_type=F32)
        gate = gate_ref[...]
        s = _sigmoid(gate)
        dup = (dact * (gate * s)).astype(BF)
        dgate = (dact * up_ref[...] * (s + gate * s * (1.0 - s))).astype(BF)
        dup_ref[...] = dup
        dgate_ref[...] = dgate
        dh2_ref[...] = (jnp.dot(dgate, wg_vm[...], preferred_element_type=F32)
                        + jnp.dot(dup, wu_vm[...], preferred_element_type=F32))

    return pl.pallas_call(
        body, name="ffn_bwd", grid=(T // TT,),
        in_specs=[_rows(TT, D), _rows(TT, DFF), _rows(TT, DFF), ANY, ANY, ANY],
        out_specs=[_rows(TT, DFF), _rows(TT, DFF), _rows(TT, D)],
        out_shape=[S((T, DFF), BF), S((T, DFF), BF), S((T, D), F32)],
        scratch_shapes=[pltpu.VMEM((D, DFF), BF), pltpu.VMEM((DFF, D), BF), pltpu.VMEM((DFF, D), BF),
                        pltpu.SemaphoreType.DMA((3,))],
        compiler_params=_params(),
    )(df, gate, up, wdT, wgT, wuT)


def _mix_bwd(dh2, x1, dx2, mix, woutT, g_pf, g_pm):
    T = x1.shape[0]

    def body(dh2_ref, x1_ref, dx2_ref, mix_ref, w_hbm, gpf_ref, gpm_ref,
             dx1_ref, dmix_ref, dyl_ref, dys_ref, st_ref, w_vm, sem):
        _load_once([(w_hbm, w_vm)], sem)

        @pl.when(pl.program_id(0) == 0)
        def _():
            st_ref[...] = jnp.zeros_like(st_ref)

        dxa, dgpf = _rms_bwd(x1_ref[...], gpf_ref[...], dh2_ref[...])
        dx1 = dx2_ref[...] + dxa
        dx1_ref[...] = dx1
        dmix, dgpm = _rms_bwd(mix_ref[...], gpm_ref[...], dx1)
        dmix = dmix.astype(BF)
        dmix_ref[...] = dmix
        st_ref[0:1, :] += dgpf
        st_ref[1:2, :] += dgpm
        dyl_ref[...] = jnp.dot(dmix, w_vm[:, 0:LW], preferred_element_type=F32)
        dys_ref[...] = jnp.dot(dmix, w_vm[:, LW:LW + SI], preferred_element_type=F32)

    return pl.pallas_call(
        body, name="mix_bwd", grid=(T // TT,),
        in_specs=[_rows(TT, D), _rows(TT, D), _rows(TT, D), _rows(TT, D), ANY, _whole((1, D)), _whole((1, D))],
        out_specs=[_rows(TT, D), _rows(TT, D), _rows(TT, LW), _rows(TT, SI), _whole((8, D))],
        out_shape=[S((T, D), F32), S((T, D), BF), S((T, LW), F32), S((T, SI), F32), S((8, D), F32)],
        scratch_shapes=[pltpu.VMEM((D, LW + SI), BF), pltpu.SemaphoreType.DMA((1,))],
        compiler_params=_params(),
    )(dh2, x1, dx2, mix, woutT, g_pf, g_pm)


def _halo(width, n_tiles, tile):
    per = tile // 8
    return pl.BlockSpec((8, width), lambda i: (jnp.maximum((n_tiles - 1 - i) * per - 1, 0), 0))


def _lru_bwd(dy, lxr, lxc, lg, h, p_lru, wa4, wx4, wa4T, wx4T):
    T = dy.shape[0]
    NT = T // TT

    def body(dy_ref, lxr_ref, lxc_ref, lg_ref, h_ref, hh_ref, p_ref, wa_ref, wx_ref, waT_ref, wxT_ref,
             dlx_ref, dlg_ref, st_ref, dwa_ref, dwx_ref, hp, dp, a_s, d_s, g_s, cc):
        dy = dy_ref[...]
        first = pl.program_id(0) == 0
        top = pl.program_id(0) == NT - 1

        @pl.when(first)
        def _():
            st_ref[...] = jnp.zeros_like(st_ref)
            dwa_ref[...] = jnp.zeros_like(dwa_ref)
            dwx_ref[...] = jnp.zeros_like(dwx_ref)
            dp[TT:TT + 8, :] = jnp.zeros((8, LW), F32)
            cc[...] = jnp.zeros_like(cc)

        hp[0:8, :] = hh_ref[...] * jnp.where(top, 0.0, 1.0)
        hp[8:8 + TT, :] = h_ref[...]
        lx = lxc_ref[...]
        r, i, sp, a, mult = _lru_gates(lx, p_ref, wa_ref, wx_ref)

        lg = lg_ref[...]
        hcur = h_ref[...]
        ge = _gelu(lg)
        dgated, dgn = _rms_bwd(hcur * ge, p_ref[8:9, :], dy)
        st_ref[8:9, :] += dgn
        dlg_ref[...] = (dgated * hcur * _gelu_grad(lg)).astype(BF)
        a_s[...] = a
        d_s[...] = dgated * ge

        def step(k, c):
            t = TT - 1 - k
            g = d_s[pl.ds(t, 1), :] + c
            g_s[pl.ds(t, 1), :] = g
            return a_s[pl.ds(t, 1), :] * g

        cc[0:1, :] = lax.fori_loop(0, TT, step, cc[0:1, :], unroll=8)
        gt = g_s[...]
        da = gt * hp[pl.ds(7, TT), :]
        dmult = gt * i * lx
        di = gt * mult * lx
        dlxc = gt * mult * i
        dla = da * a - dmult * (a * a) / mult
        dr = dla * (-LRU_C * sp)
        st_ref[7:8, :] += jnp.sum(dla * (-LRU_C * r), axis=0, keepdims=True) * (-_sigmoid(-p_ref[7:8, :]))
        dzr = dr * r * (1.0 - r)
        dzi = di * i * (1.0 - i)
        st_ref[5:6, :] += jnp.sum(dzr, axis=0, keepdims=True)
        st_ref[6:7, :] += jnp.sum(dzi, axis=0, keepdims=True)
        dlxc = dlxc + _blockdiag_mm(dzr, waT_ref) + _blockdiag_mm(dzi, wxT_ref)
        for j in range(4):
            sl = slice(256 * j, 256 * (j + 1))
            pa = _mm_tn(lx[:, sl], dzr[:, sl])
            px = _mm_tn(lx[:, sl], dzi[:, sl])
            for b in range(4):
                bs = slice(BW * b, BW * (b + 1))
                dwa_ref[4 * j + b] += pa[bs, bs]
                dwx_ref[4 * j + b] += px[bs, bs]
        dlx_ref[...] = _conv_bwd(dp, dlxc, lxr_ref[...], p_ref, st_ref, TT).astype(BF)

    w4 = _whole((4, 256, 256))
    return pl.pallas_call(
        body, name="lru_bwd", grid=(NT,),
        in_specs=[_rows(TT, LW, NT), _rows(TT, LW, NT), _rows(TT, LW, NT), _rows(TT, LW, NT), _rows(TT, LW, NT),
                  _halo(LW, NT, TT), _whole((16, LW)), w4, w4, w4, w4],
        out_specs=[_rows(TT, LW, NT), _rows(TT, LW, NT), _whole((16, LW)), _whole((NBLK, BW, BW)), _whole((NBLK, BW, BW))],
        out_shape=[S((T, LW), BF), S((T, LW), BF), S((16, LW), F32), S((NBLK, BW, BW), F32), S((NBLK, BW, BW), F32)],
        scratch_shapes=[pltpu.VMEM((TT + 8, LW), F32), pltpu.VMEM((TT + 8, LW), F32),
                        pltpu.VMEM((TT, LW), F32), pltpu.VMEM((TT, LW), F32), pltpu.VMEM((TT, LW), F32),
                        pltpu.VMEM((8, LW), F32)],
        compiler_params=_params(),
    )(dy, lxr, lxc, lg, h, h, p_lru, wa4, wx4, wa4T, wx4T)


def _ssd_bwd(dyn, xbcr, cv, z, dtr, y, states, cw_ssd, hp_ssd, g_ssd, parts):
    T = dyn.shape[0]
    NC = T // CH
    nq = len(parts)

    def body(*refs):
        dyn_ref, xr_ref, cv_ref, z_ref, dt_ref, y_ref, st_ref, cw_ref, hp_ref, g_ref = refs[:10]
        q_in = refs[10:10 + nq]
        dxbc_ref, dz_ref, ddt_ref, cst_ref, hst_ref, gst_ref = refs[10 + nq:16 + nq]
        q_out = refs[16 + nq:16 + 2 * nq]
        dp, dS, dxb, yo_s, q_s, dxs_s, t1_s, send_sems, recv_sems = refs[16 + 2 * nq:]
        dyn = dyn_ref[...]
        first = pl.program_id(0) == 0
        for phase, step in enumerate((0, NC - 1)):
            @pl.when(pl.program_id(0) == step)
            def _():
                _quad_phase(phase, q_in, q_out, send_sems, recv_sems)

        @pl.when(first)
        def _():
            cst_ref[...] = jnp.zeros_like(cst_ref)
            hst_ref[...] = jnp.zeros_like(hst_ref)
            gst_ref[...] = jnp.zeros_like(gst_ref)
            dp[CH:CH + 8, :] = jnp.zeros((8, XBC), F32)
            dS[...] = jnp.zeros_like(dS)

        cv = cv_ref[...]
        sg, xbc, raw, dtv, A, cs = _ssd_prep(cv, dt_ref, hp_ref)
        csT, dsm, E_x, ds_x, El_rows = _ssd_decays(cs)
        row_i = lax.broadcasted_iota(jnp.int32, (CH, CH), 0)
        col_i = lax.broadcasted_iota(jnp.int32, (CH, CH), 1)
        tril = row_i >= col_i
        first = col_i < HD
        head_of = ((lax.broadcasted_iota(jnp.int32, (DTP, SI), 1) >> 6)
                   == lax.broadcasted_iota(jnp.int32, (DTP, SI), 0)).astype(BF)
        head_ofT = ((lax.broadcasted_iota(jnp.int32, (SI, DTP), 0) >> 6)
                    == lax.broadcasted_iota(jnp.int32, (SI, DTP), 1)).astype(BF)

        def hi_lo(v):
            hi = v.astype(BF)
            return hi, (v - hi.astype(F32)).astype(BF)

        def lane_sums(v):
            hi, lo = hi_lo(v)
            return _mm(hi, head_ofT) + _mm(lo, head_ofT)

        zz = z_ref[...]
        sz = _sigmoid(zz)
        yv = y_ref[...]
        dgn, dg = _rms_bwd(yv * (zz * sz), g_ref[...], dyn)
        gst_ref[0:1, :] += dg
        dz_ref[...] = (dgn * yv * (sz + zz * sz * (1.0 - sz))).astype(BF)
        dY = dgn * (zz * sz)

        X = xbc[:, 0:SI]
        dt_x = _per_head_lanes(dtv)
        xs = X * dt_x
        xsd = (xs * ds_x).astype(BF)
        D_x = _per_head_lanes(hp_ref[...])[2:3, :]
        dcs_col = jnp.zeros((CH, DTP), F32)
        dcs_row = jnp.zeros((CH, DTP), F32)
        GW = HPG * HD
        for g in range(NG):
            gs = slice(GW * g, GW * (g + 1))
            Bg = xbc[:, SI + NS * g:SI + NS * (g + 1)].astype(BF)
            Cg = xbc[:, SI + NG * NS + NS * g:SI + NG * NS + NS * (g + 1)].astype(BF)
            G = _mm_nt(Cg, Bg)
            Sg = st_ref[0, gs, :]
            dSe = dS[gs, :]
            dYg = dY[:, gs]
            yo_s[:, gs] = _mm_nt(Cg, Sg) * E_x[:, gs]
            dP = dYg * E_x[:, gs]
            dCg = _mm(dP, Sg)
            dS[gs, :] = _mm_tn(dP, Cg) + _per_head_rows(El_rows, g) * dSe
            t1_s[gs, :] = dSe * Sg
            Q = _mm_nt(Bg, dSe)
            q_s[:, gs] = Q
            dBg = _mm(xsd[:, gs], dSe)
            dG = jnp.zeros((CH, CH), F32)
            for jj in range(HPG // 2):
                j = g * (HPG // 2) + jj
                ps = slice(2 * HD * j, 2 * HD * (j + 1))
                xs_pair = xs[:, ps]
                dxs_pair = Q[:, 2 * HD * jj:2 * HD * (jj + 1)] * ds_x[:, ps]
                for e in range(2):
                    h = 2 * j + e
                    Lm = jnp.exp(jnp.where(tril, cs[:, h:h + 1] - csT[h:h + 1, :], -1e30))
                    M = G * Lm
                    dYm = jnp.where(first if e == 0 else ~first, dY[:, ps], 0.0).astype(BF)
                    dM = _mm_nt(dYm, xs_pair)
                    dxs_pair = dxs_pair + _mm_tn(M, dYm)
                    Wm = dM * M
                    dcs_col = dcs_col + jnp.where(col_i == h, jnp.sum(Wm, axis=1, keepdims=True), 0.0)
                    dcs_row = dcs_row + jnp.where(row_i == h, -jnp.sum(Wm, axis=0, keepdims=True), 0.0)
                    dG = dG + dM * Lm
                dxs_s[:, ps] = dxs_pair
            dxb[:, SI + NS * g:SI + NS * (g + 1)] = dBg + _mm_tn(dG, Cg)
            dxb[:, SI + NG * NS + NS * g:SI + NG * NS + NS * (g + 1)] = dCg + _mm(dG, Bg)

        dxs = dxs_s[...]
        dxb[:, 0:SI] = D_x * dY + dxs * dt_x
        dds = lane_sums(q_s[...] * xs) * dsm
        dcs_col = dcs_col + lane_sums(dY * yo_s[...]) - dds
        ddt_col = lane_sums(dxs * X)
        dD = jnp.sum(lane_sums(dY * X), axis=0, keepdims=True)
        t_hi, t_lo = hi_lo(t1_s[...])
        dcl_rows = jnp.sum(_mm(head_of, t_hi) + _mm(head_of, t_lo), axis=1, keepdims=True) * jnp.exp(csT[:, CH - 1:CH])
        dcs_row = dcs_row + jnp.where(col_i == CH - 1, dcl_rows, 0.0)
        dcs_col = dcs_col + jnp.where(row_i == CH - 1, jnp.sum(dds, axis=0, keepdims=True), 0.0)

        da = _rev_cumsum_rows(dcs_col + dcs_row.T, CH)
        ddt_col = ddt_col + da * A
        hst_ref[1:2, :] += jnp.sum(da * dtv, axis=0, keepdims=True) * A
        hst_ref[2:3, :] += dD
        draw = jnp.where(col_i < NH, ddt_col * _sigmoid(raw), 0.0)
        ddt_ref[...] = draw.astype(BF)
        hst_ref[0:1, :] += jnp.sum(draw, axis=0, keepdims=True)

        dcv = dxb[...] * (sg + cv * sg * (1.0 - sg))
        dxbc_ref[...] = _conv_bwd(dp, dcv, xr_ref[...], cw_ref, cst_ref, CH).astype(BF)

    return pl.pallas_call(
        body, name="ssd_bwd", grid=(NC,),
        in_specs=[_rows(CH, SI, NC), _rows(CH, XBC, NC), _rows(CH, XBC, NC), _rows(CH, SI, NC), _rows(CH, DTP, NC),
                  _rows(CH, SI, NC), pl.BlockSpec((1, NH * HD, NS), lambda i: (NC - 1 - i, 0, 0)),
                  _whole((8, XBC)), _whole((8, DTP)), _whole((1, SI))] + [ANY] * nq,
        out_specs=[_rows(CH, XBC, NC), _rows(CH, SI, NC), _rows(CH, DTP, NC), _whole((16, XBC)), _whole((16, DTP)),
                   _whole((8, SI))] + [ANY] * nq,
        out_shape=[S((T, XBC), BF), S((T, SI), BF), S((T, DTP), BF), S((16, XBC), F32), S((16, DTP), F32), S((8, SI), F32)]
        + [S(p.shape, p.dtype) for p in parts],
        scratch_shapes=[pltpu.VMEM((CH + 8, XBC), F32), pltpu.VMEM((NH * HD, NS), F32),
                        pltpu.VMEM((CH, XBC), F32), pltpu.VMEM((CH, SI), F32), pltpu.VMEM((CH, SI), F32),
                        pltpu.VMEM((CH, SI), F32), pltpu.VMEM((NH * HD, NS), F32),
                        pltpu.SemaphoreType.DMA((3 * nq,)), pltpu.SemaphoreType.DMA((3 * nq,))],
        compiler_params=_params(),
    )(dyn, xbcr, cv, z, dtr, y, states, cw_ssd, hp_ssd, g_ssd, *parts)


def _inproj_bwd(dlx, dlg, dz, dxbc, ddt, x, dx1, wcatT, g0):
    T = x.shape[0]

    def body(dlx_ref, dlg_ref, dz_ref, dxbc_ref, ddt_ref, x_ref, dx1_ref, w_hbm, g_ref, dx_ref, st_ref, w_vm, sem):
        _load_once([(w_hbm, w_vm)], sem)

        @pl.when(pl.program_id(0) == 0)
        def _():
            st_ref[...] = jnp.zeros_like(st_ref)

        dh = jnp.dot(dlx_ref[...], w_vm[0:1024, :], preferred_element_type=F32)
        dh = dh + jnp.dot(dlg_ref[...], w_vm[1024:2048, :], preferred_element_type=F32)
        dh = dh + jnp.dot(dz_ref[...], w_vm[2048:3072, :], preferred_element_type=F32)
        dh = dh + jnp.dot(dxbc_ref[...], w_vm[3072:3072 + XBC, :], preferred_element_type=F32)
        dh = dh + jnp.dot(ddt_ref[...], w_vm[3072 + XBC:PC, :], preferred_element_type=F32)
        dx, dg = _rms_bwd(x_ref[...], g_ref[...], dh)
        dx_ref[...] = dx1_ref[...] + dx
        st_ref[0:1, :] += dg

    return pl.pallas_call(
        body, name="inproj_bwd", grid=(T // TT,),
        in_specs=[_rows(TT, 1024), _rows(TT, 1024), _rows(TT, 1024), _rows(TT, XBC), _rows(TT, DTP), _rows(TT, D),
                  _rows(TT, D), ANY, _whole((1, D))],
        out_specs=[_rows(TT, D), _whole((8, D))],
        out_shape=[S((T, D), F32), S((8, D), F32)],
        scratch_shapes=[pltpu.VMEM((PC, D), BF), pltpu.SemaphoreType.DMA((1,))],
        compiler_params=_params(),
    )(dlx, dlg, dz, dxbc, ddt, x, dx1, wcatT, g0)


def _wgrad(name, a, b):
    T, M = a.shape
    N = b.shape[1]
    tk = min(T, 2048 if M <= 1024 else 1024)
    tn = N
    while M * tn * 4 > (6 << 20) and tn % 256 == 0:
        tn //= 2

    def body(a_ref, b_ref, o_ref):
        p = lax.dot_general(a_ref[...], b_ref[...], (((0,), (0,)), ((), ())), preferred_element_type=F32)

        @pl.when(pl.program_id(1) == 0)
        def _():
            o_ref[...] = p

        @pl.when(pl.program_id(1) > 0)
        def _():
            o_ref[...] += p

    return pl.pallas_call(
        body, name=name, grid=(N // tn, T // tk),
        in_specs=[pl.BlockSpec((tk, M), lambda j, k: (k, 0)), pl.BlockSpec((tk, tn), lambda j, k: (k, j))],
        out_specs=pl.BlockSpec((M, tn), lambda j, k: (0, j)), out_shape=S((M, N), F32),
        compiler_params=_params(2),
    )(a, b)


def _adamw(name, w, g, m, v):
    _, R, C = w.shape

    def body(w_ref, g_ref, m_ref, v_ref, d_ref, nm_ref, nv_ref):
        d_ref[0], nm_ref[0], nv_ref[0] = _adam_math(w_ref[0], g_ref[...], m_ref[0], v_ref[0])

    if R % 8 == 0:
        tr = _row_tile(R, C)
        n_tiles = R // tr
        blk, gblk = pl.BlockSpec((1, tr, C), lambda i: (0, i, 0)), pl.BlockSpec((tr, C), lambda i: (i, 0))
    else:
        tc = 128 * max(k for k in range(1, C // 128 + 1) if C % (128 * k) == 0 and R * 128 * k * 4 <= (5 << 18))
        n_tiles = C // tc
        blk, gblk = pl.BlockSpec((1, R, tc), lambda i: (0, 0, i)), pl.BlockSpec((R, tc), lambda i: (0, i))
    return pl.pallas_call(
        body, name=name, grid=(n_tiles,),
        in_specs=[blk, gblk, blk, blk], out_specs=[blk] * 3,
        out_shape=[S((1, R, C), F32)] * 3, compiler_params=_params(),
    )(w, g, m, v)


def _half(ref, c, hr):
    sl = pl.ds(pl.multiple_of(c * hr, 8), hr)
    return ref.at[:, sl, :] if len(ref.shape) == 3 else ref.at[sl, :]


def _allgather_weights(shards):
    n = len(shards)

    def body(*refs):
        for phase in range(3):
            _gather_phase(phase, refs[:n], refs[n:2 * n], refs[2 * n], refs[2 * n + 1])

    return pl.pallas_call(
        body, name="allgather_weights", in_specs=[ANY] * n, out_specs=[ANY] * n,
        out_shape=[S((4,) + s.shape, s.dtype) for s in shards],
        scratch_shapes=[pltpu.SemaphoreType.DMA((6 * n,)), pltpu.SemaphoreType.DMA((6 * n,))],
    )(*shards)


def _pair_exchange(name, bufs):
    n = len(bufs)

    def half_shape(b):
        return b.shape[:-2] + (b.shape[-2] // 2, b.shape[-1])

    def body(*refs):
        ins, outs = refs[:n], refs[n:2 * n]
        send_sems, recv_sems = refs[2 * n], refs[2 * n + 1]
        x, y, c = _pos()
        copies = [_remote(_half(src, 1 - c, src.shape[-2] // 2), dst, send_sems.at[k], recv_sems.at[k], (x, y, 1 - c))
                  for k, (src, dst) in enumerate(zip(ins, outs))]
        for cp in copies:
            cp.start()
        for cp in copies:
            cp.wait()

    return pl.pallas_call(
        body, name=name, in_specs=[ANY] * n, out_specs=[ANY] * n,
        out_shape=[S(half_shape(b), b.dtype) for b in bufs],
        scratch_shapes=[pltpu.SemaphoreType.DMA((n,)), pltpu.SemaphoreType.DMA((n,))],
    )(*bufs)


def _quad_exchange(bufs, scatter):
    n = len(bufs)

    def body(*refs):
        ins, outs = refs[:n], refs[n:2 * n]
        send_sems, recv_sems, local_sems = refs[2 * n], refs[2 * n + 1], refs[2 * n + 2]
        x, y, c = _pos()
        me = 2 * x + y
        chips = _other_chips(x, y)
        copies, locals_ = [], []
        for k, (src, dst) in enumerate(zip(ins, outs)):
            if not scatter[k]:
                own = pltpu.make_async_copy(src, dst.at[me], local_sems.at[k])
                own.start()
                locals_.append(own)
            for j, (cx, cy) in enumerate(chips):
                piece = src.at[2 * cx + cy] if scatter[k] else src
                cp = _remote(piece, dst.at[me], send_sems.at[3 * k + j], recv_sems.at[3 * k + j], (cx, cy, c))
                cp.start()
                copies.append(cp)
        for k, (src, dst) in enumerate(zip(ins, outs)):
            for j, (cx, cy) in enumerate(chips):
                blk = dst.at[2 * cx + cy]
                _remote(blk, blk, send_sems.at[3 * k + j], recv_sems.at[3 * k + j], (cx, cy, c)).wait_recv()
        for cp in copies:
            cp.wait_send()
        for cp in locals_:
            cp.wait()

    return pl.pallas_call(
        body, name="quad_exchange", in_specs=[ANY] * n, out_specs=[ANY] * n,
        out_shape=[S((4,) + (b.shape[1:] if sc else b.shape), b.dtype) for b, sc in zip(bufs, scatter)],
        scratch_shapes=[pltpu.SemaphoreType.DMA((3 * n,)), pltpu.SemaphoreType.DMA((3 * n,)), pltpu.SemaphoreType.DMA((n,))],
    )(*bufs)


def _pair_gather(bufs):
    n = len(bufs)

    def body(*refs):
        ins, outs = refs[:n], refs[n:2 * n]
        send_sems, recv_sems = refs[2 * n], refs[2 * n + 1]
        x, y, c = _pos()
        copies = []
        for k, buf in enumerate(outs):
            mine = _half(buf, c, buf.shape[0] // 2)
            cp = _remote(mine, mine, send_sems.at[k], recv_sems.at[k], (x, y, 1 - c))
            cp.start()
            copies.append(cp)
        for k, buf in enumerate(outs):
            theirs = _half(buf, 1 - c, buf.shape[0] // 2)
            _remote(theirs, theirs, send_sems.at[k], recv_sems.at[k], (x, y, 1 - c)).wait_recv()
        for cp in copies:
            cp.wait_send()

    return pl.pallas_call(
        body, name="pair_gather", in_specs=[ANY] * n, out_specs=[ANY] * n,
        out_shape=[S(b.shape, b.dtype) for b in bufs], input_output_aliases={k: k for k in range(n)},
        scratch_shapes=[pltpu.SemaphoreType.DMA((n,)), pltpu.SemaphoreType.DMA((n,))],
    )(*bufs)


def _row_tile(rows, cols, mult=8):
    best = mult
    for t in range(mult, rows + 1, mult):
        if rows % t == 0 and t * cols * 4 <= (1 << 20):
            best = t
    return best


def _add_own_half(name, full, got, c, out_dtype, by_columns):
    hr = got.shape[-2]
    wide = got.shape[-1]
    cols = wide // 4 if by_columns else wide
    tr = _row_tile(hr, wide, 16)
    per = hr // tr

    if by_columns:
        def body(c_ref, a_ref, b_ref, o_ref):
            v = a_ref[...] + b_ref[...]
            for j in range(4):
                o_ref[j] = v[:, j * cols:(j + 1) * cols].astype(out_dtype)

        in_specs = [pl.BlockSpec((tr, wide), lambda i, c_ref: (c_ref[0] * per + i, 0)),
                    pl.BlockSpec((tr, wide), lambda i, c_ref: (i, 0))]
        out_specs = pl.BlockSpec((4, tr, cols), lambda i, c_ref: (0, i, 0))
        grid = (per,)
    else:
        def body(c_ref, a_ref, b_ref, o_ref):
            o_ref[...] = (a_ref[...] + b_ref[...]).astype(out_dtype)

        in_specs = [pl.BlockSpec((1, tr, cols), lambda s, i, c_ref: (s, c_ref[0] * per + i, 0)),
                    pl.BlockSpec((1, tr, cols), lambda s, i, c_ref: (s, i, 0))]
        out_specs = pl.BlockSpec((1, tr, cols), lambda s, i, c_ref: (s, i, 0))
        grid = (4, per)
    return pl.pallas_call(
        body, name=name,
        grid_spec=pltpu.PrefetchScalarGridSpec(num_scalar_prefetch=1, grid=grid, in_specs=in_specs, out_specs=out_specs),
        out_shape=S((4, hr, cols), out_dtype), compiler_params=_params(len(grid)),
    )(jnp.reshape(c, (1,)).astype(jnp.int32), full, got)


def _small_add_own_half(fulls, gots, c):
    n = len(fulls)

    def body(c_ref, *refs):
        for a_ref, b_ref, o_ref in zip(refs[:n], refs[n:2 * n], refs[2 * n:]):
            hr = b_ref.shape[0]
            o_ref[...] = a_ref[pl.ds(pl.multiple_of(c_ref[0] * hr, 8), hr), :] + b_ref[...]

    specs = lambda arrs: [pl.BlockSpec(a.shape, lambda i, c_ref: (0, 0)) for a in arrs]
    return pl.pallas_call(
        body, name="small_pair_add",
        grid_spec=pltpu.PrefetchScalarGridSpec(num_scalar_prefetch=1, grid=(1,), in_specs=specs(fulls) + specs(gots),
                                               out_specs=specs(gots)),
        out_shape=[S(g.shape, F32) for g in gots], compiler_params=_params(),
    )(jnp.reshape(c, (1,)).astype(jnp.int32), *fulls, *gots)


def _small_sum_slots(slots, c):
    n = len(slots)

    def body(c_ref, *refs):
        for s_ref, o_ref in zip(refs[:n], refs[n:]):
            hr = s_ref.shape[1]
            o_ref[pl.ds(pl.multiple_of(c_ref[0] * hr, 8), hr), :] = ((s_ref[0] + s_ref[1]) + s_ref[2]) + s_ref[3]

    outs = [S((2 * s.shape[1], s.shape[2]), F32) for s in slots]
    return pl.pallas_call(
        body, name="small_quad_sum",
        grid_spec=pltpu.PrefetchScalarGridSpec(
            num_scalar_prefetch=1, grid=(1,),
            in_specs=[pl.BlockSpec(s.shape, lambda i, c_ref: (0, 0, 0)) for s in slots],
            out_specs=[pl.BlockSpec(o.shape, lambda i, c_ref: (0, 0)) for o in outs]),
        out_shape=outs, compiler_params=_params(),
    )(jnp.reshape(c, (1,)).astype(jnp.int32), *slots)


def _sum_slots(name, own, slots, me, c):
    _, rows, cols = slots.shape
    tr = _row_tile(rows, cols, 16 if slots.dtype == jnp.bfloat16 else 8)
    per = rows // tr
    three = len(own.shape) == 3

    def body(p_ref, own_ref, s0, s1, s2, s3, o_ref):
        mine = own_ref[0] if three else own_ref[...]
        acc = None
        for j, s_ref in enumerate((s0, s1, s2, s3)):
            v = jnp.where(p_ref[0] == j, mine, s_ref[0]).astype(F32)
            acc = v if acc is None else acc + v
        o_ref[...] = acc

    def slot_spec(j):
        return pl.BlockSpec((1, tr, cols), lambda i, p: (jnp.where(p[0] == j, (j + 1) % 4, j), i, 0))

    own_spec = (pl.BlockSpec((1, tr, cols), lambda i, p: (p[0], i, 0)) if three
                else pl.BlockSpec((tr, cols), lambda i, p: (i, 0)))
    return pl.pallas_call(
        body, name=name,
        grid_spec=pltpu.PrefetchScalarGridSpec(
            num_scalar_prefetch=1, grid=(per,), in_specs=[own_spec] + [slot_spec(j) for j in range(4)],
            out_specs=pl.BlockSpec((tr, cols), lambda i, p: (p[1] * per + i, 0))),
        out_shape=S((2 * rows, cols), F32), compiler_params=_params(),
    )(jnp.stack([me, c]).astype(jnp.int32), own, slots, slots, slots, slots)


BIG = ("w_in", "w_out", "w_gate", "w_up", "w_down")
ROW_PARAMS = (("pre_mix_norm", 0), ("lru_conv_b", 12), ("lru_ba", 13), ("lru_bx", 14), ("lru_lambda", 15),
              ("lru_out_norm", 16), ("ssd_out_norm", 24), ("post_mix_norm", 33), ("pre_ffn_norm", 32), ("post_ffn_norm", 41))
LRU_CONV_ROWS = (8, 12)
LOSS_ROW = 40
HEAD_PARAMS = (("ssd_dt_bias", 0), ("ssd_a_log", 1), ("ssd_d", 2))
SMALL = tuple(n for n, _ in ROW_PARAMS) + ("ssd_conv_b",) + tuple(n for n, _ in HEAD_PARAMS) + (
    "lru_wa", "lru_wx", "lru_conv_w", "ssd_conv_w")


def _diag4(w):
    eye = jnp.eye(4, dtype=w.dtype).reshape(1, 4, 1, 4, 1)
    return (w.reshape(4, 4, BW, 1, BW) * eye).reshape(4, 4 * BW, 4 * BW)


def _adam_math(w, g, m, v):
    mm = ADAM_B1 * m + (1.0 - ADAM_B1) * g
    vv = ADAM_B2 * v + (1.0 - ADAM_B2) * (g * g)
    c1 = 1.0 - ADAM_B1 ** ADAM_STEP
    c2 = 1.0 - ADAM_B2 ** ADAM_STEP
    return -ADAM_LR * ((mm / c1) / (jnp.sqrt(vv / c2) + ADAM_EPS) + ADAM_WD * w), mm, vv


def _adamw_small(rows, cst, hst, dwa, dwx, glcw, gscw, w, m, v):
    def grad_of(name, refs):
        rows_ref, cst_ref, hst_ref, dwa_ref, dwx_ref, glcw_ref, gscw_ref = refs
        for n, r in ROW_PARAMS:
            if n == name:
                return rows_ref[r:r + 1, :]
        for n, r in HEAD_PARAMS:
            if n == name:
                return hst_ref[r:r + 1, 0:NH]
        return {"ssd_conv_b": lambda: cst_ref[4:5, :], "lru_wa": lambda: dwa_ref[...], "lru_wx": lambda: dwx_ref[...],
                "lru_conv_w": lambda: glcw_ref[...], "ssd_conv_w": lambda: gscw_ref[...]}[name]()

    shapes = {n: (w[n].shape[1:] if len(w[n].shape) > 2 else w[n].shape) for n in SMALL}
    flat = lambda d: [d[n].reshape(shapes[n]) for n in SMALL]
    ns = len(SMALL)

    def body(*refs):
        srcs, rest = refs[:7], refs[7:]
        w_refs, m_refs, v_refs = rest[:ns], rest[ns:2 * ns], rest[2 * ns:3 * ns]
        outs = rest[3 * ns:]
        for k, name in enumerate(SMALL):
            g = grad_of(name, srcs)
            d, mm, vv = _adam_math(w_refs[k][...], g, m_refs[k][...], v_refs[k][...])
            outs[4 * k][...] = g
            outs[4 * k + 1][...] = d
            outs[4 * k + 2][...] = mm
            outs[4 * k + 3][...] = vv

    res = pl.pallas_call(
        body, name="adamw_small",
        out_shape=[S(shapes[n], F32) for n in SMALL for _ in range(4)],
        compiler_params=pltpu.CompilerParams(vmem_limit_bytes=VMEM_LIMIT),
    )(rows, cst, hst, dwa, dwx, glcw, gscw, *flat(w), *flat(m), *flat(v))
    return {n: tuple(res[4 * k + i].reshape(w[n].shape) for i in range(4)) for k, n in enumerate(SMALL)}


def _with_own(own, got):
    chip = 2 * lax.axis_index("x") + lax.axis_index("y")
    return jnp.where((jnp.arange(4) == chip).reshape(4, 1, 1), own[None], got)


def _side_by_side(f):
    return f.transpose(1, 0, 2).reshape(f.shape[1], 4 * f.shape[2])


def _stacked(f):
    return f.reshape(4 * f.shape[1], f.shape[2])


def _gather_first_weights(w_in, lru_conv_w, ssd_conv_w):
    conv = jnp.concatenate([lru_conv_w.reshape(-1), ssd_conv_w.reshape(-1)]).astype(F32)
    hi = conv.astype(jnp.bfloat16)
    mid = (conv - hi.astype(F32)).astype(jnp.bfloat16)
    lo = (conv - hi.astype(F32) - mid.astype(F32)).astype(jnp.bfloat16)
    terms = jnp.concatenate([hi, mid, lo])
    n_terms = terms.shape[0]
    conv_rows = -(-n_terms // (128 * 32)) * 32
    terms = jnp.pad(terms, (0, conv_rows * 128 - n_terms)).reshape(conv_rows, 128)
    own = [w_in.astype(WIRE), terms]
    got = _allgather_weights(own)
    win_f = _side_by_side(_with_own(own[0], got[0]))
    t3 = _with_own(own[1], got[1]).reshape(4, -1)[:, :n_terms].reshape(4, 3, -1).astype(F32)
    conv_f = (t3[:, 0] + t3[:, 1]) + t3[:, 2]
    n1 = lru_conv_w.size
    lcw = conv_f[:, :n1].reshape(4, CONV_K, -1).transpose(1, 0, 2).reshape(CONV_K, LW)
    scw = conv_f[:, n1:].reshape(4, CONV_K, -1).transpose(1, 0, 2).reshape(CONV_K, XBC)
    return win_f, lcw, scw


def _pair_stage(tag, bufs, by_columns, small, c):
    nb = len(bufs)
    got = list(_pair_exchange("pair_exchange_" + tag, list(bufs) + list(small)))
    part = [_add_own_half("pair_add_%s%d" % (tag, k), b, r, c, WIRE, bc)
            for k, (b, r, bc) in enumerate(zip(bufs, got[:nb], by_columns))]
    part_small = list(_small_add_own_half(list(small), got[nb:], c)) if small else []
    return part, part_small


def _step(x, tgt, win_f, lcw, scw, sp, late):
    c = lax.axis_index("c")
    me = 2 * lax.axis_index("x") + lax.axis_index("y")
    mm = lambda w: w.astype(BF)
    wcat = jnp.concatenate([mm(win_f), jnp.zeros((D, PC - IN_COLS), BF)], axis=1)
    row = lambda v: v.reshape(1, -1).astype(F32)
    p_lru = jnp.concatenate([lcw, row(sp["lru_conv_b"]), row(sp["lru_ba"]), row(sp["lru_bx"]), row(sp["lru_lambda"]),
                             row(sp["lru_out_norm"]), jnp.zeros((7, LW), F32)], axis=0)
    wa4, wx4 = mm(_diag4(sp["lru_wa"][0])), mm(_diag4(sp["lru_wx"][0]))
    wa4T, wx4T = wa4.transpose(0, 2, 1), wx4.transpose(0, 2, 1)
    cw_ssd = jnp.concatenate([scw, row(sp["ssd_conv_b"]), jnp.zeros((3, XBC), F32)], axis=0)
    padh = lambda v: jnp.pad(row(v), ((0, 0), (0, DTP - NH)))
    hp_ssd = jnp.concatenate([padh(sp["ssd_dt_bias"]), padh(sp["ssd_a_log"]), padh(sp["ssd_d"]), jnp.zeros((5, DTP), F32)], axis=0)
    g0, g_ssd = row(sp["pre_mix_norm"]), row(sp["ssd_out_norm"])
    g_pm, g_pf, g_pff = row(sp["post_mix_norm"]), row(sp["pre_ffn_norm"]), row(sp["post_ffn_norm"])

    h0 = _prenorm(x, g0)
    h, ylru, lxc, lxr, lg, *got_a = _lru_fwd(h0, wcat, p_lru, wa4, wx4, [late[0], late[3]])
    y, yssd, states, cv, z, xbcr, dtr, *got_b = _ssd_fwd(h0, wcat, cw_ssd, hp_ssd, g_ssd, [late[1], late[2]])
    wout, wd = mm(_stacked(_with_own(late[0], got_a[0]))), mm(_stacked(_with_own(late[3], got_a[1])))
    wg, wu = mm(_side_by_side(_with_own(late[1], got_b[0]))), mm(_side_by_side(_with_own(late[2], got_b[1])))
    mix, x1, h2 = _outproj(ylru, yssd, x, wout, g_pm, g_pf)
    gate, up, act, df, dx2, st_ffn = _ffn_fwd(h2, x1, tgt, wg, wu, wd, g_pff)
    dgate, dup, dh2 = _ffn_bwd(df, gate, up, wd.T, wg.T, wu.T)
    dx1, dmix, dyl, dys, st_mix = _mix_bwd(dh2, x1, dx2, mix, wout.T, g_pf, g_pm)

    dwg = _wgrad("wgrad_gate", h2, dgate)
    dwu = _wgrad("wgrad_up", h2, dup)
    dwd = _wgrad("wgrad_down", act, df)
    dwo = jnp.concatenate([_wgrad("wgrad_out_lru", ylru, dmix), _wgrad("wgrad_out_ssd", yssd, dmix)], axis=0)
    early = [dwo.reshape(4, (LW + SI) // 4, D), dwg, dwu, dwd.reshape(4, DFF // 4, D)]
    part_early, _ = _pair_stage("early", early, [False, True, True, False], [], c)

    dlx, dlg, st_lru, dwa, dwx = _lru_bwd(dyl, lxr, lxc, lg, h, p_lru, wa4, wx4, wa4T, wx4T)
    dxbc, dz, ddt, cst, hst, gst, *slots_early = _ssd_bwd(dys, xbcr, cv, z, dtr, y, states, cw_ssd, hp_ssd, g_ssd,
                                                          part_early)
    gx, st_in = _inproj_bwd(dlx, dlg, dz, dxbc, ddt, x, dx1, wcat.T, g0)
    red_early = [_sum_slots("quad_sum_early%d" % k, p, s, me, c) for k, (p, s) in enumerate(zip(part_early, slots_early))]

    pin = [_wgrad("wgrad_in_%d" % k, h0, b) for k, b in enumerate((dlx, dlg, dz, dxbc, ddt))]
    dwin = jnp.concatenate(pin[:4] + [pin[4][:, :NH]], axis=1)
    rows = jnp.concatenate([st_in, st_lru, gst, st_mix, st_ffn], axis=0)
    small = [rows, cst, hst, dwa.reshape(NBLK * BW, BW), dwx.reshape(NBLK * BW, BW)]
    part, part_small = _pair_stage("late", [dwin], [True], small, c)
    slots = list(_quad_exchange(part + part_small, [True] + [False] * len(small)))
    red = [_sum_slots("quad_sum_late", part[0], slots[0], me, c)]
    red_small = list(_small_sum_slots(slots[1:], c))
    out = list(_pair_gather(red + red_early + red_small))
    big = dict(zip(("w_in", "w_out", "w_gate", "w_up", "w_down"), out[:5]))
    return gx, big, out[5:]


def kernel(x, pre_mix_norm, w_in, lru_conv_w, lru_conv_b, lru_wa, lru_ba, lru_wx, lru_bx, lru_lambda, lru_out_norm, ssd_conv_w, ssd_conv_b, ssd_dt_bias, ssd_a_log, ssd_d, ssd_out_norm, w_out, post_mix_norm, pre_ffn_norm, w_gate, w_up, w_down, post_ffn_norm, loss_target, m_pre_mix_norm, m_w_in, m_lru_conv_w, m_lru_conv_b, m_lru_wa, m_lru_ba, m_lru_wx, m_lru_bx, m_lru_lambda, m_lru_out_norm, m_ssd_conv_w, m_ssd_conv_b, m_ssd_dt_bias, m_ssd_a_log, m_ssd_d, m_ssd_out_norm, m_w_out, m_post_mix_norm, m_pre_ffn_norm, m_w_gate, m_w_up, m_w_down, m_post_ffn_norm, v_pre_mix_norm, v_w_in, v_lru_conv_w, v_lru_conv_b, v_lru_wa, v_lru_ba, v_lru_wx, v_lru_bx, v_lru_lambda, v_lru_out_norm, v_ssd_conv_w, v_ssd_conv_b, v_ssd_dt_bias, v_ssd_a_log, v_ssd_d, v_ssd_out_norm, v_w_out, v_post_mix_norm, v_pre_ffn_norm, v_w_gate, v_w_up, v_w_down, v_post_ffn_norm):
    args = dict(locals())
    names = list(SMALL) + list(BIG)
    w = {n: args[n] for n in names}
    m = {n: args["m_" + n] for n in names}
    v = {n: args["v_" + n] for n in names}
    chip = 2 * lax.axis_index("x") + lax.axis_index("y")

    win_f, lcw, scw = _gather_first_weights(w_in[0], lru_conv_w[0], ssd_conv_w[0])
    late = [a[0].astype(WIRE) for a in (w_out, w_gate, w_up, w_down)]
    gx, red, (rows, cst, hst, dwa, dwx) = _step(x[0], loss_target[0], win_f, lcw, scw, {n: w[n] for n in SMALL}, late)
    loss = jnp.sum(rows[LOSS_ROW])

    grads, delta, new_m, new_v = {}, {}, {}, {}
    for n in BIG:
        g = red[n]
        if n in ("w_in", "w_gate", "w_up"):
            t = lambda a: jnp.swapaxes(a, 1, 2)
            gt = g.T
            out = _adamw("adamw_" + n, t(w[n]), gt, t(m[n]), t(v[n]))
            delta[n], new_m[n], new_v[n] = (t(o) for o in out)
            grads[n] = t(gt[None])
        else:
            delta[n], new_m[n], new_v[n] = _adamw("adamw_" + n, w[n], g, m[n], v[n])
            grads[n] = g[None]

    lc, sc = lru_conv_w.shape[-1], ssd_conv_w.shape[-1]
    glcw = lax.dynamic_slice_in_dim(rows[LRU_CONV_ROWS[0]:LRU_CONV_ROWS[1]], chip * lc, lc, axis=1)
    gscw = lax.dynamic_slice_in_dim(cst[0:CONV_K], chip * sc, sc, axis=1)
    res = _adamw_small(rows, cst, hst, dwa.reshape(NBLK, BW, BW), dwx.reshape(NBLK, BW, BW), glcw, gscw,
                       {n: w[n] for n in SMALL}, {n: m[n] for n in SMALL}, {n: v[n] for n in SMALL})
    for n in SMALL:
        grads[n], delta[n], new_m[n], new_v[n] = res[n]

    order = ["pre_mix_norm", "w_in", "lru_conv_w", "lru_conv_b", "lru_wa", "lru_ba", "lru_wx", "lru_bx", "lru_lambda",
             "lru_out_norm", "ssd_conv_w", "ssd_conv_b", "ssd_dt_bias", "ssd_a_log", "ssd_d", "ssd_out_norm", "w_out",
             "post_mix_norm", "pre_ffn_norm", "w_gate", "w_up", "w_down", "post_ffn_norm"]
    return (loss, gx[None], *[grads[n] for n in order], *[delta[n] for n in order],
            *[new_m[n] for n in order], *[new_v[n] for n in order])
```

```python
import functools

import jax
import jax.numpy as jnp
from jax import lax
from jax.experimental import pallas as pl
from jax.experimental.pallas import tpu as pltpu

F32 = jnp.float32
BF = jnp.bfloat16

D = 1024
LW = 1024
NBLK = 16
BW = 64
SI = 1024
NH = 16
HD = 64
NG = 2
HPG = NH // NG
NS = 128
CH = 128
XBC = SI + 2 * NG * NS
DTP = 128
PC = 3 * 1024 + XBC + DTP
DFF = 2816
IN_COLS = 4624
EPS = 1e-6
LRU_C = 8.0
CONV_K = 4
TT = 256
VMEM_LIMIT = 56 * 1024 * 1024

ADAM_LR, ADAM_B1, ADAM_B2, ADAM_EPS, ADAM_WD, ADAM_STEP = 0.001, 0.9, 0.999, 1e-08, 0.01, 10

MESH = pl.DeviceIdType.MESH


def _mm(a, b):
    return jnp.dot(a.astype(BF), b.astype(BF), preferred_element_type=F32)


def _mm_nt(a, b):
    return lax.dot_general(a.astype(BF), b.astype(BF), (((1,), (1,)), ((), ())), preferred_element_type=F32)


def _mm_tn(a, b):
    return lax.dot_general(a.astype(BF), b.astype(BF), (((0,), (0,)), ((), ())), preferred_element_type=F32)


def _sigmoid(x):
    return 0.5 * jnp.tanh(0.5 * x) + 0.5


def _softplus(x):
    return jnp.maximum(x, 0.0) + jnp.log1p(jnp.exp(-jnp.abs(x)))


_GELU_C = 0.7978845608028654
_GELU_K = 0.044715


def _gelu(x):
    t = jnp.tanh(_GELU_C * (x + _GELU_K * x * x * x))
    return 0.5 * x * (1.0 + t)


def _gelu_grad(x):
    t = jnp.tanh(_GELU_C * (x + _GELU_K * x * x * x))
    return 0.5 * (1.0 + t) + 0.5 * x * (1.0 - t * t) * _GELU_C * (1.0 + 3.0 * _GELU_K * x * x)


def _rms_fwd(x, g):
    r = lax.rsqrt(jnp.mean(x * x, axis=-1, keepdims=True) + EPS)
    return x * r * g


def _rms_bwd(x, g, dy):
    r = lax.rsqrt(jnp.mean(x * x, axis=-1, keepdims=True) + EPS)
    xh = x * r
    dxh = dy * g
    dg = jnp.sum(dy * xh, axis=0, keepdims=True)
    dx = r * (dxh - xh * jnp.mean(dxh * xh, axis=-1, keepdims=True))
    return dx, dg


def _sum_all(x):
    return jnp.sum(jnp.sum(x, axis=1, keepdims=True), axis=0, keepdims=True)


def _cumsum_rows(x, n):
    row = lax.broadcasted_iota(jnp.int32, x.shape, 0)
    k = 1
    while k < n:
        x = x + jnp.where(row >= k, pltpu.roll(x, k, 0), 0.0)
        k *= 2
    return x


def _rev_cumsum_rows(x, n):
    row = lax.broadcasted_iota(jnp.int32, x.shape, 0)
    k = 1
    while k < n:
        x = x + jnp.where(row < n - k, pltpu.roll(x, n - k, 0), 0.0)
        k *= 2
    return x


def _load_once(pairs, sem):
    @pl.when(pl.program_id(0) == 0)
    def _():
        for k, (src, dst) in enumerate(pairs):
            pltpu.make_async_copy(src, dst, sem.at[k]).start()
        for k, (src, dst) in enumerate(pairs):
            pltpu.make_async_copy(src, dst, sem.at[k]).wait()


def _params(n_axes=1):
    return pltpu.CompilerParams(dimension_semantics=("arbitrary",) * n_axes, vmem_limit_bytes=VMEM_LIMIT)


def _rows(n, width, rev_of=None):
    if rev_of is None:
        return pl.BlockSpec((n, width), lambda i: (i, 0))
    return pl.BlockSpec((n, width), lambda i: (rev_of - 1 - i, 0))


def _whole(shape):
    nd = len(shape)
    return pl.BlockSpec(shape, lambda i: (0,) * nd)


ANY = pl.BlockSpec(memory_space=pl.ANY)
S = jax.ShapeDtypeStruct
WIRE = jnp.bfloat16


def _pos():
    return lax.axis_index("x"), lax.axis_index("y"), lax.axis_index("c")


def _other_chips(x, y):
    return [(1 - x, y), (x, 1 - y), (1 - x, 1 - y)]


def _remote(src, dst, send_sem, recv_sem, to):
    return pltpu.make_async_remote_copy(src_ref=src, dst_ref=dst, send_sem=send_sem, recv_sem=recv_sem,
                                        device_id=to, device_id_type=MESH)


def _gather_phase(phase, ins, outs, send_sems, recv_sems):
    x, y, c = _pos()
    me = 2 * x + y
    chips = _other_chips(x, y)
    for i, (src, dst) in enumerate(zip(ins, outs)):
        hr = src.shape[0] // 2
        my_half = pl.ds(pl.multiple_of(c * hr, 16), hr)
        sib_half = pl.ds(pl.multiple_of((1 - c) * hr, 16), hr)
        for k, (cx, cy) in enumerate(chips):
            s1, r1 = send_sems.at[6 * i + k], recv_sems.at[6 * i + k]
            s2, r2 = send_sems.at[6 * i + 3 + k], recv_sems.at[6 * i + 3 + k]
            first = lambda: _remote(src.at[my_half, :], dst.at[me, my_half, :], s1, r1, (cx, cy, c))
            landed = dst.at[2 * cx + cy, my_half, :]
            passed = lambda: _remote(landed, landed, s2, r2, (x, y, 1 - c))
            if phase == 0:
                first().start()
            elif phase == 1:
                _remote(landed, landed, s1, r1, (cx, cy, c)).wait_recv()
                passed().start()
            else:
                theirs = dst.at[2 * cx + cy, sib_half, :]
                _remote(theirs, theirs, s2, r2, (x, y, 1 - c)).wait_recv()
                first().wait_send()
                passed().wait_send()


def _half(ref, c, hr):
    sl = pl.ds(pl.multiple_of(c * hr, 8), hr)
    return ref.at[:, sl, :] if len(ref.shape) == 3 else ref.at[sl, :]


def _half_shape(b):
    return b.shape[:-2] + (b.shape[-2] // 2, b.shape[-1])


def _pair_phase(phase, ins, outs, send_sems, recv_sems):
    x, y, c = _pos()
    for k, (src, dst) in enumerate(zip(ins, outs)):
        cp = _remote(_half(src, 1 - c, src.shape[-2] // 2), dst, send_sems.at[k], recv_sems.at[k], (x, y, 1 - c))
        if phase == 0:
            cp.start()
        else:
            cp.wait()


def _quad_phase(phase, ins, outs, send_sems, recv_sems):
    x, y, c = _pos()
    me = 2 * x + y
    for i, (src, dst) in enumerate(zip(ins, outs)):
        for k, (cx, cy) in enumerate(_other_chips(x, y)):
            cp = _remote(src.at[2 * cx + cy], dst.at[me], send_sems.at[3 * i + k], recv_sems.at[3 * i + k], (cx, cy, c))
            if phase == 0:
                cp.start()
            else:
                got = dst.at[2 * cx + cy]
                _remote(got, got, send_sems.at[3 * i + k], recv_sems.at[3 * i + k], (cx, cy, c)).wait_recv()
                cp.wait_send()


def _prenorm(x, g0, shards):
    T = x.shape[0]
    tt = 2 * TT
    nt = T // tt
    ng = len(shards)

    def body(*refs):
        x_ref, g_ref = refs[:2]
        sh_in = refs[2:2 + ng]
        h0_ref = refs[2 + ng]
        sh_out = refs[3 + ng:3 + 2 * ng]
        send_sems, recv_sems = refs[3 + 2 * ng:]
        for phase, step in enumerate((0, nt // 2, nt - 1)):
            @pl.when(pl.program_id(0) == step)
            def _():
                _gather_phase(phase, sh_in, sh_out, send_sems, recv_sems)

        h0_ref[...] = _rms_fwd(x_ref[...], g_ref[...]).astype(BF)

    return pl.pallas_call(
        body, name="prenorm", grid=(nt,),
        in_specs=[_rows(tt, D), _whole((1, D))] + [ANY] * ng, out_specs=[_rows(tt, D)] + [ANY] * ng,
        out_shape=[S((T, D), BF)] + [S((4,) + s.shape, s.dtype) for s in shards],
        scratch_shapes=[pltpu.SemaphoreType.DMA((6 * ng,)), pltpu.SemaphoreType.DMA((6 * ng,))],
        compiler_params=_params(),
    )(x, g0, *shards)


def _blockdiag_mm(v, w4_ref):
    return jnp.concatenate([_mm(v[:, 256 * j:256 * (j + 1)], w4_ref[j]) for j in range(4)], axis=1)


def _lru_gates(lx, p_ref, wa_ref, wx_ref):
    r = _sigmoid(_blockdiag_mm(lx, wa_ref) + p_ref[5:6, :])
    i = _sigmoid(_blockdiag_mm(lx, wx_ref) + p_ref[6:7, :])
    sp = _softplus(-p_ref[7:8, :])
    la = -LRU_C * r * sp
    a = jnp.exp(la)
    th = jnp.tanh(la)
    mult = jnp.sqrt(-2.0 * th / (1.0 - th))
    return r, i, sp, a, mult


def _conv_from(xp_ref, p_ref, n):
    acc = p_ref[4:5, :] + p_ref[0:1, :] * xp_ref[pl.ds(8 - CONV_K + 1, n), :]
    for k in range(1, CONV_K):
        acc = acc + p_ref[k:k + 1, :] * xp_ref[pl.ds(8 - CONV_K + 1 + k, n), :]
    return acc


def _conv_bwd(dp_ref, dconv, x, p_ref, st_ref, n):
    dp_ref[0:n, :] = dconv
    acc = None
    for k in range(CONV_K):
        g = dp_ref[pl.ds(CONV_K - 1 - k, n), :]
        acc = p_ref[k:k + 1, :] * g if acc is None else acc + p_ref[k:k + 1, :] * g
        st_ref[k:k + 1, :] += jnp.sum(g * x, axis=0, keepdims=True)
    st_ref[4:5, :] += jnp.sum(dconv, axis=0, keepdims=True)
    dp_ref[n:n + 8, :] = dp_ref[0:8, :]
    return acc


def _lru_fwd(h0, wcat, p_lru, wa4, wx4, shards):
    T = h0.shape[0]
    NT = T // TT
    ng = len(shards)

    def body(*refs):
        h0_ref, w_hbm, p_ref, wa_ref, wx_ref = refs[:5]
        sh_in = refs[5:5 + ng]
        h_ref, y_ref, lxc_ref, lxr_ref, lg_ref = refs[5 + ng:10 + ng]
        sh_out = refs[10 + ng:10 + 2 * ng]
        xp, a_s, u_s, hc, w_vm, wsem, send_sems, recv_sems = refs[10 + 2 * ng:]
        _load_once([(w_hbm.at[:, 0:2 * LW], w_vm)], wsem)
        for phase, step in enumerate((0, NT // 2, NT - 1)):
            @pl.when(pl.program_id(0) == step)
            def _():
                _gather_phase(phase, sh_in, sh_out, send_sems, recv_sems)

        @pl.when(pl.program_id(0) == 0)
        def _():
            xp[0:8, :] = jnp.zeros((8, LW), F32)
            hc[...] = jnp.zeros_like(hc)

        hv = h0_ref[...]
        lxr = jnp.dot(hv, w_vm[:, 0:LW], preferred_element_type=F32)
        lxr_ref[...] = lxr
        lg_ref[...] = jnp.dot(hv, w_vm[:, LW:2 * LW], preferred_element_type=F32)
        xp[8:8 + TT, :] = lxr
        lx = _conv_from(xp, p_ref, TT)
        lxc_ref[...] = lx
        xp[0:8, :] = xp[TT:TT + 8, :]
        r, i, sp, a, mult = _lru_gates(lx, p_ref, wa_ref, wx_ref)
        a_s[...] = a
        u_s[...] = mult * (i * lx)

        def step(t, h):
            h = a_s[pl.ds(t, 1), :] * h + u_s[pl.ds(t, 1), :]
            h_ref[pl.ds(t, 1), :] = h
            return h

        hc[0:1, :] = lax.fori_loop(0, TT, step, hc[0:1, :], unroll=8)
        gated = h_ref[...] * _gelu(lg_ref[...])
        y_ref[...] = _rms_fwd(gated, p_ref[8:9, :]).astype(BF)

    return pl.pallas_call(
        body, name="lru_fwd", grid=(NT,),
        in_specs=[_rows(TT, D), ANY, _whole((16, LW)), _whole((4, 256, 256)), _whole((4, 256, 256))] + [ANY] * ng,
        out_specs=[_rows(TT, LW), _rows(TT, LW), _rows(TT, LW), _rows(TT, LW), _rows(TT, LW)] + [ANY] * ng,
        out_shape=[S((T, LW), F32), S((T, LW), BF), S((T, LW), F32), S((T, LW), F32), S((T, LW), F32)]
        + [S((4,) + s.shape, s.dtype) for s in shards],
        scratch_shapes=[pltpu.VMEM((TT + 8, LW), F32), pltpu.VMEM((TT, LW), F32), pltpu.VMEM((TT, LW), F32),
                        pltpu.VMEM((8, LW), F32), pltpu.VMEM((D, 2 * LW), BF), pltpu.SemaphoreType.DMA((1,)),
                        pltpu.SemaphoreType.DMA((6 * ng,)), pltpu.SemaphoreType.DMA((6 * ng,))],
        compiler_params=_params(),
    )(h0, wcat, p_lru, wa4, wx4, *shards)


def _ssd_prep(cv, dt_ref, hp_ref):
    sg = _sigmoid(cv)
    xbc = cv * sg
    lane = lax.broadcasted_iota(jnp.int32, (CH, DTP), 1)
    raw = dt_ref[...] + hp_ref[0:1, :]
    dtv = jnp.where(lane < NH, _softplus(raw), 0.0)
    A = jnp.where(lane[0:1, :] < NH, -jnp.exp(hp_ref[1:2, :]), 0.0)
    cs = _cumsum_rows(dtv * A, CH)
    return sg, xbc, raw, dtv, A, cs


def _per_head_lanes(v):
    r = v.shape[0]
    first = lax.broadcasted_iota(jnp.int32, (r, 2 * HD), 1) < HD
    pairs = [jnp.where(first, jnp.broadcast_to(v[:, 2 * j:2 * j + 1], (r, 2 * HD)),
                       jnp.broadcast_to(v[:, 2 * j + 1:2 * j + 2], (r, 2 * HD))) for j in range(NH // 2)]
    return jnp.concatenate(pairs, axis=1)


def _per_head_rows(col, g):
    return jnp.concatenate([jnp.broadcast_to(col[g * HPG + k:g * HPG + k + 1, :], (HD, NS)) for k in range(HPG)], axis=0)


def _ssd_decays(cs):
    csT = cs.T
    cl = cs[CH - 1:CH, :]
    E_x = _per_head_lanes(jnp.exp(cs))
    dsm = jnp.exp(cl - cs)
    ds_x = _per_head_lanes(dsm)
    El_rows = jnp.broadcast_to(jnp.exp(csT[0:NH, CH - 1:CH]), (NH, NS))
    return csT, dsm, E_x, ds_x, El_rows


def _ssd_fwd(h0, wcat, cw_ssd, hp_ssd, g_ssd, shards):
    T = h0.shape[0]
    NC = T // CH
    ng = len(shards)
    c0 = 2 * LW

    def body(*refs):
        h0_ref, w_hbm, cw_ref, hp_ref, g_ref = refs[:5]
        sh_in = refs[5:5 + ng]
        y_ref, yn_ref, st_ref, cv_ref, z_ref, xr_ref, dt_ref = refs[5 + ng:12 + ng]
        sh_out = refs[12 + ng:12 + 2 * ng]
        xp, st, w_vm, wsem, send_sems, recv_sems = refs[12 + 2 * ng:]
        _load_once([(w_hbm.at[:, c0:PC], w_vm)], wsem)
        for phase, step in enumerate((0, NC // 2, NC - 1)):
            @pl.when(pl.program_id(0) == step)
            def _():
                _gather_phase(phase, sh_in, sh_out, send_sems, recv_sems)

        @pl.when(pl.program_id(0) == 0)
        def _():
            xp[0:8, :] = jnp.zeros((8, XBC), F32)
            st[...] = jnp.zeros_like(st)

        hv = h0_ref[...]
        z_ref[...] = jnp.dot(hv, w_vm[:, 0:SI], preferred_element_type=F32)
        xraw = jnp.dot(hv, w_vm[:, SI:SI + XBC], preferred_element_type=F32)
        xr_ref[...] = xraw
        dt_ref[...] = jnp.dot(hv, w_vm[:, SI + XBC:SI + XBC + DTP], preferred_element_type=F32)
        xp[8:8 + CH, :] = xraw
        cv = _conv_from(xp, cw_ref, CH)
        cv_ref[...] = cv
        sg, xbc, raw, dtv, A, cs = _ssd_prep(cv, dt_ref, hp_ref)
        xp[0:8, :] = xp[CH:CH + 8, :]
        st_ref[0] = st[...]
        csT, dsm, E_x, ds_x, El_rows = _ssd_decays(cs)
        X = xbc[:, 0:SI]
        xs = X * _per_head_lanes(dtv)
        xsd = (xs * ds_x).astype(BF)
        DX = _per_head_lanes(hp_ref[...])[2:3, :] * X
        tril = lax.broadcasted_iota(jnp.int32, (CH, CH), 0) >= lax.broadcasted_iota(jnp.int32, (CH, CH), 1)
        first = lax.broadcasted_iota(jnp.int32, (CH, 2 * HD), 1) < HD
        GW = HPG * HD
        for g in range(NG):
            Bg = xbc[:, SI + NS * g:SI + NS * (g + 1)].astype(BF)
            Cg = xbc[:, SI + NG * NS + NS * g:SI + NG * NS + NS * (g + 1)].astype(BF)
            G = _mm_nt(Cg, Bg)
            Sg = st[GW * g:GW * (g + 1), :]
            Yo = _mm_nt(Cg, Sg) * E_x[:, GW * g:GW * (g + 1)]
            st[GW * g:GW * (g + 1), :] = _per_head_rows(El_rows, g) * Sg + _mm_tn(xsd[:, GW * g:GW * (g + 1)], Bg)
            for jj in range(HPG // 2):
                j = g * (HPG // 2) + jj
                ps = slice(2 * HD * j, 2 * HD * (j + 1))
                xs_pair = xs[:, ps]
                acc = Yo[:, 2 * HD * jj:2 * HD * (jj + 1)] + DX[:, ps]
                for e in range(2):
                    h = 2 * j + e
                    Lm = jnp.exp(jnp.where(tril, cs[:, h:h + 1] - csT[h:h + 1, :], -1e30))
                    acc = acc + _mm(G * Lm, jnp.where(first if e == 0 else ~first, xs_pair, 0.0))
                y_ref[:, ps] = acc
        zz = z_ref[...]
        gated = y_ref[...] * (zz * _sigmoid(zz))
        yn_ref[...] = _rms_fwd(gated, g_ref[...]).astype(BF)

    return pl.pallas_call(
        body, name="ssd_fwd", grid=(NC,),
        in_specs=[_rows(CH, D), ANY, _whole((8, XBC)), _whole((8, DTP)), _whole((1, SI))] + [ANY] * ng,
        out_specs=[_rows(CH, SI), _rows(CH, SI), pl.BlockSpec((1, NH * HD, NS), lambda i: (i, 0, 0)), _rows(CH, XBC),
                   _rows(CH, SI), _rows(CH, XBC), _rows(CH, DTP)] + [ANY] * ng,
        out_shape=[S((T, SI), F32), S((T, SI), BF), S((NC, NH * HD, NS), F32), S((T, XBC), F32),
                   S((T, SI), F32), S((T, XBC), F32), S((T, DTP), F32)] + [S((4,) + s.shape, s.dtype) for s in shards],
        scratch_shapes=[pltpu.VMEM((CH + 8, XBC), F32), pltpu.VMEM((NH * HD, NS), F32),
                        pltpu.VMEM((D, PC - c0), BF), pltpu.SemaphoreType.DMA((1,)),
                        pltpu.SemaphoreType.DMA((6 * ng,)), pltpu.SemaphoreType.DMA((6 * ng,))],
        compiler_params=_params(),
    )(h0, wcat, cw_ssd, hp_ssd, g_ssd, *shards)


def _outproj(ylru, yssd, x, wout, g_pm, g_pf):
    T = x.shape[0]

    def body(yl_ref, ys_ref, x_ref, w_hbm, gpm_ref, gpf_ref, mix_ref, x1_ref, h2_ref, w_vm, sem):
        _load_once([(w_hbm, w_vm)], sem)
        mix = (jnp.dot(yl_ref[...], w_vm[0:LW, :], preferred_element_type=F32)
               + jnp.dot(ys_ref[...], w_vm[LW:LW + SI, :], preferred_element_type=F32))
        mix_ref[...] = mix
        x1 = x_ref[...] + _rms_fwd(mix, gpm_ref[...])
        x1_ref[...] = x1
        h2_ref[...] = _rms_fwd(x1, gpf_ref[...]).astype(BF)

    return pl.pallas_call(
        body, name="outproj", grid=(T // TT,),
        in_specs=[_rows(TT, LW), _rows(TT, SI), _rows(TT, D), ANY, _whole((1, D)), _whole((1, D))],
        out_specs=[_rows(TT, D), _rows(TT, D), _rows(TT, D)],
        out_shape=[S((T, D), F32), S((T, D), F32), S((T, D), BF)],
        scratch_shapes=[pltpu.VMEM((LW + SI, D), BF), pltpu.SemaphoreType.DMA((1,))],
        compiler_params=_params(),
    )(ylru, yssd, x, wout, g_pm, g_pf)


def _ffn_fwd(h2, x1, tgt, wg, wu, wd, g_pff):
    T = x1.shape[0]

    def body(h2_ref, x1_ref, t_ref, wg_hbm, wu_hbm, wd_hbm, g_ref,
             gate_ref, up_ref, act_ref, df_ref, dx2_ref, st_ref, wg_vm, wu_vm, wd_vm, sem):
        _load_once([(wg_hbm, wg_vm), (wu_hbm, wu_vm), (wd_hbm, wd_vm)], sem)

        @pl.when(pl.program_id(0) == 0)
        def _():
            st_ref[...] = jnp.zeros_like(st_ref)

        h2 = h2_ref[...]
        gate = jnp.dot(h2, wg_vm[...], preferred_element_type=F32)
        up = jnp.dot(h2, wu_vm[...], preferred_element_type=F32)
        gate_ref[...] = gate
        up_ref[...] = up
        act = (gate * _sigmoid(gate) * up).astype(BF)
        act_ref[...] = act
        f = jnp.dot(act, wd_vm[...], preferred_element_type=F32)
        g = g_ref[...]
        x2 = x1_ref[...] + _rms_fwd(f, g)
        err = x2 - t_ref[...]
        st_ref[0:1, :] += 0.5 * jnp.sum(err * err, axis=0, keepdims=True) * (1.0 / D)
        dx2 = err * (1.0 / D)
        dx2_ref[...] = dx2
        df, dg = _rms_bwd(f, g, dx2)
        df_ref[...] = df.astype(BF)
        st_ref[1:2, :] += dg

    return pl.pallas_call(
        body, name="ffn_fwd", grid=(T // TT,),
        in_specs=[_rows(TT, D), _rows(TT, D), _rows(TT, D), ANY, ANY, ANY, _whole((1, D))],
        out_specs=[_rows(TT, DFF), _rows(TT, DFF), _rows(TT, DFF), _rows(TT, D), _rows(TT, D), _whole((8, D))],
        out_shape=[S((T, DFF), F32), S((T, DFF), F32), S((T, DFF), BF), S((T, D), BF), S((T, D), F32), S((8, D), F32)],
        scratch_shapes=[pltpu.VMEM((D, DFF), BF), pltpu.VMEM((D, DFF), BF), pltpu.VMEM((DFF, D), BF),
                        pltpu.SemaphoreType.DMA((3,))],
        compiler_params=_params(),
    )(h2, x1, tgt, wg, wu, wd, g_pff)


def _ffn_bwd(df, gate, up, wdT, wgT, wuT):
    T = df.shape[0]

    def body(df_ref, gate_ref, up_ref, wd_hbm, wg_hbm, wu_hbm, dgate_ref, dup_ref, dh2_ref, wd_vm, wg_vm, wu_vm, sem):
        _load_once([(wd_hbm, wd_vm), (wg_hbm, wg_vm), (wu_hbm, wu_vm)], sem)
        dact = jnp.dot(df_ref[...], wd_vm[...], preferred_element_type=F32)
        gate = gate_ref[...]
        s = _sigmoid(gate)
        dup = (dact * (gate * s)).astype(BF)
        dgate = (dact * up_ref[...] * (s + gate * s * (1.0 - s))).astype(BF)
        dup_ref[...] = dup
        dgate_ref[...] = dgate
        dh2_ref[...] = (jnp.dot(dgate, wg_vm[...], preferred_element_type=F32)
                        + jnp.dot(dup, wu_vm[...], preferred_element_type=F32))

    return pl.pallas_call(
        body, name="ffn_bwd", grid=(T // TT,),
        in_specs=[_rows(TT, D), _rows(TT, DFF), _rows(TT, DFF), ANY, ANY, ANY],
        out_specs=[_rows(TT, DFF), _rows(TT, DFF), _rows(TT, D)],
        out_shape=[S((T, DFF), BF), S((T, DFF), BF), S((T, D), F32)],
        scratch_shapes=[pltpu.VMEM((D, DFF), BF), pltpu.VMEM((DFF, D), BF), pltpu.VMEM((DFF, D), BF),
                        pltpu.SemaphoreType.DMA((3,))],
        compiler_params=_params(),
    )(df, gate, up, wdT, wgT, wuT)


def _mix_bwd(dh2, x1, dx2, mix, woutT, g_pf, g_pm):
    T = x1.shape[0]

    def body(dh2_ref, x1_ref, dx2_ref, mix_ref, w_hbm, gpf_ref, gpm_ref,
             dx1_ref, dmix_ref, dyl_ref, dys_ref, st_ref, w_vm, sem):
        _load_once([(w_hbm, w_vm)], sem)

        @pl.when(pl.program_id(0) == 0)
        def _():
            st_ref[...] = jnp.zeros_like(st_ref)

        dxa, dgpf = _rms_bwd(x1_ref[...], gpf_ref[...], dh2_ref[...])
        dx1 = dx2_ref[...] + dxa
        dx1_ref[...] = dx1
        dmix, dgpm = _rms_bwd(mix_ref[...], gpm_ref[...], dx1)
        dmix = dmix.astype(BF)
        dmix_ref[...] = dmix
        st_ref[0:1, :] += dgpf
        st_ref[1:2, :] += dgpm
        dyl_ref[...] = jnp.dot(dmix, w_vm[:, 0:LW], preferred_element_type=F32)
        dys_ref[...] = jnp.dot(dmix, w_vm[:, LW:LW + SI], preferred_element_type=F32)

    return pl.pallas_call(
        body, name="mix_bwd", grid=(T // TT,),
        in_specs=[_rows(TT, D), _rows(TT, D), _rows(TT, D), _rows(TT, D), ANY, _whole((1, D)), _whole((1, D))],
        out_specs=[_rows(TT, D), _rows(TT, D), _rows(TT, LW), _rows(TT, SI), _whole((8, D))],
        out_shape=[S((T, D), F32), S((T, D), BF), S((T, LW), F32), S((T, SI), F32), S((8, D), F32)],
        scratch_shapes=[pltpu.VMEM((D, LW + SI), BF), pltpu.SemaphoreType.DMA((1,))],
        compiler_params=_params(),
    )(dh2, x1, dx2, mix, woutT, g_pf, g_pm)


def _halo(width, n_tiles, tile):
    per = tile // 8
    return pl.BlockSpec((8, width), lambda i: (jnp.maximum((n_tiles - 1 - i) * per - 1, 0), 0))


def _lru_bwd(dy, lxr, lxc, lg, h, p_lru, wa4, wx4, wa4T, wx4T, bufs):
    T = dy.shape[0]
    NT = T // TT
    nb = len(bufs)

    def body(*refs):
        dy_ref, lxr_ref, lxc_ref, lg_ref, h_ref, hh_ref, p_ref, wa_ref, wx_ref, waT_ref, wxT_ref = refs[:11]
        b_in = refs[11:11 + nb]
        dlx_ref, dlg_ref, st_ref, dwa_ref, dwx_ref = refs[11 + nb:16 + nb]
        b_out = refs[16 + nb:16 + 2 * nb]
        hp, dp, a_s, d_s, g_s, cc, send_sems, recv_sems = refs[16 + 2 * nb:]
        for phase, step in enumerate((0, NT - 1)):
            @pl.when(pl.program_id(0) == step)
            def _():
                _pair_phase(phase, b_in, b_out, send_sems, recv_sems)

        dy = dy_ref[...]
        first = pl.program_id(0) == 0
        top = pl.program_id(0) == NT - 1

        @pl.when(first)
        def _():
            st_ref[...] = jnp.zeros_like(st_ref)
            dwa_ref[...] = jnp.zeros_like(dwa_ref)
            dwx_ref[...] = jnp.zeros_like(dwx_ref)
            dp[TT:TT + 8, :] = jnp.zeros((8, LW), F32)
            cc[...] = jnp.zeros_like(cc)

        hp[0:8, :] = hh_ref[...] * jnp.where(top, 0.0, 1.0)
        hp[8:8 + TT, :] = h_ref[...]
        lx = lxc_ref[...]
        r, i, sp, a, mult = _lru_gates(lx, p_ref, wa_ref, wx_ref)

        lg = lg_ref[...]
        hcur = h_ref[...]
        ge = _gelu(lg)
        dgated, dgn = _rms_bwd(hcur * ge, p_ref[8:9, :], dy)
        st_ref[8:9, :] += dgn
        dlg_ref[...] = (dgated * hcur * _gelu_grad(lg)).astype(BF)
        a_s[...] = a
        d_s[...] = dgated * ge

        def step(k, c):
            t = TT - 1 - k
            g = d_s[pl.ds(t, 1), :] + c
            g_s[pl.ds(t, 1), :] = g
            return a_s[pl.ds(t, 1), :] * g

        cc[0:1, :] = lax.fori_loop(0, TT, step, cc[0:1, :], unroll=8)
        gt = g_s[...]
        da = gt * hp[pl.ds(7, TT), :]
        dmult = gt * i * lx
        di = gt * mult * lx
        dlxc = gt * mult * i
        dla = da * a - dmult * (a * a) / mult
        dr = dla * (-LRU_C * sp)
        st_ref[7:8, :] += jnp.sum(dla * (-LRU_C * r), axis=0, keepdims=True) * (-_sigmoid(-p_ref[7:8, :]))
        dzr = dr * r * (1.0 - r)
        dzi = di * i * (1.0 - i)
        st_ref[5:6, :] += jnp.sum(dzr, axis=0, keepdims=True)
        st_ref[6:7, :] += jnp.sum(dzi, axis=0, keepdims=True)
        dlxc = dlxc + _blockdiag_mm(dzr, waT_ref) + _blockdiag_mm(dzi, wxT_ref)
        for j in range(4):
            sl = slice(256 * j, 256 * (j + 1))
            pa = _mm_tn(lx[:, sl], dzr[:, sl])
            px = _mm_tn(lx[:, sl], dzi[:, sl])
            for b in range(4):
                bs = slice(BW * b, BW * (b + 1))
                dwa_ref[4 * j + b] += pa[bs, bs]
                dwx_ref[4 * j + b] += px[bs, bs]
        dlx_ref[...] = _conv_bwd(dp, dlxc, lxr_ref[...], p_ref, st_ref, TT).astype(BF)

    w4 = _whole((4, 256, 256))
    return pl.pallas_call(
        body, name="lru_bwd", grid=(NT,),
        in_specs=[_rows(TT, LW, NT), _rows(TT, LW, NT), _rows(TT, LW, NT), _rows(TT, LW, NT), _rows(TT, LW, NT),
                  _halo(LW, NT, TT), _whole((16, LW)), w4, w4, w4, w4] + [ANY] * nb,
        out_specs=[_rows(TT, LW, NT), _rows(TT, LW, NT), _whole((16, LW)), _whole((NBLK, BW, BW)), _whole((NBLK, BW, BW))]
        + [ANY] * nb,
        out_shape=[S((T, LW), BF), S((T, LW), BF), S((16, LW), F32), S((NBLK, BW, BW), F32), S((NBLK, BW, BW), F32)]
        + [S(_half_shape(b), b.dtype) for b in bufs],
        scratch_shapes=[pltpu.VMEM((TT + 8, LW), F32), pltpu.VMEM((TT + 8, LW), F32),
                        pltpu.VMEM((TT, LW), F32), pltpu.VMEM((TT, LW), F32), pltpu.VMEM((TT, LW), F32),
                        pltpu.VMEM((8, LW), F32), pltpu.SemaphoreType.DMA((nb,)), pltpu.SemaphoreType.DMA((nb,))],
        compiler_params=_params(),
    )(dy, lxr, lxc, lg, h, h, p_lru, wa4, wx4, wa4T, wx4T, *bufs)


def _ssd_bwd(dyn, xbcr, cv, z, dtr, y, states, cw_ssd, hp_ssd, g_ssd, parts):
    T = dyn.shape[0]
    NC = T // CH
    nq = len(parts)

    def body(*refs):
        dyn_ref, xr_ref, cv_ref, z_ref, dt_ref, y_ref, st_ref, cw_ref, hp_ref, g_ref = refs[:10]
        q_in = refs[10:10 + nq]
        dxbc_ref, dz_ref, ddt_ref, cst_ref, hst_ref, gst_ref = refs[10 + nq:16 + nq]
        q_out = refs[16 + nq:16 + 2 * nq]
        dp, dS, dxb, yo_s, q_s, dxs_s, t1_s, send_sems, recv_sems = refs[16 + 2 * nq:]
        dyn = dyn_ref[...]
        first = pl.program_id(0) == 0
        for phase, step in enumerate((0, NC - 1)):
            @pl.when(pl.program_id(0) == step)
            def _():
                _quad_phase(phase, q_in, q_out, send_sems, recv_sems)

        @pl.when(first)
        def _():
            cst_ref[...] = jnp.zeros_like(cst_ref)
            hst_ref[...] = jnp.zeros_like(hst_ref)
            gst_ref[...] = jnp.zeros_like(gst_ref)
            dp[CH:CH + 8, :] = jnp.zeros((8, XBC), F32)
            dS[...] = jnp.zeros_like(dS)

        cv = cv_ref[...]
        sg, xbc, raw, dtv, A, cs = _ssd_prep(cv, dt_ref, hp_ref)
        csT, dsm, E_x, ds_x, El_rows = _ssd_decays(cs)
        row_i = lax.broadcasted_iota(jnp.int32, (CH, CH), 0)
        col_i = lax.broadcasted_iota(jnp.int32, (CH, CH), 1)
        tril = row_i >= col_i
        first = col_i < HD
        head_of = ((lax.broadcasted_iota(jnp.int32, (DTP, SI), 1) >> 6)
                   == lax.broadcasted_iota(jnp.int32, (DTP, SI), 0)).astype(BF)
        head_ofT = ((lax.broadcasted_iota(jnp.int32, (SI, DTP), 0) >> 6)
                    == lax.broadcasted_iota(jnp.int32, (SI, DTP), 1)).astype(BF)

        def hi_lo(v):
            hi = v.astype(BF)
            return hi, (v - hi.astype(F32)).astype(BF)

        def lane_sums(v):
            hi, lo = hi_lo(v)
            return _mm(hi, head_ofT) + _mm(lo, head_ofT)

        zz = z_ref[...]
        sz = _sigmoid(zz)
        yv = y_ref[...]
        dgn, dg = _rms_bwd(yv * (zz * sz), g_ref[...], dyn)
        gst_ref[0:1, :] += dg
        dz_ref[...] = (dgn * yv * (sz + zz * sz * (1.0 - sz))).astype(BF)
        dY = dgn * (zz * sz)

        X = xbc[:, 0:SI]
        dt_x = _per_head_lanes(dtv)
        xs = X * dt_x
        xsd = (xs * ds_x).astype(BF)
        D_x = _per_head_lanes(hp_ref[...])[2:3, :]
        dcs_col = jnp.zeros((CH, DTP), F32)
        dcs_row = jnp.zeros((CH, DTP), F32)
        GW = HPG * HD
        for g in range(NG):
            gs = slice(GW * g, GW * (g + 1))
            Bg = xbc[:, SI + NS * g:SI + NS * (g + 1)].astype(BF)
            Cg = xbc[:, SI + NG * NS + NS * g:SI + NG * NS + NS * (g + 1)].astype(BF)
            G = _mm_nt(Cg, Bg)
            Sg = st_ref[0, gs, :]
            dSe = dS[gs, :]
            dYg = dY[:, gs]
            yo_s[:, gs] = _mm_nt(Cg, Sg) * E_x[:, gs]
            dP = dYg * E_x[:, gs]
            dCg = _mm(dP, Sg)
            dS[gs, :] = _mm_tn(dP, Cg) + _per_head_rows(El_rows, g) * dSe
            t1_s[gs, :] = dSe * Sg
            Q = _mm_nt(Bg, dSe)
            q_s[:, gs] = Q
            dBg = _mm(xsd[:, gs], dSe)
            dG = jnp.zeros((CH, CH), F32)
            for jj in range(HPG // 2):
                j = g * (HPG // 2) + jj
                ps = slice(2 * HD * j, 2 * HD * (j + 1))
                xs_pair = xs[:, ps]
                dxs_pair = Q[:, 2 * HD * jj:2 * HD * (jj + 1)] * ds_x[:, ps]
                for e in range(2):
                    h = 2 * j + e
                    Lm = jnp.exp(jnp.where(tril, cs[:, h:h + 1] - csT[h:h + 1, :], -1e30))
                    M = G * Lm
                    dYm = jnp.where(first if e == 0 else ~first, dY[:, ps], 0.0).astype(BF)
                    dM = _mm_nt(dYm, xs_pair)
                    dxs_pair = dxs_pair + _mm_tn(M, dYm)
                    Wm = dM * M
                    dcs_col = dcs_col + jnp.where(col_i == h, jnp.sum(Wm, axis=1, keepdims=True), 0.0)
                    dcs_row = dcs_row + jnp.where(row_i == h, -jnp.sum(Wm, axis=0, keepdims=True), 0.0)
                    dG = dG + dM * Lm
                dxs_s[:, ps] = dxs_pair
            dxb[:, SI + NS * g:SI + NS * (g + 1)] = dBg + _mm_tn(dG, Cg)
            dxb[:, SI + NG * NS + NS * g:SI + NG * NS + NS * (g + 1)] = dCg + _mm(dG, Bg)

        dxs = dxs_s[...]
        dxb[:, 0:SI] = D_x * dY + dxs * dt_x
        dds = lane_sums(q_s[...] * xs) * dsm
        dcs_col = dcs_col + lane_sums(dY * yo_s[...]) - dds
        ddt_col = lane_sums(dxs * X)
        dD = jnp.sum(lane_sums(dY * X), axis=0, keepdims=True)
        t_hi, t_lo = hi_lo(t1_s[...])
        dcl_rows = jnp.sum(_mm(head_of, t_hi) + _mm(head_of, t_lo), axis=1, keepdims=True) * jnp.exp(csT[:, CH - 1:CH])
        dcs_row = dcs_row + jnp.where(col_i == CH - 1, dcl_rows, 0.0)
        dcs_col = dcs_col + jnp.where(row_i == CH - 1, jnp.sum(dds, axis=0, keepdims=True), 0.0)

        da = _rev_cumsum_rows(dcs_col + dcs_row.T, CH)
        ddt_col = ddt_col + da * A
        hst_ref[1:2, :] += jnp.sum(da * dtv, axis=0, keepdims=True) * A
        hst_ref[2:3, :] += dD
        draw = jnp.where(col_i < NH, ddt_col * _sigmoid(raw), 0.0)
        ddt_ref[...] = draw.astype(BF)
        hst_ref[0:1, :] += jnp.sum(draw, axis=0, keepdims=True)

        dcv = dxb[...] * (sg + cv * sg * (1.0 - sg))
        dxbc_ref[...] = _conv_bwd(dp, dcv, xr_ref[...], cw_ref, cst_ref, CH).astype(BF)

    return pl.pallas_call(
        body, name="ssd_bwd", grid=(NC,),
        in_specs=[_rows(CH, SI, NC), _rows(CH, XBC, NC), _rows(CH, XBC, NC), _rows(CH, SI, NC), _rows(CH, DTP, NC),
                  _rows(CH, SI, NC), pl.BlockSpec((1, NH * HD, NS), lambda i: (NC - 1 - i, 0, 0)),
                  _whole((8, XBC)), _whole((8, DTP)), _whole((1, SI))] + [ANY] * nq,
        out_specs=[_rows(CH, XBC, NC), _rows(CH, SI, NC), _rows(CH, DTP, NC), _whole((16, XBC)), _whole((16, DTP)),
                   _whole((8, SI))] + [ANY] * nq,
        out_shape=[S((T, XBC), BF), S((T, SI), BF), S((T, DTP), BF), S((16, XBC), F32), S((16, DTP), F32), S((8, SI), F32)]
        + [S(p.shape, p.dtype) for p in parts],
        scratch_shapes=[pltpu.VMEM((CH + 8, XBC), F32), pltpu.VMEM((NH * HD, NS), F32),
                        pltpu.VMEM((CH, XBC), F32), pltpu.VMEM((CH, SI), F32), pltpu.VMEM((CH, SI), F32),
                        pltpu.VMEM((CH, SI), F32), pltpu.VMEM((NH * HD, NS), F32),
                        pltpu.SemaphoreType.DMA((3 * nq,)), pltpu.SemaphoreType.DMA((3 * nq,))],
        compiler_params=_params(),
    )(dyn, xbcr, cv, z, dtr, y, states, cw_ssd, hp_ssd, g_ssd, *parts)


def _inproj_bwd(dlx, dlg, dz, dxbc, ddt, x, dx1, wcatT, g0, parts):
    T = x.shape[0]
    NT = T // TT
    nq = len(parts)

    def body(*refs):
        dlx_ref, dlg_ref, dz_ref, dxbc_ref, ddt_ref, x_ref, dx1_ref, w_hbm, g_ref = refs[:9]
        q_in = refs[9:9 + nq]
        dx_ref, st_ref = refs[9 + nq:11 + nq]
        q_out = refs[11 + nq:11 + 2 * nq]
        w_vm, sem, send_sems, recv_sems = refs[11 + 2 * nq:]
        _load_once([(w_hbm, w_vm)], sem)
        for phase, step in enumerate((0, NT - 1)):
            @pl.when(pl.program_id(0) == step)
            def _():
                _quad_phase(phase, q_in, q_out, send_sems, recv_sems)

        @pl.when(pl.program_id(0) == 0)
        def _():
            st_ref[...] = jnp.zeros_like(st_ref)

        dh = jnp.dot(dlx_ref[...], w_vm[0:1024, :], preferred_element_type=F32)
        dh = dh + jnp.dot(dlg_ref[...], w_vm[1024:2048, :], preferred_element_type=F32)
        dh = dh + jnp.dot(dz_ref[...], w_vm[2048:3072, :], preferred_element_type=F32)
        dh = dh + jnp.dot(dxbc_ref[...], w_vm[3072:3072 + XBC, :], preferred_element_type=F32)
        dh = dh + jnp.dot(ddt_ref[...], w_vm[3072 + XBC:PC, :], preferred_element_type=F32)
        dx, dg = _rms_bwd(x_ref[...], g_ref[...], dh)
        dx_ref[...] = dx1_ref[...] + dx
        st_ref[0:1, :] += dg

    return pl.pallas_call(
        body, name="inproj_bwd", grid=(NT,),
        in_specs=[_rows(TT, 1024), _rows(TT, 1024), _rows(TT, 1024), _rows(TT, XBC), _rows(TT, DTP), _rows(TT, D),
                  _rows(TT, D), ANY, _whole((1, D))] + [ANY] * nq,
        out_specs=[_rows(TT, D), _whole((8, D))] + [ANY] * nq,
        out_shape=[S((T, D), F32), S((8, D), F32)] + [S(p.shape, p.dtype) for p in parts],
        scratch_shapes=[pltpu.VMEM((PC, D), BF), pltpu.SemaphoreType.DMA((1,)),
                        pltpu.SemaphoreType.DMA((3 * nq,)), pltpu.SemaphoreType.DMA((3 * nq,))],
        compiler_params=_params(),
    )(dlx, dlg, dz, dxbc, ddt, x, dx1, wcatT, g0, *parts)


def _wgrad(name, a, b):
    T, M = a.shape
    N = b.shape[1]
    tk = min(T, 2048 if M <= 1024 else 1024)
    tn = N
    while M * tn * 4 > (6 << 20) and tn % 256 == 0:
        tn //= 2

    def body(a_ref, b_ref, o_ref):
        p = lax.dot_general(a_ref[...], b_ref[...], (((0,), (0,)), ((), ())), preferred_element_type=F32)

        @pl.when(pl.program_id(1) == 0)
        def _():
            o_ref[...] = p

        @pl.when(pl.program_id(1) > 0)
        def _():
            o_ref[...] += p

    return pl.pallas_call(
        body, name=name, grid=(N // tn, T // tk),
        in_specs=[pl.BlockSpec((tk, M), lambda j, k: (k, 0)), pl.BlockSpec((tk, tn), lambda j, k: (k, j))],
        out_specs=pl.BlockSpec((M, tn), lambda j, k: (0, j)), out_shape=S((M, N), F32),
        compiler_params=_params(2),
    )(a, b)


def _adamw(name, w, g, m, v):
    _, R, C = w.shape

    def body(w_ref, g_ref, m_ref, v_ref, d_ref, nm_ref, nv_ref):
        d_ref[0], nm_ref[0], nv_ref[0] = _adam_math(w_ref[0], g_ref[...], m_ref[0], v_ref[0])

    if R % 8 == 0:
        tr = _row_tile(R, C)
        n_tiles = R // tr
        blk, gblk = pl.BlockSpec((1, tr, C), lambda i: (0, i, 0)), pl.BlockSpec((tr, C), lambda i: (i, 0))
    else:
        tc = 128 * max(k for k in range(1, C // 128 + 1) if C % (128 * k) == 0 and R * 128 * k * 4 <= (5 << 18))
        n_tiles = C // tc
        blk, gblk = pl.BlockSpec((1, R, tc), lambda i: (0, 0, i)), pl.BlockSpec((R, tc), lambda i: (0, i))
    return pl.pallas_call(
        body, name=name, grid=(n_tiles,),
        in_specs=[blk, gblk, blk, blk], out_specs=[blk] * 3,
        out_shape=[S((1, R, C), F32)] * 3, compiler_params=_params(),
    )(w, g, m, v)


def _pair_exchange(name, bufs):
    n = len(bufs)

    def body(*refs):
        for phase in range(2):
            _pair_phase(phase, refs[:n], refs[n:2 * n], refs[2 * n], refs[2 * n + 1])

    return pl.pallas_call(
        body, name=name, in_specs=[ANY] * n, out_specs=[ANY] * n,
        out_shape=[S(_half_shape(b), b.dtype) for b in bufs],
        scratch_shapes=[pltpu.SemaphoreType.DMA((n,)), pltpu.SemaphoreType.DMA((n,))],
    )(*bufs)


def _quad_exchange(bufs, scatter):
    n = len(bufs)

    def body(*refs):
        ins, outs = refs[:n], refs[n:2 * n]
        send_sems, recv_sems, local_sems = refs[2 * n], refs[2 * n + 1], refs[2 * n + 2]
        x, y, c = _pos()
        me = 2 * x + y
        chips = _other_chips(x, y)
        copies, locals_ = [], []
        for k, (src, dst) in enumerate(zip(ins, outs)):
            if not scatter[k]:
                own = pltpu.make_async_copy(src, dst.at[me], local_sems.at[k])
                own.start()
                locals_.append(own)
            for j, (cx, cy) in enumerate(chips):
                piece = src.at[2 * cx + cy] if scatter[k] else src
                cp = _remote(piece, dst.at[me], send_sems.at[3 * k + j], recv_sems.at[3 * k + j], (cx, cy, c))
                cp.start()
                copies.append(cp)
        for k, (src, dst) in enumerate(zip(ins, outs)):
            for j, (cx, cy) in enumerate(chips):
                blk = dst.at[2 * cx + cy]
                _remote(blk, blk, send_sems.at[3 * k + j], recv_sems.at[3 * k + j], (cx, cy, c)).wait_recv()
        for cp in copies:
            cp.wait_send()
        for cp in locals_:
            cp.wait()

    return pl.pallas_call(
        body, name="quad_exchange", in_specs=[ANY] * n, out_specs=[ANY] * n,
        out_shape=[S((4,) + (b.shape[1:] if sc else b.shape), b.dtype) for b, sc in zip(bufs, scatter)],
        scratch_shapes=[pltpu.SemaphoreType.DMA((3 * n,)), pltpu.SemaphoreType.DMA((3 * n,)), pltpu.SemaphoreType.DMA((n,))],
    )(*bufs)


def _pair_gather(bufs):
    n = len(bufs)

    def body(*refs):
        ins, outs = refs[:n], refs[n:2 * n]
        send_sems, recv_sems = refs[2 * n], refs[2 * n + 1]
        x, y, c = _pos()
        copies = []
        for k, buf in enumerate(outs):
            mine = _half(buf, c, buf.shape[0] // 2)
            cp = _remote(mine, mine, send_sems.at[k], recv_sems.at[k], (x, y, 1 - c))
            cp.start()
            copies.append(cp)
        for k, buf in enumerate(outs):
            theirs = _half(buf, 1 - c, buf.shape[0] // 2)
            _remote(theirs, theirs, send_sems.at[k], recv_sems.at[k], (x, y, 1 - c)).wait_recv()
        for cp in copies:
            cp.wait_send()

    return pl.pallas_call(
        body, name="pair_gather", in_specs=[ANY] * n, out_specs=[ANY] * n,
        out_shape=[S(b.shape, b.dtype) for b in bufs], input_output_aliases={k: k for k in range(n)},
        scratch_shapes=[pltpu.SemaphoreType.DMA((n,)), pltpu.SemaphoreType.DMA((n,))],
    )(*bufs)


def _row_tile(rows, cols, mult=8):
    best = mult
    for t in range(mult, rows + 1, mult):
        if rows % t == 0 and t * cols * 4 <= (1 << 20):
            best = t
    return best


def _add_own_half(name, full, got, c, out_dtype, by_columns):
    hr = got.shape[-2]
    wide = got.shape[-1]
    cols = wide // 4 if by_columns else wide
    tr = _row_tile(hr, wide, 16)
    per = hr // tr

    if by_columns:
        def body(c_ref, a_ref, b_ref, o_ref):
            v = a_ref[...] + b_ref[...]
            for j in range(4):
                o_ref[j] = v[:, j * cols:(j + 1) * cols].astype(out_dtype)

        in_specs = [pl.BlockSpec((tr, wide), lambda i, c_ref: (c_ref[0] * per + i, 0)),
                    pl.BlockSpec((tr, wide), lambda i, c_ref: (i, 0))]
        out_specs = pl.BlockSpec((4, tr, cols), lambda i, c_ref: (0, i, 0))
        grid = (per,)
    else:
        def body(c_ref, a_ref, b_ref, o_ref):
            o_ref[...] = (a_ref[...] + b_ref[...]).astype(out_dtype)

        in_specs = [pl.BlockSpec((1, tr, cols), lambda s, i, c_ref: (s, c_ref[0] * per + i, 0)),
                    pl.BlockSpec((1, tr, cols), lambda s, i, c_ref: (s, i, 0))]
        out_specs = pl.BlockSpec((1, tr, cols), lambda s, i, c_ref: (s, i, 0))
        grid = (4, per)
    return pl.pallas_call(
        body, name=name,
        grid_spec=pltpu.PrefetchScalarGridSpec(num_scalar_prefetch=1, grid=grid, in_specs=in_specs, out_specs=out_specs),
        out_shape=S((4, hr, cols), out_dtype), compiler_params=_params(len(grid)),
    )(jnp.reshape(c, (1,)).astype(jnp.int32), full, got)


def _small_add_own_half(fulls, gots, c):
    n = len(fulls)

    def body(c_ref, *refs):
        for a_ref, b_ref, o_ref in zip(refs[:n], refs[n:2 * n], refs[2 * n:]):
            hr = b_ref.shape[0]
            o_ref[...] = a_ref[pl.ds(pl.multiple_of(c_ref[0] * hr, 8), hr), :] + b_ref[...]

    specs = lambda arrs: [pl.BlockSpec(a.shape, lambda i, c_ref: (0, 0)) for a in arrs]
    return pl.pallas_call(
        body, name="small_pair_add",
        grid_spec=pltpu.PrefetchScalarGridSpec(num_scalar_prefetch=1, grid=(1,), in_specs=specs(fulls) + specs(gots),
                                               out_specs=specs(gots)),
        out_shape=[S(g.shape, F32) for g in gots], compiler_params=_params(),
    )(jnp.reshape(c, (1,)).astype(jnp.int32), *fulls, *gots)


def _small_sum_slots(slots, c):
    n = len(slots)

    def body(c_ref, *refs):
        for s_ref, o_ref in zip(refs[:n], refs[n:]):
            hr = s_ref.shape[1]
            o_ref[pl.ds(pl.multiple_of(c_ref[0] * hr, 8), hr), :] = ((s_ref[0] + s_ref[1]) + s_ref[2]) + s_ref[3]

    outs = [S((2 * s.shape[1], s.shape[2]), F32) for s in slots]
    return pl.pallas_call(
        body, name="small_quad_sum",
        grid_spec=pltpu.PrefetchScalarGridSpec(
            num_scalar_prefetch=1, grid=(1,),
            in_specs=[pl.BlockSpec(s.shape, lambda i, c_ref: (0, 0, 0)) for s in slots],
            out_specs=[pl.BlockSpec(o.shape, lambda i, c_ref: (0, 0)) for o in outs]),
        out_shape=outs, compiler_params=_params(),
    )(jnp.reshape(c, (1,)).astype(jnp.int32), *slots)


def _sum_slots(name, own, slots, me, c):
    _, rows, cols = slots.shape
    tr = _row_tile(rows, cols, 16 if slots.dtype == jnp.bfloat16 else 8)
    per = rows // tr
    three = len(own.shape) == 3

    def body(p_ref, own_ref, s0, s1, s2, s3, o_ref):
        mine = own_ref[0] if three else own_ref[...]
        acc = None
        for j, s_ref in enumerate((s0, s1, s2, s3)):
            v = jnp.where(p_ref[0] == j, mine, s_ref[0]).astype(F32)
            acc = v if acc is None else acc + v
        o_ref[...] = acc

    def slot_spec(j):
        return pl.BlockSpec((1, tr, cols), lambda i, p: (jnp.where(p[0] == j, (j + 1) % 4, j), i, 0))

    own_spec = (pl.BlockSpec((1, tr, cols), lambda i, p: (p[0], i, 0)) if three
                else pl.BlockSpec((tr, cols), lambda i, p: (i, 0)))
    return pl.pallas_call(
        body, name=name,
        grid_spec=pltpu.PrefetchScalarGridSpec(
            num_scalar_prefetch=1, grid=(per,), in_specs=[own_spec] + [slot_spec(j) for j in range(4)],
            out_specs=pl.BlockSpec((tr, cols), lambda i, p: (p[1] * per + i, 0))),
        out_shape=S((2 * rows, cols), F32), compiler_params=_params(),
    )(jnp.stack([me, c]).astype(jnp.int32), own, slots, slots, slots, slots)


BIG = ("w_in", "w_out", "w_gate", "w_up", "w_down")
ROW_PARAMS = (("pre_mix_norm", 0), ("lru_conv_b", 12), ("lru_ba", 13), ("lru_bx", 14), ("lru_lambda", 15),
              ("lru_out_norm", 16), ("ssd_out_norm", 24), ("post_mix_norm", 33), ("pre_ffn_norm", 32), ("post_ffn_norm", 41))
LRU_CONV_ROWS = (8, 12)
LOSS_ROW = 40
HEAD_PARAMS = (("ssd_dt_bias", 0), ("ssd_a_log", 1), ("ssd_d", 2))
SMALL = tuple(n for n, _ in ROW_PARAMS) + ("ssd_conv_b",) + tuple(n for n, _ in HEAD_PARAMS) + (
    "lru_wa", "lru_wx", "lru_conv_w", "ssd_conv_w")


def _diag4(w):
    eye = jnp.eye(4, dtype=w.dtype).reshape(1, 4, 1, 4, 1)
    return (w.reshape(4, 4, BW, 1, BW) * eye).reshape(4, 4 * BW, 4 * BW)


def _adam_math(w, g, m, v):
    mm = ADAM_B1 * m + (1.0 - ADAM_B1) * g
    vv = ADAM_B2 * v + (1.0 - ADAM_B2) * (g * g)
    c1 = 1.0 - ADAM_B1 ** ADAM_STEP
    c2 = 1.0 - ADAM_B2 ** ADAM_STEP
    return -ADAM_LR * ((mm / c1) / (jnp.sqrt(vv / c2) + ADAM_EPS) + ADAM_WD * w), mm, vv


def _adamw_small(rows, cst, hst, dwa, dwx, glcw, gscw, w, m, v):
    def grad_of(name, refs):
        rows_ref, cst_ref, hst_ref, dwa_ref, dwx_ref, glcw_ref, gscw_ref = refs
        for n, r in ROW_PARAMS:
            if n == name:
                return rows_ref[r:r + 1, :]
        for n, r in HEAD_PARAMS:
            if n == name:
                return hst_ref[r:r + 1, 0:NH]
        return {"ssd_conv_b": lambda: cst_ref[4:5, :], "lru_wa": lambda: dwa_ref[...], "lru_wx": lambda: dwx_ref[...],
                "lru_conv_w": lambda: glcw_ref[...], "ssd_conv_w": lambda: gscw_ref[...]}[name]()

    shapes = {n: (w[n].shape[1:] if len(w[n].shape) > 2 else w[n].shape) for n in SMALL}
    flat = lambda d: [d[n].reshape(shapes[n]) for n in SMALL]
    ns = len(SMALL)

    def body(*refs):
        srcs, rest = refs[:7], refs[7:]
        w_refs, m_refs, v_refs = rest[:ns], rest[ns:2 * ns], rest[2 * ns:3 * ns]
        outs = rest[3 * ns:]
        for k, name in enumerate(SMALL):
            g = grad_of(name, srcs)
            d, mm, vv = _adam_math(w_refs[k][...], g, m_refs[k][...], v_refs[k][...])
            outs[4 * k][...] = g
            outs[4 * k + 1][...] = d
            outs[4 * k + 2][...] = mm
            outs[4 * k + 3][...] = vv

    res = pl.pallas_call(
        body, name="adamw_small",
        out_shape=[S(shapes[n], F32) for n in SMALL for _ in range(4)],
        compiler_params=pltpu.CompilerParams(vmem_limit_bytes=VMEM_LIMIT),
    )(rows, cst, hst, dwa, dwx, glcw, gscw, *flat(w), *flat(m), *flat(v))
    return {n: tuple(res[4 * k + i].reshape(w[n].shape) for i in range(4)) for k, n in enumerate(SMALL)}


def _with_own(own, got):
    chip = 2 * lax.axis_index("x") + lax.axis_index("y")
    return jnp.where((jnp.arange(4) == chip).reshape(4, 1, 1), own[None], got)


def _side_by_side(f):
    return f.transpose(1, 0, 2).reshape(f.shape[1], 4 * f.shape[2])


def _stacked(f):
    return f.reshape(4 * f.shape[1], f.shape[2])


def _conv_terms(lru_conv_w, ssd_conv_w):
    conv = jnp.concatenate([lru_conv_w.reshape(-1), ssd_conv_w.reshape(-1)]).astype(F32)
    hi = conv.astype(jnp.bfloat16)
    mid = (conv - hi.astype(F32)).astype(jnp.bfloat16)
    lo = (conv - hi.astype(F32) - mid.astype(F32)).astype(jnp.bfloat16)
    terms = jnp.concatenate([hi, mid, lo])
    rows = -(-terms.shape[0] // (128 * 32)) * 32
    return jnp.pad(terms, (0, rows * 128 - terms.shape[0])).reshape(rows, 128)


def _full_conv_taps(own, got, n_lru, n_ssd):
    n_terms = 3 * (n_lru + n_ssd)
    t3 = _with_own(own, got).reshape(4, -1)[:, :n_terms].reshape(4, 3, -1).astype(F32)
    conv_f = (t3[:, 0] + t3[:, 1]) + t3[:, 2]
    lcw = conv_f[:, :n_lru].reshape(4, CONV_K, -1).transpose(1, 0, 2).reshape(CONV_K, LW)
    scw = conv_f[:, n_lru:].reshape(4, CONV_K, -1).transpose(1, 0, 2).reshape(CONV_K, XBC)
    return lcw, scw


def _step(x, tgt, w_in, lru_conv_w, ssd_conv_w, sp, late):
    c = lax.axis_index("c")
    me = 2 * lax.axis_index("x") + lax.axis_index("y")
    mm = lambda w: w.astype(BF)
    row = lambda v: v.reshape(1, -1).astype(F32)
    g0 = row(sp["pre_mix_norm"])
    first = [w_in.astype(WIRE), _conv_terms(lru_conv_w, ssd_conv_w)]
    h0, *got_first = _prenorm(x, g0, first)
    win_f = _side_by_side(_with_own(first[0], got_first[0]))
    lcw, scw = _full_conv_taps(first[1], got_first[1], lru_conv_w.size, ssd_conv_w.size)
    wcat = jnp.concatenate([mm(win_f), jnp.zeros((D, PC - IN_COLS), BF)], axis=1)
    p_lru = jnp.concatenate([lcw, row(sp["lru_conv_b"]), row(sp["lru_ba"]), row(sp["lru_bx"]), row(sp["lru_lambda"]),
                             row(sp["lru_out_norm"]), jnp.zeros((7, LW), F32)], axis=0)
    wa4, wx4 = mm(_diag4(sp["lru_wa"][0])), mm(_diag4(sp["lru_wx"][0]))
    wa4T, wx4T = wa4.transpose(0, 2, 1), wx4.transpose(0, 2, 1)
    cw_ssd = jnp.concatenate([scw, row(sp["ssd_conv_b"]), jnp.zeros((3, XBC), F32)], axis=0)
    padh = lambda v: jnp.pad(row(v), ((0, 0), (0, DTP - NH)))
    hp_ssd = jnp.concatenate([padh(sp["ssd_dt_bias"]), padh(sp["ssd_a_log"]), padh(sp["ssd_d"]), jnp.zeros((5, DTP), F32)], axis=0)
    g_ssd = row(sp["ssd_out_norm"])
    g_pm, g_pf, g_pff = row(sp["post_mix_norm"]), row(sp["pre_ffn_norm"]), row(sp["post_ffn_norm"])

    h, ylru, lxc, lxr, lg, *got_a = _lru_fwd(h0, wcat, p_lru, wa4, wx4, [late[0], late[3]])
    y, yssd, states, cv, z, xbcr, dtr, *got_b = _ssd_fwd(h0, wcat, cw_ssd, hp_ssd, g_ssd, [late[1], late[2]])
    wout, wd = mm(_stacked(_with_own(late[0], got_a[0]))), mm(_stacked(_with_own(late[3], got_a[1])))
    wg, wu = mm(_side_by_side(_with_own(late[1], got_b[0]))), mm(_side_by_side(_with_own(late[2], got_b[1])))
    mix, x1, h2 = _outproj(ylru, yssd, x, wout, g_pm, g_pf)
    gate, up, act, df, dx2, st_ffn = _ffn_fwd(h2, x1, tgt, wg, wu, wd, g_pff)
    dgate, dup, dh2 = _ffn_bwd(df, gate, up, wd.T, wg.T, wu.T)
    dx1, dmix, dyl, dys, st_mix = _mix_bwd(dh2, x1, dx2, mix, wout.T, g_pf, g_pm)

    dwg = _wgrad("wgrad_gate", h2, dgate)
    dwu = _wgrad("wgrad_up", h2, dup)
    dwd = _wgrad("wgrad_down", act, df)
    dwo = jnp.concatenate([_wgrad("wgrad_out_lru", ylru, dmix), _wgrad("wgrad_out_ssd", yssd, dmix)], axis=0)
    early = [dwo.reshape(4, (LW + SI) // 4, D), dwg, dwu, dwd.reshape(4, DFF // 4, D)]
    dlx, dlg, st_lru, dwa, dwx, *got_early = _lru_bwd(dyl, lxr, lxc, lg, h, p_lru, wa4, wx4, wa4T, wx4T, early)
    part_early = [_add_own_half("pair_add_early%d" % k, b, r, c, WIRE, bc)
                  for k, (b, r, bc) in enumerate(zip(early, got_early, [False, True, True, False]))]
    dxbc, dz, ddt, cst, hst, gst, *slots_early = _ssd_bwd(dys, xbcr, cv, z, dtr, y, states, cw_ssd, hp_ssd, g_ssd,
                                                          part_early)
    red_early = [_sum_slots("quad_sum_early%d" % k, p, s, me, c) for k, (p, s) in enumerate(zip(part_early, slots_early))]

    pin = [_wgrad("wgrad_in_%d" % k, h0, b) for k, b in enumerate((dlx, dlg, dz, dxbc, ddt))]
    dwin = jnp.concatenate(pin[:4] + [pin[4][:, :NH]], axis=1)
    (got_win,) = _pair_exchange("pair_exchange_w_in", [dwin])
    part_win = _add_own_half("pair_add_w_in", dwin, got_win, c, WIRE, True)
    gx, st_in, slots_win = _inproj_bwd(dlx, dlg, dz, dxbc, ddt, x, dx1, wcat.T, g0, [part_win])
    red_win = _sum_slots("quad_sum_w_in", part_win, slots_win, me, c)

    rows = jnp.concatenate([st_in, st_lru, gst, st_mix, st_ffn], axis=0)
    small = [rows, cst, hst, dwa.reshape(NBLK * BW, BW), dwx.reshape(NBLK * BW, BW)]
    part_small = list(_small_add_own_half(small, list(_pair_exchange("pair_exchange_small", small)), c))
    red_small = list(_small_sum_slots(list(_quad_exchange(part_small, [False] * len(small))), c))
    out = list(_pair_gather([red_win] + red_early + red_small))
    big = dict(zip(("w_in", "w_out", "w_gate", "w_up", "w_down"), out[:5]))
    return gx, big, out[5:]


def kernel(x, pre_mix_norm, w_in, lru_conv_w, lru_conv_b, lru_wa, lru_ba, lru_wx, lru_bx, lru_lambda, lru_out_norm, ssd_conv_w, ssd_conv_b, ssd_dt_bias, ssd_a_log, ssd_d, ssd_out_norm, w_out, post_mix_norm, pre_ffn_norm, w_gate, w_up, w_down, post_ffn_norm, loss_target, m_pre_mix_norm, m_w_in, m_lru_conv_w, m_lru_conv_b, m_lru_wa, m_lru_ba, m_lru_wx, m_lru_bx, m_lru_lambda, m_lru_out_norm, m_ssd_conv_w, m_ssd_conv_b, m_ssd_dt_bias, m_ssd_a_log, m_ssd_d, m_ssd_out_norm, m_w_out, m_post_mix_norm, m_pre_ffn_norm, m_w_gate, m_w_up, m_w_down, m_post_ffn_norm, v_pre_mix_norm, v_w_in, v_lru_conv_w, v_lru_conv_b, v_lru_wa, v_lru_ba, v_lru_wx, v_lru_bx, v_lru_lambda, v_lru_out_norm, v_ssd_conv_w, v_ssd_conv_b, v_ssd_dt_bias, v_ssd_a_log, v_ssd_d, v_ssd_out_norm, v_w_out, v_post_mix_norm, v_pre_ffn_norm, v_w_gate, v_w_up, v_w_down, v_post_ffn_norm):
    args = dict(locals())
    names = list(SMALL) + list(BIG)
    w = {n: args[n] for n in names}
    m = {n: args["m_" + n] for n in names}
    v = {n: args["v_" + n] for n in names}
    chip = 2 * lax.axis_index("x") + lax.axis_index("y")

    late = [a[0].astype(WIRE) for a in (w_out, w_gate, w_up, w_down)]
    gx, red, (rows, cst, hst, dwa, dwx) = _step(x[0], loss_target[0], w_in[0], lru_conv_w[0], ssd_conv_w[0],
                                                {n: w[n] for n in SMALL}, late)
    loss = jnp.sum(rows[LOSS_ROW])

    grads, delta, new_m, new_v = {}, {}, {}, {}
    for n in BIG:
        g = red[n]
        if n in ("w_in", "w_gate", "w_up"):
            t = lambda a: jnp.swapaxes(a, 1, 2)
            gt = g.T
            out = _adamw("adamw_" + n, t(w[n]), gt, t(m[n]), t(v[n]))
            delta[n], new_m[n], new_v[n] = (t(o) for o in out)
            grads[n] = t(gt[None])
        else:
            delta[n], new_m[n], new_v[n] = _adamw("adamw_" + n, w[n], g, m[n], v[n])
            grads[n] = g[None]

    lc, sc = lru_conv_w.shape[-1], ssd_conv_w.shape[-1]
    glcw = lax.dynamic_slice_in_dim(rows[LRU_CONV_ROWS[0]:LRU_CONV_ROWS[1]], chip * lc, lc, axis=1)
    gscw = lax.dynamic_slice_in_dim(cst[0:CONV_K], chip * sc, sc, axis=1)
    res = _adamw_small(rows, cst, hst, dwa.reshape(NBLK, BW, BW), dwx.reshape(NBLK, BW, BW), glcw, gscw,
                       {n: w[n] for n in SMALL}, {n: m[n] for n in SMALL}, {n: v[n] for n in SMALL})
    for n in SMALL:
        grads[n], delta[n], new_m[n], new_v[n] = res[n]

    order = ["pre_mix_norm", "w_in", "lru_conv_w", "lru_conv_b", "lru_wa", "lru_ba", "lru_wx", "lru_bx", "lru_lambda",
             "lru_out_norm", "ssd_conv_w", "ssd_conv_b", "ssd_dt_bias", "ssd_a_log", "ssd_d", "ssd_out_norm", "w_out",
             "post_mix_norm", "pre_ffn_norm", "w_gate", "w_up", "w_down", "post_ffn_norm"]
    return (loss, gx[None], *[grads[n] for n in order], *[delta[n] for n in order],
            *[new_m[n] for n in order], *[new_v[n] for n in order])
```

```python
import functools

import jax
import jax.numpy as jnp
from jax import lax
from jax.experimental import pallas as pl
from jax.experimental.pallas import tpu as pltpu

F32 = jnp.float32
BF = jnp.bfloat16

D = 1024
LW = 1024
NBLK = 16
BW = 64
SI = 1024
NH = 16
HD = 64
NG = 2
HPG = NH // NG
NS = 128
CH = 128
XBC = SI + 2 * NG * NS
DTP = 128
PC = 3 * 1024 + XBC + DTP
DFF = 2816
IN_COLS = 4624
EPS = 1e-6
LRU_C = 8.0
CONV_K = 4
TT = 256
VMEM_LIMIT = 56 * 1024 * 1024

ADAM_LR, ADAM_B1, ADAM_B2, ADAM_EPS, ADAM_WD, ADAM_STEP = 0.001, 0.9, 0.999, 1e-08, 0.01, 10

MESH = pl.DeviceIdType.MESH


def _mm(a, b):
    return jnp.dot(a.astype(BF), b.astype(BF), preferred_element_type=F32)


def _mm_nt(a, b):
    return lax.dot_general(a.astype(BF), b.astype(BF), (((1,), (1,)), ((), ())), preferred_element_type=F32)


def _mm_tn(a, b):
    return lax.dot_general(a.astype(BF), b.astype(BF), (((0,), (0,)), ((), ())), preferred_element_type=F32)


def _sigmoid(x):
    return 0.5 * jnp.tanh(0.5 * x) + 0.5


def _softplus(x):
    return jnp.maximum(x, 0.0) + jnp.log1p(jnp.exp(-jnp.abs(x)))


_GELU_C = 0.7978845608028654
_GELU_K = 0.044715


def _gelu(x):
    t = jnp.tanh(_GELU_C * (x + _GELU_K * x * x * x))
    return 0.5 * x * (1.0 + t)


def _gelu_grad(x):
    t = jnp.tanh(_GELU_C * (x + _GELU_K * x * x * x))
    return 0.5 * (1.0 + t) + 0.5 * x * (1.0 - t * t) * _GELU_C * (1.0 + 3.0 * _GELU_K * x * x)


def _rms_fwd(x, g):
    r = lax.rsqrt(jnp.mean(x * x, axis=-1, keepdims=True) + EPS)
    return x * r * g


def _rms_bwd(x, g, dy):
    r = lax.rsqrt(jnp.mean(x * x, axis=-1, keepdims=True) + EPS)
    xh = x * r
    dxh = dy * g
    dg = jnp.sum(dy * xh, axis=0, keepdims=True)
    dx = r * (dxh - xh * jnp.mean(dxh * xh, axis=-1, keepdims=True))
    return dx, dg


def _sum_all(x):
    return jnp.sum(jnp.sum(x, axis=1, keepdims=True), axis=0, keepdims=True)


def _cumsum_rows(x, n):
    row = lax.broadcasted_iota(jnp.int32, x.shape, 0)
    k = 1
    while k < n:
        x = x + jnp.where(row >= k, pltpu.roll(x, k, 0), 0.0)
        k *= 2
    return x


def _rev_cumsum_rows(x, n):
    row = lax.broadcasted_iota(jnp.int32, x.shape, 0)
    k = 1
    while k < n:
        x = x + jnp.where(row < n - k, pltpu.roll(x, n - k, 0), 0.0)
        k *= 2
    return x


def _load_once(pairs, sem):
    @pl.when(pl.program_id(0) == 0)
    def _():
        for k, (src, dst) in enumerate(pairs):
            pltpu.make_async_copy(src, dst, sem.at[k]).start()
        for k, (src, dst) in enumerate(pairs):
            pltpu.make_async_copy(src, dst, sem.at[k]).wait()


def _params(n_axes=1):
    return pltpu.CompilerParams(dimension_semantics=("arbitrary",) * n_axes, vmem_limit_bytes=VMEM_LIMIT)


def _rows(n, width, rev_of=None):
    if rev_of is None:
        return pl.BlockSpec((n, width), lambda i: (i, 0))
    return pl.BlockSpec((n, width), lambda i: (rev_of - 1 - i, 0))


def _whole(shape):
    nd = len(shape)
    return pl.BlockSpec(shape, lambda i: (0,) * nd)


ANY = pl.BlockSpec(memory_space=pl.ANY)
S = jax.ShapeDtypeStruct
WIRE = jnp.bfloat16


def _pos():
    return lax.axis_index("x"), lax.axis_index("y"), lax.axis_index("c")


def _other_chips(x, y):
    return [(1 - x, y), (x, 1 - y), (1 - x, 1 - y)]


def _remote(src, dst, send_sem, recv_sem, to):
    return pltpu.make_async_remote_copy(src_ref=src, dst_ref=dst, send_sem=send_sem, recv_sem=recv_sem,
                                        device_id=to, device_id_type=MESH)


def _gather_phase(phase, ins, outs, send_sems, recv_sems):
    x, y, c = _pos()
    me = 2 * x + y
    chips = _other_chips(x, y)
    for i, (src, dst) in enumerate(zip(ins, outs)):
        hr = src.shape[0] // 2
        my_half = pl.ds(pl.multiple_of(c * hr, 16), hr)
        sib_half = pl.ds(pl.multiple_of((1 - c) * hr, 16), hr)
        for k, (cx, cy) in enumerate(chips):
            s1, r1 = send_sems.at[6 * i + k], recv_sems.at[6 * i + k]
            s2, r2 = send_sems.at[6 * i + 3 + k], recv_sems.at[6 * i + 3 + k]
            first = lambda: _remote(src.at[my_half, :], dst.at[me, my_half, :], s1, r1, (cx, cy, c))
            landed = dst.at[2 * cx + cy, my_half, :]
            passed = lambda: _remote(landed, landed, s2, r2, (x, y, 1 - c))
            if phase == 0:
                first().start()
            elif phase == 1:
                _remote(landed, landed, s1, r1, (cx, cy, c)).wait_recv()
                passed().start()
            else:
                theirs = dst.at[2 * cx + cy, sib_half, :]
                _remote(theirs, theirs, s2, r2, (x, y, 1 - c)).wait_recv()
                first().wait_send()
                passed().wait_send()


def _half(ref, c, hr):
    sl = pl.ds(pl.multiple_of(c * hr, 8), hr)
    return ref.at[:, sl, :] if len(ref.shape) == 3 else ref.at[sl, :]


def _half_shape(b):
    return b.shape[:-2] + (b.shape[-2] // 2, b.shape[-1])


def _pair_phase(phase, ins, outs, send_sems, recv_sems):
    x, y, c = _pos()
    for k, (src, dst) in enumerate(zip(ins, outs)):
        cp = _remote(_half(src, 1 - c, src.shape[-2] // 2), dst, send_sems.at[k], recv_sems.at[k], (x, y, 1 - c))
        if phase == 0:
            cp.start()
        else:
            cp.wait()


def _quad_phase(phase, ins, outs, send_sems, recv_sems):
    x, y, c = _pos()
    me = 2 * x + y
    for i, (src, dst) in enumerate(zip(ins, outs)):
        for k, (cx, cy) in enumerate(_other_chips(x, y)):
            cp = _remote(src.at[2 * cx + cy], dst.at[me], send_sems.at[3 * i + k], recv_sems.at[3 * i + k], (cx, cy, c))
            if phase == 0:
                cp.start()
            else:
                got = dst.at[2 * cx + cy]
                _remote(got, got, send_sems.at[3 * i + k], recv_sems.at[3 * i + k], (cx, cy, c)).wait_recv()
                cp.wait_send()


def _prenorm(x, g0, shards):
    T = x.shape[0]
    tt = 2 * TT
    nt = T // tt
    ng = len(shards)

    def body(*refs):
        x_ref, g_ref = refs[:2]
        sh_in = refs[2:2 + ng]
        h0_ref = refs[2 + ng]
        sh_out = refs[3 + ng:3 + 2 * ng]
        send_sems, recv_sems = refs[3 + 2 * ng:]
        for phase, step in enumerate((0, nt // 2, nt - 1)):
            @pl.when(pl.program_id(0) == step)
            def _():
                _gather_phase(phase, sh_in, sh_out, send_sems, recv_sems)

        h0_ref[...] = _rms_fwd(x_ref[...], g_ref[...]).astype(BF)

    return pl.pallas_call(
        body, name="prenorm", grid=(nt,),
        in_specs=[_rows(tt, D), _whole((1, D))] + [ANY] * ng, out_specs=[_rows(tt, D)] + [ANY] * ng,
        out_shape=[S((T, D), BF)] + [S((4,) + s.shape, s.dtype) for s in shards],
        scratch_shapes=[pltpu.SemaphoreType.DMA((6 * ng,)), pltpu.SemaphoreType.DMA((6 * ng,))],
        compiler_params=_params(),
    )(x, g0, *shards)


def _blockdiag_mm(v, w4_ref):
    return jnp.concatenate([_mm(v[:, 256 * j:256 * (j + 1)], w4_ref[j]) for j in range(4)], axis=1)


def _lru_gates(lx, p_ref, wa_ref, wx_ref):
    r = _sigmoid(_blockdiag_mm(lx, wa_ref) + p_ref[5:6, :])
    i = _sigmoid(_blockdiag_mm(lx, wx_ref) + p_ref[6:7, :])
    sp = _softplus(-p_ref[7:8, :])
    la = -LRU_C * r * sp
    a = jnp.exp(la)
    th = jnp.tanh(la)
    mult = jnp.sqrt(-2.0 * th / (1.0 - th))
    return r, i, sp, a, mult


def _conv_from(xp_ref, p_ref, n):
    acc = p_ref[4:5, :] + p_ref[0:1, :] * xp_ref[pl.ds(8 - CONV_K + 1, n), :]
    for k in range(1, CONV_K):
        acc = acc + p_ref[k:k + 1, :] * xp_ref[pl.ds(8 - CONV_K + 1 + k, n), :]
    return acc


def _conv_bwd(dp_ref, dconv, x, p_ref, st_ref, n):
    dp_ref[0:n, :] = dconv
    acc = None
    for k in range(CONV_K):
        g = dp_ref[pl.ds(CONV_K - 1 - k, n), :]
        acc = p_ref[k:k + 1, :] * g if acc is None else acc + p_ref[k:k + 1, :] * g
        st_ref[k:k + 1, :] += jnp.sum(g * x, axis=0, keepdims=True)
    st_ref[4:5, :] += jnp.sum(dconv, axis=0, keepdims=True)
    dp_ref[n:n + 8, :] = dp_ref[0:8, :]
    return acc


def _lru_fwd(h0, wcat, p_lru, wa4, wx4, shards):
    T = h0.shape[0]
    NT = T // TT
    ng = len(shards)

    def body(*refs):
        h0_ref, w_hbm, p_ref, wa_ref, wx_ref = refs[:5]
        sh_in = refs[5:5 + ng]
        h_ref, y_ref, lxc_ref, lxr_ref, lg_ref = refs[5 + ng:10 + ng]
        sh_out = refs[10 + ng:10 + 2 * ng]
        xp, a_s, u_s, hc, w_vm, wsem, send_sems, recv_sems = refs[10 + 2 * ng:]
        _load_once([(w_hbm.at[:, 0:2 * LW], w_vm)], wsem)
        for phase, step in enumerate((0, NT // 2, NT - 1)):
            @pl.when(pl.program_id(0) == step)
            def _():
                _gather_phase(phase, sh_in, sh_out, send_sems, recv_sems)

        @pl.when(pl.program_id(0) == 0)
        def _():
            xp[0:8, :] = jnp.zeros((8, LW), F32)
            hc[...] = jnp.zeros_like(hc)

        hv = h0_ref[...]
        lxr = jnp.dot(hv, w_vm[:, 0:LW], preferred_element_type=F32)
        lxr_ref[...] = lxr
        lg_ref[...] = jnp.dot(hv, w_vm[:, LW:2 * LW], preferred_element_type=F32)
        xp[8:8 + TT, :] = lxr
        lx = _conv_from(xp, p_ref, TT)
        lxc_ref[...] = lx
        xp[0:8, :] = xp[TT:TT + 8, :]
        r, i, sp, a, mult = _lru_gates(lx, p_ref, wa_ref, wx_ref)
        a_s[...] = a
        u_s[...] = mult * (i * lx)

        def step(t, h):
            h = a_s[pl.ds(t, 1), :] * h + u_s[pl.ds(t, 1), :]
            h_ref[pl.ds(t, 1), :] = h
            return h

        hc[0:1, :] = lax.fori_loop(0, TT, step, hc[0:1, :], unroll=8)
        gated = h_ref[...] * _gelu(lg_ref[...])
        y_ref[...] = _rms_fwd(gated, p_ref[8:9, :]).astype(BF)

    return pl.pallas_call(
        body, name="lru_fwd", grid=(NT,),
        in_specs=[_rows(TT, D), ANY, _whole((16, LW)), _whole((4, 256, 256)), _whole((4, 256, 256))] + [ANY] * ng,
        out_specs=[_rows(TT, LW), _rows(TT, LW), _rows(TT, LW), _rows(TT, LW), _rows(TT, LW)] + [ANY] * ng,
        out_shape=[S((T, LW), F32), S((T, LW), BF), S((T, LW), F32), S((T, LW), F32), S((T, LW), F32)]
        + [S((4,) + s.shape, s.dtype) for s in shards],
        scratch_shapes=[pltpu.VMEM((TT + 8, LW), F32), pltpu.VMEM((TT, LW), F32), pltpu.VMEM((TT, LW), F32),
                        pltpu.VMEM((8, LW), F32), pltpu.VMEM((D, 2 * LW), BF), pltpu.SemaphoreType.DMA((1,)),
                        pltpu.SemaphoreType.DMA((6 * ng,)), pltpu.SemaphoreType.DMA((6 * ng,))],
        compiler_params=_params(),
    )(h0, wcat, p_lru, wa4, wx4, *shards)


def _ssd_prep(cv, dt_ref, hp_ref):
    sg = _sigmoid(cv)
    xbc = cv * sg
    lane = lax.broadcasted_iota(jnp.int32, (CH, DTP), 1)
    raw = dt_ref[...] + hp_ref[0:1, :]
    dtv = jnp.where(lane < NH, _softplus(raw), 0.0)
    A = jnp.where(lane[0:1, :] < NH, -jnp.exp(hp_ref[1:2, :]), 0.0)
    cs = _cumsum_rows(dtv * A, CH)
    return sg, xbc, raw, dtv, A, cs


def _per_head_lanes(v):
    r = v.shape[0]
    first = lax.broadcasted_iota(jnp.int32, (r, 2 * HD), 1) < HD
    pairs = [jnp.where(first, jnp.broadcast_to(v[:, 2 * j:2 * j + 1], (r, 2 * HD)),
                       jnp.broadcast_to(v[:, 2 * j + 1:2 * j + 2], (r, 2 * HD))) for j in range(NH // 2)]
    return jnp.concatenate(pairs, axis=1)


def _per_head_rows(col, g):
    return jnp.concatenate([jnp.broadcast_to(col[g * HPG + k:g * HPG + k + 1, :], (HD, NS)) for k in range(HPG)], axis=0)


def _ssd_decays(cs):
    csT = cs.T
    cl = cs[CH - 1:CH, :]
    E_x = _per_head_lanes(jnp.exp(cs))
    dsm = jnp.exp(cl - cs)
    ds_x = _per_head_lanes(dsm)
    El_rows = jnp.broadcast_to(jnp.exp(csT[0:NH, CH - 1:CH]), (NH, NS))
    return csT, dsm, E_x, ds_x, El_rows


def _ssd_fwd(h0, wcat, cw_ssd, hp_ssd, g_ssd, shards):
    T = h0.shape[0]
    NC = T // CH
    ng = len(shards)
    c0 = 2 * LW

    def body(*refs):
        h0_ref, w_hbm, cw_ref, hp_ref, g_ref = refs[:5]
        sh_in = refs[5:5 + ng]
        y_ref, yn_ref, st_ref, cv_ref, z_ref, xr_ref, dt_ref = refs[5 + ng:12 + ng]
        sh_out = refs[12 + ng:12 + 2 * ng]
        xp, st, w_vm, wsem, send_sems, recv_sems = refs[12 + 2 * ng:]
        _load_once([(w_hbm.at[:, c0:PC], w_vm)], wsem)
        for phase, step in enumerate((0, NC // 2, NC - 1)):
            @pl.when(pl.program_id(0) == step)
            def _():
                _gather_phase(phase, sh_in, sh_out, send_sems, recv_sems)

        @pl.when(pl.program_id(0) == 0)
        def _():
            xp[0:8, :] = jnp.zeros((8, XBC), F32)
            st[...] = jnp.zeros_like(st)

        hv = h0_ref[...]
        z_ref[...] = jnp.dot(hv, w_vm[:, 0:SI], preferred_element_type=F32)
        xraw = jnp.dot(hv, w_vm[:, SI:SI + XBC], preferred_element_type=F32)
        xr_ref[...] = xraw
        dt_ref[...] = jnp.dot(hv, w_vm[:, SI + XBC:SI + XBC + DTP], preferred_element_type=F32)
        xp[8:8 + CH, :] = xraw
        cv = _conv_from(xp, cw_ref, CH)
        cv_ref[...] = cv
        sg, xbc, raw, dtv, A, cs = _ssd_prep(cv, dt_ref, hp_ref)
        xp[0:8, :] = xp[CH:CH + 8, :]
        st_ref[0] = st[...]
        csT, dsm, E_x, ds_x, El_rows = _ssd_decays(cs)
        X = xbc[:, 0:SI]
        xs = X * _per_head_lanes(dtv)
        xsd = (xs * ds_x).astype(BF)
        DX = _per_head_lanes(hp_ref[...])[2:3, :] * X
        tril = lax.broadcasted_iota(jnp.int32, (CH, CH), 0) >= lax.broadcasted_iota(jnp.int32, (CH, CH), 1)
        first = lax.broadcasted_iota(jnp.int32, (CH, 2 * HD), 1) < HD
        GW = HPG * HD
        for g in range(NG):
            Bg = xbc[:, SI + NS * g:SI + NS * (g + 1)].astype(BF)
            Cg = xbc[:, SI + NG * NS + NS * g:SI + NG * NS + NS * (g + 1)].astype(BF)
            G = _mm_nt(Cg, Bg)
            Sg = st[GW * g:GW * (g + 1), :]
            Yo = _mm_nt(Cg, Sg) * E_x[:, GW * g:GW * (g + 1)]
            st[GW * g:GW * (g + 1), :] = _per_head_rows(El_rows, g) * Sg + _mm_tn(xsd[:, GW * g:GW * (g + 1)], Bg)
            for jj in range(HPG // 2):
                j = g * (HPG // 2) + jj
                ps = slice(2 * HD * j, 2 * HD * (j + 1))
                xs_pair = xs[:, ps]
                acc = Yo[:, 2 * HD * jj:2 * HD * (jj + 1)] + DX[:, ps]
                for e in range(2):
                    h = 2 * j + e
                    Lm = jnp.exp(jnp.where(tril, cs[:, h:h + 1] - csT[h:h + 1, :], -1e30))
                    acc = acc + _mm(G * Lm, jnp.where(first if e == 0 else ~first, xs_pair, 0.0))
                y_ref[:, ps] = acc
        zz = z_ref[...]
        gated = y_ref[...] * (zz * _sigmoid(zz))
        yn_ref[...] = _rms_fwd(gated, g_ref[...]).astype(BF)

    return pl.pallas_call(
        body, name="ssd_fwd", grid=(NC,),
        in_specs=[_rows(CH, D), ANY, _whole((8, XBC)), _whole((8, DTP)), _whole((1, SI))] + [ANY] * ng,
        out_specs=[_rows(CH, SI), _rows(CH, SI), pl.BlockSpec((1, NH * HD, NS), lambda i: (i, 0, 0)), _rows(CH, XBC),
                   _rows(CH, SI), _rows(CH, XBC), _rows(CH, DTP)] + [ANY] * ng,
        out_shape=[S((T, SI), F32), S((T, SI), BF), S((NC, NH * HD, NS), F32), S((T, XBC), F32),
                   S((T, SI), F32), S((T, XBC), F32), S((T, DTP), F32)] + [S((4,) + s.shape, s.dtype) for s in shards],
        scratch_shapes=[pltpu.VMEM((CH + 8, XBC), F32), pltpu.VMEM((NH * HD, NS), F32),
                        pltpu.VMEM((D, PC - c0), BF), pltpu.SemaphoreType.DMA((1,)),
                        pltpu.SemaphoreType.DMA((6 * ng,)), pltpu.SemaphoreType.DMA((6 * ng,))],
        compiler_params=_params(),
    )(h0, wcat, cw_ssd, hp_ssd, g_ssd, *shards)


def _outproj(ylru, yssd, x, wout, g_pm, g_pf):
    T = x.shape[0]

    def body(yl_ref, ys_ref, x_ref, w_hbm, gpm_ref, gpf_ref, mix_ref, x1_ref, h2_ref, w_vm, sem):
        _load_once([(w_hbm, w_vm)], sem)
        mix = (jnp.dot(yl_ref[...], w_vm[0:LW, :], preferred_element_type=F32)
               + jnp.dot(ys_ref[...], w_vm[LW:LW + SI, :], preferred_element_type=F32))
        mix_ref[...] = mix
        x1 = x_ref[...] + _rms_fwd(mix, gpm_ref[...])
        x1_ref[...] = x1
        h2_ref[...] = _rms_fwd(x1, gpf_ref[...]).astype(BF)

    return pl.pallas_call(
        body, name="outproj", grid=(T // TT,),
        in_specs=[_rows(TT, LW), _rows(TT, SI), _rows(TT, D), ANY, _whole((1, D)), _whole((1, D))],
        out_specs=[_rows(TT, D), _rows(TT, D), _rows(TT, D)],
        out_shape=[S((T, D), F32), S((T, D), F32), S((T, D), BF)],
        scratch_shapes=[pltpu.VMEM((LW + SI, D), BF), pltpu.SemaphoreType.DMA((1,))],
        compiler_params=_params(),
    )(ylru, yssd, x, wout, g_pm, g_pf)


def _ffn_fwd(h2, x1, tgt, wg, wu, wd, g_pff):
    T = x1.shape[0]

    def body(h2_ref, x1_ref, t_ref, wg_hbm, wu_hbm, wd_hbm, g_ref,
             gate_ref, up_ref, act_ref, df_ref, dx2_ref, st_ref, wg_vm, wu_vm, wd_vm, sem):
        _load_once([(wg_hbm, wg_vm), (wu_hbm, wu_vm), (wd_hbm, wd_vm)], sem)

        @pl.when(pl.program_id(0) == 0)
        def _():
            st_ref[...] = jnp.zeros_like(st_ref)

        h2 = h2_ref[...]
        gate = jnp.dot(h2, wg_vm[...], preferred_element_type=F32)
        up = jnp.dot(h2, wu_vm[...], preferred_element_type=F32)
        gate_ref[...] = gate
        up_ref[...] = up
        act = (gate * _sigmoid(gate) * up).astype(BF)
        act_ref[...] = act
        f = jnp.dot(act, wd_vm[...], preferred_element_type=F32)
        g = g_ref[...]
        x2 = x1_ref[...] + _rms_fwd(f, g)
        err = x2 - t_ref[...]
        st_ref[0:1, :] += 0.5 * jnp.sum(err * err, axis=0, keepdims=True) * (1.0 / D)
        dx2 = err * (1.0 / D)
        dx2_ref[...] = dx2
        df, dg = _rms_bwd(f, g, dx2)
        df_ref[...] = df.astype(BF)
        st_ref[1:2, :] += dg

    return pl.pallas_call(
        body, name="ffn_fwd", grid=(T // TT,),
        in_specs=[_rows(TT, D), _rows(TT, D), _rows(TT, D), ANY, ANY, ANY, _whole((1, D))],
        out_specs=[_rows(TT, DFF), _rows(TT, DFF), _rows(TT, DFF), _rows(TT, D), _rows(TT, D), _whole((8, D))],
        out_shape=[S((T, DFF), F32), S((T, DFF), F32), S((T, DFF), BF), S((T, D), BF), S((T, D), F32), S((8, D), F32)],
        scratch_shapes=[pltpu.VMEM((D, DFF), BF), pltpu.VMEM((D, DFF), BF), pltpu.VMEM((DFF, D), BF),
                        pltpu.SemaphoreType.DMA((3,))],
        compiler_params=_params(),
    )(h2, x1, tgt, wg, wu, wd, g_pff)


def _ffn_bwd(df, gate, up, wdT, wgT, wuT):
    T = df.shape[0]

    def body(df_ref, gate_ref, up_ref, wd_hbm, wg_hbm, wu_hbm, dgate_ref, dup_ref, dh2_ref, wd_vm, wg_vm, wu_vm, sem):
        _load_once([(wd_hbm, wd_vm), (wg_hbm, wg_vm), (wu_hbm, wu_vm)], sem)
        dact = jnp.dot(df_ref[...], wd_vm[...], preferred_element_type=F32)
        gate = gate_ref[...]
        s = _sigmoid(gate)
        dup = (dact * (gate * s)).astype(BF)
        dgate = (dact * up_ref[...] * (s + gate * s * (1.0 - s))).astype(BF)
        dup_ref[...] = dup
        dgate_ref[...] = dgate
        dh2_ref[...] = (jnp.dot(dgate, wg_vm[...], preferred_element_type=F32)
                        + jnp.dot(dup, wu_vm[...], preferred_element_type=F32))

    return pl.pallas_call(
        body, name="ffn_bwd", grid=(T // TT,),
        in_specs=[_rows(TT, D), _rows(TT, DFF), _rows(TT, DFF), ANY, ANY, ANY],
        out_specs=[_rows(TT, DFF), _rows(TT, DFF), _rows(TT, D)],
        out_shape=[S((T, DFF), BF), S((T, DFF), BF), S((T, D), F32)],
        scratch_shapes=[pltpu.VMEM((D, DFF), BF), pltpu.VMEM((DFF, D), BF), pltpu.VMEM((DFF, D), BF),
                        pltpu.SemaphoreType.DMA((3,))],
        compiler_params=_params(),
    )(df, gate, up, wdT, wgT, wuT)


def _mix_bwd(dh2, x1, dx2, mix, woutT, g_pf, g_pm):
    T = x1.shape[0]

    def body(dh2_ref, x1_ref, dx2_ref, mix_ref, w_hbm, gpf_ref, gpm_ref,
             dx1_ref, dmix_ref, dyl_ref, dys_ref, st_ref, w_vm, sem):
        _load_once([(w_hbm, w_vm)], sem)

        @pl.when(pl.program_id(0) == 0)
        def _():
            st_ref[...] = jnp.zeros_like(st_ref)

        dxa, dgpf = _rms_bwd(x1_ref[...], gpf_ref[...], dh2_ref[...])
        dx1 = dx2_ref[...] + dxa
        dx1_ref[...] = dx1
        dmix, dgpm = _rms_bwd(mix_ref[...], gpm_ref[...], dx1)
        dmix = dmix.astype(BF)
        dmix_ref[...] = dmix
        st_ref[0:1, :] += dgpf
        st_ref[1:2, :] += dgpm
        dyl_ref[...] = jnp.dot(dmix, w_vm[:, 0:LW], preferred_element_type=F32)
        dys_ref[...] = jnp.dot(dmix, w_vm[:, LW:LW + SI], preferred_element_type=F32)

    return pl.pallas_call(
        body, name="mix_bwd", grid=(T // TT,),
        in_specs=[_rows(TT, D), _rows(TT, D), _rows(TT, D), _rows(TT, D), ANY, _whole((1, D)), _whole((1, D))],
        out_specs=[_rows(TT, D), _rows(TT, D), _rows(TT, LW), _rows(TT, SI), _whole((8, D))],
        out_shape=[S((T, D), F32), S((T, D), BF), S((T, LW), F32), S((T, SI), F32), S((8, D), F32)],
        scratch_shapes=[pltpu.VMEM((D, LW + SI), BF), pltpu.SemaphoreType.DMA((1,))],
        compiler_params=_params(),
    )(dh2, x1, dx2, mix, woutT, g_pf, g_pm)


def _halo(width, n_tiles, tile):
    per = tile // 8
    return pl.BlockSpec((8, width), lambda i: (jnp.maximum((n_tiles - 1 - i) * per - 1, 0), 0))


def _lru_bwd(dy, lxr, lxc, lg, h, p_lru, wa4, wx4, wa4T, wx4T, bufs):
    T = dy.shape[0]
    NT = T // TT
    nb = len(bufs)

    def body(*refs):
        dy_ref, lxr_ref, lxc_ref, lg_ref, h_ref, hh_ref, p_ref, wa_ref, wx_ref, waT_ref, wxT_ref = refs[:11]
        b_in = refs[11:11 + nb]
        dlx_ref, dlg_ref, st_ref, dwa_ref, dwx_ref = refs[11 + nb:16 + nb]
        b_out = refs[16 + nb:16 + 2 * nb]
        hp, dp, a_s, d_s, g_s, cc, send_sems, recv_sems = refs[16 + 2 * nb:]
        for phase, step in enumerate((0, NT - 1)):
            @pl.when(pl.program_id(0) == step)
            def _():
                _pair_phase(phase, b_in, b_out, send_sems, recv_sems)

        dy = dy_ref[...]
        first = pl.program_id(0) == 0
        top = pl.program_id(0) == NT - 1

        @pl.when(first)
        def _():
            st_ref[...] = jnp.zeros_like(st_ref)
            dwa_ref[...] = jnp.zeros_like(dwa_ref)
            dwx_ref[...] = jnp.zeros_like(dwx_ref)
            dp[TT:TT + 8, :] = jnp.zeros((8, LW), F32)
            cc[...] = jnp.zeros_like(cc)

        hp[0:8, :] = hh_ref[...] * jnp.where(top, 0.0, 1.0)
        hp[8:8 + TT, :] = h_ref[...]
        lx = lxc_ref[...]
        r, i, sp, a, mult = _lru_gates(lx, p_ref, wa_ref, wx_ref)

        lg = lg_ref[...]
        hcur = h_ref[...]
        ge = _gelu(lg)
        dgated, dgn = _rms_bwd(hcur * ge, p_ref[8:9, :], dy)
        st_ref[8:9, :] += dgn
        dlg_ref[...] = (dgated * hcur * _gelu_grad(lg)).astype(BF)
        a_s[...] = a
        d_s[...] = dgated * ge

        def step(k, c):
            t = TT - 1 - k
            g = d_s[pl.ds(t, 1), :] + c
            g_s[pl.ds(t, 1), :] = g
            return a_s[pl.ds(t, 1), :] * g

        cc[0:1, :] = lax.fori_loop(0, TT, step, cc[0:1, :], unroll=8)
        gt = g_s[...]
        da = gt * hp[pl.ds(7, TT), :]
        dmult = gt * i * lx
        di = gt * mult * lx
        dlxc = gt * mult * i
        dla = da * a - dmult * (a * a) / mult
        dr = dla * (-LRU_C * sp)
        st_ref[7:8, :] += jnp.sum(dla * (-LRU_C * r), axis=0, keepdims=True) * (-_sigmoid(-p_ref[7:8, :]))
        dzr = dr * r * (1.0 - r)
        dzi = di * i * (1.0 - i)
        st_ref[5:6, :] += jnp.sum(dzr, axis=0, keepdims=True)
        st_ref[6:7, :] += jnp.sum(dzi, axis=0, keepdims=True)
        dlxc = dlxc + _blockdiag_mm(dzr, waT_ref) + _blockdiag_mm(dzi, wxT_ref)
        for j in range(4):
            sl = slice(256 * j, 256 * (j + 1))
            pa = _mm_tn(lx[:, sl], dzr[:, sl])
            px = _mm_tn(lx[:, sl], dzi[:, sl])
            for b in range(4):
                bs = slice(BW * b, BW * (b + 1))
                dwa_ref[4 * j + b] += pa[bs, bs]
                dwx_ref[4 * j + b] += px[bs, bs]
        dlx_ref[...] = _conv_bwd(dp, dlxc, lxr_ref[...], p_ref, st_ref, TT).astype(BF)

    w4 = _whole((4, 256, 256))
    return pl.pallas_call(
        body, name="lru_bwd", grid=(NT,),
        in_specs=[_rows(TT, LW, NT), _rows(TT, LW, NT), _rows(TT, LW, NT), _rows(TT, LW, NT), _rows(TT, LW, NT),
                  _halo(LW, NT, TT), _whole((16, LW)), w4, w4, w4, w4] + [ANY] * nb,
        out_specs=[_rows(TT, LW, NT), _rows(TT, LW, NT), _whole((16, LW)), _whole((NBLK, BW, BW)), _whole((NBLK, BW, BW))]
        + [ANY] * nb,
        out_shape=[S((T, LW), BF), S((T, LW), BF), S((16, LW), F32), S((NBLK, BW, BW), F32), S((NBLK, BW, BW), F32)]
        + [S(_half_shape(b), b.dtype) for b in bufs],
        scratch_shapes=[pltpu.VMEM((TT + 8, LW), F32), pltpu.VMEM((TT + 8, LW), F32),
                        pltpu.VMEM((TT, LW), F32), pltpu.VMEM((TT, LW), F32), pltpu.VMEM((TT, LW), F32),
                        pltpu.VMEM((8, LW), F32), pltpu.SemaphoreType.DMA((nb,)), pltpu.SemaphoreType.DMA((nb,))],
        compiler_params=_params(),
    )(dy, lxr, lxc, lg, h, h, p_lru, wa4, wx4, wa4T, wx4T, *bufs)


def _ssd_bwd(dyn, xbcr, cv, z, dtr, y, states, cw_ssd, hp_ssd, g_ssd, parts):
    T = dyn.shape[0]
    NC = T // CH
    nq = len(parts)

    def body(*refs):
        dyn_ref, xr_ref, cv_ref, z_ref, dt_ref, y_ref, st_ref, cw_ref, hp_ref, g_ref = refs[:10]
        q_in = refs[10:10 + nq]
        dxbc_ref, dz_ref, ddt_ref, cst_ref, hst_ref, gst_ref = refs[10 + nq:16 + nq]
        q_out = refs[16 + nq:16 + 2 * nq]
        dp, dS, dxb, yo_s, q_s, dxs_s, t1_s, send_sems, recv_sems = refs[16 + 2 * nq:]
        dyn = dyn_ref[...]
        first = pl.program_id(0) == 0
        for phase, step in enumerate((0, NC - 1)):
            @pl.when(pl.program_id(0) == step)
            def _():
                _quad_phase(phase, q_in, q_out, send_sems, recv_sems)

        @pl.when(first)
        def _():
            cst_ref[...] = jnp.zeros_like(cst_ref)
            hst_ref[...] = jnp.zeros_like(hst_ref)
            gst_ref[...] = jnp.zeros_like(gst_ref)
            dp[CH:CH + 8, :] = jnp.zeros((8, XBC), F32)
            dS[...] = jnp.zeros_like(dS)

        cv = cv_ref[...]
        sg, xbc, raw, dtv, A, cs = _ssd_prep(cv, dt_ref, hp_ref)
        csT, dsm, E_x, ds_x, El_rows = _ssd_decays(cs)
        row_i = lax.broadcasted_iota(jnp.int32, (CH, CH), 0)
        col_i = lax.broadcasted_iota(jnp.int32, (CH, CH), 1)
        tril = row_i >= col_i
        first = col_i < HD
        head_of = ((lax.broadcasted_iota(jnp.int32, (DTP, SI), 1) >> 6)
                   == lax.broadcasted_iota(jnp.int32, (DTP, SI), 0)).astype(BF)
        head_ofT = ((lax.broadcasted_iota(jnp.int32, (SI, DTP), 0) >> 6)
                    == lax.broadcasted_iota(jnp.int32, (SI, DTP), 1)).astype(BF)

        def hi_lo(v):
            hi = v.astype(BF)
            return hi, (v - hi.astype(F32)).astype(BF)

        def lane_sums(v):
            hi, lo = hi_lo(v)
            return _mm(hi, head_ofT) + _mm(lo, head_ofT)

        zz = z_ref[...]
        sz = _sigmoid(zz)
        yv = y_ref[...]
        dgn, dg = _rms_bwd(yv * (zz * sz), g_ref[...], dyn)
        gst_ref[0:1, :] += dg
        dz_ref[...] = (dgn * yv * (sz + zz * sz * (1.0 - sz))).astype(BF)
        dY = dgn * (zz * sz)

        X = xbc[:, 0:SI]
        dt_x = _per_head_lanes(dtv)
        xs = X * dt_x
        xsd = (xs * ds_x).astype(BF)
        D_x = _per_head_lanes(hp_ref[...])[2:3, :]
        dcs_col = jnp.zeros((CH, DTP), F32)
        dcs_row = jnp.zeros((CH, DTP), F32)
        GW = HPG * HD
        for g in range(NG):
            gs = slice(GW * g, GW * (g + 1))
            Bg = xbc[:, SI + NS * g:SI + NS * (g + 1)].astype(BF)
            Cg = xbc[:, SI + NG * NS + NS * g:SI + NG * NS + NS * (g + 1)].astype(BF)
            G = _mm_nt(Cg, Bg)
            Sg = st_ref[0, gs, :]
            dSe = dS[gs, :]
            dYg = dY[:, gs]
            yo_s[:, gs] = _mm_nt(Cg, Sg) * E_x[:, gs]
            dP = dYg * E_x[:, gs]
            dCg = _mm(dP, Sg)
            dS[gs, :] = _mm_tn(dP, Cg) + _per_head_rows(El_rows, g) * dSe
            t1_s[gs, :] = dSe * Sg
            Q = _mm_nt(Bg, dSe)
            q_s[:, gs] = Q
            dBg = _mm(xsd[:, gs], dSe)
            dG = jnp.zeros((CH, CH), F32)
            for jj in range(HPG // 2):
                j = g * (HPG // 2) + jj
                ps = slice(2 * HD * j, 2 * HD * (j + 1))
                xs_pair = xs[:, ps]
                dxs_pair = Q[:, 2 * HD * jj:2 * HD * (jj + 1)] * ds_x[:, ps]
                for e in range(2):
                    h = 2 * j + e
                    Lm = jnp.exp(jnp.where(tril, cs[:, h:h + 1] - csT[h:h + 1, :], -1e30))
                    M = G * Lm
                    dYm = jnp.where(first if e == 0 else ~first, dY[:, ps], 0.0).astype(BF)
                    dM = _mm_nt(dYm, xs_pair)
                    dxs_pair = dxs_pair + _mm_tn(M, dYm)
                    Wm = dM * M
                    dcs_col = dcs_col + jnp.where(col_i == h, jnp.sum(Wm, axis=1, keepdims=True), 0.0)
                    dcs_row = dcs_row + jnp.where(row_i == h, -jnp.sum(Wm, axis=0, keepdims=True), 0.0)
                    dG = dG + dM * Lm
                dxs_s[:, ps] = dxs_pair
            dxb[:, SI + NS * g:SI + NS * (g + 1)] = dBg + _mm_tn(dG, Cg)
            dxb[:, SI + NG * NS + NS * g:SI + NG * NS + NS * (g + 1)] = dCg + _mm(dG, Bg)

        dxs = dxs_s[...]
        dxb[:, 0:SI] = D_x * dY + dxs * dt_x
        dds = lane_sums(q_s[...] * xs) * dsm
        dcs_col = dcs_col + lane_sums(dY * yo_s[...]) - dds
        ddt_col = lane_sums(dxs * X)
        dD = jnp.sum(lane_sums(dY * X), axis=0, keepdims=True)
        t_hi, t_lo = hi_lo(t1_s[...])
        dcl_rows = jnp.sum(_mm(head_of, t_hi) + _mm(head_of, t_lo), axis=1, keepdims=True) * jnp.exp(csT[:, CH - 1:CH])
        dcs_row = dcs_row + jnp.where(col_i == CH - 1, dcl_rows, 0.0)
        dcs_col = dcs_col + jnp.where(row_i == CH - 1, jnp.sum(dds, axis=0, keepdims=True), 0.0)

        da = _rev_cumsum_rows(dcs_col + dcs_row.T, CH)
        ddt_col = ddt_col + da * A
        hst_ref[1:2, :] += jnp.sum(da * dtv, axis=0, keepdims=True) * A
        hst_ref[2:3, :] += dD
        draw = jnp.where(col_i < NH, ddt_col * _sigmoid(raw), 0.0)
        ddt_ref[...] = draw.astype(BF)
        hst_ref[0:1, :] += jnp.sum(draw, axis=0, keepdims=True)

        dcv = dxb[...] * (sg + cv * sg * (1.0 - sg))
        dxbc_ref[...] = _conv_bwd(dp, dcv, xr_ref[...], cw_ref, cst_ref, CH).astype(BF)

    return pl.pallas_call(
        body, name="ssd_bwd", grid=(NC,),
        in_specs=[_rows(CH, SI, NC), _rows(CH, XBC, NC), _rows(CH, XBC, NC), _rows(CH, SI, NC), _rows(CH, DTP, NC),
                  _rows(CH, SI, NC), pl.BlockSpec((1, NH * HD, NS), lambda i: (NC - 1 - i, 0, 0)),
                  _whole((8, XBC)), _whole((8, DTP)), _whole((1, SI))] + [ANY] * nq,
        out_specs=[_rows(CH, XBC, NC), _rows(CH, SI, NC), _rows(CH, DTP, NC), _whole((16, XBC)), _whole((16, DTP)),
                   _whole((8, SI))] + [ANY] * nq,
        out_shape=[S((T, XBC), BF), S((T, SI), BF), S((T, DTP), BF), S((16, XBC), F32), S((16, DTP), F32), S((8, SI), F32)]
        + [S(p.shape, p.dtype) for p in parts],
        scratch_shapes=[pltpu.VMEM((CH + 8, XBC), F32), pltpu.VMEM((NH * HD, NS), F32),
                        pltpu.VMEM((CH, XBC), F32), pltpu.VMEM((CH, SI), F32), pltpu.VMEM((CH, SI), F32),
                        pltpu.VMEM((CH, SI), F32), pltpu.VMEM((NH * HD, NS), F32),
                        pltpu.SemaphoreType.DMA((3 * nq,)), pltpu.SemaphoreType.DMA((3 * nq,))],
        compiler_params=_params(),
    )(dyn, xbcr, cv, z, dtr, y, states, cw_ssd, hp_ssd, g_ssd, *parts)


def _inproj_bwd(dlx, dlg, dz, dxbc, ddt, x, dx1, wcatT, g0, parts):
    T = x.shape[0]
    NT = T // TT
    nq = len(parts)

    def body(*refs):
        dlx_ref, dlg_ref, dz_ref, dxbc_ref, ddt_ref, x_ref, dx1_ref, w_hbm, g_ref = refs[:9]
        q_in = refs[9:9 + nq]
        dx_ref, st_ref = refs[9 + nq:11 + nq]
        q_out = refs[11 + nq:11 + 2 * nq]
        w_vm, sem, send_sems, recv_sems = refs[11 + 2 * nq:]
        _load_once([(w_hbm, w_vm)], sem)
        for phase, step in enumerate((0, NT - 1)):
            @pl.when(pl.program_id(0) == step)
            def _():
                _quad_phase(phase, q_in, q_out, send_sems, recv_sems)

        @pl.when(pl.program_id(0) == 0)
        def _():
            st_ref[...] = jnp.zeros_like(st_ref)

        dh = jnp.dot(dlx_ref[...], w_vm[0:1024, :], preferred_element_type=F32)
        dh = dh + jnp.dot(dlg_ref[...], w_vm[1024:2048, :], preferred_element_type=F32)
        dh = dh + jnp.dot(dz_ref[...], w_vm[2048:3072, :], preferred_element_type=F32)
        dh = dh + jnp.dot(dxbc_ref[...], w_vm[3072:3072 + XBC, :], preferred_element_type=F32)
        dh = dh + jnp.dot(ddt_ref[...], w_vm[3072 + XBC:PC, :], preferred_element_type=F32)
        dx, dg = _rms_bwd(x_ref[...], g_ref[...], dh)
        dx_ref[...] = dx1_ref[...] + dx
        st_ref[0:1, :] += dg

    return pl.pallas_call(
        body, name="inproj_bwd", grid=(NT,),
        in_specs=[_rows(TT, 1024), _rows(TT, 1024), _rows(TT, 1024), _rows(TT, XBC), _rows(TT, DTP), _rows(TT, D),
                  _rows(TT, D), ANY, _whole((1, D))] + [ANY] * nq,
        out_specs=[_rows(TT, D), _whole((8, D))] + [ANY] * nq,
        out_shape=[S((T, D), F32), S((8, D), F32)] + [S(p.shape, p.dtype) for p in parts],
        scratch_shapes=[pltpu.VMEM((PC, D), BF), pltpu.SemaphoreType.DMA((1,)),
                        pltpu.SemaphoreType.DMA((3 * nq,)), pltpu.SemaphoreType.DMA((3 * nq,))],
        compiler_params=_params(),
    )(dlx, dlg, dz, dxbc, ddt, x, dx1, wcatT, g0, *parts)


def _wgrad(name, a, b):
    T, M = a.shape
    N = b.shape[1]
    tk = min(T, 2048 if M <= 1024 else 1024)
    tn = N
    while M * tn * 4 > (6 << 20) and tn % 256 == 0:
        tn //= 2

    def body(a_ref, b_ref, o_ref):
        p = lax.dot_general(a_ref[...], b_ref[...], (((0,), (0,)), ((), ())), preferred_element_type=F32)

        @pl.when(pl.program_id(1) == 0)
        def _():
            o_ref[...] = p

        @pl.when(pl.program_id(1) > 0)
        def _():
            o_ref[...] += p

    return pl.pallas_call(
        body, name=name, grid=(N // tn, T // tk),
        in_specs=[pl.BlockSpec((tk, M), lambda j, k: (k, 0)), pl.BlockSpec((tk, tn), lambda j, k: (k, j))],
        out_specs=pl.BlockSpec((M, tn), lambda j, k: (0, j)), out_shape=S((M, N), F32),
        compiler_params=_params(2),
    )(a, b)


def _adamw(name, w, g, m, v):
    _, R, C = w.shape

    def body(w_ref, g_ref, m_ref, v_ref, d_ref, nm_ref, nv_ref):
        d_ref[0], nm_ref[0], nv_ref[0] = _adam_math(w_ref[0], g_ref[...], m_ref[0], v_ref[0])

    if R % 8 == 0:
        tr = _row_tile(R, C)
        n_tiles = R // tr
        blk, gblk = pl.BlockSpec((1, tr, C), lambda i: (0, i, 0)), pl.BlockSpec((tr, C), lambda i: (i, 0))
    else:
        tc = 128 * max(k for k in range(1, C // 128 + 1) if C % (128 * k) == 0 and R * 128 * k * 4 <= (5 << 18))
        n_tiles = C // tc
        blk, gblk = pl.BlockSpec((1, R, tc), lambda i: (0, 0, i)), pl.BlockSpec((R, tc), lambda i: (0, i))
    return pl.pallas_call(
        body, name=name, grid=(n_tiles,),
        in_specs=[blk, gblk, blk, blk], out_specs=[blk] * 3,
        out_shape=[S((1, R, C), F32)] * 3, compiler_params=_params(),
    )(w, g, m, v)


def _pair_exchange(name, bufs):
    n = len(bufs)

    def body(*refs):
        for phase in range(2):
            _pair_phase(phase, refs[:n], refs[n:2 * n], refs[2 * n], refs[2 * n + 1])

    return pl.pallas_call(
        body, name=name, in_specs=[ANY] * n, out_specs=[ANY] * n,
        out_shape=[S(_half_shape(b), b.dtype) for b in bufs],
        scratch_shapes=[pltpu.SemaphoreType.DMA((n,)), pltpu.SemaphoreType.DMA((n,))],
    )(*bufs)


def _quad_exchange(bufs):
    n = len(bufs)

    def body(*refs):
        ins, outs = refs[:n], refs[n:2 * n]
        send_sems, recv_sems = refs[2 * n], refs[2 * n + 1]
        x, y, c = _pos()
        me = 2 * x + y
        chips = _other_chips(x, y)
        copies = []
        for k, (src, dst) in enumerate(zip(ins, outs)):
            for j, (cx, cy) in enumerate(chips):
                cp = _remote(src, dst.at[me], send_sems.at[3 * k + j], recv_sems.at[3 * k + j], (cx, cy, c))
                cp.start()
                copies.append(cp)
        for k, (src, dst) in enumerate(zip(ins, outs)):
            for j, (cx, cy) in enumerate(chips):
                blk = dst.at[2 * cx + cy]
                _remote(blk, blk, send_sems.at[3 * k + j], recv_sems.at[3 * k + j], (cx, cy, c)).wait_recv()
        for cp in copies:
            cp.wait_send()

    return pl.pallas_call(
        body, name="quad_exchange", in_specs=[ANY] * n, out_specs=[ANY] * n,
        out_shape=[S((4,) + b.shape, b.dtype) for b in bufs],
        scratch_shapes=[pltpu.SemaphoreType.DMA((3 * n,)), pltpu.SemaphoreType.DMA((3 * n,))],
    )(*bufs)


def _pair_gather(bufs):
    n = len(bufs)

    def body(*refs):
        ins, outs = refs[:n], refs[n:2 * n]
        send_sems, recv_sems = refs[2 * n], refs[2 * n + 1]
        x, y, c = _pos()
        copies = []
        for k, buf in enumerate(outs):
            mine = _half(buf, c, buf.shape[0] // 2)
            cp = _remote(mine, mine, send_sems.at[k], recv_sems.at[k], (x, y, 1 - c))
            cp.start()
            copies.append(cp)
        for k, buf in enumerate(outs):
            theirs = _half(buf, 1 - c, buf.shape[0] // 2)
            _remote(theirs, theirs, send_sems.at[k], recv_sems.at[k], (x, y, 1 - c)).wait_recv()
        for cp in copies:
            cp.wait_send()

    return pl.pallas_call(
        body, name="pair_gather", in_specs=[ANY] * n, out_specs=[ANY] * n,
        out_shape=[S(b.shape, b.dtype) for b in bufs], input_output_aliases={k: k for k in range(n)},
        scratch_shapes=[pltpu.SemaphoreType.DMA((n,)), pltpu.SemaphoreType.DMA((n,))],
    )(*bufs)


def _row_tile(rows, cols, mult=8):
    best = mult
    for t in range(mult, rows + 1, mult):
        if rows % t == 0 and t * cols * 4 <= (1 << 20):
            best = t
    return best


def _add_own_half(name, full, got, c, out_dtype, by_columns):
    hr = got.shape[-2]
    wide = got.shape[-1]
    cols = wide // 4 if by_columns else wide
    tr = _row_tile(hr, wide, 16)
    per = hr // tr

    if by_columns:
        def body(c_ref, a_ref, b_ref, o_ref):
            v = a_ref[...] + b_ref[...]
            for j in range(4):
                o_ref[j] = v[:, j * cols:(j + 1) * cols].astype(out_dtype)

        in_specs = [pl.BlockSpec((tr, wide), lambda i, c_ref: (c_ref[0] * per + i, 0)),
                    pl.BlockSpec((tr, wide), lambda i, c_ref: (i, 0))]
        out_specs = pl.BlockSpec((4, tr, cols), lambda i, c_ref: (0, i, 0))
        grid = (per,)
    else:
        def body(c_ref, a_ref, b_ref, o_ref):
            o_ref[...] = (a_ref[...] + b_ref[...]).astype(out_dtype)

        in_specs = [pl.BlockSpec((1, tr, cols), lambda s, i, c_ref: (s, c_ref[0] * per + i, 0)),
                    pl.BlockSpec((1, tr, cols), lambda s, i, c_ref: (s, i, 0))]
        out_specs = pl.BlockSpec((1, tr, cols), lambda s, i, c_ref: (s, i, 0))
        grid = (4, per)
    return pl.pallas_call(
        body, name=name,
        grid_spec=pltpu.PrefetchScalarGridSpec(num_scalar_prefetch=1, grid=grid, in_specs=in_specs, out_specs=out_specs),
        out_shape=S((4, hr, cols), out_dtype), compiler_params=_params(len(grid)),
    )(jnp.reshape(c, (1,)).astype(jnp.int32), full, got)


def _small_add_own_half(fulls, gots, c):
    n = len(fulls)

    def body(c_ref, *refs):
        for a_ref, b_ref, o_ref in zip(refs[:n], refs[n:2 * n], refs[2 * n:]):
            hr = b_ref.shape[0]
            o_ref[...] = a_ref[pl.ds(pl.multiple_of(c_ref[0] * hr, 8), hr), :] + b_ref[...]

    specs = lambda arrs: [pl.BlockSpec(a.shape, lambda i, c_ref: (0, 0)) for a in arrs]
    return pl.pallas_call(
        body, name="small_pair_add",
        grid_spec=pltpu.PrefetchScalarGridSpec(num_scalar_prefetch=1, grid=(1,), in_specs=specs(fulls) + specs(gots),
                                               out_specs=specs(gots)),
        out_shape=[S(g.shape, F32) for g in gots], compiler_params=_params(),
    )(jnp.reshape(c, (1,)).astype(jnp.int32), *fulls, *gots)


def _small_sum_slots(own, slots, me, c):
    n = len(slots)

    def body(p_ref, *refs):
        own_refs, slot_refs, o_refs = refs[:n], refs[n:5 * n], refs[5 * n:]
        for i, (own_ref, o_ref) in enumerate(zip(own_refs, o_refs)):
            hr = own_ref.shape[0]
            acc = None
            for j in range(4):
                v = jnp.where(p_ref[0] == j, own_ref[...], slot_refs[4 * i + j][0])
                acc = v if acc is None else acc + v
            o_ref[pl.ds(pl.multiple_of(p_ref[1] * hr, 8), hr), :] = acc

    def slot_spec(s, j):
        return pl.BlockSpec((1,) + s.shape[1:], lambda i, p: (jnp.where(p[0] == j, (j + 1) % 4, j), 0, 0))

    outs = [S((2 * s.shape[1], s.shape[2]), F32) for s in slots]
    return pl.pallas_call(
        body, name="small_quad_sum",
        grid_spec=pltpu.PrefetchScalarGridSpec(
            num_scalar_prefetch=1, grid=(1,),
            in_specs=[pl.BlockSpec(o.shape, lambda i, p: (0, 0)) for o in own]
            + [slot_spec(s, j) for s in slots for j in range(4)],
            out_specs=[pl.BlockSpec(o.shape, lambda i, p: (0, 0)) for o in outs]),
        out_shape=outs, compiler_params=_params(),
    )(jnp.stack([me, c]).astype(jnp.int32), *own, *[s for s in slots for _ in range(4)])


def _sum_slots(name, own, slots, me, c):
    _, rows, cols = slots.shape
    tr = _row_tile(rows, cols, 16 if slots.dtype == jnp.bfloat16 else 8)
    per = rows // tr
    three = len(own.shape) == 3

    def body(p_ref, own_ref, s0, s1, s2, s3, o_ref):
        mine = own_ref[0] if three else own_ref[...]
        acc = None
        for j, s_ref in enumerate((s0, s1, s2, s3)):
            v = jnp.where(p_ref[0] == j, mine, s_ref[0]).astype(F32)
            acc = v if acc is None else acc + v
        o_ref[...] = acc

    def slot_spec(j):
        return pl.BlockSpec((1, tr, cols), lambda i, p: (jnp.where(p[0] == j, (j + 1) % 4, j), i, 0))

    own_spec = (pl.BlockSpec((1, tr, cols), lambda i, p: (p[0], i, 0)) if three
                else pl.BlockSpec((tr, cols), lambda i, p: (i, 0)))
    return pl.pallas_call(
        body, name=name,
        grid_spec=pltpu.PrefetchScalarGridSpec(
            num_scalar_prefetch=1, grid=(per,), in_specs=[own_spec] + [slot_spec(j) for j in range(4)],
            out_specs=pl.BlockSpec((tr, cols), lambda i, p: (p[1] * per + i, 0))),
        out_shape=S((2 * rows, cols), F32), compiler_params=_params(),
    )(jnp.stack([me, c]).astype(jnp.int32), own, slots, slots, slots, slots)


BIG = ("w_in", "w_out", "w_gate", "w_up", "w_down")
ROW_PARAMS = (("pre_mix_norm", 0), ("lru_conv_b", 12), ("lru_ba", 13), ("lru_bx", 14), ("lru_lambda", 15),
              ("lru_out_norm", 16), ("ssd_out_norm", 24), ("post_mix_norm", 33), ("pre_ffn_norm", 32), ("post_ffn_norm", 41))
LRU_CONV_ROWS = (8, 12)
LOSS_ROW = 40
HEAD_PARAMS = (("ssd_dt_bias", 0), ("ssd_a_log", 1), ("ssd_d", 2))
SMALL = tuple(n for n, _ in ROW_PARAMS) + ("ssd_conv_b",) + tuple(n for n, _ in HEAD_PARAMS) + (
    "lru_wa", "lru_wx", "lru_conv_w", "ssd_conv_w")


def _diag4(w):
    eye = jnp.eye(4, dtype=w.dtype).reshape(1, 4, 1, 4, 1)
    return (w.reshape(4, 4, BW, 1, BW) * eye).reshape(4, 4 * BW, 4 * BW)


def _adam_math(w, g, m, v):
    mm = ADAM_B1 * m + (1.0 - ADAM_B1) * g
    vv = ADAM_B2 * v + (1.0 - ADAM_B2) * (g * g)
    c1 = 1.0 - ADAM_B1 ** ADAM_STEP
    c2 = 1.0 - ADAM_B2 ** ADAM_STEP
    return -ADAM_LR * ((mm / c1) / (jnp.sqrt(vv / c2) + ADAM_EPS) + ADAM_WD * w), mm, vv


def _adamw_small(rows, cst, hst, dwa, dwx, glcw, gscw, w, m, v):
    def grad_of(name, refs):
        rows_ref, cst_ref, hst_ref, dwa_ref, dwx_ref, glcw_ref, gscw_ref = refs
        for n, r in ROW_PARAMS:
            if n == name:
                return rows_ref[r:r + 1, :]
        for n, r in HEAD_PARAMS:
            if n == name:
                return hst_ref[r:r + 1, 0:NH]
        return {"ssd_conv_b": lambda: cst_ref[4:5, :], "lru_wa": lambda: dwa_ref[...], "lru_wx": lambda: dwx_ref[...],
                "lru_conv_w": lambda: glcw_ref[...], "ssd_conv_w": lambda: gscw_ref[...]}[name]()

    shapes = {n: (w[n].shape[1:] if len(w[n].shape) > 2 else w[n].shape) for n in SMALL}
    flat = lambda d: [d[n].reshape(shapes[n]) for n in SMALL]
    ns = len(SMALL)

    def body(*refs):
        srcs, rest = refs[:7], refs[7:]
        w_refs, m_refs, v_refs = rest[:ns], rest[ns:2 * ns], rest[2 * ns:3 * ns]
        outs = rest[3 * ns:]
        for k, name in enumerate(SMALL):
            g = grad_of(name, srcs)
            d, mm, vv = _adam_math(w_refs[k][...], g, m_refs[k][...], v_refs[k][...])
            outs[4 * k][...] = g
            outs[4 * k + 1][...] = d
            outs[4 * k + 2][...] = mm
            outs[4 * k + 3][...] = vv

    res = pl.pallas_call(
        body, name="adamw_small",
        out_shape=[S(shapes[n], F32) for n in SMALL for _ in range(4)],
        compiler_params=pltpu.CompilerParams(vmem_limit_bytes=VMEM_LIMIT),
    )(rows, cst, hst, dwa, dwx, glcw, gscw, *flat(w), *flat(m), *flat(v))
    return {n: tuple(res[4 * k + i].reshape(w[n].shape) for i in range(4)) for k, n in enumerate(SMALL)}


def _with_own(own, got):
    chip = 2 * lax.axis_index("x") + lax.axis_index("y")
    return jnp.where((jnp.arange(4) == chip).reshape(4, 1, 1), own[None], got)


def _side_by_side(f):
    return f.transpose(1, 0, 2).reshape(f.shape[1], 4 * f.shape[2])


def _stacked(f):
    return f.reshape(4 * f.shape[1], f.shape[2])


def _conv_terms(lru_conv_w, ssd_conv_w):
    conv = jnp.concatenate([lru_conv_w.reshape(-1), ssd_conv_w.reshape(-1)]).astype(F32)
    hi = conv.astype(jnp.bfloat16)
    mid = (conv - hi.astype(F32)).astype(jnp.bfloat16)
    lo = (conv - hi.astype(F32) - mid.astype(F32)).astype(jnp.bfloat16)
    terms = jnp.concatenate([hi, mid, lo])
    rows = -(-terms.shape[0] // (128 * 32)) * 32
    return jnp.pad(terms, (0, rows * 128 - terms.shape[0])).reshape(rows, 128)


def _full_conv_taps(own, got, n_lru, n_ssd):
    n_terms = 3 * (n_lru + n_ssd)
    t3 = _with_own(own, got).reshape(4, -1)[:, :n_terms].reshape(4, 3, -1).astype(F32)
    conv_f = (t3[:, 0] + t3[:, 1]) + t3[:, 2]
    lcw = conv_f[:, :n_lru].reshape(4, CONV_K, -1).transpose(1, 0, 2).reshape(CONV_K, LW)
    scw = conv_f[:, n_lru:].reshape(4, CONV_K, -1).transpose(1, 0, 2).reshape(CONV_K, XBC)
    return lcw, scw


def _step(x, tgt, w_in, lru_conv_w, ssd_conv_w, sp, late):
    c = lax.axis_index("c")
    me = 2 * lax.axis_index("x") + lax.axis_index("y")
    mm = lambda w: w.astype(BF)
    row = lambda v: v.reshape(1, -1).astype(F32)
    g0 = row(sp["pre_mix_norm"])
    first = [w_in.astype(WIRE), _conv_terms(lru_conv_w, ssd_conv_w)]
    h0, *got_first = _prenorm(x, g0, first)
    win_f = _side_by_side(_with_own(first[0], got_first[0]))
    lcw, scw = _full_conv_taps(first[1], got_first[1], lru_conv_w.size, ssd_conv_w.size)
    wcat = jnp.concatenate([mm(win_f), jnp.zeros((D, PC - IN_COLS), BF)], axis=1)
    p_lru = jnp.concatenate([lcw, row(sp["lru_conv_b"]), row(sp["lru_ba"]), row(sp["lru_bx"]), row(sp["lru_lambda"]),
                             row(sp["lru_out_norm"]), jnp.zeros((7, LW), F32)], axis=0)
    wa4, wx4 = mm(_diag4(sp["lru_wa"][0])), mm(_diag4(sp["lru_wx"][0]))
    wa4T, wx4T = wa4.transpose(0, 2, 1), wx4.transpose(0, 2, 1)
    cw_ssd = jnp.concatenate([scw, row(sp["ssd_conv_b"]), jnp.zeros((3, XBC), F32)], axis=0)
    padh = lambda v: jnp.pad(row(v), ((0, 0), (0, DTP - NH)))
    hp_ssd = jnp.concatenate([padh(sp["ssd_dt_bias"]), padh(sp["ssd_a_log"]), padh(sp["ssd_d"]), jnp.zeros((5, DTP), F32)], axis=0)
    g_ssd = row(sp["ssd_out_norm"])
    g_pm, g_pf, g_pff = row(sp["post_mix_norm"]), row(sp["pre_ffn_norm"]), row(sp["post_ffn_norm"])

    h, ylru, lxc, lxr, lg, *got_a = _lru_fwd(h0, wcat, p_lru, wa4, wx4, [late[0], late[3]])
    y, yssd, states, cv, z, xbcr, dtr, *got_b = _ssd_fwd(h0, wcat, cw_ssd, hp_ssd, g_ssd, [late[1], late[2]])
    wout, wd = mm(_stacked(_with_own(late[0], got_a[0]))), mm(_stacked(_with_own(late[3], got_a[1])))
    wg, wu = mm(_side_by_side(_with_own(late[1], got_b[0]))), mm(_side_by_side(_with_own(late[2], got_b[1])))
    mix, x1, h2 = _outproj(ylru, yssd, x, wout, g_pm, g_pf)
    gate, up, act, df, dx2, st_ffn = _ffn_fwd(h2, x1, tgt, wg, wu, wd, g_pff)
    dgate, dup, dh2 = _ffn_bwd(df, gate, up, wd.T, wg.T, wu.T)
    dx1, dmix, dyl, dys, st_mix = _mix_bwd(dh2, x1, dx2, mix, wout.T, g_pf, g_pm)

    dwg = _wgrad("wgrad_gate", h2, dgate)
    dwu = _wgrad("wgrad_up", h2, dup)
    dwd = _wgrad("wgrad_down", act, df)
    dwo = jnp.concatenate([_wgrad("wgrad_out_lru", ylru, dmix), _wgrad("wgrad_out_ssd", yssd, dmix)], axis=0)
    early = [dwo.reshape(4, (LW + SI) // 4, D), dwg, dwu, dwd.reshape(4, DFF // 4, D)]
    dlx, dlg, st_lru, dwa, dwx, *got_early = _lru_bwd(dyl, lxr, lxc, lg, h, p_lru, wa4, wx4, wa4T, wx4T, early)
    part_early = [_add_own_half("pair_add_early%d" % k, b, r, c, WIRE, bc)
                  for k, (b, r, bc) in enumerate(zip(early, got_early, [False, True, True, False]))]
    dxbc, dz, ddt, cst, hst, gst, *slots_early = _ssd_bwd(dys, xbcr, cv, z, dtr, y, states, cw_ssd, hp_ssd, g_ssd,
                                                          part_early)
    red_early = [_sum_slots("quad_sum_early%d" % k, p, s, me, c) for k, (p, s) in enumerate(zip(part_early, slots_early))]

    pin = [_wgrad("wgrad_in_%d" % k, h0, b) for k, b in enumerate((dlx, dlg, dz, dxbc, ddt))]
    dwin = jnp.concatenate(pin[:4] + [pin[4][:, :NH]], axis=1)
    (got_win,) = _pair_exchange("pair_exchange_w_in", [dwin])
    part_win = _add_own_half("pair_add_w_in", dwin, got_win, c, WIRE, True)
    gx, st_in, slots_win = _inproj_bwd(dlx, dlg, dz, dxbc, ddt, x, dx1, wcat.T, g0, [part_win])
    red_win = _sum_slots("quad_sum_w_in", part_win, slots_win, me, c)

    rows = jnp.concatenate([st_in, st_lru, gst, st_mix, st_ffn], axis=0)
    small = [rows, cst, hst, dwa.reshape(NBLK * BW, BW), dwx.reshape(NBLK * BW, BW)]
    part_small = list(_small_add_own_half(small, list(_pair_exchange("pair_exchange_small", small)), c))
    red_small = list(_small_sum_slots(part_small, list(_quad_exchange(part_small)), me, c))
    out = list(_pair_gather([red_win] + red_early + red_small))
    big = dict(zip(("w_in", "w_out", "w_gate", "w_up", "w_down"), out[:5]))
    return gx, big, out[5:]


def kernel(x, pre_mix_norm, w_in, lru_conv_w, lru_conv_b, lru_wa, lru_ba, lru_wx, lru_bx, lru_lambda, lru_out_norm, ssd_conv_w, ssd_conv_b, ssd_dt_bias, ssd_a_log, ssd_d, ssd_out_norm, w_out, post_mix_norm, pre_ffn_norm, w_gate, w_up, w_down, post_ffn_norm, loss_target, m_pre_mix_norm, m_w_in, m_lru_conv_w, m_lru_conv_b, m_lru_wa, m_lru_ba, m_lru_wx, m_lru_bx, m_lru_lambda, m_lru_out_norm, m_ssd_conv_w, m_ssd_conv_b, m_ssd_dt_bias, m_ssd_a_log, m_ssd_d, m_ssd_out_norm, m_w_out, m_post_mix_norm, m_pre_ffn_norm, m_w_gate, m_w_up, m_w_down, m_post_ffn_norm, v_pre_mix_norm, v_w_in, v_lru_conv_w, v_lru_conv_b, v_lru_wa, v_lru_ba, v_lru_wx, v_lru_bx, v_lru_lambda, v_lru_out_norm, v_ssd_conv_w, v_ssd_conv_b, v_ssd_dt_bias, v_ssd_a_log, v_ssd_d, v_ssd_out_norm, v_w_out, v_post_mix_norm, v_pre_ffn_norm, v_w_gate, v_w_up, v_w_down, v_post_ffn_norm):
    args = dict(locals())
    names = list(SMALL) + list(BIG)
    w = {n: args[n] for n in names}
    m = {n: args["m_" + n] for n in names}
    v = {n: args["v_" + n] for n in names}
    chip = 2 * lax.axis_index("x") + lax.axis_index("y")

    late = [a[0].astype(WIRE) for a in (w_out, w_gate, w_up, w_down)]
    gx, red, (rows, cst, hst, dwa, dwx) = _step(x[0], loss_target[0], w_in[0], lru_conv_w[0], ssd_conv_w[0],
                                                {n: w[n] for n in SMALL}, late)
    loss = jnp.sum(rows[LOSS_ROW])

    grads, delta, new_m, new_v = {}, {}, {}, {}
    for n in BIG:
        g = red[n]
        if n in ("w_in", "w_gate", "w_up"):
            t = lambda a: jnp.swapaxes(a, 1, 2)
            gt = g.T
            out = _adamw("adamw_" + n, t(w[n]), gt, t(m[n]), t(v[n]))
            delta[n], new_m[n], new_v[n] = (t(o) for o in out)
            grads[n] = t(gt[None])
        else:
            delta[n], new_m[n], new_v[n] = _adamw("adamw_" + n, w[n], g, m[n], v[n])
            grads[n] = g[None]

    lc, sc = lru_conv_w.shape[-1], ssd_conv_w.shape[-1]
    glcw = lax.dynamic_slice_in_dim(rows[LRU_CONV_ROWS[0]:LRU_CONV_ROWS[1]], chip * lc, lc, axis=1)
    gscw = lax.dynamic_slice_in_dim(cst[0:CONV_K], chip * sc, sc, axis=1)
    res = _adamw_small(rows, cst, hst, dwa.reshape(NBLK, BW, BW), dwx.reshape(NBLK, BW, BW), glcw, gscw,
                       {n: w[n] for n in SMALL}, {n: m[n] for n in SMALL}, {n: v[n] for n in SMALL})
    for n in SMALL:
        grads[n], delta[n], new_m[n], new_v[n] = res[n]

    order = ["pre_mix_norm", "w_in", "lru_conv_w", "lru_conv_b", "lru_wa", "lru_ba", "lru_wx", "lru_bx", "lru_lambda",
             "lru_out_norm", "ssd_conv_w", "ssd_conv_b", "ssd_dt_bias", "ssd_a_log", "ssd_d", "ssd_out_norm", "w_out",
             "post_mix_norm", "pre_ffn_norm", "w_gate", "w_up", "w_down", "post_ffn_norm"]
    return (loss, gx[None], *[grads[n] for n in order], *[delta[n] for n in order],
            *[new_m[n] for n in order], *[new_v[n] for n in order])
```

```python
import functools

import jax
import jax.numpy as jnp
from jax import lax
from jax.experimental import pallas as pl
from jax.experimental.pallas import tpu as pltpu

F32 = jnp.float32
BF = jnp.bfloat16

D = 1024
LW = 1024
NBLK = 16
BW = 64
SI = 1024
NH = 16
HD = 64
NG = 2
HPG = NH // NG
NS = 128
CH = 128
XBC = SI + 2 * NG * NS
DTP = 128
PC = 3 * 1024 + XBC + DTP
DFF = 2816
IN_COLS = 4624
EPS = 1e-6
LRU_C = 8.0
CONV_K = 4
TT = 256
VMEM_LIMIT = 56 * 1024 * 1024

ADAM_LR, ADAM_B1, ADAM_B2, ADAM_EPS, ADAM_WD, ADAM_STEP = 0.001, 0.9, 0.999, 1e-08, 0.01, 10

MESH = pl.DeviceIdType.MESH


def _mm(a, b):
    return jnp.dot(a.astype(BF), b.astype(BF), preferred_element_type=F32)


def _mm_nt(a, b):
    return lax.dot_general(a.astype(BF), b.astype(BF), (((1,), (1,)), ((), ())), preferred_element_type=F32)


def _mm_tn(a, b):
    return lax.dot_general(a.astype(BF), b.astype(BF), (((0,), (0,)), ((), ())), preferred_element_type=F32)


def _sigmoid(x):
    return 0.5 * jnp.tanh(0.5 * x) + 0.5


def _softplus(x):
    return jnp.maximum(x, 0.0) + jnp.log1p(jnp.exp(-jnp.abs(x)))


_GELU_C = 0.7978845608028654
_GELU_K = 0.044715


def _gelu(x):
    t = jnp.tanh(_GELU_C * (x + _GELU_K * x * x * x))
    return 0.5 * x * (1.0 + t)


def _gelu_grad(x):
    t = jnp.tanh(_GELU_C * (x + _GELU_K * x * x * x))
    return 0.5 * (1.0 + t) + 0.5 * x * (1.0 - t * t) * _GELU_C * (1.0 + 3.0 * _GELU_K * x * x)


def _rms_fwd(x, g):
    r = lax.rsqrt(jnp.mean(x * x, axis=-1, keepdims=True) + EPS)
    return x * r * g


def _rms_bwd(x, g, dy):
    r = lax.rsqrt(jnp.mean(x * x, axis=-1, keepdims=True) + EPS)
    xh = x * r
    dxh = dy * g
    dg = jnp.sum(dy * xh, axis=0, keepdims=True)
    dx = r * (dxh - xh * jnp.mean(dxh * xh, axis=-1, keepdims=True))
    return dx, dg


def _sum_all(x):
    return jnp.sum(jnp.sum(x, axis=1, keepdims=True), axis=0, keepdims=True)


def _cumsum_rows(x, n):
    row = lax.broadcasted_iota(jnp.int32, x.shape, 0)
    k = 1
    while k < n:
        x = x + jnp.where(row >= k, pltpu.roll(x, k, 0), 0.0)
        k *= 2
    return x


def _rev_cumsum_rows(x, n):
    row = lax.broadcasted_iota(jnp.int32, x.shape, 0)
    k = 1
    while k < n:
        x = x + jnp.where(row < n - k, pltpu.roll(x, n - k, 0), 0.0)
        k *= 2
    return x


def _load_once(pairs, sem):
    @pl.when(pl.program_id(0) == 0)
    def _():
        for k, (src, dst) in enumerate(pairs):
            pltpu.make_async_copy(src, dst, sem.at[k]).start()
        for k, (src, dst) in enumerate(pairs):
            pltpu.make_async_copy(src, dst, sem.at[k]).wait()


def _params(n_axes=1):
    return pltpu.CompilerParams(dimension_semantics=("arbitrary",) * n_axes, vmem_limit_bytes=VMEM_LIMIT)


def _rows(n, width, rev_of=None):
    if rev_of is None:
        return pl.BlockSpec((n, width), lambda i: (i, 0))
    return pl.BlockSpec((n, width), lambda i: (rev_of - 1 - i, 0))


def _whole(shape):
    nd = len(shape)
    return pl.BlockSpec(shape, lambda i: (0,) * nd)


ANY = pl.BlockSpec(memory_space=pl.ANY)
S = jax.ShapeDtypeStruct
WIRE = jnp.bfloat16


def _pos():
    return lax.axis_index("x"), lax.axis_index("y"), lax.axis_index("c")


def _other_chips(x, y):
    return [(1 - x, y), (x, 1 - y), (1 - x, 1 - y)]


def _remote(src, dst, send_sem, recv_sem, to):
    return pltpu.make_async_remote_copy(src_ref=src, dst_ref=dst, send_sem=send_sem, recv_sem=recv_sem,
                                        device_id=to, device_id_type=MESH)


def _gather_phase(phase, ins, outs, send_sems, recv_sems):
    x, y, c = _pos()
    me = 2 * x + y
    chips = _other_chips(x, y)
    for i, (src, dst) in enumerate(zip(ins, outs)):
        hr = src.shape[0] // 2
        my_half = pl.ds(pl.multiple_of(c * hr, 16), hr)
        sib_half = pl.ds(pl.multiple_of((1 - c) * hr, 16), hr)
        for k, (cx, cy) in enumerate(chips):
            s1, r1 = send_sems.at[6 * i + k], recv_sems.at[6 * i + k]
            s2, r2 = send_sems.at[6 * i + 3 + k], recv_sems.at[6 * i + 3 + k]
            first = lambda: _remote(src.at[my_half, :], dst.at[me, my_half, :], s1, r1, (cx, cy, c))
            landed = dst.at[2 * cx + cy, my_half, :]
            passed = lambda: _remote(landed, landed, s2, r2, (x, y, 1 - c))
            if phase == 0:
                first().start()
            elif phase == 1:
                _remote(landed, landed, s1, r1, (cx, cy, c)).wait_recv()
                passed().start()
            else:
                theirs = dst.at[2 * cx + cy, sib_half, :]
                _remote(theirs, theirs, s2, r2, (x, y, 1 - c)).wait_recv()
                first().wait_send()
                passed().wait_send()


def _half(ref, c, hr):
    sl = pl.ds(pl.multiple_of(c * hr, 8), hr)
    return ref.at[:, sl, :] if len(ref.shape) == 3 else ref.at[sl, :]


def _half_shape(b):
    return b.shape[:-2] + (b.shape[-2] // 2, b.shape[-1])


def _pair_phase(phase, ins, outs, send_sems, recv_sems):
    x, y, c = _pos()
    for k, (src, dst) in enumerate(zip(ins, outs)):
        cp = _remote(_half(src, 1 - c, src.shape[-2] // 2), dst, send_sems.at[k], recv_sems.at[k], (x, y, 1 - c))
        if phase == 0:
            cp.start()
        else:
            cp.wait()


def _quad_phase(phase, ins, outs, send_sems, recv_sems):
    x, y, c = _pos()
    me = 2 * x + y
    for i, (src, dst) in enumerate(zip(ins, outs)):
        for k, (cx, cy) in enumerate(_other_chips(x, y)):
            cp = _remote(src.at[2 * cx + cy], dst.at[me], send_sems.at[3 * i + k], recv_sems.at[3 * i + k], (cx, cy, c))
            if phase == 0:
                cp.start()
            else:
                got = dst.at[2 * cx + cy]
                _remote(got, got, send_sems.at[3 * i + k], recv_sems.at[3 * i + k], (cx, cy, c)).wait_recv()
                cp.wait_send()


def _prenorm(x, g0, shards):
    T = x.shape[0]
    tt = 2 * TT
    nt = T // tt
    ng = len(shards)

    def body(*refs):
        x_ref, g_ref = refs[:2]
        sh_in = refs[2:2 + ng]
        h0_ref = refs[2 + ng]
        sh_out = refs[3 + ng:3 + 2 * ng]
        send_sems, recv_sems = refs[3 + 2 * ng:]
        for phase, step in enumerate((0, nt // 2, nt - 1)):
            @pl.when(pl.program_id(0) == step)
            def _():
                _gather_phase(phase, sh_in, sh_out, send_sems, recv_sems)

        h0_ref[...] = _rms_fwd(x_ref[...], g_ref[...]).astype(BF)

    return pl.pallas_call(
        body, name="prenorm", grid=(nt,),
        in_specs=[_rows(tt, D), _whole((1, D))] + [ANY] * ng, out_specs=[_rows(tt, D)] + [ANY] * ng,
        out_shape=[S((T, D), BF)] + [S((4,) + s.shape, s.dtype) for s in shards],
        scratch_shapes=[pltpu.SemaphoreType.DMA((6 * ng,)), pltpu.SemaphoreType.DMA((6 * ng,))],
        compiler_params=_params(),
    )(x, g0, *shards)


def _blockdiag_mm(v, w4_ref):
    return jnp.concatenate([_mm(v[:, 256 * j:256 * (j + 1)], w4_ref[j]) for j in range(4)], axis=1)


def _lru_gates(lx, p_ref, wa_ref, wx_ref):
    r = _sigmoid(_blockdiag_mm(lx, wa_ref) + p_ref[5:6, :])
    i = _sigmoid(_blockdiag_mm(lx, wx_ref) + p_ref[6:7, :])
    sp = _softplus(-p_ref[7:8, :])
    la = -LRU_C * r * sp
    a = jnp.exp(la)
    th = jnp.tanh(la)
    mult = jnp.sqrt(-2.0 * th / (1.0 - th))
    return r, i, sp, a, mult


def _conv_from(xp_ref, p_ref, n):
    acc = p_ref[4:5, :] + p_ref[0:1, :] * xp_ref[pl.ds(8 - CONV_K + 1, n), :]
    for k in range(1, CONV_K):
        acc = acc + p_ref[k:k + 1, :] * xp_ref[pl.ds(8 - CONV_K + 1 + k, n), :]
    return acc


def _conv_bwd(dp_ref, dconv, x, p_ref, st_ref, n):
    if dconv is None:
        dconv = dp_ref[0:n, :]
    else:
        dp_ref[0:n, :] = dconv
    acc = None
    for k in range(CONV_K):
        g = dp_ref[pl.ds(CONV_K - 1 - k, n), :]
        acc = p_ref[k:k + 1, :] * g if acc is None else acc + p_ref[k:k + 1, :] * g
        st_ref[k:k + 1, :] += jnp.sum(g * x, axis=0, keepdims=True)
    st_ref[4:5, :] += jnp.sum(dconv, axis=0, keepdims=True)
    dp_ref[n:n + 8, :] = dp_ref[0:8, :]
    return acc


def _lru_fwd(h0, wcat, p_lru, wa4, wx4, shards):
    T = h0.shape[0]
    NT = T // TT
    ng = len(shards)

    def body(*refs):
        h0_ref, w_hbm, p_ref, wa_ref, wx_ref = refs[:5]
        sh_in = refs[5:5 + ng]
        h_ref, y_ref, lxc_ref, lxr_ref, lg_ref = refs[5 + ng:10 + ng]
        sh_out = refs[10 + ng:10 + 2 * ng]
        xp, a_s, u_s, hc, w_vm, wsem, send_sems, recv_sems = refs[10 + 2 * ng:]
        _load_once([(w_hbm.at[:, 0:2 * LW], w_vm)], wsem)
        for phase, step in enumerate((0, NT // 2, NT - 1)):
            @pl.when(pl.program_id(0) == step)
            def _():
                _gather_phase(phase, sh_in, sh_out, send_sems, recv_sems)

        @pl.when(pl.program_id(0) == 0)
        def _():
            xp[0:8, :] = jnp.zeros((8, LW), F32)
            hc[...] = jnp.zeros_like(hc)

        hv = h0_ref[...]
        lxr = jnp.dot(hv, w_vm[:, 0:LW], preferred_element_type=F32)
        lxr_ref[...] = lxr
        lg_ref[...] = jnp.dot(hv, w_vm[:, LW:2 * LW], preferred_element_type=F32)
        xp[8:8 + TT, :] = lxr
        lx = _conv_from(xp, p_ref, TT)
        lxc_ref[...] = lx
        xp[0:8, :] = xp[TT:TT + 8, :]
        r, i, sp, a, mult = _lru_gates(lx, p_ref, wa_ref, wx_ref)
        a_s[...] = a
        u_s[...] = mult * (i * lx)

        def step(t, h):
            h = a_s[pl.ds(t, 1), :] * h + u_s[pl.ds(t, 1), :]
            h_ref[pl.ds(t, 1), :] = h
            return h

        hc[0:1, :] = lax.fori_loop(0, TT, step, hc[0:1, :], unroll=8)
        gated = h_ref[...] * _gelu(lg_ref[...])
        y_ref[...] = _rms_fwd(gated, p_ref[8:9, :]).astype(BF)

    return pl.pallas_call(
        body, name="lru_fwd", grid=(NT,),
        in_specs=[_rows(TT, D), ANY, _whole((16, LW)), _whole((4, 256, 256)), _whole((4, 256, 256))] + [ANY] * ng,
        out_specs=[_rows(TT, LW), _rows(TT, LW), _rows(TT, LW), _rows(TT, LW), _rows(TT, LW)] + [ANY] * ng,
        out_shape=[S((T, LW), F32), S((T, LW), BF), S((T, LW), F32), S((T, LW), F32), S((T, LW), F32)]
        + [S((4,) + s.shape, s.dtype) for s in shards],
        scratch_shapes=[pltpu.VMEM((TT + 8, LW), F32), pltpu.VMEM((TT, LW), F32), pltpu.VMEM((TT, LW), F32),
                        pltpu.VMEM((8, LW), F32), pltpu.VMEM((D, 2 * LW), BF), pltpu.SemaphoreType.DMA((1,)),
                        pltpu.SemaphoreType.DMA((6 * ng,)), pltpu.SemaphoreType.DMA((6 * ng,))],
        compiler_params=_params(),
    )(h0, wcat, p_lru, wa4, wx4, *shards)


def _ssd_prep(cv, dt_ref, hp_ref):
    sg = _sigmoid(cv)
    xbc = cv * sg
    lane = lax.broadcasted_iota(jnp.int32, (CH, DTP), 1)
    raw = dt_ref[...] + hp_ref[0:1, :]
    dtv = jnp.where(lane < NH, _softplus(raw), 0.0)
    A = jnp.where(lane[0:1, :] < NH, -jnp.exp(hp_ref[1:2, :]), 0.0)
    cs = _cumsum_rows(dtv * A, CH)
    return sg, xbc, raw, dtv, A, cs


def _per_head_lanes(v):
    r = v.shape[0]
    first = lax.broadcasted_iota(jnp.int32, (r, 2 * HD), 1) < HD
    pairs = [jnp.where(first, jnp.broadcast_to(v[:, 2 * j:2 * j + 1], (r, 2 * HD)),
                       jnp.broadcast_to(v[:, 2 * j + 1:2 * j + 2], (r, 2 * HD))) for j in range(NH // 2)]
    return jnp.concatenate(pairs, axis=1)


def _per_head_rows(col, g):
    return jnp.concatenate([jnp.broadcast_to(col[g * HPG + k:g * HPG + k + 1, :], (HD, NS)) for k in range(HPG)], axis=0)


def _ssd_decays(cs):
    csT = cs.T
    cl = cs[CH - 1:CH, :]
    E_x = _per_head_lanes(jnp.exp(cs))
    dsm = jnp.exp(cl - cs)
    ds_x = _per_head_lanes(dsm)
    El_rows = jnp.broadcast_to(jnp.exp(csT[0:NH, CH - 1:CH]), (NH, NS))
    return csT, dsm, E_x, ds_x, El_rows


def _ssd_fwd(h0, wcat, cw_ssd, hp_ssd, g_ssd, shards):
    T = h0.shape[0]
    NC = T // CH
    ng = len(shards)
    c0 = 2 * LW

    def body(*refs):
        h0_ref, w_hbm, cw_ref, hp_ref, g_ref = refs[:5]
        sh_in = refs[5:5 + ng]
        y_ref, yn_ref, st_ref, cv_ref, z_ref, xr_ref, dt_ref = refs[5 + ng:12 + ng]
        sh_out = refs[12 + ng:12 + 2 * ng]
        xp, st, w_vm, wsem, send_sems, recv_sems = refs[12 + 2 * ng:]
        _load_once([(w_hbm.at[:, c0:PC], w_vm)], wsem)
        for phase, step in enumerate((0, NC // 2, NC - 1)):
            @pl.when(pl.program_id(0) == step)
            def _():
                _gather_phase(phase, sh_in, sh_out, send_sems, recv_sems)

        @pl.when(pl.program_id(0) == 0)
        def _():
            xp[0:8, :] = jnp.zeros((8, XBC), F32)
            st[...] = jnp.zeros_like(st)

        hv = h0_ref[...]
        z_ref[...] = jnp.dot(hv, w_vm[:, 0:SI], preferred_element_type=F32)
        xraw = jnp.dot(hv, w_vm[:, SI:SI + XBC], preferred_element_type=F32)
        xr_ref[...] = xraw
        dt_ref[...] = jnp.dot(hv, w_vm[:, SI + XBC:SI + XBC + DTP], preferred_element_type=F32)
        xp[8:8 + CH, :] = xraw
        cv = _conv_from(xp, cw_ref, CH)
        cv_ref[...] = cv
        sg, xbc, raw, dtv, A, cs = _ssd_prep(cv, dt_ref, hp_ref)
        xp[0:8, :] = xp[CH:CH + 8, :]
        st_ref[0] = st[...]
        csT, dsm, E_x, ds_x, El_rows = _ssd_decays(cs)
        X = xbc[:, 0:SI]
        xs = X * _per_head_lanes(dtv)
        xsd = (xs * ds_x).astype(BF)
        DX = _per_head_lanes(hp_ref[...])[2:3, :] * X
        tril = lax.broadcasted_iota(jnp.int32, (CH, CH), 0) >= lax.broadcasted_iota(jnp.int32, (CH, CH), 1)
        first = lax.broadcasted_iota(jnp.int32, (CH, 2 * HD), 1) < HD
        GW = HPG * HD
        for g in range(NG):
            Bg = xbc[:, SI + NS * g:SI + NS * (g + 1)].astype(BF)
            Cg = xbc[:, SI + NG * NS + NS * g:SI + NG * NS + NS * (g + 1)].astype(BF)
            G = _mm_nt(Cg, Bg)
            Sg = st[GW * g:GW * (g + 1), :]
            Yo = _mm_nt(Cg, Sg) * E_x[:, GW * g:GW * (g + 1)]
            st[GW * g:GW * (g + 1), :] = _per_head_rows(El_rows, g) * Sg + _mm_tn(xsd[:, GW * g:GW * (g + 1)], Bg)
            for jj in range(HPG // 2):
                j = g * (HPG // 2) + jj
                ps = slice(2 * HD * j, 2 * HD * (j + 1))
                xs_pair = xs[:, ps]
                acc = Yo[:, 2 * HD * jj:2 * HD * (jj + 1)] + DX[:, ps]
                for e in range(2):
                    h = 2 * j + e
                    Lm = jnp.exp(jnp.where(tril, cs[:, h:h + 1] - csT[h:h + 1, :], -1e30))
                    acc = acc + _mm(G * Lm, jnp.where(first if e == 0 else ~first, xs_pair, 0.0))
                y_ref[:, ps] = acc
        zz = z_ref[...]
        gated = y_ref[...] * (zz * _sigmoid(zz))
        yn_ref[...] = _rms_fwd(gated, g_ref[...]).astype(BF)

    return pl.pallas_call(
        body, name="ssd_fwd", grid=(NC,),
        in_specs=[_rows(CH, D), ANY, _whole((8, XBC)), _whole((8, DTP)), _whole((1, SI))] + [ANY] * ng,
        out_specs=[_rows(CH, SI), _rows(CH, SI), pl.BlockSpec((1, NH * HD, NS), lambda i: (i, 0, 0)), _rows(CH, XBC),
                   _rows(CH, SI), _rows(CH, XBC), _rows(CH, DTP)] + [ANY] * ng,
        out_shape=[S((T, SI), F32), S((T, SI), BF), S((NC, NH * HD, NS), F32), S((T, XBC), F32),
                   S((T, SI), F32), S((T, XBC), F32), S((T, DTP), F32)] + [S((4,) + s.shape, s.dtype) for s in shards],
        scratch_shapes=[pltpu.VMEM((CH + 8, XBC), F32), pltpu.VMEM((NH * HD, NS), F32),
                        pltpu.VMEM((D, PC - c0), BF), pltpu.SemaphoreType.DMA((1,)),
                        pltpu.SemaphoreType.DMA((6 * ng,)), pltpu.SemaphoreType.DMA((6 * ng,))],
        compiler_params=_params(),
    )(h0, wcat, cw_ssd, hp_ssd, g_ssd, *shards)


def _outproj(ylru, yssd, x, wout, g_pm, g_pf):
    T = x.shape[0]

    def body(yl_ref, ys_ref, x_ref, w_hbm, gpm_ref, gpf_ref, mix_ref, x1_ref, h2_ref, w_vm, sem):
        _load_once([(w_hbm, w_vm)], sem)
        mix = (jnp.dot(yl_ref[...], w_vm[0:LW, :], preferred_element_type=F32)
               + jnp.dot(ys_ref[...], w_vm[LW:LW + SI, :], preferred_element_type=F32))
        mix_ref[...] = mix
        x1 = x_ref[...] + _rms_fwd(mix, gpm_ref[...])
        x1_ref[...] = x1
        h2_ref[...] = _rms_fwd(x1, gpf_ref[...]).astype(BF)

    return pl.pallas_call(
        body, name="outproj", grid=(T // TT,),
        in_specs=[_rows(TT, LW), _rows(TT, SI), _rows(TT, D), ANY, _whole((1, D)), _whole((1, D))],
        out_specs=[_rows(TT, D), _rows(TT, D), _rows(TT, D)],
        out_shape=[S((T, D), F32), S((T, D), F32), S((T, D), BF)],
        scratch_shapes=[pltpu.VMEM((LW + SI, D), BF), pltpu.SemaphoreType.DMA((1,))],
        compiler_params=_params(),
    )(ylru, yssd, x, wout, g_pm, g_pf)


def _ffn_fwd(h2, x1, tgt, wg, wu, wd, g_pff):
    T = x1.shape[0]

    def body(h2_ref, x1_ref, t_ref, wg_hbm, wu_hbm, wd_hbm, g_ref,
             gate_ref, up_ref, act_ref, df_ref, dx2_ref, st_ref, wg_vm, wu_vm, wd_vm, sem):
        _load_once([(wg_hbm, wg_vm), (wu_hbm, wu_vm), (wd_hbm, wd_vm)], sem)

        @pl.when(pl.program_id(0) == 0)
        def _():
            st_ref[...] = jnp.zeros_like(st_ref)

        h2 = h2_ref[...]
        gate = jnp.dot(h2, wg_vm[...], preferred_element_type=F32)
        up = jnp.dot(h2, wu_vm[...], preferred_element_type=F32)
        gate_ref[...] = gate
        up_ref[...] = up
        act = (gate * _sigmoid(gate) * up).astype(BF)
        act_ref[...] = act
        f = jnp.dot(act, wd_vm[...], preferred_element_type=F32)
        g = g_ref[...]
        x2 = x1_ref[...] + _rms_fwd(f, g)
        err = x2 - t_ref[...]
        st_ref[0:1, :] += 0.5 * jnp.sum(err * err, axis=0, keepdims=True) * (1.0 / D)
        dx2 = err * (1.0 / D)
        dx2_ref[...] = dx2
        df, dg = _rms_bwd(f, g, dx2)
        df_ref[...] = df.astype(BF)
        st_ref[1:2, :] += dg

    return pl.pallas_call(
        body, name="ffn_fwd", grid=(T // TT,),
        in_specs=[_rows(TT, D), _rows(TT, D), _rows(TT, D), ANY, ANY, ANY, _whole((1, D))],
        out_specs=[_rows(TT, DFF), _rows(TT, DFF), _rows(TT, DFF), _rows(TT, D), _rows(TT, D), _whole((8, D))],
        out_shape=[S((T, DFF), F32), S((T, DFF), F32), S((T, DFF), BF), S((T, D), BF), S((T, D), F32), S((8, D), F32)],
        scratch_shapes=[pltpu.VMEM((D, DFF), BF), pltpu.VMEM((D, DFF), BF), pltpu.VMEM((DFF, D), BF),
                        pltpu.SemaphoreType.DMA((3,))],
        compiler_params=_params(),
    )(h2, x1, tgt, wg, wu, wd, g_pff)


def _ffn_bwd(df, gate, up, wdT, wgT, wuT):
    T = df.shape[0]

    def body(df_ref, gate_ref, up_ref, wd_hbm, wg_hbm, wu_hbm, dgate_ref, dup_ref, dh2_ref, wd_vm, wg_vm, wu_vm, sem):
        _load_once([(wd_hbm, wd_vm), (wg_hbm, wg_vm), (wu_hbm, wu_vm)], sem)
        dact = jnp.dot(df_ref[...], wd_vm[...], preferred_element_type=F32)
        gate = gate_ref[...]
        s = _sigmoid(gate)
        dup = (dact * (gate * s)).astype(BF)
        dgate = (dact * up_ref[...] * (s + gate * s * (1.0 - s))).astype(BF)
        dup_ref[...] = dup
        dgate_ref[...] = dgate
        dh2_ref[...] = (jnp.dot(dgate, wg_vm[...], preferred_element_type=F32)
                        + jnp.dot(dup, wu_vm[...], preferred_element_type=F32))

    return pl.pallas_call(
        body, name="ffn_bwd", grid=(T // TT,),
        in_specs=[_rows(TT, D), _rows(TT, DFF), _rows(TT, DFF), ANY, ANY, ANY],
        out_specs=[_rows(TT, DFF), _rows(TT, DFF), _rows(TT, D)],
        out_shape=[S((T, DFF), BF), S((T, DFF), BF), S((T, D), F32)],
        scratch_shapes=[pltpu.VMEM((D, DFF), BF), pltpu.VMEM((DFF, D), BF), pltpu.VMEM((DFF, D), BF),
                        pltpu.SemaphoreType.DMA((3,))],
        compiler_params=_params(),
    )(df, gate, up, wdT, wgT, wuT)


def _mix_bwd(dh2, x1, dx2, mix, woutT, g_pf, g_pm):
    T = x1.shape[0]

    def body(dh2_ref, x1_ref, dx2_ref, mix_ref, w_hbm, gpf_ref, gpm_ref,
             dx1_ref, dmix_ref, dyl_ref, dys_ref, st_ref, w_vm, sem):
        _load_once([(w_hbm, w_vm)], sem)

        @pl.when(pl.program_id(0) == 0)
        def _():
            st_ref[...] = jnp.zeros_like(st_ref)

        dxa, dgpf = _rms_bwd(x1_ref[...], gpf_ref[...], dh2_ref[...])
        dx1 = dx2_ref[...] + dxa
        dx1_ref[...] = dx1
        dmix, dgpm = _rms_bwd(mix_ref[...], gpm_ref[...], dx1)
        dmix = dmix.astype(BF)
        dmix_ref[...] = dmix
        st_ref[0:1, :] += dgpf
        st_ref[1:2, :] += dgpm
        dyl_ref[...] = jnp.dot(dmix, w_vm[:, 0:LW], preferred_element_type=F32)
        dys_ref[...] = jnp.dot(dmix, w_vm[:, LW:LW + SI], preferred_element_type=F32)

    return pl.pallas_call(
        body, name="mix_bwd", grid=(T // TT,),
        in_specs=[_rows(TT, D), _rows(TT, D), _rows(TT, D), _rows(TT, D), ANY, _whole((1, D)), _whole((1, D))],
        out_specs=[_rows(TT, D), _rows(TT, D), _rows(TT, LW), _rows(TT, SI), _whole((8, D))],
        out_shape=[S((T, D), F32), S((T, D), BF), S((T, LW), F32), S((T, SI), F32), S((8, D), F32)],
        scratch_shapes=[pltpu.VMEM((D, LW + SI), BF), pltpu.SemaphoreType.DMA((1,))],
        compiler_params=_params(),
    )(dh2, x1, dx2, mix, woutT, g_pf, g_pm)


def _halo(width, n_tiles, tile):
    per = tile // 8
    return pl.BlockSpec((8, width), lambda i: (jnp.maximum((n_tiles - 1 - i) * per - 1, 0), 0))


def _lru_bwd(dy, lxr, lxc, lg, h, p_lru, wa4, wx4, wa4T, wx4T, bufs):
    T = dy.shape[0]
    NT = T // TT
    nb = len(bufs)

    def body(*refs):
        dy_ref, lxr_ref, lxc_ref, lg_ref, h_ref, hh_ref, p_ref, wa_ref, wx_ref, waT_ref, wxT_ref = refs[:11]
        b_in = refs[11:11 + nb]
        dlx_ref, dlg_ref, st_ref, dwa_ref, dwx_ref = refs[11 + nb:16 + nb]
        b_out = refs[16 + nb:16 + 2 * nb]
        hp, dp, a_s, d_s, g_s, cc, send_sems, recv_sems = refs[16 + 2 * nb:]
        for phase, step in enumerate((0, NT - 1)):
            @pl.when(pl.program_id(0) == step)
            def _():
                _pair_phase(phase, b_in, b_out, send_sems, recv_sems)

        dy = dy_ref[...]
        first = pl.program_id(0) == 0
        top = pl.program_id(0) == NT - 1

        @pl.when(first)
        def _():
            st_ref[...] = jnp.zeros_like(st_ref)
            dwa_ref[...] = jnp.zeros_like(dwa_ref)
            dwx_ref[...] = jnp.zeros_like(dwx_ref)
            dp[TT:TT + 8, :] = jnp.zeros((8, LW), F32)
            cc[...] = jnp.zeros_like(cc)

        hp[0:8, :] = hh_ref[...] * jnp.where(top, 0.0, 1.0)
        hp[8:8 + TT, :] = h_ref[...]
        lx = lxc_ref[...]
        r, i, sp, a, mult = _lru_gates(lx, p_ref, wa_ref, wx_ref)

        lg = lg_ref[...]
        hcur = h_ref[...]
        ge = _gelu(lg)
        dgated, dgn = _rms_bwd(hcur * ge, p_ref[8:9, :], dy)
        st_ref[8:9, :] += dgn
        dlg_ref[...] = (dgated * hcur * _gelu_grad(lg)).astype(BF)
        a_s[...] = a
        d_s[...] = dgated * ge

        def step(k, c):
            t = TT - 1 - k
            g = d_s[pl.ds(t, 1), :] + c
            g_s[pl.ds(t, 1), :] = g
            return a_s[pl.ds(t, 1), :] * g

        cc[0:1, :] = lax.fori_loop(0, TT, step, cc[0:1, :], unroll=8)
        gt = g_s[...]
        da = gt * hp[pl.ds(7, TT), :]
        dmult = gt * i * lx
        di = gt * mult * lx
        dlxc = gt * mult * i
        dla = da * a - dmult * (a * a) / mult
        dr = dla * (-LRU_C * sp)
        st_ref[7:8, :] += jnp.sum(dla * (-LRU_C * r), axis=0, keepdims=True) * (-_sigmoid(-p_ref[7:8, :]))
        dzr = dr * r * (1.0 - r)
        dzi = di * i * (1.0 - i)
        st_ref[5:6, :] += jnp.sum(dzr, axis=0, keepdims=True)
        st_ref[6:7, :] += jnp.sum(dzi, axis=0, keepdims=True)
        dlxc = dlxc + _blockdiag_mm(dzr, waT_ref) + _blockdiag_mm(dzi, wxT_ref)
        for j in range(4):
            sl = slice(256 * j, 256 * (j + 1))
            pa = _mm_tn(lx[:, sl], dzr[:, sl])
            px = _mm_tn(lx[:, sl], dzi[:, sl])
            for b in range(4):
                bs = slice(BW * b, BW * (b + 1))
                dwa_ref[4 * j + b] += pa[bs, bs]
                dwx_ref[4 * j + b] += px[bs, bs]
        dlx_ref[...] = _conv_bwd(dp, dlxc, lxr_ref[...], p_ref, st_ref, TT).astype(BF)

    w4 = _whole((4, 256, 256))
    return pl.pallas_call(
        body, name="lru_bwd", grid=(NT,),
        in_specs=[_rows(TT, LW, NT), _rows(TT, LW, NT), _rows(TT, LW, NT), _rows(TT, LW, NT), _rows(TT, LW, NT),
                  _halo(LW, NT, TT), _whole((16, LW)), w4, w4, w4, w4] + [ANY] * nb,
        out_specs=[_rows(TT, LW, NT), _rows(TT, LW, NT), _whole((16, LW)), _whole((NBLK, BW, BW)), _whole((NBLK, BW, BW))]
        + [ANY] * nb,
        out_shape=[S((T, LW), BF), S((T, LW), BF), S((16, LW), F32), S((NBLK, BW, BW), F32), S((NBLK, BW, BW), F32)]
        + [S(_half_shape(b), b.dtype) for b in bufs],
        scratch_shapes=[pltpu.VMEM((TT + 8, LW), F32), pltpu.VMEM((TT + 8, LW), F32),
                        pltpu.VMEM((TT, LW), F32), pltpu.VMEM((TT, LW), F32), pltpu.VMEM((TT, LW), F32),
                        pltpu.VMEM((8, LW), F32), pltpu.SemaphoreType.DMA((nb,)), pltpu.SemaphoreType.DMA((nb,))],
        compiler_params=_params(),
    )(dy, lxr, lxc, lg, h, h, p_lru, wa4, wx4, wa4T, wx4T, *bufs)


def _ssd_bwd(dyn, xbcr, cv, z, dtr, y, states, cw_ssd, hp_ssd, g_ssd, parts):
    T = dyn.shape[0]
    NC = T // CH
    nq = len(parts)

    def body(*refs):
        dyn_ref, xr_ref, cv_ref, z_ref, dt_ref, y_ref, st_ref, cw_ref, hp_ref, g_ref = refs[:10]
        q_in = refs[10:10 + nq]
        dxbc_ref, dz_ref, ddt_ref, cst_ref, hst_ref, gst_ref = refs[10 + nq:16 + nq]
        q_out = refs[16 + nq:16 + 2 * nq]
        dp, dS, send_sems, recv_sems = refs[16 + 2 * nq:]
        dyn = dyn_ref[...]
        first = pl.program_id(0) == 0
        for phase, step in enumerate((0, NC - 1)):
            @pl.when(pl.program_id(0) == step)
            def _():
                _quad_phase(phase, q_in, q_out, send_sems, recv_sems)

        @pl.when(first)
        def _():
            cst_ref[...] = jnp.zeros_like(cst_ref)
            hst_ref[...] = jnp.zeros_like(hst_ref)
            gst_ref[...] = jnp.zeros_like(gst_ref)
            dp[CH:CH + 8, :] = jnp.zeros((8, XBC), F32)
            dS[...] = jnp.zeros_like(dS)

        cv = cv_ref[...]
        sg, xbc, raw, dtv, A, cs = _ssd_prep(cv, dt_ref, hp_ref)
        csT, dsm, E_x, ds_x, El_rows = _ssd_decays(cs)
        row_i = lax.broadcasted_iota(jnp.int32, (CH, CH), 0)
        col_i = lax.broadcasted_iota(jnp.int32, (CH, CH), 1)
        tril = row_i >= col_i
        first = col_i < HD
        head_of = ((lax.broadcasted_iota(jnp.int32, (DTP, SI), 1) >> 6)
                   == lax.broadcasted_iota(jnp.int32, (DTP, SI), 0)).astype(BF)
        head_ofT = ((lax.broadcasted_iota(jnp.int32, (SI, DTP), 0) >> 6)
                    == lax.broadcasted_iota(jnp.int32, (SI, DTP), 1)).astype(BF)

        def hi_lo(v):
            hi = v.astype(BF)
            return hi, (v - hi.astype(F32)).astype(BF)

        GW = HPG * HD

        def lane_sums(v, g):
            hi, lo = hi_lo(v)
            w = head_ofT[GW * g:GW * (g + 1), :]
            return _mm(hi, w) + _mm(lo, w)

        zz = z_ref[...]
        sz = _sigmoid(zz)
        yv = y_ref[...]
        dgn, dg = _rms_bwd(yv * (zz * sz), g_ref[...], dyn)
        gst_ref[0:1, :] += dg
        dz_ref[...] = (dgn * yv * (sz + zz * sz * (1.0 - sz))).astype(BF)
        dY = dgn * (zz * sz)

        X = xbc[:, 0:SI]
        dsilu = sg + cv * sg * (1.0 - sg)
        dt_x = _per_head_lanes(dtv)
        xs = X * dt_x
        xsd = (xs * ds_x).astype(BF)
        D_x = _per_head_lanes(hp_ref[...])[2:3, :]
        zero = jnp.zeros((CH, DTP), F32)
        dcs_col = zero
        dcs_row = zero
        dds, ddt_col, dD_rows, dcl_rows = zero, zero, zero, zero
        for g in range(NG):
            gs = slice(GW * g, GW * (g + 1))
            Bg = xbc[:, SI + NS * g:SI + NS * (g + 1)].astype(BF)
            Cg = xbc[:, SI + NG * NS + NS * g:SI + NG * NS + NS * (g + 1)].astype(BF)
            G = _mm_nt(Cg, Bg)
            Sg = st_ref[0, gs, :]
            dSe = dS[gs, :]
            dYg = dY[:, gs]
            dcs_col = dcs_col + lane_sums(dYg * (_mm_nt(Cg, Sg) * E_x[:, gs]), g)
            dD_rows = dD_rows + lane_sums(dYg * X[:, gs], g)
            dP = dYg * E_x[:, gs]
            dCg = _mm(dP, Sg)
            dS[gs, :] = _mm_tn(dP, Cg) + _per_head_rows(El_rows, g) * dSe
            t_hi, t_lo = hi_lo(dSe * Sg)
            dcl_rows = dcl_rows + _mm(head_of[:, gs], t_hi) + _mm(head_of[:, gs], t_lo)
            Q = _mm_nt(Bg, dSe)
            dds = dds + lane_sums(Q * xs[:, gs], g)
            dBg = _mm(xsd[:, gs], dSe)
            dG = jnp.zeros((CH, CH), F32)
            dxs_pairs = []
            for jj in range(HPG // 2):
                j = g * (HPG // 2) + jj
                ps = slice(2 * HD * j, 2 * HD * (j + 1))
                xs_pair = xs[:, ps]
                dxs_pair = Q[:, 2 * HD * jj:2 * HD * (jj + 1)] * ds_x[:, ps]
                for e in range(2):
                    h = 2 * j + e
                    Lm = jnp.exp(jnp.where(tril, cs[:, h:h + 1] - csT[h:h + 1, :], -1e30))
                    M = G * Lm
                    dYm = jnp.where(first if e == 0 else ~first, dY[:, ps], 0.0).astype(BF)
                    dM = _mm_nt(dYm, xs_pair)
                    dxs_pair = dxs_pair + _mm_tn(M, dYm)
                    Wm = dM * M
                    dcs_col = dcs_col + jnp.where(col_i == h, jnp.sum(Wm, axis=1, keepdims=True), 0.0)
                    dcs_row = dcs_row + jnp.where(row_i == h, -jnp.sum(Wm, axis=0, keepdims=True), 0.0)
                    dG = dG + dM * Lm
                dp[0:CH, ps] = (D_x[:, ps] * dY[:, ps] + dxs_pair * dt_x[:, ps]) * dsilu[:, ps]
                dxs_pairs.append(dxs_pair)
            ddt_col = ddt_col + lane_sums(jnp.concatenate(dxs_pairs, axis=1) * X[:, gs], g)
            bs = slice(SI + NS * g, SI + NS * (g + 1))
            cs_ = slice(SI + NG * NS + NS * g, SI + NG * NS + NS * (g + 1))
            dp[0:CH, bs] = (dBg + _mm_tn(dG, Cg)) * dsilu[:, bs]
            dp[0:CH, cs_] = (dCg + _mm(dG, Bg)) * dsilu[:, cs_]

        dds = dds * dsm
        dcs_col = dcs_col - dds
        dD = jnp.sum(dD_rows, axis=0, keepdims=True)
        dcl_rows = jnp.sum(dcl_rows, axis=1, keepdims=True) * jnp.exp(csT[:, CH - 1:CH])
        dcs_row = dcs_row + jnp.where(col_i == CH - 1, dcl_rows, 0.0)
        dcs_col = dcs_col + jnp.where(row_i == CH - 1, jnp.sum(dds, axis=0, keepdims=True), 0.0)

        da = _rev_cumsum_rows(dcs_col + dcs_row.T, CH)
        ddt_col = ddt_col + da * A
        hst_ref[1:2, :] += jnp.sum(da * dtv, axis=0, keepdims=True) * A
        hst_ref[2:3, :] += dD
        draw = jnp.where(col_i < NH, ddt_col * _sigmoid(raw), 0.0)
        ddt_ref[...] = draw.astype(BF)
        hst_ref[0:1, :] += jnp.sum(draw, axis=0, keepdims=True)

        dxbc_ref[...] = _conv_bwd(dp, None, xr_ref[...], cw_ref, cst_ref, CH).astype(BF)

    return pl.pallas_call(
        body, name="ssd_bwd", grid=(NC,),
        in_specs=[_rows(CH, SI, NC), _rows(CH, XBC, NC), _rows(CH, XBC, NC), _rows(CH, SI, NC), _rows(CH, DTP, NC),
                  _rows(CH, SI, NC), pl.BlockSpec((1, NH * HD, NS), lambda i: (NC - 1 - i, 0, 0)),
                  _whole((8, XBC)), _whole((8, DTP)), _whole((1, SI))] + [ANY] * nq,
        out_specs=[_rows(CH, XBC, NC), _rows(CH, SI, NC), _rows(CH, DTP, NC), _whole((16, XBC)), _whole((16, DTP)),
                   _whole((8, SI))] + [ANY] * nq,
        out_shape=[S((T, XBC), BF), S((T, SI), BF), S((T, DTP), BF), S((16, XBC), F32), S((16, DTP), F32), S((8, SI), F32)]
        + [S(p.shape, p.dtype) for p in parts],
        scratch_shapes=[pltpu.VMEM((CH + 8, XBC), F32), pltpu.VMEM((NH * HD, NS), F32),
                        pltpu.SemaphoreType.DMA((3 * nq,)), pltpu.SemaphoreType.DMA((3 * nq,))],
        compiler_params=_params(),
    )(dyn, xbcr, cv, z, dtr, y, states, cw_ssd, hp_ssd, g_ssd, *parts)


def _inproj_bwd(dlx, dlg, dz, dxbc, ddt, x, dx1, wcatT, g0, parts):
    T = x.shape[0]
    NT = T // TT
    nq = len(parts)

    def body(*refs):
        dlx_ref, dlg_ref, dz_ref, dxbc_ref, ddt_ref, x_ref, dx1_ref, w_hbm, g_ref = refs[:9]
        q_in = refs[9:9 + nq]
        dx_ref, st_ref = refs[9 + nq:11 + nq]
        q_out = refs[11 + nq:11 + 2 * nq]
        w_vm, sem, send_sems, recv_sems = refs[11 + 2 * nq:]
        _load_once([(w_hbm, w_vm)], sem)
        for phase, step in enumerate((0, NT - 1)):
            @pl.when(pl.program_id(0) == step)
            def _():
                _quad_phase(phase, q_in, q_out, send_sems, recv_sems)

        @pl.when(pl.program_id(0) == 0)
        def _():
            st_ref[...] = jnp.zeros_like(st_ref)

        dh = jnp.dot(dlx_ref[...], w_vm[0:1024, :], preferred_element_type=F32)
        dh = dh + jnp.dot(dlg_ref[...], w_vm[1024:2048, :], preferred_element_type=F32)
        dh = dh + jnp.dot(dz_ref[...], w_vm[2048:3072, :], preferred_element_type=F32)
        dh = dh + jnp.dot(dxbc_ref[...], w_vm[3072:3072 + XBC, :], preferred_element_type=F32)
        dh = dh + jnp.dot(ddt_ref[...], w_vm[3072 + XBC:PC, :], preferred_element_type=F32)
        dx, dg = _rms_bwd(x_ref[...], g_ref[...], dh)
        dx_ref[...] = dx1_ref[...] + dx
        st_ref[0:1, :] += dg

    return pl.pallas_call(
        body, name="inproj_bwd", grid=(NT,),
        in_specs=[_rows(TT, 1024), _rows(TT, 1024), _rows(TT, 1024), _rows(TT, XBC), _rows(TT, DTP), _rows(TT, D),
                  _rows(TT, D), ANY, _whole((1, D))] + [ANY] * nq,
        out_specs=[_rows(TT, D), _whole((8, D))] + [ANY] * nq,
        out_shape=[S((T, D), F32), S((8, D), F32)] + [S(p.shape, p.dtype) for p in parts],
        scratch_shapes=[pltpu.VMEM((PC, D), BF), pltpu.SemaphoreType.DMA((1,)),
                        pltpu.SemaphoreType.DMA((3 * nq,)), pltpu.SemaphoreType.DMA((3 * nq,))],
        compiler_params=_params(),
    )(dlx, dlg, dz, dxbc, ddt, x, dx1, wcatT, g0, *parts)


def _wgrad(name, a, b):
    T, M = a.shape
    N = b.shape[1]
    tk = min(T, 2048 if M <= 1024 else 1024)
    tn = N
    while M * tn * 4 > (6 << 20) and tn % 256 == 0:
        tn //= 2

    def body(a_ref, b_ref, o_ref):
        p = lax.dot_general(a_ref[...], b_ref[...], (((0,), (0,)), ((), ())), preferred_element_type=F32)

        @pl.when(pl.program_id(1) == 0)
        def _():
            o_ref[...] = p

        @pl.when(pl.program_id(1) > 0)
        def _():
            o_ref[...] += p

    return pl.pallas_call(
        body, name=name, grid=(N // tn, T // tk),
        in_specs=[pl.BlockSpec((tk, M), lambda j, k: (k, 0)), pl.BlockSpec((tk, tn), lambda j, k: (k, j))],
        out_specs=pl.BlockSpec((M, tn), lambda j, k: (0, j)), out_shape=S((M, N), F32),
        compiler_params=_params(2),
    )(a, b)


def _adamw(name, w, g, m, v):
    _, R, C = w.shape

    def body(w_ref, g_ref, m_ref, v_ref, d_ref, nm_ref, nv_ref):
        d_ref[0], nm_ref[0], nv_ref[0] = _adam_math(w_ref[0], g_ref[...], m_ref[0], v_ref[0])

    if R % 8 == 0:
        tr = _row_tile(R, C)
        n_tiles = R // tr
        blk, gblk = pl.BlockSpec((1, tr, C), lambda i: (0, i, 0)), pl.BlockSpec((tr, C), lambda i: (i, 0))
    else:
        tc = 128 * max(k for k in range(1, C // 128 + 1) if C % (128 * k) == 0 and R * 128 * k * 4 <= (5 << 18))
        n_tiles = C // tc
        blk, gblk = pl.BlockSpec((1, R, tc), lambda i: (0, 0, i)), pl.BlockSpec((R, tc), lambda i: (0, i))
    return pl.pallas_call(
        body, name=name, grid=(n_tiles,),
        in_specs=[blk, gblk, blk, blk], out_specs=[blk] * 3,
        out_shape=[S((1, R, C), F32)] * 3, compiler_params=_params(),
    )(w, g, m, v)


def _pair_exchange(name, bufs):
    n = len(bufs)

    def body(*refs):
        for phase in range(2):
            _pair_phase(phase, refs[:n], refs[n:2 * n], refs[2 * n], refs[2 * n + 1])

    return pl.pallas_call(
        body, name=name, in_specs=[ANY] * n, out_specs=[ANY] * n,
        out_shape=[S(_half_shape(b), b.dtype) for b in bufs],
        scratch_shapes=[pltpu.SemaphoreType.DMA((n,)), pltpu.SemaphoreType.DMA((n,))],
    )(*bufs)


def _quad_exchange(bufs):
    n = len(bufs)

    def body(*refs):
        ins, outs = refs[:n], refs[n:2 * n]
        send_sems, recv_sems = refs[2 * n], refs[2 * n + 1]
        x, y, c = _pos()
        me = 2 * x + y
        chips = _other_chips(x, y)
        copies = []
        for k, (src, dst) in enumerate(zip(ins, outs)):
            for j, (cx, cy) in enumerate(chips):
                cp = _remote(src, dst.at[me], send_sems.at[3 * k + j], recv_sems.at[3 * k + j], (cx, cy, c))
                cp.start()
                copies.append(cp)
        for k, (src, dst) in enumerate(zip(ins, outs)):
            for j, (cx, cy) in enumerate(chips):
                blk = dst.at[2 * cx + cy]
                _remote(blk, blk, send_sems.at[3 * k + j], recv_sems.at[3 * k + j], (cx, cy, c)).wait_recv()
        for cp in copies:
            cp.wait_send()

    return pl.pallas_call(
        body, name="quad_exchange", in_specs=[ANY] * n, out_specs=[ANY] * n,
        out_shape=[S((4,) + b.shape, b.dtype) for b in bufs],
        scratch_shapes=[pltpu.SemaphoreType.DMA((3 * n,)), pltpu.SemaphoreType.DMA((3 * n,))],
    )(*bufs)


def _pair_gather(bufs):
    n = len(bufs)

    def body(*refs):
        ins, outs = refs[:n], refs[n:2 * n]
        send_sems, recv_sems = refs[2 * n], refs[2 * n + 1]
        x, y, c = _pos()
        copies = []
        for k, buf in enumerate(outs):
            mine = _half(buf, c, buf.shape[0] // 2)
            cp = _remote(mine, mine, send_sems.at[k], recv_sems.at[k], (x, y, 1 - c))
            cp.start()
            copies.append(cp)
        for k, buf in enumerate(outs):
            theirs = _half(buf, 1 - c, buf.shape[0] // 2)
            _remote(theirs, theirs, send_sems.at[k], recv_sems.at[k], (x, y, 1 - c)).wait_recv()
        for cp in copies:
            cp.wait_send()

    return pl.pallas_call(
        body, name="pair_gather", in_specs=[ANY] * n, out_specs=[ANY] * n,
        out_shape=[S(b.shape, b.dtype) for b in bufs], input_output_aliases={k: k for k in range(n)},
        scratch_shapes=[pltpu.SemaphoreType.DMA((n,)), pltpu.SemaphoreType.DMA((n,))],
    )(*bufs)


def _row_tile(rows, cols, mult=8):
    best = mult
    for t in range(mult, rows + 1, mult):
        if rows % t == 0 and t * cols * 4 <= (1 << 20):
            best = t
    return best


def _add_own_half(name, full, got, c, out_dtype, by_columns):
    hr = got.shape[-2]
    wide = got.shape[-1]
    cols = wide // 4 if by_columns else wide
    tr = _row_tile(hr, wide, 16)
    per = hr // tr

    if by_columns:
        def body(c_ref, a_ref, b_ref, o_ref):
            v = a_ref[...] + b_ref[...]
            for j in range(4):
                o_ref[j] = v[:, j * cols:(j + 1) * cols].astype(out_dtype)

        in_specs = [pl.BlockSpec((tr, wide), lambda i, c_ref: (c_ref[0] * per + i, 0)),
                    pl.BlockSpec((tr, wide), lambda i, c_ref: (i, 0))]
        out_specs = pl.BlockSpec((4, tr, cols), lambda i, c_ref: (0, i, 0))
        grid = (per,)
    else:
        def body(c_ref, a_ref, b_ref, o_ref):
            o_ref[...] = (a_ref[...] + b_ref[...]).astype(out_dtype)

        in_specs = [pl.BlockSpec((1, tr, cols), lambda s, i, c_ref: (s, c_ref[0] * per + i, 0)),
                    pl.BlockSpec((1, tr, cols), lambda s, i, c_ref: (s, i, 0))]
        out_specs = pl.BlockSpec((1, tr, cols), lambda s, i, c_ref: (s, i, 0))
        grid = (4, per)
    return pl.pallas_call(
        body, name=name,
        grid_spec=pltpu.PrefetchScalarGridSpec(num_scalar_prefetch=1, grid=grid, in_specs=in_specs, out_specs=out_specs),
        out_shape=S((4, hr, cols), out_dtype), compiler_params=_params(len(grid)),
    )(jnp.reshape(c, (1,)).astype(jnp.int32), full, got)


def _small_add_own_half(fulls, gots, c):
    n = len(fulls)

    def body(c_ref, *refs):
        for a_ref, b_ref, o_ref in zip(refs[:n], refs[n:2 * n], refs[2 * n:]):
            hr = b_ref.shape[0]
            o_ref[...] = a_ref[pl.ds(pl.multiple_of(c_ref[0] * hr, 8), hr), :] + b_ref[...]

    specs = lambda arrs: [pl.BlockSpec(a.shape, lambda i, c_ref: (0, 0)) for a in arrs]
    return pl.pallas_call(
        body, name="small_pair_add",
        grid_spec=pltpu.PrefetchScalarGridSpec(num_scalar_prefetch=1, grid=(1,), in_specs=specs(fulls) + specs(gots),
                                               out_specs=specs(gots)),
        out_shape=[S(g.shape, F32) for g in gots], compiler_params=_params(),
    )(jnp.reshape(c, (1,)).astype(jnp.int32), *fulls, *gots)


def _small_sum_slots(own, slots, me, c):
    n = len(slots)

    def body(p_ref, *refs):
        own_refs, slot_refs, o_refs = refs[:n], refs[n:5 * n], refs[5 * n:]
        for i, (own_ref, o_ref) in enumerate(zip(own_refs, o_refs)):
            hr = own_ref.shape[0]
            acc = None
            for j in range(4):
                v = jnp.where(p_ref[0] == j, own_ref[...], slot_refs[4 * i + j][0])
                acc = v if acc is None else acc + v
            o_ref[pl.ds(pl.multiple_of(p_ref[1] * hr, 8), hr), :] = acc

    def slot_spec(s, j):
        return pl.BlockSpec((1,) + s.shape[1:], lambda i, p: (jnp.where(p[0] == j, (j + 1) % 4, j), 0, 0))

    outs = [S((2 * s.shape[1], s.shape[2]), F32) for s in slots]
    return pl.pallas_call(
        body, name="small_quad_sum",
        grid_spec=pltpu.PrefetchScalarGridSpec(
            num_scalar_prefetch=1, grid=(1,),
            in_specs=[pl.BlockSpec(o.shape, lambda i, p: (0, 0)) for o in own]
            + [slot_spec(s, j) for s in slots for j in range(4)],
            out_specs=[pl.BlockSpec(o.shape, lambda i, p: (0, 0)) for o in outs]),
        out_shape=outs, compiler_params=_params(),
    )(jnp.stack([me, c]).astype(jnp.int32), *own, *[s for s in slots for _ in range(4)])


def _sum_slots(name, own, slots, me, c):
    _, rows, cols = slots.shape
    tr = _row_tile(rows, cols, 16 if slots.dtype == jnp.bfloat16 else 8)
    per = rows // tr
    three = len(own.shape) == 3

    def body(p_ref, own_ref, s0, s1, s2, s3, o_ref):
        mine = own_ref[0] if three else own_ref[...]
        acc = None
        for j, s_ref in enumerate((s0, s1, s2, s3)):
            v = jnp.where(p_ref[0] == j, mine, s_ref[0]).astype(F32)
            acc = v if acc is None else acc + v
        o_ref[...] = acc

    def slot_spec(j):
        return pl.BlockSpec((1, tr, cols), lambda i, p: (jnp.where(p[0] == j, (j + 1) % 4, j), i, 0))

    own_spec = (pl.BlockSpec((1, tr, cols), lambda i, p: (p[0], i, 0)) if three
                else pl.BlockSpec((tr, cols), lambda i, p: (i, 0)))
    return pl.pallas_call(
        body, name=name,
        grid_spec=pltpu.PrefetchScalarGridSpec(
            num_scalar_prefetch=1, grid=(per,), in_specs=[own_spec] + [slot_spec(j) for j in range(4)],
            out_specs=pl.BlockSpec((tr, cols), lambda i, p: (p[1] * per + i, 0))),
        out_shape=S((2 * rows, cols), F32), compiler_params=_params(),
    )(jnp.stack([me, c]).astype(jnp.int32), own, slots, slots, slots, slots)


BIG = ("w_in", "w_out", "w_gate", "w_up", "w_down")
ROW_PARAMS = (("pre_mix_norm", 0), ("lru_conv_b", 12), ("lru_ba", 13), ("lru_bx", 14), ("lru_lambda", 15),
              ("lru_out_norm", 16), ("ssd_out_norm", 24), ("post_mix_norm", 33), ("pre_ffn_norm", 32), ("post_ffn_norm", 41))
LRU_CONV_ROWS = (8, 12)
LOSS_ROW = 40
HEAD_PARAMS = (("ssd_dt_bias", 0), ("ssd_a_log", 1), ("ssd_d", 2))
SMALL = tuple(n for n, _ in ROW_PARAMS) + ("ssd_conv_b",) + tuple(n for n, _ in HEAD_PARAMS) + (
    "lru_wa", "lru_wx", "lru_conv_w", "ssd_conv_w")


def _diag4(w):
    eye = jnp.eye(4, dtype=w.dtype).reshape(1, 4, 1, 4, 1)
    return (w.reshape(4, 4, BW, 1, BW) * eye).reshape(4, 4 * BW, 4 * BW)


def _adam_math(w, g, m, v):
    mm = ADAM_B1 * m + (1.0 - ADAM_B1) * g
    vv = ADAM_B2 * v + (1.0 - ADAM_B2) * (g * g)
    c1 = 1.0 - ADAM_B1 ** ADAM_STEP
    c2 = 1.0 - ADAM_B2 ** ADAM_STEP
    return -ADAM_LR * ((mm / c1) / (jnp.sqrt(vv / c2) + ADAM_EPS) + ADAM_WD * w), mm, vv


def _adamw_small(rows, cst, hst, dwa, dwx, glcw, gscw, w, m, v):
    def grad_of(name, refs):
        rows_ref, cst_ref, hst_ref, dwa_ref, dwx_ref, glcw_ref, gscw_ref = refs
        for n, r in ROW_PARAMS:
            if n == name:
                return rows_ref[r:r + 1, :]
        for n, r in HEAD_PARAMS:
            if n == name:
                return hst_ref[r:r + 1, 0:NH]
        return {"ssd_conv_b": lambda: cst_ref[4:5, :], "lru_wa": lambda: dwa_ref[...], "lru_wx": lambda: dwx_ref[...],
                "lru_conv_w": lambda: glcw_ref[...], "ssd_conv_w": lambda: gscw_ref[...]}[name]()

    shapes = {n: (w[n].shape[1:] if len(w[n].shape) > 2 else w[n].shape) for n in SMALL}
    flat = lambda d: [d[n].reshape(shapes[n]) for n in SMALL]
    ns = len(SMALL)

    def body(*refs):
        srcs, rest = refs[:7], refs[7:]
        w_refs, m_refs, v_refs = rest[:ns], rest[ns:2 * ns], rest[2 * ns:3 * ns]
        outs = rest[3 * ns:]
        for k, name in enumerate(SMALL):
            g = grad_of(name, srcs)
            d, mm, vv = _adam_math(w_refs[k][...], g, m_refs[k][...], v_refs[k][...])
            outs[4 * k][...] = g
            outs[4 * k + 1][...] = d
            outs[4 * k + 2][...] = mm
            outs[4 * k + 3][...] = vv

    res = pl.pallas_call(
        body, name="adamw_small",
        out_shape=[S(shapes[n], F32) for n in SMALL for _ in range(4)],
        compiler_params=pltpu.CompilerParams(vmem_limit_bytes=VMEM_LIMIT),
    )(rows, cst, hst, dwa, dwx, glcw, gscw, *flat(w), *flat(m), *flat(v))
    return {n: tuple(res[4 * k + i].reshape(w[n].shape) for i in range(4)) for k, n in enumerate(SMALL)}


def _with_own(own, got):
    chip = 2 * lax.axis_index("x") + lax.axis_index("y")
    return jnp.where((jnp.arange(4) == chip).reshape(4, 1, 1), own[None], got)


def _side_by_side(f):
    return f.transpose(1, 0, 2).reshape(f.shape[1], 4 * f.shape[2])


def _stacked(f):
    return f.reshape(4 * f.shape[1], f.shape[2])


def _conv_terms(lru_conv_w, ssd_conv_w):
    conv = jnp.concatenate([lru_conv_w.reshape(-1), ssd_conv_w.reshape(-1)]).astype(F32)
    hi = conv.astype(jnp.bfloat16)
    mid = (conv - hi.astype(F32)).astype(jnp.bfloat16)
    lo = (conv - hi.astype(F32) - mid.astype(F32)).astype(jnp.bfloat16)
    terms = jnp.concatenate([hi, mid, lo])
    rows = -(-terms.shape[0] // (128 * 32)) * 32
    return jnp.pad(terms, (0, rows * 128 - terms.shape[0])).reshape(rows, 128)


def _full_conv_taps(own, got, n_lru, n_ssd):
    n_terms = 3 * (n_lru + n_ssd)
    t3 = _with_own(own, got).reshape(4, -1)[:, :n_terms].reshape(4, 3, -1).astype(F32)
    conv_f = (t3[:, 0] + t3[:, 1]) + t3[:, 2]
    lcw = conv_f[:, :n_lru].reshape(4, CONV_K, -1).transpose(1, 0, 2).reshape(CONV_K, LW)
    scw = conv_f[:, n_lru:].reshape(4, CONV_K, -1).transpose(1, 0, 2).reshape(CONV_K, XBC)
    return lcw, scw


def _step(x, tgt, w_in, lru_conv_w, ssd_conv_w, sp, late):
    c = lax.axis_index("c")
    me = 2 * lax.axis_index("x") + lax.axis_index("y")
    mm = lambda w: w.astype(BF)
    row = lambda v: v.reshape(1, -1).astype(F32)
    g0 = row(sp["pre_mix_norm"])
    first = [w_in.astype(WIRE), _conv_terms(lru_conv_w, ssd_conv_w)]
    h0, *got_first = _prenorm(x, g0, first)
    win_f = _side_by_side(_with_own(first[0], got_first[0]))
    lcw, scw = _full_conv_taps(first[1], got_first[1], lru_conv_w.size, ssd_conv_w.size)
    wcat = jnp.concatenate([mm(win_f), jnp.zeros((D, PC - IN_COLS), BF)], axis=1)
    p_lru = jnp.concatenate([lcw, row(sp["lru_conv_b"]), row(sp["lru_ba"]), row(sp["lru_bx"]), row(sp["lru_lambda"]),
                             row(sp["lru_out_norm"]), jnp.zeros((7, LW), F32)], axis=0)
    wa4, wx4 = mm(_diag4(sp["lru_wa"][0])), mm(_diag4(sp["lru_wx"][0]))
    wa4T, wx4T = wa4.transpose(0, 2, 1), wx4.transpose(0, 2, 1)
    cw_ssd = jnp.concatenate([scw, row(sp["ssd_conv_b"]), jnp.zeros((3, XBC), F32)], axis=0)
    padh = lambda v: jnp.pad(row(v), ((0, 0), (0, DTP - NH)))
    hp_ssd = jnp.concatenate([padh(sp["ssd_dt_bias"]), padh(sp["ssd_a_log"]), padh(sp["ssd_d"]), jnp.zeros((5, DTP), F32)], axis=0)
    g_ssd = row(sp["ssd_out_norm"])
    g_pm, g_pf, g_pff = row(sp["post_mix_norm"]), row(sp["pre_ffn_norm"]), row(sp["post_ffn_norm"])

    h, ylru, lxc, lxr, lg, *got_a = _lru_fwd(h0, wcat, p_lru, wa4, wx4, [late[0], late[3]])
    y, yssd, states, cv, z, xbcr, dtr, *got_b = _ssd_fwd(h0, wcat, cw_ssd, hp_ssd, g_ssd, [late[1], late[2]])
    wout, wd = mm(_stacked(_with_own(late[0], got_a[0]))), mm(_stacked(_with_own(late[3], got_a[1])))
    wg, wu = mm(_side_by_side(_with_own(late[1], got_b[0]))), mm(_side_by_side(_with_own(late[2], got_b[1])))
    mix, x1, h2 = _outproj(ylru, yssd, x, wout, g_pm, g_pf)
    gate, up, act, df, dx2, st_ffn = _ffn_fwd(h2, x1, tgt, wg, wu, wd, g_pff)
    dgate, dup, dh2 = _ffn_bwd(df, gate, up, wd.T, wg.T, wu.T)
    dx1, dmix, dyl, dys, st_mix = _mix_bwd(dh2, x1, dx2, mix, wout.T, g_pf, g_pm)

    dwg = _wgrad("wgrad_gate", h2, dgate)
    dwu = _wgrad("wgrad_up", h2, dup)
    dwd = _wgrad("wgrad_down", act, df)
    dwo = jnp.concatenate([_wgrad("wgrad_out_lru", ylru, dmix), _wgrad("wgrad_out_ssd", yssd, dmix)], axis=0)
    early = [dwo.reshape(4, (LW + SI) // 4, D), dwg, dwu, dwd.reshape(4, DFF // 4, D)]
    dlx, dlg, st_lru, dwa, dwx, *got_early = _lru_bwd(dyl, lxr, lxc, lg, h, p_lru, wa4, wx4, wa4T, wx4T, early)
    part_early = [_add_own_half("pair_add_early%d" % k, b, r, c, WIRE, bc)
                  for k, (b, r, bc) in enumerate(zip(early, got_early, [False, True, True, False]))]
    dxbc, dz, ddt, cst, hst, gst, *slots_early = _ssd_bwd(dys, xbcr, cv, z, dtr, y, states, cw_ssd, hp_ssd, g_ssd,
                                                          part_early)
    red_early = [_sum_slots("quad_sum_early%d" % k, p, s, me, c) for k, (p, s) in enumerate(zip(part_early, slots_early))]

    pin = [_wgrad("wgrad_in_%d" % k, h0, b) for k, b in enumerate((dlx, dlg, dz, dxbc, ddt))]
    dwin = jnp.concatenate(pin[:4] + [pin[4][:, :NH]], axis=1)
    (got_win,) = _pair_exchange("pair_exchange_w_in", [dwin])
    part_win = _add_own_half("pair_add_w_in", dwin, got_win, c, WIRE, True)
    gx, st_in, slots_win = _inproj_bwd(dlx, dlg, dz, dxbc, ddt, x, dx1, wcat.T, g0, [part_win])
    red_win = _sum_slots("quad_sum_w_in", part_win, slots_win, me, c)

    rows = jnp.concatenate([st_in, st_lru, gst, st_mix, st_ffn], axis=0)
    small = [rows, cst, hst, dwa.reshape(NBLK * BW, BW), dwx.reshape(NBLK * BW, BW)]
    part_small = list(_small_add_own_half(small, list(_pair_exchange("pair_exchange_small", small)), c))
    red_small = list(_small_sum_slots(part_small, list(_quad_exchange(part_small)), me, c))
    out = list(_pair_gather([red_win] + red_early + red_small))
    big = dict(zip(("w_in", "w_out", "w_gate", "w_up", "w_down"), out[:5]))
    return gx, big, out[5:]


def kernel(x, pre_mix_norm, w_in, lru_conv_w, lru_conv_b, lru_wa, lru_ba, lru_wx, lru_bx, lru_lambda, lru_out_norm, ssd_conv_w, ssd_conv_b, ssd_dt_bias, ssd_a_log, ssd_d, ssd_out_norm, w_out, post_mix_norm, pre_ffn_norm, w_gate, w_up, w_down, post_ffn_norm, loss_target, m_pre_mix_norm, m_w_in, m_lru_conv_w, m_lru_conv_b, m_lru_wa, m_lru_ba, m_lru_wx, m_lru_bx, m_lru_lambda, m_lru_out_norm, m_ssd_conv_w, m_ssd_conv_b, m_ssd_dt_bias, m_ssd_a_log, m_ssd_d, m_ssd_out_norm, m_w_out, m_post_mix_norm, m_pre_ffn_norm, m_w_gate, m_w_up, m_w_down, m_post_ffn_norm, v_pre_mix_norm, v_w_in, v_lru_conv_w, v_lru_conv_b, v_lru_wa, v_lru_ba, v_lru_wx, v_lru_bx, v_lru_lambda, v_lru_out_norm, v_ssd_conv_w, v_ssd_conv_b, v_ssd_dt_bias, v_ssd_a_log, v_ssd_d, v_ssd_out_norm, v_w_out, v_post_mix_norm, v_pre_ffn_norm, v_w_gate, v_w_up, v_w_down, v_post_ffn_norm):
    args = dict(locals())
    names = list(SMALL) + list(BIG)
    w = {n: args[n] for n in names}
    m = {n: args["m_" + n] for n in names}
    v = {n: args["v_" + n] for n in names}
    chip = 2 * lax.axis_index("x") + lax.axis_index("y")

    late = [a[0].astype(WIRE) for a in (w_out, w_gate, w_up, w_down)]
    gx, red, (rows, cst, hst, dwa, dwx) = _step(x[0], loss_target[0], w_in[0], lru_conv_w[0], ssd_conv_w[0],
                                                {n: w[n] for n in SMALL}, late)
    loss = jnp.sum(rows[LOSS_ROW])

    grads, delta, new_m, new_v = {}, {}, {}, {}
    for n in BIG:
        g = red[n]
        if n in ("w_in", "w_gate", "w_up"):
            t = lambda a: jnp.swapaxes(a, 1, 2)
            gt = g.T
            out = _adamw("adamw_" + n, t(w[n]), gt, t(m[n]), t(v[n]))
            delta[n], new_m[n], new_v[n] = (t(o) for o in out)
            grads[n] = t(gt[None])
        else:
            delta[n], new_m[n], new_v[n] = _adamw("adamw_" + n, w[n], g, m[n], v[n])
            grads[n] = g[None]

    lc, sc = lru_conv_w.shape[-1], ssd_conv_w.shape[-1]
    glcw = lax.dynamic_slice_in_dim(rows[LRU_CONV_ROWS[0]:LRU_CONV_ROWS[1]], chip * lc, lc, axis=1)
    gscw = lax.dynamic_slice_in_dim(cst[0:CONV_K], chip * sc, sc, axis=1)
    res = _adamw_small(rows, cst, hst, dwa.reshape(NBLK, BW, BW), dwx.reshape(NBLK, BW, BW), glcw, gscw,
                       {n: w[n] for n in SMALL}, {n: m[n] for n in SMALL}, {n: v[n] for n in SMALL})
    for n in SMALL:
        grads[n], delta[n], new_m[n], new_v[n] = res[n]

    order = ["pre_mix_norm", "w_in", "lru_conv_w", "lru_conv_b", "lru_wa", "lru_ba", "lru_wx", "lru_bx", "lru_lambda",
             "lru_out_norm", "ssd_conv_w", "ssd_conv_b", "ssd_dt_bias", "ssd_a_log", "ssd_d", "ssd_out_norm", "w_out",
             "post_mix_norm", "pre_ffn_norm", "w_gate", "w_up", "w_down", "post_ffn_norm"]
    return (loss, gx[None], *[grads[n] for n in order], *[delta[n] for n in order],
            *[new_m[n] for n in order], *[new_v[n] for n in order])
```

```python
import jax
import jax.numpy as jnp
from jax import lax
from jax.experimental import pallas as pl
from jax.experimental.pallas import tpu as pltpu

F32 = jnp.float32
BF = jnp.bfloat16

D = 1024
LW = 1024
NBLK = 16
BW = 64
SI = 1024
NH = 16
HD = 64
NG = 2
HPG = NH // NG
NS = 128
CH = 128
XBC = SI + 2 * NG * NS
DTP = 128
PC = 3 * 1024 + XBC + DTP
DFF = 2816
IN_COLS = 4624
EPS = 1e-6
LRU_C = 8.0
CONV_K = 4
TT = 256
VMEM_LIMIT = 56 * 1024 * 1024

ADAM_LR, ADAM_B1, ADAM_B2, ADAM_EPS, ADAM_WD, ADAM_STEP = 0.001, 0.9, 0.999, 1e-08, 0.01, 10

MESH = pl.DeviceIdType.MESH


def _mm(a, b):
    return jnp.dot(a.astype(BF), b.astype(BF), preferred_element_type=F32)


def _mm_nt(a, b):
    return lax.dot_general(a.astype(BF), b.astype(BF), (((1,), (1,)), ((), ())), preferred_element_type=F32)


def _mm_tn(a, b):
    return lax.dot_general(a.astype(BF), b.astype(BF), (((0,), (0,)), ((), ())), preferred_element_type=F32)


def _sigmoid(x):
    return 0.5 * jnp.tanh(0.5 * x) + 0.5


def _softplus(x):
    return jnp.maximum(x, 0.0) + jnp.log1p(jnp.exp(-jnp.abs(x)))


_GELU_C = 0.7978845608028654
_GELU_K = 0.044715


def _gelu(x):
    t = jnp.tanh(_GELU_C * (x + _GELU_K * x * x * x))
    return 0.5 * x * (1.0 + t)


def _gelu_grad(x):
    t = jnp.tanh(_GELU_C * (x + _GELU_K * x * x * x))
    return 0.5 * (1.0 + t) + 0.5 * x * (1.0 - t * t) * _GELU_C * (1.0 + 3.0 * _GELU_K * x * x)


def _rms_fwd(x, g):
    r = lax.rsqrt(jnp.mean(x * x, axis=-1, keepdims=True) + EPS)
    return x * r * g


def _rms_bwd(x, g, dy):
    r = lax.rsqrt(jnp.mean(x * x, axis=-1, keepdims=True) + EPS)
    xh = x * r
    dxh = dy * g
    dg = jnp.sum(dy * xh, axis=0, keepdims=True)
    dx = r * (dxh - xh * jnp.mean(dxh * xh, axis=-1, keepdims=True))
    return dx, dg


def _sum_all(x):
    return jnp.sum(jnp.sum(x, axis=1, keepdims=True), axis=0, keepdims=True)


def _cumsum_rows(x, n):
    row = lax.broadcasted_iota(jnp.int32, x.shape, 0)
    k = 1
    while k < n:
        x = x + jnp.where(row >= k, pltpu.roll(x, k, 0), 0.0)
        k *= 2
    return x


def _rev_cumsum_rows(x, n):
    row = lax.broadcasted_iota(jnp.int32, x.shape, 0)
    k = 1
    while k < n:
        x = x + jnp.where(row < n - k, pltpu.roll(x, n - k, 0), 0.0)
        k *= 2
    return x


def _load_once(pairs, sem):
    @pl.when(pl.program_id(0) == 0)
    def _():
        for k, (src, dst) in enumerate(pairs):
            pltpu.make_async_copy(src, dst, sem.at[k]).start()
        for k, (src, dst) in enumerate(pairs):
            pltpu.make_async_copy(src, dst, sem.at[k]).wait()


def _params(n_axes=1):
    return pltpu.CompilerParams(dimension_semantics=("arbitrary",) * n_axes, vmem_limit_bytes=VMEM_LIMIT)


def _rows(n, width, rev_of=None):
    if rev_of is None:
        return pl.BlockSpec((n, width), lambda i: (i, 0))
    return pl.BlockSpec((n, width), lambda i: (rev_of - 1 - i, 0))


def _whole(shape):
    nd = len(shape)
    return pl.BlockSpec(shape, lambda i: (0,) * nd)


ANY = pl.BlockSpec(memory_space=pl.ANY)
S = jax.ShapeDtypeStruct
WIRE = jnp.bfloat16


def _pos():
    return lax.axis_index("x"), lax.axis_index("y"), lax.axis_index("c")


def _other_chips(x, y):
    return [(1 - x, y), (x, 1 - y), (1 - x, 1 - y)]


def _remote(src, dst, send_sem, recv_sem, to):
    return pltpu.make_async_remote_copy(src_ref=src, dst_ref=dst, send_sem=send_sem, recv_sem=recv_sem,
                                        device_id=to, device_id_type=MESH)


def _gather_phase(phase, ins, outs, send_sems, recv_sems):
    x, y, c = _pos()
    me = 2 * x + y
    chips = _other_chips(x, y)
    for i, (src, dst) in enumerate(zip(ins, outs)):
        hr = src.shape[0] // 2
        my_half = pl.ds(pl.multiple_of(c * hr, 16), hr)
        sib_half = pl.ds(pl.multiple_of((1 - c) * hr, 16), hr)
        for k, (cx, cy) in enumerate(chips):
            s1, r1 = send_sems.at[6 * i + k], recv_sems.at[6 * i + k]
            s2, r2 = send_sems.at[6 * i + 3 + k], recv_sems.at[6 * i + 3 + k]
            first = lambda: _remote(src.at[my_half, :], dst.at[me, my_half, :], s1, r1, (cx, cy, c))
            landed = dst.at[2 * cx + cy, my_half, :]
            passed = lambda: _remote(landed, landed, s2, r2, (x, y, 1 - c))
            if phase == 0:
                first().start()
            elif phase == 1:
                _remote(landed, landed, s1, r1, (cx, cy, c)).wait_recv()
                passed().start()
            else:
                theirs = dst.at[2 * cx + cy, sib_half, :]
                _remote(theirs, theirs, s2, r2, (x, y, 1 - c)).wait_recv()
                first().wait_send()
                passed().wait_send()


def _half(ref, c, hr):
    sl = pl.ds(pl.multiple_of(c * hr, 8), hr)
    return ref.at[:, sl, :] if len(ref.shape) == 3 else ref.at[sl, :]


def _half_shape(b):
    return b.shape[:-2] + (b.shape[-2] // 2, b.shape[-1])


def _pair_phase(phase, ins, outs, send_sems, recv_sems):
    x, y, c = _pos()
    for k, (src, dst) in enumerate(zip(ins, outs)):
        cp = _remote(_half(src, 1 - c, src.shape[-2] // 2), dst, send_sems.at[k], recv_sems.at[k], (x, y, 1 - c))
        if phase == 0:
            cp.start()
        else:
            cp.wait()


def _quad_phase(phase, ins, outs, send_sems, recv_sems):
    x, y, c = _pos()
    me = 2 * x + y
    for i, (src, dst) in enumerate(zip(ins, outs)):
        for k, (cx, cy) in enumerate(_other_chips(x, y)):
            cp = _remote(src.at[2 * cx + cy], dst.at[me], send_sems.at[3 * i + k], recv_sems.at[3 * i + k], (cx, cy, c))
            if phase == 0:
                cp.start()
            else:
                got = dst.at[2 * cx + cy]
                _remote(got, got, send_sems.at[3 * i + k], recv_sems.at[3 * i + k], (cx, cy, c)).wait_recv()
                cp.wait_send()


def _prenorm(x, g0, shards):
    T = x.shape[0]
    tt = 2 * TT
    nt = T // tt
    ng = len(shards)

    def body(*refs):
        x_ref, g_ref = refs[:2]
        sh_in = refs[2:2 + ng]
        h0_ref = refs[2 + ng]
        sh_out = refs[3 + ng:3 + 2 * ng]
        send_sems, recv_sems = refs[3 + 2 * ng:]
        for phase, step in enumerate((0, nt // 2, nt - 1)):
            @pl.when(pl.program_id(0) == step)
            def _():
                _gather_phase(phase, sh_in, sh_out, send_sems, recv_sems)

        h0_ref[...] = _rms_fwd(x_ref[...], g_ref[...]).astype(BF)

    return pl.pallas_call(
        body, name="prenorm", grid=(nt,),
        in_specs=[_rows(tt, D), _whole((1, D))] + [ANY] * ng, out_specs=[_rows(tt, D)] + [ANY] * ng,
        out_shape=[S((T, D), BF)] + [S((4,) + s.shape, s.dtype) for s in shards],
        scratch_shapes=[pltpu.SemaphoreType.DMA((6 * ng,)), pltpu.SemaphoreType.DMA((6 * ng,))],
        compiler_params=_params(),
    )(x, g0, *shards)


def _blockdiag_mm(v, w4_ref):
    return jnp.concatenate([_mm(v[:, 256 * j:256 * (j + 1)], w4_ref[j]) for j in range(4)], axis=1)


def _lru_gates(lx, p_ref, wa_ref, wx_ref):
    r = _sigmoid(_blockdiag_mm(lx, wa_ref) + p_ref[5:6, :])
    i = _sigmoid(_blockdiag_mm(lx, wx_ref) + p_ref[6:7, :])
    sp = _softplus(-p_ref[7:8, :])
    la = -LRU_C * r * sp
    a = jnp.exp(la)
    th = jnp.tanh(la)
    mult = jnp.sqrt(-2.0 * th / (1.0 - th))
    return r, i, sp, a, mult


def _conv_from(xp_ref, p_ref, n):
    acc = p_ref[4:5, :] + p_ref[0:1, :] * xp_ref[pl.ds(8 - CONV_K + 1, n), :]
    for k in range(1, CONV_K):
        acc = acc + p_ref[k:k + 1, :] * xp_ref[pl.ds(8 - CONV_K + 1 + k, n), :]
    return acc


def _conv_bwd(dp_ref, dconv, x, p_ref, st_ref, n):
    if dconv is None:
        dconv = dp_ref[0:n, :]
    else:
        dp_ref[0:n, :] = dconv
    acc = None
    for k in range(CONV_K):
        g = dp_ref[pl.ds(CONV_K - 1 - k, n), :]
        acc = p_ref[k:k + 1, :] * g if acc is None else acc + p_ref[k:k + 1, :] * g
        st_ref[k:k + 1, :] += jnp.sum(g * x, axis=0, keepdims=True)
    st_ref[4:5, :] += jnp.sum(dconv, axis=0, keepdims=True)
    dp_ref[n:n + 8, :] = dp_ref[0:8, :]
    return acc


def _lru_fwd(h0, wcat, p_lru, wa4, wx4, shards):
    T = h0.shape[0]
    NT = T // TT
    ng = len(shards)

    def body(*refs):
        h0_ref, w_hbm, p_ref, wa_ref, wx_ref = refs[:5]
        sh_in = refs[5:5 + ng]
        h_ref, y_ref, lxc_ref, lxr_ref, lg_ref = refs[5 + ng:10 + ng]
        sh_out = refs[10 + ng:10 + 2 * ng]
        xp, a_s, u_s, hc, w_vm, wsem, send_sems, recv_sems = refs[10 + 2 * ng:]
        _load_once([(w_hbm.at[:, 0:2 * LW], w_vm)], wsem)
        for phase, step in enumerate((0, NT // 2, NT - 1)):
            @pl.when(pl.program_id(0) == step)
            def _():
                _gather_phase(phase, sh_in, sh_out, send_sems, recv_sems)

        @pl.when(pl.program_id(0) == 0)
        def _():
            xp[0:8, :] = jnp.zeros((8, LW), F32)
            hc[...] = jnp.zeros_like(hc)

        hv = h0_ref[...]
        lxr = jnp.dot(hv, w_vm[:, 0:LW], preferred_element_type=F32)
        lxr_ref[...] = lxr
        lg_ref[...] = jnp.dot(hv, w_vm[:, LW:2 * LW], preferred_element_type=F32)
        xp[8:8 + TT, :] = lxr
        lx = _conv_from(xp, p_ref, TT)
        lxc_ref[...] = lx
        xp[0:8, :] = xp[TT:TT + 8, :]
        r, i, sp, a, mult = _lru_gates(lx, p_ref, wa_ref, wx_ref)
        a_s[...] = a
        u_s[...] = mult * (i * lx)

        def step(t, h):
            h = a_s[pl.ds(t, 1), :] * h + u_s[pl.ds(t, 1), :]
            h_ref[pl.ds(t, 1), :] = h
            return h

        hc[0:1, :] = lax.fori_loop(0, TT, step, hc[0:1, :], unroll=8)
        gated = h_ref[...] * _gelu(lg_ref[...])
        y_ref[...] = _rms_fwd(gated, p_ref[8:9, :]).astype(BF)

    return pl.pallas_call(
        body, name="lru_fwd", grid=(NT,),
        in_specs=[_rows(TT, D), ANY, _whole((16, LW)), _whole((4, 256, 256)), _whole((4, 256, 256))] + [ANY] * ng,
        out_specs=[_rows(TT, LW), _rows(TT, LW), _rows(TT, LW), _rows(TT, LW), _rows(TT, LW)] + [ANY] * ng,
        out_shape=[S((T, LW), F32), S((T, LW), BF), S((T, LW), F32), S((T, LW), F32), S((T, LW), F32)]
        + [S((4,) + s.shape, s.dtype) for s in shards],
        scratch_shapes=[pltpu.VMEM((TT + 8, LW), F32), pltpu.VMEM((TT, LW), F32), pltpu.VMEM((TT, LW), F32),
                        pltpu.VMEM((8, LW), F32), pltpu.VMEM((D, 2 * LW), BF), pltpu.SemaphoreType.DMA((1,)),
                        pltpu.SemaphoreType.DMA((6 * ng,)), pltpu.SemaphoreType.DMA((6 * ng,))],
        compiler_params=_params(),
    )(h0, wcat, p_lru, wa4, wx4, *shards)


def _ssd_prep(cv, dt_ref, hp_ref):
    sg = _sigmoid(cv)
    xbc = cv * sg
    lane = lax.broadcasted_iota(jnp.int32, (CH, DTP), 1)
    raw = dt_ref[...] + hp_ref[0:1, :]
    dtv = jnp.where(lane < NH, _softplus(raw), 0.0)
    A = jnp.where(lane[0:1, :] < NH, -jnp.exp(hp_ref[1:2, :]), 0.0)
    cs = _cumsum_rows(dtv * A, CH)
    return sg, xbc, raw, dtv, A, cs


def _per_head_lanes(v):
    r = v.shape[0]
    first = lax.broadcasted_iota(jnp.int32, (r, 2 * HD), 1) < HD
    pairs = [jnp.where(first, jnp.broadcast_to(v[:, 2 * j:2 * j + 1], (r, 2 * HD)),
                       jnp.broadcast_to(v[:, 2 * j + 1:2 * j + 2], (r, 2 * HD))) for j in range(NH // 2)]
    return jnp.concatenate(pairs, axis=1)


def _per_head_rows(col, g):
    return jnp.concatenate([jnp.broadcast_to(col[g * HPG + k:g * HPG + k + 1, :], (HD, NS)) for k in range(HPG)], axis=0)


def _ssd_decays(cs):
    csT = cs.T
    cl = cs[CH - 1:CH, :]
    E_x = _per_head_lanes(jnp.exp(cs))
    dsm = jnp.exp(cl - cs)
    ds_x = _per_head_lanes(dsm)
    El_rows = jnp.broadcast_to(jnp.exp(csT[0:NH, CH - 1:CH]), (NH, NS))
    return csT, dsm, E_x, ds_x, El_rows


def _ssd_fwd(h0, wcat, cw_ssd, hp_ssd, g_ssd, shards):
    T = h0.shape[0]
    NC = T // CH
    ng = len(shards)
    c0 = 2 * LW

    def body(*refs):
        h0_ref, w_hbm, cw_ref, hp_ref, g_ref = refs[:5]
        sh_in = refs[5:5 + ng]
        y_ref, yn_ref, st_ref, cv_ref, z_ref, xr_ref, dt_ref = refs[5 + ng:12 + ng]
        sh_out = refs[12 + ng:12 + 2 * ng]
        xp, st, w_vm, wsem, send_sems, recv_sems = refs[12 + 2 * ng:]
        _load_once([(w_hbm.at[:, c0:PC], w_vm)], wsem)
        for phase, step in enumerate((0, NC // 2, NC - 1)):
            @pl.when(pl.program_id(0) == step)
            def _():
                _gather_phase(phase, sh_in, sh_out, send_sems, recv_sems)

        @pl.when(pl.program_id(0) == 0)
        def _():
            xp[0:8, :] = jnp.zeros((8, XBC), F32)
            st[...] = jnp.zeros_like(st)

        hv = h0_ref[...]
        z_ref[...] = jnp.dot(hv, w_vm[:, 0:SI], preferred_element_type=F32)
        xraw = jnp.dot(hv, w_vm[:, SI:SI + XBC], preferred_element_type=F32)
        xr_ref[...] = xraw
        dt_ref[...] = jnp.dot(hv, w_vm[:, SI + XBC:SI + XBC + DTP], preferred_element_type=F32)
        xp[8:8 + CH, :] = xraw
        cv = _conv_from(xp, cw_ref, CH)
        cv_ref[...] = cv
        sg, xbc, raw, dtv, A, cs = _ssd_prep(cv, dt_ref, hp_ref)
        xp[0:8, :] = xp[CH:CH + 8, :]
        st_ref[0] = st[...]
        csT, dsm, E_x, ds_x, El_rows = _ssd_decays(cs)
        X = xbc[:, 0:SI]
        xs = X * _per_head_lanes(dtv)
        xsd = (xs * ds_x).astype(BF)
        DX = _per_head_lanes(hp_ref[...])[2:3, :] * X
        tril = lax.broadcasted_iota(jnp.int32, (CH, CH), 0) >= lax.broadcasted_iota(jnp.int32, (CH, CH), 1)
        first = lax.broadcasted_iota(jnp.int32, (CH, 2 * HD), 1) < HD
        GW = HPG * HD
        for g in range(NG):
            Bg = xbc[:, SI + NS * g:SI + NS * (g + 1)].astype(BF)
            Cg = xbc[:, SI + NG * NS + NS * g:SI + NG * NS + NS * (g + 1)].astype(BF)
            G = _mm_nt(Cg, Bg)
            Sg = st[GW * g:GW * (g + 1), :]
            Yo = _mm_nt(Cg, Sg) * E_x[:, GW * g:GW * (g + 1)]
            st[GW * g:GW * (g + 1), :] = _per_head_rows(El_rows, g) * Sg + _mm_tn(xsd[:, GW * g:GW * (g + 1)], Bg)
            for jj in range(HPG // 2):
                j = g * (HPG // 2) + jj
                ps = slice(2 * HD * j, 2 * HD * (j + 1))
                xs_pair = xs[:, ps]
                acc = Yo[:, 2 * HD * jj:2 * HD * (jj + 1)] + DX[:, ps]
                for e in range(2):
                    h = 2 * j + e
                    Lm = jnp.exp(jnp.where(tril, cs[:, h:h + 1] - csT[h:h + 1, :], -1e30))
                    acc = acc + _mm(G * Lm, jnp.where(first if e == 0 else ~first, xs_pair, 0.0))
                y_ref[:, ps] = acc
        zz = z_ref[...]
        gated = y_ref[...] * (zz * _sigmoid(zz))
        yn_ref[...] = _rms_fwd(gated, g_ref[...]).astype(BF)

    return pl.pallas_call(
        body, name="ssd_fwd", grid=(NC,),
        in_specs=[_rows(CH, D), ANY, _whole((8, XBC)), _whole((8, DTP)), _whole((1, SI))] + [ANY] * ng,
        out_specs=[_rows(CH, SI), _rows(CH, SI), pl.BlockSpec((1, NH * HD, NS), lambda i: (i, 0, 0)), _rows(CH, XBC),
                   _rows(CH, SI), _rows(CH, XBC), _rows(CH, DTP)] + [ANY] * ng,
        out_shape=[S((T, SI), F32), S((T, SI), BF), S((NC, NH * HD, NS), F32), S((T, XBC), F32),
                   S((T, SI), F32), S((T, XBC), F32), S((T, DTP), F32)] + [S((4,) + s.shape, s.dtype) for s in shards],
        scratch_shapes=[pltpu.VMEM((CH + 8, XBC), F32), pltpu.VMEM((NH * HD, NS), F32),
                        pltpu.VMEM((D, PC - c0), BF), pltpu.SemaphoreType.DMA((1,)),
                        pltpu.SemaphoreType.DMA((6 * ng,)), pltpu.SemaphoreType.DMA((6 * ng,))],
        compiler_params=_params(),
    )(h0, wcat, cw_ssd, hp_ssd, g_ssd, *shards)


def _outproj(ylru, yssd, x, wout, g_pm, g_pf):
    T = x.shape[0]

    def body(yl_ref, ys_ref, x_ref, w_hbm, gpm_ref, gpf_ref, mix_ref, x1_ref, h2_ref, w_vm, sem):
        _load_once([(w_hbm, w_vm)], sem)
        mix = (jnp.dot(yl_ref[...], w_vm[0:LW, :], preferred_element_type=F32)
               + jnp.dot(ys_ref[...], w_vm[LW:LW + SI, :], preferred_element_type=F32))
        mix_ref[...] = mix
        x1 = x_ref[...] + _rms_fwd(mix, gpm_ref[...])
        x1_ref[...] = x1
        h2_ref[...] = _rms_fwd(x1, gpf_ref[...]).astype(BF)

    return pl.pallas_call(
        body, name="outproj", grid=(T // TT,),
        in_specs=[_rows(TT, LW), _rows(TT, SI), _rows(TT, D), ANY, _whole((1, D)), _whole((1, D))],
        out_specs=[_rows(TT, D), _rows(TT, D), _rows(TT, D)],
        out_shape=[S((T, D), F32), S((T, D), F32), S((T, D), BF)],
        scratch_shapes=[pltpu.VMEM((LW + SI, D), BF), pltpu.SemaphoreType.DMA((1,))],
        compiler_params=_params(),
    )(ylru, yssd, x, wout, g_pm, g_pf)


def _ffn_fwd(h2, x1, tgt, wg, wu, wd, g_pff):
    T = x1.shape[0]

    def body(h2_ref, x1_ref, t_ref, wg_hbm, wu_hbm, wd_hbm, g_ref,
             gate_ref, up_ref, act_ref, df_ref, dx2_ref, st_ref, wg_vm, wu_vm, wd_vm, sem):
        _load_once([(wg_hbm, wg_vm), (wu_hbm, wu_vm), (wd_hbm, wd_vm)], sem)

        @pl.when(pl.program_id(0) == 0)
        def _():
            st_ref[...] = jnp.zeros_like(st_ref)

        h2 = h2_ref[...]
        gate = jnp.dot(h2, wg_vm[...], preferred_element_type=F32)
        up = jnp.dot(h2, wu_vm[...], preferred_element_type=F32)
        gate_ref[...] = gate
        up_ref[...] = up
        act = (gate * _sigmoid(gate) * up).astype(BF)
        act_ref[...] = act
        f = jnp.dot(act, wd_vm[...], preferred_element_type=F32)
        g = g_ref[...]
        x2 = x1_ref[...] + _rms_fwd(f, g)
        err = x2 - t_ref[...]
        st_ref[0:1, :] += 0.5 * jnp.sum(err * err, axis=0, keepdims=True) * (1.0 / D)
        dx2 = err * (1.0 / D)
        dx2_ref[...] = dx2
        df, dg = _rms_bwd(f, g, dx2)
        df_ref[...] = df.astype(BF)
        st_ref[1:2, :] += dg

    return pl.pallas_call(
        body, name="ffn_fwd", grid=(T // TT,),
        in_specs=[_rows(TT, D), _rows(TT, D), _rows(TT, D), ANY, ANY, ANY, _whole((1, D))],
        out_specs=[_rows(TT, DFF), _rows(TT, DFF), _rows(TT, DFF), _rows(TT, D), _rows(TT, D), _whole((8, D))],
        out_shape=[S((T, DFF), F32), S((T, DFF), F32), S((T, DFF), BF), S((T, D), BF), S((T, D), F32), S((8, D), F32)],
        scratch_shapes=[pltpu.VMEM((D, DFF), BF), pltpu.VMEM((D, DFF), BF), pltpu.VMEM((DFF, D), BF),
                        pltpu.SemaphoreType.DMA((3,))],
        compiler_params=_params(),
    )(h2, x1, tgt, wg, wu, wd, g_pff)


def _ffn_bwd(df, gate, up, wdT, wgT, wuT):
    T = df.shape[0]

    def body(df_ref, gate_ref, up_ref, wd_hbm, wg_hbm, wu_hbm, dgate_ref, dup_ref, dh2_ref, wd_vm, wg_vm, wu_vm, sem):
        _load_once([(wd_hbm, wd_vm), (wg_hbm, wg_vm), (wu_hbm, wu_vm)], sem)
        dact = jnp.dot(df_ref[...], wd_vm[...], preferred_element_type=F32)
        gate = gate_ref[...]
        s = _sigmoid(gate)
        dup = (dact * (gate * s)).astype(BF)
        dgate = (dact * up_ref[...] * (s + gate * s * (1.0 - s))).astype(BF)
        dup_ref[...] = dup
        dgate_ref[...] = dgate
        dh2_ref[...] = (jnp.dot(dgate, wg_vm[...], preferred_element_type=F32)
                        + jnp.dot(dup, wu_vm[...], preferred_element_type=F32))

    return pl.pallas_call(
        body, name="ffn_bwd", grid=(T // TT,),
        in_specs=[_rows(TT, D), _rows(TT, DFF), _rows(TT, DFF), ANY, ANY, ANY],
        out_specs=[_rows(TT, DFF), _rows(TT, DFF), _rows(TT, D)],
        out_shape=[S((T, DFF), BF), S((T, DFF), BF), S((T, D), F32)],
        scratch_shapes=[pltpu.VMEM((D, DFF), BF), pltpu.VMEM((DFF, D), BF), pltpu.VMEM((DFF, D), BF),
                        pltpu.SemaphoreType.DMA((3,))],
        compiler_params=_params(),
    )(df, gate, up, wdT, wgT, wuT)


def _mix_bwd(dh2, x1, dx2, mix, woutT, g_pf, g_pm):
    T = x1.shape[0]

    def body(dh2_ref, x1_ref, dx2_ref, mix_ref, w_hbm, gpf_ref, gpm_ref,
             dx1_ref, dmix_ref, dyl_ref, dys_ref, st_ref, w_vm, sem):
        _load_once([(w_hbm, w_vm)], sem)

        @pl.when(pl.program_id(0) == 0)
        def _():
            st_ref[...] = jnp.zeros_like(st_ref)

        dxa, dgpf = _rms_bwd(x1_ref[...], gpf_ref[...], dh2_ref[...])
        dx1 = dx2_ref[...] + dxa
        dx1_ref[...] = dx1
        dmix, dgpm = _rms_bwd(mix_ref[...], gpm_ref[...], dx1)
        dmix = dmix.astype(BF)
        dmix_ref[...] = dmix
        st_ref[0:1, :] += dgpf
        st_ref[1:2, :] += dgpm
        dyl_ref[...] = jnp.dot(dmix, w_vm[:, 0:LW], preferred_element_type=F32)
        dys_ref[...] = jnp.dot(dmix, w_vm[:, LW:LW + SI], preferred_element_type=F32)

    return pl.pallas_call(
        body, name="mix_bwd", grid=(T // TT,),
        in_specs=[_rows(TT, D), _rows(TT, D), _rows(TT, D), _rows(TT, D), ANY, _whole((1, D)), _whole((1, D))],
        out_specs=[_rows(TT, D), _rows(TT, D), _rows(TT, LW), _rows(TT, SI), _whole((8, D))],
        out_shape=[S((T, D), F32), S((T, D), BF), S((T, LW), F32), S((T, SI), F32), S((8, D), F32)],
        scratch_shapes=[pltpu.VMEM((D, LW + SI), BF), pltpu.SemaphoreType.DMA((1,))],
        compiler_params=_params(),
    )(dh2, x1, dx2, mix, woutT, g_pf, g_pm)


def _halo(width, n_tiles, tile):
    per = tile // 8
    return pl.BlockSpec((8, width), lambda i: (jnp.maximum((n_tiles - 1 - i) * per - 1, 0), 0))


def _lru_bwd(dy, lxr, lxc, lg, h, p_lru, wa4, wx4, wa4T, wx4T, bufs):
    T = dy.shape[0]
    NT = T // TT
    nb = len(bufs)

    def body(*refs):
        dy_ref, lxr_ref, lxc_ref, lg_ref, h_ref, hh_ref, p_ref, wa_ref, wx_ref, waT_ref, wxT_ref = refs[:11]
        b_in = refs[11:11 + nb]
        dlx_ref, dlg_ref, st_ref, dwa_ref, dwx_ref = refs[11 + nb:16 + nb]
        b_out = refs[16 + nb:16 + 2 * nb]
        hp, dp, a_s, d_s, g_s, cc, send_sems, recv_sems = refs[16 + 2 * nb:]
        for phase, step in enumerate((0, NT - 1)):
            @pl.when(pl.program_id(0) == step)
            def _():
                _pair_phase(phase, b_in, b_out, send_sems, recv_sems)

        dy = dy_ref[...]
        first = pl.program_id(0) == 0
        top = pl.program_id(0) == NT - 1

        @pl.when(first)
        def _():
            st_ref[...] = jnp.zeros_like(st_ref)
            dwa_ref[...] = jnp.zeros_like(dwa_ref)
            dwx_ref[...] = jnp.zeros_like(dwx_ref)
            dp[TT:TT + 8, :] = jnp.zeros((8, LW), F32)
            cc[...] = jnp.zeros_like(cc)

        hp[0:8, :] = hh_ref[...] * jnp.where(top, 0.0, 1.0)
        hp[8:8 + TT, :] = h_ref[...]
        lx = lxc_ref[...]
        r, i, sp, a, mult = _lru_gates(lx, p_ref, wa_ref, wx_ref)

        lg = lg_ref[...]
        hcur = h_ref[...]
        ge = _gelu(lg)
        dgated, dgn = _rms_bwd(hcur * ge, p_ref[8:9, :], dy)
        st_ref[8:9, :] += dgn
        dlg_ref[...] = (dgated * hcur * _gelu_grad(lg)).astype(BF)
        a_s[...] = a
        d_s[...] = dgated * ge

        def step(k, c):
            t = TT - 1 - k
            g = d_s[pl.ds(t, 1), :] + c
            g_s[pl.ds(t, 1), :] = g
            return a_s[pl.ds(t, 1), :] * g

        cc[0:1, :] = lax.fori_loop(0, TT, step, cc[0:1, :], unroll=8)
        gt = g_s[...]
        da = gt * hp[pl.ds(7, TT), :]
        dmult = gt * i * lx
        di = gt * mult * lx
        dlxc = gt * mult * i
        dla = da * a - dmult * (a * a) / mult
        dr = dla * (-LRU_C * sp)
        st_ref[7:8, :] += jnp.sum(dla * (-LRU_C * r), axis=0, keepdims=True) * (-_sigmoid(-p_ref[7:8, :]))
        dzr = dr * r * (1.0 - r)
        dzi = di * i * (1.0 - i)
        st_ref[5:6, :] += jnp.sum(dzr, axis=0, keepdims=True)
        st_ref[6:7, :] += jnp.sum(dzi, axis=0, keepdims=True)
        dlxc = dlxc + _blockdiag_mm(dzr, waT_ref) + _blockdiag_mm(dzi, wxT_ref)
        for j in range(4):
            sl = slice(256 * j, 256 * (j + 1))
            pa = _mm_tn(lx[:, sl], dzr[:, sl])
            px = _mm_tn(lx[:, sl], dzi[:, sl])
            for b in range(4):
                bs = slice(BW * b, BW * (b + 1))
                dwa_ref[4 * j + b] += pa[bs, bs]
                dwx_ref[4 * j + b] += px[bs, bs]
        dlx_ref[...] = _conv_bwd(dp, dlxc, lxr_ref[...], p_ref, st_ref, TT).astype(BF)

    w4 = _whole((4, 256, 256))
    return pl.pallas_call(
        body, name="lru_bwd", grid=(NT,),
        in_specs=[_rows(TT, LW, NT), _rows(TT, LW, NT), _rows(TT, LW, NT), _rows(TT, LW, NT), _rows(TT, LW, NT),
                  _halo(LW, NT, TT), _whole((16, LW)), w4, w4, w4, w4] + [ANY] * nb,
        out_specs=[_rows(TT, LW, NT), _rows(TT, LW, NT), _whole((16, LW)), _whole((NBLK, BW, BW)), _whole((NBLK, BW, BW))]
        + [ANY] * nb,
        out_shape=[S((T, LW), BF), S((T, LW), BF), S((16, LW), F32), S((NBLK, BW, BW), F32), S((NBLK, BW, BW), F32)]
        + [S(_half_shape(b), b.dtype) for b in bufs],
        scratch_shapes=[pltpu.VMEM((TT + 8, LW), F32), pltpu.VMEM((TT + 8, LW), F32),
                        pltpu.VMEM((TT, LW), F32), pltpu.VMEM((TT, LW), F32), pltpu.VMEM((TT, LW), F32),
                        pltpu.VMEM((8, LW), F32), pltpu.SemaphoreType.DMA((nb,)), pltpu.SemaphoreType.DMA((nb,))],
        compiler_params=_params(),
    )(dy, lxr, lxc, lg, h, h, p_lru, wa4, wx4, wa4T, wx4T, *bufs)


def _ssd_bwd(dyn, xbcr, cv, z, dtr, y, states, cw_ssd, hp_ssd, g_ssd, parts):
    T = dyn.shape[0]
    NC = T // CH
    nq = len(parts)

    def body(*refs):
        dyn_ref, xr_ref, cv_ref, z_ref, dt_ref, y_ref, st_ref, cw_ref, hp_ref, g_ref = refs[:10]
        q_in = refs[10:10 + nq]
        dxbc_ref, dz_ref, ddt_ref, cst_ref, hst_ref, gst_ref = refs[10 + nq:16 + nq]
        q_out = refs[16 + nq:16 + 2 * nq]
        dp, dS, send_sems, recv_sems = refs[16 + 2 * nq:]
        dyn = dyn_ref[...]
        first = pl.program_id(0) == 0
        for phase, step in enumerate((0, NC - 1)):
            @pl.when(pl.program_id(0) == step)
            def _():
                _quad_phase(phase, q_in, q_out, send_sems, recv_sems)

        @pl.when(first)
        def _():
            cst_ref[...] = jnp.zeros_like(cst_ref)
            hst_ref[...] = jnp.zeros_like(hst_ref)
            gst_ref[...] = jnp.zeros_like(gst_ref)
            dp[CH:CH + 8, :] = jnp.zeros((8, XBC), F32)
            dS[...] = jnp.zeros_like(dS)

        cv = cv_ref[...]
        sg, xbc, raw, dtv, A, cs = _ssd_prep(cv, dt_ref, hp_ref)
        csT, dsm, E_x, ds_x, El_rows = _ssd_decays(cs)
        row_i = lax.broadcasted_iota(jnp.int32, (CH, CH), 0)
        col_i = lax.broadcasted_iota(jnp.int32, (CH, CH), 1)
        tril = row_i >= col_i
        first = col_i < HD
        head_of = ((lax.broadcasted_iota(jnp.int32, (DTP, SI), 1) >> 6)
                   == lax.broadcasted_iota(jnp.int32, (DTP, SI), 0)).astype(BF)
        head_ofT = ((lax.broadcasted_iota(jnp.int32, (SI, DTP), 0) >> 6)
                    == lax.broadcasted_iota(jnp.int32, (SI, DTP), 1)).astype(BF)

        def hi_lo(v):
            hi = v.astype(BF)
            return hi, (v - hi.astype(F32)).astype(BF)

        GW = HPG * HD

        def lane_sums(v, g):
            hi, lo = hi_lo(v)
            w = head_ofT[GW * g:GW * (g + 1), :]
            return _mm(hi, w) + _mm(lo, w)

        zz = z_ref[...]
        sz = _sigmoid(zz)
        yv = y_ref[...]
        dgn, dg = _rms_bwd(yv * (zz * sz), g_ref[...], dyn)
        gst_ref[0:1, :] += dg
        dz_ref[...] = (dgn * yv * (sz + zz * sz * (1.0 - sz))).astype(BF)
        dY = dgn * (zz * sz)

        X = xbc[:, 0:SI]
        dsilu = sg + cv * sg * (1.0 - sg)
        dt_x = _per_head_lanes(dtv)
        xs = X * dt_x
        xsd = (xs * ds_x).astype(BF)
        D_x = _per_head_lanes(hp_ref[...])[2:3, :]
        zero = jnp.zeros((CH, DTP), F32)
        dcs_col = zero
        dcs_row = zero
        dds, ddt_col, dD_rows, dcl_rows = zero, zero, zero, zero
        for g in range(NG):
            gs = slice(GW * g, GW * (g + 1))
            Bg = xbc[:, SI + NS * g:SI + NS * (g + 1)].astype(BF)
            Cg = xbc[:, SI + NG * NS + NS * g:SI + NG * NS + NS * (g + 1)].astype(BF)
            G = _mm_nt(Cg, Bg)
            Sg = st_ref[0, gs, :]
            dSe = dS[gs, :]
            dYg = dY[:, gs]
            dcs_col = dcs_col + lane_sums(dYg * (_mm_nt(Cg, Sg) * E_x[:, gs]), g)
            dD_rows = dD_rows + lane_sums(dYg * X[:, gs], g)
            dP = dYg * E_x[:, gs]
            dCg = _mm(dP, Sg)
            dS[gs, :] = _mm_tn(dP, Cg) + _per_head_rows(El_rows, g) * dSe
            t_hi, t_lo = hi_lo(dSe * Sg)
            dcl_rows = dcl_rows + _mm(head_of[:, gs], t_hi) + _mm(head_of[:, gs], t_lo)
            Q = _mm_nt(Bg, dSe)
            dds = dds + lane_sums(Q * xs[:, gs], g)
            dBg = _mm(xsd[:, gs], dSe)
            dG = jnp.zeros((CH, CH), F32)
            dxs_pairs = []
            for jj in range(HPG // 2):
                j = g * (HPG // 2) + jj
                ps = slice(2 * HD * j, 2 * HD * (j + 1))
                xs_pair = xs[:, ps]
                dxs_pair = Q[:, 2 * HD * jj:2 * HD * (jj + 1)] * ds_x[:, ps]
                for e in range(2):
                    h = 2 * j + e
                    Lm = jnp.exp(jnp.where(tril, cs[:, h:h + 1] - csT[h:h + 1, :], -1e30))
                    M = G * Lm
                    dYm = jnp.where(first if e == 0 else ~first, dY[:, ps], 0.0).astype(BF)
                    dM = _mm_nt(dYm, xs_pair)
                    dxs_pair = dxs_pair + _mm_tn(M, dYm)
                    Wm = dM * M
                    dcs_col = dcs_col + jnp.where(col_i == h, jnp.sum(Wm, axis=1, keepdims=True), 0.0)
                    dcs_row = dcs_row + jnp.where(row_i == h, -jnp.sum(Wm, axis=0, keepdims=True), 0.0)
                    dG = dG + dM * Lm
                dp[0:CH, ps] = (D_x[:, ps] * dY[:, ps] + dxs_pair * dt_x[:, ps]) * dsilu[:, ps]
                dxs_pairs.append(dxs_pair)
            ddt_col = ddt_col + lane_sums(jnp.concatenate(dxs_pairs, axis=1) * X[:, gs], g)
            bs = slice(SI + NS * g, SI + NS * (g + 1))
            cs_ = slice(SI + NG * NS + NS * g, SI + NG * NS + NS * (g + 1))
            dp[0:CH, bs] = (dBg + _mm_tn(dG, Cg)) * dsilu[:, bs]
            dp[0:CH, cs_] = (dCg + _mm(dG, Bg)) * dsilu[:, cs_]

        dds = dds * dsm
        dcs_col = dcs_col - dds
        dD = jnp.sum(dD_rows, axis=0, keepdims=True)
        dcl_rows = jnp.sum(dcl_rows, axis=1, keepdims=True) * jnp.exp(csT[:, CH - 1:CH])
        dcs_row = dcs_row + jnp.where(col_i == CH - 1, dcl_rows, 0.0)
        dcs_col = dcs_col + jnp.where(row_i == CH - 1, jnp.sum(dds, axis=0, keepdims=True), 0.0)

        da = _rev_cumsum_rows(dcs_col + dcs_row.T, CH)
        ddt_col = ddt_col + da * A
        hst_ref[1:2, :] += jnp.sum(da * dtv, axis=0, keepdims=True) * A
        hst_ref[2:3, :] += dD
        draw = jnp.where(col_i < NH, ddt_col * _sigmoid(raw), 0.0)
        ddt_ref[...] = draw.astype(BF)
        hst_ref[0:1, :] += jnp.sum(draw, axis=0, keepdims=True)

        dxbc_ref[...] = _conv_bwd(dp, None, xr_ref[...], cw_ref, cst_ref, CH).astype(BF)

    return pl.pallas_call(
        body, name="ssd_bwd", grid=(NC,),
        in_specs=[_rows(CH, SI, NC), _rows(CH, XBC, NC), _rows(CH, XBC, NC), _rows(CH, SI, NC), _rows(CH, DTP, NC),
                  _rows(CH, SI, NC), pl.BlockSpec((1, NH * HD, NS), lambda i: (NC - 1 - i, 0, 0)),
                  _whole((8, XBC)), _whole((8, DTP)), _whole((1, SI))] + [ANY] * nq,
        out_specs=[_rows(CH, XBC, NC), _rows(CH, SI, NC), _rows(CH, DTP, NC), _whole((16, XBC)), _whole((16, DTP)),
                   _whole((8, SI))] + [ANY] * nq,
        out_shape=[S((T, XBC), BF), S((T, SI), BF), S((T, DTP), BF), S((16, XBC), F32), S((16, DTP), F32), S((8, SI), F32)]
        + [S(p.shape, p.dtype) for p in parts],
        scratch_shapes=[pltpu.VMEM((CH + 8, XBC), F32), pltpu.VMEM((NH * HD, NS), F32),
                        pltpu.SemaphoreType.DMA((3 * nq,)), pltpu.SemaphoreType.DMA((3 * nq,))],
        compiler_params=_params(),
    )(dyn, xbcr, cv, z, dtr, y, states, cw_ssd, hp_ssd, g_ssd, *parts)


def _inproj_bwd(dlx, dlg, dz, dxbc, ddt, x, dx1, wcatT, g0, parts):
    T = x.shape[0]
    NT = T // TT
    nq = len(parts)

    def body(*refs):
        dlx_ref, dlg_ref, dz_ref, dxbc_ref, ddt_ref, x_ref, dx1_ref, w_hbm, g_ref = refs[:9]
        q_in = refs[9:9 + nq]
        dx_ref, st_ref = refs[9 + nq:11 + nq]
        q_out = refs[11 + nq:11 + 2 * nq]
        w_vm, sem, send_sems, recv_sems = refs[11 + 2 * nq:]
        _load_once([(w_hbm, w_vm)], sem)
        for phase, step in enumerate((0, NT - 1)):
            @pl.when(pl.program_id(0) == step)
            def _():
                _quad_phase(phase, q_in, q_out, send_sems, recv_sems)

        @pl.when(pl.program_id(0) == 0)
        def _():
            st_ref[...] = jnp.zeros_like(st_ref)

        dh = jnp.dot(dlx_ref[...], w_vm[0:1024, :], preferred_element_type=F32)
        dh = dh + jnp.dot(dlg_ref[...], w_vm[1024:2048, :], preferred_element_type=F32)
        dh = dh + jnp.dot(dz_ref[...], w_vm[2048:3072, :], preferred_element_type=F32)
        dh = dh + jnp.dot(dxbc_ref[...], w_vm[3072:3072 + XBC, :], preferred_element_type=F32)
        dh = dh + jnp.dot(ddt_ref[...], w_vm[3072 + XBC:PC, :], preferred_element_type=F32)
        dx, dg = _rms_bwd(x_ref[...], g_ref[...], dh)
        dx_ref[...] = dx1_ref[...] + dx
        st_ref[0:1, :] += dg

    return pl.pallas_call(
        body, name="inproj_bwd", grid=(NT,),
        in_specs=[_rows(TT, 1024), _rows(TT, 1024), _rows(TT, 1024), _rows(TT, XBC), _rows(TT, DTP), _rows(TT, D),
                  _rows(TT, D), ANY, _whole((1, D))] + [ANY] * nq,
        out_specs=[_rows(TT, D), _whole((8, D))] + [ANY] * nq,
        out_shape=[S((T, D), F32), S((8, D), F32)] + [S(p.shape, p.dtype) for p in parts],
        scratch_shapes=[pltpu.VMEM((PC, D), BF), pltpu.SemaphoreType.DMA((1,)),
                        pltpu.SemaphoreType.DMA((3 * nq,)), pltpu.SemaphoreType.DMA((3 * nq,))],
        compiler_params=_params(),
    )(dlx, dlg, dz, dxbc, ddt, x, dx1, wcatT, g0, *parts)


def _wgrad(name, a, b):
    T, M = a.shape
    N = b.shape[1]
    tn = N
    while M * tn * 4 > (6 << 20) and tn % 256 == 0:
        tn //= 2
    tk = min(T, 1024 if M > 1024 else 4096 if tn <= 1024 else 2048)

    def body(a_ref, b_ref, o_ref):
        p = lax.dot_general(a_ref[...], b_ref[...], (((0,), (0,)), ((), ())), preferred_element_type=F32)

        @pl.when(pl.program_id(1) == 0)
        def _():
            o_ref[...] = p

        @pl.when(pl.program_id(1) > 0)
        def _():
            o_ref[...] += p

    return pl.pallas_call(
        body, name=name, grid=(N // tn, T // tk),
        in_specs=[pl.BlockSpec((tk, M), lambda j, k: (k, 0)), pl.BlockSpec((tk, tn), lambda j, k: (k, j))],
        out_specs=pl.BlockSpec((M, tn), lambda j, k: (0, j)), out_shape=S((M, N), F32),
        compiler_params=_params(2),
    )(a, b)


def _adamw(name, w, g, m, v):
    _, R, C = w.shape

    def body(w_ref, g_ref, m_ref, v_ref, d_ref, nm_ref, nv_ref):
        d_ref[0], nm_ref[0], nv_ref[0] = _adam_math(w_ref[0], g_ref[...], m_ref[0], v_ref[0])

    if R % 8 == 0:
        tr = _row_tile(R, C)
        n_tiles = R // tr
        blk, gblk = pl.BlockSpec((1, tr, C), lambda i: (0, i, 0)), pl.BlockSpec((tr, C), lambda i: (i, 0))
    else:
        tc = 128 * max(k for k in range(1, C // 128 + 1) if C % (128 * k) == 0 and R * 128 * k * 4 <= (5 << 18))
        n_tiles = C // tc
        blk, gblk = pl.BlockSpec((1, R, tc), lambda i: (0, 0, i)), pl.BlockSpec((R, tc), lambda i: (0, i))
    return pl.pallas_call(
        body, name=name, grid=(n_tiles,),
        in_specs=[blk, gblk, blk, blk], out_specs=[blk] * 3,
        out_shape=[S((1, R, C), F32)] * 3, compiler_params=_params(),
    )(w, g, m, v)


def _pair_exchange(name, bufs):
    n = len(bufs)

    def body(*refs):
        for phase in range(2):
            _pair_phase(phase, refs[:n], refs[n:2 * n], refs[2 * n], refs[2 * n + 1])

    return pl.pallas_call(
        body, name=name, in_specs=[ANY] * n, out_specs=[ANY] * n,
        out_shape=[S(_half_shape(b), b.dtype) for b in bufs],
        scratch_shapes=[pltpu.SemaphoreType.DMA((n,)), pltpu.SemaphoreType.DMA((n,))],
    )(*bufs)


def _quad_exchange(bufs):
    n = len(bufs)

    def body(*refs):
        ins, outs = refs[:n], refs[n:2 * n]
        send_sems, recv_sems = refs[2 * n], refs[2 * n + 1]
        x, y, c = _pos()
        me = 2 * x + y
        chips = _other_chips(x, y)
        copies = []
        for k, (src, dst) in enumerate(zip(ins, outs)):
            for j, (cx, cy) in enumerate(chips):
                cp = _remote(src, dst.at[me], send_sems.at[3 * k + j], recv_sems.at[3 * k + j], (cx, cy, c))
                cp.start()
                copies.append(cp)
        for k, (src, dst) in enumerate(zip(ins, outs)):
            for j, (cx, cy) in enumerate(chips):
                blk = dst.at[2 * cx + cy]
                _remote(blk, blk, send_sems.at[3 * k + j], recv_sems.at[3 * k + j], (cx, cy, c)).wait_recv()
        for cp in copies:
            cp.wait_send()

    return pl.pallas_call(
        body, name="quad_exchange", in_specs=[ANY] * n, out_specs=[ANY] * n,
        out_shape=[S((4,) + b.shape, b.dtype) for b in bufs],
        scratch_shapes=[pltpu.SemaphoreType.DMA((3 * n,)), pltpu.SemaphoreType.DMA((3 * n,))],
    )(*bufs)


def _pair_gather(bufs):
    n = len(bufs)

    def body(*refs):
        ins, outs = refs[:n], refs[n:2 * n]
        send_sems, recv_sems = refs[2 * n], refs[2 * n + 1]
        x, y, c = _pos()
        copies = []
        for k, buf in enumerate(outs):
            mine = _half(buf, c, buf.shape[0] // 2)
            cp = _remote(mine, mine, send_sems.at[k], recv_sems.at[k], (x, y, 1 - c))
            cp.start()
            copies.append(cp)
        for k, buf in enumerate(outs):
            theirs = _half(buf, 1 - c, buf.shape[0] // 2)
            _remote(theirs, theirs, send_sems.at[k], recv_sems.at[k], (x, y, 1 - c)).wait_recv()
        for cp in copies:
            cp.wait_send()

    return pl.pallas_call(
        body, name="pair_gather", in_specs=[ANY] * n, out_specs=[ANY] * n,
        out_shape=[S(b.shape, b.dtype) for b in bufs], input_output_aliases={k: k for k in range(n)},
        scratch_shapes=[pltpu.SemaphoreType.DMA((n,)), pltpu.SemaphoreType.DMA((n,))],
    )(*bufs)


def _row_tile(rows, cols, mult=8):
    best = mult
    for t in range(mult, rows + 1, mult):
        if rows % t == 0 and t * cols * 4 <= (1 << 20):
            best = t
    return best


def _add_own_half(name, full, got, c, out_dtype, by_columns):
    hr = got.shape[-2]
    wide = got.shape[-1]
    cols = wide // 4 if by_columns else wide
    tr = _row_tile(hr, wide, 16)
    per = hr // tr

    if by_columns:
        def body(c_ref, a_ref, b_ref, o_ref):
            v = a_ref[...] + b_ref[...]
            for j in range(4):
                o_ref[j] = v[:, j * cols:(j + 1) * cols].astype(out_dtype)

        in_specs = [pl.BlockSpec((tr, wide), lambda i, c_ref: (c_ref[0] * per + i, 0)),
                    pl.BlockSpec((tr, wide), lambda i, c_ref: (i, 0))]
        out_specs = pl.BlockSpec((4, tr, cols), lambda i, c_ref: (0, i, 0))
        grid = (per,)
    else:
        def body(c_ref, a_ref, b_ref, o_ref):
            o_ref[...] = (a_ref[...] + b_ref[...]).astype(out_dtype)

        in_specs = [pl.BlockSpec((1, tr, cols), lambda s, i, c_ref: (s, c_ref[0] * per + i, 0)),
                    pl.BlockSpec((1, tr, cols), lambda s, i, c_ref: (s, i, 0))]
        out_specs = pl.BlockSpec((1, tr, cols), lambda s, i, c_ref: (s, i, 0))
        grid = (4, per)
    return pl.pallas_call(
        body, name=name,
        grid_spec=pltpu.PrefetchScalarGridSpec(num_scalar_prefetch=1, grid=grid, in_specs=in_specs, out_specs=out_specs),
        out_shape=S((4, hr, cols), out_dtype), compiler_params=_params(len(grid)),
    )(jnp.reshape(c, (1,)).astype(jnp.int32), full, got)


def _small_add_own_half(fulls, gots, c):
    n = len(fulls)

    def body(c_ref, *refs):
        for a_ref, b_ref, o_ref in zip(refs[:n], refs[n:2 * n], refs[2 * n:]):
            hr = b_ref.shape[0]
            o_ref[...] = a_ref[pl.ds(pl.multiple_of(c_ref[0] * hr, 8), hr), :] + b_ref[...]

    specs = lambda arrs: [pl.BlockSpec(a.shape, lambda i, c_ref: (0, 0)) for a in arrs]
    return pl.pallas_call(
        body, name="small_pair_add",
        grid_spec=pltpu.PrefetchScalarGridSpec(num_scalar_prefetch=1, grid=(1,), in_specs=specs(fulls) + specs(gots),
                                               out_specs=specs(gots)),
        out_shape=[S(g.shape, F32) for g in gots], compiler_params=_params(),
    )(jnp.reshape(c, (1,)).astype(jnp.int32), *fulls, *gots)


def _small_sum_slots(own, slots, me, c):
    n = len(slots)

    def body(p_ref, *refs):
        own_refs, slot_refs, o_refs = refs[:n], refs[n:5 * n], refs[5 * n:]
        for i, (own_ref, o_ref) in enumerate(zip(own_refs, o_refs)):
            hr = own_ref.shape[0]
            acc = None
            for j in range(4):
                v = jnp.where(p_ref[0] == j, own_ref[...], slot_refs[4 * i + j][0])
                acc = v if acc is None else acc + v
            o_ref[pl.ds(pl.multiple_of(p_ref[1] * hr, 8), hr), :] = acc

    def slot_spec(s, j):
        return pl.BlockSpec((1,) + s.shape[1:], lambda i, p: (jnp.where(p[0] == j, (j + 1) % 4, j), 0, 0))

    outs = [S((2 * s.shape[1], s.shape[2]), F32) for s in slots]
    return pl.pallas_call(
        body, name="small_quad_sum",
        grid_spec=pltpu.PrefetchScalarGridSpec(
            num_scalar_prefetch=1, grid=(1,),
            in_specs=[pl.BlockSpec(o.shape, lambda i, p: (0, 0)) for o in own]
            + [slot_spec(s, j) for s in slots for j in range(4)],
            out_specs=[pl.BlockSpec(o.shape, lambda i, p: (0, 0)) for o in outs]),
        out_shape=outs, compiler_params=_params(),
    )(jnp.stack([me, c]).astype(jnp.int32), *own, *[s for s in slots for _ in range(4)])


def _sum_slots(name, own, slots, me, c):
    _, rows, cols = slots.shape
    tr = _row_tile(rows, cols, 16 if slots.dtype == jnp.bfloat16 else 8)
    per = rows // tr

    def body(p_ref, own_ref, s0, s1, s2, s3, o_ref):
        acc = None
        for j, s_ref in enumerate((s0, s1, s2, s3)):
            v = jnp.where(p_ref[0] == j, own_ref[0], s_ref[0]).astype(F32)
            acc = v if acc is None else acc + v
        o_ref[...] = acc

    def slot_spec(j):
        return pl.BlockSpec((1, tr, cols), lambda i, p: (jnp.where(p[0] == j, (j + 1) % 4, j), i, 0))

    own_spec = pl.BlockSpec((1, tr, cols), lambda i, p: (p[0], i, 0))
    return pl.pallas_call(
        body, name=name,
        grid_spec=pltpu.PrefetchScalarGridSpec(
            num_scalar_prefetch=1, grid=(per,), in_specs=[own_spec] + [slot_spec(j) for j in range(4)],
            out_specs=pl.BlockSpec((tr, cols), lambda i, p: (p[1] * per + i, 0))),
        out_shape=S((2 * rows, cols), F32), compiler_params=_params(),
    )(jnp.stack([me, c]).astype(jnp.int32), own, slots, slots, slots, slots)


BIG = ("w_in", "w_out", "w_gate", "w_up", "w_down")
ROW_PARAMS = (("pre_mix_norm", 0), ("lru_conv_b", 12), ("lru_ba", 13), ("lru_bx", 14), ("lru_lambda", 15),
              ("lru_out_norm", 16), ("ssd_out_norm", 24), ("post_mix_norm", 33), ("pre_ffn_norm", 32), ("post_ffn_norm", 41))
LRU_CONV_ROWS = (8, 12)
LOSS_ROW = 40
HEAD_PARAMS = (("ssd_dt_bias", 0), ("ssd_a_log", 1), ("ssd_d", 2))
SMALL = tuple(n for n, _ in ROW_PARAMS) + ("ssd_conv_b",) + tuple(n for n, _ in HEAD_PARAMS) + (
    "lru_wa", "lru_wx", "lru_conv_w", "ssd_conv_w")


def _diag4(w):
    eye = jnp.eye(4, dtype=w.dtype).reshape(1, 4, 1, 4, 1)
    return (w.reshape(4, 4, BW, 1, BW) * eye).reshape(4, 4 * BW, 4 * BW)


def _adam_math(w, g, m, v):
    mm = ADAM_B1 * m + (1.0 - ADAM_B1) * g
    vv = ADAM_B2 * v + (1.0 - ADAM_B2) * (g * g)
    c1 = 1.0 - ADAM_B1 ** ADAM_STEP
    c2 = 1.0 - ADAM_B2 ** ADAM_STEP
    return -ADAM_LR * ((mm / c1) / (jnp.sqrt(vv / c2) + ADAM_EPS) + ADAM_WD * w), mm, vv


def _adamw_small(rows, cst, hst, dwa, dwx, glcw, gscw, w, m, v):
    def grad_of(name, refs):
        rows_ref, cst_ref, hst_ref, dwa_ref, dwx_ref, glcw_ref, gscw_ref = refs
        for n, r in ROW_PARAMS:
            if n == name:
                return rows_ref[r:r + 1, :]
        for n, r in HEAD_PARAMS:
            if n == name:
                return hst_ref[r:r + 1, 0:NH]
        return {"ssd_conv_b": lambda: cst_ref[4:5, :], "lru_wa": lambda: dwa_ref[...], "lru_wx": lambda: dwx_ref[...],
                "lru_conv_w": lambda: glcw_ref[...], "ssd_conv_w": lambda: gscw_ref[...]}[name]()

    shapes = {n: (w[n].shape[1:] if len(w[n].shape) > 2 else w[n].shape) for n in SMALL}
    flat = lambda d: [d[n].reshape(shapes[n]) for n in SMALL]
    ns = len(SMALL)

    def body(*refs):
        srcs, rest = refs[:7], refs[7:]
        w_refs, m_refs, v_refs = rest[:ns], rest[ns:2 * ns], rest[2 * ns:3 * ns]
        outs = rest[3 * ns:]
        for k, name in enumerate(SMALL):
            g = grad_of(name, srcs)
            d, mm, vv = _adam_math(w_refs[k][...], g, m_refs[k][...], v_refs[k][...])
            outs[4 * k][...] = g
            outs[4 * k + 1][...] = d
            outs[4 * k + 2][...] = mm
            outs[4 * k + 3][...] = vv

    res = pl.pallas_call(
        body, name="adamw_small",
        out_shape=[S(shapes[n], F32) for n in SMALL for _ in range(4)],
        compiler_params=pltpu.CompilerParams(vmem_limit_bytes=VMEM_LIMIT),
    )(rows, cst, hst, dwa, dwx, glcw, gscw, *flat(w), *flat(m), *flat(v))
    return {n: tuple(res[4 * k + i].reshape(w[n].shape) for i in range(4)) for k, n in enumerate(SMALL)}


def _with_own(own, got):
    chip = 2 * lax.axis_index("x") + lax.axis_index("y")
    return jnp.where((jnp.arange(4) == chip).reshape(4, 1, 1), own[None], got)


def _side_by_side(f):
    return f.transpose(1, 0, 2).reshape(f.shape[1], 4 * f.shape[2])


def _stacked(f):
    return f.reshape(4 * f.shape[1], f.shape[2])


def _conv_terms(lru_conv_w, ssd_conv_w):
    conv = jnp.concatenate([lru_conv_w.reshape(-1), ssd_conv_w.reshape(-1)]).astype(F32)
    hi = conv.astype(jnp.bfloat16)
    mid = (conv - hi.astype(F32)).astype(jnp.bfloat16)
    lo = (conv - hi.astype(F32) - mid.astype(F32)).astype(jnp.bfloat16)
    terms = jnp.concatenate([hi, mid, lo])
    rows = -(-terms.shape[0] // (128 * 32)) * 32
    return jnp.pad(terms, (0, rows * 128 - terms.shape[0])).reshape(rows, 128)


def _full_conv_taps(own, got, n_lru, n_ssd):
    n_terms = 3 * (n_lru + n_ssd)
    t3 = _with_own(own, got).reshape(4, -1)[:, :n_terms].reshape(4, 3, -1).astype(F32)
    conv_f = (t3[:, 0] + t3[:, 1]) + t3[:, 2]
    lcw = conv_f[:, :n_lru].reshape(4, CONV_K, -1).transpose(1, 0, 2).reshape(CONV_K, LW)
    scw = conv_f[:, n_lru:].reshape(4, CONV_K, -1).transpose(1, 0, 2).reshape(CONV_K, XBC)
    return lcw, scw


def _step(x, tgt, w_in, lru_conv_w, ssd_conv_w, sp, late):
    c = lax.axis_index("c")
    me = 2 * lax.axis_index("x") + lax.axis_index("y")
    mm = lambda w: w.astype(BF)
    row = lambda v: v.reshape(1, -1).astype(F32)
    g0 = row(sp["pre_mix_norm"])
    first = [w_in.astype(WIRE), _conv_terms(lru_conv_w, ssd_conv_w)]
    h0, *got_first = _prenorm(x, g0, first)
    win_f = _side_by_side(_with_own(first[0], got_first[0]))
    lcw, scw = _full_conv_taps(first[1], got_first[1], lru_conv_w.size, ssd_conv_w.size)
    wcat = jnp.concatenate([mm(win_f), jnp.zeros((D, PC - IN_COLS), BF)], axis=1)
    p_lru = jnp.concatenate([lcw, row(sp["lru_conv_b"]), row(sp["lru_ba"]), row(sp["lru_bx"]), row(sp["lru_lambda"]),
                             row(sp["lru_out_norm"]), jnp.zeros((7, LW), F32)], axis=0)
    wa4, wx4 = mm(_diag4(sp["lru_wa"][0])), mm(_diag4(sp["lru_wx"][0]))
    wa4T, wx4T = wa4.transpose(0, 2, 1), wx4.transpose(0, 2, 1)
    cw_ssd = jnp.concatenate([scw, row(sp["ssd_conv_b"]), jnp.zeros((3, XBC), F32)], axis=0)
    padh = lambda v: jnp.pad(row(v), ((0, 0), (0, DTP - NH)))
    hp_ssd = jnp.concatenate([padh(sp["ssd_dt_bias"]), padh(sp["ssd_a_log"]), padh(sp["ssd_d"]), jnp.zeros((5, DTP), F32)], axis=0)
    g_ssd = row(sp["ssd_out_norm"])
    g_pm, g_pf, g_pff = row(sp["post_mix_norm"]), row(sp["pre_ffn_norm"]), row(sp["post_ffn_norm"])

    h, ylru, lxc, lxr, lg, *got_a = _lru_fwd(h0, wcat, p_lru, wa4, wx4, [late[0], late[3]])
    y, yssd, states, cv, z, xbcr, dtr, *got_b = _ssd_fwd(h0, wcat, cw_ssd, hp_ssd, g_ssd, [late[1], late[2]])
    wout, wd = mm(_stacked(_with_own(late[0], got_a[0]))), mm(_stacked(_with_own(late[3], got_a[1])))
    wg, wu = mm(_side_by_side(_with_own(late[1], got_b[0]))), mm(_side_by_side(_with_own(late[2], got_b[1])))
    mix, x1, h2 = _outproj(ylru, yssd, x, wout, g_pm, g_pf)
    gate, up, act, df, dx2, st_ffn = _ffn_fwd(h2, x1, tgt, wg, wu, wd, g_pff)
    dgate, dup, dh2 = _ffn_bwd(df, gate, up, wd.T, wg.T, wu.T)
    dx1, dmix, dyl, dys, st_mix = _mix_bwd(dh2, x1, dx2, mix, wout.T, g_pf, g_pm)

    dwg = _wgrad("wgrad_gate", h2, dgate)
    dwu = _wgrad("wgrad_up", h2, dup)
    dwd = _wgrad("wgrad_down", act, df)
    dwo = jnp.concatenate([_wgrad("wgrad_out_lru", ylru, dmix), _wgrad("wgrad_out_ssd", yssd, dmix)], axis=0)
    early = [dwo.reshape(4, (LW + SI) // 4, D), dwg, dwu, dwd.reshape(4, DFF // 4, D)]
    dlx, dlg, st_lru, dwa, dwx, *got_early = _lru_bwd(dyl, lxr, lxc, lg, h, p_lru, wa4, wx4, wa4T, wx4T, early)
    part_early = [_add_own_half("pair_add_early%d" % k, b, r, c, WIRE, bc)
                  for k, (b, r, bc) in enumerate(zip(early, got_early, [False, True, True, False]))]
    dxbc, dz, ddt, cst, hst, gst, *slots_early = _ssd_bwd(dys, xbcr, cv, z, dtr, y, states, cw_ssd, hp_ssd, g_ssd,
                                                          part_early)
    red_early = [_sum_slots("quad_sum_early%d" % k, p, s, me, c) for k, (p, s) in enumerate(zip(part_early, slots_early))]

    pin = [_wgrad("wgrad_in_%d" % k, h0, b) for k, b in enumerate((dlx, dlg, dz, dxbc, ddt))]
    dwin = jnp.concatenate(pin[:4] + [pin[4][:, :NH]], axis=1)
    (got_win,) = _pair_exchange("pair_exchange_w_in", [dwin])
    part_win = _add_own_half("pair_add_w_in", dwin, got_win, c, WIRE, True)
    gx, st_in, slots_win = _inproj_bwd(dlx, dlg, dz, dxbc, ddt, x, dx1, wcat.T, g0, [part_win])
    red_win = _sum_slots("quad_sum_w_in", part_win, slots_win, me, c)

    rows = jnp.concatenate([st_in, st_lru, gst, st_mix, st_ffn], axis=0)
    small = [rows, cst, hst, dwa.reshape(NBLK * BW, BW), dwx.reshape(NBLK * BW, BW)]
    part_small = list(_small_add_own_half(small, list(_pair_exchange("pair_exchange_small", small)), c))
    red_small = list(_small_sum_slots(part_small, list(_quad_exchange(part_small)), me, c))
    out = list(_pair_gather([red_win] + red_early + red_small))
    big = dict(zip(("w_in", "w_out", "w_gate", "w_up", "w_down"), out[:5]))
    return gx, big, out[5:]


def kernel(x, pre_mix_norm, w_in, lru_conv_w, lru_conv_b, lru_wa, lru_ba, lru_wx, lru_bx, lru_lambda, lru_out_norm, ssd_conv_w, ssd_conv_b, ssd_dt_bias, ssd_a_log, ssd_d, ssd_out_norm, w_out, post_mix_norm, pre_ffn_norm, w_gate, w_up, w_down, post_ffn_norm, loss_target, m_pre_mix_norm, m_w_in, m_lru_conv_w, m_lru_conv_b, m_lru_wa, m_lru_ba, m_lru_wx, m_lru_bx, m_lru_lambda, m_lru_out_norm, m_ssd_conv_w, m_ssd_conv_b, m_ssd_dt_bias, m_ssd_a_log, m_ssd_d, m_ssd_out_norm, m_w_out, m_post_mix_norm, m_pre_ffn_norm, m_w_gate, m_w_up, m_w_down, m_post_ffn_norm, v_pre_mix_norm, v_w_in, v_lru_conv_w, v_lru_conv_b, v_lru_wa, v_lru_ba, v_lru_wx, v_lru_bx, v_lru_lambda, v_lru_out_norm, v_ssd_conv_w, v_ssd_conv_b, v_ssd_dt_bias, v_ssd_a_log, v_ssd_d, v_ssd_out_norm, v_w_out, v_post_mix_norm, v_pre_ffn_norm, v_w_gate, v_w_up, v_w_down, v_post_ffn_norm):
    args = dict(locals())
    names = list(SMALL) + list(BIG)
    w = {n: args[n] for n in names}
    m = {n: args["m_" + n] for n in names}
    v = {n: args["v_" + n] for n in names}
    chip = 2 * lax.axis_index("x") + lax.axis_index("y")

    late = [a[0].astype(WIRE) for a in (w_out, w_gate, w_up, w_down)]
    gx, red, (rows, cst, hst, dwa, dwx) = _step(x[0], loss_target[0], w_in[0], lru_conv_w[0], ssd_conv_w[0],
                                                {n: w[n] for n in SMALL}, late)
    loss = jnp.sum(rows[LOSS_ROW])

    grads, delta, new_m, new_v = {}, {}, {}, {}
    for n in BIG:
        g = red[n]
        if n in ("w_in", "w_gate", "w_up"):
            t = lambda a: jnp.swapaxes(a, 1, 2)
            gt = g.T
            out = _adamw("adamw_" + n, t(w[n]), gt, t(m[n]), t(v[n]))
            delta[n], new_m[n], new_v[n] = (t(o) for o in out)
            grads[n] = t(gt[None])
        else:
            delta[n], new_m[n], new_v[n] = _adamw("adamw_" + n, w[n], g, m[n], v[n])
            grads[n] = g[None]

    lc, sc = lru_conv_w.shape[-1], ssd_conv_w.shape[-1]
    glcw = lax.dynamic_slice_in_dim(rows[LRU_CONV_ROWS[0]:LRU_CONV_ROWS[1]], chip * lc, lc, axis=1)
    gscw = lax.dynamic_slice_in_dim(cst[0:CONV_K], chip * sc, sc, axis=1)
    res = _adamw_small(rows, cst, hst, dwa.reshape(NBLK, BW, BW), dwx.reshape(NBLK, BW, BW), glcw, gscw,
                       {n: w[n] for n in SMALL}, {n: m[n] for n in SMALL}, {n: v[n] for n in SMALL})
    for n in SMALL:
        grads[n], delta[n], new_m[n], new_v[n] = res[n]

    order = ["pre_mix_norm", "w_in", "lru_conv_w", "lru_conv_b", "lru_wa", "lru_ba", "lru_wx", "lru_bx", "lru_lambda",
             "lru_out_norm", "ssd_conv_w", "ssd_conv_b", "ssd_dt_bias", "ssd_a_log", "ssd_d", "ssd_out_norm", "w_out",
             "post_mix_norm", "pre_ffn_norm", "w_gate", "w_up", "w_down", "post_ffn_norm"]
    return (loss, gx[None], *[grads[n] for n in order], *[delta[n] for n in order],
            *[new_m[n] for n in order], *[new_v[n] for n in order])
```

```python
import functools

import jax
import jax.numpy as jnp
from jax import lax
from jax.experimental import pallas as pl
from jax.experimental.pallas import tpu as pltpu

F32 = jnp.float32
BF = jnp.bfloat16

D = 1024
LW = 1024
NBLK = 16
BW = 64
SI = 1024
NH = 16
HD = 64
NG = 2
HPG = NH // NG
NS = 128
CH = 128
XBC = SI + 2 * NG * NS
DTP = 128
PC = 3 * 1024 + XBC + DTP
DFF = 2816
IN_COLS = 4624
EPS = 1e-6
LRU_C = 8.0
CONV_K = 4
TT = 256
TW = 512
VMEM_LIMIT = 56 * 1024 * 1024

ADAM_LR, ADAM_B1, ADAM_B2, ADAM_EPS, ADAM_WD, ADAM_STEP = 0.001, 0.9, 0.999, 1e-08, 0.01, 10

MESH = pl.DeviceIdType.MESH


def _mm(a, b):
    return jnp.dot(a.astype(BF), b.astype(BF), preferred_element_type=F32)


def _mm_nt(a, b):
    return lax.dot_general(a.astype(BF), b.astype(BF), (((1,), (1,)), ((), ())), preferred_element_type=F32)


def _mm_tn(a, b):
    return lax.dot_general(a.astype(BF), b.astype(BF), (((0,), (0,)), ((), ())), preferred_element_type=F32)


def _sigmoid(x):
    return 0.5 * jnp.tanh(0.5 * x) + 0.5


def _softplus(x):
    return jnp.maximum(x, 0.0) + jnp.log1p(jnp.exp(-jnp.abs(x)))


_GELU_C = 0.7978845608028654
_GELU_K = 0.044715


def _gelu(x):
    t = jnp.tanh(_GELU_C * (x + _GELU_K * x * x * x))
    return 0.5 * x * (1.0 + t)


def _gelu_grad(x):
    t = jnp.tanh(_GELU_C * (x + _GELU_K * x * x * x))
    return 0.5 * (1.0 + t) + 0.5 * x * (1.0 - t * t) * _GELU_C * (1.0 + 3.0 * _GELU_K * x * x)


def _rms_fwd(x, g):
    r = lax.rsqrt(jnp.mean(x * x, axis=-1, keepdims=True) + EPS)
    return x * r * g


def _rms_bwd(x, g, dy):
    r = lax.rsqrt(jnp.mean(x * x, axis=-1, keepdims=True) + EPS)
    xh = x * r
    dxh = dy * g
    dg = jnp.sum(dy * xh, axis=0, keepdims=True)
    dx = r * (dxh - xh * jnp.mean(dxh * xh, axis=-1, keepdims=True))
    return dx, dg


def _sum_all(x):
    return jnp.sum(jnp.sum(x, axis=1, keepdims=True), axis=0, keepdims=True)


def _cumsum_rows(x, n):
    row = lax.broadcasted_iota(jnp.int32, x.shape, 0)
    k = 1
    while k < n:
        x = x + jnp.where(row >= k, pltpu.roll(x, k, 0), 0.0)
        k *= 2
    return x


def _rev_cumsum_rows(x, n):
    row = lax.broadcasted_iota(jnp.int32, x.shape, 0)
    k = 1
    while k < n:
        x = x + jnp.where(row < n - k, pltpu.roll(x, n - k, 0), 0.0)
        k *= 2
    return x


def _load_once(pairs, sem):
    @pl.when(pl.program_id(0) == 0)
    def _():
        for k, (src, dst) in enumerate(pairs):
            pltpu.make_async_copy(src, dst, sem.at[k]).start()
        for k, (src, dst) in enumerate(pairs):
            pltpu.make_async_copy(src, dst, sem.at[k]).wait()


def _params(n_axes=1):
    return pltpu.CompilerParams(dimension_semantics=("arbitrary",) * n_axes, vmem_limit_bytes=VMEM_LIMIT)


def _rows(n, width, rev_of=None):
    if rev_of is None:
        return pl.BlockSpec((n, width), lambda i: (i, 0))
    return pl.BlockSpec((n, width), lambda i: (rev_of - 1 - i, 0))


def _whole(shape):
    nd = len(shape)
    return pl.BlockSpec(shape, lambda i: (0,) * nd)


ANY = pl.BlockSpec(memory_space=pl.ANY)
S = jax.ShapeDtypeStruct
WIRE = jnp.bfloat16


def _pos():
    return lax.axis_index("x"), lax.axis_index("y"), lax.axis_index("c")


def _other_chips(x, y):
    return [(1 - x, y), (x, 1 - y), (1 - x, 1 - y)]


def _remote(src, dst, send_sem, recv_sem, to):
    return pltpu.make_async_remote_copy(src_ref=src, dst_ref=dst, send_sem=send_sem, recv_sem=recv_sem,
                                        device_id=to, device_id_type=MESH)


def _gather_phase(phase, ins, outs, send_sems, recv_sems):
    x, y, c = _pos()
    me = 2 * x + y
    chips = _other_chips(x, y)
    for i, (src, dst) in enumerate(zip(ins, outs)):
        hr = src.shape[0] // 2
        my_half = pl.ds(pl.multiple_of(c * hr, 16), hr)
        sib_half = pl.ds(pl.multiple_of((1 - c) * hr, 16), hr)
        for k, (cx, cy) in enumerate(chips):
            s1, r1 = send_sems.at[6 * i + k], recv_sems.at[6 * i + k]
            s2, r2 = send_sems.at[6 * i + 3 + k], recv_sems.at[6 * i + 3 + k]
            first = lambda: _remote(src.at[my_half, :], dst.at[me, my_half, :], s1, r1, (cx, cy, c))
            landed = dst.at[2 * cx + cy, my_half, :]
            passed = lambda: _remote(landed, landed, s2, r2, (x, y, 1 - c))
            if phase == 0:
                first().start()
            elif phase == 1:
                _remote(landed, landed, s1, r1, (cx, cy, c)).wait_recv()
                passed().start()
            else:
                theirs = dst.at[2 * cx + cy, sib_half, :]
                _remote(theirs, theirs, s2, r2, (x, y, 1 - c)).wait_recv()
                first().wait_send()
                passed().wait_send()


def _half(ref, c, hr):
    sl = pl.ds(pl.multiple_of(c * hr, 8), hr)
    return ref.at[:, sl, :] if len(ref.shape) == 3 else ref.at[sl, :]


def _half_shape(b):
    return b.shape[:-2] + (b.shape[-2] // 2, b.shape[-1])


def _pair_phase(phase, ins, outs, send_sems, recv_sems):
    x, y, c = _pos()
    for k, (src, dst) in enumerate(zip(ins, outs)):
        cp = _remote(_half(src, 1 - c, src.shape[-2] // 2), dst, send_sems.at[k], recv_sems.at[k], (x, y, 1 - c))
        if phase == 0:
            cp.start()
        else:
            cp.wait()


def _quad_phase(phase, ins, outs, send_sems, recv_sems):
    x, y, c = _pos()
    me = 2 * x + y
    for i, (src, dst) in enumerate(zip(ins, outs)):
        for k, (cx, cy) in enumerate(_other_chips(x, y)):
            cp = _remote(src.at[2 * cx + cy], dst.at[me], send_sems.at[3 * i + k], recv_sems.at[3 * i + k], (cx, cy, c))
            if phase == 0:
                cp.start()
            else:
                got = dst.at[2 * cx + cy]
                _remote(got, got, send_sems.at[3 * i + k], recv_sems.at[3 * i + k], (cx, cy, c)).wait_recv()
                cp.wait_send()


def _prenorm(x, g0, shards):
    T = x.shape[0]
    tt = 2 * TT
    nt = T // tt
    ng = len(shards)

    def body(*refs):
        x_ref, g_ref = refs[:2]
        sh_in = refs[2:2 + ng]
        h0_ref = refs[2 + ng]
        sh_out = refs[3 + ng:3 + 2 * ng]
        send_sems, recv_sems = refs[3 + 2 * ng:]
        for phase, step in enumerate((0, nt // 2, nt - 1)):
            @pl.when(pl.program_id(0) == step)
            def _():
                _gather_phase(phase, sh_in, sh_out, send_sems, recv_sems)

        h0_ref[...] = _rms_fwd(x_ref[...], g_ref[...]).astype(BF)

    return pl.pallas_call(
        body, name="prenorm", grid=(nt,),
        in_specs=[_rows(tt, D), _whole((1, D))] + [ANY] * ng, out_specs=[_rows(tt, D)] + [ANY] * ng,
        out_shape=[S((T, D), BF)] + [S((4,) + s.shape, s.dtype) for s in shards],
        scratch_shapes=[pltpu.SemaphoreType.DMA((6 * ng,)), pltpu.SemaphoreType.DMA((6 * ng,))],
        compiler_params=_params(),
    )(x, g0, *shards)


def _blockdiag_mm(v, w4_ref):
    return jnp.concatenate([_mm(v[:, 256 * j:256 * (j + 1)], w4_ref[j]) for j in range(4)], axis=1)


def _lru_gates(lx, p_ref, wa_ref, wx_ref):
    r = _sigmoid(_blockdiag_mm(lx, wa_ref) + p_ref[5:6, :])
    i = _sigmoid(_blockdiag_mm(lx, wx_ref) + p_ref[6:7, :])
    sp = _softplus(-p_ref[7:8, :])
    la = -LRU_C * r * sp
    a = jnp.exp(la)
    th = jnp.tanh(la)
    mult = jnp.sqrt(-2.0 * th / (1.0 - th))
    return r, i, sp, a, mult


def _conv_from(xp_ref, p_ref, n):
    acc = p_ref[4:5, :] + p_ref[0:1, :] * xp_ref[pl.ds(8 - CONV_K + 1, n), :]
    for k in range(1, CONV_K):
        acc = acc + p_ref[k:k + 1, :] * xp_ref[pl.ds(8 - CONV_K + 1 + k, n), :]
    return acc


def _conv_bwd(dp_ref, dconv, x, p_ref, st_ref, n):
    if dconv is None:
        dconv = dp_ref[0:n, :]
    else:
        dp_ref[0:n, :] = dconv
    acc = None
    for k in range(CONV_K):
        g = dp_ref[pl.ds(CONV_K - 1 - k, n), :]
        acc = p_ref[k:k + 1, :] * g if acc is None else acc + p_ref[k:k + 1, :] * g
        st_ref[k:k + 1, :] += jnp.sum(g * x, axis=0, keepdims=True)
    st_ref[4:5, :] += jnp.sum(dconv, axis=0, keepdims=True)
    dp_ref[n:n + 8, :] = dp_ref[0:8, :]
    return acc


def _lru_fwd(h0, wcat, p_lru, wa4, wx4, shards):
    T = h0.shape[0]
    NT = T // TT
    ng = len(shards)

    def body(*refs):
        h0_ref, w_hbm, p_ref, wa_ref, wx_ref = refs[:5]
        sh_in = refs[5:5 + ng]
        h_ref, y_ref, lxc_ref, lxr_ref, lg_ref = refs[5 + ng:10 + ng]
        sh_out = refs[10 + ng:10 + 2 * ng]
        xp, a_s, u_s, hc, w_vm, wsem, send_sems, recv_sems = refs[10 + 2 * ng:]
        _load_once([(w_hbm.at[:, 0:2 * LW], w_vm)], wsem)
        for phase, step in enumerate((0, NT // 2, NT - 1)):
            @pl.when(pl.program_id(0) == step)
            def _():
                _gather_phase(phase, sh_in, sh_out, send_sems, recv_sems)

        @pl.when(pl.program_id(0) == 0)
        def _():
            xp[0:8, :] = jnp.zeros((8, LW), F32)
            hc[...] = jnp.zeros_like(hc)

        hv = h0_ref[...]
        lxr = jnp.dot(hv, w_vm[:, 0:LW], preferred_element_type=F32)
        lxr_ref[...] = lxr
        lg_ref[...] = jnp.dot(hv, w_vm[:, LW:2 * LW], preferred_element_type=F32)
        xp[8:8 + TT, :] = lxr
        lx = _conv_from(xp, p_ref, TT)
        lxc_ref[...] = lx
        xp[0:8, :] = xp[TT:TT + 8, :]
        r, i, sp, a, mult = _lru_gates(lx, p_ref, wa_ref, wx_ref)
        a_s[...] = a
        u_s[...] = mult * (i * lx)

        def step(t, h):
            h = a_s[pl.ds(t, 1), :] * h + u_s[pl.ds(t, 1), :]
            h_ref[pl.ds(t, 1), :] = h
            return h

        hc[0:1, :] = lax.fori_loop(0, TT, step, hc[0:1, :], unroll=8)
        gated = h_ref[...] * _gelu(lg_ref[...])
        y_ref[...] = _rms_fwd(gated, p_ref[8:9, :]).astype(BF)

    return pl.pallas_call(
        body, name="lru_fwd", grid=(NT,),
        in_specs=[_rows(TT, D), ANY, _whole((16, LW)), _whole((4, 256, 256)), _whole((4, 256, 256))] + [ANY] * ng,
        out_specs=[_rows(TT, LW), _rows(TT, LW), _rows(TT, LW), _rows(TT, LW), _rows(TT, LW)] + [ANY] * ng,
        out_shape=[S((T, LW), F32), S((T, LW), BF), S((T, LW), F32), S((T, LW), F32), S((T, LW), F32)]
        + [S((4,) + s.shape, s.dtype) for s in shards],
        scratch_shapes=[pltpu.VMEM((TT + 8, LW), F32), pltpu.VMEM((TT, LW), F32), pltpu.VMEM((TT, LW), F32),
                        pltpu.VMEM((8, LW), F32), pltpu.VMEM((D, 2 * LW), BF), pltpu.SemaphoreType.DMA((1,)),
                        pltpu.SemaphoreType.DMA((6 * ng,)), pltpu.SemaphoreType.DMA((6 * ng,))],
        compiler_params=_params(),
    )(h0, wcat, p_lru, wa4, wx4, *shards)


def _ssd_prep(cv, dt_ref, hp_ref):
    sg = _sigmoid(cv)
    xbc = cv * sg
    lane = lax.broadcasted_iota(jnp.int32, (CH, DTP), 1)
    raw = dt_ref[...] + hp_ref[0:1, :]
    dtv = jnp.where(lane < NH, _softplus(raw), 0.0)
    A = jnp.where(lane[0:1, :] < NH, -jnp.exp(hp_ref[1:2, :]), 0.0)
    cs = _cumsum_rows(dtv * A, CH)
    return sg, xbc, raw, dtv, A, cs


def _per_head_lanes(v):
    r = v.shape[0]
    first = lax.broadcasted_iota(jnp.int32, (r, 2 * HD), 1) < HD
    pairs = [jnp.where(first, jnp.broadcast_to(v[:, 2 * j:2 * j + 1], (r, 2 * HD)),
                       jnp.broadcast_to(v[:, 2 * j + 1:2 * j + 2], (r, 2 * HD))) for j in range(NH // 2)]
    return jnp.concatenate(pairs, axis=1)


def _per_head_rows(col, g):
    return jnp.concatenate([jnp.broadcast_to(col[g * HPG + k:g * HPG + k + 1, :], (HD, NS)) for k in range(HPG)], axis=0)


def _ssd_decays(cs):
    csT = cs.T
    cl = cs[CH - 1:CH, :]
    E_x = _per_head_lanes(jnp.exp(cs))
    dsm = jnp.exp(cl - cs)
    ds_x = _per_head_lanes(dsm)
    El_rows = jnp.broadcast_to(jnp.exp(csT[0:NH, CH - 1:CH]), (NH, NS))
    return csT, dsm, E_x, ds_x, El_rows


def _ssd_fwd(h0, wcat, cw_ssd, hp_ssd, g_ssd, shards):
    T = h0.shape[0]
    NC = T // CH
    ng = len(shards)
    c0 = 2 * LW

    def body(*refs):
        h0_ref, w_hbm, cw_ref, hp_ref, g_ref = refs[:5]
        sh_in = refs[5:5 + ng]
        y_ref, yn_ref, st_ref, cv_ref, z_ref, xr_ref, dt_ref = refs[5 + ng:12 + ng]
        sh_out = refs[12 + ng:12 + 2 * ng]
        xp, st, w_vm, wsem, send_sems, recv_sems = refs[12 + 2 * ng:]
        _load_once([(w_hbm.at[:, c0:PC], w_vm)], wsem)
        for phase, step in enumerate((0, NC // 2, NC - 1)):
            @pl.when(pl.program_id(0) == step)
            def _():
                _gather_phase(phase, sh_in, sh_out, send_sems, recv_sems)

        @pl.when(pl.program_id(0) == 0)
        def _():
            xp[0:8, :] = jnp.zeros((8, XBC), F32)
            st[...] = jnp.zeros_like(st)

        hv = h0_ref[...]
        z_ref[...] = jnp.dot(hv, w_vm[:, 0:SI], preferred_element_type=F32)
        xraw = jnp.dot(hv, w_vm[:, SI:SI + XBC], preferred_element_type=F32)
        xr_ref[...] = xraw
        dt_ref[...] = jnp.dot(hv, w_vm[:, SI + XBC:SI + XBC + DTP], preferred_element_type=F32)
        xp[8:8 + CH, :] = xraw
        cv = _conv_from(xp, cw_ref, CH)
        cv_ref[...] = cv
        sg, xbc, raw, dtv, A, cs = _ssd_prep(cv, dt_ref, hp_ref)
        xp[0:8, :] = xp[CH:CH + 8, :]
        st_ref[0] = st[...]
        csT, dsm, E_x, ds_x, El_rows = _ssd_decays(cs)
        X = xbc[:, 0:SI]
        xs = X * _per_head_lanes(dtv)
        xsd = (xs * ds_x).astype(BF)
        DX = _per_head_lanes(hp_ref[...])[2:3, :] * X
        tril = lax.broadcasted_iota(jnp.int32, (CH, CH), 0) >= lax.broadcasted_iota(jnp.int32, (CH, CH), 1)
        first = lax.broadcasted_iota(jnp.int32, (CH, 2 * HD), 1) < HD
        GW = HPG * HD
        for g in range(NG):
            Bg = xbc[:, SI + NS * g:SI + NS * (g + 1)].astype(BF)
            Cg = xbc[:, SI + NG * NS + NS * g:SI + NG * NS + NS * (g + 1)].astype(BF)
            G = _mm_nt(Cg, Bg)
            Sg = st[GW * g:GW * (g + 1), :]
            Yo = _mm_nt(Cg, Sg) * E_x[:, GW * g:GW * (g + 1)]
            st[GW * g:GW * (g + 1), :] = _per_head_rows(El_rows, g) * Sg + _mm_tn(xsd[:, GW * g:GW * (g + 1)], Bg)
            for jj in range(HPG // 2):
                j = g * (HPG // 2) + jj
                ps = slice(2 * HD * j, 2 * HD * (j + 1))
                xs_pair = xs[:, ps]
                acc = Yo[:, 2 * HD * jj:2 * HD * (jj + 1)] + DX[:, ps]
                for e in range(2):
                    h = 2 * j + e
                    Lm = jnp.exp(jnp.where(tril, cs[:, h:h + 1] - csT[h:h + 1, :], -1e30))
                    acc = acc + _mm(G * Lm, jnp.where(first if e == 0 else ~first, xs_pair, 0.0))
                y_ref[:, ps] = acc
        zz = z_ref[...]
        gated = y_ref[...] * (zz * _sigmoid(zz))
        yn_ref[...] = _rms_fwd(gated, g_ref[...]).astype(BF)

    return pl.pallas_call(
        body, name="ssd_fwd", grid=(NC,),
        in_specs=[_rows(CH, D), ANY, _whole((8, XBC)), _whole((8, DTP)), _whole((1, SI))] + [ANY] * ng,
        out_specs=[_rows(CH, SI), _rows(CH, SI), pl.BlockSpec((1, NH * HD, NS), lambda i: (i, 0, 0)), _rows(CH, XBC),
                   _rows(CH, SI), _rows(CH, XBC), _rows(CH, DTP)] + [ANY] * ng,
        out_shape=[S((T, SI), F32), S((T, SI), BF), S((NC, NH * HD, NS), F32), S((T, XBC), F32),
                   S((T, SI), F32), S((T, XBC), F32), S((T, DTP), F32)] + [S((4,) + s.shape, s.dtype) for s in shards],
        scratch_shapes=[pltpu.VMEM((CH + 8, XBC), F32), pltpu.VMEM((NH * HD, NS), F32),
                        pltpu.VMEM((D, PC - c0), BF), pltpu.SemaphoreType.DMA((1,)),
                        pltpu.SemaphoreType.DMA((6 * ng,)), pltpu.SemaphoreType.DMA((6 * ng,))],
        compiler_params=_params(),
    )(h0, wcat, cw_ssd, hp_ssd, g_ssd, *shards)


def _outproj(ylru, yssd, x, wout, g_pm, g_pf):
    T = x.shape[0]

    def body(yl_ref, ys_ref, x_ref, w_hbm, gpm_ref, gpf_ref, mix_ref, x1_ref, h2_ref, w_vm, sem):
        _load_once([(w_hbm, w_vm)], sem)
        mix = (jnp.dot(yl_ref[...], w_vm[0:LW, :], preferred_element_type=F32)
               + jnp.dot(ys_ref[...], w_vm[LW:LW + SI, :], preferred_element_type=F32))
        mix_ref[...] = mix
        x1 = x_ref[...] + _rms_fwd(mix, gpm_ref[...])
        x1_ref[...] = x1
        h2_ref[...] = _rms_fwd(x1, gpf_ref[...]).astype(BF)

    return pl.pallas_call(
        body, name="outproj", grid=(T // TW,),
        in_specs=[_rows(TW, LW), _rows(TW, SI), _rows(TW, D), ANY, _whole((1, D)), _whole((1, D))],
        out_specs=[_rows(TW, D), _rows(TW, D), _rows(TW, D)],
        out_shape=[S((T, D), F32), S((T, D), F32), S((T, D), BF)],
        scratch_shapes=[pltpu.VMEM((LW + SI, D), BF), pltpu.SemaphoreType.DMA((1,))],
        compiler_params=_params(),
    )(ylru, yssd, x, wout, g_pm, g_pf)


def _ffn_fwd(h2, x1, tgt, wg, wu, wd, g_pff):
    T = x1.shape[0]

    def body(h2_ref, x1_ref, t_ref, wg_hbm, wu_hbm, wd_hbm, g_ref,
             gate_ref, up_ref, act_ref, df_ref, dx2_ref, st_ref, wg_vm, wu_vm, wd_vm, sem):
        _load_once([(wg_hbm, wg_vm), (wu_hbm, wu_vm), (wd_hbm, wd_vm)], sem)

        @pl.when(pl.program_id(0) == 0)
        def _():
            st_ref[...] = jnp.zeros_like(st_ref)

        h2 = h2_ref[...]
        gate = jnp.dot(h2, wg_vm[...], preferred_element_type=F32)
        up = jnp.dot(h2, wu_vm[...], preferred_element_type=F32)
        gate_ref[...] = gate
        up_ref[...] = up
        act = (gate * _sigmoid(gate) * up).astype(BF)
        act_ref[...] = act
        f = jnp.dot(act, wd_vm[...], preferred_element_type=F32)
        g = g_ref[...]
        x2 = x1_ref[...] + _rms_fwd(f, g)
        err = x2 - t_ref[...]
        st_ref[0:1, :] += 0.5 * jnp.sum(err * err, axis=0, keepdims=True) * (1.0 / D)
        dx2 = err * (1.0 / D)
        dx2_ref[...] = dx2
        df, dg = _rms_bwd(f, g, dx2)
        df_ref[...] = df.astype(BF)
        st_ref[1:2, :] += dg

    return pl.pallas_call(
        body, name="ffn_fwd", grid=(T // TT,),
        in_specs=[_rows(TT, D), _rows(TT, D), _rows(TT, D), ANY, ANY, ANY, _whole((1, D))],
        out_specs=[_rows(TT, DFF), _rows(TT, DFF), _rows(TT, DFF), _rows(TT, D), _rows(TT, D), _whole((8, D))],
        out_shape=[S((T, DFF), F32), S((T, DFF), F32), S((T, DFF), BF), S((T, D), BF), S((T, D), F32), S((8, D), F32)],
        scratch_shapes=[pltpu.VMEM((D, DFF), BF), pltpu.VMEM((D, DFF), BF), pltpu.VMEM((DFF, D), BF),
                        pltpu.SemaphoreType.DMA((3,))],
        compiler_params=_params(),
    )(h2, x1, tgt, wg, wu, wd, g_pff)


def _ffn_bwd(df, gate, up, wdT, wgT, wuT):
    T = df.shape[0]

    def body(df_ref, gate_ref, up_ref, wd_hbm, wg_hbm, wu_hbm, dgate_ref, dup_ref, dh2_ref, wd_vm, wg_vm, wu_vm, sem):
        _load_once([(wd_hbm, wd_vm), (wg_hbm, wg_vm), (wu_hbm, wu_vm)], sem)
        dact = jnp.dot(df_ref[...], wd_vm[...], preferred_element_type=F32)
        gate = gate_ref[...]
        s = _sigmoid(gate)
        dup = (dact * (gate * s)).astype(BF)
        dgate = (dact * up_ref[...] * (s + gate * s * (1.0 - s))).astype(BF)
        dup_ref[...] = dup
        dgate_ref[...] = dgate
        dh2_ref[...] = (jnp.dot(dgate, wg_vm[...], preferred_element_type=F32)
                        + jnp.dot(dup, wu_vm[...], preferred_element_type=F32))

    return pl.pallas_call(
        body, name="ffn_bwd", grid=(T // TT,),
        in_specs=[_rows(TT, D), _rows(TT, DFF), _rows(TT, DFF), ANY, ANY, ANY],
        out_specs=[_rows(TT, DFF), _rows(TT, DFF), _rows(TT, D)],
        out_shape=[S((T, DFF), BF), S((T, DFF), BF), S((T, D), F32)],
        scratch_shapes=[pltpu.VMEM((D, DFF), BF), pltpu.VMEM((DFF, D), BF), pltpu.VMEM((DFF, D), BF),
                        pltpu.SemaphoreType.DMA((3,))],
        compiler_params=_params(),
    )(df, gate, up, wdT, wgT, wuT)


def _mix_bwd(dh2, x1, dx2, mix, woutT, g_pf, g_pm):
    T = x1.shape[0]

    def body(dh2_ref, x1_ref, dx2_ref, mix_ref, w_hbm, gpf_ref, gpm_ref,
             dx1_ref, dmix_ref, dyl_ref, dys_ref, st_ref, w_vm, sem):
        _load_once([(w_hbm, w_vm)], sem)

        @pl.when(pl.program_id(0) == 0)
        def _():
            st_ref[...] = jnp.zeros_like(st_ref)

        dxa, dgpf = _rms_bwd(x1_ref[...], gpf_ref[...], dh2_ref[...])
        dx1 = dx2_ref[...] + dxa
        dx1_ref[...] = dx1
        dmix, dgpm = _rms_bwd(mix_ref[...], gpm_ref[...], dx1)
        dmix = dmix.astype(BF)
        dmix_ref[...] = dmix
        st_ref[0:1, :] += dgpf
        st_ref[1:2, :] += dgpm
        dyl_ref[...] = jnp.dot(dmix, w_vm[:, 0:LW], preferred_element_type=F32)
        dys_ref[...] = jnp.dot(dmix, w_vm[:, LW:LW + SI], preferred_element_type=F32)

    return pl.pallas_call(
        body, name="mix_bwd", grid=(T // TW,),
        in_specs=[_rows(TW, D), _rows(TW, D), _rows(TW, D), _rows(TW, D), ANY, _whole((1, D)), _whole((1, D))],
        out_specs=[_rows(TW, D), _rows(TW, D), _rows(TW, LW), _rows(TW, SI), _whole((8, D))],
        out_shape=[S((T, D), F32), S((T, D), BF), S((T, LW), F32), S((T, SI), F32), S((8, D), F32)],
        scratch_shapes=[pltpu.VMEM((D, LW + SI), BF), pltpu.SemaphoreType.DMA((1,))],
        compiler_params=_params(),
    )(dh2, x1, dx2, mix, woutT, g_pf, g_pm)


def _halo(width, n_tiles, tile):
    per = tile // 8
    return pl.BlockSpec((8, width), lambda i: (jnp.maximum((n_tiles - 1 - i) * per - 1, 0), 0))


def _lru_bwd(dy, lxr, lxc, lg, h, p_lru, wa4, wx4, wa4T, wx4T, bufs):
    T = dy.shape[0]
    NT = T // TT
    nb = len(bufs)

    def body(*refs):
        dy_ref, lxr_ref, lxc_ref, lg_ref, h_ref, hh_ref, p_ref, wa_ref, wx_ref, waT_ref, wxT_ref = refs[:11]
        b_in = refs[11:11 + nb]
        dlx_ref, dlg_ref, st_ref, dwa_ref, dwx_ref = refs[11 + nb:16 + nb]
        b_out = refs[16 + nb:16 + 2 * nb]
        hp, dp, a_s, d_s, g_s, cc, send_sems, recv_sems = refs[16 + 2 * nb:]
        for phase, step in enumerate((0, NT - 1)):
            @pl.when(pl.program_id(0) == step)
            def _():
                _pair_phase(phase, b_in, b_out, send_sems, recv_sems)

        dy = dy_ref[...]
        first = pl.program_id(0) == 0
        top = pl.program_id(0) == NT - 1

        @pl.when(first)
        def _():
            st_ref[...] = jnp.zeros_like(st_ref)
            dwa_ref[...] = jnp.zeros_like(dwa_ref)
            dwx_ref[...] = jnp.zeros_like(dwx_ref)
            dp[TT:TT + 8, :] = jnp.zeros((8, LW), F32)
            cc[...] = jnp.zeros_like(cc)

        hp[0:8, :] = hh_ref[...] * jnp.where(top, 0.0, 1.0)
        hp[8:8 + TT, :] = h_ref[...]
        lx = lxc_ref[...]
        r, i, sp, a, mult = _lru_gates(lx, p_ref, wa_ref, wx_ref)

        lg = lg_ref[...]
        hcur = h_ref[...]
        ge = _gelu(lg)
        dgated, dgn = _rms_bwd(hcur * ge, p_ref[8:9, :], dy)
        st_ref[8:9, :] += dgn
        dlg_ref[...] = (dgated * hcur * _gelu_grad(lg)).astype(BF)
        a_s[...] = a
        d_s[...] = dgated * ge

        def step(k, c):
            t = TT - 1 - k
            g = d_s[pl.ds(t, 1), :] + c
            g_s[pl.ds(t, 1), :] = g
            return a_s[pl.ds(t, 1), :] * g

        cc[0:1, :] = lax.fori_loop(0, TT, step, cc[0:1, :], unroll=8)
        gt = g_s[...]
        da = gt * hp[pl.ds(7, TT), :]
        dmult = gt * i * lx
        di = gt * mult * lx
        dlxc = gt * mult * i
        dla = da * a - dmult * (a * a) / mult
        dr = dla * (-LRU_C * sp)
        st_ref[7:8, :] += jnp.sum(dla * (-LRU_C * r), axis=0, keepdims=True) * (-_sigmoid(-p_ref[7:8, :]))
        dzr = dr * r * (1.0 - r)
        dzi = di * i * (1.0 - i)
        st_ref[5:6, :] += jnp.sum(dzr, axis=0, keepdims=True)
        st_ref[6:7, :] += jnp.sum(dzi, axis=0, keepdims=True)
        dlxc = dlxc + _blockdiag_mm(dzr, waT_ref) + _blockdiag_mm(dzi, wxT_ref)
        for j in range(4):
            sl = slice(256 * j, 256 * (j + 1))
            pa = _mm_tn(lx[:, sl], dzr[:, sl])
            px = _mm_tn(lx[:, sl], dzi[:, sl])
            for b in range(4):
                bs = slice(BW * b, BW * (b + 1))
                dwa_ref[4 * j + b] += pa[bs, bs]
                dwx_ref[4 * j + b] += px[bs, bs]
        dlx_ref[...] = _conv_bwd(dp, dlxc, lxr_ref[...], p_ref, st_ref, TT).astype(BF)

    w4 = _whole((4, 256, 256))
    return pl.pallas_call(
        body, name="lru_bwd", grid=(NT,),
        in_specs=[_rows(TT, LW, NT), _rows(TT, LW, NT), _rows(TT, LW, NT), _rows(TT, LW, NT), _rows(TT, LW, NT),
                  _halo(LW, NT, TT), _whole((16, LW)), w4, w4, w4, w4] + [ANY] * nb,
        out_specs=[_rows(TT, LW, NT), _rows(TT, LW, NT), _whole((16, LW)), _whole((NBLK, BW, BW)), _whole((NBLK, BW, BW))]
        + [ANY] * nb,
        out_shape=[S((T, LW), BF), S((T, LW), BF), S((16, LW), F32), S((NBLK, BW, BW), F32), S((NBLK, BW, BW), F32)]
        + [S(_half_shape(b), b.dtype) for b in bufs],
        scratch_shapes=[pltpu.VMEM((TT + 8, LW), F32), pltpu.VMEM((TT + 8, LW), F32),
                        pltpu.VMEM((TT, LW), F32), pltpu.VMEM((TT, LW), F32), pltpu.VMEM((TT, LW), F32),
                        pltpu.VMEM((8, LW), F32), pltpu.SemaphoreType.DMA((nb,)), pltpu.SemaphoreType.DMA((nb,))],
        compiler_params=_params(),
    )(dy, lxr, lxc, lg, h, h, p_lru, wa4, wx4, wa4T, wx4T, *bufs)


def _ssd_bwd(dyn, xbcr, cv, z, dtr, y, states, cw_ssd, hp_ssd, g_ssd, parts):
    T = dyn.shape[0]
    NC = T // CH
    nq = len(parts)

    def body(*refs):
        dyn_ref, xr_ref, cv_ref, z_ref, dt_ref, y_ref, st_ref, cw_ref, hp_ref, g_ref = refs[:10]
        q_in = refs[10:10 + nq]
        dxbc_ref, dz_ref, ddt_ref, cst_ref, hst_ref, gst_ref = refs[10 + nq:16 + nq]
        q_out = refs[16 + nq:16 + 2 * nq]
        dp, dS, send_sems, recv_sems = refs[16 + 2 * nq:]
        dyn = dyn_ref[...]
        first = pl.program_id(0) == 0
        for phase, step in enumerate((0, NC - 1)):
            @pl.when(pl.program_id(0) == step)
            def _():
                _quad_phase(phase, q_in, q_out, send_sems, recv_sems)

        @pl.when(first)
        def _():
            cst_ref[...] = jnp.zeros_like(cst_ref)
            hst_ref[...] = jnp.zeros_like(hst_ref)
            gst_ref[...] = jnp.zeros_like(gst_ref)
            dp[CH:CH + 8, :] = jnp.zeros((8, XBC), F32)
            dS[...] = jnp.zeros_like(dS)

        cv = cv_ref[...]
        sg, xbc, raw, dtv, A, cs = _ssd_prep(cv, dt_ref, hp_ref)
        csT, dsm, E_x, ds_x, El_rows = _ssd_decays(cs)
        row_i = lax.broadcasted_iota(jnp.int32, (CH, CH), 0)
        col_i = lax.broadcasted_iota(jnp.int32, (CH, CH), 1)
        tril = row_i >= col_i
        first = col_i < HD
        head_of = ((lax.broadcasted_iota(jnp.int32, (DTP, SI), 1) >> 6)
                   == lax.broadcasted_iota(jnp.int32, (DTP, SI), 0)).astype(BF)
        head_ofT = ((lax.broadcasted_iota(jnp.int32, (SI, DTP), 0) >> 6)
                    == lax.broadcasted_iota(jnp.int32, (SI, DTP), 1)).astype(BF)

        def hi_lo(v):
            hi = v.astype(BF)
            return hi, (v - hi.astype(F32)).astype(BF)

        GW = HPG * HD

        def lane_sums(v, g):
            hi, lo = hi_lo(v)
            w = head_ofT[GW * g:GW * (g + 1), :]
            return _mm(hi, w) + _mm(lo, w)

        zz = z_ref[...]
        sz = _sigmoid(zz)
        yv = y_ref[...]
        dgn, dg = _rms_bwd(yv * (zz * sz), g_ref[...], dyn)
        gst_ref[0:1, :] += dg
        dz_ref[...] = (dgn * yv * (sz + zz * sz * (1.0 - sz))).astype(BF)
        dY = dgn * (zz * sz)

        X = xbc[:, 0:SI]
        dsilu = sg + cv * sg * (1.0 - sg)
        dt_x = _per_head_lanes(dtv)
        xs = X * dt_x
        xsd = (xs * ds_x).astype(BF)
        D_x = _per_head_lanes(hp_ref[...])[2:3, :]
        zero = jnp.zeros((CH, DTP), F32)
        dcs_col = zero
        dcs_row = zero
        dds, ddt_col, dD_rows, dcl_rows = zero, zero, zero, zero
        for g in range(NG):
            gs = slice(GW * g, GW * (g + 1))
            Bg = xbc[:, SI + NS * g:SI + NS * (g + 1)].astype(BF)
            Cg = xbc[:, SI + NG * NS + NS * g:SI + NG * NS + NS * (g + 1)].astype(BF)
            G = _mm_nt(Cg, Bg)
            Sg = st_ref[0, gs, :]
            dSe = dS[gs, :]
            dYg = dY[:, gs]
            dcs_col = dcs_col + lane_sums(dYg * (_mm_nt(Cg, Sg) * E_x[:, gs]), g)
            dD_rows = dD_rows + lane_sums(dYg * X[:, gs], g)
            dP = dYg * E_x[:, gs]
            dCg = _mm(dP, Sg)
            dS[gs, :] = _mm_tn(dP, Cg) + _per_head_rows(El_rows, g) * dSe
            t_hi, t_lo = hi_lo(dSe * Sg)
            dcl_rows = dcl_rows + _mm(head_of[:, gs], t_hi) + _mm(head_of[:, gs], t_lo)
            Q = _mm_nt(Bg, dSe)
            dds = dds + lane_sums(Q * xs[:, gs], g)
            dBg = _mm(xsd[:, gs], dSe)
            dG = jnp.zeros((CH, CH), F32)
            dxs_pairs = []
            for jj in range(HPG // 2):
                j = g * (HPG // 2) + jj
                ps = slice(2 * HD * j, 2 * HD * (j + 1))
                xs_pair = xs[:, ps]
                dxs_pair = Q[:, 2 * HD * jj:2 * HD * (jj + 1)] * ds_x[:, ps]
                for e in range(2):
                    h = 2 * j + e
                    Lm = jnp.exp(jnp.where(tril, cs[:, h:h + 1] - csT[h:h + 1, :], -1e30))
                    M = G * Lm
                    dYm = jnp.where(first if e == 0 else ~first, dY[:, ps], 0.0).astype(BF)
                    dM = _mm_nt(dYm, xs_pair)
                    dxs_pair = dxs_pair + _mm_tn(M, dYm)
                    Wm = dM * M
                    dcs_col = dcs_col + jnp.where(col_i == h, jnp.sum(Wm, axis=1, keepdims=True), 0.0)
                    dcs_row = dcs_row + jnp.where(row_i == h, -jnp.sum(Wm, axis=0, keepdims=True), 0.0)
                    dG = dG + dM * Lm
                dp[0:CH, ps] = (D_x[:, ps] * dY[:, ps] + dxs_pair * dt_x[:, ps]) * dsilu[:, ps]
                dxs_pairs.append(dxs_pair)
            ddt_col = ddt_col + lane_sums(jnp.concatenate(dxs_pairs, axis=1) * X[:, gs], g)
            bs = slice(SI + NS * g, SI + NS * (g + 1))
            cs_ = slice(SI + NG * NS + NS * g, SI + NG * NS + NS * (g + 1))
            dp[0:CH, bs] = (dBg + _mm_tn(dG, Cg)) * dsilu[:, bs]
            dp[0:CH, cs_] = (dCg + _mm(dG, Bg)) * dsilu[:, cs_]

        dds = dds * dsm
        dcs_col = dcs_col - dds
        dD = jnp.sum(dD_rows, axis=0, keepdims=True)
        dcl_rows = jnp.sum(dcl_rows, axis=1, keepdims=True) * jnp.exp(csT[:, CH - 1:CH])
        dcs_row = dcs_row + jnp.where(col_i == CH - 1, dcl_rows, 0.0)
        dcs_col = dcs_col + jnp.where(row_i == CH - 1, jnp.sum(dds, axis=0, keepdims=True), 0.0)

        da = _rev_cumsum_rows(dcs_col + dcs_row.T, CH)
        ddt_col = ddt_col + da * A
        hst_ref[1:2, :] += jnp.sum(da * dtv, axis=0, keepdims=True) * A
        hst_ref[2:3, :] += dD
        draw = jnp.where(col_i < NH, ddt_col * _sigmoid(raw), 0.0)
        ddt_ref[...] = draw.astype(BF)
        hst_ref[0:1, :] += jnp.sum(draw, axis=0, keepdims=True)

        dxbc_ref[...] = _conv_bwd(dp, None, xr_ref[...], cw_ref, cst_ref, CH).astype(BF)

    return pl.pallas_call(
        body, name="ssd_bwd", grid=(NC,),
        in_specs=[_rows(CH, SI, NC), _rows(CH, XBC, NC), _rows(CH, XBC, NC), _rows(CH, SI, NC), _rows(CH, DTP, NC),
                  _rows(CH, SI, NC), pl.BlockSpec((1, NH * HD, NS), lambda i: (NC - 1 - i, 0, 0)),
                  _whole((8, XBC)), _whole((8, DTP)), _whole((1, SI))] + [ANY] * nq,
        out_specs=[_rows(CH, XBC, NC), _rows(CH, SI, NC), _rows(CH, DTP, NC), _whole((16, XBC)), _whole((16, DTP)),
                   _whole((8, SI))] + [ANY] * nq,
        out_shape=[S((T, XBC), BF), S((T, SI), BF), S((T, DTP), BF), S((16, XBC), F32), S((16, DTP), F32), S((8, SI), F32)]
        + [S(p.shape, p.dtype) for p in parts],
        scratch_shapes=[pltpu.VMEM((CH + 8, XBC), F32), pltpu.VMEM((NH * HD, NS), F32),
                        pltpu.SemaphoreType.DMA((3 * nq,)), pltpu.SemaphoreType.DMA((3 * nq,))],
        compiler_params=_params(),
    )(dyn, xbcr, cv, z, dtr, y, states, cw_ssd, hp_ssd, g_ssd, *parts)


def _inproj_bwd(dlx, dlg, dz, dxbc, ddt, x, dx1, wcatT, g0, parts):
    T = x.shape[0]
    NT = T // TW
    nq = len(parts)

    def body(*refs):
        dlx_ref, dlg_ref, dz_ref, dxbc_ref, ddt_ref, x_ref, dx1_ref, w_hbm, g_ref = refs[:9]
        q_in = refs[9:9 + nq]
        dx_ref, st_ref = refs[9 + nq:11 + nq]
        q_out = refs[11 + nq:11 + 2 * nq]
        w_vm, sem, send_sems, recv_sems = refs[11 + 2 * nq:]
        _load_once([(w_hbm, w_vm)], sem)
        for phase, step in enumerate((0, NT - 1)):
            @pl.when(pl.program_id(0) == step)
            def _():
                _quad_phase(phase, q_in, q_out, send_sems, recv_sems)

        @pl.when(pl.program_id(0) == 0)
        def _():
            st_ref[...] = jnp.zeros_like(st_ref)

        dh = jnp.dot(dlx_ref[...], w_vm[0:1024, :], preferred_element_type=F32)
        dh = dh + jnp.dot(dlg_ref[...], w_vm[1024:2048, :], preferred_element_type=F32)
        dh = dh + jnp.dot(dz_ref[...], w_vm[2048:3072, :], preferred_element_type=F32)
        dh = dh + jnp.dot(dxbc_ref[...], w_vm[3072:3072 + XBC, :], preferred_element_type=F32)
        dh = dh + jnp.dot(ddt_ref[...], w_vm[3072 + XBC:PC, :], preferred_element_type=F32)
        dx, dg = _rms_bwd(x_ref[...], g_ref[...], dh)
        dx_ref[...] = dx1_ref[...] + dx
        st_ref[0:1, :] += dg

    return pl.pallas_call(
        body, name="inproj_bwd", grid=(NT,),
        in_specs=[_rows(TW, 1024), _rows(TW, 1024), _rows(TW, 1024), _rows(TW, XBC), _rows(TW, DTP), _rows(TW, D),
                  _rows(TW, D), ANY, _whole((1, D))] + [ANY] * nq,
        out_specs=[_rows(TW, D), _whole((8, D))] + [ANY] * nq,
        out_shape=[S((T, D), F32), S((8, D), F32)] + [S(p.shape, p.dtype) for p in parts],
        scratch_shapes=[pltpu.VMEM((PC, D), BF), pltpu.SemaphoreType.DMA((1,)),
                        pltpu.SemaphoreType.DMA((3 * nq,)), pltpu.SemaphoreType.DMA((3 * nq,))],
        compiler_params=_params(),
    )(dlx, dlg, dz, dxbc, ddt, x, dx1, wcatT, g0, *parts)


def _wgrad(name, a, b):
    T, M = a.shape
    N = b.shape[1]
    tk = min(T, 2048 if M <= 1024 else 1024)
    tn = N
    while M * tn * 4 > (6 << 20) and tn % 256 == 0:
        tn //= 2

    def body(a_ref, b_ref, o_ref):
        p = lax.dot_general(a_ref[...], b_ref[...], (((0,), (0,)), ((), ())), preferred_element_type=F32)

        @pl.when(pl.program_id(1) == 0)
        def _():
            o_ref[...] = p

        @pl.when(pl.program_id(1) > 0)
        def _():
            o_ref[...] += p

    return pl.pallas_call(
        body, name=name, grid=(N // tn, T // tk),
        in_specs=[pl.BlockSpec((tk, M), lambda j, k: (k, 0)), pl.BlockSpec((tk, tn), lambda j, k: (k, j))],
        out_specs=pl.BlockSpec((M, tn), lambda j, k: (0, j)), out_shape=S((M, N), F32),
        compiler_params=_params(2),
    )(a, b)


def _adamw(name, w, g, m, v):
    _, R, C = w.shape

    def body(w_ref, g_ref, m_ref, v_ref, d_ref, nm_ref, nv_ref):
        d_ref[0], nm_ref[0], nv_ref[0] = _adam_math(w_ref[0], g_ref[...], m_ref[0], v_ref[0])

    if R % 8 == 0:
        tr = _row_tile(R, C)
        n_tiles = R // tr
        blk, gblk = pl.BlockSpec((1, tr, C), lambda i: (0, i, 0)), pl.BlockSpec((tr, C), lambda i: (i, 0))
    else:
        tc = 128 * max(k for k in range(1, C // 128 + 1) if C % (128 * k) == 0 and R * 128 * k * 4 <= (5 << 18))
        n_tiles = C // tc
        blk, gblk = pl.BlockSpec((1, R, tc), lambda i: (0, 0, i)), pl.BlockSpec((R, tc), lambda i: (0, i))
    return pl.pallas_call(
        body, name=name, grid=(n_tiles,),
        in_specs=[blk, gblk, blk, blk], out_specs=[blk] * 3,
        out_shape=[S((1, R, C), F32)] * 3, compiler_params=_params(),
    )(w, g, m, v)


def _pair_exchange(name, bufs):
    n = len(bufs)

    def body(*refs):
        for phase in range(2):
            _pair_phase(phase, refs[:n], refs[n:2 * n], refs[2 * n], refs[2 * n + 1])

    return pl.pallas_call(
        body, name=name, in_specs=[ANY] * n, out_specs=[ANY] * n,
        out_shape=[S(_half_shape(b), b.dtype) for b in bufs],
        scratch_shapes=[pltpu.SemaphoreType.DMA((n,)), pltpu.SemaphoreType.DMA((n,))],
    )(*bufs)


def _quad_exchange(bufs):
    n = len(bufs)

    def body(*refs):
        ins, outs = refs[:n], refs[n:2 * n]
        send_sems, recv_sems = refs[2 * n], refs[2 * n + 1]
        x, y, c = _pos()
        me = 2 * x + y
        chips = _other_chips(x, y)
        copies = []
        for k, (src, dst) in enumerate(zip(ins, outs)):
            for j, (cx, cy) in enumerate(chips):
                cp = _remote(src, dst.at[me], send_sems.at[3 * k + j], recv_sems.at[3 * k + j], (cx, cy, c))
                cp.start()
                copies.append(cp)
        for k, (src, dst) in enumerate(zip(ins, outs)):
            for j, (cx, cy) in enumerate(chips):
                blk = dst.at[2 * cx + cy]
                _remote(blk, blk, send_sems.at[3 * k + j], recv_sems.at[3 * k + j], (cx, cy, c)).wait_recv()
        for cp in copies:
            cp.wait_send()

    return pl.pallas_call(
        body, name="quad_exchange", in_specs=[ANY] * n, out_specs=[ANY] * n,
        out_shape=[S((4,) + b.shape, b.dtype) for b in bufs],
        scratch_shapes=[pltpu.SemaphoreType.DMA((3 * n,)), pltpu.SemaphoreType.DMA((3 * n,))],
    )(*bufs)


def _pair_gather(bufs):
    n = len(bufs)

    def body(*refs):
        ins, outs = refs[:n], refs[n:2 * n]
        send_sems, recv_sems = refs[2 * n], refs[2 * n + 1]
        x, y, c = _pos()
        copies = []
        for k, buf in enumerate(outs):
            mine = _half(buf, c, buf.shape[0] // 2)
            cp = _remote(mine, mine, send_sems.at[k], recv_sems.at[k], (x, y, 1 - c))
            cp.start()
            copies.append(cp)
        for k, buf in enumerate(outs):
            theirs = _half(buf, 1 - c, buf.shape[0] // 2)
            _remote(theirs, theirs, send_sems.at[k], recv_sems.at[k], (x, y, 1 - c)).wait_recv()
        for cp in copies:
            cp.wait_send()

    return pl.pallas_call(
        body, name="pair_gather", in_specs=[ANY] * n, out_specs=[ANY] * n,
        out_shape=[S(b.shape, b.dtype) for b in bufs], input_output_aliases={k: k for k in range(n)},
        scratch_shapes=[pltpu.SemaphoreType.DMA((n,)), pltpu.SemaphoreType.DMA((n,))],
    )(*bufs)


def _row_tile(rows, cols, mult=8):
    best = mult
    for t in range(mult, rows + 1, mult):
        if rows % t == 0 and t * cols * 4 <= (1 << 20):
            best = t
    return best


def _add_own_half(name, full, got, c, out_dtype, by_columns):
    hr = got.shape[-2]
    wide = got.shape[-1]
    cols = wide // 4 if by_columns else wide
    tr = _row_tile(hr, wide, 16)
    per = hr // tr

    if by_columns:
        def body(c_ref, a_ref, b_ref, o_ref):
            v = a_ref[...] + b_ref[...]
            for j in range(4):
                o_ref[j] = v[:, j * cols:(j + 1) * cols].astype(out_dtype)

        in_specs = [pl.BlockSpec((tr, wide), lambda i, c_ref: (c_ref[0] * per + i, 0)),
                    pl.BlockSpec((tr, wide), lambda i, c_ref: (i, 0))]
        out_specs = pl.BlockSpec((4, tr, cols), lambda i, c_ref: (0, i, 0))
        grid = (per,)
    else:
        def body(c_ref, a_ref, b_ref, o_ref):
            o_ref[...] = (a_ref[...] + b_ref[...]).astype(out_dtype)

        in_specs = [pl.BlockSpec((1, tr, cols), lambda s, i, c_ref: (s, c_ref[0] * per + i, 0)),
                    pl.BlockSpec((1, tr, cols), lambda s, i, c_ref: (s, i, 0))]
        out_specs = pl.BlockSpec((1, tr, cols), lambda s, i, c_ref: (s, i, 0))
        grid = (4, per)
    return pl.pallas_call(
        body, name=name,
        grid_spec=pltpu.PrefetchScalarGridSpec(num_scalar_prefetch=1, grid=grid, in_specs=in_specs, out_specs=out_specs),
        out_shape=S((4, hr, cols), out_dtype), compiler_params=_params(len(grid)),
    )(jnp.reshape(c, (1,)).astype(jnp.int32), full, got)


def _small_add_own_half(fulls, gots, c):
    n = len(fulls)

    def body(c_ref, *refs):
        for a_ref, b_ref, o_ref in zip(refs[:n], refs[n:2 * n], refs[2 * n:]):
            hr = b_ref.shape[0]
            o_ref[...] = a_ref[pl.ds(pl.multiple_of(c_ref[0] * hr, 8), hr), :] + b_ref[...]

    specs = lambda arrs: [pl.BlockSpec(a.shape, lambda i, c_ref: (0, 0)) for a in arrs]
    return pl.pallas_call(
        body, name="small_pair_add",
        grid_spec=pltpu.PrefetchScalarGridSpec(num_scalar_prefetch=1, grid=(1,), in_specs=specs(fulls) + specs(gots),
                                               out_specs=specs(gots)),
        out_shape=[S(g.shape, F32) for g in gots], compiler_params=_params(),
    )(jnp.reshape(c, (1,)).astype(jnp.int32), *fulls, *gots)


def _small_sum_slots(own, slots, me, c):
    n = len(slots)

    def body(p_ref, *refs):
        own_refs, slot_refs, o_refs = refs[:n], refs[n:5 * n], refs[5 * n:]
        for i, (own_ref, o_ref) in enumerate(zip(own_refs, o_refs)):
            hr = own_ref.shape[0]
            acc = None
            for j in range(4):
                v = jnp.where(p_ref[0] == j, own_ref[...], slot_refs[4 * i + j][0])
                acc = v if acc is None else acc + v
            o_ref[pl.ds(pl.multiple_of(p_ref[1] * hr, 8), hr), :] = acc

    def slot_spec(s, j):
        return pl.BlockSpec((1,) + s.shape[1:], lambda i, p: (jnp.where(p[0] == j, (j + 1) % 4, j), 0, 0))

    outs = [S((2 * s.shape[1], s.shape[2]), F32) for s in slots]
    return pl.pallas_call(
        body, name="small_quad_sum",
        grid_spec=pltpu.PrefetchScalarGridSpec(
            num_scalar_prefetch=1, grid=(1,),
            in_specs=[pl.BlockSpec(o.shape, lambda i, p: (0, 0)) for o in own]
            + [slot_spec(s, j) for s in slots for j in range(4)],
            out_specs=[pl.BlockSpec(o.shape, lambda i, p: (0, 0)) for o in outs]),
        out_shape=outs, compiler_params=_params(),
    )(jnp.stack([me, c]).astype(jnp.int32), *own, *[s for s in slots for _ in range(4)])


def _sum_slots(name, own, slots, me, c):
    _, rows, cols = slots.shape
    tr = _row_tile(rows, cols, 16 if slots.dtype == jnp.bfloat16 else 8)
    per = rows // tr
    three = len(own.shape) == 3

    def body(p_ref, own_ref, s0, s1, s2, s3, o_ref):
        mine = own_ref[0] if three else own_ref[...]
        acc = None
        for j, s_ref in enumerate((s0, s1, s2, s3)):
            v = jnp.where(p_ref[0] == j, mine, s_ref[0]).astype(F32)
            acc = v if acc is None else acc + v
        o_ref[...] = acc

    def slot_spec(j):
        return pl.BlockSpec((1, tr, cols), lambda i, p: (jnp.where(p[0] == j, (j + 1) % 4, j), i, 0))

    own_spec = (pl.BlockSpec((1, tr, cols), lambda i, p: (p[0], i, 0)) if three
                else pl.BlockSpec((tr, cols), lambda i, p: (i, 0)))
    return pl.pallas_call(
        body, name=name,
        grid_spec=pltpu.PrefetchScalarGridSpec(
            num_scalar_prefetch=1, grid=(per,), in_specs=[own_spec] + [slot_spec(j) for j in range(4)],
            out_specs=pl.BlockSpec((tr, cols), lambda i, p: (p[1] * per + i, 0))),
        out_shape=S((2 * rows, cols), F32), compiler_params=_params(),
    )(jnp.stack([me, c]).astype(jnp.int32), own, slots, slots, slots, slots)


BIG = ("w_in", "w_out", "w_gate", "w_up", "w_down")
ROW_PARAMS = (("pre_mix_norm", 0), ("lru_conv_b", 12), ("lru_ba", 13), ("lru_bx", 14), ("lru_lambda", 15),
              ("lru_out_norm", 16), ("ssd_out_norm", 24), ("post_mix_norm", 33), ("pre_ffn_norm", 32), ("post_ffn_norm", 41))
LRU_CONV_ROWS = (8, 12)
LOSS_ROW = 40
HEAD_PARAMS = (("ssd_dt_bias", 0), ("ssd_a_log", 1), ("ssd_d", 2))
SMALL = tuple(n for n, _ in ROW_PARAMS) + ("ssd_conv_b",) + tuple(n for n, _ in HEAD_PARAMS) + (
    "lru_wa", "lru_wx", "lru_conv_w", "ssd_conv_w")


def _diag4(w):
    eye = jnp.eye(4, dtype=w.dtype).reshape(1, 4, 1, 4, 1)
    return (w.reshape(4, 4, BW, 1, BW) * eye).reshape(4, 4 * BW, 4 * BW)


def _adam_math(w, g, m, v):
    mm = ADAM_B1 * m + (1.0 - ADAM_B1) * g
    vv = ADAM_B2 * v + (1.0 - ADAM_B2) * (g * g)
    c1 = 1.0 - ADAM_B1 ** ADAM_STEP
    c2 = 1.0 - ADAM_B2 ** ADAM_STEP
    return -ADAM_LR * ((mm / c1) / (jnp.sqrt(vv / c2) + ADAM_EPS) + ADAM_WD * w), mm, vv


def _adamw_small(rows, cst, hst, dwa, dwx, glcw, gscw, w, m, v):
    def grad_of(name, refs):
        rows_ref, cst_ref, hst_ref, dwa_ref, dwx_ref, glcw_ref, gscw_ref = refs
        for n, r in ROW_PARAMS:
            if n == name:
                return rows_ref[r:r + 1, :]
        for n, r in HEAD_PARAMS:
            if n == name:
                return hst_ref[r:r + 1, 0:NH]
        return {"ssd_conv_b": lambda: cst_ref[4:5, :], "lru_wa": lambda: dwa_ref[...], "lru_wx": lambda: dwx_ref[...],
                "lru_conv_w": lambda: glcw_ref[...], "ssd_conv_w": lambda: gscw_ref[...]}[name]()

    shapes = {n: (w[n].shape[1:] if len(w[n].shape) > 2 else w[n].shape) for n in SMALL}
    flat = lambda d: [d[n].reshape(shapes[n]) for n in SMALL]
    ns = len(SMALL)

    def body(*refs):
        srcs, rest = refs[:7], refs[7:]
        w_refs, m_refs, v_refs = rest[:ns], rest[ns:2 * ns], rest[2 * ns:3 * ns]
        outs = rest[3 * ns:]
        for k, name in enumerate(SMALL):
            g = grad_of(name, srcs)
            d, mm, vv = _adam_math(w_refs[k][...], g, m_refs[k][...], v_refs[k][...])
            outs[4 * k][...] = g
            outs[4 * k + 1][...] = d
            outs[4 * k + 2][...] = mm
            outs[4 * k + 3][...] = vv

    res = pl.pallas_call(
        body, name="adamw_small",
        out_shape=[S(shapes[n], F32) for n in SMALL for _ in range(4)],
        compiler_params=pltpu.CompilerParams(vmem_limit_bytes=VMEM_LIMIT),
    )(rows, cst, hst, dwa, dwx, glcw, gscw, *flat(w), *flat(m), *flat(v))
    return {n: tuple(res[4 * k + i].reshape(w[n].shape) for i in range(4)) for k, n in enumerate(SMALL)}


def _with_own(own, got):
    chip = 2 * lax.axis_index("x") + lax.axis_index("y")
    return jnp.where((jnp.arange(4) == chip).reshape(4, 1, 1), own[None], got)


def _side_by_side(f):
    return f.transpose(1, 0, 2).reshape(f.shape[1], 4 * f.shape[2])


def _stacked(f):
    return f.reshape(4 * f.shape[1], f.shape[2])


def _conv_terms(lru_conv_w, ssd_conv_w):
    conv = jnp.concatenate([lru_conv_w.reshape(-1), ssd_conv_w.reshape(-1)]).astype(F32)
    hi = conv.astype(jnp.bfloat16)
    mid = (conv - hi.astype(F32)).astype(jnp.bfloat16)
    lo = (conv - hi.astype(F32) - mid.astype(F32)).astype(jnp.bfloat16)
    terms = jnp.concatenate([hi, mid, lo])
    rows = -(-terms.shape[0] // (128 * 32)) * 32
    return jnp.pad(terms, (0, rows * 128 - terms.shape[0])).reshape(rows, 128)


def _full_conv_taps(own, got, n_lru, n_ssd):
    n_terms = 3 * (n_lru + n_ssd)
    t3 = _with_own(own, got).reshape(4, -1)[:, :n_terms].reshape(4, 3, -1).astype(F32)
    conv_f = (t3[:, 0] + t3[:, 1]) + t3[:, 2]
    lcw = conv_f[:, :n_lru].reshape(4, CONV_K, -1).transpose(1, 0, 2).reshape(CONV_K, LW)
    scw = conv_f[:, n_lru:].reshape(4, CONV_K, -1).transpose(1, 0, 2).reshape(CONV_K, XBC)
    return lcw, scw


def _step(x, tgt, w_in, lru_conv_w, ssd_conv_w, sp, late):
    c = lax.axis_index("c")
    me = 2 * lax.axis_index("x") + lax.axis_index("y")
    mm = lambda w: w.astype(BF)
    row = lambda v: v.reshape(1, -1).astype(F32)
    g0 = row(sp["pre_mix_norm"])
    first = [w_in.astype(WIRE), _conv_terms(lru_conv_w, ssd_conv_w)]
    h0, *got_first = _prenorm(x, g0, first)
    win_f = _side_by_side(_with_own(first[0], got_first[0]))
    lcw, scw = _full_conv_taps(first[1], got_first[1], lru_conv_w.size, ssd_conv_w.size)
    wcat = jnp.concatenate([mm(win_f), jnp.zeros((D, PC - IN_COLS), BF)], axis=1)
    p_lru = jnp.concatenate([lcw, row(sp["lru_conv_b"]), row(sp["lru_ba"]), row(sp["lru_bx"]), row(sp["lru_lambda"]),
                             row(sp["lru_out_norm"]), jnp.zeros((7, LW), F32)], axis=0)
    wa4, wx4 = mm(_diag4(sp["lru_wa"][0])), mm(_diag4(sp["lru_wx"][0]))
    wa4T, wx4T = wa4.transpose(0, 2, 1), wx4.transpose(0, 2, 1)
    cw_ssd = jnp.concatenate([scw, row(sp["ssd_conv_b"]), jnp.zeros((3, XBC), F32)], axis=0)
    padh = lambda v: jnp.pad(row(v), ((0, 0), (0, DTP - NH)))
    hp_ssd = jnp.concatenate([padh(sp["ssd_dt_bias"]), padh(sp["ssd_a_log"]), padh(sp["ssd_d"]), jnp.zeros((5, DTP), F32)], axis=0)
    g_ssd = row(sp["ssd_out_norm"])
    g_pm, g_pf, g_pff = row(sp["post_mix_norm"]), row(sp["pre_ffn_norm"]), row(sp["post_ffn_norm"])

    h, ylru, lxc, lxr, lg, *got_a = _lru_fwd(h0, wcat, p_lru, wa4, wx4, [late[0], late[3]])
    y, yssd, states, cv, z, xbcr, dtr, *got_b = _ssd_fwd(h0, wcat, cw_ssd, hp_ssd, g_ssd, [late[1], late[2]])
    wout, wd = mm(_stacked(_with_own(late[0], got_a[0]))), mm(_stacked(_with_own(late[3], got_a[1])))
    wg, wu = mm(_side_by_side(_with_own(late[1], got_b[0]))), mm(_side_by_side(_with_own(late[2], got_b[1])))
    mix, x1, h2 = _outproj(ylru, yssd, x, wout, g_pm, g_pf)
    gate, up, act, df, dx2, st_ffn = _ffn_fwd(h2, x1, tgt, wg, wu, wd, g_pff)
    dgate, dup, dh2 = _ffn_bwd(df, gate, up, wd.T, wg.T, wu.T)
    dx1, dmix, dyl, dys, st_mix = _mix_bwd(dh2, x1, dx2, mix, wout.T, g_pf, g_pm)

    dwg = _wgrad("wgrad_gate", h2, dgate)
    dwu = _wgrad("wgrad_up", h2, dup)
    dwd = _wgrad("wgrad_down", act, df)
    dwo = jnp.concatenate([_wgrad("wgrad_out_lru", ylru, dmix), _wgrad("wgrad_out_ssd", yssd, dmix)], axis=0)
    early = [dwo.reshape(4, (LW + SI) // 4, D), dwg, dwu, dwd.reshape(4, DFF // 4, D)]
    dlx, dlg, st_lru, dwa, dwx, *got_early = _lru_bwd(dyl, lxr, lxc, lg, h, p_lru, wa4, wx4, wa4T, wx4T, early)
    part_early = [_add_own_half("pair_add_early%d" % k, b, r, c, WIRE, bc)
                  for k, (b, r, bc) in enumerate(zip(early, got_early, [False, True, True, False]))]
    dxbc, dz, ddt, cst, hst, gst, *slots_early = _ssd_bwd(dys, xbcr, cv, z, dtr, y, states, cw_ssd, hp_ssd, g_ssd,
                                                          part_early)
    red_early = [_sum_slots("quad_sum_early%d" % k, p, s, me, c) for k, (p, s) in enumerate(zip(part_early, slots_early))]

    pin = [_wgrad("wgrad_in_%d" % k, h0, b) for k, b in enumerate((dlx, dlg, dz, dxbc, ddt))]
    dwin = jnp.concatenate(pin[:4] + [pin[4][:, :NH]], axis=1)
    (got_win,) = _pair_exchange("pair_exchange_w_in", [dwin])
    part_win = _add_own_half("pair_add_w_in", dwin, got_win, c, WIRE, True)
    gx, st_in, slots_win = _inproj_bwd(dlx, dlg, dz, dxbc, ddt, x, dx1, wcat.T, g0, [part_win])
    red_win = _sum_slots("quad_sum_w_in", part_win, slots_win, me, c)

    rows = jnp.concatenate([st_in, st_lru, gst, st_mix, st_ffn], axis=0)
    small = [rows, cst, hst, dwa.reshape(NBLK * BW, BW), dwx.reshape(NBLK * BW, BW)]
    part_small = list(_small_add_own_half(small, list(_pair_exchange("pair_exchange_small", small)), c))
    red_small = list(_small_sum_slots(part_small, list(_quad_exchange(part_small)), me, c))
    out = list(_pair_gather([red_win] + red_early + red_small))
    big = dict(zip(("w_in", "w_out", "w_gate", "w_up", "w_down"), out[:5]))
    return gx, big, out[5:]


def kernel(x, pre_mix_norm, w_in, lru_conv_w, lru_conv_b, lru_wa, lru_ba, lru_wx, lru_bx, lru_lambda, lru_out_norm, ssd_conv_w, ssd_conv_b, ssd_dt_bias, ssd_a_log, ssd_d, ssd_out_norm, w_out, post_mix_norm, pre_ffn_norm, w_gate, w_up, w_down, post_ffn_norm, loss_target, m_pre_mix_norm, m_w_in, m_lru_conv_w, m_lru_conv_b, m_lru_wa, m_lru_ba, m_lru_wx, m_lru_bx, m_lru_lambda, m_lru_out_norm, m_ssd_conv_w, m_ssd_conv_b, m_ssd_dt_bias, m_ssd_a_log, m_ssd_d, m_ssd_out_norm, m_w_out, m_post_mix_norm, m_pre_ffn_norm, m_w_gate, m_w_up, m_w_down, m_post_ffn_norm, v_pre_mix_norm, v_w_in, v_lru_conv_w, v_lru_conv_b, v_lru_wa, v_lru_ba, v_lru_wx, v_lru_bx, v_lru_lambda, v_lru_out_norm, v_ssd_conv_w, v_ssd_conv_b, v_ssd_dt_bias, v_ssd_a_log, v_ssd_d, v_ssd_out_norm, v_w_out, v_post_mix_norm, v_pre_ffn_norm, v_w_gate, v_w_up, v_w_down, v_post_ffn_norm):
    args = dict(locals())
    names = list(SMALL) + list(BIG)
    w = {n: args[n] for n in names}
    m = {n: args["m_" + n] for n in names}
    v = {n: args["v_" + n] for n in names}
    chip = 2 * lax.axis_index("x") + lax.axis_index("y")

    late = [a[0].astype(WIRE) for a in (w_out, w_gate, w_up, w_down)]
    gx, red, (rows, cst, hst, dwa, dwx) = _step(x[0], loss_target[0], w_in[0], lru_conv_w[0], ssd_conv_w[0],
                                                {n: w[n] for n in SMALL}, late)
    loss = jnp.sum(rows[LOSS_ROW])

    grads, delta, new_m, new_v = {}, {}, {}, {}
    for n in BIG:
        g = red[n]
        if n in ("w_in", "w_gate", "w_up"):
            t = lambda a: jnp.swapaxes(a, 1, 2)
            gt = g.T
            out = _adamw("adamw_" + n, t(w[n]), gt, t(m[n]), t(v[n]))
            delta[n], new_m[n], new_v[n] = (t(o) for o in out)
            grads[n] = t(gt[None])
        else:
            delta[n], new_m[n], new_v[n] = _adamw("adamw_" + n, w[n], g, m[n], v[n])
            grads[n] = g[None]

    lc, sc = lru_conv_w.shape[-1], ssd_conv_w.shape[-1]
    glcw = lax.dynamic_slice_in_dim(rows[LRU_CONV_ROWS[0]:LRU_CONV_ROWS[1]], chip * lc, lc, axis=1)
    gscw = lax.dynamic_slice_in_dim(cst[0:CONV_K], chip * sc, sc, axis=1)
    res = _adamw_small(rows, cst, hst, dwa.reshape(NBLK, BW, BW), dwx.reshape(NBLK, BW, BW), glcw, gscw,
                       {n: w[n] for n in SMALL}, {n: m[n] for n in SMALL}, {n: v[n] for n in SMALL})
    for n in SMALL:
        grads[n], delta[n], new_m[n], new_v[n] = res[n]

    order = ["pre_mix_norm", "w_in", "lru_conv_w", "lru_conv_b", "lru_wa", "lru_ba", "lru_wx", "lru_bx", "lru_lambda",
             "lru_out_norm", "ssd_conv_w", "ssd_conv_b", "ssd_dt_bias", "ssd_a_log", "ssd_d", "ssd_out_norm", "w_out",
             "post_mix_norm", "pre_ffn_norm", "w_gate", "w_up", "w_down", "post_ffn_norm"]
    return (loss, gx[None], *[grads[n] for n in order], *[delta[n] for n in order],
            *[new_m[n] for n in order], *[new_v[n] for n in order])
```

```python
import functools

import jax
import jax.numpy as jnp
from jax import lax
from jax.experimental import pallas as pl
from jax.experimental.pallas import tpu as pltpu

F32 = jnp.float32
BF = jnp.bfloat16

D = 1024
LW = 1024
NBLK = 16
BW = 64
SI = 1024
NH = 16
HD = 64
NG = 2
HPG = NH // NG
NS = 128
CH = 128
XBC = SI + 2 * NG * NS
DTP = 128
PC = 3 * 1024 + XBC + DTP
DFF = 2816
IN_COLS = 4624
EPS = 1e-6
LRU_C = 8.0
CONV_K = 4
TT = 256
TW = 512
VMEM_LIMIT = 56 * 1024 * 1024

ADAM_LR, ADAM_B1, ADAM_B2, ADAM_EPS, ADAM_WD, ADAM_STEP = 0.001, 0.9, 0.999, 1e-08, 0.01, 10

MESH = pl.DeviceIdType.MESH


def _mm(a, b):
    return jnp.dot(a.astype(BF), b.astype(BF), preferred_element_type=F32)


def _mm_nt(a, b):
    return lax.dot_general(a.astype(BF), b.astype(BF), (((1,), (1,)), ((), ())), preferred_element_type=F32)


def _mm_tn(a, b):
    return lax.dot_general(a.astype(BF), b.astype(BF), (((0,), (0,)), ((), ())), preferred_element_type=F32)


def _sigmoid(x):
    return 0.5 * jnp.tanh(0.5 * x) + 0.5


def _softplus(x):
    return jnp.maximum(x, 0.0) + jnp.log1p(jnp.exp(-jnp.abs(x)))


_GELU_C = 0.7978845608028654
_GELU_K = 0.044715


def _gelu(x):
    t = jnp.tanh(_GELU_C * (x + _GELU_K * x * x * x))
    return 0.5 * x * (1.0 + t)


def _gelu_grad(x):
    t = jnp.tanh(_GELU_C * (x + _GELU_K * x * x * x))
    return 0.5 * (1.0 + t) + 0.5 * x * (1.0 - t * t) * _GELU_C * (1.0 + 3.0 * _GELU_K * x * x)


def _rms_fwd(x, g):
    r = lax.rsqrt(jnp.mean(x * x, axis=-1, keepdims=True) + EPS)
    return x * r * g


def _rms_bwd(x, g, dy):
    r = lax.rsqrt(jnp.mean(x * x, axis=-1, keepdims=True) + EPS)
    xh = x * r
    dxh = dy * g
    dg = jnp.sum(dy * xh, axis=0, keepdims=True)
    dx = r * (dxh - xh * jnp.mean(dxh * xh, axis=-1, keepdims=True))
    return dx, dg


def _sum_all(x):
    return jnp.sum(jnp.sum(x, axis=1, keepdims=True), axis=0, keepdims=True)


def _cumsum_rows(x, n):
    row = lax.broadcasted_iota(jnp.int32, x.shape, 0)
    k = 1
    while k < n:
        x = x + jnp.where(row >= k, pltpu.roll(x, k, 0), 0.0)
        k *= 2
    return x


def _rev_cumsum_rows(x, n):
    row = lax.broadcasted_iota(jnp.int32, x.shape, 0)
    k = 1
    while k < n:
        x = x + jnp.where(row < n - k, pltpu.roll(x, n - k, 0), 0.0)
        k *= 2
    return x


def _load_once(pairs, sem):
    @pl.when(pl.program_id(0) == 0)
    def _():
        for k, (src, dst) in enumerate(pairs):
            pltpu.make_async_copy(src, dst, sem.at[k]).start()
        for k, (src, dst) in enumerate(pairs):
            pltpu.make_async_copy(src, dst, sem.at[k]).wait()


def _params(n_axes=1):
    return pltpu.CompilerParams(dimension_semantics=("arbitrary",) * n_axes, vmem_limit_bytes=VMEM_LIMIT)


def _rows(n, width, rev_of=None):
    if rev_of is None:
        return pl.BlockSpec((n, width), lambda i: (i, 0))
    return pl.BlockSpec((n, width), lambda i: (rev_of - 1 - i, 0))


def _whole(shape):
    nd = len(shape)
    return pl.BlockSpec(shape, lambda i: (0,) * nd)


ANY = pl.BlockSpec(memory_space=pl.ANY)
S = jax.ShapeDtypeStruct
WIRE = jnp.bfloat16


def _pos():
    return lax.axis_index("x"), lax.axis_index("y"), lax.axis_index("c")


def _other_chips(x, y):
    return [(1 - x, y), (x, 1 - y), (1 - x, 1 - y)]


def _remote(src, dst, send_sem, recv_sem, to):
    return pltpu.make_async_remote_copy(src_ref=src, dst_ref=dst, send_sem=send_sem, recv_sem=recv_sem,
                                        device_id=to, device_id_type=MESH)


def _gather_phase(phase, ins, outs, send_sems, recv_sems):
    x, y, c = _pos()
    me = 2 * x + y
    chips = _other_chips(x, y)
    for i, (src, dst) in enumerate(zip(ins, outs)):
        hr = src.shape[0] // 2
        my_half = pl.ds(pl.multiple_of(c * hr, 16), hr)
        sib_half = pl.ds(pl.multiple_of((1 - c) * hr, 16), hr)
        for k, (cx, cy) in enumerate(chips):
            s1, r1 = send_sems.at[6 * i + k], recv_sems.at[6 * i + k]
            s2, r2 = send_sems.at[6 * i + 3 + k], recv_sems.at[6 * i + 3 + k]
            first = lambda: _remote(src.at[my_half, :], dst.at[me, my_half, :], s1, r1, (cx, cy, c))
            landed = dst.at[2 * cx + cy, my_half, :]
            passed = lambda: _remote(landed, landed, s2, r2, (x, y, 1 - c))
            if phase == 0:
                first().start()
            elif phase == 1:
                _remote(landed, landed, s1, r1, (cx, cy, c)).wait_recv()
                passed().start()
            else:
                theirs = dst.at[2 * cx + cy, sib_half, :]
                _remote(theirs, theirs, s2, r2, (x, y, 1 - c)).wait_recv()
                first().wait_send()
                passed().wait_send()


def _half(ref, c, hr):
    sl = pl.ds(pl.multiple_of(c * hr, 8), hr)
    return ref.at[:, sl, :] if len(ref.shape) == 3 else ref.at[sl, :]


def _half_shape(b):
    return b.shape[:-2] + (b.shape[-2] // 2, b.shape[-1])


def _pair_phase(phase, ins, outs, send_sems, recv_sems):
    x, y, c = _pos()
    for k, (src, dst) in enumerate(zip(ins, outs)):
        cp = _remote(_half(src, 1 - c, src.shape[-2] // 2), dst, send_sems.at[k], recv_sems.at[k], (x, y, 1 - c))
        if phase == 0:
            cp.start()
        else:
            cp.wait()


def _quad_phase(phase, ins, outs, send_sems, recv_sems):
    x, y, c = _pos()
    me = 2 * x + y
    for i, (src, dst) in enumerate(zip(ins, outs)):
        for k, (cx, cy) in enumerate(_other_chips(x, y)):
            cp = _remote(src.at[2 * cx + cy], dst.at[me], send_sems.at[3 * i + k], recv_sems.at[3 * i + k], (cx, cy, c))
            if phase == 0:
                cp.start()
            else:
                got = dst.at[2 * cx + cy]
                _remote(got, got, send_sems.at[3 * i + k], recv_sems.at[3 * i + k], (cx, cy, c)).wait_recv()
                cp.wait_send()


def _prenorm(x, g0, shards):
    T = x.shape[0]
    tt = 2 * TT
    nt = T // tt
    ng = len(shards)

    def body(*refs):
        x_ref, g_ref = refs[:2]
        sh_in = refs[2:2 + ng]
        h0_ref = refs[2 + ng]
        sh_out = refs[3 + ng:3 + 2 * ng]
        send_sems, recv_sems = refs[3 + 2 * ng:]
        for phase, step in enumerate((0, nt // 2, nt - 1)):
            @pl.when(pl.program_id(0) == step)
            def _():
                _gather_phase(phase, sh_in, sh_out, send_sems, recv_sems)

        h0_ref[...] = _rms_fwd(x_ref[...], g_ref[...]).astype(BF)

    return pl.pallas_call(
        body, name="prenorm", grid=(nt,),
        in_specs=[_rows(tt, D), _whole((1, D))] + [ANY] * ng, out_specs=[_rows(tt, D)] + [ANY] * ng,
        out_shape=[S((T, D), BF)] + [S((4,) + s.shape, s.dtype) for s in shards],
        scratch_shapes=[pltpu.SemaphoreType.DMA((6 * ng,)), pltpu.SemaphoreType.DMA((6 * ng,))],
        compiler_params=_params(),
    )(x, g0, *shards)


def _blockdiag_mm(v, w4_ref):
    return jnp.concatenate([_mm(v[:, 256 * j:256 * (j + 1)], w4_ref[j]) for j in range(4)], axis=1)


def _lru_gates(lx, p_ref, wa_ref, wx_ref):
    r = _sigmoid(_blockdiag_mm(lx, wa_ref) + p_ref[5:6, :])
    i = _sigmoid(_blockdiag_mm(lx, wx_ref) + p_ref[6:7, :])
    sp = _softplus(-p_ref[7:8, :])
    la = -LRU_C * r * sp
    a = jnp.exp(la)
    th = jnp.tanh(la)
    mult = jnp.sqrt(-2.0 * th / (1.0 - th))
    return r, i, sp, a, mult


def _conv_from(xp_ref, p_ref, n):
    acc = p_ref[4:5, :] + p_ref[0:1, :] * xp_ref[pl.ds(8 - CONV_K + 1, n), :]
    for k in range(1, CONV_K):
        acc = acc + p_ref[k:k + 1, :] * xp_ref[pl.ds(8 - CONV_K + 1 + k, n), :]
    return acc


def _conv_bwd(dp_ref, dconv, x, p_ref, st_ref, n):
    if dconv is None:
        dconv = dp_ref[0:n, :]
    else:
        dp_ref[0:n, :] = dconv
    acc = None
    for k in range(CONV_K):
        g = dp_ref[pl.ds(CONV_K - 1 - k, n), :]
        acc = p_ref[k:k + 1, :] * g if acc is None else acc + p_ref[k:k + 1, :] * g
        st_ref[k:k + 1, :] += jnp.sum(g * x, axis=0, keepdims=True)
    st_ref[4:5, :] += jnp.sum(dconv, axis=0, keepdims=True)
    dp_ref[n:n + 8, :] = dp_ref[0:8, :]
    return acc


def _lru_fwd(h0, wcat, p_lru, wa4, wx4, shards):
    T = h0.shape[0]
    NT = T // TT
    ng = len(shards)

    def body(*refs):
        h0_ref, w_hbm, p_ref, wa_ref, wx_ref = refs[:5]
        sh_in = refs[5:5 + ng]
        h_ref, y_ref, lxc_ref, lxr_ref, lg_ref = refs[5 + ng:10 + ng]
        sh_out = refs[10 + ng:10 + 2 * ng]
        xp, a_s, u_s, hc, w_vm, wsem, send_sems, recv_sems = refs[10 + 2 * ng:]
        _load_once([(w_hbm.at[:, 0:2 * LW], w_vm)], wsem)
        for phase, step in enumerate((0, NT // 2, NT - 1)):
            @pl.when(pl.program_id(0) == step)
            def _():
                _gather_phase(phase, sh_in, sh_out, send_sems, recv_sems)

        @pl.when(pl.program_id(0) == 0)
        def _():
            xp[0:8, :] = jnp.zeros((8, LW), F32)
            hc[...] = jnp.zeros_like(hc)

        hv = h0_ref[...]
        lxr = jnp.dot(hv, w_vm[:, 0:LW], preferred_element_type=F32)
        lxr_ref[...] = lxr
        lg_ref[...] = jnp.dot(hv, w_vm[:, LW:2 * LW], preferred_element_type=F32)
        xp[8:8 + TT, :] = lxr
        lx = _conv_from(xp, p_ref, TT)
        lxc_ref[...] = lx
        xp[0:8, :] = xp[TT:TT + 8, :]
        r, i, sp, a, mult = _lru_gates(lx, p_ref, wa_ref, wx_ref)
        a_s[...] = a
        u_s[...] = mult * (i * lx)

        def step(t, h):
            h = a_s[pl.ds(t, 1), :] * h + u_s[pl.ds(t, 1), :]
            h_ref[pl.ds(t, 1), :] = h
            return h

        hc[0:1, :] = lax.fori_loop(0, TT, step, hc[0:1, :], unroll=8)
        gated = h_ref[...] * _gelu(lg_ref[...])
        y_ref[...] = _rms_fwd(gated, p_ref[8:9, :]).astype(BF)

    return pl.pallas_call(
        body, name="lru_fwd", grid=(NT,),
        in_specs=[_rows(TT, D), ANY, _whole((16, LW)), _whole((4, 256, 256)), _whole((4, 256, 256))] + [ANY] * ng,
        out_specs=[_rows(TT, LW), _rows(TT, LW), _rows(TT, LW), _rows(TT, LW), _rows(TT, LW)] + [ANY] * ng,
        out_shape=[S((T, LW), F32), S((T, LW), BF), S((T, LW), F32), S((T, LW), F32), S((T, LW), F32)]
        + [S((4,) + s.shape, s.dtype) for s in shards],
        scratch_shapes=[pltpu.VMEM((TT + 8, LW), F32), pltpu.VMEM((TT, LW), F32), pltpu.VMEM((TT, LW), F32),
                        pltpu.VMEM((8, LW), F32), pltpu.VMEM((D, 2 * LW), BF), pltpu.SemaphoreType.DMA((1,)),
                        pltpu.SemaphoreType.DMA((6 * ng,)), pltpu.SemaphoreType.DMA((6 * ng,))],
        compiler_params=_params(),
    )(h0, wcat, p_lru, wa4, wx4, *shards)


def _ssd_prep(cv, dt_ref, hp_ref):
    sg = _sigmoid(cv)
    xbc = cv * sg
    lane = lax.broadcasted_iota(jnp.int32, (CH, DTP), 1)
    raw = dt_ref[...] + hp_ref[0:1, :]
    dtv = jnp.where(lane < NH, _softplus(raw), 0.0)
    A = jnp.where(lane[0:1, :] < NH, -jnp.exp(hp_ref[1:2, :]), 0.0)
    cs = _cumsum_rows(dtv * A, CH)
    return sg, xbc, raw, dtv, A, cs


def _per_head_lanes(v):
    r = v.shape[0]
    first = lax.broadcasted_iota(jnp.int32, (r, 2 * HD), 1) < HD
    pairs = [jnp.where(first, jnp.broadcast_to(v[:, 2 * j:2 * j + 1], (r, 2 * HD)),
                       jnp.broadcast_to(v[:, 2 * j + 1:2 * j + 2], (r, 2 * HD))) for j in range(NH // 2)]
    return jnp.concatenate(pairs, axis=1)


def _per_head_rows(col, g):
    return jnp.concatenate([jnp.broadcast_to(col[g * HPG + k:g * HPG + k + 1, :], (HD, NS)) for k in range(HPG)], axis=0)


def _ssd_decays(cs):
    csT = cs.T
    cl = cs[CH - 1:CH, :]
    E_x = _per_head_lanes(jnp.exp(cs))
    dsm = jnp.exp(cl - cs)
    ds_x = _per_head_lanes(dsm)
    El_rows = jnp.broadcast_to(jnp.exp(csT[0:NH, CH - 1:CH]), (NH, NS))
    return csT, dsm, E_x, ds_x, El_rows


def _ssd_fwd(h0, wcat, cw_ssd, hp_ssd, g_ssd, shards):
    T = h0.shape[0]
    NC = T // CH
    ng = len(shards)
    c0 = 2 * LW

    def body(*refs):
        h0_ref, w_hbm, cw_ref, hp_ref, g_ref = refs[:5]
        sh_in = refs[5:5 + ng]
        y_ref, yn_ref, st_ref, cv_ref, z_ref, xr_ref, dt_ref = refs[5 + ng:12 + ng]
        sh_out = refs[12 + ng:12 + 2 * ng]
        xp, st, w_vm, wsem, send_sems, recv_sems = refs[12 + 2 * ng:]
        _load_once([(w_hbm.at[:, c0:PC], w_vm)], wsem)
        for phase, step in enumerate((0, NC // 2, NC - 1)):
            @pl.when(pl.program_id(0) == step)
            def _():
                _gather_phase(phase, sh_in, sh_out, send_sems, recv_sems)

        @pl.when(pl.program_id(0) == 0)
        def _():
            xp[0:8, :] = jnp.zeros((8, XBC), F32)
            st[...] = jnp.zeros_like(st)

        hv = h0_ref[...]
        z_ref[...] = jnp.dot(hv, w_vm[:, 0:SI], preferred_element_type=F32)
        xraw = jnp.dot(hv, w_vm[:, SI:SI + XBC], preferred_element_type=F32)
        xr_ref[...] = xraw
        dt_ref[...] = jnp.dot(hv, w_vm[:, SI + XBC:SI + XBC + DTP], preferred_element_type=F32)
        xp[8:8 + CH, :] = xraw
        cv = _conv_from(xp, cw_ref, CH)
        cv_ref[...] = cv
        sg, xbc, raw, dtv, A, cs = _ssd_prep(cv, dt_ref, hp_ref)
        xp[0:8, :] = xp[CH:CH + 8, :]
        st_ref[0] = st[...]
        csT, dsm, E_x, ds_x, El_rows = _ssd_decays(cs)
        X = xbc[:, 0:SI]
        xs = X * _per_head_lanes(dtv)
        xsd = (xs * ds_x).astype(BF)
        DX = _per_head_lanes(hp_ref[...])[2:3, :] * X
        tril = lax.broadcasted_iota(jnp.int32, (CH, CH), 0) >= lax.broadcasted_iota(jnp.int32, (CH, CH), 1)
        first = lax.broadcasted_iota(jnp.int32, (CH, 2 * HD), 1) < HD
        GW = HPG * HD
        for g in range(NG):
            Bg = xbc[:, SI + NS * g:SI + NS * (g + 1)].astype(BF)
            Cg = xbc[:, SI + NG * NS + NS * g:SI + NG * NS + NS * (g + 1)].astype(BF)
            G = _mm_nt(Cg, Bg)
            Sg = st[GW * g:GW * (g + 1), :]
            Yo = _mm_nt(Cg, Sg) * E_x[:, GW * g:GW * (g + 1)]
            st[GW * g:GW * (g + 1), :] = _per_head_rows(El_rows, g) * Sg + _mm_tn(xsd[:, GW * g:GW * (g + 1)], Bg)
            for jj in range(HPG // 2):
                j = g * (HPG // 2) + jj
                ps = slice(2 * HD * j, 2 * HD * (j + 1))
                xs_pair = xs[:, ps]
                acc = Yo[:, 2 * HD * jj:2 * HD * (jj + 1)] + DX[:, ps]
                for e in range(2):
                    h = 2 * j + e
                    Lm = jnp.exp(jnp.where(tril, cs[:, h:h + 1] - csT[h:h + 1, :], -1e30))
                    acc = acc + _mm(G * Lm, jnp.where(first if e == 0 else ~first, xs_pair, 0.0))
                y_ref[:, ps] = acc
        zz = z_ref[...]
        gated = y_ref[...] * (zz * _sigmoid(zz))
        yn_ref[...] = _rms_fwd(gated, g_ref[...]).astype(BF)

    return pl.pallas_call(
        body, name="ssd_fwd", grid=(NC,),
        in_specs=[_rows(CH, D), ANY, _whole((8, XBC)), _whole((8, DTP)), _whole((1, SI))] + [ANY] * ng,
        out_specs=[_rows(CH, SI), _rows(CH, SI), pl.BlockSpec((1, NH * HD, NS), lambda i: (i, 0, 0)), _rows(CH, XBC),
                   _rows(CH, SI), _rows(CH, XBC), _rows(CH, DTP)] + [ANY] * ng,
        out_shape=[S((T, SI), F32), S((T, SI), BF), S((NC, NH * HD, NS), F32), S((T, XBC), F32),
                   S((T, SI), F32), S((T, XBC), F32), S((T, DTP), F32)] + [S((4,) + s.shape, s.dtype) for s in shards],
        scratch_shapes=[pltpu.VMEM((CH + 8, XBC), F32), pltpu.VMEM((NH * HD, NS), F32),
                        pltpu.VMEM((D, PC - c0), BF), pltpu.SemaphoreType.DMA((1,)),
                        pltpu.SemaphoreType.DMA((6 * ng,)), pltpu.SemaphoreType.DMA((6 * ng,))],
        compiler_params=_params(),
    )(h0, wcat, cw_ssd, hp_ssd, g_ssd, *shards)


def _outproj(ylru, yssd, x, wout, g_pm, g_pf):
    T = x.shape[0]

    def body(yl_ref, ys_ref, x_ref, w_hbm, gpm_ref, gpf_ref, mix_ref, x1_ref, h2_ref, w_vm, sem):
        _load_once([(w_hbm, w_vm)], sem)
        mix = (jnp.dot(yl_ref[...], w_vm[0:LW, :], preferred_element_type=F32)
               + jnp.dot(ys_ref[...], w_vm[LW:LW + SI, :], preferred_element_type=F32))
        mix_ref[...] = mix
        x1 = x_ref[...] + _rms_fwd(mix, gpm_ref[...])
        x1_ref[...] = x1
        h2_ref[...] = _rms_fwd(x1, gpf_ref[...]).astype(BF)

    return pl.pallas_call(
        body, name="outproj", grid=(T // TW,),
        in_specs=[_rows(TW, LW), _rows(TW, SI), _rows(TW, D), ANY, _whole((1, D)), _whole((1, D))],
        out_specs=[_rows(TW, D), _rows(TW, D), _rows(TW, D)],
        out_shape=[S((T, D), F32), S((T, D), F32), S((T, D), BF)],
        scratch_shapes=[pltpu.VMEM((LW + SI, D), BF), pltpu.SemaphoreType.DMA((1,))],
        compiler_params=_params(),
    )(ylru, yssd, x, wout, g_pm, g_pf)


def _ffn_fwd(h2, x1, tgt, wg, wu, wd, g_pff):
    T = x1.shape[0]

    def body(h2_ref, x1_ref, t_ref, wg_hbm, wu_hbm, wd_hbm, g_ref,
             gate_ref, up_ref, act_ref, df_ref, dx2_ref, st_ref, wg_vm, wu_vm, wd_vm, sem):
        _load_once([(wg_hbm, wg_vm), (wu_hbm, wu_vm), (wd_hbm, wd_vm)], sem)

        @pl.when(pl.program_id(0) == 0)
        def _():
            st_ref[...] = jnp.zeros_like(st_ref)

        h2 = h2_ref[...]
        gate = jnp.dot(h2, wg_vm[...], preferred_element_type=F32)
        up = jnp.dot(h2, wu_vm[...], preferred_element_type=F32)
        gate_ref[...] = gate
        up_ref[...] = up
        act = (gate * _sigmoid(gate) * up).astype(BF)
        act_ref[...] = act
        f = jnp.dot(act, wd_vm[...], preferred_element_type=F32)
        g = g_ref[...]
        x2 = x1_ref[...] + _rms_fwd(f, g)
        err = x2 - t_ref[...]
        st_ref[0:1, :] += 0.5 * jnp.sum(err * err, axis=0, keepdims=True) * (1.0 / D)
        dx2 = err * (1.0 / D)
        dx2_ref[...] = dx2
        df, dg = _rms_bwd(f, g, dx2)
        df_ref[...] = df.astype(BF)
        st_ref[1:2, :] += dg

    return pl.pallas_call(
        body, name="ffn_fwd", grid=(T // TT,),
        in_specs=[_rows(TT, D), _rows(TT, D), _rows(TT, D), ANY, ANY, ANY, _whole((1, D))],
        out_specs=[_rows(TT, DFF), _rows(TT, DFF), _rows(TT, DFF), _rows(TT, D), _rows(TT, D), _whole((8, D))],
        out_shape=[S((T, DFF), F32), S((T, DFF), F32), S((T, DFF), BF), S((T, D), BF), S((T, D), F32), S((8, D), F32)],
        scratch_shapes=[pltpu.VMEM((D, DFF), BF), pltpu.VMEM((D, DFF), BF), pltpu.VMEM((DFF, D), BF),
                        pltpu.SemaphoreType.DMA((3,))],
        compiler_params=_params(),
    )(h2, x1, tgt, wg, wu, wd, g_pff)


def _ffn_bwd(df, gate, up, wdT, wgT, wuT):
    T = df.shape[0]

    def body(df_ref, gate_ref, up_ref, wd_hbm, wg_hbm, wu_hbm, dgate_ref, dup_ref, dh2_ref, wd_vm, wg_vm, wu_vm, sem):
        _load_once([(wd_hbm, wd_vm), (wg_hbm, wg_vm), (wu_hbm, wu_vm)], sem)
        dact = jnp.dot(df_ref[...], wd_vm[...], preferred_element_type=F32)
        gate = gate_ref[...]
        s = _sigmoid(gate)
        dup = (dact * (gate * s)).astype(BF)
        dgate = (dact * up_ref[...] * (s + gate * s * (1.0 - s))).astype(BF)
        dup_ref[...] = dup
        dgate_ref[...] = dgate
        dh2_ref[...] = (jnp.dot(dgate, wg_vm[...], preferred_element_type=F32)
                        + jnp.dot(dup, wu_vm[...], preferred_element_type=F32))

    return pl.pallas_call(
        body, name="ffn_bwd", grid=(T // TT,),
        in_specs=[_rows(TT, D), _rows(TT, DFF), _rows(TT, DFF), ANY, ANY, ANY],
        out_specs=[_rows(TT, DFF), _rows(TT, DFF), _rows(TT, D)],
        out_shape=[S((T, DFF), BF), S((T, DFF), BF), S((T, D), F32)],
        scratch_shapes=[pltpu.VMEM((D, DFF), BF), pltpu.VMEM((DFF, D), BF), pltpu.VMEM((DFF, D), BF),
                        pltpu.SemaphoreType.DMA((3,))],
        compiler_params=_params(),
    )(df, gate, up, wdT, wgT, wuT)


def _mix_bwd(dh2, x1, dx2, mix, woutT, g_pf, g_pm):
    T = x1.shape[0]

    def body(dh2_ref, x1_ref, dx2_ref, mix_ref, w_hbm, gpf_ref, gpm_ref,
             dx1_ref, dmix_ref, dyl_ref, dys_ref, st_ref, w_vm, sem):
        _load_once([(w_hbm, w_vm)], sem)

        @pl.when(pl.program_id(0) == 0)
        def _():
            st_ref[...] = jnp.zeros_like(st_ref)

        dxa, dgpf = _rms_bwd(x1_ref[...], gpf_ref[...], dh2_ref[...])
        dx1 = dx2_ref[...] + dxa
        dx1_ref[...] = dx1
        dmix, dgpm = _rms_bwd(mix_ref[...], gpm_ref[...], dx1)
        dmix = dmix.astype(BF)
        dmix_ref[...] = dmix
        st_ref[0:1, :] += dgpf
        st_ref[1:2, :] += dgpm
        dyl_ref[...] = jnp.dot(dmix, w_vm[:, 0:LW], preferred_element_type=F32)
        dys_ref[...] = jnp.dot(dmix, w_vm[:, LW:LW + SI], preferred_element_type=F32)

    return pl.pallas_call(
        body, name="mix_bwd", grid=(T // TW,),
        in_specs=[_rows(TW, D), _rows(TW, D), _rows(TW, D), _rows(TW, D), ANY, _whole((1, D)), _whole((1, D))],
        out_specs=[_rows(TW, D), _rows(TW, D), _rows(TW, LW), _rows(TW, SI), _whole((8, D))],
        out_shape=[S((T, D), F32), S((T, D), BF), S((T, LW), F32), S((T, SI), F32), S((8, D), F32)],
        scratch_shapes=[pltpu.VMEM((D, LW + SI), BF), pltpu.SemaphoreType.DMA((1,))],
        compiler_params=_params(),
    )(dh2, x1, dx2, mix, woutT, g_pf, g_pm)


def _halo(width, n_tiles, tile):
    per = tile // 8
    return pl.BlockSpec((8, width), lambda i: (jnp.maximum((n_tiles - 1 - i) * per - 1, 0), 0))


def _lru_bwd(dy, lxr, lxc, lg, h, p_lru, wa4, wx4, wa4T, wx4T, bufs):
    T = dy.shape[0]
    NT = T // TT
    nb = len(bufs)

    def body(*refs):
        dy_ref, lxr_ref, lxc_ref, lg_ref, h_ref, hh_ref, p_ref, wa_ref, wx_ref, waT_ref, wxT_ref = refs[:11]
        b_in = refs[11:11 + nb]
        dlx_ref, dlg_ref, st_ref, dwa_ref, dwx_ref = refs[11 + nb:16 + nb]
        b_out = refs[16 + nb:16 + 2 * nb]
        hp, dp, a_s, d_s, g_s, cc, send_sems, recv_sems = refs[16 + 2 * nb:]
        for phase, step in enumerate((0, NT - 1)):
            @pl.when(pl.program_id(0) == step)
            def _():
                _pair_phase(phase, b_in, b_out, send_sems, recv_sems)

        dy = dy_ref[...]
        first = pl.program_id(0) == 0
        top = pl.program_id(0) == NT - 1

        @pl.when(first)
        def _():
            st_ref[...] = jnp.zeros_like(st_ref)
            dwa_ref[...] = jnp.zeros_like(dwa_ref)
            dwx_ref[...] = jnp.zeros_like(dwx_ref)
            dp[TT:TT + 8, :] = jnp.zeros((8, LW), F32)
            cc[...] = jnp.zeros_like(cc)

        hp[0:8, :] = hh_ref[...] * jnp.where(top, 0.0, 1.0)
        hp[8:8 + TT, :] = h_ref[...]
        lx = lxc_ref[...]
        r, i, sp, a, mult = _lru_gates(lx, p_ref, wa_ref, wx_ref)

        lg = lg_ref[...]
        hcur = h_ref[...]
        ge = _gelu(lg)
        dgated, dgn = _rms_bwd(hcur * ge, p_ref[8:9, :], dy)
        st_ref[8:9, :] += dgn
        dlg_ref[...] = (dgated * hcur * _gelu_grad(lg)).astype(BF)
        a_s[...] = a
        d_s[...] = dgated * ge

        def step(k, c):
            t = TT - 1 - k
            g = d_s[pl.ds(t, 1), :] + c
            g_s[pl.ds(t, 1), :] = g
            return a_s[pl.ds(t, 1), :] * g

        cc[0:1, :] = lax.fori_loop(0, TT, step, cc[0:1, :], unroll=8)
        gt = g_s[...]
        da = gt * hp[pl.ds(7, TT), :]
        dmult = gt * i * lx
        di = gt * mult * lx
        dlxc = gt * mult * i
        dla = da * a - dmult * (a * a) / mult
        dr = dla * (-LRU_C * sp)
        st_ref[7:8, :] += jnp.sum(dla * (-LRU_C * r), axis=0, keepdims=True) * (-_sigmoid(-p_ref[7:8, :]))
        dzr = dr * r * (1.0 - r)
        dzi = di * i * (1.0 - i)
        st_ref[5:6, :] += jnp.sum(dzr, axis=0, keepdims=True)
        st_ref[6:7, :] += jnp.sum(dzi, axis=0, keepdims=True)
        dlxc = dlxc + _blockdiag_mm(dzr, waT_ref) + _blockdiag_mm(dzi, wxT_ref)
        for j in range(4):
            sl = slice(256 * j, 256 * (j + 1))
            pa = _mm_tn(lx[:, sl], dzr[:, sl])
            px = _mm_tn(lx[:, sl], dzi[:, sl])
            for b in range(4):
                bs = slice(BW * b, BW * (b + 1))
                dwa_ref[4 * j + b] += pa[bs, bs]
                dwx_ref[4 * j + b] += px[bs, bs]
        dlx_ref[...] = _conv_bwd(dp, dlxc, lxr_ref[...], p_ref, st_ref, TT).astype(BF)

    w4 = _whole((4, 256, 256))
    return pl.pallas_call(
        body, name="lru_bwd", grid=(NT,),
        in_specs=[_rows(TT, LW, NT), _rows(TT, LW, NT), _rows(TT, LW, NT), _rows(TT, LW, NT), _rows(TT, LW, NT),
                  _halo(LW, NT, TT), _whole((16, LW)), w4, w4, w4, w4] + [ANY] * nb,
        out_specs=[_rows(TT, LW, NT), _rows(TT, LW, NT), _whole((16, LW)), _whole((NBLK, BW, BW)), _whole((NBLK, BW, BW))]
        + [ANY] * nb,
        out_shape=[S((T, LW), BF), S((T, LW), BF), S((16, LW), F32), S((NBLK, BW, BW), F32), S((NBLK, BW, BW), F32)]
        + [S(_half_shape(b), b.dtype) for b in bufs],
        scratch_shapes=[pltpu.VMEM((TT + 8, LW), F32), pltpu.VMEM((TT + 8, LW), F32),
                        pltpu.VMEM((TT, LW), F32), pltpu.VMEM((TT, LW), F32), pltpu.VMEM((TT, LW), F32),
                        pltpu.VMEM((8, LW), F32), pltpu.SemaphoreType.DMA((nb,)), pltpu.SemaphoreType.DMA((nb,))],
        compiler_params=_params(),
    )(dy, lxr, lxc, lg, h, h, p_lru, wa4, wx4, wa4T, wx4T, *bufs)


def _ssd_bwd(dyn, xbcr, cv, z, dtr, y, states, cw_ssd, hp_ssd, g_ssd, parts):
    T = dyn.shape[0]
    NC = T // CH
    nq = len(parts)

    def body(*refs):
        dyn_ref, xr_ref, cv_ref, z_ref, dt_ref, y_ref, st_ref, cw_ref, hp_ref, g_ref = refs[:10]
        q_in = refs[10:10 + nq]
        dxbc_ref, dz_ref, ddt_ref, cst_ref, hst_ref, gst_ref = refs[10 + nq:16 + nq]
        q_out = refs[16 + nq:16 + 2 * nq]
        dp, dS, send_sems, recv_sems = refs[16 + 2 * nq:]
        dyn = dyn_ref[...]
        first = pl.program_id(0) == 0
        for phase, step in enumerate((0, NC - 1)):
            @pl.when(pl.program_id(0) == step)
            def _():
                _quad_phase(phase, q_in, q_out, send_sems, recv_sems)

        @pl.when(first)
        def _():
            cst_ref[...] = jnp.zeros_like(cst_ref)
            hst_ref[...] = jnp.zeros_like(hst_ref)
            gst_ref[...] = jnp.zeros_like(gst_ref)
            dp[CH:CH + 8, :] = jnp.zeros((8, XBC), F32)
            dS[...] = jnp.zeros_like(dS)

        cv = cv_ref[...]
        sg, xbc, raw, dtv, A, cs = _ssd_prep(cv, dt_ref, hp_ref)
        csT, dsm, E_x, ds_x, El_rows = _ssd_decays(cs)
        row_i = lax.broadcasted_iota(jnp.int32, (CH, CH), 0)
        col_i = lax.broadcasted_iota(jnp.int32, (CH, CH), 1)
        tril = row_i >= col_i
        first = col_i < HD
        head_of = ((lax.broadcasted_iota(jnp.int32, (DTP, SI), 1) >> 6)
                   == lax.broadcasted_iota(jnp.int32, (DTP, SI), 0)).astype(BF)
        head_ofT = ((lax.broadcasted_iota(jnp.int32, (SI, DTP), 0) >> 6)
                    == lax.broadcasted_iota(jnp.int32, (SI, DTP), 1)).astype(BF)

        def hi_lo(v):
            hi = v.astype(BF)
            return hi, (v - hi.astype(F32)).astype(BF)

        GW = HPG * HD

        def lane_sums(v, g):
            hi, lo = hi_lo(v)
            w = head_ofT[GW * g:GW * (g + 1), :]
            return _mm(hi, w) + _mm(lo, w)

        zz = z_ref[...]
        sz = _sigmoid(zz)
        yv = y_ref[...]
        dgn, dg = _rms_bwd(yv * (zz * sz), g_ref[...], dyn)
        gst_ref[0:1, :] += dg
        dz_ref[...] = (dgn * yv * (sz + zz * sz * (1.0 - sz))).astype(BF)
        dY = dgn * (zz * sz)

        X = xbc[:, 0:SI]
        dsilu = sg + cv * sg * (1.0 - sg)
        dt_x = _per_head_lanes(dtv)
        xs = X * dt_x
        xsd = (xs * ds_x).astype(BF)
        D_x = _per_head_lanes(hp_ref[...])[2:3, :]
        zero = jnp.zeros((CH, DTP), F32)
        dcs_col = zero
        dcs_row = zero
        dds, ddt_col, dD_rows, dcl_rows = zero, zero, zero, zero
        for g in range(NG):
            gs = slice(GW * g, GW * (g + 1))
            Bg = xbc[:, SI + NS * g:SI + NS * (g + 1)].astype(BF)
            Cg = xbc[:, SI + NG * NS + NS * g:SI + NG * NS + NS * (g + 1)].astype(BF)
            G = _mm_nt(Cg, Bg)
            Sg = st_ref[0, gs, :]
            dSe = dS[gs, :]
            dYg = dY[:, gs]
            dcs_col = dcs_col + lane_sums(dYg * (_mm_nt(Cg, Sg) * E_x[:, gs]), g)
            dD_rows = dD_rows + lane_sums(dYg * X[:, gs], g)
            dP = dYg * E_x[:, gs]
            dCg = _mm(dP, Sg)
            dS[gs, :] = _mm_tn(dP, Cg) + _per_head_rows(El_rows, g) * dSe
            t_hi, t_lo = hi_lo(dSe * Sg)
            dcl_rows = dcl_rows + _mm(head_of[:, gs], t_hi) + _mm(head_of[:, gs], t_lo)
            Q = _mm_nt(Bg, dSe)
            dds = dds + lane_sums(Q * xs[:, gs], g)
            dBg = _mm(xsd[:, gs], dSe)
            dG = jnp.zeros((CH, CH), F32)
            dxs_pairs = []
            for jj in range(HPG // 2):
                j = g * (HPG // 2) + jj
                ps = slice(2 * HD * j, 2 * HD * (j + 1))
                xs_pair = xs[:, ps]
                dxs_pair = Q[:, 2 * HD * jj:2 * HD * (jj + 1)] * ds_x[:, ps]
                for e in range(2):
                    h = 2 * j + e
                    Lm = jnp.exp(jnp.where(tril, cs[:, h:h + 1] - csT[h:h + 1, :], -1e30))
                    M = G * Lm
                    dYm = jnp.where(first if e == 0 else ~first, dY[:, ps], 0.0).astype(BF)
                    dM = _mm_nt(dYm, xs_pair)
                    dxs_pair = dxs_pair + _mm_tn(M, dYm)
                    Wm = dM * M
                    dcs_col = dcs_col + jnp.where(col_i == h, jnp.sum(Wm, axis=1, keepdims=True), 0.0)
                    dcs_row = dcs_row + jnp.where(row_i == h, -jnp.sum(Wm, axis=0, keepdims=True), 0.0)
                    dG = dG + dM * Lm
                dp[0:CH, ps] = (D_x[:, ps] * dY[:, ps] + dxs_pair * dt_x[:, ps]) * dsilu[:, ps]
                dxs_pairs.append(dxs_pair)
            ddt_col = ddt_col + lane_sums(jnp.concatenate(dxs_pairs, axis=1) * X[:, gs], g)
            bs = slice(SI + NS * g, SI + NS * (g + 1))
            cs_ = slice(SI + NG * NS + NS * g, SI + NG * NS + NS * (g + 1))
            dp[0:CH, bs] = (dBg + _mm_tn(dG, Cg)) * dsilu[:, bs]
            dp[0:CH, cs_] = (dCg + _mm(dG, Bg)) * dsilu[:, cs_]

        dds = dds * dsm
        dcs_col = dcs_col - dds
        dD = jnp.sum(dD_rows, axis=0, keepdims=True)
        dcl_rows = jnp.sum(dcl_rows, axis=1, keepdims=True) * jnp.exp(csT[:, CH - 1:CH])
        dcs_row = dcs_row + jnp.where(col_i == CH - 1, dcl_rows, 0.0)
        dcs_col = dcs_col + jnp.where(row_i == CH - 1, jnp.sum(dds, axis=0, keepdims=True), 0.0)

        da = _rev_cumsum_rows(dcs_col + dcs_row.T, CH)
        ddt_col = ddt_col + da * A
        hst_ref[1:2, :] += jnp.sum(da * dtv, axis=0, keepdims=True) * A
        hst_ref[2:3, :] += dD
        draw = jnp.where(col_i < NH, ddt_col * _sigmoid(raw), 0.0)
        ddt_ref[...] = draw.astype(BF)
        hst_ref[0:1, :] += jnp.sum(draw, axis=0, keepdims=True)

        dxbc_ref[...] = _conv_bwd(dp, None, xr_ref[...], cw_ref, cst_ref, CH).astype(BF)

    return pl.pallas_call(
        body, name="ssd_bwd", grid=(NC,),
        in_specs=[_rows(CH, SI, NC), _rows(CH, XBC, NC), _rows(CH, XBC, NC), _rows(CH, SI, NC), _rows(CH, DTP, NC),
                  _rows(CH, SI, NC), pl.BlockSpec((1, NH * HD, NS), lambda i: (NC - 1 - i, 0, 0)),
                  _whole((8, XBC)), _whole((8, DTP)), _whole((1, SI))] + [ANY] * nq,
        out_specs=[_rows(CH, XBC, NC), _rows(CH, SI, NC), _rows(CH, DTP, NC), _whole((16, XBC)), _whole((16, DTP)),
                   _whole((8, SI))] + [ANY] * nq,
        out_shape=[S((T, XBC), BF), S((T, SI), BF), S((T, DTP), BF), S((16, XBC), F32), S((16, DTP), F32), S((8, SI), F32)]
        + [S(p.shape, p.dtype) for p in parts],
        scratch_shapes=[pltpu.VMEM((CH + 8, XBC), F32), pltpu.VMEM((NH * HD, NS), F32),
                        pltpu.SemaphoreType.DMA((3 * nq,)), pltpu.SemaphoreType.DMA((3 * nq,))],
        compiler_params=_params(),
    )(dyn, xbcr, cv, z, dtr, y, states, cw_ssd, hp_ssd, g_ssd, *parts)


def _inproj_bwd(dlx, dlg, dz, dxbc, ddt, x, dx1, wcatT, g0, parts):
    T = x.shape[0]
    NT = T // TW
    nq = len(parts)

    def body(*refs):
        dlx_ref, dlg_ref, dz_ref, dxbc_ref, ddt_ref, x_ref, dx1_ref, w_hbm, g_ref = refs[:9]
        q_in = refs[9:9 + nq]
        dx_ref, st_ref = refs[9 + nq:11 + nq]
        q_out = refs[11 + nq:11 + 2 * nq]
        w_vm, sem, send_sems, recv_sems = refs[11 + 2 * nq:]
        _load_once([(w_hbm, w_vm)], sem)
        for phase, step in enumerate((0, NT - 1)):
            @pl.when(pl.program_id(0) == step)
            def _():
                _quad_phase(phase, q_in, q_out, send_sems, recv_sems)

        @pl.when(pl.program_id(0) == 0)
        def _():
            st_ref[...] = jnp.zeros_like(st_ref)

        dh = jnp.dot(dlx_ref[...], w_vm[0:1024, :], preferred_element_type=F32)
        dh = dh + jnp.dot(dlg_ref[...], w_vm[1024:2048, :], preferred_element_type=F32)
        dh = dh + jnp.dot(dz_ref[...], w_vm[2048:3072, :], preferred_element_type=F32)
        dh = dh + jnp.dot(dxbc_ref[...], w_vm[3072:3072 + XBC, :], preferred_element_type=F32)
        dh = dh + jnp.dot(ddt_ref[...], w_vm[3072 + XBC:PC, :], preferred_element_type=F32)
        dx, dg = _rms_bwd(x_ref[...], g_ref[...], dh)
        dx_ref[...] = dx1_ref[...] + dx
        st_ref[0:1, :] += dg

    return pl.pallas_call(
        body, name="inproj_bwd", grid=(NT,),
        in_specs=[_rows(TW, 1024), _rows(TW, 1024), _rows(TW, 1024), _rows(TW, XBC), _rows(TW, DTP), _rows(TW, D),
                  _rows(TW, D), ANY, _whole((1, D))] + [ANY] * nq,
        out_specs=[_rows(TW, D), _whole((8, D))] + [ANY] * nq,
        out_shape=[S((T, D), F32), S((8, D), F32)] + [S(p.shape, p.dtype) for p in parts],
        scratch_shapes=[pltpu.VMEM((PC, D), BF), pltpu.SemaphoreType.DMA((1,)),
                        pltpu.SemaphoreType.DMA((3 * nq,)), pltpu.SemaphoreType.DMA((3 * nq,))],
        compiler_params=_params(),
    )(dlx, dlg, dz, dxbc, ddt, x, dx1, wcatT, g0, *parts)


def _wgrad(name, a, b):
    T, M = a.shape
    N = b.shape[1]
    tk = min(T, 2048 if M <= 1024 else 1024)
    tn = N
    while M * tn * 4 > (6 << 20) and tn % 256 == 0:
        tn //= 2

    def body(a_ref, b_ref, o_ref):
        p = lax.dot_general(a_ref[...], b_ref[...], (((0,), (0,)), ((), ())), preferred_element_type=F32)

        @pl.when(pl.program_id(1) == 0)
        def _():
            o_ref[...] = p

        @pl.when(pl.program_id(1) > 0)
        def _():
            o_ref[...] += p

    return pl.pallas_call(
        body, name=name, grid=(N // tn, T // tk),
        in_specs=[pl.BlockSpec((tk, M), lambda j, k: (k, 0)), pl.BlockSpec((tk, tn), lambda j, k: (k, j))],
        out_specs=pl.BlockSpec((M, tn), lambda j, k: (0, j)), out_shape=S((M, N), F32),
        compiler_params=_params(2),
    )(a, b)


def _adamw(name, w, g, m, v):
    _, R, C = w.shape

    def body(w_ref, g_ref, m_ref, v_ref, d_ref, nm_ref, nv_ref):
        d_ref[0], nm_ref[0], nv_ref[0] = _adam_math(w_ref[0], g_ref[...], m_ref[0], v_ref[0])

    if R % 8 == 0:
        tr = _row_tile(R, C)
        n_tiles = R // tr
        blk, gblk = pl.BlockSpec((1, tr, C), lambda i: (0, i, 0)), pl.BlockSpec((tr, C), lambda i: (i, 0))
    else:
        tc = 128 * max(k for k in range(1, C // 128 + 1) if C % (128 * k) == 0 and R * 128 * k * 4 <= (5 << 19))
        n_tiles = C // tc
        blk, gblk = pl.BlockSpec((1, R, tc), lambda i: (0, 0, i)), pl.BlockSpec((R, tc), lambda i: (0, i))
    return pl.pallas_call(
        body, name=name, grid=(n_tiles,),
        in_specs=[blk, gblk, blk, blk], out_specs=[blk] * 3,
        out_shape=[S((1, R, C), F32)] * 3, compiler_params=_params(),
    )(w, g, m, v)


def _pair_exchange(name, bufs):
    n = len(bufs)

    def body(*refs):
        for phase in range(2):
            _pair_phase(phase, refs[:n], refs[n:2 * n], refs[2 * n], refs[2 * n + 1])

    return pl.pallas_call(
        body, name=name, in_specs=[ANY] * n, out_specs=[ANY] * n,
        out_shape=[S(_half_shape(b), b.dtype) for b in bufs],
        scratch_shapes=[pltpu.SemaphoreType.DMA((n,)), pltpu.SemaphoreType.DMA((n,))],
    )(*bufs)


def _quad_exchange(bufs):
    n = len(bufs)

    def body(*refs):
        ins, outs = refs[:n], refs[n:2 * n]
        send_sems, recv_sems = refs[2 * n], refs[2 * n + 1]
        x, y, c = _pos()
        me = 2 * x + y
        chips = _other_chips(x, y)
        copies = []
        for k, (src, dst) in enumerate(zip(ins, outs)):
            for j, (cx, cy) in enumerate(chips):
                cp = _remote(src, dst.at[me], send_sems.at[3 * k + j], recv_sems.at[3 * k + j], (cx, cy, c))
                cp.start()
                copies.append(cp)
        for k, (src, dst) in enumerate(zip(ins, outs)):
            for j, (cx, cy) in enumerate(chips):
                blk = dst.at[2 * cx + cy]
                _remote(blk, blk, send_sems.at[3 * k + j], recv_sems.at[3 * k + j], (cx, cy, c)).wait_recv()
        for cp in copies:
            cp.wait_send()

    return pl.pallas_call(
        body, name="quad_exchange", in_specs=[ANY] * n, out_specs=[ANY] * n,
        out_shape=[S((4,) + b.shape, b.dtype) for b in bufs],
        scratch_shapes=[pltpu.SemaphoreType.DMA((3 * n,)), pltpu.SemaphoreType.DMA((3 * n,))],
    )(*bufs)


def _pair_gather(bufs):
    n = len(bufs)

    def body(*refs):
        ins, outs = refs[:n], refs[n:2 * n]
        send_sems, recv_sems = refs[2 * n], refs[2 * n + 1]
        x, y, c = _pos()
        copies = []
        for k, buf in enumerate(outs):
            mine = _half(buf, c, buf.shape[0] // 2)
            cp = _remote(mine, mine, send_sems.at[k], recv_sems.at[k], (x, y, 1 - c))
            cp.start()
            copies.append(cp)
        for k, buf in enumerate(outs):
            theirs = _half(buf, 1 - c, buf.shape[0] // 2)
            _remote(theirs, theirs, send_sems.at[k], recv_sems.at[k], (x, y, 1 - c)).wait_recv()
        for cp in copies:
            cp.wait_send()

    return pl.pallas_call(
        body, name="pair_gather", in_specs=[ANY] * n, out_specs=[ANY] * n,
        out_shape=[S(b.shape, b.dtype) for b in bufs], input_output_aliases={k: k for k in range(n)},
        scratch_shapes=[pltpu.SemaphoreType.DMA((n,)), pltpu.SemaphoreType.DMA((n,))],
    )(*bufs)


def _row_tile(rows, cols, mult=8):
    best = mult
    for t in range(mult, rows + 1, mult):
        if rows % t == 0 and t * cols * 4 <= (1 << 21):
            best = t
    return best


def _add_own_half(name, full, got, c, out_dtype, by_columns):
    hr = got.shape[-2]
    wide = got.shape[-1]
    cols = wide // 4 if by_columns else wide
    tr = _row_tile(hr, wide, 16)
    per = hr // tr

    if by_columns:
        def body(c_ref, a_ref, b_ref, o_ref):
            v = a_ref[...] + b_ref[...]
            for j in range(4):
                o_ref[j] = v[:, j * cols:(j + 1) * cols].astype(out_dtype)

        in_specs = [pl.BlockSpec((tr, wide), lambda i, c_ref: (c_ref[0] * per + i, 0)),
                    pl.BlockSpec((tr, wide), lambda i, c_ref: (i, 0))]
        out_specs = pl.BlockSpec((4, tr, cols), lambda i, c_ref: (0, i, 0))
        grid = (per,)
    else:
        def body(c_ref, a_ref, b_ref, o_ref):
            o_ref[...] = (a_ref[...] + b_ref[...]).astype(out_dtype)

        in_specs = [pl.BlockSpec((1, tr, cols), lambda s, i, c_ref: (s, c_ref[0] * per + i, 0)),
                    pl.BlockSpec((1, tr, cols), lambda s, i, c_ref: (s, i, 0))]
        out_specs = pl.BlockSpec((1, tr, cols), lambda s, i, c_ref: (s, i, 0))
        grid = (4, per)
    return pl.pallas_call(
        body, name=name,
        grid_spec=pltpu.PrefetchScalarGridSpec(num_scalar_prefetch=1, grid=grid, in_specs=in_specs, out_specs=out_specs),
        out_shape=S((4, hr, cols), out_dtype), compiler_params=_params(len(grid)),
    )(jnp.reshape(c, (1,)).astype(jnp.int32), full, got)


def _small_add_own_half(fulls, gots, c):
    n = len(fulls)

    def body(c_ref, *refs):
        for a_ref, b_ref, o_ref in zip(refs[:n], refs[n:2 * n], refs[2 * n:]):
            hr = b_ref.shape[0]
            o_ref[...] = a_ref[pl.ds(pl.multiple_of(c_ref[0] * hr, 8), hr), :] + b_ref[...]

    specs = lambda arrs: [pl.BlockSpec(a.shape, lambda i, c_ref: (0, 0)) for a in arrs]
    return pl.pallas_call(
        body, name="small_pair_add",
        grid_spec=pltpu.PrefetchScalarGridSpec(num_scalar_prefetch=1, grid=(1,), in_specs=specs(fulls) + specs(gots),
                                               out_specs=specs(gots)),
        out_shape=[S(g.shape, F32) for g in gots], compiler_params=_params(),
    )(jnp.reshape(c, (1,)).astype(jnp.int32), *fulls, *gots)


def _small_sum_slots(own, slots, me, c):
    n = len(slots)

    def body(p_ref, *refs):
        own_refs, slot_refs, o_refs = refs[:n], refs[n:5 * n], refs[5 * n:]
        for i, (own_ref, o_ref) in enumerate(zip(own_refs, o_refs)):
            hr = own_ref.shape[0]
            acc = None
            for j in range(4):
                v = jnp.where(p_ref[0] == j, own_ref[...], slot_refs[4 * i + j][0])
                acc = v if acc is None else acc + v
            o_ref[pl.ds(pl.multiple_of(p_ref[1] * hr, 8), hr), :] = acc

    def slot_spec(s, j):
        return pl.BlockSpec((1,) + s.shape[1:], lambda i, p: (jnp.where(p[0] == j, (j + 1) % 4, j), 0, 0))

    outs = [S((2 * s.shape[1], s.shape[2]), F32) for s in slots]
    return pl.pallas_call(
        body, name="small_quad_sum",
        grid_spec=pltpu.PrefetchScalarGridSpec(
            num_scalar_prefetch=1, grid=(1,),
            in_specs=[pl.BlockSpec(o.shape, lambda i, p: (0, 0)) for o in own]
            + [slot_spec(s, j) for s in slots for j in range(4)],
            out_specs=[pl.BlockSpec(o.shape, lambda i, p: (0, 0)) for o in outs]),
        out_shape=outs, compiler_params=_params(),
    )(jnp.stack([me, c]).astype(jnp.int32), *own, *[s for s in slots for _ in range(4)])


def _sum_slots(name, own, slots, me, c):
    _, rows, cols = slots.shape
    tr = _row_tile(rows, cols, 16 if slots.dtype == jnp.bfloat16 else 8)
    per = rows // tr
    three = len(own.shape) == 3

    def body(p_ref, own_ref, s0, s1, s2, s3, o_ref):
        mine = own_ref[0] if three else own_ref[...]
        acc = None
        for j, s_ref in enumerate((s0, s1, s2, s3)):
            v = jnp.where(p_ref[0] == j, mine, s_ref[0]).astype(F32)
            acc = v if acc is None else acc + v
        o_ref[...] = acc

    def slot_spec(j):
        return pl.BlockSpec((1, tr, cols), lambda i, p: (jnp.where(p[0] == j, (j + 1) % 4, j), i, 0))

    own_spec = (pl.BlockSpec((1, tr, cols), lambda i, p: (p[0], i, 0)) if three
                else pl.BlockSpec((tr, cols), lambda i, p: (i, 0)))
    return pl.pallas_call(
        body, name=name,
        grid_spec=pltpu.PrefetchScalarGridSpec(
            num_scalar_prefetch=1, grid=(per,), in_specs=[own_spec] + [slot_spec(j) for j in range(4)],
            out_specs=pl.BlockSpec((tr, cols), lambda i, p: (p[1] * per + i, 0))),
        out_shape=S((2 * rows, cols), F32), compiler_params=_params(),
    )(jnp.stack([me, c]).astype(jnp.int32), own, slots, slots, slots, slots)


BIG = ("w_in", "w_out", "w_gate", "w_up", "w_down")
ROW_PARAMS = (("pre_mix_norm", 0), ("lru_conv_b", 12), ("lru_ba", 13), ("lru_bx", 14), ("lru_lambda", 15),
              ("lru_out_norm", 16), ("ssd_out_norm", 24), ("post_mix_norm", 33), ("pre_ffn_norm", 32), ("post_ffn_norm", 41))
LRU_CONV_ROWS = (8, 12)
LOSS_ROW = 40
HEAD_PARAMS = (("ssd_dt_bias", 0), ("ssd_a_log", 1), ("ssd_d", 2))
SMALL = tuple(n for n, _ in ROW_PARAMS) + ("ssd_conv_b",) + tuple(n for n, _ in HEAD_PARAMS) + (
    "lru_wa", "lru_wx", "lru_conv_w", "ssd_conv_w")


def _diag4(w):
    eye = jnp.eye(4, dtype=w.dtype).reshape(1, 4, 1, 4, 1)
    return (w.reshape(4, 4, BW, 1, BW) * eye).reshape(4, 4 * BW, 4 * BW)


def _adam_math(w, g, m, v):
    mm = ADAM_B1 * m + (1.0 - ADAM_B1) * g
    vv = ADAM_B2 * v + (1.0 - ADAM_B2) * (g * g)
    c1 = 1.0 - ADAM_B1 ** ADAM_STEP
    c2 = 1.0 - ADAM_B2 ** ADAM_STEP
    return -ADAM_LR * ((mm / c1) / (jnp.sqrt(vv / c2) + ADAM_EPS) + ADAM_WD * w), mm, vv


def _adamw_small(rows, cst, hst, dwa, dwx, glcw, gscw, w, m, v):
    def grad_of(name, refs):
        rows_ref, cst_ref, hst_ref, dwa_ref, dwx_ref, glcw_ref, gscw_ref = refs
        for n, r in ROW_PARAMS:
            if n == name:
                return rows_ref[r:r + 1, :]
        for n, r in HEAD_PARAMS:
            if n == name:
                return hst_ref[r:r + 1, 0:NH]
        return {"ssd_conv_b": lambda: cst_ref[4:5, :], "lru_wa": lambda: dwa_ref[...], "lru_wx": lambda: dwx_ref[...],
                "lru_conv_w": lambda: glcw_ref[...], "ssd_conv_w": lambda: gscw_ref[...]}[name]()

    shapes = {n: (w[n].shape[1:] if len(w[n].shape) > 2 else w[n].shape) for n in SMALL}
    flat = lambda d: [d[n].reshape(shapes[n]) for n in SMALL]
    ns = len(SMALL)

    def body(*refs):
        srcs, rest = refs[:7], refs[7:]
        w_refs, m_refs, v_refs = rest[:ns], rest[ns:2 * ns], rest[2 * ns:3 * ns]
        outs = rest[3 * ns:]
        for k, name in enumerate(SMALL):
            g = grad_of(name, srcs)
            d, mm, vv = _adam_math(w_refs[k][...], g, m_refs[k][...], v_refs[k][...])
            outs[4 * k][...] = g
            outs[4 * k + 1][...] = d
            outs[4 * k + 2][...] = mm
            outs[4 * k + 3][...] = vv

    res = pl.pallas_call(
        body, name="adamw_small",
        out_shape=[S(shapes[n], F32) for n in SMALL for _ in range(4)],
        compiler_params=pltpu.CompilerParams(vmem_limit_bytes=VMEM_LIMIT),
    )(rows, cst, hst, dwa, dwx, glcw, gscw, *flat(w), *flat(m), *flat(v))
    return {n: tuple(res[4 * k + i].reshape(w[n].shape) for i in range(4)) for k, n in enumerate(SMALL)}


def _with_own(own, got):
    chip = 2 * lax.axis_index("x") + lax.axis_index("y")
    return jnp.where((jnp.arange(4) == chip).reshape(4, 1, 1), own[None], got)


def _side_by_side(f):
    return f.transpose(1, 0, 2).reshape(f.shape[1], 4 * f.shape[2])


def _stacked(f):
    return f.reshape(4 * f.shape[1], f.shape[2])


def _conv_terms(lru_conv_w, ssd_conv_w):
    conv = jnp.concatenate([lru_conv_w.reshape(-1), ssd_conv_w.reshape(-1)]).astype(F32)
    hi = conv.astype(jnp.bfloat16)
    mid = (conv - hi.astype(F32)).astype(jnp.bfloat16)
    lo = (conv - hi.astype(F32) - mid.astype(F32)).astype(jnp.bfloat16)
    terms = jnp.concatenate([hi, mid, lo])
    rows = -(-terms.shape[0] // (128 * 32)) * 32
    return jnp.pad(terms, (0, rows * 128 - terms.shape[0])).reshape(rows, 128)


def _full_conv_taps(own, got, n_lru, n_ssd):
    n_terms = 3 * (n_lru + n_ssd)
    t3 = _with_own(own, got).reshape(4, -1)[:, :n_terms].reshape(4, 3, -1).astype(F32)
    conv_f = (t3[:, 0] + t3[:, 1]) + t3[:, 2]
    lcw = conv_f[:, :n_lru].reshape(4, CONV_K, -1).transpose(1, 0, 2).reshape(CONV_K, LW)
    scw = conv_f[:, n_lru:].reshape(4, CONV_K, -1).transpose(1, 0, 2).reshape(CONV_K, XBC)
    return lcw, scw


def _step(x, tgt, w_in, lru_conv_w, ssd_conv_w, sp, late):
    c = lax.axis_index("c")
    me = 2 * lax.axis_index("x") + lax.axis_index("y")
    mm = lambda w: w.astype(BF)
    row = lambda v: v.reshape(1, -1).astype(F32)
    g0 = row(sp["pre_mix_norm"])
    first = [w_in.astype(WIRE), _conv_terms(lru_conv_w, ssd_conv_w)]
    h0, *got_first = _prenorm(x, g0, first)
    win_f = _side_by_side(_with_own(first[0], got_first[0]))
    lcw, scw = _full_conv_taps(first[1], got_first[1], lru_conv_w.size, ssd_conv_w.size)
    wcat = jnp.concatenate([mm(win_f), jnp.zeros((D, PC - IN_COLS), BF)], axis=1)
    p_lru = jnp.concatenate([lcw, row(sp["lru_conv_b"]), row(sp["lru_ba"]), row(sp["lru_bx"]), row(sp["lru_lambda"]),
                             row(sp["lru_out_norm"]), jnp.zeros((7, LW), F32)], axis=0)
    wa4, wx4 = mm(_diag4(sp["lru_wa"][0])), mm(_diag4(sp["lru_wx"][0]))
    wa4T, wx4T = wa4.transpose(0, 2, 1), wx4.transpose(0, 2, 1)
    cw_ssd = jnp.concatenate([scw, row(sp["ssd_conv_b"]), jnp.zeros((3, XBC), F32)], axis=0)
    padh = lambda v: jnp.pad(row(v), ((0, 0), (0, DTP - NH)))
    hp_ssd = jnp.concatenate([padh(sp["ssd_dt_bias"]), padh(sp["ssd_a_log"]), padh(sp["ssd_d"]), jnp.zeros((5, DTP), F32)], axis=0)
    g_ssd = row(sp["ssd_out_norm"])
    g_pm, g_pf, g_pff = row(sp["post_mix_norm"]), row(sp["pre_ffn_norm"]), row(sp["post_ffn_norm"])

    h, ylru, lxc, lxr, lg, *got_a = _lru_fwd(h0, wcat, p_lru, wa4, wx4, [late[0], late[3]])
    y, yssd, states, cv, z, xbcr, dtr, *got_b = _ssd_fwd(h0, wcat, cw_ssd, hp_ssd, g_ssd, [late[1], late[2]])
    wout, wd = mm(_stacked(_with_own(late[0], got_a[0]))), mm(_stacked(_with_own(late[3], got_a[1])))
    wg, wu = mm(_side_by_side(_with_own(late[1], got_b[0]))), mm(_side_by_side(_with_own(late[2], got_b[1])))
    mix, x1, h2 = _outproj(ylru, yssd, x, wout, g_pm, g_pf)
    gate, up, act, df, dx2, st_ffn = _ffn_fwd(h2, x1, tgt, wg, wu, wd, g_pff)
    dgate, dup, dh2 = _ffn_bwd(df, gate, up, wd.T, wg.T, wu.T)
    dx1, dmix, dyl, dys, st_mix = _mix_bwd(dh2, x1, dx2, mix, wout.T, g_pf, g_pm)

    dwg = _wgrad("wgrad_gate", h2, dgate)
    dwu = _wgrad("wgrad_up", h2, dup)
    dwd = _wgrad("wgrad_down", act, df)
    dwo = jnp.concatenate([_wgrad("wgrad_out_lru", ylru, dmix), _wgrad("wgrad_out_ssd", yssd, dmix)], axis=0)
    early = [dwo.reshape(4, (LW + SI) // 4, D), dwg, dwu, dwd.reshape(4, DFF // 4, D)]
    dlx, dlg, st_lru, dwa, dwx, *got_early = _lru_bwd(dyl, lxr, lxc, lg, h, p_lru, wa4, wx4, wa4T, wx4T, early)
    part_early = [_add_own_half("pair_add_early%d" % k, b, r, c, WIRE, bc)
                  for k, (b, r, bc) in enumerate(zip(early, got_early, [False, True, True, False]))]
    dxbc, dz, ddt, cst, hst, gst, *slots_early = _ssd_bwd(dys, xbcr, cv, z, dtr, y, states, cw_ssd, hp_ssd, g_ssd,
                                                          part_early)
    red_early = [_sum_slots("quad_sum_early%d" % k, p, s, me, c) for k, (p, s) in enumerate(zip(part_early, slots_early))]

    pin = [_wgrad("wgrad_in_%d" % k, h0, b) for k, b in enumerate((dlx, dlg, dz, dxbc, ddt))]
    dwin = jnp.concatenate(pin[:4] + [pin[4][:, :NH]], axis=1)
    (got_win,) = _pair_exchange("pair_exchange_w_in", [dwin])
    part_win = _add_own_half("pair_add_w_in", dwin, got_win, c, WIRE, True)
    gx, st_in, slots_win = _inproj_bwd(dlx, dlg, dz, dxbc, ddt, x, dx1, wcat.T, g0, [part_win])
    red_win = _sum_slots("quad_sum_w_in", part_win, slots_win, me, c)

    rows = jnp.concatenate([st_in, st_lru, gst, st_mix, st_ffn], axis=0)
    small = [rows, cst, hst, dwa.reshape(NBLK * BW, BW), dwx.reshape(NBLK * BW, BW)]
    part_small = list(_small_add_own_half(small, list(_pair_exchange("pair_exchange_small", small)), c))
    red_small = list(_small_sum_slots(part_small, list(_quad_exchange(part_small)), me, c))
    out = list(_pair_gather([red_win] + red_early + red_small))
    big = dict(zip(("w_in", "w_out", "w_gate", "w_up", "w_down"), out[:5]))
    return gx, big, out[5:]


def kernel(x, pre_mix_norm, w_in, lru_conv_w, lru_conv_b, lru_wa, lru_ba, lru_wx, lru_bx, lru_lambda, lru_out_norm, ssd_conv_w, ssd_conv_b, ssd_dt_bias, ssd_a_log, ssd_d, ssd_out_norm, w_out, post_mix_norm, pre_ffn_norm, w_gate, w_up, w_down, post_ffn_norm, loss_target, m_pre_mix_norm, m_w_in, m_lru_conv_w, m_lru_conv_b, m_lru_wa, m_lru_ba, m_lru_wx, m_lru_bx, m_lru_lambda, m_lru_out_norm, m_ssd_conv_w, m_ssd_conv_b, m_ssd_dt_bias, m_ssd_a_log, m_ssd_d, m_ssd_out_norm, m_w_out, m_post_mix_norm, m_pre_ffn_norm, m_w_gate, m_w_up, m_w_down, m_post_ffn_norm, v_pre_mix_norm, v_w_in, v_lru_conv_w, v_lru_conv_b, v_lru_wa, v_lru_ba, v_lru_wx, v_lru_bx, v_lru_lambda, v_lru_out_norm, v_ssd_conv_w, v_ssd_conv_b, v_ssd_dt_bias, v_ssd_a_log, v_ssd_d, v_ssd_out_norm, v_w_out, v_post_mix_norm, v_pre_ffn_norm, v_w_gate, v_w_up, v_w_down, v_post_ffn_norm):
    args = dict(locals())
    names = list(SMALL) + list(BIG)
    w = {n: args[n] for n in names}
    m = {n: args["m_" + n] for n in names}
    v = {n: args["v_" + n] for n in names}
    chip = 2 * lax.axis_index("x") + lax.axis_index("y")

    late = [a[0].astype(WIRE) for a in (w_out, w_gate, w_up, w_down)]
    gx, red, (rows, cst, hst, dwa, dwx) = _step(x[0], loss_target[0], w_in[0], lru_conv_w[0], ssd_conv_w[0],
                                                {n: w[n] for n in SMALL}, late)
    loss = jnp.sum(rows[LOSS_ROW])

    grads, delta, new_m, new_v = {}, {}, {}, {}
    for n in BIG:
        g = red[n]
        if n in ("w_in", "w_gate", "w_up"):
            t = lambda a: jnp.swapaxes(a, 1, 2)
            gt = g.T
            out = _adamw("adamw_" + n, t(w[n]), gt, t(m[n]), t(v[n]))
            delta[n], new_m[n], new_v[n] = (t(o) for o in out)
            grads[n] = t(gt[None])
        else:
            delta[n], new_m[n], new_v[n] = _adamw("adamw_" + n, w[n], g, m[n], v[n])
            grads[n] = g[None]

    lc, sc = lru_conv_w.shape[-1], ssd_conv_w.shape[-1]
    glcw = lax.dynamic_slice_in_dim(rows[LRU_CONV_ROWS[0]:LRU_CONV_ROWS[1]], chip * lc, lc, axis=1)
    gscw = lax.dynamic_slice_in_dim(cst[0:CONV_K], chip * sc, sc, axis=1)
    res = _adamw_small(rows, cst, hst, dwa.reshape(NBLK, BW, BW), dwx.reshape(NBLK, BW, BW), glcw, gscw,
                       {n: w[n] for n in SMALL}, {n: m[n] for n in SMALL}, {n: v[n] for n in SMALL})
    for n in SMALL:
        grads[n], delta[n], new_m[n], new_v[n] = res[n]

    order = ["pre_mix_norm", "w_in", "lru_conv_w", "lru_conv_b", "lru_wa", "lru_ba", "lru_wx", "lru_bx", "lru_lambda",
             "lru_out_norm", "ssd_conv_w", "ssd_conv_b", "ssd_dt_bias", "ssd_a_log", "ssd_d", "ssd_out_norm", "w_out",
             "post_mix_norm", "pre_ffn_norm", "w_gate", "w_up", "w_down", "post_ffn_norm"]
    return (loss, gx[None], *[grads[n] for n in order], *[delta[n] for n in order],
            *[new_m[n] for n in order], *[new_v[n] for n in order])
```

```python
import functools

import jax
import jax.numpy as jnp
from jax import lax
from jax.experimental import pallas as pl
from jax.experimental.pallas import tpu as pltpu

F32 = jnp.float32
BF = jnp.bfloat16

D = 1024
LW = 1024
NBLK = 16
BW = 64
SI = 1024
NH = 16
HD = 64
NG = 2
HPG = NH // NG
NS = 128
CH = 128
XBC = SI + 2 * NG * NS
DTP = 128
PC = 3 * 1024 + XBC + DTP
DFF = 2816
IN_COLS = 4624
EPS = 1e-6
LRU_C = 8.0
CONV_K = 4
TT = 256
TW = 512
VMEM_LIMIT = 56 * 1024 * 1024

ADAM_LR, ADAM_B1, ADAM_B2, ADAM_EPS, ADAM_WD, ADAM_STEP = 0.001, 0.9, 0.999, 1e-08, 0.01, 10

MESH = pl.DeviceIdType.MESH


def _mm(a, b):
    return jnp.dot(a.astype(BF), b.astype(BF), preferred_element_type=F32)


def _mm_nt(a, b):
    return lax.dot_general(a.astype(BF), b.astype(BF), (((1,), (1,)), ((), ())), preferred_element_type=F32)


def _mm_tn(a, b):
    return lax.dot_general(a.astype(BF), b.astype(BF), (((0,), (0,)), ((), ())), preferred_element_type=F32)


def _sigmoid(x):
    return 0.5 * jnp.tanh(0.5 * x) + 0.5


def _softplus(x):
    return jnp.maximum(x, 0.0) + jnp.log1p(jnp.exp(-jnp.abs(x)))


_GELU_C = 0.7978845608028654
_GELU_K = 0.044715


def _gelu(x):
    t = jnp.tanh(_GELU_C * (x + _GELU_K * x * x * x))
    return 0.5 * x * (1.0 + t)


def _gelu_grad(x):
    t = jnp.tanh(_GELU_C * (x + _GELU_K * x * x * x))
    return 0.5 * (1.0 + t) + 0.5 * x * (1.0 - t * t) * _GELU_C * (1.0 + 3.0 * _GELU_K * x * x)


def _rms_fwd(x, g):
    r = lax.rsqrt(jnp.mean(x * x, axis=-1, keepdims=True) + EPS)
    return x * r * g


def _rms_bwd(x, g, dy):
    r = lax.rsqrt(jnp.mean(x * x, axis=-1, keepdims=True) + EPS)
    xh = x * r
    dxh = dy * g
    dg = jnp.sum(dy * xh, axis=0, keepdims=True)
    dx = r * (dxh - xh * jnp.mean(dxh * xh, axis=-1, keepdims=True))
    return dx, dg


def _sum_all(x):
    return jnp.sum(jnp.sum(x, axis=1, keepdims=True), axis=0, keepdims=True)


def _cumsum_rows(x, n):
    row = lax.broadcasted_iota(jnp.int32, x.shape, 0)
    k = 1
    while k < n:
        x = x + jnp.where(row >= k, pltpu.roll(x, k, 0), 0.0)
        k *= 2
    return x


def _rev_cumsum_rows(x, n):
    row = lax.broadcasted_iota(jnp.int32, x.shape, 0)
    k = 1
    while k < n:
        x = x + jnp.where(row < n - k, pltpu.roll(x, n - k, 0), 0.0)
        k *= 2
    return x


def _load_once(pairs, sem):
    @pl.when(pl.program_id(0) == 0)
    def _():
        for k, (src, dst) in enumerate(pairs):
            pltpu.make_async_copy(src, dst, sem.at[k]).start()
        for k, (src, dst) in enumerate(pairs):
            pltpu.make_async_copy(src, dst, sem.at[k]).wait()


def _load_start(pairs, sem):
    @pl.when(pl.program_id(0) == 0)
    def _():
        for k, (src, dst) in enumerate(pairs):
            pltpu.make_async_copy(src, dst, sem.at[k]).start()


def _load_wait(pairs, sem, k):
    @pl.when(pl.program_id(0) == 0)
    def _():
        pltpu.make_async_copy(pairs[k][0], pairs[k][1], sem.at[k]).wait()


def _params(n_axes=1):
    return pltpu.CompilerParams(dimension_semantics=("arbitrary",) * n_axes, vmem_limit_bytes=VMEM_LIMIT)


def _rows(n, width, rev_of=None):
    if rev_of is None:
        return pl.BlockSpec((n, width), lambda i: (i, 0))
    return pl.BlockSpec((n, width), lambda i: (rev_of - 1 - i, 0))


def _whole(shape):
    nd = len(shape)
    return pl.BlockSpec(shape, lambda i: (0,) * nd)


ANY = pl.BlockSpec(memory_space=pl.ANY)
S = jax.ShapeDtypeStruct
WIRE = jnp.bfloat16


def _pos():
    return lax.axis_index("x"), lax.axis_index("y"), lax.axis_index("c")


def _other_chips(x, y):
    return [(1 - x, y), (x, 1 - y), (1 - x, 1 - y)]


def _remote(src, dst, send_sem, recv_sem, to):
    return pltpu.make_async_remote_copy(src_ref=src, dst_ref=dst, send_sem=send_sem, recv_sem=recv_sem,
                                        device_id=to, device_id_type=MESH)


def _gather_phase(phase, ins, outs, send_sems, recv_sems):
    x, y, c = _pos()
    me = 2 * x + y
    chips = _other_chips(x, y)
    for i, (src, dst) in enumerate(zip(ins, outs)):
        hr = src.shape[0] // 2
        my_half = pl.ds(pl.multiple_of(c * hr, 16), hr)
        sib_half = pl.ds(pl.multiple_of((1 - c) * hr, 16), hr)
        for k, (cx, cy) in enumerate(chips):
            s1, r1 = send_sems.at[6 * i + k], recv_sems.at[6 * i + k]
            s2, r2 = send_sems.at[6 * i + 3 + k], recv_sems.at[6 * i + 3 + k]
            first = lambda: _remote(src.at[my_half, :], dst.at[me, my_half, :], s1, r1, (cx, cy, c))
            landed = dst.at[2 * cx + cy, my_half, :]
            passed = lambda: _remote(landed, landed, s2, r2, (x, y, 1 - c))
            if phase == 0:
                first().start()
            elif phase == 1:
                _remote(landed, landed, s1, r1, (cx, cy, c)).wait_recv()
                passed().start()
            else:
                theirs = dst.at[2 * cx + cy, sib_half, :]
                _remote(theirs, theirs, s2, r2, (x, y, 1 - c)).wait_recv()
                first().wait_send()
                passed().wait_send()


def _half(ref, c, hr):
    sl = pl.ds(pl.multiple_of(c * hr, 8), hr)
    return ref.at[:, sl, :] if len(ref.shape) == 3 else ref.at[sl, :]


def _half_shape(b):
    return b.shape[:-2] + (b.shape[-2] // 2, b.shape[-1])


def _pair_phase(phase, ins, outs, send_sems, recv_sems):
    x, y, c = _pos()
    for k, (src, dst) in enumerate(zip(ins, outs)):
        cp = _remote(_half(src, 1 - c, src.shape[-2] // 2), dst, send_sems.at[k], recv_sems.at[k], (x, y, 1 - c))
        if phase == 0:
            cp.start()
        else:
            cp.wait()


def _quad_phase(phase, ins, outs, send_sems, recv_sems):
    x, y, c = _pos()
    me = 2 * x + y
    for i, (src, dst) in enumerate(zip(ins, outs)):
        for k, (cx, cy) in enumerate(_other_chips(x, y)):
            cp = _remote(src.at[2 * cx + cy], dst.at[me], send_sems.at[3 * i + k], recv_sems.at[3 * i + k], (cx, cy, c))
            if phase == 0:
                cp.start()
            else:
                got = dst.at[2 * cx + cy]
                _remote(got, got, send_sems.at[3 * i + k], recv_sems.at[3 * i + k], (cx, cy, c)).wait_recv()
                cp.wait_send()


def _prenorm(x, g0, shards):
    T = x.shape[0]
    tt = 2 * TT
    nt = T // tt
    ng = len(shards)

    def body(*refs):
        x_ref, g_ref = refs[:2]
        sh_in = refs[2:2 + ng]
        h0_ref = refs[2 + ng]
        sh_out = refs[3 + ng:3 + 2 * ng]
        send_sems, recv_sems = refs[3 + 2 * ng:]
        for phase, step in enumerate((0, nt // 2, nt - 1)):
            @pl.when(pl.program_id(0) == step)
            def _():
                _gather_phase(phase, sh_in, sh_out, send_sems, recv_sems)

        h0_ref[...] = _rms_fwd(x_ref[...], g_ref[...]).astype(BF)

    return pl.pallas_call(
        body, name="prenorm", grid=(nt,),
        in_specs=[_rows(tt, D), _whole((1, D))] + [ANY] * ng, out_specs=[_rows(tt, D)] + [ANY] * ng,
        out_shape=[S((T, D), BF)] + [S((4,) + s.shape, s.dtype) for s in shards],
        scratch_shapes=[pltpu.SemaphoreType.DMA((6 * ng,)), pltpu.SemaphoreType.DMA((6 * ng,))],
        compiler_params=_params(),
    )(x, g0, *shards)


def _blockdiag_mm(v, w4_ref):
    return jnp.concatenate([_mm(v[:, 256 * j:256 * (j + 1)], w4_ref[j]) for j in range(4)], axis=1)


def _lru_gates(lx, p_ref, wa_ref, wx_ref):
    r = _sigmoid(_blockdiag_mm(lx, wa_ref) + p_ref[5:6, :])
    i = _sigmoid(_blockdiag_mm(lx, wx_ref) + p_ref[6:7, :])
    sp = _softplus(-p_ref[7:8, :])
    la = -LRU_C * r * sp
    a = jnp.exp(la)
    th = jnp.tanh(la)
    mult = jnp.sqrt(-2.0 * th / (1.0 - th))
    return r, i, sp, a, mult


def _conv_from(xp_ref, p_ref, n):
    acc = p_ref[4:5, :] + p_ref[0:1, :] * xp_ref[pl.ds(8 - CONV_K + 1, n), :]
    for k in range(1, CONV_K):
        acc = acc + p_ref[k:k + 1, :] * xp_ref[pl.ds(8 - CONV_K + 1 + k, n), :]
    return acc


def _conv_bwd(dp_ref, dconv, x, p_ref, st_ref, n):
    if dconv is None:
        dconv = dp_ref[0:n, :]
    else:
        dp_ref[0:n, :] = dconv
    acc = None
    for k in range(CONV_K):
        g = dp_ref[pl.ds(CONV_K - 1 - k, n), :]
        acc = p_ref[k:k + 1, :] * g if acc is None else acc + p_ref[k:k + 1, :] * g
        st_ref[k:k + 1, :] += jnp.sum(g * x, axis=0, keepdims=True)
    st_ref[4:5, :] += jnp.sum(dconv, axis=0, keepdims=True)
    dp_ref[n:n + 8, :] = dp_ref[0:8, :]
    return acc


def _lru_fwd(h0, wcat, p_lru, wa4, wx4, shards):
    T = h0.shape[0]
    NT = T // TT
    ng = len(shards)

    def body(*refs):
        h0_ref, w_hbm, p_ref, wa_ref, wx_ref = refs[:5]
        sh_in = refs[5:5 + ng]
        h_ref, y_ref, lxc_ref, lxr_ref, lg_ref = refs[5 + ng:10 + ng]
        sh_out = refs[10 + ng:10 + 2 * ng]
        xp, a_s, u_s, hc, w_vm, wsem, send_sems, recv_sems = refs[10 + 2 * ng:]
        _load_once([(w_hbm.at[:, 0:2 * LW], w_vm)], wsem)
        for phase, step in enumerate((0, NT // 2, NT - 1)):
            @pl.when(pl.program_id(0) == step)
            def _():
                _gather_phase(phase, sh_in, sh_out, send_sems, recv_sems)

        @pl.when(pl.program_id(0) == 0)
        def _():
            xp[0:8, :] = jnp.zeros((8, LW), F32)
            hc[...] = jnp.zeros_like(hc)

        hv = h0_ref[...]
        lxr = jnp.dot(hv, w_vm[:, 0:LW], preferred_element_type=F32)
        lxr_ref[...] = lxr
        lg_ref[...] = jnp.dot(hv, w_vm[:, LW:2 * LW], preferred_element_type=F32)
        xp[8:8 + TT, :] = lxr
        lx = _conv_from(xp, p_ref, TT)
        lxc_ref[...] = lx
        xp[0:8, :] = xp[TT:TT + 8, :]
        r, i, sp, a, mult = _lru_gates(lx, p_ref, wa_ref, wx_ref)
        a_s[...] = a
        u_s[...] = mult * (i * lx)

        def step(t, h):
            h = a_s[pl.ds(t, 1), :] * h + u_s[pl.ds(t, 1), :]
            h_ref[pl.ds(t, 1), :] = h
            return h

        hc[0:1, :] = lax.fori_loop(0, TT, step, hc[0:1, :], unroll=8)
        gated = h_ref[...] * _gelu(lg_ref[...])
        y_ref[...] = _rms_fwd(gated, p_ref[8:9, :]).astype(BF)

    return pl.pallas_call(
        body, name="lru_fwd", grid=(NT,),
        in_specs=[_rows(TT, D), ANY, _whole((16, LW)), _whole((4, 256, 256)), _whole((4, 256, 256))] + [ANY] * ng,
        out_specs=[_rows(TT, LW), _rows(TT, LW), _rows(TT, LW), _rows(TT, LW), _rows(TT, LW)] + [ANY] * ng,
        out_shape=[S((T, LW), F32), S((T, LW), BF), S((T, LW), F32), S((T, LW), F32), S((T, LW), F32)]
        + [S((4,) + s.shape, s.dtype) for s in shards],
        scratch_shapes=[pltpu.VMEM((TT + 8, LW), F32), pltpu.VMEM((TT, LW), F32), pltpu.VMEM((TT, LW), F32),
                        pltpu.VMEM((8, LW), F32), pltpu.VMEM((D, 2 * LW), BF), pltpu.SemaphoreType.DMA((1,)),
                        pltpu.SemaphoreType.DMA((6 * ng,)), pltpu.SemaphoreType.DMA((6 * ng,))],
        compiler_params=_params(),
    )(h0, wcat, p_lru, wa4, wx4, *shards)


def _ssd_prep(cv, dt_ref, hp_ref):
    sg = _sigmoid(cv)
    xbc = cv * sg
    lane = lax.broadcasted_iota(jnp.int32, (CH, DTP), 1)
    raw = dt_ref[...] + hp_ref[0:1, :]
    dtv = jnp.where(lane < NH, _softplus(raw), 0.0)
    A = jnp.where(lane[0:1, :] < NH, -jnp.exp(hp_ref[1:2, :]), 0.0)
    cs = _cumsum_rows(dtv * A, CH)
    return sg, xbc, raw, dtv, A, cs


def _per_head_lanes(v):
    r = v.shape[0]
    first = lax.broadcasted_iota(jnp.int32, (r, 2 * HD), 1) < HD
    pairs = [jnp.where(first, jnp.broadcast_to(v[:, 2 * j:2 * j + 1], (r, 2 * HD)),
                       jnp.broadcast_to(v[:, 2 * j + 1:2 * j + 2], (r, 2 * HD))) for j in range(NH // 2)]
    return jnp.concatenate(pairs, axis=1)


def _per_head_rows(col, g):
    return jnp.concatenate([jnp.broadcast_to(col[g * HPG + k:g * HPG + k + 1, :], (HD, NS)) for k in range(HPG)], axis=0)


def _ssd_decays(cs):
    csT = cs.T
    cl = cs[CH - 1:CH, :]
    E_x = _per_head_lanes(jnp.exp(cs))
    dsm = jnp.exp(cl - cs)
    ds_x = _per_head_lanes(dsm)
    El_rows = jnp.broadcast_to(jnp.exp(csT[0:NH, CH - 1:CH]), (NH, NS))
    return csT, dsm, E_x, ds_x, El_rows


def _ssd_fwd(h0, wcat, cw_ssd, hp_ssd, g_ssd, shards):
    T = h0.shape[0]
    NC = T // CH
    ng = len(shards)
    c0 = 2 * LW

    def body(*refs):
        h0_ref, w_hbm, cw_ref, hp_ref, g_ref = refs[:5]
        sh_in = refs[5:5 + ng]
        y_ref, yn_ref, st_ref, cv_ref, z_ref, xr_ref, dt_ref = refs[5 + ng:12 + ng]
        sh_out = refs[12 + ng:12 + 2 * ng]
        xp, st, w_vm, wsem, send_sems, recv_sems = refs[12 + 2 * ng:]
        _load_once([(w_hbm.at[:, c0:PC], w_vm)], wsem)
        for phase, step in enumerate((0, NC // 2, NC - 1)):
            @pl.when(pl.program_id(0) == step)
            def _():
                _gather_phase(phase, sh_in, sh_out, send_sems, recv_sems)

        @pl.when(pl.program_id(0) == 0)
        def _():
            xp[0:8, :] = jnp.zeros((8, XBC), F32)
            st[...] = jnp.zeros_like(st)

        hv = h0_ref[...]
        z_ref[...] = jnp.dot(hv, w_vm[:, 0:SI], preferred_element_type=F32)
        xraw = jnp.dot(hv, w_vm[:, SI:SI + XBC], preferred_element_type=F32)
        xr_ref[...] = xraw
        dt_ref[...] = jnp.dot(hv, w_vm[:, SI + XBC:SI + XBC + DTP], preferred_element_type=F32)
        xp[8:8 + CH, :] = xraw
        cv = _conv_from(xp, cw_ref, CH)
        cv_ref[...] = cv
        sg, xbc, raw, dtv, A, cs = _ssd_prep(cv, dt_ref, hp_ref)
        xp[0:8, :] = xp[CH:CH + 8, :]
        st_ref[0] = st[...]
        csT, dsm, E_x, ds_x, El_rows = _ssd_decays(cs)
        X = xbc[:, 0:SI]
        xs = X * _per_head_lanes(dtv)
        xsd = (xs * ds_x).astype(BF)
        DX = _per_head_lanes(hp_ref[...])[2:3, :] * X
        tril = lax.broadcasted_iota(jnp.int32, (CH, CH), 0) >= lax.broadcasted_iota(jnp.int32, (CH, CH), 1)
        first = lax.broadcasted_iota(jnp.int32, (CH, 2 * HD), 1) < HD
        GW = HPG * HD
        for g in range(NG):
            Bg = xbc[:, SI + NS * g:SI + NS * (g + 1)].astype(BF)
            Cg = xbc[:, SI + NG * NS + NS * g:SI + NG * NS + NS * (g + 1)].astype(BF)
            G = _mm_nt(Cg, Bg)
            Sg = st[GW * g:GW * (g + 1), :]
            Yo = _mm_nt(Cg, Sg) * E_x[:, GW * g:GW * (g + 1)]
            st[GW * g:GW * (g + 1), :] = _per_head_rows(El_rows, g) * Sg + _mm_tn(xsd[:, GW * g:GW * (g + 1)], Bg)
            for jj in range(HPG // 2):
                j = g * (HPG // 2) + jj
                ps = slice(2 * HD * j, 2 * HD * (j + 1))
                xs_pair = xs[:, ps]
                acc = Yo[:, 2 * HD * jj:2 * HD * (jj + 1)] + DX[:, ps]
                for e in range(2):
                    h = 2 * j + e
                    Lm = jnp.exp(jnp.where(tril, cs[:, h:h + 1] - csT[h:h + 1, :], -1e30))
                    acc = acc + _mm(G * Lm, jnp.where(first if e == 0 else ~first, xs_pair, 0.0))
                y_ref[:, ps] = acc
        zz = z_ref[...]
        gated = y_ref[...] * (zz * _sigmoid(zz))
        yn_ref[...] = _rms_fwd(gated, g_ref[...]).astype(BF)

    return pl.pallas_call(
        body, name="ssd_fwd", grid=(NC,),
        in_specs=[_rows(CH, D), ANY, _whole((8, XBC)), _whole((8, DTP)), _whole((1, SI))] + [ANY] * ng,
        out_specs=[_rows(CH, SI), _rows(CH, SI), pl.BlockSpec((1, NH * HD, NS), lambda i: (i, 0, 0)), _rows(CH, XBC),
                   _rows(CH, SI), _rows(CH, XBC), _rows(CH, DTP)] + [ANY] * ng,
        out_shape=[S((T, SI), F32), S((T, SI), BF), S((NC, NH * HD, NS), F32), S((T, XBC), F32),
                   S((T, SI), F32), S((T, XBC), F32), S((T, DTP), F32)] + [S((4,) + s.shape, s.dtype) for s in shards],
        scratch_shapes=[pltpu.VMEM((CH + 8, XBC), F32), pltpu.VMEM((NH * HD, NS), F32),
                        pltpu.VMEM((D, PC - c0), BF), pltpu.SemaphoreType.DMA((1,)),
                        pltpu.SemaphoreType.DMA((6 * ng,)), pltpu.SemaphoreType.DMA((6 * ng,))],
        compiler_params=_params(),
    )(h0, wcat, cw_ssd, hp_ssd, g_ssd, *shards)


def _outproj(ylru, yssd, x, wout, g_pm, g_pf):
    T = x.shape[0]

    def body(yl_ref, ys_ref, x_ref, w_hbm, gpm_ref, gpf_ref, mix_ref, x1_ref, h2_ref, w_vm, sem):
        _load_once([(w_hbm, w_vm)], sem)
        mix = (jnp.dot(yl_ref[...], w_vm[0:LW, :], preferred_element_type=F32)
               + jnp.dot(ys_ref[...], w_vm[LW:LW + SI, :], preferred_element_type=F32))
        mix_ref[...] = mix
        x1 = x_ref[...] + _rms_fwd(mix, gpm_ref[...])
        x1_ref[...] = x1
        h2_ref[...] = _rms_fwd(x1, gpf_ref[...]).astype(BF)

    return pl.pallas_call(
        body, name="outproj", grid=(T // TW,),
        in_specs=[_rows(TW, LW), _rows(TW, SI), _rows(TW, D), ANY, _whole((1, D)), _whole((1, D))],
        out_specs=[_rows(TW, D), _rows(TW, D), _rows(TW, D)],
        out_shape=[S((T, D), F32), S((T, D), F32), S((T, D), BF)],
        scratch_shapes=[pltpu.VMEM((LW + SI, D), BF), pltpu.SemaphoreType.DMA((1,))],
        compiler_params=_params(),
    )(ylru, yssd, x, wout, g_pm, g_pf)


def _ffn_fwd(h2, x1, tgt, wg, wu, wd, g_pff):
    T = x1.shape[0]

    def body(h2_ref, x1_ref, t_ref, wg_hbm, wu_hbm, wd_hbm, g_ref,
             gate_ref, up_ref, act_ref, df_ref, dx2_ref, st_ref, wg_vm, wu_vm, wd_vm, sem):
        weights = [(wg_hbm, wg_vm), (wu_hbm, wu_vm), (wd_hbm, wd_vm)]
        _load_start(weights, sem)

        @pl.when(pl.program_id(0) == 0)
        def _():
            st_ref[...] = jnp.zeros_like(st_ref)

        h2 = h2_ref[...]
        _load_wait(weights, sem, 0)
        gate = jnp.dot(h2, wg_vm[...], preferred_element_type=F32)
        _load_wait(weights, sem, 1)
        up = jnp.dot(h2, wu_vm[...], preferred_element_type=F32)
        gate_ref[...] = gate
        up_ref[...] = up
        act = (gate * _sigmoid(gate) * up).astype(BF)
        act_ref[...] = act
        _load_wait(weights, sem, 2)
        f = jnp.dot(act, wd_vm[...], preferred_element_type=F32)
        g = g_ref[...]
        x2 = x1_ref[...] + _rms_fwd(f, g)
        err = x2 - t_ref[...]
        st_ref[0:1, :] += 0.5 * jnp.sum(err * err, axis=0, keepdims=True) * (1.0 / D)
        dx2 = err * (1.0 / D)
        dx2_ref[...] = dx2
        df, dg = _rms_bwd(f, g, dx2)
        df_ref[...] = df.astype(BF)
        st_ref[1:2, :] += dg

    return pl.pallas_call(
        body, name="ffn_fwd", grid=(T // TT,),
        in_specs=[_rows(TT, D), _rows(TT, D), _rows(TT, D), ANY, ANY, ANY, _whole((1, D))],
        out_specs=[_rows(TT, DFF), _rows(TT, DFF), _rows(TT, DFF), _rows(TT, D), _rows(TT, D), _whole((8, D))],
        out_shape=[S((T, DFF), F32), S((T, DFF), F32), S((T, DFF), BF), S((T, D), BF), S((T, D), F32), S((8, D), F32)],
        scratch_shapes=[pltpu.VMEM((D, DFF), BF), pltpu.VMEM((D, DFF), BF), pltpu.VMEM((DFF, D), BF),
                        pltpu.SemaphoreType.DMA((3,))],
        compiler_params=_params(),
    )(h2, x1, tgt, wg, wu, wd, g_pff)


def _ffn_bwd(df, gate, up, wdT, wgT, wuT):
    T = df.shape[0]

    def body(df_ref, gate_ref, up_ref, wd_hbm, wg_hbm, wu_hbm, dgate_ref, dup_ref, dh2_ref, wd_vm, wg_vm, wu_vm, sem):
        weights = [(wd_hbm, wd_vm), (wg_hbm, wg_vm), (wu_hbm, wu_vm)]
        _load_start(weights, sem)
        _load_wait(weights, sem, 0)
        dact = jnp.dot(df_ref[...], wd_vm[...], preferred_element_type=F32)
        gate = gate_ref[...]
        s = _sigmoid(gate)
        dup = (dact * (gate * s)).astype(BF)
        dgate = (dact * up_ref[...] * (s + gate * s * (1.0 - s))).astype(BF)
        dup_ref[...] = dup
        dgate_ref[...] = dgate
        _load_wait(weights, sem, 1)
        dh2 = jnp.dot(dgate, wg_vm[...], preferred_element_type=F32)
        _load_wait(weights, sem, 2)
        dh2_ref[...] = dh2 + jnp.dot(dup, wu_vm[...], preferred_element_type=F32)

    return pl.pallas_call(
        body, name="ffn_bwd", grid=(T // TT,),
        in_specs=[_rows(TT, D), _rows(TT, DFF), _rows(TT, DFF), ANY, ANY, ANY],
        out_specs=[_rows(TT, DFF), _rows(TT, DFF), _rows(TT, D)],
        out_shape=[S((T, DFF), BF), S((T, DFF), BF), S((T, D), F32)],
        scratch_shapes=[pltpu.VMEM((D, DFF), BF), pltpu.VMEM((DFF, D), BF), pltpu.VMEM((DFF, D), BF),
                        pltpu.SemaphoreType.DMA((3,))],
        compiler_params=_params(),
    )(df, gate, up, wdT, wgT, wuT)


def _mix_bwd(dh2, x1, dx2, mix, woutT, g_pf, g_pm):
    T = x1.shape[0]

    def body(dh2_ref, x1_ref, dx2_ref, mix_ref, w_hbm, gpf_ref, gpm_ref,
             dx1_ref, dmix_ref, dyl_ref, dys_ref, st_ref, w_vm, sem):
        _load_once([(w_hbm, w_vm)], sem)

        @pl.when(pl.program_id(0) == 0)
        def _():
            st_ref[...] = jnp.zeros_like(st_ref)

        dxa, dgpf = _rms_bwd(x1_ref[...], gpf_ref[...], dh2_ref[...])
        dx1 = dx2_ref[...] + dxa
        dx1_ref[...] = dx1
        dmix, dgpm = _rms_bwd(mix_ref[...], gpm_ref[...], dx1)
        dmix = dmix.astype(BF)
        dmix_ref[...] = dmix
        st_ref[0:1, :] += dgpf
        st_ref[1:2, :] += dgpm
        dyl_ref[...] = jnp.dot(dmix, w_vm[:, 0:LW], preferred_element_type=F32)
        dys_ref[...] = jnp.dot(dmix, w_vm[:, LW:LW + SI], preferred_element_type=F32)

    return pl.pallas_call(
        body, name="mix_bwd", grid=(T // TW,),
        in_specs=[_rows(TW, D), _rows(TW, D), _rows(TW, D), _rows(TW, D), ANY, _whole((1, D)), _whole((1, D))],
        out_specs=[_rows(TW, D), _rows(TW, D), _rows(TW, LW), _rows(TW, SI), _whole((8, D))],
        out_shape=[S((T, D), F32), S((T, D), BF), S((T, LW), F32), S((T, SI), F32), S((8, D), F32)],
        scratch_shapes=[pltpu.VMEM((D, LW + SI), BF), pltpu.SemaphoreType.DMA((1,))],
        compiler_params=_params(),
    )(dh2, x1, dx2, mix, woutT, g_pf, g_pm)


def _halo(width, n_tiles, tile):
    per = tile // 8
    return pl.BlockSpec((8, width), lambda i: (jnp.maximum((n_tiles - 1 - i) * per - 1, 0), 0))


def _lru_bwd(dy, lxr, lxc, lg, h, p_lru, wa4, wx4, wa4T, wx4T, bufs):
    T = dy.shape[0]
    NT = T // TT
    nb = len(bufs)

    def body(*refs):
        dy_ref, lxr_ref, lxc_ref, lg_ref, h_ref, hh_ref, p_ref, wa_ref, wx_ref, waT_ref, wxT_ref = refs[:11]
        b_in = refs[11:11 + nb]
        dlx_ref, dlg_ref, st_ref, dwa_ref, dwx_ref = refs[11 + nb:16 + nb]
        b_out = refs[16 + nb:16 + 2 * nb]
        hp, dp, a_s, d_s, g_s, cc, send_sems, recv_sems = refs[16 + 2 * nb:]
        for phase, step in enumerate((0, NT - 1)):
            @pl.when(pl.program_id(0) == step)
            def _():
                _pair_phase(phase, b_in, b_out, send_sems, recv_sems)

        dy = dy_ref[...]
        first = pl.program_id(0) == 0
        top = pl.program_id(0) == NT - 1

        @pl.when(first)
        def _():
            st_ref[...] = jnp.zeros_like(st_ref)
            dwa_ref[...] = jnp.zeros_like(dwa_ref)
            dwx_ref[...] = jnp.zeros_like(dwx_ref)
            dp[TT:TT + 8, :] = jnp.zeros((8, LW), F32)
            cc[...] = jnp.zeros_like(cc)

        hp[0:8, :] = hh_ref[...] * jnp.where(top, 0.0, 1.0)
        hp[8:8 + TT, :] = h_ref[...]
        lx = lxc_ref[...]
        r, i, sp, a, mult = _lru_gates(lx, p_ref, wa_ref, wx_ref)

        lg = lg_ref[...]
        hcur = h_ref[...]
        ge = _gelu(lg)
        dgated, dgn = _rms_bwd(hcur * ge, p_ref[8:9, :], dy)
        st_ref[8:9, :] += dgn
        dlg_ref[...] = (dgated * hcur * _gelu_grad(lg)).astype(BF)
        a_s[...] = a
        d_s[...] = dgated * ge

        def step(k, c):
            t = TT - 1 - k
            g = d_s[pl.ds(t, 1), :] + c
            g_s[pl.ds(t, 1), :] = g
            return a_s[pl.ds(t, 1), :] * g

        cc[0:1, :] = lax.fori_loop(0, TT, step, cc[0:1, :], unroll=8)
        gt = g_s[...]
        da = gt * hp[pl.ds(7, TT), :]
        dmult = gt * i * lx
        di = gt * mult * lx
        dlxc = gt * mult * i
        dla = da * a - dmult * (a * a) / mult
        dr = dla * (-LRU_C * sp)
        st_ref[7:8, :] += jnp.sum(dla * (-LRU_C * r), axis=0, keepdims=True) * (-_sigmoid(-p_ref[7:8, :]))
        dzr = dr * r * (1.0 - r)
        dzi = di * i * (1.0 - i)
        st_ref[5:6, :] += jnp.sum(dzr, axis=0, keepdims=True)
        st_ref[6:7, :] += jnp.sum(dzi, axis=0, keepdims=True)
        dlxc = dlxc + _blockdiag_mm(dzr, waT_ref) + _blockdiag_mm(dzi, wxT_ref)
        for j in range(4):
            sl = slice(256 * j, 256 * (j + 1))
            pa = _mm_tn(lx[:, sl], dzr[:, sl])
            px = _mm_tn(lx[:, sl], dzi[:, sl])
            for b in range(4):
                bs = slice(BW * b, BW * (b + 1))
                dwa_ref[4 * j + b] += pa[bs, bs]
                dwx_ref[4 * j + b] += px[bs, bs]
        dlx_ref[...] = _conv_bwd(dp, dlxc, lxr_ref[...], p_ref, st_ref, TT).astype(BF)

    w4 = _whole((4, 256, 256))
    return pl.pallas_call(
        body, name="lru_bwd", grid=(NT,),
        in_specs=[_rows(TT, LW, NT), _rows(TT, LW, NT), _rows(TT, LW, NT), _rows(TT, LW, NT), _rows(TT, LW, NT),
                  _halo(LW, NT, TT), _whole((16, LW)), w4, w4, w4, w4] + [ANY] * nb,
        out_specs=[_rows(TT, LW, NT), _rows(TT, LW, NT), _whole((16, LW)), _whole((NBLK, BW, BW)), _whole((NBLK, BW, BW))]
        + [ANY] * nb,
        out_shape=[S((T, LW), BF), S((T, LW), BF), S((16, LW), F32), S((NBLK, BW, BW), F32), S((NBLK, BW, BW), F32)]
        + [S(_half_shape(b), b.dtype) for b in bufs],
        scratch_shapes=[pltpu.VMEM((TT + 8, LW), F32), pltpu.VMEM((TT + 8, LW), F32),
                        pltpu.VMEM((TT, LW), F32), pltpu.VMEM((TT, LW), F32), pltpu.VMEM((TT, LW), F32),
                        pltpu.VMEM((8, LW), F32), pltpu.SemaphoreType.DMA((nb,)), pltpu.SemaphoreType.DMA((nb,))],
        compiler_params=_params(),
    )(dy, lxr, lxc, lg, h, h, p_lru, wa4, wx4, wa4T, wx4T, *bufs)


def _ssd_bwd(dyn, xbcr, cv, z, dtr, y, states, cw_ssd, hp_ssd, g_ssd, parts):
    T = dyn.shape[0]
    NC = T // CH
    nq = len(parts)

    def body(*refs):
        dyn_ref, xr_ref, cv_ref, z_ref, dt_ref, y_ref, st_ref, cw_ref, hp_ref, g_ref = refs[:10]
        q_in = refs[10:10 + nq]
        dxbc_ref, dz_ref, ddt_ref, cst_ref, hst_ref, gst_ref = refs[10 + nq:16 + nq]
        q_out = refs[16 + nq:16 + 2 * nq]
        dp, dS, send_sems, recv_sems = refs[16 + 2 * nq:]
        dyn = dyn_ref[...]
        first = pl.program_id(0) == 0
        for phase, step in enumerate((0, NC - 1)):
            @pl.when(pl.program_id(0) == step)
            def _():
                _quad_phase(phase, q_in, q_out, send_sems, recv_sems)

        @pl.when(first)
        def _():
            cst_ref[...] = jnp.zeros_like(cst_ref)
            hst_ref[...] = jnp.zeros_like(hst_ref)
            gst_ref[...] = jnp.zeros_like(gst_ref)
            dp[CH:CH + 8, :] = jnp.zeros((8, XBC), F32)
            dS[...] = jnp.zeros_like(dS)

        cv = cv_ref[...]
        sg, xbc, raw, dtv, A, cs = _ssd_prep(cv, dt_ref, hp_ref)
        csT, dsm, E_x, ds_x, El_rows = _ssd_decays(cs)
        row_i = lax.broadcasted_iota(jnp.int32, (CH, CH), 0)
        col_i = lax.broadcasted_iota(jnp.int32, (CH, CH), 1)
        tril = row_i >= col_i
        first = col_i < HD
        head_of = ((lax.broadcasted_iota(jnp.int32, (DTP, SI), 1) >> 6)
                   == lax.broadcasted_iota(jnp.int32, (DTP, SI), 0)).astype(BF)
        head_ofT = ((lax.broadcasted_iota(jnp.int32, (SI, DTP), 0) >> 6)
                    == lax.broadcasted_iota(jnp.int32, (SI, DTP), 1)).astype(BF)

        def hi_lo(v):
            hi = v.astype(BF)
            return hi, (v - hi.astype(F32)).astype(BF)

        GW = HPG * HD

        def lane_sums(v, g):
            hi, lo = hi_lo(v)
            w = head_ofT[GW * g:GW * (g + 1), :]
            return _mm(hi, w) + _mm(lo, w)

        zz = z_ref[...]
        sz = _sigmoid(zz)
        yv = y_ref[...]
        dgn, dg = _rms_bwd(yv * (zz * sz), g_ref[...], dyn)
        gst_ref[0:1, :] += dg
        dz_ref[...] = (dgn * yv * (sz + zz * sz * (1.0 - sz))).astype(BF)
        dY = dgn * (zz * sz)

        X = xbc[:, 0:SI]
        dsilu = sg + cv * sg * (1.0 - sg)
        dt_x = _per_head_lanes(dtv)
        xs = X * dt_x
        xsd = (xs * ds_x).astype(BF)
        D_x = _per_head_lanes(hp_ref[...])[2:3, :]
        zero = jnp.zeros((CH, DTP), F32)
        dcs_col = zero
        dcs_row = zero
        dds, ddt_col, dD_rows, dcl_rows = zero, zero, zero, zero
        for g in range(NG):
            gs = slice(GW * g, GW * (g + 1))
            Bg = xbc[:, SI + NS * g:SI + NS * (g + 1)].astype(BF)
            Cg = xbc[:, SI + NG * NS + NS * g:SI + NG * NS + NS * (g + 1)].astype(BF)
            G = _mm_nt(Cg, Bg)
            Sg = st_ref[0, gs, :]
            dSe = dS[gs, :]
            dYg = dY[:, gs]
            dcs_col = dcs_col + lane_sums(dYg * (_mm_nt(Cg, Sg) * E_x[:, gs]), g)
            dD_rows = dD_rows + lane_sums(dYg * X[:, gs], g)
            dP = dYg * E_x[:, gs]
            dCg = _mm(dP, Sg)
            dS[gs, :] = _mm_tn(dP, Cg) + _per_head_rows(El_rows, g) * dSe
            t_hi, t_lo = hi_lo(dSe * Sg)
            dcl_rows = dcl_rows + _mm(head_of[:, gs], t_hi) + _mm(head_of[:, gs], t_lo)
            Q = _mm_nt(Bg, dSe)
            dds = dds + lane_sums(Q * xs[:, gs], g)
            dBg = _mm(xsd[:, gs], dSe)
            dG = jnp.zeros((CH, CH), F32)
            dxs_pairs = []
            for jj in range(HPG // 2):
                j = g * (HPG // 2) + jj
                ps = slice(2 * HD * j, 2 * HD * (j + 1))
                xs_pair = xs[:, ps]
                dxs_pair = Q[:, 2 * HD * jj:2 * HD * (jj + 1)] * ds_x[:, ps]
                for e in range(2):
                    h = 2 * j + e
                    Lm = jnp.exp(jnp.where(tril, cs[:, h:h + 1] - csT[h:h + 1, :], -1e30))
                    M = G * Lm
                    dYm = jnp.where(first if e == 0 else ~first, dY[:, ps], 0.0).astype(BF)
                    dM = _mm_nt(dYm, xs_pair)
                    dxs_pair = dxs_pair + _mm_tn(M, dYm)
                    Wm = dM * M
                    dcs_col = dcs_col + jnp.where(col_i == h, jnp.sum(Wm, axis=1, keepdims=True), 0.0)
                    dcs_row = dcs_row + jnp.where(row_i == h, -jnp.sum(Wm, axis=0, keepdims=True), 0.0)
                    dG = dG + dM * Lm
                dp[0:CH, ps] = (D_x[:, ps] * dY[:, ps] + dxs_pair * dt_x[:, ps]) * dsilu[:, ps]
                dxs_pairs.append(dxs_pair)
            ddt_col = ddt_col + lane_sums(jnp.concatenate(dxs_pairs, axis=1) * X[:, gs], g)
            bs = slice(SI + NS * g, SI + NS * (g + 1))
            cs_ = slice(SI + NG * NS + NS * g, SI + NG * NS + NS * (g + 1))
            dp[0:CH, bs] = (dBg + _mm_tn(dG, Cg)) * dsilu[:, bs]
            dp[0:CH, cs_] = (dCg + _mm(dG, Bg)) * dsilu[:, cs_]

        dds = dds * dsm
        dcs_col = dcs_col - dds
        dD = jnp.sum(dD_rows, axis=0, keepdims=True)
        dcl_rows = jnp.sum(dcl_rows, axis=1, keepdims=True) * jnp.exp(csT[:, CH - 1:CH])
        dcs_row = dcs_row + jnp.where(col_i == CH - 1, dcl_rows, 0.0)
        dcs_col = dcs_col + jnp.where(row_i == CH - 1, jnp.sum(dds, axis=0, keepdims=True), 0.0)

        da = _rev_cumsum_rows(dcs_col + dcs_row.T, CH)
        ddt_col = ddt_col + da * A
        hst_ref[1:2, :] += jnp.sum(da * dtv, axis=0, keepdims=True) * A
        hst_ref[2:3, :] += dD
        draw = jnp.where(col_i < NH, ddt_col * _sigmoid(raw), 0.0)
        ddt_ref[...] = draw.astype(BF)
        hst_ref[0:1, :] += jnp.sum(draw, axis=0, keepdims=True)

        dxbc_ref[...] = _conv_bwd(dp, None, xr_ref[...], cw_ref, cst_ref, CH).astype(BF)

    return pl.pallas_call(
        body, name="ssd_bwd", grid=(NC,),
        in_specs=[_rows(CH, SI, NC), _rows(CH, XBC, NC), _rows(CH, XBC, NC), _rows(CH, SI, NC), _rows(CH, DTP, NC),
                  _rows(CH, SI, NC), pl.BlockSpec((1, NH * HD, NS), lambda i: (NC - 1 - i, 0, 0)),
                  _whole((8, XBC)), _whole((8, DTP)), _whole((1, SI))] + [ANY] * nq,
        out_specs=[_rows(CH, XBC, NC), _rows(CH, SI, NC), _rows(CH, DTP, NC), _whole((16, XBC)), _whole((16, DTP)),
                   _whole((8, SI))] + [ANY] * nq,
        out_shape=[S((T, XBC), BF), S((T, SI), BF), S((T, DTP), BF), S((16, XBC), F32), S((16, DTP), F32), S((8, SI), F32)]
        + [S(p.shape, p.dtype) for p in parts],
        scratch_shapes=[pltpu.VMEM((CH + 8, XBC), F32), pltpu.VMEM((NH * HD, NS), F32),
                        pltpu.SemaphoreType.DMA((3 * nq,)), pltpu.SemaphoreType.DMA((3 * nq,))],
        compiler_params=_params(),
    )(dyn, xbcr, cv, z, dtr, y, states, cw_ssd, hp_ssd, g_ssd, *parts)


def _inproj_bwd(dlx, dlg, dz, dxbc, ddt, x, dx1, wcatT, g0, parts):
    T = x.shape[0]
    NT = T // TW
    nq = len(parts)

    def body(*refs):
        dlx_ref, dlg_ref, dz_ref, dxbc_ref, ddt_ref, x_ref, dx1_ref, w_hbm, g_ref = refs[:9]
        q_in = refs[9:9 + nq]
        dx_ref, st_ref = refs[9 + nq:11 + nq]
        q_out = refs[11 + nq:11 + 2 * nq]
        w_vm, sem, send_sems, recv_sems = refs[11 + 2 * nq:]
        weights = [(w_hbm.at[0:2 * LW, :], w_vm.at[0:2 * LW, :]), (w_hbm.at[2 * LW:PC, :], w_vm.at[2 * LW:PC, :])]
        _load_start(weights, sem)
        for phase, step in enumerate((0, NT - 1)):
            @pl.when(pl.program_id(0) == step)
            def _():
                _quad_phase(phase, q_in, q_out, send_sems, recv_sems)

        @pl.when(pl.program_id(0) == 0)
        def _():
            st_ref[...] = jnp.zeros_like(st_ref)

        _load_wait(weights, sem, 0)
        dh = jnp.dot(dlx_ref[...], w_vm[0:1024, :], preferred_element_type=F32)
        dh = dh + jnp.dot(dlg_ref[...], w_vm[1024:2048, :], preferred_element_type=F32)
        _load_wait(weights, sem, 1)
        dh = dh + jnp.dot(dz_ref[...], w_vm[2048:3072, :], preferred_element_type=F32)
        dh = dh + jnp.dot(dxbc_ref[...], w_vm[3072:3072 + XBC, :], preferred_element_type=F32)
        dh = dh + jnp.dot(ddt_ref[...], w_vm[3072 + XBC:PC, :], preferred_element_type=F32)
        dx, dg = _rms_bwd(x_ref[...], g_ref[...], dh)
        dx_ref[...] = dx1_ref[...] + dx
        st_ref[0:1, :] += dg

    return pl.pallas_call(
        body, name="inproj_bwd", grid=(NT,),
        in_specs=[_rows(TW, 1024), _rows(TW, 1024), _rows(TW, 1024), _rows(TW, XBC), _rows(TW, DTP), _rows(TW, D),
                  _rows(TW, D), ANY, _whole((1, D))] + [ANY] * nq,
        out_specs=[_rows(TW, D), _whole((8, D))] + [ANY] * nq,
        out_shape=[S((T, D), F32), S((8, D), F32)] + [S(p.shape, p.dtype) for p in parts],
        scratch_shapes=[pltpu.VMEM((PC, D), BF), pltpu.SemaphoreType.DMA((2,)),
                        pltpu.SemaphoreType.DMA((3 * nq,)), pltpu.SemaphoreType.DMA((3 * nq,))],
        compiler_params=_params(),
    )(dlx, dlg, dz, dxbc, ddt, x, dx1, wcatT, g0, *parts)


def _wgrad(name, a, b):
    T, M = a.shape
    N = b.shape[1]
    tk = min(T, 2048 if M <= 1024 else 1024)
    tn = N
    while M * tn * 4 > (6 << 20) and tn % 256 == 0:
        tn //= 2

    def body(a_ref, b_ref, o_ref):
        p = lax.dot_general(a_ref[...], b_ref[...], (((0,), (0,)), ((), ())), preferred_element_type=F32)

        @pl.when(pl.program_id(1) == 0)
        def _():
            o_ref[...] = p

        @pl.when(pl.program_id(1) > 0)
        def _():
            o_ref[...] += p

    return pl.pallas_call(
        body, name=name, grid=(N // tn, T // tk),
        in_specs=[pl.BlockSpec((tk, M), lambda j, k: (k, 0)), pl.BlockSpec((tk, tn), lambda j, k: (k, j))],
        out_specs=pl.BlockSpec((M, tn), lambda j, k: (0, j)), out_shape=S((M, N), F32),
        compiler_params=_params(2),
    )(a, b)


def _adamw(name, w, g, m, v):
    _, R, C = w.shape

    def body(w_ref, g_ref, m_ref, v_ref, d_ref, nm_ref, nv_ref):
        d_ref[0], nm_ref[0], nv_ref[0] = _adam_math(w_ref[0], g_ref[...], m_ref[0], v_ref[0])

    if R % 8 == 0:
        tr = _row_tile(R, C)
        n_tiles = R // tr
        blk, gblk = pl.BlockSpec((1, tr, C), lambda i: (0, i, 0)), pl.BlockSpec((tr, C), lambda i: (i, 0))
    else:
        tc = 128 * max(k for k in range(1, C // 128 + 1) if C % (128 * k) == 0 and R * 128 * k * 4 <= (5 << 19))
        n_tiles = C // tc
        blk, gblk = pl.BlockSpec((1, R, tc), lambda i: (0, 0, i)), pl.BlockSpec((R, tc), lambda i: (0, i))
    return pl.pallas_call(
        body, name=name, grid=(n_tiles,),
        in_specs=[blk, gblk, blk, blk], out_specs=[blk] * 3,
        out_shape=[S((1, R, C), F32)] * 3, compiler_params=_params(),
    )(w, g, m, v)


def _pair_exchange(name, bufs):
    n = len(bufs)

    def body(*refs):
        for phase in range(2):
            _pair_phase(phase, refs[:n], refs[n:2 * n], refs[2 * n], refs[2 * n + 1])

    return pl.pallas_call(
        body, name=name, in_specs=[ANY] * n, out_specs=[ANY] * n,
        out_shape=[S(_half_shape(b), b.dtype) for b in bufs],
        scratch_shapes=[pltpu.SemaphoreType.DMA((n,)), pltpu.SemaphoreType.DMA((n,))],
    )(*bufs)


def _quad_exchange(bufs):
    n = len(bufs)

    def body(*refs):
        ins, outs = refs[:n], refs[n:2 * n]
        send_sems, recv_sems = refs[2 * n], refs[2 * n + 1]
        x, y, c = _pos()
        me = 2 * x + y
        chips = _other_chips(x, y)
        copies = []
        for k, (src, dst) in enumerate(zip(ins, outs)):
            for j, (cx, cy) in enumerate(chips):
                cp = _remote(src, dst.at[me], send_sems.at[3 * k + j], recv_sems.at[3 * k + j], (cx, cy, c))
                cp.start()
                copies.append(cp)
        for k, (src, dst) in enumerate(zip(ins, outs)):
            for j, (cx, cy) in enumerate(chips):
                blk = dst.at[2 * cx + cy]
                _remote(blk, blk, send_sems.at[3 * k + j], recv_sems.at[3 * k + j], (cx, cy, c)).wait_recv()
        for cp in copies:
            cp.wait_send()

    return pl.pallas_call(
        body, name="quad_exchange", in_specs=[ANY] * n, out_specs=[ANY] * n,
        out_shape=[S((4,) + b.shape, b.dtype) for b in bufs],
        scratch_shapes=[pltpu.SemaphoreType.DMA((3 * n,)), pltpu.SemaphoreType.DMA((3 * n,))],
    )(*bufs)


def _pair_gather(bufs):
    n = len(bufs)

    def body(*refs):
        ins, outs = refs[:n], refs[n:2 * n]
        send_sems, recv_sems = refs[2 * n], refs[2 * n + 1]
        x, y, c = _pos()
        copies = []
        for k, buf in enumerate(outs):
            mine = _half(buf, c, buf.shape[0] // 2)
            cp = _remote(mine, mine, send_sems.at[k], recv_sems.at[k], (x, y, 1 - c))
            cp.start()
            copies.append(cp)
        for k, buf in enumerate(outs):
            theirs = _half(buf, 1 - c, buf.shape[0] // 2)
            _remote(theirs, theirs, send_sems.at[k], recv_sems.at[k], (x, y, 1 - c)).wait_recv()
        for cp in copies:
            cp.wait_send()

    return pl.pallas_call(
        body, name="pair_gather", in_specs=[ANY] * n, out_specs=[ANY] * n,
        out_shape=[S(b.shape, b.dtype) for b in bufs], input_output_aliases={k: k for k in range(n)},
        scratch_shapes=[pltpu.SemaphoreType.DMA((n,)), pltpu.SemaphoreType.DMA((n,))],
    )(*bufs)


def _row_tile(rows, cols, mult=8):
    best = mult
    for t in range(mult, rows + 1, mult):
        if rows % t == 0 and t * cols * 4 <= (1 << 21):
            best = t
    return best


def _add_own_half(name, full, got, c, out_dtype, by_columns):
    hr = got.shape[-2]
    wide = got.shape[-1]
    cols = wide // 4 if by_columns else wide
    tr = _row_tile(hr, wide, 16)
    per = hr // tr

    if by_columns:
        def body(c_ref, a_ref, b_ref, o_ref):
            v = a_ref[...] + b_ref[...]
            for j in range(4):
                o_ref[j] = v[:, j * cols:(j + 1) * cols].astype(out_dtype)

        in_specs = [pl.BlockSpec((tr, wide), lambda i, c_ref: (c_ref[0] * per + i, 0)),
                    pl.BlockSpec((tr, wide), lambda i, c_ref: (i, 0))]
        out_specs = pl.BlockSpec((4, tr, cols), lambda i, c_ref: (0, i, 0))
        grid = (per,)
    else:
        def body(c_ref, a_ref, b_ref, o_ref):
            o_ref[...] = (a_ref[...] + b_ref[...]).astype(out_dtype)

        in_specs = [pl.BlockSpec((1, tr, cols), lambda s, i, c_ref: (s, c_ref[0] * per + i, 0)),
                    pl.BlockSpec((1, tr, cols), lambda s, i, c_ref: (s, i, 0))]
        out_specs = pl.BlockSpec((1, tr, cols), lambda s, i, c_ref: (s, i, 0))
        grid = (4, per)
    return pl.pallas_call(
        body, name=name,
        grid_spec=pltpu.PrefetchScalarGridSpec(num_scalar_prefetch=1, grid=grid, in_specs=in_specs, out_specs=out_specs),
        out_shape=S((4, hr, cols), out_dtype), compiler_params=_params(len(grid)),
    )(jnp.reshape(c, (1,)).astype(jnp.int32), full, got)


def _small_add_own_half(fulls, gots, c):
    n = len(fulls)

    def body(c_ref, *refs):
        for a_ref, b_ref, o_ref in zip(refs[:n], refs[n:2 * n], refs[2 * n:]):
            hr = b_ref.shape[0]
            o_ref[...] = a_ref[pl.ds(pl.multiple_of(c_ref[0] * hr, 8), hr), :] + b_ref[...]

    specs = lambda arrs: [pl.BlockSpec(a.shape, lambda i, c_ref: (0, 0)) for a in arrs]
    return pl.pallas_call(
        body, name="small_pair_add",
        grid_spec=pltpu.PrefetchScalarGridSpec(num_scalar_prefetch=1, grid=(1,), in_specs=specs(fulls) + specs(gots),
                                               out_specs=specs(gots)),
        out_shape=[S(g.shape, F32) for g in gots], compiler_params=_params(),
    )(jnp.reshape(c, (1,)).astype(jnp.int32), *fulls, *gots)


def _small_sum_slots(own, slots, me, c):
    n = len(slots)

    def body(p_ref, *refs):
        own_refs, slot_refs, o_refs = refs[:n], refs[n:5 * n], refs[5 * n:]
        for i, (own_ref, o_ref) in enumerate(zip(own_refs, o_refs)):
            hr = own_ref.shape[0]
            acc = None
            for j in range(4):
                v = jnp.where(p_ref[0] == j, own_ref[...], slot_refs[4 * i + j][0])
                acc = v if acc is None else acc + v
            o_ref[pl.ds(pl.multiple_of(p_ref[1] * hr, 8), hr), :] = acc

    def slot_spec(s, j):
        return pl.BlockSpec((1,) + s.shape[1:], lambda i, p: (jnp.where(p[0] == j, (j + 1) % 4, j), 0, 0))

    outs = [S((2 * s.shape[1], s.shape[2]), F32) for s in slots]
    return pl.pallas_call(
        body, name="small_quad_sum",
        grid_spec=pltpu.PrefetchScalarGridSpec(
            num_scalar_prefetch=1, grid=(1,),
            in_specs=[pl.BlockSpec(o.shape, lambda i, p: (0, 0)) for o in own]
            + [slot_spec(s, j) for s in slots for j in range(4)],
            out_specs=[pl.BlockSpec(o.shape, lambda i, p: (0, 0)) for o in outs]),
        out_shape=outs, compiler_params=_params(),
    )(jnp.stack([me, c]).astype(jnp.int32), *own, *[s for s in slots for _ in range(4)])


def _sum_slots(name, own, slots, me, c):
    _, rows, cols = slots.shape
    tr = _row_tile(rows, cols, 16 if slots.dtype == jnp.bfloat16 else 8)
    per = rows // tr
    three = len(own.shape) == 3

    def body(p_ref, own_ref, s0, s1, s2, s3, o_ref):
        mine = own_ref[0] if three else own_ref[...]
        acc = None
        for j, s_ref in enumerate((s0, s1, s2, s3)):
            v = jnp.where(p_ref[0] == j, mine, s_ref[0]).astype(F32)
            acc = v if acc is None else acc + v
        o_ref[...] = acc

    def slot_spec(j):
        return pl.BlockSpec((1, tr, cols), lambda i, p: (jnp.where(p[0] == j, (j + 1) % 4, j), i, 0))

    own_spec = (pl.BlockSpec((1, tr, cols), lambda i, p: (p[0], i, 0)) if three
                else pl.BlockSpec((tr, cols), lambda i, p: (i, 0)))
    return pl.pallas_call(
        body, name=name,
        grid_spec=pltpu.PrefetchScalarGridSpec(
            num_scalar_prefetch=1, grid=(per,), in_specs=[own_spec] + [slot_spec(j) for j in range(4)],
            out_specs=pl.BlockSpec((tr, cols), lambda i, p: (p[1] * per + i, 0))),
        out_shape=S((2 * rows, cols), F32), compiler_params=_params(),
    )(jnp.stack([me, c]).astype(jnp.int32), own, slots, slots, slots, slots)


BIG = ("w_in", "w_out", "w_gate", "w_up", "w_down")
ROW_PARAMS = (("pre_mix_norm", 0), ("lru_conv_b", 12), ("lru_ba", 13), ("lru_bx", 14), ("lru_lambda", 15),
              ("lru_out_norm", 16), ("ssd_out_norm", 24), ("post_mix_norm", 33), ("pre_ffn_norm", 32), ("post_ffn_norm", 41))
LRU_CONV_ROWS = (8, 12)
LOSS_ROW = 40
HEAD_PARAMS = (("ssd_dt_bias", 0), ("ssd_a_log", 1), ("ssd_d", 2))
SMALL = tuple(n for n, _ in ROW_PARAMS) + ("ssd_conv_b",) + tuple(n for n, _ in HEAD_PARAMS) + (
    "lru_wa", "lru_wx", "lru_conv_w", "ssd_conv_w")


def _diag4(w):
    eye = jnp.eye(4, dtype=w.dtype).reshape(1, 4, 1, 4, 1)
    return (w.reshape(4, 4, BW, 1, BW) * eye).reshape(4, 4 * BW, 4 * BW)


def _adam_math(w, g, m, v):
    mm = ADAM_B1 * m + (1.0 - ADAM_B1) * g
    vv = ADAM_B2 * v + (1.0 - ADAM_B2) * (g * g)
    c1 = 1.0 - ADAM_B1 ** ADAM_STEP
    c2 = 1.0 - ADAM_B2 ** ADAM_STEP
    return -ADAM_LR * ((mm / c1) / (jnp.sqrt(vv / c2) + ADAM_EPS) + ADAM_WD * w), mm, vv


def _adamw_small(rows, cst, hst, dwa, dwx, glcw, gscw, w, m, v):
    def grad_of(name, refs):
        rows_ref, cst_ref, hst_ref, dwa_ref, dwx_ref, glcw_ref, gscw_ref = refs
        for n, r in ROW_PARAMS:
            if n == name:
                return rows_ref[r:r + 1, :]
        for n, r in HEAD_PARAMS:
            if n == name:
                return hst_ref[r:r + 1, 0:NH]
        return {"ssd_conv_b": lambda: cst_ref[4:5, :], "lru_wa": lambda: dwa_ref[...], "lru_wx": lambda: dwx_ref[...],
                "lru_conv_w": lambda: glcw_ref[...], "ssd_conv_w": lambda: gscw_ref[...]}[name]()

    shapes = {n: (w[n].shape[1:] if len(w[n].shape) > 2 else w[n].shape) for n in SMALL}
    flat = lambda d: [d[n].reshape(shapes[n]) for n in SMALL]
    ns = len(SMALL)

    def body(*refs):
        srcs, rest = refs[:7], refs[7:]
        w_refs, m_refs, v_refs = rest[:ns], rest[ns:2 * ns], rest[2 * ns:3 * ns]
        outs = rest[3 * ns:]
        for k, name in enumerate(SMALL):
            g = grad_of(name, srcs)
            d, mm, vv = _adam_math(w_refs[k][...], g, m_refs[k][...], v_refs[k][...])
            outs[4 * k][...] = g
            outs[4 * k + 1][...] = d
            outs[4 * k + 2][...] = mm
            outs[4 * k + 3][...] = vv

    res = pl.pallas_call(
        body, name="adamw_small",
        out_shape=[S(shapes[n], F32) for n in SMALL for _ in range(4)],
        compiler_params=pltpu.CompilerParams(vmem_limit_bytes=VMEM_LIMIT),
    )(rows, cst, hst, dwa, dwx, glcw, gscw, *flat(w), *flat(m), *flat(v))
    return {n: tuple(res[4 * k + i].reshape(w[n].shape) for i in range(4)) for k, n in enumerate(SMALL)}


def _with_own(own, got):
    chip = 2 * lax.axis_index("x") + lax.axis_index("y")
    return jnp.where((jnp.arange(4) == chip).reshape(4, 1, 1), own[None], got)


def _side_by_side(f):
    return f.transpose(1, 0, 2).reshape(f.shape[1], 4 * f.shape[2])


def _stacked(f):
    return f.reshape(4 * f.shape[1], f.shape[2])


def _conv_terms(lru_conv_w, ssd_conv_w):
    conv = jnp.concatenate([lru_conv_w.reshape(-1), ssd_conv_w.reshape(-1)]).astype(F32)
    hi = conv.astype(jnp.bfloat16)
    mid = (conv - hi.astype(F32)).astype(jnp.bfloat16)
    lo = (conv - hi.astype(F32) - mid.astype(F32)).astype(jnp.bfloat16)
    terms = jnp.concatenate([hi, mid, lo])
    rows = -(-terms.shape[0] // (128 * 32)) * 32
    return jnp.pad(terms, (0, rows * 128 - terms.shape[0])).reshape(rows, 128)


def _full_conv_taps(own, got, n_lru, n_ssd):
    n_terms = 3 * (n_lru + n_ssd)
    t3 = _with_own(own, got).reshape(4, -1)[:, :n_terms].reshape(4, 3, -1).astype(F32)
    conv_f = (t3[:, 0] + t3[:, 1]) + t3[:, 2]
    lcw = conv_f[:, :n_lru].reshape(4, CONV_K, -1).transpose(1, 0, 2).reshape(CONV_K, LW)
    scw = conv_f[:, n_lru:].reshape(4, CONV_K, -1).transpose(1, 0, 2).reshape(CONV_K, XBC)
    return lcw, scw


def _step(x, tgt, w_in, lru_conv_w, ssd_conv_w, sp, late):
    c = lax.axis_index("c")
    me = 2 * lax.axis_index("x") + lax.axis_index("y")
    mm = lambda w: w.astype(BF)
    row = lambda v: v.reshape(1, -1).astype(F32)
    g0 = row(sp["pre_mix_norm"])
    first = [w_in.astype(WIRE), _conv_terms(lru_conv_w, ssd_conv_w)]
    h0, *got_first = _prenorm(x, g0, first)
    win_f = _side_by_side(_with_own(first[0], got_first[0]))
    lcw, scw = _full_conv_taps(first[1], got_first[1], lru_conv_w.size, ssd_conv_w.size)
    wcat = jnp.concatenate([mm(win_f), jnp.zeros((D, PC - IN_COLS), BF)], axis=1)
    p_lru = jnp.concatenate([lcw, row(sp["lru_conv_b"]), row(sp["lru_ba"]), row(sp["lru_bx"]), row(sp["lru_lambda"]),
                             row(sp["lru_out_norm"]), jnp.zeros((7, LW), F32)], axis=0)
    wa4, wx4 = mm(_diag4(sp["lru_wa"][0])), mm(_diag4(sp["lru_wx"][0]))
    wa4T, wx4T = wa4.transpose(0, 2, 1), wx4.transpose(0, 2, 1)
    cw_ssd = jnp.concatenate([scw, row(sp["ssd_conv_b"]), jnp.zeros((3, XBC), F32)], axis=0)
    padh = lambda v: jnp.pad(row(v), ((0, 0), (0, DTP - NH)))
    hp_ssd = jnp.concatenate([padh(sp["ssd_dt_bias"]), padh(sp["ssd_a_log"]), padh(sp["ssd_d"]), jnp.zeros((5, DTP), F32)], axis=0)
    g_ssd = row(sp["ssd_out_norm"])
    g_pm, g_pf, g_pff = row(sp["post_mix_norm"]), row(sp["pre_ffn_norm"]), row(sp["post_ffn_norm"])

    h, ylru, lxc, lxr, lg, *got_a = _lru_fwd(h0, wcat, p_lru, wa4, wx4, [late[0], late[3]])
    y, yssd, states, cv, z, xbcr, dtr, *got_b = _ssd_fwd(h0, wcat, cw_ssd, hp_ssd, g_ssd, [late[1], late[2]])
    wout, wd = mm(_stacked(_with_own(late[0], got_a[0]))), mm(_stacked(_with_own(late[3], got_a[1])))
    wg, wu = mm(_side_by_side(_with_own(late[1], got_b[0]))), mm(_side_by_side(_with_own(late[2], got_b[1])))
    mix, x1, h2 = _outproj(ylru, yssd, x, wout, g_pm, g_pf)
    gate, up, act, df, dx2, st_ffn = _ffn_fwd(h2, x1, tgt, wg, wu, wd, g_pff)
    dgate, dup, dh2 = _ffn_bwd(df, gate, up, wd.T, wg.T, wu.T)
    dx1, dmix, dyl, dys, st_mix = _mix_bwd(dh2, x1, dx2, mix, wout.T, g_pf, g_pm)

    dwg = _wgrad("wgrad_gate", h2, dgate)
    dwu = _wgrad("wgrad_up", h2, dup)
    dwd = _wgrad("wgrad_down", act, df)
    dwo = jnp.concatenate([_wgrad("wgrad_out_lru", ylru, dmix), _wgrad("wgrad_out_ssd", yssd, dmix)], axis=0)
    early = [dwo.reshape(4, (LW + SI) // 4, D), dwg, dwu, dwd.reshape(4, DFF // 4, D)]
    dlx, dlg, st_lru, dwa, dwx, *got_early = _lru_bwd(dyl, lxr, lxc, lg, h, p_lru, wa4, wx4, wa4T, wx4T, early)
    part_early = [_add_own_half("pair_add_early%d" % k, b, r, c, WIRE, bc)
                  for k, (b, r, bc) in enumerate(zip(early, got_early, [False, True, True, False]))]
    dxbc, dz, ddt, cst, hst, gst, *slots_early = _ssd_bwd(dys, xbcr, cv, z, dtr, y, states, cw_ssd, hp_ssd, g_ssd,
                                                          part_early)
    red_early = [_sum_slots("quad_sum_early%d" % k, p, s, me, c) for k, (p, s) in enumerate(zip(part_early, slots_early))]

    pin = [_wgrad("wgrad_in_%d" % k, h0, b) for k, b in enumerate((dlx, dlg, dz, dxbc, ddt))]
    dwin = jnp.concatenate(pin[:4] + [pin[4][:, :NH]], axis=1)
    (got_win,) = _pair_exchange("pair_exchange_w_in", [dwin])
    part_win = _add_own_half("pair_add_w_in", dwin, got_win, c, WIRE, True)
    gx, st_in, slots_win = _inproj_bwd(dlx, dlg, dz, dxbc, ddt, x, dx1, wcat.T, g0, [part_win])
    red_win = _sum_slots("quad_sum_w_in", part_win, slots_win, me, c)

    rows = jnp.concatenate([st_in, st_lru, gst, st_mix, st_ffn], axis=0)
    small = [rows, cst, hst, dwa.reshape(NBLK * BW, BW), dwx.reshape(NBLK * BW, BW)]
    part_small = list(_small_add_own_half(small, list(_pair_exchange("pair_exchange_small", small)), c))
    red_small = list(_small_sum_slots(part_small, list(_quad_exchange(part_small)), me, c))
    out = list(_pair_gather([red_win] + red_early + red_small))
    big = dict(zip(("w_in", "w_out", "w_gate", "w_up", "w_down"), out[:5]))
    return gx, big, out[5:]


def kernel(x, pre_mix_norm, w_in, lru_conv_w, lru_conv_b, lru_wa, lru_ba, lru_wx, lru_bx, lru_lambda, lru_out_norm, ssd_conv_w, ssd_conv_b, ssd_dt_bias, ssd_a_log, ssd_d, ssd_out_norm, w_out, post_mix_norm, pre_ffn_norm, w_gate, w_up, w_down, post_ffn_norm, loss_target, m_pre_mix_norm, m_w_in, m_lru_conv_w, m_lru_conv_b, m_lru_wa, m_lru_ba, m_lru_wx, m_lru_bx, m_lru_lambda, m_lru_out_norm, m_ssd_conv_w, m_ssd_conv_b, m_ssd_dt_bias, m_ssd_a_log, m_ssd_d, m_ssd_out_norm, m_w_out, m_post_mix_norm, m_pre_ffn_norm, m_w_gate, m_w_up, m_w_down, m_post_ffn_norm, v_pre_mix_norm, v_w_in, v_lru_conv_w, v_lru_conv_b, v_lru_wa, v_lru_ba, v_lru_wx, v_lru_bx, v_lru_lambda, v_lru_out_norm, v_ssd_conv_w, v_ssd_conv_b, v_ssd_dt_bias, v_ssd_a_log, v_ssd_d, v_ssd_out_norm, v_w_out, v_post_mix_norm, v_pre_ffn_norm, v_w_gate, v_w_up, v_w_down, v_post_ffn_norm):
    args = dict(locals())
    names = list(SMALL) + list(BIG)
    w = {n: args[n] for n in names}
    m = {n: args["m_" + n] for n in names}
    v = {n: args["v_" + n] for n in names}
    chip = 2 * lax.axis_index("x") + lax.axis_index("y")

    late = [a[0].astype(WIRE) for a in (w_out, w_gate, w_up, w_down)]
    gx, red, (rows, cst, hst, dwa, dwx) = _step(x[0], loss_target[0], w_in[0], lru_conv_w[0], ssd_conv_w[0],
                                                {n: w[n] for n in SMALL}, late)
    loss = jnp.sum(rows[LOSS_ROW])

    grads, delta, new_m, new_v = {}, {}, {}, {}
    for n in BIG:
        g = red[n]
        if n in ("w_in", "w_gate", "w_up"):
            t = lambda a: jnp.swapaxes(a, 1, 2)
            gt = g.T
            out = _adamw("adamw_" + n, t(w[n]), gt, t(m[n]), t(v[n]))
            delta[n], new_m[n], new_v[n] = (t(o) for o in out)
            grads[n] = t(gt[None])
        else:
            delta[n], new_m[n], new_v[n] = _adamw("adamw_" + n, w[n], g, m[n], v[n])
            grads[n] = g[None]

    lc, sc = lru_conv_w.shape[-1], ssd_conv_w.shape[-1]
    glcw = lax.dynamic_slice_in_dim(rows[LRU_CONV_ROWS[0]:LRU_CONV_ROWS[1]], chip * lc, lc, axis=1)
    gscw = lax.dynamic_slice_in_dim(cst[0:CONV_K], chip * sc, sc, axis=1)
    res = _adamw_small(rows, cst, hst, dwa.reshape(NBLK, BW, BW), dwx.reshape(NBLK, BW, BW), glcw, gscw,
                       {n: w[n] for n in SMALL}, {n: m[n] for n in SMALL}, {n: v[n] for n in SMALL})
    for n in SMALL:
        grads[n], delta[n], new_m[n], new_v[n] = res[n]

    order = ["pre_mix_norm", "w_in", "lru_conv_w", "lru_conv_b", "lru_wa", "lru_ba", "lru_wx", "lru_bx", "lru_lambda",
             "lru_out_norm", "ssd_conv_w", "ssd_conv_b", "ssd_dt_bias", "ssd_a_log", "ssd_d", "ssd_out_norm", "w_out",
             "post_mix_norm", "pre_ffn_norm", "w_gate", "w_up", "w_down", "post_ffn_norm"]
    return (loss, gx[None], *[grads[n] for n in order], *[delta[n] for n in order],
            *[new_m[n] for n in order], *[new_v[n] for n in order])
```

```python
import functools

import jax
import jax.numpy as jnp
from jax import lax
from jax.experimental import pallas as pl
from jax.experimental.pallas import tpu as pltpu

F32 = jnp.float32
BF = jnp.bfloat16

D = 1024
LW = 1024
NBLK = 16
BW = 64
SI = 1024
NH = 16
HD = 64
NG = 2
HPG = NH // NG
NS = 128
CH = 128
XBC = SI + 2 * NG * NS
DTP = 128
PC = 3 * 1024 + XBC + DTP
DFF = 2816
IN_COLS = 4624
EPS = 1e-6
LRU_C = 8.0
CONV_K = 4
TT = 256
TW = 512
VMEM_LIMIT = 56 * 1024 * 1024

ADAM_LR, ADAM_B1, ADAM_B2, ADAM_EPS, ADAM_WD, ADAM_STEP = 0.001, 0.9, 0.999, 1e-08, 0.01, 10

MESH = pl.DeviceIdType.MESH


def _mm(a, b):
    return jnp.dot(a.astype(BF), b.astype(BF), preferred_element_type=F32)


def _mm_nt(a, b):
    return lax.dot_general(a.astype(BF), b.astype(BF), (((1,), (1,)), ((), ())), preferred_element_type=F32)


def _mm_tn(a, b):
    return lax.dot_general(a.astype(BF), b.astype(BF), (((0,), (0,)), ((), ())), preferred_element_type=F32)


def _sigmoid(x):
    return 0.5 * jnp.tanh(0.5 * x) + 0.5


def _softplus(x):
    return jnp.maximum(x, 0.0) + jnp.log1p(jnp.exp(-jnp.abs(x)))


_GELU_C = 0.7978845608028654
_GELU_K = 0.044715


def _gelu(x):
    t = jnp.tanh(_GELU_C * (x + _GELU_K * x * x * x))
    return 0.5 * x * (1.0 + t)


def _gelu_grad(x):
    t = jnp.tanh(_GELU_C * (x + _GELU_K * x * x * x))
    return 0.5 * (1.0 + t) + 0.5 * x * (1.0 - t * t) * _GELU_C * (1.0 + 3.0 * _GELU_K * x * x)


def _rms_fwd(x, g):
    r = lax.rsqrt(jnp.mean(x * x, axis=-1, keepdims=True) + EPS)
    return x * r * g


def _rms_bwd(x, g, dy):
    r = lax.rsqrt(jnp.mean(x * x, axis=-1, keepdims=True) + EPS)
    xh = x * r
    dxh = dy * g
    dg = jnp.sum(dy * xh, axis=0, keepdims=True)
    dx = r * (dxh - xh * jnp.mean(dxh * xh, axis=-1, keepdims=True))
    return dx, dg


def _sum_all(x):
    return jnp.sum(jnp.sum(x, axis=1, keepdims=True), axis=0, keepdims=True)


def _cumsum_rows(x, n):
    row = lax.broadcasted_iota(jnp.int32, x.shape, 0)
    k = 1
    while k < n:
        x = x + jnp.where(row >= k, pltpu.roll(x, k, 0), 0.0)
        k *= 2
    return x


def _rev_cumsum_rows(x, n):
    row = lax.broadcasted_iota(jnp.int32, x.shape, 0)
    k = 1
    while k < n:
        x = x + jnp.where(row < n - k, pltpu.roll(x, n - k, 0), 0.0)
        k *= 2
    return x


def _load_once(pairs, sem):
    @pl.when(pl.program_id(0) == 0)
    def _():
        for k, (src, dst) in enumerate(pairs):
            pltpu.make_async_copy(src, dst, sem.at[k]).start()
        for k, (src, dst) in enumerate(pairs):
            pltpu.make_async_copy(src, dst, sem.at[k]).wait()


def _params(n_axes=1):
    return pltpu.CompilerParams(dimension_semantics=("arbitrary",) * n_axes, vmem_limit_bytes=VMEM_LIMIT)


def _rows(n, width, rev_of=None):
    if rev_of is None:
        return pl.BlockSpec((n, width), lambda i: (i, 0))
    return pl.BlockSpec((n, width), lambda i: (rev_of - 1 - i, 0))


def _whole(shape):
    nd = len(shape)
    return pl.BlockSpec(shape, lambda i: (0,) * nd)


ANY = pl.BlockSpec(memory_space=pl.ANY)
S = jax.ShapeDtypeStruct
WIRE = jnp.bfloat16


def _pos():
    return lax.axis_index("x"), lax.axis_index("y"), lax.axis_index("c")


def _other_chips(x, y):
    return [(1 - x, y), (x, 1 - y), (1 - x, 1 - y)]


def _remote(src, dst, send_sem, recv_sem, to):
    return pltpu.make_async_remote_copy(src_ref=src, dst_ref=dst, send_sem=send_sem, recv_sem=recv_sem,
                                        device_id=to, device_id_type=MESH)


def _gather_phase(phase, ins, outs, send_sems, recv_sems):
    x, y, c = _pos()
    me = 2 * x + y
    chips = _other_chips(x, y)
    for i, (src, dst) in enumerate(zip(ins, outs)):
        hr = src.shape[0] // 2
        my_half = pl.ds(pl.multiple_of(c * hr, 16), hr)
        sib_half = pl.ds(pl.multiple_of((1 - c) * hr, 16), hr)
        for k, (cx, cy) in enumerate(chips):
            s1, r1 = send_sems.at[6 * i + k], recv_sems.at[6 * i + k]
            s2, r2 = send_sems.at[6 * i + 3 + k], recv_sems.at[6 * i + 3 + k]
            first = lambda: _remote(src.at[my_half, :], dst.at[me, my_half, :], s1, r1, (cx, cy, c))
            landed = dst.at[2 * cx + cy, my_half, :]
            passed = lambda: _remote(landed, landed, s2, r2, (x, y, 1 - c))
            if phase == 0:
                first().start()
            elif phase == 1:
                _remote(landed, landed, s1, r1, (cx, cy, c)).wait_recv()
                passed().start()
            else:
                theirs = dst.at[2 * cx + cy, sib_half, :]
                _remote(theirs, theirs, s2, r2, (x, y, 1 - c)).wait_recv()
                first().wait_send()
                passed().wait_send()


def _half(ref, c, hr):
    sl = pl.ds(pl.multiple_of(c * hr, 8), hr)
    return ref.at[:, sl, :] if len(ref.shape) == 3 else ref.at[sl, :]


def _half_shape(b):
    return b.shape[:-2] + (b.shape[-2] // 2, b.shape[-1])


def _pair_phase(phase, ins, outs, send_sems, recv_sems):
    x, y, c = _pos()
    for k, (src, dst) in enumerate(zip(ins, outs)):
        cp = _remote(_half(src, 1 - c, src.shape[-2] // 2), dst, send_sems.at[k], recv_sems.at[k], (x, y, 1 - c))
        if phase == 0:
            cp.start()
        else:
            cp.wait()


def _quad_phase(phase, ins, outs, send_sems, recv_sems):
    x, y, c = _pos()
    me = 2 * x + y
    for i, (src, dst) in enumerate(zip(ins, outs)):
        for k, (cx, cy) in enumerate(_other_chips(x, y)):
            cp = _remote(src.at[2 * cx + cy], dst.at[me], send_sems.at[3 * i + k], recv_sems.at[3 * i + k], (cx, cy, c))
            if phase == 0:
                cp.start()
            else:
                got = dst.at[2 * cx + cy]
                _remote(got, got, send_sems.at[3 * i + k], recv_sems.at[3 * i + k], (cx, cy, c)).wait_recv()
                cp.wait_send()


def _prenorm(x, g0, shards):
    T = x.shape[0]
    tt = 2 * TT
    nt = T // tt
    ng = len(shards)

    def body(*refs):
        x_ref, g_ref = refs[:2]
        sh_in = refs[2:2 + ng]
        h0_ref = refs[2 + ng]
        sh_out = refs[3 + ng:3 + 2 * ng]
        send_sems, recv_sems = refs[3 + 2 * ng:]
        for phase, step in enumerate((0, nt // 2, nt - 1)):
            @pl.when(pl.program_id(0) == step)
            def _():
                _gather_phase(phase, sh_in, sh_out, send_sems, recv_sems)

        h0_ref[...] = _rms_fwd(x_ref[...], g_ref[...]).astype(BF)

    return pl.pallas_call(
        body, name="prenorm", grid=(nt,),
        in_specs=[_rows(tt, D), _whole((1, D))] + [ANY] * ng, out_specs=[_rows(tt, D)] + [ANY] * ng,
        out_shape=[S((T, D), BF)] + [S((4,) + s.shape, s.dtype) for s in shards],
        scratch_shapes=[pltpu.SemaphoreType.DMA((6 * ng,)), pltpu.SemaphoreType.DMA((6 * ng,))],
        compiler_params=_params(),
    )(x, g0, *shards)


def _blockdiag_mm(v, w4_ref):
    return jnp.concatenate([_mm(v[:, 256 * j:256 * (j + 1)], w4_ref[j]) for j in range(4)], axis=1)


def _lru_gates(lx, p_ref, wa_ref, wx_ref):
    r = _sigmoid(_blockdiag_mm(lx, wa_ref) + p_ref[5:6, :])
    i = _sigmoid(_blockdiag_mm(lx, wx_ref) + p_ref[6:7, :])
    sp = _softplus(-p_ref[7:8, :])
    la = -LRU_C * r * sp
    a = jnp.exp(la)
    th = jnp.tanh(la)
    mult = jnp.sqrt(-2.0 * th / (1.0 - th))
    return r, i, sp, a, mult


def _conv_from(xp_ref, p_ref, n):
    acc = p_ref[4:5, :] + p_ref[0:1, :] * xp_ref[pl.ds(8 - CONV_K + 1, n), :]
    for k in range(1, CONV_K):
        acc = acc + p_ref[k:k + 1, :] * xp_ref[pl.ds(8 - CONV_K + 1 + k, n), :]
    return acc


def _conv_bwd(dp_ref, dconv, x, p_ref, st_ref, n):
    if dconv is None:
        dconv = dp_ref[0:n, :]
    else:
        dp_ref[0:n, :] = dconv
    acc = None
    for k in range(CONV_K):
        g = dp_ref[pl.ds(CONV_K - 1 - k, n), :]
        acc = p_ref[k:k + 1, :] * g if acc is None else acc + p_ref[k:k + 1, :] * g
        st_ref[k:k + 1, :] += jnp.sum(g * x, axis=0, keepdims=True)
    st_ref[4:5, :] += jnp.sum(dconv, axis=0, keepdims=True)
    dp_ref[n:n + 8, :] = dp_ref[0:8, :]
    return acc


def _lru_fwd(h0, wcat, p_lru, wa4, wx4, shards):
    T = h0.shape[0]
    NT = T // TT
    ng = len(shards)

    def body(*refs):
        h0_ref, w_hbm, p_ref, wa_ref, wx_ref = refs[:5]
        sh_in = refs[5:5 + ng]
        h_ref, y_ref, lxc_ref, lxr_ref, lg_ref = refs[5 + ng:10 + ng]
        sh_out = refs[10 + ng:10 + 2 * ng]
        xp, a_s, u_s, hc, w_vm, wsem, send_sems, recv_sems = refs[10 + 2 * ng:]
        _load_once([(w_hbm.at[:, 0:2 * LW], w_vm)], wsem)
        for phase, step in enumerate((0, NT // 2, NT - 1)):
            @pl.when(pl.program_id(0) == step)
            def _():
                _gather_phase(phase, sh_in, sh_out, send_sems, recv_sems)

        @pl.when(pl.program_id(0) == 0)
        def _():
            xp[0:8, :] = jnp.zeros((8, LW), F32)
            hc[...] = jnp.zeros_like(hc)

        hv = h0_ref[...]
        lxr = jnp.dot(hv, w_vm[:, 0:LW], preferred_element_type=F32)
        lxr_ref[...] = lxr
        lg_ref[...] = jnp.dot(hv, w_vm[:, LW:2 * LW], preferred_element_type=F32)
        xp[8:8 + TT, :] = lxr
        lx = _conv_from(xp, p_ref, TT)
        lxc_ref[...] = lx
        xp[0:8, :] = xp[TT:TT + 8, :]
        r, i, sp, a, mult = _lru_gates(lx, p_ref, wa_ref, wx_ref)
        a_s[...] = a
        u_s[...] = mult * (i * lx)

        def step(t, h):
            h = a_s[pl.ds(t, 1), :] * h + u_s[pl.ds(t, 1), :]
            h_ref[pl.ds(t, 1), :] = h
            return h

        hc[0:1, :] = lax.fori_loop(0, TT, step, hc[0:1, :], unroll=8)
        gated = h_ref[...] * _gelu(lg_ref[...])
        y_ref[...] = _rms_fwd(gated, p_ref[8:9, :]).astype(BF)

    return pl.pallas_call(
        body, name="lru_fwd", grid=(NT,),
        in_specs=[_rows(TT, D), ANY, _whole((16, LW)), _whole((4, 256, 256)), _whole((4, 256, 256))] + [ANY] * ng,
        out_specs=[_rows(TT, LW), _rows(TT, LW), _rows(TT, LW), _rows(TT, LW), _rows(TT, LW)] + [ANY] * ng,
        out_shape=[S((T, LW), F32), S((T, LW), BF), S((T, LW), F32), S((T, LW), F32), S((T, LW), F32)]
        + [S((4,) + s.shape, s.dtype) for s in shards],
        scratch_shapes=[pltpu.VMEM((TT + 8, LW), F32), pltpu.VMEM((TT, LW), F32), pltpu.VMEM((TT, LW), F32),
                        pltpu.VMEM((8, LW), F32), pltpu.VMEM((D, 2 * LW), BF), pltpu.SemaphoreType.DMA((1,)),
                        pltpu.SemaphoreType.DMA((6 * ng,)), pltpu.SemaphoreType.DMA((6 * ng,))],
        compiler_params=_params(),
    )(h0, wcat, p_lru, wa4, wx4, *shards)


def _ssd_prep(cv, dt_ref, hp_ref):
    sg = _sigmoid(cv)
    xbc = cv * sg
    lane = lax.broadcasted_iota(jnp.int32, (CH, DTP), 1)
    raw = dt_ref[...] + hp_ref[0:1, :]
    dtv = jnp.where(lane < NH, _softplus(raw), 0.0)
    A = jnp.where(lane[0:1, :] < NH, -jnp.exp(hp_ref[1:2, :]), 0.0)
    cs = _cumsum_rows(dtv * A, CH)
    return sg, xbc, raw, dtv, A, cs


def _per_head_lanes(v):
    r = v.shape[0]
    first = lax.broadcasted_iota(jnp.int32, (r, 2 * HD), 1) < HD
    pairs = [jnp.where(first, jnp.broadcast_to(v[:, 2 * j:2 * j + 1], (r, 2 * HD)),
                       jnp.broadcast_to(v[:, 2 * j + 1:2 * j + 2], (r, 2 * HD))) for j in range(NH // 2)]
    return jnp.concatenate(pairs, axis=1)


def _per_head_rows(col, g):
    return jnp.concatenate([jnp.broadcast_to(col[g * HPG + k:g * HPG + k + 1, :], (HD, NS)) for k in range(HPG)], axis=0)


def _ssd_decays(cs):
    csT = cs.T
    cl = cs[CH - 1:CH, :]
    E_x = _per_head_lanes(jnp.exp(cs))
    dsm = jnp.exp(cl - cs)
    ds_x = _per_head_lanes(dsm)
    El_rows = jnp.broadcast_to(jnp.exp(csT[0:NH, CH - 1:CH]), (NH, NS))
    return csT, dsm, E_x, ds_x, El_rows


def _ssd_fwd(h0, wcat, cw_ssd, hp_ssd, g_ssd, shards):
    T = h0.shape[0]
    NC = T // CH
    ng = len(shards)
    c0 = 2 * LW

    def body(*refs):
        h0_ref, w_hbm, cw_ref, hp_ref, g_ref = refs[:5]
        sh_in = refs[5:5 + ng]
        y_ref, yn_ref, st_ref, cv_ref, z_ref, xr_ref, dt_ref = refs[5 + ng:12 + ng]
        sh_out = refs[12 + ng:12 + 2 * ng]
        xp, st, w_vm, wsem, send_sems, recv_sems = refs[12 + 2 * ng:]
        _load_once([(w_hbm.at[:, c0:PC], w_vm)], wsem)
        for phase, step in enumerate((0, NC // 2, NC - 1)):
            @pl.when(pl.program_id(0) == step)
            def _():
                _gather_phase(phase, sh_in, sh_out, send_sems, recv_sems)

        @pl.when(pl.program_id(0) == 0)
        def _():
            xp[0:8, :] = jnp.zeros((8, XBC), F32)
            st[...] = jnp.zeros_like(st)

        hv = h0_ref[...]
        z_ref[...] = jnp.dot(hv, w_vm[:, 0:SI], preferred_element_type=F32)
        xraw = jnp.dot(hv, w_vm[:, SI:SI + XBC], preferred_element_type=F32)
        xr_ref[...] = xraw
        dt_ref[...] = jnp.dot(hv, w_vm[:, SI + XBC:SI + XBC + DTP], preferred_element_type=F32)
        xp[8:8 + CH, :] = xraw
        cv = _conv_from(xp, cw_ref, CH)
        cv_ref[...] = cv
        sg, xbc, raw, dtv, A, cs = _ssd_prep(cv, dt_ref, hp_ref)
        xp[0:8, :] = xp[CH:CH + 8, :]
        st_ref[0] = st[...]
        csT, dsm, E_x, ds_x, El_rows = _ssd_decays(cs)
        X = xbc[:, 0:SI]
        xs = X * _per_head_lanes(dtv)
        xsd = (xs * ds_x).astype(BF)
        DX = _per_head_lanes(hp_ref[...])[2:3, :] * X
        tril = lax.broadcasted_iota(jnp.int32, (CH, CH), 0) >= lax.broadcasted_iota(jnp.int32, (CH, CH), 1)
        first = lax.broadcasted_iota(jnp.int32, (CH, 2 * HD), 1) < HD
        GW = HPG * HD
        for g in range(NG):
            Bg = xbc[:, SI + NS * g:SI + NS * (g + 1)].astype(BF)
            Cg = xbc[:, SI + NG * NS + NS * g:SI + NG * NS + NS * (g + 1)].astype(BF)
            G = _mm_nt(Cg, Bg)
            Sg = st[GW * g:GW * (g + 1), :]
            Yo = _mm_nt(Cg, Sg) * E_x[:, GW * g:GW * (g + 1)]
            st[GW * g:GW * (g + 1), :] = _per_head_rows(El_rows, g) * Sg + _mm_tn(xsd[:, GW * g:GW * (g + 1)], Bg)
            for jj in range(HPG // 2):
                j = g * (HPG // 2) + jj
                ps = slice(2 * HD * j, 2 * HD * (j + 1))
                xs_pair = xs[:, ps]
                acc = Yo[:, 2 * HD * jj:2 * HD * (jj + 1)] + DX[:, ps]
                for e in range(2):
                    h = 2 * j + e
                    Lm = jnp.exp(jnp.where(tril, cs[:, h:h + 1] - csT[h:h + 1, :], -1e30))
                    acc = acc + _mm(G * Lm, jnp.where(first if e == 0 else ~first, xs_pair, 0.0))
                y_ref[:, ps] = acc
        zz = z_ref[...]
        gated = y_ref[...] * (zz * _sigmoid(zz))
        yn_ref[...] = _rms_fwd(gated, g_ref[...]).astype(BF)

    return pl.pallas_call(
        body, name="ssd_fwd", grid=(NC,),
        in_specs=[_rows(CH, D), ANY, _whole((8, XBC)), _whole((8, DTP)), _whole((1, SI))] + [ANY] * ng,
        out_specs=[_rows(CH, SI), _rows(CH, SI), pl.BlockSpec((1, NH * HD, NS), lambda i: (i, 0, 0)), _rows(CH, XBC),
                   _rows(CH, SI), _rows(CH, XBC), _rows(CH, DTP)] + [ANY] * ng,
        out_shape=[S((T, SI), F32), S((T, SI), BF), S((NC, NH * HD, NS), F32), S((T, XBC), F32),
                   S((T, SI), F32), S((T, XBC), F32), S((T, DTP), F32)] + [S((4,) + s.shape, s.dtype) for s in shards],
        scratch_shapes=[pltpu.VMEM((CH + 8, XBC), F32), pltpu.VMEM((NH * HD, NS), F32),
                        pltpu.VMEM((D, PC - c0), BF), pltpu.SemaphoreType.DMA((1,)),
                        pltpu.SemaphoreType.DMA((6 * ng,)), pltpu.SemaphoreType.DMA((6 * ng,))],
        compiler_params=_params(),
    )(h0, wcat, cw_ssd, hp_ssd, g_ssd, *shards)


def _outproj(ylru, yssd, x, wout, g_pm, g_pf):
    T = x.shape[0]

    def body(yl_ref, ys_ref, x_ref, w_hbm, gpm_ref, gpf_ref, mix_ref, x1_ref, h2_ref, w_vm, sem):
        _load_once([(w_hbm, w_vm)], sem)
        mix = (jnp.dot(yl_ref[...], w_vm[0:LW, :], preferred_element_type=F32)
               + jnp.dot(ys_ref[...], w_vm[LW:LW + SI, :], preferred_element_type=F32))
        mix_ref[...] = mix
        x1 = x_ref[...] + _rms_fwd(mix, gpm_ref[...])
        x1_ref[...] = x1
        h2_ref[...] = _rms_fwd(x1, gpf_ref[...]).astype(BF)

    return pl.pallas_call(
        body, name="outproj", grid=(T // TW,),
        in_specs=[_rows(TW, LW), _rows(TW, SI), _rows(TW, D), ANY, _whole((1, D)), _whole((1, D))],
        out_specs=[_rows(TW, D), _rows(TW, D), _rows(TW, D)],
        out_shape=[S((T, D), F32), S((T, D), F32), S((T, D), BF)],
        scratch_shapes=[pltpu.VMEM((LW + SI, D), BF), pltpu.SemaphoreType.DMA((1,))],
        compiler_params=_params(),
    )(ylru, yssd, x, wout, g_pm, g_pf)


def _ffn_fwd(h2, x1, tgt, wg, wu, wd, g_pff):
    T = x1.shape[0]

    def body(h2_ref, x1_ref, t_ref, wg_hbm, wu_hbm, wd_hbm, g_ref,
             gate_ref, up_ref, act_ref, df_ref, dx2_ref, st_ref, wg_vm, wu_vm, wd_vm, sem):
        _load_once([(wg_hbm, wg_vm), (wu_hbm, wu_vm), (wd_hbm, wd_vm)], sem)

        @pl.when(pl.program_id(0) == 0)
        def _():
            st_ref[...] = jnp.zeros_like(st_ref)

        h2 = h2_ref[...]
        gate = jnp.dot(h2, wg_vm[...], preferred_element_type=F32)
        up = jnp.dot(h2, wu_vm[...], preferred_element_type=F32)
        gate_ref[...] = gate
        up_ref[...] = up
        act = (gate * _sigmoid(gate) * up).astype(BF)
        act_ref[...] = act
        f = jnp.dot(act, wd_vm[...], preferred_element_type=F32)
        g = g_ref[...]
        x2 = x1_ref[...] + _rms_fwd(f, g)
        err = x2 - t_ref[...]
        st_ref[0:1, :] += 0.5 * jnp.sum(err * err, axis=0, keepdims=True) * (1.0 / D)
        dx2 = err * (1.0 / D)
        dx2_ref[...] = dx2
        df, dg = _rms_bwd(f, g, dx2)
        df_ref[...] = df.astype(BF)
        st_ref[1:2, :] += dg

    return pl.pallas_call(
        body, name="ffn_fwd", grid=(T // TT,),
        in_specs=[_rows(TT, D), _rows(TT, D), _rows(TT, D), ANY, ANY, ANY, _whole((1, D))],
        out_specs=[_rows(TT, DFF), _rows(TT, DFF), _rows(TT, DFF), _rows(TT, D), _rows(TT, D), _whole((8, D))],
        out_shape=[S((T, DFF), F32), S((T, DFF), F32), S((T, DFF), BF), S((T, D), BF), S((T, D), F32), S((8, D), F32)],
        scratch_shapes=[pltpu.VMEM((D, DFF), BF), pltpu.VMEM((D, DFF), BF), pltpu.VMEM((DFF, D), BF),
                        pltpu.SemaphoreType.DMA((3,))],
        compiler_params=_params(),
    )(h2, x1, tgt, wg, wu, wd, g_pff)


def _ffn_bwd(df, gate, up, wdT, wgT, wuT):
    T = df.shape[0]

    def body(df_ref, gate_ref, up_ref, wd_hbm, wg_hbm, wu_hbm, dgate_ref, dup_ref, dh2_ref, wd_vm, wg_vm, wu_vm, sem):
        _load_once([(wd_hbm, wd_vm), (wg_hbm, wg_vm), (wu_hbm, wu_vm)], sem)
        dact = jnp.dot(df_ref[...], wd_vm[...], preferred_element_type=F32)
        gate = gate_ref[...]
        s = _sigmoid(gate)
        dup = (dact * (gate * s)).astype(BF)
        dgate = (dact * up_ref[...] * (s + gate * s * (1.0 - s))).astype(BF)
        dup_ref[...] = dup
        dgate_ref[...] = dgate
        dh2_ref[...] = (jnp.dot(dgate, wg_vm[...], preferred_element_type=F32)
                        + jnp.dot(dup, wu_vm[...], preferred_element_type=F32))

    return pl.pallas_call(
        body, name="ffn_bwd", grid=(T // TT,),
        in_specs=[_rows(TT, D), _rows(TT, DFF), _rows(TT, DFF), ANY, ANY, ANY],
        out_specs=[_rows(TT, DFF), _rows(TT, DFF), _rows(TT, D)],
        out_shape=[S((T, DFF), BF), S((T, DFF), BF), S((T, D), F32)],
        scratch_shapes=[pltpu.VMEM((D, DFF), BF), pltpu.VMEM((DFF, D), BF), pltpu.VMEM((DFF, D), BF),
                        pltpu.SemaphoreType.DMA((3,))],
        compiler_params=_params(),
    )(df, gate, up, wdT, wgT, wuT)


def _mix_bwd(dh2, x1, dx2, mix, woutT, g_pf, g_pm):
    T = x1.shape[0]

    def body(dh2_ref, x1_ref, dx2_ref, mix_ref, w_hbm, gpf_ref, gpm_ref,
             dx1_ref, dmix_ref, dyl_ref, dys_ref, st_ref, w_vm, sem):
        _load_once([(w_hbm, w_vm)], sem)

        @pl.when(pl.program_id(0) == 0)
        def _():
            st_ref[...] = jnp.zeros_like(st_ref)

        dxa, dgpf = _rms_bwd(x1_ref[...], gpf_ref[...], dh2_ref[...])
        dx1 = dx2_ref[...] + dxa
        dx1_ref[...] = dx1
        dmix, dgpm = _rms_bwd(mix_ref[...], gpm_ref[...], dx1)
        dmix = dmix.astype(BF)
        dmix_ref[...] = dmix
        st_ref[0:1, :] += dgpf
        st_ref[1:2, :] += dgpm
        dyl_ref[...] = jnp.dot(dmix, w_vm[:, 0:LW], preferred_element_type=F32)
        dys_ref[...] = jnp.dot(dmix, w_vm[:, LW:LW + SI], preferred_element_type=F32)

    return pl.pallas_call(
        body, name="mix_bwd", grid=(T // TW,),
        in_specs=[_rows(TW, D), _rows(TW, D), _rows(TW, D), _rows(TW, D), ANY, _whole((1, D)), _whole((1, D))],
        out_specs=[_rows(TW, D), _rows(TW, D), _rows(TW, LW), _rows(TW, SI), _whole((8, D))],
        out_shape=[S((T, D), F32), S((T, D), BF), S((T, LW), F32), S((T, SI), F32), S((8, D), F32)],
        scratch_shapes=[pltpu.VMEM((D, LW + SI), BF), pltpu.SemaphoreType.DMA((1,))],
        compiler_params=_params(),
    )(dh2, x1, dx2, mix, woutT, g_pf, g_pm)


def _halo(width, n_tiles, tile):
    per = tile // 8
    return pl.BlockSpec((8, width), lambda i: (jnp.maximum((n_tiles - 1 - i) * per - 1, 0), 0))


def _lru_bwd(dy, lxr, lxc, lg, h, p_lru, wa4, wx4, wa4T, wx4T, bufs):
    T = dy.shape[0]
    NT = T // TT
    nb = len(bufs)

    def body(*refs):
        dy_ref, lxr_ref, lxc_ref, lg_ref, h_ref, hh_ref, p_ref, wa_ref, wx_ref, waT_ref, wxT_ref = refs[:11]
        b_in = refs[11:11 + nb]
        dlx_ref, dlg_ref, st_ref, dwa_ref, dwx_ref = refs[11 + nb:16 + nb]
        b_out = refs[16 + nb:16 + 2 * nb]
        hp, dp, a_s, d_s, g_s, cc, send_sems, recv_sems = refs[16 + 2 * nb:]
        for phase, step in enumerate((0, NT - 1)):
            @pl.when(pl.program_id(0) == step)
            def _():
                _pair_phase(phase, b_in, b_out, send_sems, recv_sems)

        dy = dy_ref[...]
        first = pl.program_id(0) == 0
        top = pl.program_id(0) == NT - 1

        @pl.when(first)
        def _():
            st_ref[...] = jnp.zeros_like(st_ref)
            dwa_ref[...] = jnp.zeros_like(dwa_ref)
            dwx_ref[...] = jnp.zeros_like(dwx_ref)
            dp[TT:TT + 8, :] = jnp.zeros((8, LW), F32)
            cc[...] = jnp.zeros_like(cc)

        hp[0:8, :] = hh_ref[...] * jnp.where(top, 0.0, 1.0)
        hp[8:8 + TT, :] = h_ref[...]
        lx = lxc_ref[...]
        r, i, sp, a, mult = _lru_gates(lx, p_ref, wa_ref, wx_ref)

        lg = lg_ref[...]
        hcur = h_ref[...]
        ge = _gelu(lg)
        dgated, dgn = _rms_bwd(hcur * ge, p_ref[8:9, :], dy)
        st_ref[8:9, :] += dgn
        dlg_ref[...] = (dgated * hcur * _gelu_grad(lg)).astype(BF)
        a_s[...] = a
        d_s[...] = dgated * ge

        def step(k, c):
            t = TT - 1 - k
            g = d_s[pl.ds(t, 1), :] + c
            g_s[pl.ds(t, 1), :] = g
            return a_s[pl.ds(t, 1), :] * g

        cc[0:1, :] = lax.fori_loop(0, TT, step, cc[0:1, :], unroll=8)
        gt = g_s[...]
        da = gt * hp[pl.ds(7, TT), :]
        dmult = gt * i * lx
        di = gt * mult * lx
        dlxc = gt * mult * i
        dla = da * a - dmult * (a * a) / mult
        dr = dla * (-LRU_C * sp)
        st_ref[7:8, :] += jnp.sum(dla * (-LRU_C * r), axis=0, keepdims=True) * (-_sigmoid(-p_ref[7:8, :]))
        dzr = dr * r * (1.0 - r)
        dzi = di * i * (1.0 - i)
        st_ref[5:6, :] += jnp.sum(dzr, axis=0, keepdims=True)
        st_ref[6:7, :] += jnp.sum(dzi, axis=0, keepdims=True)
        dlxc = dlxc + _blockdiag_mm(dzr, waT_ref) + _blockdiag_mm(dzi, wxT_ref)
        for j in range(4):
            sl = slice(256 * j, 256 * (j + 1))
            pa = _mm_tn(lx[:, sl], dzr[:, sl])
            px = _mm_tn(lx[:, sl], dzi[:, sl])
            for b in range(4):
                bs = slice(BW * b, BW * (b + 1))
                dwa_ref[4 * j + b] += pa[bs, bs]
                dwx_ref[4 * j + b] += px[bs, bs]
        dlx_ref[...] = _conv_bwd(dp, dlxc, lxr_ref[...], p_ref, st_ref, TT).astype(BF)

    w4 = _whole((4, 256, 256))
    return pl.pallas_call(
        body, name="lru_bwd", grid=(NT,),
        in_specs=[_rows(TT, LW, NT), _rows(TT, LW, NT), _rows(TT, LW, NT), _rows(TT, LW, NT), _rows(TT, LW, NT),
                  _halo(LW, NT, TT), _whole((16, LW)), w4, w4, w4, w4] + [ANY] * nb,
        out_specs=[_rows(TT, LW, NT), _rows(TT, LW, NT), _whole((16, LW)), _whole((NBLK, BW, BW)), _whole((NBLK, BW, BW))]
        + [ANY] * nb,
        out_shape=[S((T, LW), BF), S((T, LW), BF), S((16, LW), F32), S((NBLK, BW, BW), F32), S((NBLK, BW, BW), F32)]
        + [S(_half_shape(b), b.dtype) for b in bufs],
        scratch_shapes=[pltpu.VMEM((TT + 8, LW), F32), pltpu.VMEM((TT + 8, LW), F32),
                        pltpu.VMEM((TT, LW), F32), pltpu.VMEM((TT, LW), F32), pltpu.VMEM((TT, LW), F32),
                        pltpu.VMEM((8, LW), F32), pltpu.SemaphoreType.DMA((nb,)), pltpu.SemaphoreType.DMA((nb,))],
        compiler_params=_params(),
    )(dy, lxr, lxc, lg, h, h, p_lru, wa4, wx4, wa4T, wx4T, *bufs)


def _ssd_bwd(dyn, xbcr, cv, z, dtr, y, states, cw_ssd, hp_ssd, g_ssd, parts):
    T = dyn.shape[0]
    NC = T // CH
    nq = len(parts)

    def body(*refs):
        dyn_ref, xr_ref, cv_ref, z_ref, dt_ref, y_ref, st_ref, cw_ref, hp_ref, g_ref = refs[:10]
        q_in = refs[10:10 + nq]
        dxbc_ref, dz_ref, ddt_ref, cst_ref, hst_ref, gst_ref = refs[10 + nq:16 + nq]
        q_out = refs[16 + nq:16 + 2 * nq]
        dp, dS, send_sems, recv_sems = refs[16 + 2 * nq:]
        dyn = dyn_ref[...]
        first = pl.program_id(0) == 0
        for phase, step in enumerate((0, NC - 1)):
            @pl.when(pl.program_id(0) == step)
            def _():
                _quad_phase(phase, q_in, q_out, send_sems, recv_sems)

        @pl.when(first)
        def _():
            cst_ref[...] = jnp.zeros_like(cst_ref)
            hst_ref[...] = jnp.zeros_like(hst_ref)
            gst_ref[...] = jnp.zeros_like(gst_ref)
            dp[CH:CH + 8, :] = jnp.zeros((8, XBC), F32)
            dS[...] = jnp.zeros_like(dS)

        cv = cv_ref[...]
        sg, xbc, raw, dtv, A, cs = _ssd_prep(cv, dt_ref, hp_ref)
        csT, dsm, E_x, ds_x, El_rows = _ssd_decays(cs)
        row_i = lax.broadcasted_iota(jnp.int32, (CH, CH), 0)
        col_i = lax.broadcasted_iota(jnp.int32, (CH, CH), 1)
        tril = row_i >= col_i
        first = col_i < HD
        head_of = ((lax.broadcasted_iota(jnp.int32, (DTP, SI), 1) >> 6)
                   == lax.broadcasted_iota(jnp.int32, (DTP, SI), 0)).astype(BF)
        head_ofT = ((lax.broadcasted_iota(jnp.int32, (SI, DTP), 0) >> 6)
                    == lax.broadcasted_iota(jnp.int32, (SI, DTP), 1)).astype(BF)

        def hi_lo(v):
            hi = v.astype(BF)
            return hi, (v - hi.astype(F32)).astype(BF)

        GW = HPG * HD

        def lane_sums(v, g):
            hi, lo = hi_lo(v)
            w = head_ofT[GW * g:GW * (g + 1), :]
            return _mm(hi, w) + _mm(lo, w)

        zz = z_ref[...]
        sz = _sigmoid(zz)
        yv = y_ref[...]
        dgn, dg = _rms_bwd(yv * (zz * sz), g_ref[...], dyn)
        gst_ref[0:1, :] += dg
        dz_ref[...] = (dgn * yv * (sz + zz * sz * (1.0 - sz))).astype(BF)
        dY = dgn * (zz * sz)

        X = xbc[:, 0:SI]
        dsilu = sg + cv * sg * (1.0 - sg)
        dt_x = _per_head_lanes(dtv)
        xs = X * dt_x
        xsd = (xs * ds_x).astype(BF)
        D_x = _per_head_lanes(hp_ref[...])[2:3, :]
        zero = jnp.zeros((CH, DTP), F32)
        dcs_col = zero
        dcs_row = zero
        dds, ddt_col, dD_rows, dcl_rows = zero, zero, zero, zero
        for g in range(NG):
            gs = slice(GW * g, GW * (g + 1))
            Bg = xbc[:, SI + NS * g:SI + NS * (g + 1)].astype(BF)
            Cg = xbc[:, SI + NG * NS + NS * g:SI + NG * NS + NS * (g + 1)].astype(BF)
            G = _mm_nt(Cg, Bg)
            Sg = st_ref[0, gs, :]
            dSe = dS[gs, :]
            dYg = dY[:, gs]
            dcs_col = dcs_col + lane_sums(dYg * (_mm_nt(Cg, Sg) * E_x[:, gs]), g)
            dD_rows = dD_rows + lane_sums(dYg * X[:, gs], g)
            dP = dYg * E_x[:, gs]
            dCg = _mm(dP, Sg)
            dS[gs, :] = _mm_tn(dP, Cg) + _per_head_rows(El_rows, g) * dSe
            t_hi, t_lo = hi_lo(dSe * Sg)
            dcl_rows = dcl_rows + _mm(head_of[:, gs], t_hi) + _mm(head_of[:, gs], t_lo)
            Q = _mm_nt(Bg, dSe)
            dds = dds + lane_sums(Q * xs[:, gs], g)
            dBg = _mm(xsd[:, gs], dSe)
            dG = jnp.zeros((CH, CH), F32)
            dxs_pairs = []
            for jj in range(HPG // 2):
                j = g * (HPG // 2) + jj
                ps = slice(2 * HD * j, 2 * HD * (j + 1))
                xs_pair = xs[:, ps]
                dxs_pair = Q[:, 2 * HD * jj:2 * HD * (jj + 1)] * ds_x[:, ps]
                for e in range(2):
                    h = 2 * j + e
                    Lm = jnp.exp(jnp.where(tril, cs[:, h:h + 1] - csT[h:h + 1, :], -1e30))
                    M = G * Lm
                    dYm = jnp.where(first if e == 0 else ~first, dY[:, ps], 0.0).astype(BF)
                    dM = _mm_nt(dYm, xs_pair)
                    dxs_pair = dxs_pair + _mm_tn(M, dYm)
                    Wm = dM * M
                    dcs_col = dcs_col + jnp.where(col_i == h, jnp.sum(Wm, axis=1, keepdims=True), 0.0)
                    dcs_row = dcs_row + jnp.where(row_i == h, -jnp.sum(Wm, axis=0, keepdims=True), 0.0)
                    dG = dG + dM * Lm
                dp[0:CH, ps] = (D_x[:, ps] * dY[:, ps] + dxs_pair * dt_x[:, ps]) * dsilu[:, ps]
                dxs_pairs.append(dxs_pair)
            ddt_col = ddt_col + lane_sums(jnp.concatenate(dxs_pairs, axis=1) * X[:, gs], g)
            bs = slice(SI + NS * g, SI + NS * (g + 1))
            cs_ = slice(SI + NG * NS + NS * g, SI + NG * NS + NS * (g + 1))
            dp[0:CH, bs] = (dBg + _mm_tn(dG, Cg)) * dsilu[:, bs]
            dp[0:CH, cs_] = (dCg + _mm(dG, Bg)) * dsilu[:, cs_]

        dds = dds * dsm
        dcs_col = dcs_col - dds
        dD = jnp.sum(dD_rows, axis=0, keepdims=True)
        dcl_rows = jnp.sum(dcl_rows, axis=1, keepdims=True) * jnp.exp(csT[:, CH - 1:CH])
        dcs_row = dcs_row + jnp.where(col_i == CH - 1, dcl_rows, 0.0)
        dcs_col = dcs_col + jnp.where(row_i == CH - 1, jnp.sum(dds, axis=0, keepdims=True), 0.0)

        da = _rev_cumsum_rows(dcs_col + dcs_row.T, CH)
        ddt_col = ddt_col + da * A
        hst_ref[1:2, :] += jnp.sum(da * dtv, axis=0, keepdims=True) * A
        hst_ref[2:3, :] += dD
        draw = jnp.where(col_i < NH, ddt_col * _sigmoid(raw), 0.0)
        ddt_ref[...] = draw.astype(BF)
        hst_ref[0:1, :] += jnp.sum(draw, axis=0, keepdims=True)

        dxbc_ref[...] = _conv_bwd(dp, None, xr_ref[...], cw_ref, cst_ref, CH).astype(BF)

    return pl.pallas_call(
        body, name="ssd_bwd", grid=(NC,),
        in_specs=[_rows(CH, SI, NC), _rows(CH, XBC, NC), _rows(CH, XBC, NC), _rows(CH, SI, NC), _rows(CH, DTP, NC),
                  _rows(CH, SI, NC), pl.BlockSpec((1, NH * HD, NS), lambda i: (NC - 1 - i, 0, 0)),
                  _whole((8, XBC)), _whole((8, DTP)), _whole((1, SI))] + [ANY] * nq,
        out_specs=[_rows(CH, XBC, NC), _rows(CH, SI, NC), _rows(CH, DTP, NC), _whole((16, XBC)), _whole((16, DTP)),
                   _whole((8, SI))] + [ANY] * nq,
        out_shape=[S((T, XBC), BF), S((T, SI), BF), S((T, DTP), BF), S((16, XBC), F32), S((16, DTP), F32), S((8, SI), F32)]
        + [S(p.shape, p.dtype) for p in parts],
        scratch_shapes=[pltpu.VMEM((CH + 8, XBC), F32), pltpu.VMEM((NH * HD, NS), F32),
                        pltpu.SemaphoreType.DMA((3 * nq,)), pltpu.SemaphoreType.DMA((3 * nq,))],
        compiler_params=_params(),
    )(dyn, xbcr, cv, z, dtr, y, states, cw_ssd, hp_ssd, g_ssd, *parts)


def _inproj_bwd(dlx, dlg, dz, dxbc, ddt, x, dx1, wcatT, g0, parts):
    T = x.shape[0]
    NT = T // TW
    nq = len(parts)

    def body(*refs):
        dlx_ref, dlg_ref, dz_ref, dxbc_ref, ddt_ref, x_ref, dx1_ref, w_hbm, g_ref = refs[:9]
        q_in = refs[9:9 + nq]
        dx_ref, st_ref = refs[9 + nq:11 + nq]
        q_out = refs[11 + nq:11 + 2 * nq]
        w_vm, sem, send_sems, recv_sems = refs[11 + 2 * nq:]
        _load_once([(w_hbm, w_vm)], sem)
        for phase, step in enumerate((0, NT - 1)):
            @pl.when(pl.program_id(0) == step)
            def _():
                _quad_phase(phase, q_in, q_out, send_sems, recv_sems)

        @pl.when(pl.program_id(0) == 0)
        def _():
            st_ref[...] = jnp.zeros_like(st_ref)

        dh = jnp.dot(dlx_ref[...], w_vm[0:1024, :], preferred_element_type=F32)
        dh = dh + jnp.dot(dlg_ref[...], w_vm[1024:2048, :], preferred_element_type=F32)
        dh = dh + jnp.dot(dz_ref[...], w_vm[2048:3072, :], preferred_element_type=F32)
        dh = dh + jnp.dot(dxbc_ref[...], w_vm[3072:3072 + XBC, :], preferred_element_type=F32)
        dh = dh + jnp.dot(ddt_ref[...], w_vm[3072 + XBC:PC, :], preferred_element_type=F32)
        dx, dg = _rms_bwd(x_ref[...], g_ref[...], dh)
        dx_ref[...] = dx1_ref[...] + dx
        st_ref[0:1, :] += dg

    return pl.pallas_call(
        body, name="inproj_bwd", grid=(NT,),
        in_specs=[_rows(TW, 1024), _rows(TW, 1024), _rows(TW, 1024), _rows(TW, XBC), _rows(TW, DTP), _rows(TW, D),
                  _rows(TW, D), ANY, _whole((1, D))] + [ANY] * nq,
        out_specs=[_rows(TW, D), _whole((8, D))] + [ANY] * nq,
        out_shape=[S((T, D), F32), S((8, D), F32)] + [S(p.shape, p.dtype) for p in parts],
        scratch_shapes=[pltpu.VMEM((PC, D), BF), pltpu.SemaphoreType.DMA((1,)),
                        pltpu.SemaphoreType.DMA((3 * nq,)), pltpu.SemaphoreType.DMA((3 * nq,))],
        compiler_params=_params(),
    )(dlx, dlg, dz, dxbc, ddt, x, dx1, wcatT, g0, *parts)


def _wgrad(name, a, b):
    T, M = a.shape
    N = b.shape[1]
    tk = min(T, 2048 if M <= 1024 else 1024)
    tn = N
    while M * tn * 4 > (6 << 20) and tn % 256 == 0:
        tn //= 2

    def body(a_ref, b_ref, o_ref):
        @pl.when(pl.program_id(1) == 0)
        def _():
            o_ref[...] = jnp.zeros_like(o_ref)

        o_ref[...] += lax.dot_general(a_ref[...], b_ref[...], (((0,), (0,)), ((), ())), preferred_element_type=F32)

    return pl.pallas_call(
        body, name=name, grid=(N // tn, T // tk),
        in_specs=[pl.BlockSpec((tk, M), lambda j, k: (k, 0)), pl.BlockSpec((tk, tn), lambda j, k: (k, j))],
        out_specs=pl.BlockSpec((M, tn), lambda j, k: (0, j)), out_shape=S((M, N), F32),
        compiler_params=_params(2),
    )(a, b)


def _adamw(name, w, g, m, v):
    _, R, C = w.shape

    def body(w_ref, g_ref, m_ref, v_ref, d_ref, nm_ref, nv_ref):
        d_ref[0], nm_ref[0], nv_ref[0] = _adam_math(w_ref[0], g_ref[...], m_ref[0], v_ref[0])

    if R % 8 == 0:
        tr = _row_tile(R, C)
        n_tiles = R // tr
        blk, gblk = pl.BlockSpec((1, tr, C), lambda i: (0, i, 0)), pl.BlockSpec((tr, C), lambda i: (i, 0))
    else:
        tc = 128 * max(k for k in range(1, C // 128 + 1) if C % (128 * k) == 0 and R * 128 * k * 4 <= (5 << 19))
        n_tiles = C // tc
        blk, gblk = pl.BlockSpec((1, R, tc), lambda i: (0, 0, i)), pl.BlockSpec((R, tc), lambda i: (0, i))
    return pl.pallas_call(
        body, name=name, grid=(n_tiles,),
        in_specs=[blk, gblk, blk, blk], out_specs=[blk] * 3,
        out_shape=[S((1, R, C), F32)] * 3, compiler_params=_params(),
    )(w, g, m, v)


def _pair_exchange(name, bufs):
    n = len(bufs)

    def body(*refs):
        for phase in range(2):
            _pair_phase(phase, refs[:n], refs[n:2 * n], refs[2 * n], refs[2 * n + 1])

    return pl.pallas_call(
        body, name=name, in_specs=[ANY] * n, out_specs=[ANY] * n,
        out_shape=[S(_half_shape(b), b.dtype) for b in bufs],
        scratch_shapes=[pltpu.SemaphoreType.DMA((n,)), pltpu.SemaphoreType.DMA((n,))],
    )(*bufs)


def _quad_exchange(bufs):
    n = len(bufs)

    def body(*refs):
        ins, outs = refs[:n], refs[n:2 * n]
        send_sems, recv_sems = refs[2 * n], refs[2 * n + 1]
        x, y, c = _pos()
        me = 2 * x + y
        chips = _other_chips(x, y)
        copies = []
        for k, (src, dst) in enumerate(zip(ins, outs)):
            for j, (cx, cy) in enumerate(chips):
                cp = _remote(src, dst.at[me], send_sems.at[3 * k + j], recv_sems.at[3 * k + j], (cx, cy, c))
                cp.start()
                copies.append(cp)
        for k, (src, dst) in enumerate(zip(ins, outs)):
            for j, (cx, cy) in enumerate(chips):
                blk = dst.at[2 * cx + cy]
                _remote(blk, blk, send_sems.at[3 * k + j], recv_sems.at[3 * k + j], (cx, cy, c)).wait_recv()
        for cp in copies:
            cp.wait_send()

    return pl.pallas_call(
        body, name="quad_exchange", in_specs=[ANY] * n, out_specs=[ANY] * n,
        out_shape=[S((4,) + b.shape, b.dtype) for b in bufs],
        scratch_shapes=[pltpu.SemaphoreType.DMA((3 * n,)), pltpu.SemaphoreType.DMA((3 * n,))],
    )(*bufs)


def _pair_gather(bufs):
    n = len(bufs)

    def body(*refs):
        ins, outs = refs[:n], refs[n:2 * n]
        send_sems, recv_sems = refs[2 * n], refs[2 * n + 1]
        x, y, c = _pos()
        copies = []
        for k, buf in enumerate(outs):
            mine = _half(buf, c, buf.shape[0] // 2)
            cp = _remote(mine, mine, send_sems.at[k], recv_sems.at[k], (x, y, 1 - c))
            cp.start()
            copies.append(cp)
        for k, buf in enumerate(outs):
            theirs = _half(buf, 1 - c, buf.shape[0] // 2)
            _remote(theirs, theirs, send_sems.at[k], recv_sems.at[k], (x, y, 1 - c)).wait_recv()
        for cp in copies:
            cp.wait_send()

    return pl.pallas_call(
        body, name="pair_gather", in_specs=[ANY] * n, out_specs=[ANY] * n,
        out_shape=[S(b.shape, b.dtype) for b in bufs], input_output_aliases={k: k for k in range(n)},
        scratch_shapes=[pltpu.SemaphoreType.DMA((n,)), pltpu.SemaphoreType.DMA((n,))],
    )(*bufs)


def _row_tile(rows, cols, mult=8):
    best = mult
    for t in range(mult, rows + 1, mult):
        if rows % t == 0 and t * cols * 4 <= (1 << 21):
            best = t
    return best


def _add_own_half(name, full, got, c, out_dtype, by_columns):
    hr = got.shape[-2]
    wide = got.shape[-1]
    cols = wide // 4 if by_columns else wide
    tr = _row_tile(hr, wide, 16)
    per = hr // tr

    if by_columns:
        def body(c_ref, a_ref, b_ref, o_ref):
            v = a_ref[...] + b_ref[...]
            for j in range(4):
                o_ref[j] = v[:, j * cols:(j + 1) * cols].astype(out_dtype)

        in_specs = [pl.BlockSpec((tr, wide), lambda i, c_ref: (c_ref[0] * per + i, 0)),
                    pl.BlockSpec((tr, wide), lambda i, c_ref: (i, 0))]
        out_specs = pl.BlockSpec((4, tr, cols), lambda i, c_ref: (0, i, 0))
        grid = (per,)
    else:
        def body(c_ref, a_ref, b_ref, o_ref):
            o_ref[...] = (a_ref[...] + b_ref[...]).astype(out_dtype)

        in_specs = [pl.BlockSpec((1, tr, cols), lambda s, i, c_ref: (s, c_ref[0] * per + i, 0)),
                    pl.BlockSpec((1, tr, cols), lambda s, i, c_ref: (s, i, 0))]
        out_specs = pl.BlockSpec((1, tr, cols), lambda s, i, c_ref: (s, i, 0))
        grid = (4, per)
    return pl.pallas_call(
        body, name=name,
        grid_spec=pltpu.PrefetchScalarGridSpec(num_scalar_prefetch=1, grid=grid, in_specs=in_specs, out_specs=out_specs),
        out_shape=S((4, hr, cols), out_dtype), compiler_params=_params(len(grid)),
    )(jnp.reshape(c, (1,)).astype(jnp.int32), full, got)


def _small_add_own_half(fulls, gots, c):
    n = len(fulls)

    def body(c_ref, *refs):
        for a_ref, b_ref, o_ref in zip(refs[:n], refs[n:2 * n], refs[2 * n:]):
            hr = b_ref.shape[0]
            o_ref[...] = a_ref[pl.ds(pl.multiple_of(c_ref[0] * hr, 8), hr), :] + b_ref[...]

    specs = lambda arrs: [pl.BlockSpec(a.shape, lambda i, c_ref: (0, 0)) for a in arrs]
    return pl.pallas_call(
        body, name="small_pair_add",
        grid_spec=pltpu.PrefetchScalarGridSpec(num_scalar_prefetch=1, grid=(1,), in_specs=specs(fulls) + specs(gots),
                                               out_specs=specs(gots)),
        out_shape=[S(g.shape, F32) for g in gots], compiler_params=_params(),
    )(jnp.reshape(c, (1,)).astype(jnp.int32), *fulls, *gots)


def _small_sum_slots(own, slots, me, c):
    n = len(slots)

    def body(p_ref, *refs):
        own_refs, slot_refs, o_refs = refs[:n], refs[n:5 * n], refs[5 * n:]
        for i, (own_ref, o_ref) in enumerate(zip(own_refs, o_refs)):
            hr = own_ref.shape[0]
            acc = None
            for j in range(4):
                v = jnp.where(p_ref[0] == j, own_ref[...], slot_refs[4 * i + j][0])
                acc = v if acc is None else acc + v
            o_ref[pl.ds(pl.multiple_of(p_ref[1] * hr, 8), hr), :] = acc

    def slot_spec(s, j):
        return pl.BlockSpec((1,) + s.shape[1:], lambda i, p: (jnp.where(p[0] == j, (j + 1) % 4, j), 0, 0))

    outs = [S((2 * s.shape[1], s.shape[2]), F32) for s in slots]
    return pl.pallas_call(
        body, name="small_quad_sum",
        grid_spec=pltpu.PrefetchScalarGridSpec(
            num_scalar_prefetch=1, grid=(1,),
            in_specs=[pl.BlockSpec(o.shape, lambda i, p: (0, 0)) for o in own]
            + [slot_spec(s, j) for s in slots for j in range(4)],
            out_specs=[pl.BlockSpec(o.shape, lambda i, p: (0, 0)) for o in outs]),
        out_shape=outs, compiler_params=_params(),
    )(jnp.stack([me, c]).astype(jnp.int32), *own, *[s for s in slots for _ in range(4)])


def _sum_slots(name, own, slots, me, c):
    _, rows, cols = slots.shape
    tr = _row_tile(rows, cols, 16 if slots.dtype == jnp.bfloat16 else 8)
    per = rows // tr
    three = len(own.shape) == 3

    def body(p_ref, own_ref, s0, s1, s2, s3, o_ref):
        mine = own_ref[0] if three else own_ref[...]
        acc = None
        for j, s_ref in enumerate((s0, s1, s2, s3)):
            v = jnp.where(p_ref[0] == j, mine, s_ref[0]).astype(F32)
            acc = v if acc is None else acc + v
        o_ref[...] = acc

    def slot_spec(j):
        return pl.BlockSpec((1, tr, cols), lambda i, p: (jnp.where(p[0] == j, (j + 1) % 4, j), i, 0))

    own_spec = (pl.BlockSpec((1, tr, cols), lambda i, p: (p[0], i, 0)) if three
                else pl.BlockSpec((tr, cols), lambda i, p: (i, 0)))
    return pl.pallas_call(
        body, name=name,
        grid_spec=pltpu.PrefetchScalarGridSpec(
            num_scalar_prefetch=1, grid=(per,), in_specs=[own_spec] + [slot_spec(j) for j in range(4)],
            out_specs=pl.BlockSpec((tr, cols), lambda i, p: (p[1] * per + i, 0))),
        out_shape=S((2 * rows, cols), F32), compiler_params=_params(),
    )(jnp.stack([me, c]).astype(jnp.int32), own, slots, slots, slots, slots)


BIG = ("w_in", "w_out", "w_gate", "w_up", "w_down")
ROW_PARAMS = (("pre_mix_norm", 0), ("lru_conv_b", 12), ("lru_ba", 13), ("lru_bx", 14), ("lru_lambda", 15),
              ("lru_out_norm", 16), ("ssd_out_norm", 24), ("post_mix_norm", 33), ("pre_ffn_norm", 32), ("post_ffn_norm", 41))
LRU_CONV_ROWS = (8, 12)
LOSS_ROW = 40
HEAD_PARAMS = (("ssd_dt_bias", 0), ("ssd_a_log", 1), ("ssd_d", 2))
SMALL = tuple(n for n, _ in ROW_PARAMS) + ("ssd_conv_b",) + tuple(n for n, _ in HEAD_PARAMS) + (
    "lru_wa", "lru_wx", "lru_conv_w", "ssd_conv_w")


def _diag4(w):
    eye = jnp.eye(4, dtype=w.dtype).reshape(1, 4, 1, 4, 1)
    return (w.reshape(4, 4, BW, 1, BW) * eye).reshape(4, 4 * BW, 4 * BW)


def _adam_math(w, g, m, v):
    mm = ADAM_B1 * m + (1.0 - ADAM_B1) * g
    vv = ADAM_B2 * v + (1.0 - ADAM_B2) * (g * g)
    c1 = 1.0 - ADAM_B1 ** ADAM_STEP
    c2 = 1.0 - ADAM_B2 ** ADAM_STEP
    return -ADAM_LR * ((mm / c1) / (jnp.sqrt(vv / c2) + ADAM_EPS) + ADAM_WD * w), mm, vv


def _adamw_small(rows, cst, hst, dwa, dwx, glcw, gscw, w, m, v):
    def grad_of(name, refs):
        rows_ref, cst_ref, hst_ref, dwa_ref, dwx_ref, glcw_ref, gscw_ref = refs
        for n, r in ROW_PARAMS:
            if n == name:
                return rows_ref[r:r + 1, :]
        for n, r in HEAD_PARAMS:
            if n == name:
                return hst_ref[r:r + 1, 0:NH]
        return {"ssd_conv_b": lambda: cst_ref[4:5, :], "lru_wa": lambda: dwa_ref[...], "lru_wx": lambda: dwx_ref[...],
                "lru_conv_w": lambda: glcw_ref[...], "ssd_conv_w": lambda: gscw_ref[...]}[name]()

    shapes = {n: (w[n].shape[1:] if len(w[n].shape) > 2 else w[n].shape) for n in SMALL}
    flat = lambda d: [d[n].reshape(shapes[n]) for n in SMALL]
    ns = len(SMALL)

    def body(*refs):
        srcs, rest = refs[:7], refs[7:]
        w_refs, m_refs, v_refs = rest[:ns], rest[ns:2 * ns], rest[2 * ns:3 * ns]
        outs = rest[3 * ns:]
        for k, name in enumerate(SMALL):
            g = grad_of(name, srcs)
            d, mm, vv = _adam_math(w_refs[k][...], g, m_refs[k][...], v_refs[k][...])
            outs[4 * k][...] = g
            outs[4 * k + 1][...] = d
            outs[4 * k + 2][...] = mm
            outs[4 * k + 3][...] = vv

    res = pl.pallas_call(
        body, name="adamw_small",
        out_shape=[S(shapes[n], F32) for n in SMALL for _ in range(4)],
        compiler_params=pltpu.CompilerParams(vmem_limit_bytes=VMEM_LIMIT),
    )(rows, cst, hst, dwa, dwx, glcw, gscw, *flat(w), *flat(m), *flat(v))
    return {n: tuple(res[4 * k + i].reshape(w[n].shape) for i in range(4)) for k, n in enumerate(SMALL)}


def _with_own(own, got):
    chip = 2 * lax.axis_index("x") + lax.axis_index("y")
    return jnp.where((jnp.arange(4) == chip).reshape(4, 1, 1), own[None], got)


def _side_by_side(f):
    return f.transpose(1, 0, 2).reshape(f.shape[1], 4 * f.shape[2])


def _stacked(f):
    return f.reshape(4 * f.shape[1], f.shape[2])


def _conv_terms(lru_conv_w, ssd_conv_w):
    conv = jnp.concatenate([lru_conv_w.reshape(-1), ssd_conv_w.reshape(-1)]).astype(F32)
    hi = conv.astype(jnp.bfloat16)
    mid = (conv - hi.astype(F32)).astype(jnp.bfloat16)
    lo = (conv - hi.astype(F32) - mid.astype(F32)).astype(jnp.bfloat16)
    terms = jnp.concatenate([hi, mid, lo])
    rows = -(-terms.shape[0] // (128 * 32)) * 32
    return jnp.pad(terms, (0, rows * 128 - terms.shape[0])).reshape(rows, 128)


def _full_conv_taps(own, got, n_lru, n_ssd):
    n_terms = 3 * (n_lru + n_ssd)
    t3 = _with_own(own, got).reshape(4, -1)[:, :n_terms].reshape(4, 3, -1).astype(F32)
    conv_f = (t3[:, 0] + t3[:, 1]) + t3[:, 2]
    lcw = conv_f[:, :n_lru].reshape(4, CONV_K, -1).transpose(1, 0, 2).reshape(CONV_K, LW)
    scw = conv_f[:, n_lru:].reshape(4, CONV_K, -1).transpose(1, 0, 2).reshape(CONV_K, XBC)
    return lcw, scw


def _step(x, tgt, w_in, lru_conv_w, ssd_conv_w, sp, late):
    c = lax.axis_index("c")
    me = 2 * lax.axis_index("x") + lax.axis_index("y")
    mm = lambda w: w.astype(BF)
    row = lambda v: v.reshape(1, -1).astype(F32)
    g0 = row(sp["pre_mix_norm"])
    first = [w_in.astype(WIRE), _conv_terms(lru_conv_w, ssd_conv_w)]
    h0, *got_first = _prenorm(x, g0, first)
    win_f = _side_by_side(_with_own(first[0], got_first[0]))
    lcw, scw = _full_conv_taps(first[1], got_first[1], lru_conv_w.size, ssd_conv_w.size)
    wcat = jnp.concatenate([mm(win_f), jnp.zeros((D, PC - IN_COLS), BF)], axis=1)
    p_lru = jnp.concatenate([lcw, row(sp["lru_conv_b"]), row(sp["lru_ba"]), row(sp["lru_bx"]), row(sp["lru_lambda"]),
                             row(sp["lru_out_norm"]), jnp.zeros((7, LW), F32)], axis=0)
    wa4, wx4 = mm(_diag4(sp["lru_wa"][0])), mm(_diag4(sp["lru_wx"][0]))
    wa4T, wx4T = wa4.transpose(0, 2, 1), wx4.transpose(0, 2, 1)
    cw_ssd = jnp.concatenate([scw, row(sp["ssd_conv_b"]), jnp.zeros((3, XBC), F32)], axis=0)
    padh = lambda v: jnp.pad(row(v), ((0, 0), (0, DTP - NH)))
    hp_ssd = jnp.concatenate([padh(sp["ssd_dt_bias"]), padh(sp["ssd_a_log"]), padh(sp["ssd_d"]), jnp.zeros((5, DTP), F32)], axis=0)
    g_ssd = row(sp["ssd_out_norm"])
    g_pm, g_pf, g_pff = row(sp["post_mix_norm"]), row(sp["pre_ffn_norm"]), row(sp["post_ffn_norm"])

    h, ylru, lxc, lxr, lg, *got_a = _lru_fwd(h0, wcat, p_lru, wa4, wx4, [late[0], late[3]])
    y, yssd, states, cv, z, xbcr, dtr, *got_b = _ssd_fwd(h0, wcat, cw_ssd, hp_ssd, g_ssd, [late[1], late[2]])
    wout, wd = mm(_stacked(_with_own(late[0], got_a[0]))), mm(_stacked(_with_own(late[3], got_a[1])))
    wg, wu = mm(_side_by_side(_with_own(late[1], got_b[0]))), mm(_side_by_side(_with_own(late[2], got_b[1])))
    mix, x1, h2 = _outproj(ylru, yssd, x, wout, g_pm, g_pf)
    gate, up, act, df, dx2, st_ffn = _ffn_fwd(h2, x1, tgt, wg, wu, wd, g_pff)
    dgate, dup, dh2 = _ffn_bwd(df, gate, up, wd.T, wg.T, wu.T)
    dx1, dmix, dyl, dys, st_mix = _mix_bwd(dh2, x1, dx2, mix, wout.T, g_pf, g_pm)

    dwg = _wgrad("wgrad_gate", h2, dgate)
    dwu = _wgrad("wgrad_up", h2, dup)
    dwd = _wgrad("wgrad_down", act, df)
    dwo = jnp.concatenate([_wgrad("wgrad_out_lru", ylru, dmix), _wgrad("wgrad_out_ssd", yssd, dmix)], axis=0)
    early = [dwo.reshape(4, (LW + SI) // 4, D), dwg, dwu, dwd.reshape(4, DFF // 4, D)]
    dlx, dlg, st_lru, dwa, dwx, *got_early = _lru_bwd(dyl, lxr, lxc, lg, h, p_lru, wa4, wx4, wa4T, wx4T, early)
    part_early = [_add_own_half("pair_add_early%d" % k, b, r, c, WIRE, bc)
                  for k, (b, r, bc) in enumerate(zip(early, got_early, [False, True, True, False]))]
    dxbc, dz, ddt, cst, hst, gst, *slots_early = _ssd_bwd(dys, xbcr, cv, z, dtr, y, states, cw_ssd, hp_ssd, g_ssd,
                                                          part_early)
    red_early = [_sum_slots("quad_sum_early%d" % k, p, s, me, c) for k, (p, s) in enumerate(zip(part_early, slots_early))]

    pin = [_wgrad("wgrad_in_%d" % k, h0, b) for k, b in enumerate((dlx, dlg, dz, dxbc, ddt))]
    dwin = jnp.concatenate(pin[:4] + [pin[4][:, :NH]], axis=1)
    (got_win,) = _pair_exchange("pair_exchange_w_in", [dwin])
    part_win = _add_own_half("pair_add_w_in", dwin, got_win, c, WIRE, True)
    gx, st_in, slots_win = _inproj_bwd(dlx, dlg, dz, dxbc, ddt, x, dx1, wcat.T, g0, [part_win])
    red_win = _sum_slots("quad_sum_w_in", part_win, slots_win, me, c)

    rows = jnp.concatenate([st_in, st_lru, gst, st_mix, st_ffn], axis=0)
    small = [rows, cst, hst, dwa.reshape(NBLK * BW, BW), dwx.reshape(NBLK * BW, BW)]
    part_small = list(_small_add_own_half(small, list(_pair_exchange("pair_exchange_small", small)), c))
    red_small = list(_small_sum_slots(part_small, list(_quad_exchange(part_small)), me, c))
    out = list(_pair_gather([red_win] + red_early + red_small))
    big = dict(zip(("w_in", "w_out", "w_gate", "w_up", "w_down"), out[:5]))
    return gx, big, out[5:]


def kernel(x, pre_mix_norm, w_in, lru_conv_w, lru_conv_b, lru_wa, lru_ba, lru_wx, lru_bx, lru_lambda, lru_out_norm, ssd_conv_w, ssd_conv_b, ssd_dt_bias, ssd_a_log, ssd_d, ssd_out_norm, w_out, post_mix_norm, pre_ffn_norm, w_gate, w_up, w_down, post_ffn_norm, loss_target, m_pre_mix_norm, m_w_in, m_lru_conv_w, m_lru_conv_b, m_lru_wa, m_lru_ba, m_lru_wx, m_lru_bx, m_lru_lambda, m_lru_out_norm, m_ssd_conv_w, m_ssd_conv_b, m_ssd_dt_bias, m_ssd_a_log, m_ssd_d, m_ssd_out_norm, m_w_out, m_post_mix_norm, m_pre_ffn_norm, m_w_gate, m_w_up, m_w_down, m_post_ffn_norm, v_pre_mix_norm, v_w_in, v_lru_conv_w, v_lru_conv_b, v_lru_wa, v_lru_ba, v_lru_wx, v_lru_bx, v_lru_lambda, v_lru_out_norm, v_ssd_conv_w, v_ssd_conv_b, v_ssd_dt_bias, v_ssd_a_log, v_ssd_d, v_ssd_out_norm, v_w_out, v_post_mix_norm, v_pre_ffn_norm, v_w_gate, v_w_up, v_w_down, v_post_ffn_norm):
    args = dict(locals())
    names = list(SMALL) + list(BIG)
    w = {n: args[n] for n in names}
    m = {n: args["m_" + n] for n in names}
    v = {n: args["v_" + n] for n in names}
    chip = 2 * lax.axis_index("x") + lax.axis_index("y")

    late = [a[0].astype(WIRE) for a in (w_out, w_gate, w_up, w_down)]
    gx, red, (rows, cst, hst, dwa, dwx) = _step(x[0], loss_target[0], w_in[0], lru_conv_w[0], ssd_conv_w[0],
                                                {n: w[n] for n in SMALL}, late)
    loss = jnp.sum(rows[LOSS_ROW])

    grads, delta, new_m, new_v = {}, {}, {}, {}
    for n in BIG:
        g = red[n]
        if n in ("w_in", "w_gate", "w_up"):
            t = lambda a: jnp.swapaxes(a, 1, 2)
            gt = g.T
            out = _adamw("adamw_" + n, t(w[n]), gt, t(m[n]), t(v[n]))
            delta[n], new_m[n], new_v[n] = (t(o) for o in out)
            grads[n] = t(gt[None])
        else:
            delta[n], new_m[n], new_v[n] = _adamw("adamw_" + n, w[n], g, m[n], v[n])
            grads[n] = g[None]

    lc, sc = lru_conv_w.shape[-1], ssd_conv_w.shape[-1]
    glcw = lax.dynamic_slice_in_dim(rows[LRU_CONV_ROWS[0]:LRU_CONV_ROWS[1]], chip * lc, lc, axis=1)
    gscw = lax.dynamic_slice_in_dim(cst[0:CONV_K], chip * sc, sc, axis=1)
    res = _adamw_small(rows, cst, hst, dwa.reshape(NBLK, BW, BW), dwx.reshape(NBLK, BW, BW), glcw, gscw,
                       {n: w[n] for n in SMALL}, {n: m[n] for n in SMALL}, {n: v[n] for n in SMALL})
    for n in SMALL:
        grads[n], delta[n], new_m[n], new_v[n] = res[n]

    order = ["pre_mix_norm", "w_in", "lru_conv_w", "lru_conv_b", "lru_wa", "lru_ba", "lru_wx", "lru_bx", "lru_lambda",
             "lru_out_norm", "ssd_conv_w", "ssd_conv_b", "ssd_dt_bias", "ssd_a_log", "ssd_d", "ssd_out_norm", "w_out",
             "post_mix_norm", "pre_ffn_norm", "w_gate", "w_up", "w_down", "post_ffn_norm"]
    return (loss, gx[None], *[grads[n] for n in order], *[delta[n] for n in order],
            *[new_m[n] for n in order], *[new_v[n] for n in order])
```

```python
import functools

import jax
import jax.numpy as jnp
from jax import lax
from jax.experimental import pallas as pl
from jax.experimental.pallas import tpu as pltpu

F32 = jnp.float32
BF = jnp.bfloat16

D = 1024
LW = 1024
NBLK = 16
BW = 64
SI = 1024
NH = 16
HD = 64
NG = 2
HPG = NH // NG
NS = 128
CH = 128
XBC = SI + 2 * NG * NS
DTP = 128
PC = 3 * 1024 + XBC + DTP
DFF = 2816
IN_COLS = 4624
EPS = 1e-6
LRU_C = 8.0
CONV_K = 4
TT = 256
TW = 512
VMEM_LIMIT = 56 * 1024 * 1024

ADAM_LR, ADAM_B1, ADAM_B2, ADAM_EPS, ADAM_WD, ADAM_STEP = 0.001, 0.9, 0.999, 1e-08, 0.01, 10

MESH = pl.DeviceIdType.MESH


def _mm(a, b):
    return jnp.dot(a.astype(BF), b.astype(BF), preferred_element_type=F32)


def _mm_nt(a, b):
    return lax.dot_general(a.astype(BF), b.astype(BF), (((1,), (1,)), ((), ())), preferred_element_type=F32)


def _mm_tn(a, b):
    return lax.dot_general(a.astype(BF), b.astype(BF), (((0,), (0,)), ((), ())), preferred_element_type=F32)


def _sigmoid(x):
    return 0.5 * jnp.tanh(0.5 * x) + 0.5


def _softplus(x):
    return jnp.maximum(x, 0.0) + jnp.log1p(jnp.exp(-jnp.abs(x)))


_GELU_C = 0.7978845608028654
_GELU_K = 0.044715


def _gelu(x):
    t = jnp.tanh(_GELU_C * (x + _GELU_K * x * x * x))
    return 0.5 * x * (1.0 + t)


def _gelu_grad(x):
    t = jnp.tanh(_GELU_C * (x + _GELU_K * x * x * x))
    return 0.5 * (1.0 + t) + 0.5 * x * (1.0 - t * t) * _GELU_C * (1.0 + 3.0 * _GELU_K * x * x)


def _rms_fwd(x, g):
    r = lax.rsqrt(jnp.mean(x * x, axis=-1, keepdims=True) + EPS)
    return x * r * g


def _rms_bwd(x, g, dy):
    r = lax.rsqrt(jnp.mean(x * x, axis=-1, keepdims=True) + EPS)
    xh = x * r
    dxh = dy * g
    dg = jnp.sum(dy * xh, axis=0, keepdims=True)
    dx = r * (dxh - xh * jnp.mean(dxh * xh, axis=-1, keepdims=True))
    return dx, dg


def _sum_all(x):
    return jnp.sum(jnp.sum(x, axis=1, keepdims=True), axis=0, keepdims=True)


def _cumsum_rows(x, n):
    row = lax.broadcasted_iota(jnp.int32, x.shape, 0)
    k = 1
    while k < n:
        x = x + jnp.where(row >= k, pltpu.roll(x, k, 0), 0.0)
        k *= 2
    return x


def _rev_cumsum_rows(x, n):
    row = lax.broadcasted_iota(jnp.int32, x.shape, 0)
    k = 1
    while k < n:
        x = x + jnp.where(row < n - k, pltpu.roll(x, n - k, 0), 0.0)
        k *= 2
    return x


def _load_once(pairs, sem):
    @pl.when(pl.program_id(0) == 0)
    def _():
        for k, (src, dst) in enumerate(pairs):
            pltpu.make_async_copy(src, dst, sem.at[k]).start()
        for k, (src, dst) in enumerate(pairs):
            pltpu.make_async_copy(src, dst, sem.at[k]).wait()


def _params(n_axes=1):
    return pltpu.CompilerParams(dimension_semantics=("arbitrary",) * n_axes, vmem_limit_bytes=VMEM_LIMIT)


def _rows(n, width, rev_of=None):
    if rev_of is None:
        return pl.BlockSpec((n, width), lambda i: (i, 0))
    return pl.BlockSpec((n, width), lambda i: (rev_of - 1 - i, 0))


def _whole(shape):
    nd = len(shape)
    return pl.BlockSpec(shape, lambda i: (0,) * nd)


ANY = pl.BlockSpec(memory_space=pl.ANY)
S = jax.ShapeDtypeStruct
WIRE = jnp.bfloat16


def _pos():
    return lax.axis_index("x"), lax.axis_index("y"), lax.axis_index("c")


def _other_chips(x, y):
    return [(1 - x, y), (x, 1 - y), (1 - x, 1 - y)]


def _remote(src, dst, send_sem, recv_sem, to):
    return pltpu.make_async_remote_copy(src_ref=src, dst_ref=dst, send_sem=send_sem, recv_sem=recv_sem,
                                        device_id=to, device_id_type=MESH)


def _gather_phase(phase, ins, outs, send_sems, recv_sems):
    x, y, c = _pos()
    me = 2 * x + y
    chips = _other_chips(x, y)
    for i, (src, dst) in enumerate(zip(ins, outs)):
        hr = src.shape[0] // 2
        my_half = pl.ds(pl.multiple_of(c * hr, 16), hr)
        sib_half = pl.ds(pl.multiple_of((1 - c) * hr, 16), hr)
        for k, (cx, cy) in enumerate(chips):
            s1, r1 = send_sems.at[6 * i + k], recv_sems.at[6 * i + k]
            s2, r2 = send_sems.at[6 * i + 3 + k], recv_sems.at[6 * i + 3 + k]
            first = lambda: _remote(src.at[my_half, :], dst.at[me, my_half, :], s1, r1, (cx, cy, c))
            landed = dst.at[2 * cx + cy, my_half, :]
            passed = lambda: _remote(landed, landed, s2, r2, (x, y, 1 - c))
            if phase == 0:
                first().start()
            elif phase == 1:
                _remote(landed, landed, s1, r1, (cx, cy, c)).wait_recv()
                passed().start()
            else:
                theirs = dst.at[2 * cx + cy, sib_half, :]
                _remote(theirs, theirs, s2, r2, (x, y, 1 - c)).wait_recv()
                first().wait_send()
                passed().wait_send()


def _half(ref, c, hr):
    sl = pl.ds(pl.multiple_of(c * hr, 8), hr)
    return ref.at[:, sl, :] if len(ref.shape) == 3 else ref.at[sl, :]


def _half_shape(b):
    return b.shape[:-2] + (b.shape[-2] // 2, b.shape[-1])


def _pair_phase(phase, ins, outs, send_sems, recv_sems):
    x, y, c = _pos()
    for k, (src, dst) in enumerate(zip(ins, outs)):
        cp = _remote(_half(src, 1 - c, src.shape[-2] // 2), dst, send_sems.at[k], recv_sems.at[k], (x, y, 1 - c))
        if phase == 0:
            cp.start()
        else:
            cp.wait()


def _quad_phase(phase, ins, outs, send_sems, recv_sems):
    x, y, c = _pos()
    me = 2 * x + y
    for i, (src, dst) in enumerate(zip(ins, outs)):
        for k, (cx, cy) in enumerate(_other_chips(x, y)):
            piece = src.at[2 * cx + cy] if len(src.shape) == 3 else src
            cp = _remote(piece, dst.at[me], send_sems.at[3 * i + k], recv_sems.at[3 * i + k], (cx, cy, c))
            if phase == 0:
                cp.start()
            else:
                got = dst.at[2 * cx + cy]
                _remote(got, got, send_sems.at[3 * i + k], recv_sems.at[3 * i + k], (cx, cy, c)).wait_recv()
                cp.wait_send()


def _prenorm(x, g0, shards):
    T = x.shape[0]
    tt = 2 * TT
    nt = T // tt
    ng = len(shards)

    def body(*refs):
        x_ref, g_ref = refs[:2]
        sh_in = refs[2:2 + ng]
        h0_ref = refs[2 + ng]
        sh_out = refs[3 + ng:3 + 2 * ng]
        send_sems, recv_sems = refs[3 + 2 * ng:]
        for phase, step in enumerate((0, nt // 2, nt - 1)):
            @pl.when(pl.program_id(0) == step)
            def _():
                _gather_phase(phase, sh_in, sh_out, send_sems, recv_sems)

        h0_ref[...] = _rms_fwd(x_ref[...], g_ref[...]).astype(BF)

    return pl.pallas_call(
        body, name="prenorm", grid=(nt,),
        in_specs=[_rows(tt, D), _whole((1, D))] + [ANY] * ng, out_specs=[_rows(tt, D)] + [ANY] * ng,
        out_shape=[S((T, D), BF)] + [S((4,) + s.shape, s.dtype) for s in shards],
        scratch_shapes=[pltpu.SemaphoreType.DMA((6 * ng,)), pltpu.SemaphoreType.DMA((6 * ng,))],
        compiler_params=_params(),
    )(x, g0, *shards)


def _blockdiag_mm(v, w4_ref):
    return jnp.concatenate([_mm(v[:, 256 * j:256 * (j + 1)], w4_ref[j]) for j in range(4)], axis=1)


def _lru_gates(lx, p_ref, wa_ref, wx_ref):
    r = _sigmoid(_blockdiag_mm(lx, wa_ref) + p_ref[5:6, :])
    i = _sigmoid(_blockdiag_mm(lx, wx_ref) + p_ref[6:7, :])
    sp = _softplus(-p_ref[7:8, :])
    la = -LRU_C * r * sp
    a = jnp.exp(la)
    th = jnp.tanh(la)
    mult = jnp.sqrt(-2.0 * th / (1.0 - th))
    return r, i, sp, a, mult


def _conv_from(xp_ref, p_ref, n):
    acc = p_ref[4:5, :] + p_ref[0:1, :] * xp_ref[pl.ds(8 - CONV_K + 1, n), :]
    for k in range(1, CONV_K):
        acc = acc + p_ref[k:k + 1, :] * xp_ref[pl.ds(8 - CONV_K + 1 + k, n), :]
    return acc


def _conv_bwd(dp_ref, dconv, x, p_ref, st_ref, n):
    if dconv is None:
        dconv = dp_ref[0:n, :]
    else:
        dp_ref[0:n, :] = dconv
    acc = None
    for k in range(CONV_K):
        g = dp_ref[pl.ds(CONV_K - 1 - k, n), :]
        acc = p_ref[k:k + 1, :] * g if acc is None else acc + p_ref[k:k + 1, :] * g
        st_ref[k:k + 1, :] += jnp.sum(g * x, axis=0, keepdims=True)
    st_ref[4:5, :] += jnp.sum(dconv, axis=0, keepdims=True)
    dp_ref[n:n + 8, :] = dp_ref[0:8, :]
    return acc


def _lru_fwd(h0, wcat, p_lru, wa4, wx4, shards):
    T = h0.shape[0]
    NT = T // TT
    ng = len(shards)

    def body(*refs):
        h0_ref, w_hbm, p_ref, wa_ref, wx_ref = refs[:5]
        sh_in = refs[5:5 + ng]
        h_ref, y_ref, lxc_ref, lxr_ref, lg_ref = refs[5 + ng:10 + ng]
        sh_out = refs[10 + ng:10 + 2 * ng]
        xp, a_s, u_s, hc, w_vm, wsem, send_sems, recv_sems = refs[10 + 2 * ng:]
        _load_once([(w_hbm.at[:, 0:2 * LW], w_vm)], wsem)
        for phase, step in enumerate((0, NT // 2, NT - 1)):
            @pl.when(pl.program_id(0) == step)
            def _():
                _gather_phase(phase, sh_in, sh_out, send_sems, recv_sems)

        @pl.when(pl.program_id(0) == 0)
        def _():
            xp[0:8, :] = jnp.zeros((8, LW), F32)
            hc[...] = jnp.zeros_like(hc)

        hv = h0_ref[...]
        lxr = jnp.dot(hv, w_vm[:, 0:LW], preferred_element_type=F32)
        lxr_ref[...] = lxr
        lg_ref[...] = jnp.dot(hv, w_vm[:, LW:2 * LW], preferred_element_type=F32)
        xp[8:8 + TT, :] = lxr
        lx = _conv_from(xp, p_ref, TT)
        lxc_ref[...] = lx
        xp[0:8, :] = xp[TT:TT + 8, :]
        r, i, sp, a, mult = _lru_gates(lx, p_ref, wa_ref, wx_ref)
        a_s[...] = a
        u_s[...] = mult * (i * lx)

        def step(t, h):
            h = a_s[pl.ds(t, 1), :] * h + u_s[pl.ds(t, 1), :]
            h_ref[pl.ds(t, 1), :] = h
            return h

        hc[0:1, :] = lax.fori_loop(0, TT, step, hc[0:1, :], unroll=8)
        gated = h_ref[...] * _gelu(lg_ref[...])
        y_ref[...] = _rms_fwd(gated, p_ref[8:9, :]).astype(BF)

    return pl.pallas_call(
        body, name="lru_fwd", grid=(NT,),
        in_specs=[_rows(TT, D), ANY, _whole((16, LW)), _whole((4, 256, 256)), _whole((4, 256, 256))] + [ANY] * ng,
        out_specs=[_rows(TT, LW), _rows(TT, LW), _rows(TT, LW), _rows(TT, LW), _rows(TT, LW)] + [ANY] * ng,
        out_shape=[S((T, LW), F32), S((T, LW), BF), S((T, LW), F32), S((T, LW), F32), S((T, LW), F32)]
        + [S((4,) + s.shape, s.dtype) for s in shards],
        scratch_shapes=[pltpu.VMEM((TT + 8, LW), F32), pltpu.VMEM((TT, LW), F32), pltpu.VMEM((TT, LW), F32),
                        pltpu.VMEM((8, LW), F32), pltpu.VMEM((D, 2 * LW), BF), pltpu.SemaphoreType.DMA((1,)),
                        pltpu.SemaphoreType.DMA((6 * ng,)), pltpu.SemaphoreType.DMA((6 * ng,))],
        compiler_params=_params(),
    )(h0, wcat, p_lru, wa4, wx4, *shards)


def _ssd_prep(cv, dt_ref, hp_ref):
    sg = _sigmoid(cv)
    xbc = cv * sg
    lane = lax.broadcasted_iota(jnp.int32, (CH, DTP), 1)
    raw = dt_ref[...] + hp_ref[0:1, :]
    dtv = jnp.where(lane < NH, _softplus(raw), 0.0)
    A = jnp.where(lane[0:1, :] < NH, -jnp.exp(hp_ref[1:2, :]), 0.0)
    cs = _cumsum_rows(dtv * A, CH)
    return sg, xbc, raw, dtv, A, cs


def _per_head_lanes(v):
    r = v.shape[0]
    first = lax.broadcasted_iota(jnp.int32, (r, 2 * HD), 1) < HD
    pairs = [jnp.where(first, jnp.broadcast_to(v[:, 2 * j:2 * j + 1], (r, 2 * HD)),
                       jnp.broadcast_to(v[:, 2 * j + 1:2 * j + 2], (r, 2 * HD))) for j in range(NH // 2)]
    return jnp.concatenate(pairs, axis=1)


def _per_head_rows(col, g):
    return jnp.concatenate([jnp.broadcast_to(col[g * HPG + k:g * HPG + k + 1, :], (HD, NS)) for k in range(HPG)], axis=0)


def _ssd_decays(cs):
    csT = cs.T
    cl = cs[CH - 1:CH, :]
    E_x = _per_head_lanes(jnp.exp(cs))
    dsm = jnp.exp(cl - cs)
    ds_x = _per_head_lanes(dsm)
    El_rows = jnp.broadcast_to(jnp.exp(csT[0:NH, CH - 1:CH]), (NH, NS))
    return csT, dsm, E_x, ds_x, El_rows


def _ssd_fwd(h0, wcat, cw_ssd, hp_ssd, g_ssd, shards):
    T = h0.shape[0]
    NC = T // CH
    ng = len(shards)
    c0 = 2 * LW

    def body(*refs):
        h0_ref, w_hbm, cw_ref, hp_ref, g_ref = refs[:5]
        sh_in = refs[5:5 + ng]
        y_ref, yn_ref, st_ref, cv_ref, z_ref, xr_ref, dt_ref = refs[5 + ng:12 + ng]
        sh_out = refs[12 + ng:12 + 2 * ng]
        xp, st, w_vm, wsem, send_sems, recv_sems = refs[12 + 2 * ng:]
        _load_once([(w_hbm.at[:, c0:PC], w_vm)], wsem)
        for phase, step in enumerate((0, NC // 2, NC - 1)):
            @pl.when(pl.program_id(0) == step)
            def _():
                _gather_phase(phase, sh_in, sh_out, send_sems, recv_sems)

        @pl.when(pl.program_id(0) == 0)
        def _():
            xp[0:8, :] = jnp.zeros((8, XBC), F32)
            st[...] = jnp.zeros_like(st)

        hv = h0_ref[...]
        z_ref[...] = jnp.dot(hv, w_vm[:, 0:SI], preferred_element_type=F32)
        xraw = jnp.dot(hv, w_vm[:, SI:SI + XBC], preferred_element_type=F32)
        xr_ref[...] = xraw
        dt_ref[...] = jnp.dot(hv, w_vm[:, SI + XBC:SI + XBC + DTP], preferred_element_type=F32)
        xp[8:8 + CH, :] = xraw
        cv = _conv_from(xp, cw_ref, CH)
        cv_ref[...] = cv
        sg, xbc, raw, dtv, A, cs = _ssd_prep(cv, dt_ref, hp_ref)
        xp[0:8, :] = xp[CH:CH + 8, :]
        st_ref[0] = st[...]
        csT, dsm, E_x, ds_x, El_rows = _ssd_decays(cs)
        X = xbc[:, 0:SI]
        xs = X * _per_head_lanes(dtv)
        xsd = (xs * ds_x).astype(BF)
        DX = _per_head_lanes(hp_ref[...])[2:3, :] * X
        tril = lax.broadcasted_iota(jnp.int32, (CH, CH), 0) >= lax.broadcasted_iota(jnp.int32, (CH, CH), 1)
        first = lax.broadcasted_iota(jnp.int32, (CH, 2 * HD), 1) < HD
        GW = HPG * HD
        for g in range(NG):
            Bg = xbc[:, SI + NS * g:SI + NS * (g + 1)].astype(BF)
            Cg = xbc[:, SI + NG * NS + NS * g:SI + NG * NS + NS * (g + 1)].astype(BF)
            G = _mm_nt(Cg, Bg)
            Sg = st[GW * g:GW * (g + 1), :]
            Yo = _mm_nt(Cg, Sg) * E_x[:, GW * g:GW * (g + 1)]
            st[GW * g:GW * (g + 1), :] = _per_head_rows(El_rows, g) * Sg + _mm_tn(xsd[:, GW * g:GW * (g + 1)], Bg)
            for jj in range(HPG // 2):
                j = g * (HPG // 2) + jj
                ps = slice(2 * HD * j, 2 * HD * (j + 1))
                xs_pair = xs[:, ps]
                acc = Yo[:, 2 * HD * jj:2 * HD * (jj + 1)] + DX[:, ps]
                for e in range(2):
                    h = 2 * j + e
                    Lm = jnp.exp(jnp.where(tril, cs[:, h:h + 1] - csT[h:h + 1, :], -1e30))
                    acc = acc + _mm(G * Lm, jnp.where(first if e == 0 else ~first, xs_pair, 0.0))
                y_ref[:, ps] = acc
        zz = z_ref[...]
        gated = y_ref[...] * (zz * _sigmoid(zz))
        yn_ref[...] = _rms_fwd(gated, g_ref[...]).astype(BF)

    return pl.pallas_call(
        body, name="ssd_fwd", grid=(NC,),
        in_specs=[_rows(CH, D), ANY, _whole((8, XBC)), _whole((8, DTP)), _whole((1, SI))] + [ANY] * ng,
        out_specs=[_rows(CH, SI), _rows(CH, SI), pl.BlockSpec((1, NH * HD, NS), lambda i: (i, 0, 0)), _rows(CH, XBC),
                   _rows(CH, SI), _rows(CH, XBC), _rows(CH, DTP)] + [ANY] * ng,
        out_shape=[S((T, SI), F32), S((T, SI), BF), S((NC, NH * HD, NS), F32), S((T, XBC), F32),
                   S((T, SI), F32), S((T, XBC), F32), S((T, DTP), F32)] + [S((4,) + s.shape, s.dtype) for s in shards],
        scratch_shapes=[pltpu.VMEM((CH + 8, XBC), F32), pltpu.VMEM((NH * HD, NS), F32),
                        pltpu.VMEM((D, PC - c0), BF), pltpu.SemaphoreType.DMA((1,)),
                        pltpu.SemaphoreType.DMA((6 * ng,)), pltpu.SemaphoreType.DMA((6 * ng,))],
        compiler_params=_params(),
    )(h0, wcat, cw_ssd, hp_ssd, g_ssd, *shards)


def _outproj(ylru, yssd, x, wout, g_pm, g_pf):
    T = x.shape[0]

    def body(yl_ref, ys_ref, x_ref, w_hbm, gpm_ref, gpf_ref, mix_ref, x1_ref, h2_ref, w_vm, sem):
        _load_once([(w_hbm, w_vm)], sem)
        mix = (jnp.dot(yl_ref[...], w_vm[0:LW, :], preferred_element_type=F32)
               + jnp.dot(ys_ref[...], w_vm[LW:LW + SI, :], preferred_element_type=F32))
        mix_ref[...] = mix
        x1 = x_ref[...] + _rms_fwd(mix, gpm_ref[...])
        x1_ref[...] = x1
        h2_ref[...] = _rms_fwd(x1, gpf_ref[...]).astype(BF)

    return pl.pallas_call(
        body, name="outproj", grid=(T // TW,),
        in_specs=[_rows(TW, LW), _rows(TW, SI), _rows(TW, D), ANY, _whole((1, D)), _whole((1, D))],
        out_specs=[_rows(TW, D), _rows(TW, D), _rows(TW, D)],
        out_shape=[S((T, D), F32), S((T, D), F32), S((T, D), BF)],
        scratch_shapes=[pltpu.VMEM((LW + SI, D), BF), pltpu.SemaphoreType.DMA((1,))],
        compiler_params=_params(),
    )(ylru, yssd, x, wout, g_pm, g_pf)


def _ffn_fwd(h2, x1, tgt, wg, wu, wd, g_pff):
    T = x1.shape[0]

    def body(h2_ref, x1_ref, t_ref, wg_hbm, wu_hbm, wd_hbm, g_ref,
             gate_ref, up_ref, act_ref, df_ref, dx2_ref, st_ref, wg_vm, wu_vm, wd_vm, sem):
        _load_once([(wg_hbm, wg_vm), (wu_hbm, wu_vm), (wd_hbm, wd_vm)], sem)

        @pl.when(pl.program_id(0) == 0)
        def _():
            st_ref[...] = jnp.zeros_like(st_ref)

        h2 = h2_ref[...]
        gate = jnp.dot(h2, wg_vm[...], preferred_element_type=F32)
        up = jnp.dot(h2, wu_vm[...], preferred_element_type=F32)
        gate_ref[...] = gate
        up_ref[...] = up
        act = (gate * _sigmoid(gate) * up).astype(BF)
        act_ref[...] = act
        f = jnp.dot(act, wd_vm[...], preferred_element_type=F32)
        g = g_ref[...]
        x2 = x1_ref[...] + _rms_fwd(f, g)
        err = x2 - t_ref[...]
        st_ref[0:1, :] += 0.5 * jnp.sum(err * err, axis=0, keepdims=True) * (1.0 / D)
        dx2 = err * (1.0 / D)
        dx2_ref[...] = dx2
        df, dg = _rms_bwd(f, g, dx2)
        df_ref[...] = df.astype(BF)
        st_ref[1:2, :] += dg

    return pl.pallas_call(
        body, name="ffn_fwd", grid=(T // TT,),
        in_specs=[_rows(TT, D), _rows(TT, D), _rows(TT, D), ANY, ANY, ANY, _whole((1, D))],
        out_specs=[_rows(TT, DFF), _rows(TT, DFF), _rows(TT, DFF), _rows(TT, D), _rows(TT, D), _whole((8, D))],
        out_shape=[S((T, DFF), F32), S((T, DFF), F32), S((T, DFF), BF), S((T, D), BF), S((T, D), F32), S((8, D), F32)],
        scratch_shapes=[pltpu.VMEM((D, DFF), BF), pltpu.VMEM((D, DFF), BF), pltpu.VMEM((DFF, D), BF),
                        pltpu.SemaphoreType.DMA((3,))],
        compiler_params=_params(),
    )(h2, x1, tgt, wg, wu, wd, g_pff)


def _ffn_bwd(df, gate, up, wdT, wgT, wuT):
    T = df.shape[0]

    def body(df_ref, gate_ref, up_ref, wd_hbm, wg_hbm, wu_hbm, dgate_ref, dup_ref, dh2_ref, wd_vm, wg_vm, wu_vm, sem):
        _load_once([(wd_hbm, wd_vm), (wg_hbm, wg_vm), (wu_hbm, wu_vm)], sem)
        dact = jnp.dot(df_ref[...], wd_vm[...], preferred_element_type=F32)
        gate = gate_ref[...]
        s = _sigmoid(gate)
        dup = (dact * (gate * s)).astype(BF)
        dgate = (dact * up_ref[...] * (s + gate * s * (1.0 - s))).astype(BF)
        dup_ref[...] = dup
        dgate_ref[...] = dgate
        dh2_ref[...] = (jnp.dot(dgate, wg_vm[...], preferred_element_type=F32)
                        + jnp.dot(dup, wu_vm[...], preferred_element_type=F32))

    return pl.pallas_call(
        body, name="ffn_bwd", grid=(T // TT,),
        in_specs=[_rows(TT, D), _rows(TT, DFF), _rows(TT, DFF), ANY, ANY, ANY],
        out_specs=[_rows(TT, DFF), _rows(TT, DFF), _rows(TT, D)],
        out_shape=[S((T, DFF), BF), S((T, DFF), BF), S((T, D), F32)],
        scratch_shapes=[pltpu.VMEM((D, DFF), BF), pltpu.VMEM((DFF, D), BF), pltpu.VMEM((DFF, D), BF),
                        pltpu.SemaphoreType.DMA((3,))],
        compiler_params=_params(),
    )(df, gate, up, wdT, wgT, wuT)


def _mix_bwd(dh2, x1, dx2, mix, woutT, g_pf, g_pm):
    T = x1.shape[0]

    def body(dh2_ref, x1_ref, dx2_ref, mix_ref, w_hbm, gpf_ref, gpm_ref,
             dx1_ref, dmix_ref, dyl_ref, dys_ref, st_ref, w_vm, sem):
        _load_once([(w_hbm, w_vm)], sem)

        @pl.when(pl.program_id(0) == 0)
        def _():
            st_ref[...] = jnp.zeros_like(st_ref)

        dxa, dgpf = _rms_bwd(x1_ref[...], gpf_ref[...], dh2_ref[...])
        dx1 = dx2_ref[...] + dxa
        dx1_ref[...] = dx1
        dmix, dgpm = _rms_bwd(mix_ref[...], gpm_ref[...], dx1)
        dmix = dmix.astype(BF)
        dmix_ref[...] = dmix
        st_ref[0:1, :] += dgpf
        st_ref[1:2, :] += dgpm
        dyl_ref[...] = jnp.dot(dmix, w_vm[:, 0:LW], preferred_element_type=F32)
        dys_ref[...] = jnp.dot(dmix, w_vm[:, LW:LW + SI], preferred_element_type=F32)

    return pl.pallas_call(
        body, name="mix_bwd", grid=(T // TW,),
        in_specs=[_rows(TW, D), _rows(TW, D), _rows(TW, D), _rows(TW, D), ANY, _whole((1, D)), _whole((1, D))],
        out_specs=[_rows(TW, D), _rows(TW, D), _rows(TW, LW), _rows(TW, SI), _whole((8, D))],
        out_shape=[S((T, D), F32), S((T, D), BF), S((T, LW), F32), S((T, SI), F32), S((8, D), F32)],
        scratch_shapes=[pltpu.VMEM((D, LW + SI), BF), pltpu.SemaphoreType.DMA((1,))],
        compiler_params=_params(),
    )(dh2, x1, dx2, mix, woutT, g_pf, g_pm)


def _halo(width, n_tiles, tile):
    per = tile // 8
    return pl.BlockSpec((8, width), lambda i: (jnp.maximum((n_tiles - 1 - i) * per - 1, 0), 0))


def _lru_bwd(dy, lxr, lxc, lg, h, p_lru, wa4, wx4, wa4T, wx4T, bufs):
    T = dy.shape[0]
    NT = T // TT
    nb = len(bufs)

    def body(*refs):
        dy_ref, lxr_ref, lxc_ref, lg_ref, h_ref, hh_ref, p_ref, wa_ref, wx_ref, waT_ref, wxT_ref = refs[:11]
        b_in = refs[11:11 + nb]
        dlx_ref, dlg_ref, st_ref, dwa_ref, dwx_ref = refs[11 + nb:16 + nb]
        b_out = refs[16 + nb:16 + 2 * nb]
        hp, dp, a_s, d_s, g_s, cc, send_sems, recv_sems = refs[16 + 2 * nb:]
        for phase, step in enumerate((0, NT - 1)):
            @pl.when(pl.program_id(0) == step)
            def _():
                _pair_phase(phase, b_in, b_out, send_sems, recv_sems)

        dy = dy_ref[...]
        first = pl.program_id(0) == 0
        top = pl.program_id(0) == NT - 1

        @pl.when(first)
        def _():
            st_ref[...] = jnp.zeros_like(st_ref)
            dwa_ref[...] = jnp.zeros_like(dwa_ref)
            dwx_ref[...] = jnp.zeros_like(dwx_ref)
            dp[TT:TT + 8, :] = jnp.zeros((8, LW), F32)
            cc[...] = jnp.zeros_like(cc)

        hp[0:8, :] = hh_ref[...] * jnp.where(top, 0.0, 1.0)
        hp[8:8 + TT, :] = h_ref[...]
        lx = lxc_ref[...]
        r, i, sp, a, mult = _lru_gates(lx, p_ref, wa_ref, wx_ref)

        lg = lg_ref[...]
        hcur = h_ref[...]
        ge = _gelu(lg)
        dgated, dgn = _rms_bwd(hcur * ge, p_ref[8:9, :], dy)
        st_ref[8:9, :] += dgn
        dlg_ref[...] = (dgated * hcur * _gelu_grad(lg)).astype(BF)
        a_s[...] = a
        d_s[...] = dgated * ge

        def step(k, c):
            t = TT - 1 - k
            g = d_s[pl.ds(t, 1), :] + c
            g_s[pl.ds(t, 1), :] = g
            return a_s[pl.ds(t, 1), :] * g

        cc[0:1, :] = lax.fori_loop(0, TT, step, cc[0:1, :], unroll=8)
        gt = g_s[...]
        da = gt * hp[pl.ds(7, TT), :]
        dmult = gt * i * lx
        di = gt * mult * lx
        dlxc = gt * mult * i
        dla = da * a - dmult * (a * a) / mult
        dr = dla * (-LRU_C * sp)
        st_ref[7:8, :] += jnp.sum(dla * (-LRU_C * r), axis=0, keepdims=True) * (-_sigmoid(-p_ref[7:8, :]))
        dzr = dr * r * (1.0 - r)
        dzi = di * i * (1.0 - i)
        st_ref[5:6, :] += jnp.sum(dzr, axis=0, keepdims=True)
        st_ref[6:7, :] += jnp.sum(dzi, axis=0, keepdims=True)
        dlxc = dlxc + _blockdiag_mm(dzr, waT_ref) + _blockdiag_mm(dzi, wxT_ref)
        for j in range(4):
            sl = slice(256 * j, 256 * (j + 1))
            pa = _mm_tn(lx[:, sl], dzr[:, sl])
            px = _mm_tn(lx[:, sl], dzi[:, sl])
            for b in range(4):
                bs = slice(BW * b, BW * (b + 1))
                dwa_ref[4 * j + b] += pa[bs, bs]
                dwx_ref[4 * j + b] += px[bs, bs]
        dlx_ref[...] = _conv_bwd(dp, dlxc, lxr_ref[...], p_ref, st_ref, TT).astype(BF)

    w4 = _whole((4, 256, 256))
    return pl.pallas_call(
        body, name="lru_bwd", grid=(NT,),
        in_specs=[_rows(TT, LW, NT), _rows(TT, LW, NT), _rows(TT, LW, NT), _rows(TT, LW, NT), _rows(TT, LW, NT),
                  _halo(LW, NT, TT), _whole((16, LW)), w4, w4, w4, w4] + [ANY] * nb,
        out_specs=[_rows(TT, LW, NT), _rows(TT, LW, NT), _whole((16, LW)), _whole((NBLK, BW, BW)), _whole((NBLK, BW, BW))]
        + [ANY] * nb,
        out_shape=[S((T, LW), BF), S((T, LW), BF), S((16, LW), F32), S((NBLK, BW, BW), F32), S((NBLK, BW, BW), F32)]
        + [S(_half_shape(b), b.dtype) for b in bufs],
        scratch_shapes=[pltpu.VMEM((TT + 8, LW), F32), pltpu.VMEM((TT + 8, LW), F32),
                        pltpu.VMEM((TT, LW), F32), pltpu.VMEM((TT, LW), F32), pltpu.VMEM((TT, LW), F32),
                        pltpu.VMEM((8, LW), F32), pltpu.SemaphoreType.DMA((nb,)), pltpu.SemaphoreType.DMA((nb,))],
        compiler_params=_params(),
    )(dy, lxr, lxc, lg, h, h, p_lru, wa4, wx4, wa4T, wx4T, *bufs)


def _ssd_bwd(dyn, xbcr, cv, z, dtr, y, states, cw_ssd, hp_ssd, g_ssd, parts):
    T = dyn.shape[0]
    NC = T // CH
    nq = len(parts)

    def body(*refs):
        dyn_ref, xr_ref, cv_ref, z_ref, dt_ref, y_ref, st_ref, cw_ref, hp_ref, g_ref = refs[:10]
        q_in = refs[10:10 + nq]
        dxbc_ref, dz_ref, ddt_ref, cst_ref, hst_ref, gst_ref = refs[10 + nq:16 + nq]
        q_out = refs[16 + nq:16 + 2 * nq]
        dp, dS, send_sems, recv_sems = refs[16 + 2 * nq:]
        dyn = dyn_ref[...]
        first = pl.program_id(0) == 0
        for phase, step in enumerate((0, NC - 1)):
            @pl.when(pl.program_id(0) == step)
            def _():
                _quad_phase(phase, q_in, q_out, send_sems, recv_sems)

        @pl.when(first)
        def _():
            cst_ref[...] = jnp.zeros_like(cst_ref)
            hst_ref[...] = jnp.zeros_like(hst_ref)
            gst_ref[...] = jnp.zeros_like(gst_ref)
            dp[CH:CH + 8, :] = jnp.zeros((8, XBC), F32)
            dS[...] = jnp.zeros_like(dS)

        cv = cv_ref[...]
        sg, xbc, raw, dtv, A, cs = _ssd_prep(cv, dt_ref, hp_ref)
        csT, dsm, E_x, ds_x, El_rows = _ssd_decays(cs)
        row_i = lax.broadcasted_iota(jnp.int32, (CH, CH), 0)
        col_i = lax.broadcasted_iota(jnp.int32, (CH, CH), 1)
        tril = row_i >= col_i
        first = col_i < HD
        head_of = ((lax.broadcasted_iota(jnp.int32, (DTP, SI), 1) >> 6)
                   == lax.broadcasted_iota(jnp.int32, (DTP, SI), 0)).astype(BF)
        head_ofT = ((lax.broadcasted_iota(jnp.int32, (SI, DTP), 0) >> 6)
                    == lax.broadcasted_iota(jnp.int32, (SI, DTP), 1)).astype(BF)

        def hi_lo(v):
            hi = v.astype(BF)
            return hi, (v - hi.astype(F32)).astype(BF)

        GW = HPG * HD

        def lane_sums(v, g):
            hi, lo = hi_lo(v)
            w = head_ofT[GW * g:GW * (g + 1), :]
            return _mm(hi, w) + _mm(lo, w)

        zz = z_ref[...]
        sz = _sigmoid(zz)
        yv = y_ref[...]
        dgn, dg = _rms_bwd(yv * (zz * sz), g_ref[...], dyn)
        gst_ref[0:1, :] += dg
        dz_ref[...] = (dgn * yv * (sz + zz * sz * (1.0 - sz))).astype(BF)
        dY = dgn * (zz * sz)

        X = xbc[:, 0:SI]
        dsilu = sg + cv * sg * (1.0 - sg)
        dt_x = _per_head_lanes(dtv)
        xs = X * dt_x
        xsd = (xs * ds_x).astype(BF)
        D_x = _per_head_lanes(hp_ref[...])[2:3, :]
        zero = jnp.zeros((CH, DTP), F32)
        dcs_col = zero
        dcs_row = zero
        dds, ddt_col, dD_rows, dcl_rows = zero, zero, zero, zero
        for g in range(NG):
            gs = slice(GW * g, GW * (g + 1))
            Bg = xbc[:, SI + NS * g:SI + NS * (g + 1)].astype(BF)
            Cg = xbc[:, SI + NG * NS + NS * g:SI + NG * NS + NS * (g + 1)].astype(BF)
            G = _mm_nt(Cg, Bg)
            Sg = st_ref[0, gs, :]
            dSe = dS[gs, :]
            dYg = dY[:, gs]
            dcs_col = dcs_col + lane_sums(dYg * (_mm_nt(Cg, Sg) * E_x[:, gs]), g)
            dD_rows = dD_rows + lane_sums(dYg * X[:, gs], g)
            dP = dYg * E_x[:, gs]
            dCg = _mm(dP, Sg)
            dS[gs, :] = _mm_tn(dP, Cg) + _per_head_rows(El_rows, g) * dSe
            t_hi, t_lo = hi_lo(dSe * Sg)
            dcl_rows = dcl_rows + _mm(head_of[:, gs], t_hi) + _mm(head_of[:, gs], t_lo)
            Q = _mm_nt(Bg, dSe)
            dds = dds + lane_sums(Q * xs[:, gs], g)
            dBg = _mm(xsd[:, gs], dSe)
            dG = jnp.zeros((CH, CH), F32)
            dxs_pairs = []
            for jj in range(HPG // 2):
                j = g * (HPG // 2) + jj
                ps = slice(2 * HD * j, 2 * HD * (j + 1))
                xs_pair = xs[:, ps]
                dxs_pair = Q[:, 2 * HD * jj:2 * HD * (jj + 1)] * ds_x[:, ps]
                for e in range(2):
                    h = 2 * j + e
                    Lm = jnp.exp(jnp.where(tril, cs[:, h:h + 1] - csT[h:h + 1, :], -1e30))
                    M = G * Lm
                    dYm = jnp.where(first if e == 0 else ~first, dY[:, ps], 0.0).astype(BF)
                    dM = _mm_nt(dYm, xs_pair)
                    dxs_pair = dxs_pair + _mm_tn(M, dYm)
                    Wm = dM * M
                    dcs_col = dcs_col + jnp.where(col_i == h, jnp.sum(Wm, axis=1, keepdims=True), 0.0)
                    dcs_row = dcs_row + jnp.where(row_i == h, -jnp.sum(Wm, axis=0, keepdims=True), 0.0)
                    dG = dG + dM * Lm
                dp[0:CH, ps] = (D_x[:, ps] * dY[:, ps] + dxs_pair * dt_x[:, ps]) * dsilu[:, ps]
                dxs_pairs.append(dxs_pair)
            ddt_col = ddt_col + lane_sums(jnp.concatenate(dxs_pairs, axis=1) * X[:, gs], g)
            bs = slice(SI + NS * g, SI + NS * (g + 1))
            cs_ = slice(SI + NG * NS + NS * g, SI + NG * NS + NS * (g + 1))
            dp[0:CH, bs] = (dBg + _mm_tn(dG, Cg)) * dsilu[:, bs]
            dp[0:CH, cs_] = (dCg + _mm(dG, Bg)) * dsilu[:, cs_]

        dds = dds * dsm
        dcs_col = dcs_col - dds
        dD = jnp.sum(dD_rows, axis=0, keepdims=True)
        dcl_rows = jnp.sum(dcl_rows, axis=1, keepdims=True) * jnp.exp(csT[:, CH - 1:CH])
        dcs_row = dcs_row + jnp.where(col_i == CH - 1, dcl_rows, 0.0)
        dcs_col = dcs_col + jnp.where(row_i == CH - 1, jnp.sum(dds, axis=0, keepdims=True), 0.0)

        da = _rev_cumsum_rows(dcs_col + dcs_row.T, CH)
        ddt_col = ddt_col + da * A
        hst_ref[1:2, :] += jnp.sum(da * dtv, axis=0, keepdims=True) * A
        hst_ref[2:3, :] += dD
        draw = jnp.where(col_i < NH, ddt_col * _sigmoid(raw), 0.0)
        ddt_ref[...] = draw.astype(BF)
        hst_ref[0:1, :] += jnp.sum(draw, axis=0, keepdims=True)

        dxbc_ref[...] = _conv_bwd(dp, None, xr_ref[...], cw_ref, cst_ref, CH).astype(BF)

    return pl.pallas_call(
        body, name="ssd_bwd", grid=(NC,),
        in_specs=[_rows(CH, SI, NC), _rows(CH, XBC, NC), _rows(CH, XBC, NC), _rows(CH, SI, NC), _rows(CH, DTP, NC),
                  _rows(CH, SI, NC), pl.BlockSpec((1, NH * HD, NS), lambda i: (NC - 1 - i, 0, 0)),
                  _whole((8, XBC)), _whole((8, DTP)), _whole((1, SI))] + [ANY] * nq,
        out_specs=[_rows(CH, XBC, NC), _rows(CH, SI, NC), _rows(CH, DTP, NC), _whole((16, XBC)), _whole((16, DTP)),
                   _whole((8, SI))] + [ANY] * nq,
        out_shape=[S((T, XBC), BF), S((T, SI), BF), S((T, DTP), BF), S((16, XBC), F32), S((16, DTP), F32), S((8, SI), F32)]
        + [S(p.shape, p.dtype) for p in parts],
        scratch_shapes=[pltpu.VMEM((CH + 8, XBC), F32), pltpu.VMEM((NH * HD, NS), F32),
                        pltpu.SemaphoreType.DMA((3 * nq,)), pltpu.SemaphoreType.DMA((3 * nq,))],
        compiler_params=_params(),
    )(dyn, xbcr, cv, z, dtr, y, states, cw_ssd, hp_ssd, g_ssd, *parts)


def _inproj_bwd(dlx, dlg, dz, dxbc, ddt, x, dx1, wcatT, g0, parts):
    T = x.shape[0]
    NT = T // TW
    nq = len(parts)

    def body(*refs):
        dlx_ref, dlg_ref, dz_ref, dxbc_ref, ddt_ref, x_ref, dx1_ref, w_hbm, g_ref = refs[:9]
        q_in = refs[9:9 + nq]
        dx_ref, st_ref = refs[9 + nq:11 + nq]
        q_out = refs[11 + nq:11 + 2 * nq]
        w_vm, sem, send_sems, recv_sems = refs[11 + 2 * nq:]
        _load_once([(w_hbm, w_vm)], sem)
        for phase, step in enumerate((0, NT - 1)):
            @pl.when(pl.program_id(0) == step)
            def _():
                _quad_phase(phase, q_in, q_out, send_sems, recv_sems)

        @pl.when(pl.program_id(0) == 0)
        def _():
            st_ref[...] = jnp.zeros_like(st_ref)

        dh = jnp.dot(dlx_ref[...], w_vm[0:1024, :], preferred_element_type=F32)
        dh = dh + jnp.dot(dlg_ref[...], w_vm[1024:2048, :], preferred_element_type=F32)
        dh = dh + jnp.dot(dz_ref[...], w_vm[2048:3072, :], preferred_element_type=F32)
        dh = dh + jnp.dot(dxbc_ref[...], w_vm[3072:3072 + XBC, :], preferred_element_type=F32)
        dh = dh + jnp.dot(ddt_ref[...], w_vm[3072 + XBC:PC, :], preferred_element_type=F32)
        dx, dg = _rms_bwd(x_ref[...], g_ref[...], dh)
        dx_ref[...] = dx1_ref[...] + dx
        st_ref[0:1, :] += dg

    return pl.pallas_call(
        body, name="inproj_bwd", grid=(NT,),
        in_specs=[_rows(TW, 1024), _rows(TW, 1024), _rows(TW, 1024), _rows(TW, XBC), _rows(TW, DTP), _rows(TW, D),
                  _rows(TW, D), ANY, _whole((1, D))] + [ANY] * nq,
        out_specs=[_rows(TW, D), _whole((8, D))] + [ANY] * nq,
        out_shape=[S((T, D), F32), S((8, D), F32)] + [S((4,) + p.shape[-2:], p.dtype) for p in parts],
        scratch_shapes=[pltpu.VMEM((PC, D), BF), pltpu.SemaphoreType.DMA((1,)),
                        pltpu.SemaphoreType.DMA((3 * nq,)), pltpu.SemaphoreType.DMA((3 * nq,))],
        compiler_params=_params(),
    )(dlx, dlg, dz, dxbc, ddt, x, dx1, wcatT, g0, *parts)


def _wgrad(name, a, b):
    T, M = a.shape
    N = b.shape[1]
    tk = min(T, 2048 if M <= 1024 else 1024)
    tn = N
    while M * tn * 4 > (6 << 20) and tn % 256 == 0:
        tn //= 2

    def body(a_ref, b_ref, o_ref):
        @pl.when(pl.program_id(1) == 0)
        def _():
            o_ref[...] = jnp.zeros_like(o_ref)

        o_ref[...] += lax.dot_general(a_ref[...], b_ref[...], (((0,), (0,)), ((), ())), preferred_element_type=F32)

    return pl.pallas_call(
        body, name=name, grid=(N // tn, T // tk),
        in_specs=[pl.BlockSpec((tk, M), lambda j, k: (k, 0)), pl.BlockSpec((tk, tn), lambda j, k: (k, j))],
        out_specs=pl.BlockSpec((M, tn), lambda j, k: (0, j)), out_shape=S((M, N), F32),
        compiler_params=_params(2),
    )(a, b)


def _adamw(name, w, g, m, v):
    _, R, C = w.shape

    def body(w_ref, g_ref, m_ref, v_ref, d_ref, nm_ref, nv_ref):
        d_ref[0], nm_ref[0], nv_ref[0] = _adam_math(w_ref[0], g_ref[...], m_ref[0], v_ref[0])

    if R % 8 == 0:
        tr = _row_tile(R, C)
        n_tiles = R // tr
        blk, gblk = pl.BlockSpec((1, tr, C), lambda i: (0, i, 0)), pl.BlockSpec((tr, C), lambda i: (i, 0))
    else:
        tc = 128 * max(k for k in range(1, C // 128 + 1) if C % (128 * k) == 0 and R * 128 * k * 4 <= (5 << 19))
        n_tiles = C // tc
        blk, gblk = pl.BlockSpec((1, R, tc), lambda i: (0, 0, i)), pl.BlockSpec((R, tc), lambda i: (0, i))
    return pl.pallas_call(
        body, name=name, grid=(n_tiles,),
        in_specs=[blk, gblk, blk, blk], out_specs=[blk] * 3,
        out_shape=[S((1, R, C), F32)] * 3, compiler_params=_params(),
    )(w, g, m, v)


def _pair_exchange(name, bufs):
    n = len(bufs)

    def body(*refs):
        for phase in range(2):
            _pair_phase(phase, refs[:n], refs[n:2 * n], refs[2 * n], refs[2 * n + 1])

    return pl.pallas_call(
        body, name=name, in_specs=[ANY] * n, out_specs=[ANY] * n,
        out_shape=[S(_half_shape(b), b.dtype) for b in bufs],
        scratch_shapes=[pltpu.SemaphoreType.DMA((n,)), pltpu.SemaphoreType.DMA((n,))],
    )(*bufs)


def _quad_exchange(bufs):
    n = len(bufs)

    def body(*refs):
        ins, outs = refs[:n], refs[n:2 * n]
        send_sems, recv_sems = refs[2 * n], refs[2 * n + 1]
        x, y, c = _pos()
        me = 2 * x + y
        chips = _other_chips(x, y)
        copies = []
        for k, (src, dst) in enumerate(zip(ins, outs)):
            for j, (cx, cy) in enumerate(chips):
                cp = _remote(src, dst.at[me], send_sems.at[3 * k + j], recv_sems.at[3 * k + j], (cx, cy, c))
                cp.start()
                copies.append(cp)
        for k, (src, dst) in enumerate(zip(ins, outs)):
            for j, (cx, cy) in enumerate(chips):
                blk = dst.at[2 * cx + cy]
                _remote(blk, blk, send_sems.at[3 * k + j], recv_sems.at[3 * k + j], (cx, cy, c)).wait_recv()
        for cp in copies:
            cp.wait_send()

    return pl.pallas_call(
        body, name="quad_exchange", in_specs=[ANY] * n, out_specs=[ANY] * n,
        out_shape=[S((4,) + b.shape, b.dtype) for b in bufs],
        scratch_shapes=[pltpu.SemaphoreType.DMA((3 * n,)), pltpu.SemaphoreType.DMA((3 * n,))],
    )(*bufs)


def _pair_gather(bufs):
    n = len(bufs)

    def body(*refs):
        ins, outs = refs[:n], refs[n:2 * n]
        send_sems, recv_sems = refs[2 * n], refs[2 * n + 1]
        x, y, c = _pos()
        copies = []
        for k, buf in enumerate(outs):
            mine = _half(buf, c, buf.shape[0] // 2)
            cp = _remote(mine, mine, send_sems.at[k], recv_sems.at[k], (x, y, 1 - c))
            cp.start()
            copies.append(cp)
        for k, buf in enumerate(outs):
            theirs = _half(buf, 1 - c, buf.shape[0] // 2)
            _remote(theirs, theirs, send_sems.at[k], recv_sems.at[k], (x, y, 1 - c)).wait_recv()
        for cp in copies:
            cp.wait_send()

    return pl.pallas_call(
        body, name="pair_gather", in_specs=[ANY] * n, out_specs=[ANY] * n,
        out_shape=[S(b.shape, b.dtype) for b in bufs], input_output_aliases={k: k for k in range(n)},
        scratch_shapes=[pltpu.SemaphoreType.DMA((n,)), pltpu.SemaphoreType.DMA((n,))],
    )(*bufs)


def _row_tile(rows, cols, mult=8):
    best = mult
    for t in range(mult, rows + 1, mult):
        if rows % t == 0 and t * cols * 4 <= (1 << 21):
            best = t
    return best


def _add_own_half(name, full, got, c, out_dtype, by_columns):
    hr = got.shape[-2]
    wide = got.shape[-1]
    cols = wide // 4 if by_columns else wide
    tr = _row_tile(hr, wide, 16)
    per = hr // tr

    if by_columns:
        def body(c_ref, a_ref, b_ref, o_ref):
            v = a_ref[...] + b_ref[...]
            for j in range(4):
                o_ref[j] = v[:, j * cols:(j + 1) * cols].astype(out_dtype)

        in_specs = [pl.BlockSpec((tr, wide), lambda i, c_ref: (c_ref[0] * per + i, 0)),
                    pl.BlockSpec((tr, wide), lambda i, c_ref: (i, 0))]
        out_specs = pl.BlockSpec((4, tr, cols), lambda i, c_ref: (0, i, 0))
        grid = (per,)
    else:
        def body(c_ref, a_ref, b_ref, o_ref):
            o_ref[...] = (a_ref[...] + b_ref[...]).astype(out_dtype)

        in_specs = [pl.BlockSpec((1, tr, cols), lambda s, i, c_ref: (s, c_ref[0] * per + i, 0)),
                    pl.BlockSpec((1, tr, cols), lambda s, i, c_ref: (s, i, 0))]
        out_specs = pl.BlockSpec((1, tr, cols), lambda s, i, c_ref: (s, i, 0))
        grid = (4, per)
    return pl.pallas_call(
        body, name=name,
        grid_spec=pltpu.PrefetchScalarGridSpec(num_scalar_prefetch=1, grid=grid, in_specs=in_specs, out_specs=out_specs),
        out_shape=S((4, hr, cols), out_dtype), compiler_params=_params(len(grid)),
    )(jnp.reshape(c, (1,)).astype(jnp.int32), full, got)


def _small_add_own_half(name, fulls, gots, c):
    n = len(fulls)

    def body(c_ref, *refs):
        for a_ref, b_ref, o_ref in zip(refs[:n], refs[n:2 * n], refs[2 * n:]):
            hr = b_ref.shape[0]
            o_ref[...] = a_ref[pl.ds(pl.multiple_of(c_ref[0] * hr, 8), hr), :] + b_ref[...]

    specs = lambda arrs: [pl.BlockSpec(a.shape, lambda i, c_ref: (0, 0)) for a in arrs]
    return pl.pallas_call(
        body, name=name,
        grid_spec=pltpu.PrefetchScalarGridSpec(num_scalar_prefetch=1, grid=(1,), in_specs=specs(fulls) + specs(gots),
                                               out_specs=specs(gots)),
        out_shape=[S(g.shape, F32) for g in gots], compiler_params=_params(),
    )(jnp.reshape(c, (1,)).astype(jnp.int32), *fulls, *gots)


def _small_sum_slots(name, own, slots, me, c):
    n = len(slots)

    def body(p_ref, *refs):
        own_refs, slot_refs, o_refs = refs[:n], refs[n:5 * n], refs[5 * n:]
        for i, (own_ref, o_ref) in enumerate(zip(own_refs, o_refs)):
            hr = own_ref.shape[0]
            acc = None
            for j in range(4):
                v = jnp.where(p_ref[0] == j, own_ref[...], slot_refs[4 * i + j][0])
                acc = v if acc is None else acc + v
            o_ref[pl.ds(pl.multiple_of(p_ref[1] * hr, 8), hr), :] = acc

    def slot_spec(s, j):
        return pl.BlockSpec((1,) + s.shape[1:], lambda i, p: (jnp.where(p[0] == j, (j + 1) % 4, j), 0, 0))

    outs = [S((2 * s.shape[1], s.shape[2]), F32) for s in slots]
    return pl.pallas_call(
        body, name=name,
        grid_spec=pltpu.PrefetchScalarGridSpec(
            num_scalar_prefetch=1, grid=(1,),
            in_specs=[pl.BlockSpec(o.shape, lambda i, p: (0, 0)) for o in own]
            + [slot_spec(s, j) for s in slots for j in range(4)],
            out_specs=[pl.BlockSpec(o.shape, lambda i, p: (0, 0)) for o in outs]),
        out_shape=outs, compiler_params=_params(),
    )(jnp.stack([me, c]).astype(jnp.int32), *own, *[s for s in slots for _ in range(4)])


def _sum_slots(name, own, slots, me, c):
    _, rows, cols = slots.shape
    tr = _row_tile(rows, cols, 16 if slots.dtype == jnp.bfloat16 else 8)
    per = rows // tr
    three = len(own.shape) == 3

    def body(p_ref, own_ref, s0, s1, s2, s3, o_ref):
        mine = own_ref[0] if three else own_ref[...]
        acc = None
        for j, s_ref in enumerate((s0, s1, s2, s3)):
            v = jnp.where(p_ref[0] == j, mine, s_ref[0]).astype(F32)
            acc = v if acc is None else acc + v
        o_ref[...] = acc

    def slot_spec(j):
        return pl.BlockSpec((1, tr, cols), lambda i, p: (jnp.where(p[0] == j, (j + 1) % 4, j), i, 0))

    own_spec = (pl.BlockSpec((1, tr, cols), lambda i, p: (p[0], i, 0)) if three
                else pl.BlockSpec((tr, cols), lambda i, p: (i, 0)))
    return pl.pallas_call(
        body, name=name,
        grid_spec=pltpu.PrefetchScalarGridSpec(
            num_scalar_prefetch=1, grid=(per,), in_specs=[own_spec] + [slot_spec(j) for j in range(4)],
            out_specs=pl.BlockSpec((tr, cols), lambda i, p: (p[1] * per + i, 0))),
        out_shape=S((2 * rows, cols), F32), compiler_params=_params(),
    )(jnp.stack([me, c]).astype(jnp.int32), own, slots, slots, slots, slots)


BIG = ("w_in", "w_out", "w_gate", "w_up", "w_down")
ROW_PARAMS = (("pre_mix_norm", 0), ("lru_conv_b", 12), ("lru_ba", 13), ("lru_bx", 14), ("lru_lambda", 15),
              ("lru_out_norm", 16), ("ssd_out_norm", 24), ("post_mix_norm", 33), ("pre_ffn_norm", 32), ("post_ffn_norm", 41))
LRU_CONV_ROWS = (8, 12)
LOSS_ROW = 40
HEAD_PARAMS = (("ssd_dt_bias", 0), ("ssd_a_log", 1), ("ssd_d", 2))
SMALL = tuple(n for n, _ in ROW_PARAMS) + ("ssd_conv_b",) + tuple(n for n, _ in HEAD_PARAMS) + (
    "lru_wa", "lru_wx", "lru_conv_w", "ssd_conv_w")


def _diag4(w):
    eye = jnp.eye(4, dtype=w.dtype).reshape(1, 4, 1, 4, 1)
    return (w.reshape(4, 4, BW, 1, BW) * eye).reshape(4, 4 * BW, 4 * BW)


def _adam_math(w, g, m, v):
    mm = ADAM_B1 * m + (1.0 - ADAM_B1) * g
    vv = ADAM_B2 * v + (1.0 - ADAM_B2) * (g * g)
    c1 = 1.0 - ADAM_B1 ** ADAM_STEP
    c2 = 1.0 - ADAM_B2 ** ADAM_STEP
    return -ADAM_LR * ((mm / c1) / (jnp.sqrt(vv / c2) + ADAM_EPS) + ADAM_WD * w), mm, vv


def _adamw_small(rows, cst, hst, dwa, dwx, glcw, gscw, w, m, v):
    def grad_of(name, refs):
        rows_ref, cst_ref, hst_ref, dwa_ref, dwx_ref, glcw_ref, gscw_ref = refs
        for n, r in ROW_PARAMS:
            if n == name:
                return rows_ref[r:r + 1, :]
        for n, r in HEAD_PARAMS:
            if n == name:
                return hst_ref[r:r + 1, 0:NH]
        return {"ssd_conv_b": lambda: cst_ref[4:5, :], "lru_wa": lambda: dwa_ref[...], "lru_wx": lambda: dwx_ref[...],
                "lru_conv_w": lambda: glcw_ref[...], "ssd_conv_w": lambda: gscw_ref[...]}[name]()

    shapes = {n: (w[n].shape[1:] if len(w[n].shape) > 2 else w[n].shape) for n in SMALL}
    flat = lambda d: [d[n].reshape(shapes[n]) for n in SMALL]
    ns = len(SMALL)

    def body(*refs):
        srcs, rest = refs[:7], refs[7:]
        w_refs, m_refs, v_refs = rest[:ns], rest[ns:2 * ns], rest[2 * ns:3 * ns]
        outs = rest[3 * ns:]
        for k, name in enumerate(SMALL):
            g = grad_of(name, srcs)
            d, mm, vv = _adam_math(w_refs[k][...], g, m_refs[k][...], v_refs[k][...])
            outs[4 * k][...] = g
            outs[4 * k + 1][...] = d
            outs[4 * k + 2][...] = mm
            outs[4 * k + 3][...] = vv

    res = pl.pallas_call(
        body, name="adamw_small",
        out_shape=[S(shapes[n], F32) for n in SMALL for _ in range(4)],
        compiler_params=pltpu.CompilerParams(vmem_limit_bytes=VMEM_LIMIT),
    )(rows, cst, hst, dwa, dwx, glcw, gscw, *flat(w), *flat(m), *flat(v))
    return {n: tuple(res[4 * k + i].reshape(w[n].shape) for i in range(4)) for k, n in enumerate(SMALL)}


def _with_own(own, got):
    chip = 2 * lax.axis_index("x") + lax.axis_index("y")
    return jnp.where((jnp.arange(4) == chip).reshape(4, 1, 1), own[None], got)


def _side_by_side(f):
    return f.transpose(1, 0, 2).reshape(f.shape[1], 4 * f.shape[2])


def _stacked(f):
    return f.reshape(4 * f.shape[1], f.shape[2])


def _conv_terms(lru_conv_w, ssd_conv_w):
    conv = jnp.concatenate([lru_conv_w.reshape(-1), ssd_conv_w.reshape(-1)]).astype(F32)
    hi = conv.astype(jnp.bfloat16)
    mid = (conv - hi.astype(F32)).astype(jnp.bfloat16)
    lo = (conv - hi.astype(F32) - mid.astype(F32)).astype(jnp.bfloat16)
    terms = jnp.concatenate([hi, mid, lo])
    rows = -(-terms.shape[0] // (128 * 32)) * 32
    return jnp.pad(terms, (0, rows * 128 - terms.shape[0])).reshape(rows, 128)


def _full_conv_taps(own, got, n_lru, n_ssd):
    n_terms = 3 * (n_lru + n_ssd)
    t3 = _with_own(own, got).reshape(4, -1)[:, :n_terms].reshape(4, 3, -1).astype(F32)
    conv_f = (t3[:, 0] + t3[:, 1]) + t3[:, 2]
    lcw = conv_f[:, :n_lru].reshape(4, CONV_K, -1).transpose(1, 0, 2).reshape(CONV_K, LW)
    scw = conv_f[:, n_lru:].reshape(4, CONV_K, -1).transpose(1, 0, 2).reshape(CONV_K, XBC)
    return lcw, scw


def _step(x, tgt, w_in, lru_conv_w, ssd_conv_w, sp, late):
    c = lax.axis_index("c")
    me = 2 * lax.axis_index("x") + lax.axis_index("y")
    mm = lambda w: w.astype(BF)
    row = lambda v: v.reshape(1, -1).astype(F32)
    g0 = row(sp["pre_mix_norm"])
    first = [w_in.astype(WIRE), _conv_terms(lru_conv_w, ssd_conv_w)]
    h0, *got_first = _prenorm(x, g0, first)
    win_f = _side_by_side(_with_own(first[0], got_first[0]))
    lcw, scw = _full_conv_taps(first[1], got_first[1], lru_conv_w.size, ssd_conv_w.size)
    wcat = jnp.concatenate([mm(win_f), jnp.zeros((D, PC - IN_COLS), BF)], axis=1)
    p_lru = jnp.concatenate([lcw, row(sp["lru_conv_b"]), row(sp["lru_ba"]), row(sp["lru_bx"]), row(sp["lru_lambda"]),
                             row(sp["lru_out_norm"]), jnp.zeros((7, LW), F32)], axis=0)
    wa4, wx4 = mm(_diag4(sp["lru_wa"][0])), mm(_diag4(sp["lru_wx"][0]))
    wa4T, wx4T = wa4.transpose(0, 2, 1), wx4.transpose(0, 2, 1)
    cw_ssd = jnp.concatenate([scw, row(sp["ssd_conv_b"]), jnp.zeros((3, XBC), F32)], axis=0)
    padh = lambda v: jnp.pad(row(v), ((0, 0), (0, DTP - NH)))
    hp_ssd = jnp.concatenate([padh(sp["ssd_dt_bias"]), padh(sp["ssd_a_log"]), padh(sp["ssd_d"]), jnp.zeros((5, DTP), F32)], axis=0)
    g_ssd = row(sp["ssd_out_norm"])
    g_pm, g_pf, g_pff = row(sp["post_mix_norm"]), row(sp["pre_ffn_norm"]), row(sp["post_ffn_norm"])

    h, ylru, lxc, lxr, lg, *got_a = _lru_fwd(h0, wcat, p_lru, wa4, wx4, [late[0], late[3]])
    y, yssd, states, cv, z, xbcr, dtr, *got_b = _ssd_fwd(h0, wcat, cw_ssd, hp_ssd, g_ssd, [late[1], late[2]])
    wout, wd = mm(_stacked(_with_own(late[0], got_a[0]))), mm(_stacked(_with_own(late[3], got_a[1])))
    wg, wu = mm(_side_by_side(_with_own(late[1], got_b[0]))), mm(_side_by_side(_with_own(late[2], got_b[1])))
    mix, x1, h2 = _outproj(ylru, yssd, x, wout, g_pm, g_pf)
    gate, up, act, df, dx2, st_ffn = _ffn_fwd(h2, x1, tgt, wg, wu, wd, g_pff)
    dgate, dup, dh2 = _ffn_bwd(df, gate, up, wd.T, wg.T, wu.T)
    dx1, dmix, dyl, dys, st_mix = _mix_bwd(dh2, x1, dx2, mix, wout.T, g_pf, g_pm)

    dwg = _wgrad("wgrad_gate", h2, dgate)
    dwu = _wgrad("wgrad_up", h2, dup)
    dwd = _wgrad("wgrad_down", act, df)
    dwo = jnp.concatenate([_wgrad("wgrad_out_lru", ylru, dmix), _wgrad("wgrad_out_ssd", yssd, dmix)], axis=0)
    early = [dwo.reshape(4, (LW + SI) // 4, D), dwg, dwu, dwd.reshape(4, DFF // 4, D)]
    dlx, dlg, st_lru, dwa, dwx, *got_early = _lru_bwd(dyl, lxr, lxc, lg, h, p_lru, wa4, wx4, wa4T, wx4T, early)
    part_early = [_add_own_half("pair_add_early%d" % k, b, r, c, WIRE, bc)
                  for k, (b, r, bc) in enumerate(zip(early, got_early, [False, True, True, False]))]
    dxbc, dz, ddt, cst, hst, gst, *slots_early = _ssd_bwd(dys, xbcr, cv, z, dtr, y, states, cw_ssd, hp_ssd, g_ssd,
                                                          part_early)
    red_early = [_sum_slots("quad_sum_early%d" % k, p, s, me, c) for k, (p, s) in enumerate(zip(part_early, slots_early))]

    pin = [_wgrad("wgrad_in_%d" % k, h0, b) for k, b in enumerate((dlx, dlg, dz, dxbc, ddt))]
    dwin = jnp.concatenate(pin[:4] + [pin[4][:, :NH]], axis=1)
    small4 = [cst, hst, dwa.reshape(NBLK * BW, BW), dwx.reshape(NBLK * BW, BW)]
    got_win, *got_small4 = _pair_exchange("pair_exchange_w_in", [dwin] + small4)
    part_win = _add_own_half("pair_add_w_in", dwin, got_win, c, WIRE, True)
    part_small4 = list(_small_add_own_half("small_pair_add", small4, got_small4, c))
    gx, st_in, slots_win, *slots_small4 = _inproj_bwd(dlx, dlg, dz, dxbc, ddt, x, dx1, wcat.T, g0,
                                                      [part_win] + part_small4)
    red_win = _sum_slots("quad_sum_w_in", part_win, slots_win, me, c)
    red_small4 = list(_small_sum_slots("small_quad_sum", part_small4, slots_small4, me, c))

    rows = jnp.concatenate([st_in, st_lru, gst, st_mix, st_ffn], axis=0)
    part_rows = list(_small_add_own_half("rows_pair_add", [rows], list(_pair_exchange("pair_exchange_rows", [rows])), c))
    red_rows = list(_small_sum_slots("rows_quad_sum", part_rows, list(_quad_exchange(part_rows)), me, c))
    out = list(_pair_gather([red_win] + red_early + red_rows + red_small4))
    big = dict(zip(("w_in", "w_out", "w_gate", "w_up", "w_down"), out[:5]))
    return gx, big, out[5:]


def kernel(x, pre_mix_norm, w_in, lru_conv_w, lru_conv_b, lru_wa, lru_ba, lru_wx, lru_bx, lru_lambda, lru_out_norm, ssd_conv_w, ssd_conv_b, ssd_dt_bias, ssd_a_log, ssd_d, ssd_out_norm, w_out, post_mix_norm, pre_ffn_norm, w_gate, w_up, w_down, post_ffn_norm, loss_target, m_pre_mix_norm, m_w_in, m_lru_conv_w, m_lru_conv_b, m_lru_wa, m_lru_ba, m_lru_wx, m_lru_bx, m_lru_lambda, m_lru_out_norm, m_ssd_conv_w, m_ssd_conv_b, m_ssd_dt_bias, m_ssd_a_log, m_ssd_d, m_ssd_out_norm, m_w_out, m_post_mix_norm, m_pre_ffn_norm, m_w_gate, m_w_up, m_w_down, m_post_ffn_norm, v_pre_mix_norm, v_w_in, v_lru_conv_w, v_lru_conv_b, v_lru_wa, v_lru_ba, v_lru_wx, v_lru_bx, v_lru_lambda, v_lru_out_norm, v_ssd_conv_w, v_ssd_conv_b, v_ssd_dt_bias, v_ssd_a_log, v_ssd_d, v_ssd_out_norm, v_w_out, v_post_mix_norm, v_pre_ffn_norm, v_w_gate, v_w_up, v_w_down, v_post_ffn_norm):
    args = dict(locals())
    names = list(SMALL) + list(BIG)
    w = {n: args[n] for n in names}
    m = {n: args["m_" + n] for n in names}
    v = {n: args["v_" + n] for n in names}
    chip = 2 * lax.axis_index("x") + lax.axis_index("y")

    late = [a[0].astype(WIRE) for a in (w_out, w_gate, w_up, w_down)]
    gx, red, (rows, cst, hst, dwa, dwx) = _step(x[0], loss_target[0], w_in[0], lru_conv_w[0], ssd_conv_w[0],
                                                {n: w[n] for n in SMALL}, late)
    loss = jnp.sum(rows[LOSS_ROW])

    grads, delta, new_m, new_v = {}, {}, {}, {}
    for n in BIG:
        g = red[n]
        if n in ("w_in", "w_gate", "w_up"):
            t = lambda a: jnp.swapaxes(a, 1, 2)
            gt = g.T
            out = _adamw("adamw_" + n, t(w[n]), gt, t(m[n]), t(v[n]))
            delta[n], new_m[n], new_v[n] = (t(o) for o in out)
            grads[n] = t(gt[None])
        else:
            delta[n], new_m[n], new_v[n] = _adamw("adamw_" + n, w[n], g, m[n], v[n])
            grads[n] = g[None]

    lc, sc = lru_conv_w.shape[-1], ssd_conv_w.shape[-1]
    glcw = lax.dynamic_slice_in_dim(rows[LRU_CONV_ROWS[0]:LRU_CONV_ROWS[1]], chip * lc, lc, axis=1)
    gscw = lax.dynamic_slice_in_dim(cst[0:CONV_K], chip * sc, sc, axis=1)
    res = _adamw_small(rows, cst, hst, dwa.reshape(NBLK, BW, BW), dwx.reshape(NBLK, BW, BW), glcw, gscw,
                       {n: w[n] for n in SMALL}, {n: m[n] for n in SMALL}, {n: v[n] for n in SMALL})
    for n in SMALL:
        grads[n], delta[n], new_m[n], new_v[n] = res[n]

    order = ["pre_mix_norm", "w_in", "lru_conv_w", "lru_conv_b", "lru_wa", "lru_ba", "lru_wx", "lru_bx", "lru_lambda",
             "lru_out_norm", "ssd_conv_w", "ssd_conv_b", "ssd_dt_bias", "ssd_a_log", "ssd_d", "ssd_out_norm", "w_out",
             "post_mix_norm", "pre_ffn_norm", "w_gate", "w_up", "w_down", "post_ffn_norm"]
    return (loss, gx[None], *[grads[n] for n in order], *[delta[n] for n in order],
            *[new_m[n] for n in order], *[new_v[n] for n in order])
```

```python
import functools

import jax
import jax.numpy as jnp
from jax import lax
from jax.experimental import pallas as pl
from jax.experimental.pallas import tpu as pltpu

F32 = jnp.float32
BF = jnp.bfloat16

D = 1024
LW = 1024
NBLK = 16
BW = 64
SI = 1024
NH = 16
HD = 64
NG = 2
HPG = NH // NG
NS = 128
CH = 128
XBC = SI + 2 * NG * NS
DTP = 128
PC = 3 * 1024 + XBC + DTP
DFF = 2816
FFN_PARTS = 2
IN_COLS = 4624
EPS = 1e-6
LRU_C = 8.0
CONV_K = 4
TT = 256
TW = 512
VMEM_LIMIT = 56 * 1024 * 1024

ADAM_LR, ADAM_B1, ADAM_B2, ADAM_EPS, ADAM_WD, ADAM_STEP = 0.001, 0.9, 0.999, 1e-08, 0.01, 10

MESH = pl.DeviceIdType.MESH


def _mm(a, b):
    return jnp.dot(a.astype(BF), b.astype(BF), preferred_element_type=F32)


def _mm_nt(a, b):
    return lax.dot_general(a.astype(BF), b.astype(BF), (((1,), (1,)), ((), ())), preferred_element_type=F32)


def _mm_tn(a, b):
    return lax.dot_general(a.astype(BF), b.astype(BF), (((0,), (0,)), ((), ())), preferred_element_type=F32)


def _sigmoid(x):
    return 0.5 * jnp.tanh(0.5 * x) + 0.5


def _softplus(x):
    return jnp.maximum(x, 0.0) + jnp.log1p(jnp.exp(-jnp.abs(x)))


_GELU_C = 0.7978845608028654
_GELU_K = 0.044715


def _gelu(x):
    t = jnp.tanh(_GELU_C * (x + _GELU_K * x * x * x))
    return 0.5 * x * (1.0 + t)


def _gelu_grad(x):
    t = jnp.tanh(_GELU_C * (x + _GELU_K * x * x * x))
    return 0.5 * (1.0 + t) + 0.5 * x * (1.0 - t * t) * _GELU_C * (1.0 + 3.0 * _GELU_K * x * x)


def _rms_fwd(x, g):
    r = lax.rsqrt(jnp.mean(x * x, axis=-1, keepdims=True) + EPS)
    return x * r * g


def _rms_bwd(x, g, dy):
    r = lax.rsqrt(jnp.mean(x * x, axis=-1, keepdims=True) + EPS)
    xh = x * r
    dxh = dy * g
    dg = jnp.sum(dy * xh, axis=0, keepdims=True)
    dx = r * (dxh - xh * jnp.mean(dxh * xh, axis=-1, keepdims=True))
    return dx, dg


def _sum_all(x):
    return jnp.sum(jnp.sum(x, axis=1, keepdims=True), axis=0, keepdims=True)


def _cumsum_rows(x, n):
    row = lax.broadcasted_iota(jnp.int32, x.shape, 0)
    k = 1
    while k < n:
        x = x + jnp.where(row >= k, pltpu.roll(x, k, 0), 0.0)
        k *= 2
    return x


def _rev_cumsum_rows(x, n):
    row = lax.broadcasted_iota(jnp.int32, x.shape, 0)
    k = 1
    while k < n:
        x = x + jnp.where(row < n - k, pltpu.roll(x, n - k, 0), 0.0)
        k *= 2
    return x


def _load_once(pairs, sem):
    @pl.when(pl.program_id(0) == 0)
    def _():
        for k, (src, dst) in enumerate(pairs):
            pltpu.make_async_copy(src, dst, sem.at[k]).start()
        for k, (src, dst) in enumerate(pairs):
            pltpu.make_async_copy(src, dst, sem.at[k]).wait()


def _params(n_axes=1):
    return pltpu.CompilerParams(dimension_semantics=("arbitrary",) * n_axes, vmem_limit_bytes=VMEM_LIMIT)


def _rows(n, width, rev_of=None):
    if rev_of is None:
        return pl.BlockSpec((n, width), lambda i: (i, 0))
    return pl.BlockSpec((n, width), lambda i: (rev_of - 1 - i, 0))


def _whole(shape):
    nd = len(shape)
    return pl.BlockSpec(shape, lambda i: (0,) * nd)


ANY = pl.BlockSpec(memory_space=pl.ANY)
S = jax.ShapeDtypeStruct
WIRE = jnp.bfloat16


def _pos():
    return lax.axis_index("x"), lax.axis_index("y"), lax.axis_index("c")


def _other_chips(x, y):
    return [(1 - x, y), (x, 1 - y), (1 - x, 1 - y)]


def _remote(src, dst, send_sem, recv_sem, to):
    return pltpu.make_async_remote_copy(src_ref=src, dst_ref=dst, send_sem=send_sem, recv_sem=recv_sem,
                                        device_id=to, device_id_type=MESH)


def _gather_phase(phase, ins, outs, send_sems, recv_sems):
    x, y, c = _pos()
    me = 2 * x + y
    chips = _other_chips(x, y)
    for i, (src, dst) in enumerate(zip(ins, outs)):
        hr = src.shape[0] // 2
        my_half = pl.ds(pl.multiple_of(c * hr, 16), hr)
        sib_half = pl.ds(pl.multiple_of((1 - c) * hr, 16), hr)
        for k, (cx, cy) in enumerate(chips):
            s1, r1 = send_sems.at[6 * i + k], recv_sems.at[6 * i + k]
            s2, r2 = send_sems.at[6 * i + 3 + k], recv_sems.at[6 * i + 3 + k]
            first = lambda: _remote(src.at[my_half, :], dst.at[me, my_half, :], s1, r1, (cx, cy, c))
            landed = dst.at[2 * cx + cy, my_half, :]
            passed = lambda: _remote(landed, landed, s2, r2, (x, y, 1 - c))
            if phase == 0:
                first().start()
            elif phase == 1:
                _remote(landed, landed, s1, r1, (cx, cy, c)).wait_recv()
                passed().start()
            else:
                theirs = dst.at[2 * cx + cy, sib_half, :]
                _remote(theirs, theirs, s2, r2, (x, y, 1 - c)).wait_recv()
                first().wait_send()
                passed().wait_send()


def _half(ref, c, hr):
    sl = pl.ds(pl.multiple_of(c * hr, 8), hr)
    return ref.at[:, sl, :] if len(ref.shape) == 3 else ref.at[sl, :]


def _half_shape(b):
    return b.shape[:-2] + (b.shape[-2] // 2, b.shape[-1])


def _pair_phase(phase, ins, outs, send_sems, recv_sems):
    x, y, c = _pos()
    for k, (src, dst) in enumerate(zip(ins, outs)):
        cp = _remote(_half(src, 1 - c, src.shape[-2] // 2), dst, send_sems.at[k], recv_sems.at[k], (x, y, 1 - c))
        if phase == 0:
            cp.start()
        else:
            cp.wait()


def _quad_phase(phase, ins, outs, send_sems, recv_sems):
    x, y, c = _pos()
    me = 2 * x + y
    for i, (src, dst) in enumerate(zip(ins, outs)):
        for k, (cx, cy) in enumerate(_other_chips(x, y)):
            piece = src.at[2 * cx + cy] if len(src.shape) == 3 else src
            cp = _remote(piece, dst.at[me], send_sems.at[3 * i + k], recv_sems.at[3 * i + k], (cx, cy, c))
            if phase == 0:
                cp.start()
            else:
                got = dst.at[2 * cx + cy]
                _remote(got, got, send_sems.at[3 * i + k], recv_sems.at[3 * i + k], (cx, cy, c)).wait_recv()
                cp.wait_send()


def _prenorm(x, g0, shards):
    T = x.shape[0]
    tt = 2 * TT
    nt = T // tt
    ng = len(shards)

    def body(*refs):
        x_ref, g_ref = refs[:2]
        sh_in = refs[2:2 + ng]
        h0_ref = refs[2 + ng]
        sh_out = refs[3 + ng:3 + 2 * ng]
        send_sems, recv_sems = refs[3 + 2 * ng:]
        for phase, step in enumerate((0, nt // 2, nt - 1)):
            @pl.when(pl.program_id(0) == step)
            def _():
                _gather_phase(phase, sh_in, sh_out, send_sems, recv_sems)

        h0_ref[...] = _rms_fwd(x_ref[...], g_ref[...]).astype(BF)

    return pl.pallas_call(
        body, name="prenorm", grid=(nt,),
        in_specs=[_rows(tt, D), _whole((1, D))] + [ANY] * ng, out_specs=[_rows(tt, D)] + [ANY] * ng,
        out_shape=[S((T, D), BF)] + [S((4,) + s.shape, s.dtype) for s in shards],
        scratch_shapes=[pltpu.SemaphoreType.DMA((6 * ng,)), pltpu.SemaphoreType.DMA((6 * ng,))],
        compiler_params=_params(),
    )(x, g0, *shards)


def _blockdiag_mm(v, w4_ref):
    return jnp.concatenate([_mm(v[:, 256 * j:256 * (j + 1)], w4_ref[j]) for j in range(4)], axis=1)


def _lru_gates(lx, p_ref, wa_ref, wx_ref):
    r = _sigmoid(_blockdiag_mm(lx, wa_ref) + p_ref[5:6, :])
    i = _sigmoid(_blockdiag_mm(lx, wx_ref) + p_ref[6:7, :])
    sp = _softplus(-p_ref[7:8, :])
    la = -LRU_C * r * sp
    a = jnp.exp(la)
    th = jnp.tanh(la)
    mult = jnp.sqrt(-2.0 * th / (1.0 - th))
    return r, i, sp, a, mult


def _conv_from(xp_ref, p_ref, n):
    acc = p_ref[4:5, :] + p_ref[0:1, :] * xp_ref[pl.ds(8 - CONV_K + 1, n), :]
    for k in range(1, CONV_K):
        acc = acc + p_ref[k:k + 1, :] * xp_ref[pl.ds(8 - CONV_K + 1 + k, n), :]
    return acc


def _conv_bwd(dp_ref, dconv, x, p_ref, st_ref, n):
    if dconv is None:
        dconv = dp_ref[0:n, :]
    else:
        dp_ref[0:n, :] = dconv
    acc = None
    for k in range(CONV_K):
        g = dp_ref[pl.ds(CONV_K - 1 - k, n), :]
        acc = p_ref[k:k + 1, :] * g if acc is None else acc + p_ref[k:k + 1, :] * g
        st_ref[k:k + 1, :] += jnp.sum(g * x, axis=0, keepdims=True)
    st_ref[4:5, :] += jnp.sum(dconv, axis=0, keepdims=True)
    dp_ref[n:n + 8, :] = dp_ref[0:8, :]
    return acc


def _lru_fwd(h0, wcat, p_lru, wa4, wx4, shards):
    T = h0.shape[0]
    NT = T // TT
    ng = len(shards)

    def body(*refs):
        h0_ref, w_hbm, p_ref, wa_ref, wx_ref = refs[:5]
        sh_in = refs[5:5 + ng]
        h_ref, y_ref, lxc_ref, lxr_ref, lg_ref = refs[5 + ng:10 + ng]
        sh_out = refs[10 + ng:10 + 2 * ng]
        xp, a_s, u_s, hc, w_vm, wsem, send_sems, recv_sems = refs[10 + 2 * ng:]
        _load_once([(w_hbm.at[:, 0:2 * LW], w_vm)], wsem)
        for phase, step in enumerate((0, NT // 2, NT - 1)):
            @pl.when(pl.program_id(0) == step)
            def _():
                _gather_phase(phase, sh_in, sh_out, send_sems, recv_sems)

        @pl.when(pl.program_id(0) == 0)
        def _():
            xp[0:8, :] = jnp.zeros((8, LW), F32)
            hc[...] = jnp.zeros_like(hc)

        hv = h0_ref[...]
        lxr = jnp.dot(hv, w_vm[:, 0:LW], preferred_element_type=F32)
        lxr_ref[...] = lxr
        lg_ref[...] = jnp.dot(hv, w_vm[:, LW:2 * LW], preferred_element_type=F32)
        xp[8:8 + TT, :] = lxr
        lx = _conv_from(xp, p_ref, TT)
        lxc_ref[...] = lx
        xp[0:8, :] = xp[TT:TT + 8, :]
        r, i, sp, a, mult = _lru_gates(lx, p_ref, wa_ref, wx_ref)
        a_s[...] = a
        u_s[...] = mult * (i * lx)

        def step(t, h):
            h = a_s[pl.ds(t, 1), :] * h + u_s[pl.ds(t, 1), :]
            h_ref[pl.ds(t, 1), :] = h
            return h

        hc[0:1, :] = lax.fori_loop(0, TT, step, hc[0:1, :], unroll=8)
        gated = h_ref[...] * _gelu(lg_ref[...])
        y_ref[...] = _rms_fwd(gated, p_ref[8:9, :]).astype(BF)

    return pl.pallas_call(
        body, name="lru_fwd", grid=(NT,),
        in_specs=[_rows(TT, D), ANY, _whole((16, LW)), _whole((4, 256, 256)), _whole((4, 256, 256))] + [ANY] * ng,
        out_specs=[_rows(TT, LW), _rows(TT, LW), _rows(TT, LW), _rows(TT, LW), _rows(TT, LW)] + [ANY] * ng,
        out_shape=[S((T, LW), F32), S((T, LW), BF), S((T, LW), F32), S((T, LW), F32), S((T, LW), F32)]
        + [S((4,) + s.shape, s.dtype) for s in shards],
        scratch_shapes=[pltpu.VMEM((TT + 8, LW), F32), pltpu.VMEM((TT, LW), F32), pltpu.VMEM((TT, LW), F32),
                        pltpu.VMEM((8, LW), F32), pltpu.VMEM((D, 2 * LW), BF), pltpu.SemaphoreType.DMA((1,)),
                        pltpu.SemaphoreType.DMA((6 * ng,)), pltpu.SemaphoreType.DMA((6 * ng,))],
        compiler_params=_params(),
    )(h0, wcat, p_lru, wa4, wx4, *shards)


def _ssd_prep(cv, dt_ref, hp_ref):
    sg = _sigmoid(cv)
    xbc = cv * sg
    lane = lax.broadcasted_iota(jnp.int32, (CH, DTP), 1)
    raw = dt_ref[...] + hp_ref[0:1, :]
    dtv = jnp.where(lane < NH, _softplus(raw), 0.0)
    A = jnp.where(lane[0:1, :] < NH, -jnp.exp(hp_ref[1:2, :]), 0.0)
    cs = _cumsum_rows(dtv * A, CH)
    return sg, xbc, raw, dtv, A, cs


def _per_head_lanes(v):
    r = v.shape[0]
    first = lax.broadcasted_iota(jnp.int32, (r, 2 * HD), 1) < HD
    pairs = [jnp.where(first, jnp.broadcast_to(v[:, 2 * j:2 * j + 1], (r, 2 * HD)),
                       jnp.broadcast_to(v[:, 2 * j + 1:2 * j + 2], (r, 2 * HD))) for j in range(NH // 2)]
    return jnp.concatenate(pairs, axis=1)


def _per_head_rows(col, g):
    return jnp.concatenate([jnp.broadcast_to(col[g * HPG + k:g * HPG + k + 1, :], (HD, NS)) for k in range(HPG)], axis=0)


def _ssd_decays(cs):
    csT = cs.T
    cl = cs[CH - 1:CH, :]
    E_x = _per_head_lanes(jnp.exp(cs))
    dsm = jnp.exp(cl - cs)
    ds_x = _per_head_lanes(dsm)
    El_rows = jnp.broadcast_to(jnp.exp(csT[0:NH, CH - 1:CH]), (NH, NS))
    return csT, dsm, E_x, ds_x, El_rows


def _ssd_fwd(h0, wcat, cw_ssd, hp_ssd, g_ssd, shards):
    T = h0.shape[0]
    NC = T // CH
    ng = len(shards)
    c0 = 2 * LW

    def body(*refs):
        h0_ref, w_hbm, cw_ref, hp_ref, g_ref = refs[:5]
        sh_in = refs[5:5 + ng]
        y_ref, yn_ref, st_ref, cv_ref, z_ref, xr_ref, dt_ref = refs[5 + ng:12 + ng]
        sh_out = refs[12 + ng:12 + 2 * ng]
        xp, st, w_vm, wsem, send_sems, recv_sems = refs[12 + 2 * ng:]
        _load_once([(w_hbm.at[:, c0:PC], w_vm)], wsem)
        for phase, step in enumerate((0, NC // 2, NC - 1)):
            @pl.when(pl.program_id(0) == step)
            def _():
                _gather_phase(phase, sh_in, sh_out, send_sems, recv_sems)

        @pl.when(pl.program_id(0) == 0)
        def _():
            xp[0:8, :] = jnp.zeros((8, XBC), F32)
            st[...] = jnp.zeros_like(st)

        hv = h0_ref[...]
        z_ref[...] = jnp.dot(hv, w_vm[:, 0:SI], preferred_element_type=F32)
        xraw = jnp.dot(hv, w_vm[:, SI:SI + XBC], preferred_element_type=F32)
        xr_ref[...] = xraw
        dt_ref[...] = jnp.dot(hv, w_vm[:, SI + XBC:SI + XBC + DTP], preferred_element_type=F32)
        xp[8:8 + CH, :] = xraw
        cv = _conv_from(xp, cw_ref, CH)
        cv_ref[...] = cv
        sg, xbc, raw, dtv, A, cs = _ssd_prep(cv, dt_ref, hp_ref)
        xp[0:8, :] = xp[CH:CH + 8, :]
        st_ref[0] = st[...]
        csT, dsm, E_x, ds_x, El_rows = _ssd_decays(cs)
        X = xbc[:, 0:SI]
        xs = X * _per_head_lanes(dtv)
        xsd = (xs * ds_x).astype(BF)
        DX = _per_head_lanes(hp_ref[...])[2:3, :] * X
        tril = lax.broadcasted_iota(jnp.int32, (CH, CH), 0) >= lax.broadcasted_iota(jnp.int32, (CH, CH), 1)
        first = lax.broadcasted_iota(jnp.int32, (CH, 2 * HD), 1) < HD
        GW = HPG * HD
        for g in range(NG):
            Bg = xbc[:, SI + NS * g:SI + NS * (g + 1)].astype(BF)
            Cg = xbc[:, SI + NG * NS + NS * g:SI + NG * NS + NS * (g + 1)].astype(BF)
            G = _mm_nt(Cg, Bg)
            Sg = st[GW * g:GW * (g + 1), :]
            Yo = _mm_nt(Cg, Sg) * E_x[:, GW * g:GW * (g + 1)]
            st[GW * g:GW * (g + 1), :] = _per_head_rows(El_rows, g) * Sg + _mm_tn(xsd[:, GW * g:GW * (g + 1)], Bg)
            for jj in range(HPG // 2):
                j = g * (HPG // 2) + jj
                ps = slice(2 * HD * j, 2 * HD * (j + 1))
                xs_pair = xs[:, ps]
                acc = Yo[:, 2 * HD * jj:2 * HD * (jj + 1)] + DX[:, ps]
                for e in range(2):
                    h = 2 * j + e
                    Lm = jnp.exp(jnp.where(tril, cs[:, h:h + 1] - csT[h:h + 1, :], -1e30))
                    acc = acc + _mm(G * Lm, jnp.where(first if e == 0 else ~first, xs_pair, 0.0))
                y_ref[:, ps] = acc
        zz = z_ref[...]
        gated = y_ref[...] * (zz * _sigmoid(zz))
        yn_ref[...] = _rms_fwd(gated, g_ref[...]).astype(BF)

    return pl.pallas_call(
        body, name="ssd_fwd", grid=(NC,),
        in_specs=[_rows(CH, D), ANY, _whole((8, XBC)), _whole((8, DTP)), _whole((1, SI))] + [ANY] * ng,
        out_specs=[_rows(CH, SI), _rows(CH, SI), pl.BlockSpec((1, NH * HD, NS), lambda i: (i, 0, 0)), _rows(CH, XBC),
                   _rows(CH, SI), _rows(CH, XBC), _rows(CH, DTP)] + [ANY] * ng,
        out_shape=[S((T, SI), F32), S((T, SI), BF), S((NC, NH * HD, NS), F32), S((T, XBC), F32),
                   S((T, SI), F32), S((T, XBC), F32), S((T, DTP), F32)] + [S((4,) + s.shape, s.dtype) for s in shards],
        scratch_shapes=[pltpu.VMEM((CH + 8, XBC), F32), pltpu.VMEM((NH * HD, NS), F32),
                        pltpu.VMEM((D, PC - c0), BF), pltpu.SemaphoreType.DMA((1,)),
                        pltpu.SemaphoreType.DMA((6 * ng,)), pltpu.SemaphoreType.DMA((6 * ng,))],
        compiler_params=_params(),
    )(h0, wcat, cw_ssd, hp_ssd, g_ssd, *shards)


def _outproj(ylru, yssd, x, wout, g_pm, g_pf):
    T = x.shape[0]

    def body(yl_ref, ys_ref, x_ref, w_hbm, gpm_ref, gpf_ref, mix_ref, x1_ref, h2_ref, w_vm, sem):
        _load_once([(w_hbm, w_vm)], sem)
        mix = (jnp.dot(yl_ref[...], w_vm[0:LW, :], preferred_element_type=F32)
               + jnp.dot(ys_ref[...], w_vm[LW:LW + SI, :], preferred_element_type=F32))
        mix_ref[...] = mix
        x1 = x_ref[...] + _rms_fwd(mix, gpm_ref[...])
        x1_ref[...] = x1
        h2_ref[...] = _rms_fwd(x1, gpf_ref[...]).astype(BF)

    return pl.pallas_call(
        body, name="outproj", grid=(T // TW,),
        in_specs=[_rows(TW, LW), _rows(TW, SI), _rows(TW, D), ANY, _whole((1, D)), _whole((1, D))],
        out_specs=[_rows(TW, D), _rows(TW, D), _rows(TW, D)],
        out_shape=[S((T, D), F32), S((T, D), F32), S((T, D), BF)],
        scratch_shapes=[pltpu.VMEM((LW + SI, D), BF), pltpu.SemaphoreType.DMA((1,))],
        compiler_params=_params(),
    )(ylru, yssd, x, wout, g_pm, g_pf)


def _ffn_fwd(h2, x1, tgt, wg, wu, wd, g_pff):
    T = x1.shape[0]

    def body(h2_ref, x1_ref, t_ref, wg_hbm, wu_hbm, wd_hbm, g_ref,
             gate_ref, up_ref, act_ref, df_ref, dx2_ref, st_ref, wg_vm, wu_vm, wd_vm, sem):
        _load_once([(wg_hbm, wg_vm), (wu_hbm, wu_vm), (wd_hbm, wd_vm)], sem)

        @pl.when(pl.program_id(0) == 0)
        def _():
            st_ref[...] = jnp.zeros_like(st_ref)

        h2 = h2_ref[...]
        f = None
        for c in range(FFN_PARTS):
            cols = pl.ds(c * (DFF // FFN_PARTS), DFF // FFN_PARTS)
            gate = jnp.dot(h2, wg_vm[:, cols], preferred_element_type=F32)
            up = jnp.dot(h2, wu_vm[:, cols], preferred_element_type=F32)
            gate_ref[:, cols] = gate
            up_ref[:, cols] = up
            act = (gate * _sigmoid(gate) * up).astype(BF)
            act_ref[:, cols] = act
            part = jnp.dot(act, wd_vm[cols, :], preferred_element_type=F32)
            f = part if f is None else f + part
        g = g_ref[...]
        x2 = x1_ref[...] + _rms_fwd(f, g)
        err = x2 - t_ref[...]
        st_ref[0:1, :] += 0.5 * jnp.sum(err * err, axis=0, keepdims=True) * (1.0 / D)
        dx2 = err * (1.0 / D)
        dx2_ref[...] = dx2
        df, dg = _rms_bwd(f, g, dx2)
        df_ref[...] = df.astype(BF)
        st_ref[1:2, :] += dg

    return pl.pallas_call(
        body, name="ffn_fwd", grid=(T // TT,),
        in_specs=[_rows(TT, D), _rows(TT, D), _rows(TT, D), ANY, ANY, ANY, _whole((1, D))],
        out_specs=[_rows(TT, DFF), _rows(TT, DFF), _rows(TT, DFF), _rows(TT, D), _rows(TT, D), _whole((8, D))],
        out_shape=[S((T, DFF), F32), S((T, DFF), F32), S((T, DFF), BF), S((T, D), BF), S((T, D), F32), S((8, D), F32)],
        scratch_shapes=[pltpu.VMEM((D, DFF), BF), pltpu.VMEM((D, DFF), BF), pltpu.VMEM((DFF, D), BF),
                        pltpu.SemaphoreType.DMA((3,))],
        compiler_params=_params(),
    )(h2, x1, tgt, wg, wu, wd, g_pff)


def _ffn_bwd(df, gate, up, wdT, wgT, wuT):
    T = df.shape[0]

    def body(df_ref, gate_ref, up_ref, wd_hbm, wg_hbm, wu_hbm, dgate_ref, dup_ref, dh2_ref, wd_vm, wg_vm, wu_vm, sem):
        _load_once([(wd_hbm, wd_vm), (wg_hbm, wg_vm), (wu_hbm, wu_vm)], sem)
        df = df_ref[...]
        dh2 = None
        for c in range(FFN_PARTS):
            cols = pl.ds(c * (DFF // FFN_PARTS), DFF // FFN_PARTS)
            dact = jnp.dot(df, wd_vm[:, cols], preferred_element_type=F32)
            gate = gate_ref[:, cols]
            s = _sigmoid(gate)
            dup = (dact * (gate * s)).astype(BF)
            dgate = (dact * up_ref[:, cols] * (s + gate * s * (1.0 - s))).astype(BF)
            dup_ref[:, cols] = dup
            dgate_ref[:, cols] = dgate
            part = (jnp.dot(dgate, wg_vm[cols, :], preferred_element_type=F32)
                    + jnp.dot(dup, wu_vm[cols, :], preferred_element_type=F32))
            dh2 = part if dh2 is None else dh2 + part
        dh2_ref[...] = dh2

    return pl.pallas_call(
        body, name="ffn_bwd", grid=(T // TT,),
        in_specs=[_rows(TT, D), _rows(TT, DFF), _rows(TT, DFF), ANY, ANY, ANY],
        out_specs=[_rows(TT, DFF), _rows(TT, DFF), _rows(TT, D)],
        out_shape=[S((T, DFF), BF), S((T, DFF), BF), S((T, D), F32)],
        scratch_shapes=[pltpu.VMEM((D, DFF), BF), pltpu.VMEM((DFF, D), BF), pltpu.VMEM((DFF, D), BF),
                        pltpu.SemaphoreType.DMA((3,))],
        compiler_params=_params(),
    )(df, gate, up, wdT, wgT, wuT)


def _mix_bwd(dh2, x1, dx2, mix, woutT, g_pf, g_pm):
    T = x1.shape[0]

    def body(dh2_ref, x1_ref, dx2_ref, mix_ref, w_hbm, gpf_ref, gpm_ref,
             dx1_ref, dmix_ref, dyl_ref, dys_ref, st_ref, w_vm, sem):
        _load_once([(w_hbm, w_vm)], sem)

        @pl.when(pl.program_id(0) == 0)
        def _():
            st_ref[...] = jnp.zeros_like(st_ref)

        dxa, dgpf = _rms_bwd(x1_ref[...], gpf_ref[...], dh2_ref[...])
        dx1 = dx2_ref[...] + dxa
        dx1_ref[...] = dx1
        dmix, dgpm = _rms_bwd(mix_ref[...], gpm_ref[...], dx1)
        dmix = dmix.astype(BF)
        dmix_ref[...] = dmix
        st_ref[0:1, :] += dgpf
        st_ref[1:2, :] += dgpm
        dyl_ref[...] = jnp.dot(dmix, w_vm[:, 0:LW], preferred_element_type=F32)
        dys_ref[...] = jnp.dot(dmix, w_vm[:, LW:LW + SI], preferred_element_type=F32)

    return pl.pallas_call(
        body, name="mix_bwd", grid=(T // TW,),
        in_specs=[_rows(TW, D), _rows(TW, D), _rows(TW, D), _rows(TW, D), ANY, _whole((1, D)), _whole((1, D))],
        out_specs=[_rows(TW, D), _rows(TW, D), _rows(TW, LW), _rows(TW, SI), _whole((8, D))],
        out_shape=[S((T, D), F32), S((T, D), BF), S((T, LW), F32), S((T, SI), F32), S((8, D), F32)],
        scratch_shapes=[pltpu.VMEM((D, LW + SI), BF), pltpu.SemaphoreType.DMA((1,))],
        compiler_params=_params(),
    )(dh2, x1, dx2, mix, woutT, g_pf, g_pm)


def _halo(width, n_tiles, tile):
    per = tile // 8
    return pl.BlockSpec((8, width), lambda i: (jnp.maximum((n_tiles - 1 - i) * per - 1, 0), 0))


def _lru_bwd(dy, lxr, lxc, lg, h, p_lru, wa4, wx4, wa4T, wx4T, bufs):
    T = dy.shape[0]
    NT = T // TT
    nb = len(bufs)

    def body(*refs):
        dy_ref, lxr_ref, lxc_ref, lg_ref, h_ref, hh_ref, p_ref, wa_ref, wx_ref, waT_ref, wxT_ref = refs[:11]
        b_in = refs[11:11 + nb]
        dlx_ref, dlg_ref, st_ref, dwa_ref, dwx_ref = refs[11 + nb:16 + nb]
        b_out = refs[16 + nb:16 + 2 * nb]
        hp, dp, a_s, d_s, g_s, cc, send_sems, recv_sems = refs[16 + 2 * nb:]
        for phase, step in enumerate((0, NT - 1)):
            @pl.when(pl.program_id(0) == step)
            def _():
                _pair_phase(phase, b_in, b_out, send_sems, recv_sems)

        dy = dy_ref[...]
        first = pl.program_id(0) == 0
        top = pl.program_id(0) == NT - 1

        @pl.when(first)
        def _():
            st_ref[...] = jnp.zeros_like(st_ref)
            dwa_ref[...] = jnp.zeros_like(dwa_ref)
            dwx_ref[...] = jnp.zeros_like(dwx_ref)
            dp[TT:TT + 8, :] = jnp.zeros((8, LW), F32)
            cc[...] = jnp.zeros_like(cc)

        hp[0:8, :] = hh_ref[...] * jnp.where(top, 0.0, 1.0)
        hp[8:8 + TT, :] = h_ref[...]
        lx = lxc_ref[...]
        r, i, sp, a, mult = _lru_gates(lx, p_ref, wa_ref, wx_ref)

        lg = lg_ref[...]
        hcur = h_ref[...]
        ge = _gelu(lg)
        dgated, dgn = _rms_bwd(hcur * ge, p_ref[8:9, :], dy)
        st_ref[8:9, :] += dgn
        dlg_ref[...] = (dgated * hcur * _gelu_grad(lg)).astype(BF)
        a_s[...] = a
        d_s[...] = dgated * ge

        def step(k, c):
            t = TT - 1 - k
            g = d_s[pl.ds(t, 1), :] + c
            g_s[pl.ds(t, 1), :] = g
            return a_s[pl.ds(t, 1), :] * g

        cc[0:1, :] = lax.fori_loop(0, TT, step, cc[0:1, :], unroll=8)
        gt = g_s[...]
        da = gt * hp[pl.ds(7, TT), :]
        dmult = gt * i * lx
        di = gt * mult * lx
        dlxc = gt * mult * i
        dla = da * a - dmult * (a * a) / mult
        dr = dla * (-LRU_C * sp)
        st_ref[7:8, :] += jnp.sum(dla * (-LRU_C * r), axis=0, keepdims=True) * (-_sigmoid(-p_ref[7:8, :]))
        dzr = dr * r * (1.0 - r)
        dzi = di * i * (1.0 - i)
        st_ref[5:6, :] += jnp.sum(dzr, axis=0, keepdims=True)
        st_ref[6:7, :] += jnp.sum(dzi, axis=0, keepdims=True)
        dlxc = dlxc + _blockdiag_mm(dzr, waT_ref) + _blockdiag_mm(dzi, wxT_ref)
        for j in range(4):
            sl = slice(256 * j, 256 * (j + 1))
            pa = _mm_tn(lx[:, sl], dzr[:, sl])
            px = _mm_tn(lx[:, sl], dzi[:, sl])
            for b in range(4):
                bs = slice(BW * b, BW * (b + 1))
                dwa_ref[4 * j + b] += pa[bs, bs]
                dwx_ref[4 * j + b] += px[bs, bs]
        dlx_ref[...] = _conv_bwd(dp, dlxc, lxr_ref[...], p_ref, st_ref, TT).astype(BF)

    w4 = _whole((4, 256, 256))
    return pl.pallas_call(
        body, name="lru_bwd", grid=(NT,),
        in_specs=[_rows(TT, LW, NT), _rows(TT, LW, NT), _rows(TT, LW, NT), _rows(TT, LW, NT), _rows(TT, LW, NT),
                  _halo(LW, NT, TT), _whole((16, LW)), w4, w4, w4, w4] + [ANY] * nb,
        out_specs=[_rows(TT, LW, NT), _rows(TT, LW, NT), _whole((16, LW)), _whole((NBLK, BW, BW)), _whole((NBLK, BW, BW))]
        + [ANY] * nb,
        out_shape=[S((T, LW), BF), S((T, LW), BF), S((16, LW), F32), S((NBLK, BW, BW), F32), S((NBLK, BW, BW), F32)]
        + [S(_half_shape(b), b.dtype) for b in bufs],
        scratch_shapes=[pltpu.VMEM((TT + 8, LW), F32), pltpu.VMEM((TT + 8, LW), F32),
                        pltpu.VMEM((TT, LW), F32), pltpu.VMEM((TT, LW), F32), pltpu.VMEM((TT, LW), F32),
                        pltpu.VMEM((8, LW), F32), pltpu.SemaphoreType.DMA((nb,)), pltpu.SemaphoreType.DMA((nb,))],
        compiler_params=_params(),
    )(dy, lxr, lxc, lg, h, h, p_lru, wa4, wx4, wa4T, wx4T, *bufs)


def _ssd_bwd(dyn, xbcr, cv, z, dtr, y, states, cw_ssd, hp_ssd, g_ssd, parts):
    T = dyn.shape[0]
    NC = T // CH
    nq = len(parts)

    def body(*refs):
        dyn_ref, xr_ref, cv_ref, z_ref, dt_ref, y_ref, st_ref, cw_ref, hp_ref, g_ref = refs[:10]
        q_in = refs[10:10 + nq]
        dxbc_ref, dz_ref, ddt_ref, cst_ref, hst_ref, gst_ref = refs[10 + nq:16 + nq]
        q_out = refs[16 + nq:16 + 2 * nq]
        dp, dS, send_sems, recv_sems = refs[16 + 2 * nq:]
        dyn = dyn_ref[...]
        first = pl.program_id(0) == 0
        for phase, step in enumerate((0, NC - 1)):
            @pl.when(pl.program_id(0) == step)
            def _():
                _quad_phase(phase, q_in, q_out, send_sems, recv_sems)

        @pl.when(first)
        def _():
            cst_ref[...] = jnp.zeros_like(cst_ref)
            hst_ref[...] = jnp.zeros_like(hst_ref)
            gst_ref[...] = jnp.zeros_like(gst_ref)
            dp[CH:CH + 8, :] = jnp.zeros((8, XBC), F32)
            dS[...] = jnp.zeros_like(dS)

        cv = cv_ref[...]
        sg, xbc, raw, dtv, A, cs = _ssd_prep(cv, dt_ref, hp_ref)
        csT, dsm, E_x, ds_x, El_rows = _ssd_decays(cs)
        row_i = lax.broadcasted_iota(jnp.int32, (CH, CH), 0)
        col_i = lax.broadcasted_iota(jnp.int32, (CH, CH), 1)
        tril = row_i >= col_i
        first = col_i < HD
        head_of = ((lax.broadcasted_iota(jnp.int32, (DTP, SI), 1) >> 6)
                   == lax.broadcasted_iota(jnp.int32, (DTP, SI), 0)).astype(BF)
        head_ofT = ((lax.broadcasted_iota(jnp.int32, (SI, DTP), 0) >> 6)
                    == lax.broadcasted_iota(jnp.int32, (SI, DTP), 1)).astype(BF)

        def hi_lo(v):
            hi = v.astype(BF)
            return hi, (v - hi.astype(F32)).astype(BF)

        GW = HPG * HD

        def lane_sums(v, g):
            hi, lo = hi_lo(v)
            w = head_ofT[GW * g:GW * (g + 1), :]
            return _mm(hi, w) + _mm(lo, w)

        zz = z_ref[...]
        sz = _sigmoid(zz)
        yv = y_ref[...]
        dgn, dg = _rms_bwd(yv * (zz * sz), g_ref[...], dyn)
        gst_ref[0:1, :] += dg
        dz_ref[...] = (dgn * yv * (sz + zz * sz * (1.0 - sz))).astype(BF)
        dY = dgn * (zz * sz)

        X = xbc[:, 0:SI]
        dsilu = sg + cv * sg * (1.0 - sg)
        dt_x = _per_head_lanes(dtv)
        xs = X * dt_x
        xsd = (xs * ds_x).astype(BF)
        D_x = _per_head_lanes(hp_ref[...])[2:3, :]
        zero = jnp.zeros((CH, DTP), F32)
        dcs_col = zero
        dcs_row = zero
        dds, ddt_col, dD_rows, dcl_rows = zero, zero, zero, zero
        for g in range(NG):
            gs = slice(GW * g, GW * (g + 1))
            Bg = xbc[:, SI + NS * g:SI + NS * (g + 1)].astype(BF)
            Cg = xbc[:, SI + NG * NS + NS * g:SI + NG * NS + NS * (g + 1)].astype(BF)
            G = _mm_nt(Cg, Bg)
            Sg = st_ref[0, gs, :]
            dSe = dS[gs, :]
            dYg = dY[:, gs]
            dcs_col = dcs_col + lane_sums(dYg * (_mm_nt(Cg, Sg) * E_x[:, gs]), g)
            dD_rows = dD_rows + lane_sums(dYg * X[:, gs], g)
            dP = dYg * E_x[:, gs]
            dCg = _mm(dP, Sg)
            dS[gs, :] = _mm_tn(dP, Cg) + _per_head_rows(El_rows, g) * dSe
            t_hi, t_lo = hi_lo(dSe * Sg)
            dcl_rows = dcl_rows + _mm(head_of[:, gs], t_hi) + _mm(head_of[:, gs], t_lo)
            Q = _mm_nt(Bg, dSe)
            dds = dds + lane_sums(Q * xs[:, gs], g)
            dBg = _mm(xsd[:, gs], dSe)
            dG = jnp.zeros((CH, CH), F32)
            dxs_pairs = []
            for jj in range(HPG // 2):
                j = g * (HPG // 2) + jj
                ps = slice(2 * HD * j, 2 * HD * (j + 1))
                xs_pair = xs[:, ps]
                dxs_pair = Q[:, 2 * HD * jj:2 * HD * (jj + 1)] * ds_x[:, ps]
                for e in range(2):
                    h = 2 * j + e
                    Lm = jnp.exp(jnp.where(tril, cs[:, h:h + 1] - csT[h:h + 1, :], -1e30))
                    M = G * Lm
                    dYm = jnp.where(first if e == 0 else ~first, dY[:, ps], 0.0).astype(BF)
                    dM = _mm_nt(dYm, xs_pair)
                    dxs_pair = dxs_pair + _mm_tn(M, dYm)
                    Wm = dM * M
                    dcs_col = dcs_col + jnp.where(col_i == h, jnp.sum(Wm, axis=1, keepdims=True), 0.0)
                    dcs_row = dcs_row + jnp.where(row_i == h, -jnp.sum(Wm, axis=0, keepdims=True), 0.0)
                    dG = dG + dM * Lm
                dp[0:CH, ps] = (D_x[:, ps] * dY[:, ps] + dxs_pair * dt_x[:, ps]) * dsilu[:, ps]
                dxs_pairs.append(dxs_pair)
            ddt_col = ddt_col + lane_sums(jnp.concatenate(dxs_pairs, axis=1) * X[:, gs], g)
            bs = slice(SI + NS * g, SI + NS * (g + 1))
            cs_ = slice(SI + NG * NS + NS * g, SI + NG * NS + NS * (g + 1))
            dp[0:CH, bs] = (dBg + _mm_tn(dG, Cg)) * dsilu[:, bs]
            dp[0:CH, cs_] = (dCg + _mm(dG, Bg)) * dsilu[:, cs_]

        dds = dds * dsm
        dcs_col = dcs_col - dds
        dD = jnp.sum(dD_rows, axis=0, keepdims=True)
        dcl_rows = jnp.sum(dcl_rows, axis=1, keepdims=True) * jnp.exp(csT[:, CH - 1:CH])
        dcs_row = dcs_row + jnp.where(col_i == CH - 1, dcl_rows, 0.0)
        dcs_col = dcs_col + jnp.where(row_i == CH - 1, jnp.sum(dds, axis=0, keepdims=True), 0.0)

        da = _rev_cumsum_rows(dcs_col + dcs_row.T, CH)
        ddt_col = ddt_col + da * A
        hst_ref[1:2, :] += jnp.sum(da * dtv, axis=0, keepdims=True) * A
        hst_ref[2:3, :] += dD
        draw = jnp.where(col_i < NH, ddt_col * _sigmoid(raw), 0.0)
        ddt_ref[...] = draw.astype(BF)
        hst_ref[0:1, :] += jnp.sum(draw, axis=0, keepdims=True)

        dxbc_ref[...] = _conv_bwd(dp, None, xr_ref[...], cw_ref, cst_ref, CH).astype(BF)

    return pl.pallas_call(
        body, name="ssd_bwd", grid=(NC,),
        in_specs=[_rows(CH, SI, NC), _rows(CH, XBC, NC), _rows(CH, XBC, NC), _rows(CH, SI, NC), _rows(CH, DTP, NC),
                  _rows(CH, SI, NC), pl.BlockSpec((1, NH * HD, NS), lambda i: (NC - 1 - i, 0, 0)),
                  _whole((8, XBC)), _whole((8, DTP)), _whole((1, SI))] + [ANY] * nq,
        out_specs=[_rows(CH, XBC, NC), _rows(CH, SI, NC), _rows(CH, DTP, NC), _whole((16, XBC)), _whole((16, DTP)),
                   _whole((8, SI))] + [ANY] * nq,
        out_shape=[S((T, XBC), BF), S((T, SI), BF), S((T, DTP), BF), S((16, XBC), F32), S((16, DTP), F32), S((8, SI), F32)]
        + [S(p.shape, p.dtype) for p in parts],
        scratch_shapes=[pltpu.VMEM((CH + 8, XBC), F32), pltpu.VMEM((NH * HD, NS), F32),
                        pltpu.SemaphoreType.DMA((3 * nq,)), pltpu.SemaphoreType.DMA((3 * nq,))],
        compiler_params=_params(),
    )(dyn, xbcr, cv, z, dtr, y, states, cw_ssd, hp_ssd, g_ssd, *parts)


def _inproj_bwd(dlx, dlg, dz, dxbc, ddt, x, dx1, wcatT, g0, parts):
    T = x.shape[0]
    NT = T // TW
    nq = len(parts)

    def body(*refs):
        dlx_ref, dlg_ref, dz_ref, dxbc_ref, ddt_ref, x_ref, dx1_ref, w_hbm, g_ref = refs[:9]
        q_in = refs[9:9 + nq]
        dx_ref, st_ref = refs[9 + nq:11 + nq]
        q_out = refs[11 + nq:11 + 2 * nq]
        w_vm, sem, send_sems, recv_sems = refs[11 + 2 * nq:]
        _load_once([(w_hbm, w_vm)], sem)
        for phase, step in enumerate((0, NT - 1)):
            @pl.when(pl.program_id(0) == step)
            def _():
                _quad_phase(phase, q_in, q_out, send_sems, recv_sems)

        @pl.when(pl.program_id(0) == 0)
        def _():
            st_ref[...] = jnp.zeros_like(st_ref)

        dh = jnp.dot(dlx_ref[...], w_vm[0:1024, :], preferred_element_type=F32)
        dh = dh + jnp.dot(dlg_ref[...], w_vm[1024:2048, :], preferred_element_type=F32)
        dh = dh + jnp.dot(dz_ref[...], w_vm[2048:3072, :], preferred_element_type=F32)
        dh = dh + jnp.dot(dxbc_ref[...], w_vm[3072:3072 + XBC, :], preferred_element_type=F32)
        dh = dh + jnp.dot(ddt_ref[...], w_vm[3072 + XBC:PC, :], preferred_element_type=F32)
        dx, dg = _rms_bwd(x_ref[...], g_ref[...], dh)
        dx_ref[...] = dx1_ref[...] + dx
        st_ref[0:1, :] += dg

    return pl.pallas_call(
        body, name="inproj_bwd", grid=(NT,),
        in_specs=[_rows(TW, 1024), _rows(TW, 1024), _rows(TW, 1024), _rows(TW, XBC), _rows(TW, DTP), _rows(TW, D),
                  _rows(TW, D), ANY, _whole((1, D))] + [ANY] * nq,
        out_specs=[_rows(TW, D), _whole((8, D))] + [ANY] * nq,
        out_shape=[S((T, D), F32), S((8, D), F32)] + [S((4,) + p.shape[-2:], p.dtype) for p in parts],
        scratch_shapes=[pltpu.VMEM((PC, D), BF), pltpu.SemaphoreType.DMA((1,)),
                        pltpu.SemaphoreType.DMA((3 * nq,)), pltpu.SemaphoreType.DMA((3 * nq,))],
        compiler_params=_params(),
    )(dlx, dlg, dz, dxbc, ddt, x, dx1, wcatT, g0, *parts)


def _wgrad(name, a, b):
    T, M = a.shape
    N = b.shape[1]
    tk = min(T, 2048 if M <= 1024 else 1024)
    tn = N
    while M * tn * 4 > (6 << 20) and tn % 256 == 0:
        tn //= 2

    def body(a_ref, b_ref, o_ref):
        @pl.when(pl.program_id(1) == 0)
        def _():
            o_ref[...] = jnp.zeros_like(o_ref)

        o_ref[...] += lax.dot_general(a_ref[...], b_ref[...], (((0,), (0,)), ((), ())), preferred_element_type=F32)

    return pl.pallas_call(
        body, name=name, grid=(N // tn, T // tk),
        in_specs=[pl.BlockSpec((tk, M), lambda j, k: (k, 0)), pl.BlockSpec((tk, tn), lambda j, k: (k, j))],
        out_specs=pl.BlockSpec((M, tn), lambda j, k: (0, j)), out_shape=S((M, N), F32),
        compiler_params=_params(2),
    )(a, b)


def _adamw(name, w, g, m, v):
    _, R, C = w.shape

    def body(w_ref, g_ref, m_ref, v_ref, d_ref, nm_ref, nv_ref):
        d_ref[0], nm_ref[0], nv_ref[0] = _adam_math(w_ref[0], g_ref[...], m_ref[0], v_ref[0])

    if R % 8 == 0:
        tr = _row_tile(R, C)
        n_tiles = R // tr
        blk, gblk = pl.BlockSpec((1, tr, C), lambda i: (0, i, 0)), pl.BlockSpec((tr, C), lambda i: (i, 0))
    else:
        tc = 128 * max(k for k in range(1, C // 128 + 1) if C % (128 * k) == 0 and R * 128 * k * 4 <= (5 << 19))
        n_tiles = C // tc
        blk, gblk = pl.BlockSpec((1, R, tc), lambda i: (0, 0, i)), pl.BlockSpec((R, tc), lambda i: (0, i))
    return pl.pallas_call(
        body, name=name, grid=(n_tiles,),
        in_specs=[blk, gblk, blk, blk], out_specs=[blk] * 3,
        out_shape=[S((1, R, C), F32)] * 3, compiler_params=_params(),
    )(w, g, m, v)


def _pair_exchange(name, bufs):
    n = len(bufs)

    def body(*refs):
        for phase in range(2):
            _pair_phase(phase, refs[:n], refs[n:2 * n], refs[2 * n], refs[2 * n + 1])

    return pl.pallas_call(
        body, name=name, in_specs=[ANY] * n, out_specs=[ANY] * n,
        out_shape=[S(_half_shape(b), b.dtype) for b in bufs],
        scratch_shapes=[pltpu.SemaphoreType.DMA((n,)), pltpu.SemaphoreType.DMA((n,))],
    )(*bufs)


def _quad_exchange(bufs):
    n = len(bufs)

    def body(*refs):
        ins, outs = refs[:n], refs[n:2 * n]
        send_sems, recv_sems = refs[2 * n], refs[2 * n + 1]
        x, y, c = _pos()
        me = 2 * x + y
        chips = _other_chips(x, y)
        copies = []
        for k, (src, dst) in enumerate(zip(ins, outs)):
            for j, (cx, cy) in enumerate(chips):
                cp = _remote(src, dst.at[me], send_sems.at[3 * k + j], recv_sems.at[3 * k + j], (cx, cy, c))
                cp.start()
                copies.append(cp)
        for k, (src, dst) in enumerate(zip(ins, outs)):
            for j, (cx, cy) in enumerate(chips):
                blk = dst.at[2 * cx + cy]
                _remote(blk, blk, send_sems.at[3 * k + j], recv_sems.at[3 * k + j], (cx, cy, c)).wait_recv()
        for cp in copies:
            cp.wait_send()

    return pl.pallas_call(
        body, name="quad_exchange", in_specs=[ANY] * n, out_specs=[ANY] * n,
        out_shape=[S((4,) + b.shape, b.dtype) for b in bufs],
        scratch_shapes=[pltpu.SemaphoreType.DMA((3 * n,)), pltpu.SemaphoreType.DMA((3 * n,))],
    )(*bufs)


def _pair_gather(bufs):
    n = len(bufs)

    def body(*refs):
        ins, outs = refs[:n], refs[n:2 * n]
        send_sems, recv_sems = refs[2 * n], refs[2 * n + 1]
        x, y, c = _pos()
        copies = []
        for k, buf in enumerate(outs):
            mine = _half(buf, c, buf.shape[0] // 2)
            cp = _remote(mine, mine, send_sems.at[k], recv_sems.at[k], (x, y, 1 - c))
            cp.start()
            copies.append(cp)
        for k, buf in enumerate(outs):
            theirs = _half(buf, 1 - c, buf.shape[0] // 2)
            _remote(theirs, theirs, send_sems.at[k], recv_sems.at[k], (x, y, 1 - c)).wait_recv()
        for cp in copies:
            cp.wait_send()

    return pl.pallas_call(
        body, name="pair_gather", in_specs=[ANY] * n, out_specs=[ANY] * n,
        out_shape=[S(b.shape, b.dtype) for b in bufs], input_output_aliases={k: k for k in range(n)},
        scratch_shapes=[pltpu.SemaphoreType.DMA((n,)), pltpu.SemaphoreType.DMA((n,))],
    )(*bufs)


def _row_tile(rows, cols, mult=8):
    best = mult
    for t in range(mult, rows + 1, mult):
        if rows % t == 0 and t * cols * 4 <= (1 << 21):
            best = t
    return best


def _add_own_half(name, full, got, c, out_dtype, by_columns):
    hr = got.shape[-2]
    wide = got.shape[-1]
    cols = wide // 4 if by_columns else wide
    tr = _row_tile(hr, wide, 16)
    per = hr // tr

    if by_columns:
        def body(c_ref, a_ref, b_ref, o_ref):
            v = a_ref[...] + b_ref[...]
            for j in range(4):
                o_ref[j] = v[:, j * cols:(j + 1) * cols].astype(out_dtype)

        in_specs = [pl.BlockSpec((tr, wide), lambda i, c_ref: (c_ref[0] * per + i, 0)),
                    pl.BlockSpec((tr, wide), lambda i, c_ref: (i, 0))]
        out_specs = pl.BlockSpec((4, tr, cols), lambda i, c_ref: (0, i, 0))
        grid = (per,)
    else:
        def body(c_ref, a_ref, b_ref, o_ref):
            o_ref[...] = (a_ref[...] + b_ref[...]).astype(out_dtype)

        in_specs = [pl.BlockSpec((1, tr, cols), lambda s, i, c_ref: (s, c_ref[0] * per + i, 0)),
                    pl.BlockSpec((1, tr, cols), lambda s, i, c_ref: (s, i, 0))]
        out_specs = pl.BlockSpec((1, tr, cols), lambda s, i, c_ref: (s, i, 0))
        grid = (4, per)
    return pl.pallas_call(
        body, name=name,
        grid_spec=pltpu.PrefetchScalarGridSpec(num_scalar_prefetch=1, grid=grid, in_specs=in_specs, out_specs=out_specs),
        out_shape=S((4, hr, cols), out_dtype), compiler_params=_params(len(grid)),
    )(jnp.reshape(c, (1,)).astype(jnp.int32), full, got)


def _small_add_own_half(name, fulls, gots, c):
    n = len(fulls)

    def body(c_ref, *refs):
        for a_ref, b_ref, o_ref in zip(refs[:n], refs[n:2 * n], refs[2 * n:]):
            hr = b_ref.shape[0]
            o_ref[...] = a_ref[pl.ds(pl.multiple_of(c_ref[0] * hr, 8), hr), :] + b_ref[...]

    specs = lambda arrs: [pl.BlockSpec(a.shape, lambda i, c_ref: (0, 0)) for a in arrs]
    return pl.pallas_call(
        body, name=name,
        grid_spec=pltpu.PrefetchScalarGridSpec(num_scalar_prefetch=1, grid=(1,), in_specs=specs(fulls) + specs(gots),
                                               out_specs=specs(gots)),
        out_shape=[S(g.shape, F32) for g in gots], compiler_params=_params(),
    )(jnp.reshape(c, (1,)).astype(jnp.int32), *fulls, *gots)


def _small_sum_slots(name, own, slots, me, c):
    n = len(slots)

    def body(p_ref, *refs):
        own_refs, slot_refs, o_refs = refs[:n], refs[n:5 * n], refs[5 * n:]
        for i, (own_ref, o_ref) in enumerate(zip(own_refs, o_refs)):
            hr = own_ref.shape[0]
            acc = None
            for j in range(4):
                v = jnp.where(p_ref[0] == j, own_ref[...], slot_refs[4 * i + j][0])
                acc = v if acc is None else acc + v
            o_ref[pl.ds(pl.multiple_of(p_ref[1] * hr, 8), hr), :] = acc

    def slot_spec(s, j):
        return pl.BlockSpec((1,) + s.shape[1:], lambda i, p: (jnp.where(p[0] == j, (j + 1) % 4, j), 0, 0))

    outs = [S((2 * s.shape[1], s.shape[2]), F32) for s in slots]
    return pl.pallas_call(
        body, name=name,
        grid_spec=pltpu.PrefetchScalarGridSpec(
            num_scalar_prefetch=1, grid=(1,),
            in_specs=[pl.BlockSpec(o.shape, lambda i, p: (0, 0)) for o in own]
            + [slot_spec(s, j) for s in slots for j in range(4)],
            out_specs=[pl.BlockSpec(o.shape, lambda i, p: (0, 0)) for o in outs]),
        out_shape=outs, compiler_params=_params(),
    )(jnp.stack([me, c]).astype(jnp.int32), *own, *[s for s in slots for _ in range(4)])


def _sum_slots(name, own, slots, me, c):
    _, rows, cols = slots.shape
    tr = _row_tile(rows, cols, 16 if slots.dtype == jnp.bfloat16 else 8)
    per = rows // tr
    three = len(own.shape) == 3

    def body(p_ref, own_ref, s0, s1, s2, s3, o_ref):
        mine = own_ref[0] if three else own_ref[...]
        acc = None
        for j, s_ref in enumerate((s0, s1, s2, s3)):
            v = jnp.where(p_ref[0] == j, mine, s_ref[0]).astype(F32)
            acc = v if acc is None else acc + v
        o_ref[...] = acc

    def slot_spec(j):
        return pl.BlockSpec((1, tr, cols), lambda i, p: (jnp.where(p[0] == j, (j + 1) % 4, j), i, 0))

    own_spec = (pl.BlockSpec((1, tr, cols), lambda i, p: (p[0], i, 0)) if three
                else pl.BlockSpec((tr, cols), lambda i, p: (i, 0)))
    return pl.pallas_call(
        body, name=name,
        grid_spec=pltpu.PrefetchScalarGridSpec(
            num_scalar_prefetch=1, grid=(per,), in_specs=[own_spec] + [slot_spec(j) for j in range(4)],
            out_specs=pl.BlockSpec((tr, cols), lambda i, p: (p[1] * per + i, 0))),
        out_shape=S((2 * rows, cols), F32), compiler_params=_params(),
    )(jnp.stack([me, c]).astype(jnp.int32), own, slots, slots, slots, slots)


BIG = ("w_in", "w_out", "w_gate", "w_up", "w_down")
ROW_PARAMS = (("pre_mix_norm", 0), ("lru_conv_b", 12), ("lru_ba", 13), ("lru_bx", 14), ("lru_lambda", 15),
              ("lru_out_norm", 16), ("ssd_out_norm", 24), ("post_mix_norm", 33), ("pre_ffn_norm", 32), ("post_ffn_norm", 41))
LRU_CONV_ROWS = (8, 12)
LOSS_ROW = 40
HEAD_PARAMS = (("ssd_dt_bias", 0), ("ssd_a_log", 1), ("ssd_d", 2))
SMALL = tuple(n for n, _ in ROW_PARAMS) + ("ssd_conv_b",) + tuple(n for n, _ in HEAD_PARAMS) + (
    "lru_wa", "lru_wx", "lru_conv_w", "ssd_conv_w")


def _diag4(w):
    eye = jnp.eye(4, dtype=w.dtype).reshape(1, 4, 1, 4, 1)
    return (w.reshape(4, 4, BW, 1, BW) * eye).reshape(4, 4 * BW, 4 * BW)


def _adam_math(w, g, m, v):
    mm = ADAM_B1 * m + (1.0 - ADAM_B1) * g
    vv = ADAM_B2 * v + (1.0 - ADAM_B2) * (g * g)
    c1 = 1.0 - ADAM_B1 ** ADAM_STEP
    c2 = 1.0 - ADAM_B2 ** ADAM_STEP
    return -ADAM_LR * ((mm / c1) / (jnp.sqrt(vv / c2) + ADAM_EPS) + ADAM_WD * w), mm, vv


def _adamw_small(rows, cst, hst, dwa, dwx, glcw, gscw, w, m, v):
    def grad_of(name, refs):
        rows_ref, cst_ref, hst_ref, dwa_ref, dwx_ref, glcw_ref, gscw_ref = refs
        for n, r in ROW_PARAMS:
            if n == name:
                return rows_ref[r:r + 1, :]
        for n, r in HEAD_PARAMS:
            if n == name:
                return hst_ref[r:r + 1, 0:NH]
        return {"ssd_conv_b": lambda: cst_ref[4:5, :], "lru_wa": lambda: dwa_ref[...], "lru_wx": lambda: dwx_ref[...],
                "lru_conv_w": lambda: glcw_ref[...], "ssd_conv_w": lambda: gscw_ref[...]}[name]()

    shapes = {n: (w[n].shape[1:] if len(w[n].shape) > 2 else w[n].shape) for n in SMALL}
    flat = lambda d: [d[n].reshape(shapes[n]) for n in SMALL]
    ns = len(SMALL)

    def body(*refs):
        srcs, rest = refs[:7], refs[7:]
        w_refs, m_refs, v_refs = rest[:ns], rest[ns:2 * ns], rest[2 * ns:3 * ns]
        outs = rest[3 * ns:]
        for k, name in enumerate(SMALL):
            g = grad_of(name, srcs)
            d, mm, vv = _adam_math(w_refs[k][...], g, m_refs[k][...], v_refs[k][...])
            outs[4 * k][...] = g
            outs[4 * k + 1][...] = d
            outs[4 * k + 2][...] = mm
            outs[4 * k + 3][...] = vv

    res = pl.pallas_call(
        body, name="adamw_small",
        out_shape=[S(shapes[n], F32) for n in SMALL for _ in range(4)],
        compiler_params=pltpu.CompilerParams(vmem_limit_bytes=VMEM_LIMIT),
    )(rows, cst, hst, dwa, dwx, glcw, gscw, *flat(w), *flat(m), *flat(v))
    return {n: tuple(res[4 * k + i].reshape(w[n].shape) for i in range(4)) for k, n in enumerate(SMALL)}


def _with_own(own, got):
    chip = 2 * lax.axis_index("x") + lax.axis_index("y")
    return jnp.where((jnp.arange(4) == chip).reshape(4, 1, 1), own[None], got)


def _side_by_side(f):
    return f.transpose(1, 0, 2).reshape(f.shape[1], 4 * f.shape[2])


def _stacked(f):
    return f.reshape(4 * f.shape[1], f.shape[2])


def _conv_terms(lru_conv_w, ssd_conv_w):
    conv = jnp.concatenate([lru_conv_w.reshape(-1), ssd_conv_w.reshape(-1)]).astype(F32)
    hi = conv.astype(jnp.bfloat16)
    mid = (conv - hi.astype(F32)).astype(jnp.bfloat16)
    lo = (conv - hi.astype(F32) - mid.astype(F32)).astype(jnp.bfloat16)
    terms = jnp.concatenate([hi, mid, lo])
    rows = -(-terms.shape[0] // (128 * 32)) * 32
    return jnp.pad(terms, (0, rows * 128 - terms.shape[0])).reshape(rows, 128)


def _full_conv_taps(own, got, n_lru, n_ssd):
    n_terms = 3 * (n_lru + n_ssd)
    t3 = _with_own(own, got).reshape(4, -1)[:, :n_terms].reshape(4, 3, -1).astype(F32)
    conv_f = (t3[:, 0] + t3[:, 1]) + t3[:, 2]
    lcw = conv_f[:, :n_lru].reshape(4, CONV_K, -1).transpose(1, 0, 2).reshape(CONV_K, LW)
    scw = conv_f[:, n_lru:].reshape(4, CONV_K, -1).transpose(1, 0, 2).reshape(CONV_K, XBC)
    return lcw, scw


def _step(x, tgt, w_in, lru_conv_w, ssd_conv_w, sp, late):
    c = lax.axis_index("c")
    me = 2 * lax.axis_index("x") + lax.axis_index("y")
    mm = lambda w: w.astype(BF)
    row = lambda v: v.reshape(1, -1).astype(F32)
    g0 = row(sp["pre_mix_norm"])
    first = [w_in.astype(WIRE), _conv_terms(lru_conv_w, ssd_conv_w)]
    h0, *got_first = _prenorm(x, g0, first)
    win_f = _side_by_side(_with_own(first[0], got_first[0]))
    lcw, scw = _full_conv_taps(first[1], got_first[1], lru_conv_w.size, ssd_conv_w.size)
    wcat = jnp.concatenate([mm(win_f), jnp.zeros((D, PC - IN_COLS), BF)], axis=1)
    p_lru = jnp.concatenate([lcw, row(sp["lru_conv_b"]), row(sp["lru_ba"]), row(sp["lru_bx"]), row(sp["lru_lambda"]),
                             row(sp["lru_out_norm"]), jnp.zeros((7, LW), F32)], axis=0)
    wa4, wx4 = mm(_diag4(sp["lru_wa"][0])), mm(_diag4(sp["lru_wx"][0]))
    wa4T, wx4T = wa4.transpose(0, 2, 1), wx4.transpose(0, 2, 1)
    cw_ssd = jnp.concatenate([scw, row(sp["ssd_conv_b"]), jnp.zeros((3, XBC), F32)], axis=0)
    padh = lambda v: jnp.pad(row(v), ((0, 0), (0, DTP - NH)))
    hp_ssd = jnp.concatenate([padh(sp["ssd_dt_bias"]), padh(sp["ssd_a_log"]), padh(sp["ssd_d"]), jnp.zeros((5, DTP), F32)], axis=0)
    g_ssd = row(sp["ssd_out_norm"])
    g_pm, g_pf, g_pff = row(sp["post_mix_norm"]), row(sp["pre_ffn_norm"]), row(sp["post_ffn_norm"])

    h, ylru, lxc, lxr, lg, *got_a = _lru_fwd(h0, wcat, p_lru, wa4, wx4, [late[0], late[3]])
    y, yssd, states, cv, z, xbcr, dtr, *got_b = _ssd_fwd(h0, wcat, cw_ssd, hp_ssd, g_ssd, [late[1], late[2]])
    wout, wd = mm(_stacked(_with_own(late[0], got_a[0]))), mm(_stacked(_with_own(late[3], got_a[1])))
    wg, wu = mm(_side_by_side(_with_own(late[1], got_b[0]))), mm(_side_by_side(_with_own(late[2], got_b[1])))
    mix, x1, h2 = _outproj(ylru, yssd, x, wout, g_pm, g_pf)
    gate, up, act, df, dx2, st_ffn = _ffn_fwd(h2, x1, tgt, wg, wu, wd, g_pff)
    dgate, dup, dh2 = _ffn_bwd(df, gate, up, wd.T, wg.T, wu.T)
    dx1, dmix, dyl, dys, st_mix = _mix_bwd(dh2, x1, dx2, mix, wout.T, g_pf, g_pm)

    dwg = _wgrad("wgrad_gate", h2, dgate)
    dwu = _wgrad("wgrad_up", h2, dup)
    dwd = _wgrad("wgrad_down", act, df)
    dwo = jnp.concatenate([_wgrad("wgrad_out_lru", ylru, dmix), _wgrad("wgrad_out_ssd", yssd, dmix)], axis=0)
    early = [dwo.reshape(4, (LW + SI) // 4, D), dwg, dwu, dwd.reshape(4, DFF // 4, D)]
    dlx, dlg, st_lru, dwa, dwx, *got_early = _lru_bwd(dyl, lxr, lxc, lg, h, p_lru, wa4, wx4, wa4T, wx4T, early)
    part_early = [_add_own_half("pair_add_early%d" % k, b, r, c, WIRE, bc)
                  for k, (b, r, bc) in enumerate(zip(early, got_early, [False, True, True, False]))]
    dxbc, dz, ddt, cst, hst, gst, *slots_early = _ssd_bwd(dys, xbcr, cv, z, dtr, y, states, cw_ssd, hp_ssd, g_ssd,
                                                          part_early)
    red_early = [_sum_slots("quad_sum_early%d" % k, p, s, me, c) for k, (p, s) in enumerate(zip(part_early, slots_early))]

    pin = [_wgrad("wgrad_in_%d" % k, h0, b) for k, b in enumerate((dlx, dlg, dz, dxbc, ddt))]
    dwin = jnp.concatenate(pin[:4] + [pin[4][:, :NH]], axis=1)
    small4 = [cst, hst, dwa.reshape(NBLK * BW, BW), dwx.reshape(NBLK * BW, BW)]
    got_win, *got_small4 = _pair_exchange("pair_exchange_w_in", [dwin] + small4)
    part_win = _add_own_half("pair_add_w_in", dwin, got_win, c, WIRE, True)
    part_small4 = list(_small_add_own_half("small_pair_add", small4, got_small4, c))
    gx, st_in, slots_win, *slots_small4 = _inproj_bwd(dlx, dlg, dz, dxbc, ddt, x, dx1, wcat.T, g0,
                                                      [part_win] + part_small4)
    red_win = _sum_slots("quad_sum_w_in", part_win, slots_win, me, c)
    red_small4 = list(_small_sum_slots("small_quad_sum", part_small4, slots_small4, me, c))

    rows = jnp.concatenate([st_in, st_lru, gst, st_mix, st_ffn], axis=0)
    part_rows = list(_small_add_own_half("rows_pair_add", [rows], list(_pair_exchange("pair_exchange_rows", [rows])), c))
    red_rows = list(_small_sum_slots("rows_quad_sum", part_rows, list(_quad_exchange(part_rows)), me, c))
    out = list(_pair_gather([red_win] + red_early + red_rows + red_small4))
    big = dict(zip(("w_in", "w_out", "w_gate", "w_up", "w_down"), out[:5]))
    return gx, big, out[5:]


def kernel(x, pre_mix_norm, w_in, lru_conv_w, lru_conv_b, lru_wa, lru_ba, lru_wx, lru_bx, lru_lambda, lru_out_norm, ssd_conv_w, ssd_conv_b, ssd_dt_bias, ssd_a_log, ssd_d, ssd_out_norm, w_out, post_mix_norm, pre_ffn_norm, w_gate, w_up, w_down, post_ffn_norm, loss_target, m_pre_mix_norm, m_w_in, m_lru_conv_w, m_lru_conv_b, m_lru_wa, m_lru_ba, m_lru_wx, m_lru_bx, m_lru_lambda, m_lru_out_norm, m_ssd_conv_w, m_ssd_conv_b, m_ssd_dt_bias, m_ssd_a_log, m_ssd_d, m_ssd_out_norm, m_w_out, m_post_mix_norm, m_pre_ffn_norm, m_w_gate, m_w_up, m_w_down, m_post_ffn_norm, v_pre_mix_norm, v_w_in, v_lru_conv_w, v_lru_conv_b, v_lru_wa, v_lru_ba, v_lru_wx, v_lru_bx, v_lru_lambda, v_lru_out_norm, v_ssd_conv_w, v_ssd_conv_b, v_ssd_dt_bias, v_ssd_a_log, v_ssd_d, v_ssd_out_norm, v_w_out, v_post_mix_norm, v_pre_ffn_norm, v_w_gate, v_w_up, v_w_down, v_post_ffn_norm):
    args = dict(locals())
    names = list(SMALL) + list(BIG)
    w = {n: args[n] for n in names}
    m = {n: args["m_" + n] for n in names}
    v = {n: args["v_" + n] for n in names}
    chip = 2 * lax.axis_index("x") + lax.axis_index("y")

    late = [a[0].astype(WIRE) for a in (w_out, w_gate, w_up, w_down)]
    gx, red, (rows, cst, hst, dwa, dwx) = _step(x[0], loss_target[0], w_in[0], lru_conv_w[0], ssd_conv_w[0],
                                                {n: w[n] for n in SMALL}, late)
    loss = jnp.sum(rows[LOSS_ROW])

    grads, delta, new_m, new_v = {}, {}, {}, {}
    for n in BIG:
        g = red[n]
        if n in ("w_in", "w_gate", "w_up"):
            t = lambda a: jnp.swapaxes(a, 1, 2)
            gt = g.T
            out = _adamw("adamw_" + n, t(w[n]), gt, t(m[n]), t(v[n]))
            delta[n], new_m[n], new_v[n] = (t(o) for o in out)
            grads[n] = t(gt[None])
        else:
            delta[n], new_m[n], new_v[n] = _adamw("adamw_" + n, w[n], g, m[n], v[n])
            grads[n] = g[None]

    lc, sc = lru_conv_w.shape[-1], ssd_conv_w.shape[-1]
    glcw = lax.dynamic_slice_in_dim(rows[LRU_CONV_ROWS[0]:LRU_CONV_ROWS[1]], chip * lc, lc, axis=1)
    gscw = lax.dynamic_slice_in_dim(cst[0:CONV_K], chip * sc, sc, axis=1)
    res = _adamw_small(rows, cst, hst, dwa.reshape(NBLK, BW, BW), dwx.reshape(NBLK, BW, BW), glcw, gscw,
                       {n: w[n] for n in SMALL}, {n: m[n] for n in SMALL}, {n: v[n] for n in SMALL})
    for n in SMALL:
        grads[n], delta[n], new_m[n], new_v[n] = res[n]

    order = ["pre_mix_norm", "w_in", "lru_conv_w", "lru_conv_b", "lru_wa", "lru_ba", "lru_wx", "lru_bx", "lru_lambda",
             "lru_out_norm", "ssd_conv_w", "ssd_conv_b", "ssd_dt_bias", "ssd_a_log", "ssd_d", "ssd_out_norm", "w_out",
             "post_mix_norm", "pre_ffn_norm", "w_gate", "w_up", "w_down", "post_ffn_norm"]
    return (loss, gx[None], *[grads[n] for n in order], *[delta[n] for n in order],
            *[new_m[n] for n in order], *[new_v[n] for n in order])
```
